```python
import jax, jax.numpy as jnp
from jax import lax
import numpy as np

D_MODEL = 1024
BATCH = 16
SEQ = 2048
DEPTH = 1

CHUNK = 64
N_META = 16
Q_BLOCK = 128
C_CONV = D_MODEL // 2
CONV_K = 31
N_HEADS = D_MODEL // 128
QK_NOPE = 64
QK_ROPE = 32
V_DIM = 64
Q_LORA = 3 * D_MODEL // 8
KV_LORA = D_MODEL // 4
ROPE_THETA = 10000.0
D_ATTN = N_HEADS * V_DIM
D_MIX = C_CONV + D_ATTN
N_IN = 2 * C_CONV + Q_LORA + KV_LORA + QK_ROPE
N_GROUPS = 4
EXPERTS_PER_GROUP = 8
N_EXPERTS = N_GROUPS * EXPERTS_PER_GROUP
TOP_K = 2
D_EXPERT = D_MODEL // 4
MOE_BLOCK = 128
DEEPNORM_ALPHA = (2.0 * DEPTH) ** 0.25
DEEPNORM_BETA = (8.0 * DEPTH) ** -0.25
EPS = 1e-5
NEG_INF = -1e30

kernel_name = 'hybrid_conv_mla_hmoe_deepnorm'


def _layer_norm(x, g, b):
    xf = x.astype(jnp.float32)
    mu = jnp.mean(xf, axis=-1, keepdims=True)
    var = jnp.mean(jnp.square(xf - mu), axis=-1, keepdims=True)
    return ((xf - mu) * lax.rsqrt(var + EPS)).astype(x.dtype) * g + b


def _rms_norm(x, g):
    xf = x.astype(jnp.float32)
    return (xf * lax.rsqrt(jnp.mean(xf * xf, axis=-1, keepdims=True) + EPS)).astype(x.dtype) * g


def _chunk_id(pos):
    return jnp.where(pos < N_META, 0, (pos - N_META) // CHUNK + 1)


def _rope_tables(length, dtype):
    half = QK_ROPE // 2
    inv_freq = ROPE_THETA ** (-jnp.arange(half, dtype=jnp.float32) / half)
    ang = jnp.arange(length, dtype=jnp.float32)[:, None] * inv_freq[None, :]
    return jnp.cos(ang).astype(dtype), jnp.sin(ang).astype(dtype)


def _apply_rope(x, cos, sin):
    x1, x2 = jnp.split(x, 2, axis=-1)
    return jnp.concatenate([x1 * cos - x2 * sin, x1 * sin + x2 * cos], axis=-1)


def _causal_depthwise_conv(u, w, b):
    y = lax.conv_general_dilated(u, w[:, None, :], window_strides=(1,), padding=[(CONV_K - 1, 0)],
                                 dimension_numbers=('NWC', 'WIO', 'NWC'), feature_group_count=u.shape[-1])
    return y + b


def _chunk_causal_mla(q_nope, q_rope, k_nope, k_rope, v):
    B, L, H, _ = q_nope.shape
    n_blk = -(-L // Q_BLOCK)
    lp = n_blk * Q_BLOCK

    def to_blocks(t):
        t = jnp.pad(t, [(0, 0), (0, lp - L)] + [(0, 0)] * (t.ndim - 2))
        return jnp.swapaxes(t.reshape((B, n_blk, Q_BLOCK) + t.shape[2:]), 0, 1)

    qn_b, qr_b = to_blocks(q_nope), to_blocks(q_rope)
    q_pos = jnp.arange(lp, dtype=jnp.int32).reshape(n_blk, Q_BLOCK)
    k_chunk = _chunk_id(jnp.arange(L, dtype=jnp.int32))
    scale = (QK_NOPE + QK_ROPE) ** -0.5

    def attend(args):
        qn, qr, qp = args
        s = (jnp.einsum('bqhd,bkhd->bhqk', qn, k_nope)
             + jnp.einsum('bqhd,bkd->bhqk', qr, k_rope)).astype(jnp.float32) * scale
        visible = k_chunk[None, :] <= _chunk_id(qp)[:, None]
        p = jax.nn.softmax(jnp.where(visible, s, NEG_INF), axis=-1).astype(v.dtype)
        return jnp.einsum('bhqk,bkhd->bqhd', p, v)

    o = lax.map(attend, (qn_b, qr_b, q_pos))
    return jnp.swapaxes(o, 0, 1).reshape(B, lp, H, V_DIM)[:, :L]


def _mixer(h, cos, sin, w_in, conv_w, conv_b, conv_ln_g, conv_ln_b, q_norm_g, w_uq,
           kv_norm_g, w_ukv, conv_out_g, attn_out_g, w_o):
    B, L, _ = h.shape
    z = h @ w_in
    c1 = C_CONV
    c2 = 2 * C_CONV
    c3 = c2 + Q_LORA
    c4 = c3 + KV_LORA
    a, gate, cq, ckv, kr = jnp.split(z, [c1, c2, c3, c4], axis=-1)
    u = a * jax.nn.sigmoid(gate)
    c = jax.nn.silu(_layer_norm(_causal_depthwise_conv(u, conv_w, conv_b), conv_ln_g, conv_ln_b))
    q = (_rms_norm(cq, q_norm_g) @ w_uq).reshape(B, L, N_HEADS, QK_NOPE + QK_ROPE)
    q_nope = q[..., :QK_NOPE]
    q_rope = _apply_rope(q[..., QK_NOPE:], cos[:, None, :], sin[:, None, :])
    kv = (_rms_norm(ckv, kv_norm_g) @ w_ukv).reshape(B, L, N_HEADS, QK_NOPE + V_DIM)
    k_nope, v = kv[..., :QK_NOPE], kv[..., QK_NOPE:]
    k_rope = _apply_rope(kr, cos, sin)
    o = _chunk_causal_mla(q_nope, q_rope, k_nope, k_rope, v).reshape(B, L, D_ATTN)
    merged = jnp.concatenate([_rms_norm(c, conv_out_g), _rms_norm(o, attn_out_g)], axis=-1)
    return merged @ w_o


def _hier_moe(h, w_rg, b_rg, w_re, b_re, w_gate, w_up, w_down):
    B, L, D = h.shape
    T = B * L
    xt = h.reshape(T, D)
    g_logits = (xt @ w_rg).astype(jnp.float32) + b_rg.astype(jnp.float32)
    g_prob = jax.nn.softmax(g_logits, axis=-1)
    g_idx = jnp.argmax(g_logits, axis=-1).astype(jnp.int32)
    g_w = jnp.take_along_axis(g_prob, g_idx[:, None], axis=1)[:, 0]
    e_logits = ((xt @ w_re).astype(jnp.float32) + b_re.astype(jnp.float32)).reshape(T, N_GROUPS, EXPERTS_PER_GROUP)
    e_in = jnp.take_along_axis(e_logits, g_idx[:, None, None], axis=1)[:, 0]
    top_v, top_i = lax.top_k(e_in, TOP_K)
    comb = g_w[:, None] * jax.nn.softmax(top_v, axis=-1)
    eid = (g_idx[:, None] * EXPERTS_PER_GROUP + top_i).reshape(-1).astype(jnp.int32)
    A = T * TOP_K
    tok = jnp.repeat(jnp.arange(T, dtype=jnp.int32), TOP_K)
    order = jnp.argsort(eid)
    e_sorted = eid[order]
    counts = jnp.bincount(eid, length=N_EXPERTS).astype(jnp.int32)
    padded = (counts + MOE_BLOCK - 1) // MOE_BLOCK * MOE_BLOCK
    pad_end = jnp.cumsum(padded)
    pad_start = pad_end - padded
    start = jnp.cumsum(counts) - counts
    dest = pad_start[e_sorted] + (jnp.arange(A, dtype=jnp.int32) - start[e_sorted])
    n_blocks = -(-A // MOE_BLOCK) + N_EXPERTS
    R = n_blocks * MOE_BLOCK
    tok_buf = jnp.zeros((R,), jnp.int32).at[dest].set(tok[order])
    w_buf = jnp.zeros((R,), h.dtype).at[dest].set(comb.reshape(-1)[order].astype(h.dtype))
    blk_e = jnp.clip(jnp.searchsorted(pad_end, jnp.arange(n_blocks, dtype=jnp.int32) * MOE_BLOCK, side='right'),
                     0, N_EXPERTS - 1)
    x_buf = xt[tok_buf].reshape(n_blocks, MOE_BLOCK, D)

    def expert_block(args):
        xb, e = args
        act = jax.nn.silu(xb @ w_gate[e]) * (xb @ w_up[e])
        return act @ w_down[e]

    y_buf = lax.map(expert_block, (x_buf, blk_e)).reshape(R, D)
    out = jnp.zeros((T, D), h.dtype).at[tok_buf].add(y_buf * w_buf[:, None])
    return out.reshape(B, L, D)


def setup_inputs(seed: int = 0) -> dict:
    key = jax.random.key(seed)
    ks = jax.random.split(key, 32)
    f32 = jnp.float32

    def nrm(k, shape, scale):
        return jax.random.normal(k, shape, f32) * scale

    def gain(k, shape):
        return 1.0 + 0.05 * jax.random.normal(k, shape, f32)

    beta = DEEPNORM_BETA
    in_scale = jnp.concatenate([jnp.full((C_CONV,), beta, f32), jnp.ones((N_IN - C_CONV,), f32)])
    kv_scale = jnp.tile(jnp.concatenate([jnp.ones((QK_NOPE,), f32), jnp.full((V_DIM,), beta, f32)]), N_HEADS)
    return {
        'x': nrm(ks[0], (BATCH, SEQ, D_MODEL), 1.0),
        'meta_tokens': nrm(ks[1], (N_META, D_MODEL), 1.0),
        'ln_in_g': gain(ks[2], (D_MODEL,)),
        'ln_in_b': nrm(ks[3], (D_MODEL,), 0.01),
        'w_in': nrm(ks[4], (DEPTH, D_MODEL, N_IN), D_MODEL ** -0.5) * in_scale,
        'conv_w': nrm(ks[5], (DEPTH, CONV_K, C_CONV), CONV_K ** -0.5),
        'conv_b': nrm(ks[6], (DEPTH, C_CONV), 0.01),
        'conv_ln_g': gain(ks[7], (DEPTH, C_CONV)),
        'conv_ln_b': nrm(ks[8], (DEPTH, C_CONV), 0.01),
        'q_norm_g': gain(ks[9], (DEPTH, Q_LORA)),
        'w_uq': nrm(ks[10], (DEPTH, Q_LORA, N_HEADS * (QK_NOPE + QK_ROPE)), Q_LORA ** -0.5),
        'kv_norm_g': gain(ks[11], (DEPTH, KV_LORA)),
        'w_ukv': nrm(ks[12], (DEPTH, KV_LORA, N_HEADS * (QK_NOPE + V_DIM)), KV_LORA ** -0.5) * kv_scale,
        'conv_out_g': gain(ks[13], (DEPTH, C_CONV)),
        'attn_out_g': gain(ks[14], (DEPTH, D_ATTN)),
        'w_o': nrm(ks[15], (DEPTH, D_MIX, D_MODEL), D_MIX ** -0.5 * beta),
        'ln1_g': gain(ks[16], (DEPTH, D_MODEL)),
        'ln1_b': nrm(ks[17], (DEPTH, D_MODEL), 0.01),
        'w_rg': nrm(ks[18], (DEPTH, D_MODEL, N_GROUPS), D_MODEL ** -0.5),
        'b_rg': nrm(ks[19], (DEPTH, N_GROUPS), 0.01),
        'w_re': nrm(ks[20], (DEPTH, D_MODEL, N_EXPERTS), D_MODEL ** -0.5),
        'b_re': nrm(ks[21], (DEPTH, N_EXPERTS), 0.01),
        'w_gate': nrm(ks[22], (DEPTH, N_EXPERTS, D_MODEL, D_EXPERT), D_MODEL ** -0.5),
        'w_up': nrm(ks[23], (DEPTH, N_EXPERTS, D_MODEL, D_EXPERT), D_MODEL ** -0.5 * beta),
        'w_down': nrm(ks[24], (DEPTH, N_EXPERTS, D_EXPERT, D_MODEL), D_EXPERT ** -0.5 * beta),
        'ln2_g': gain(ks[25], (DEPTH, D_MODEL)),
        'ln2_b': nrm(ks[26], (DEPTH, D_MODEL), 0.01),
    }


def reference(x, meta_tokens, ln_in_g, ln_in_b, w_in, conv_w, conv_b, conv_ln_g, conv_ln_b,
              q_norm_g, w_uq, kv_norm_g, w_ukv, conv_out_g, attn_out_g, w_o, ln1_g, ln1_b,
              w_rg, b_rg, w_re, b_re, w_gate, w_up, w_down, ln2_g, ln2_b):
    B = x.shape[0]
    meta = jnp.broadcast_to(meta_tokens.astype(x.dtype)[None], (B, N_META, x.shape[-1]))
    h = _layer_norm(jnp.concatenate([meta, x], axis=1), ln_in_g, ln_in_b)
    cos, sin = _rope_tables(h.shape[1], h.dtype)
    for l in range(DEPTH):
        y = _mixer(h, cos, sin, w_in[l], conv_w[l], conv_b[l], conv_ln_g[l], conv_ln_b[l],
                   q_norm_g[l], w_uq[l], kv_norm_g[l], w_ukv[l], conv_out_g[l], attn_out_g[l], w_o[l])
        h = _layer_norm(DEEPNORM_ALPHA * h + y, ln1_g[l], ln1_b[l])
        y = _hier_moe(h, w_rg[l], b_rg[l], w_re[l], b_re[l], w_gate[l], w_up[l], w_down[l])
        h = _layer_norm(DEEPNORM_ALPHA * h + y, ln2_g[l], ln2_b[l])
    return h[:, N_META:]
```

```python
import math

import numpy as np
import jax
import jax.numpy as jnp
from jax import lax
from jax.experimental import pallas as pl
from jax.experimental.pallas import tpu as pltpu

F32 = jnp.float32
BF16 = jnp.bfloat16

D_MODEL = 1024
N_META = 16
CHUNK = 64
C_CONV = 512
CONV_K = 31
N_HEADS = 8
QK_NOPE = 64
QK_ROPE = 32
V_DIM = 64
Q_LORA = 384
KV_LORA = 256
ROPE_THETA = 10000.0
D_ATTN = N_HEADS * V_DIM
N_GROUPS = 4
EXPERTS_PER_GROUP = 8
N_EXPERTS = N_GROUPS * EXPERTS_PER_GROUP
D_EXPERT = 256
DEEPNORM_ALPHA = 2.0 ** 0.25
EPS = 1e-5
NEG_INF = -1e30

LANES = 128
HEAD_PAD = LANES
ROPE_HALF = QK_ROPE // 2
X2_LANE = 64
Z_COLS = 2 * C_CONV + Q_LORA + KV_LORA + LANES
Q_SCALE = (QK_NOPE + QK_ROPE) ** -0.5 * math.log2(math.e)

PROJ_TM = 512
ATT_T = 256
MIX_TM = 256
HALO = 32
MOE_BM = 256
DISP_TM = 512
COMB_TM = 512
VMEM_LIMIT = 56 * 1024 * 1024


def _ln(x, g, b):
    mu = jnp.mean(x, axis=-1, keepdims=True)
    xc = x - mu
    var = jnp.mean(xc * xc, axis=-1, keepdims=True)
    return xc * lax.rsqrt(var + EPS) * g + b


def _rms(x, g):
    return x * lax.rsqrt(jnp.mean(x * x, axis=-1, keepdims=True) + EPS) * g


def _sigmoid(x):
    return 1.0 / (1.0 + jnp.exp(-x))


def _proj_kernel(x_ref, g_ref, b_ref, win_ref, qg_ref, wuq_ref, kvg_ref, wukv_ref, c_ref, s_ref,
                 u_ref, q_ref, k_ref, v_ref):
    h = _ln(x_ref[...], g_ref[...], b_ref[...])
    z = jnp.dot(h.astype(BF16), win_ref[...], preferred_element_type=F32)
    u_ref[...] = z[:, :C_CONV] * _sigmoid(z[:, C_CONV:2 * C_CONV])
    c0 = 2 * C_CONV
    cq = z[:, c0:c0 + Q_LORA]
    ckv = z[:, c0 + Q_LORA:c0 + Q_LORA + KV_LORA]
    kr = z[:, c0 + Q_LORA + KV_LORA:]
    cs = c_ref[...]
    sn = s_ref[...]
    q = jnp.dot(_rms(cq, qg_ref[...]).astype(BF16), wuq_ref[...], preferred_element_type=F32)
    kv = jnp.dot(_rms(ckv, kvg_ref[...]).astype(BF16), wukv_ref[...], preferred_element_type=F32)
    kr_rot = kr * cs + pltpu.roll(kr, X2_LANE, 1) * sn
    for hh in range(N_HEADS):
        sl = slice(HEAD_PAD * hh, HEAD_PAD * (hh + 1))
        qh = q[:, sl]
        qh = (qh * cs + pltpu.roll(qh, X2_LANE, 1) * sn) * Q_SCALE
        q_ref[:, sl] = qh.astype(BF16)
        k_ref[:, sl] = (kv[:, sl] + kr_rot).astype(BF16)
    v_ref[...] = kv[:, N_HEADS * HEAD_PAD:].astype(BF16)


def _proj_call(x2d, tm, pos_blocks, ln_g, ln_b, w_in, qg, wuq, kvg, wukv, ctab, stab):
    t = x2d.shape[0]
    full = lambda shape: pl.BlockSpec(shape, lambda i: (0,) * len(shape))
    row = lambda w: pl.BlockSpec((tm, w), lambda i: (i, 0))
    tab = pl.BlockSpec((tm, LANES), lambda i: (i % pos_blocks, 0))
    return pl.pallas_call(
        _proj_kernel,
        grid=(t // tm,),
        in_specs=[row(D_MODEL), full((1, D_MODEL)), full((1, D_MODEL)), full((D_MODEL, Z_COLS)),
                  full((1, Q_LORA)), full((Q_LORA, N_HEADS * HEAD_PAD)),
                  full((1, KV_LORA)), full((KV_LORA, N_HEADS * HEAD_PAD + D_ATTN)), tab, tab],
        out_specs=[row(C_CONV), row(N_HEADS * HEAD_PAD), row(N_HEADS * HEAD_PAD), row(D_ATTN)],
        out_shape=[jax.ShapeDtypeStruct((t, C_CONV), F32),
                   jax.ShapeDtypeStruct((t, N_HEADS * HEAD_PAD), BF16),
                   jax.ShapeDtypeStruct((t, N_HEADS * HEAD_PAD), BF16),
                   jax.ShapeDtypeStruct((t, D_ATTN), BF16)],
        compiler_params=pltpu.CompilerParams(dimension_semantics=("arbitrary",),
                                             vmem_limit_bytes=VMEM_LIMIT),
        name="proj",
    )(x2d, ln_g, ln_b, w_in, qg, wuq, kvg, wukv, ctab, stab)


def _attn_kernel(q_ref, k_ref, v_ref, km_ref, vm_ref, o_ref):
    seq = q_ref.shape[1]
    nq = seq // ATT_T
    lane = lax.broadcasted_iota(jnp.int32, (ATT_T, LANES), 1)
    rr = lax.broadcasted_iota(jnp.int32, (ATT_T, ATT_T), 0) // CHUNK
    cc = lax.broadcasted_iota(jnp.int32, (ATT_T, ATT_T), 1) // CHUNK
    diag_visible = cc <= rr
    nt = (((1,), (1,)), ((), ()))

    def q_body(qi, carry):
        row0 = pl.multiple_of(qi * ATT_T, ATT_T)
        outs = []
        for hh in range(2):
            sl = slice(HEAD_PAD * hh, HEAD_PAD * (hh + 1))
            q = q_ref[0, pl.ds(row0, ATT_T), sl]
            s = lax.dot_general(q, km_ref[:, sl], nt, preferred_element_type=F32)
            m = jnp.max(s, axis=1, keepdims=True)
            p = jnp.exp2(s - m)
            l = jnp.sum(p, axis=1, keepdims=True)
            acc = jnp.dot(p.astype(BF16), vm_ref[...], preferred_element_type=F32)

            def update(s, v, m, l, acc):
                m_new = jnp.maximum(m, jnp.max(s, axis=1, keepdims=True))
                alpha = jnp.exp2(m - m_new)
                p = jnp.exp2(s - m_new)
                l = alpha * l + jnp.sum(p, axis=1, keepdims=True)
                acc = alpha * acc + jnp.dot(p.astype(BF16), v, preferred_element_type=F32)
                return m_new, l, acc

            def kv_body(j, c3):
                k0 = pl.multiple_of(j * ATT_T, ATT_T)
                k = k_ref[0, pl.ds(k0, ATT_T), sl]
                v = v_ref[0, pl.ds(k0, ATT_T), :]
                s = lax.dot_general(q, k, nt, preferred_element_type=F32)
                return update(s, v, *c3)

            m, l, acc = lax.fori_loop(0, qi, kv_body, (m, l, acc))
            k = k_ref[0, pl.ds(row0, ATT_T), sl]
            v = v_ref[0, pl.ds(row0, ATT_T), :]
            s = lax.dot_general(q, k, nt, preferred_element_type=F32)
            s = jnp.where(diag_visible, s, NEG_INF)
            m, l, acc = update(s, v, m, l, acc)
            outs.append(acc / l)
        o_ref[0, pl.ds(row0, ATT_T), :] = jnp.where(lane < V_DIM, outs[0], outs[1])
        return carry

    lax.fori_loop(0, nq, q_body, 0)


def _attn_call(q, k, v, k_meta, v_meta):
    b, seq, _ = q.shape
    pairs = N_HEADS // 2
    return pl.pallas_call(
        _attn_kernel,
        grid=(b, pairs),
        in_specs=[pl.BlockSpec((1, seq, 2 * HEAD_PAD), lambda i, j: (i, 0, j)),
                  pl.BlockSpec((1, seq, 2 * HEAD_PAD), lambda i, j: (i, 0, j)),
                  pl.BlockSpec((1, seq, 2 * V_DIM), lambda i, j: (i, 0, j)),
                  pl.BlockSpec((N_META, 2 * HEAD_PAD), lambda i, j: (0, j)),
                  pl.BlockSpec((N_META, 2 * V_DIM), lambda i, j: (0, j))],
        out_specs=pl.BlockSpec((1, seq, 2 * V_DIM), lambda i, j: (i, 0, j)),
        out_shape=jax.ShapeDtypeStruct((b, seq, D_ATTN), F32),
        compiler_params=pltpu.CompilerParams(dimension_semantics=("arbitrary", "arbitrary"),
                                             vmem_limit_bytes=VMEM_LIMIT),
        name="attn",
    )(q, k, v, k_meta, v_meta)


def _mix_kernel(x_ref, ig_ref, ib_ref, u_ref, uh_ref, pre_ref, cw_ref, cb_ref, clg_ref, clb_ref,
                cog_ref, o_ref, aog_ref, wo_ref, l1g_ref, l1b_ref, wr_ref, br_ref,
                h1_ref, ri_ref, rw_ref, cnt_ref, win_sc, carry_sc):
    bi = pl.program_id(0)
    ti = pl.program_id(1)
    tm = MIX_TM

    @pl.when((bi == 0) & (ti == 0))
    def _():
        carry_sc[...] = jnp.zeros_like(carry_sc)

    win_sc[0:HALO, :] = jnp.where(ti == 0, pre_ref[...], uh_ref[0])
    win_sc[HALO:, :] = u_ref[0]
    acc = jnp.broadcast_to(cb_ref[...], (tm, C_CONV))
    for kk in range(CONV_K):
        acc = acc + win_sc[pl.ds(HALO - (CONV_K - 1) + kk, tm), :] * cw_ref[kk:kk + 1, :]
    c = _ln(acc, clg_ref[...], clb_ref[...])
    c = c * _sigmoid(c)
    cn = _rms(c, cog_ref[...])
    on = _rms(o_ref[0], aog_ref[...])
    y = (jnp.dot(cn.astype(BF16), wo_ref[0:C_CONV, :], preferred_element_type=F32)
         + jnp.dot(on.astype(BF16), wo_ref[C_CONV:, :], preferred_element_type=F32))
    h = _ln(x_ref[0], ig_ref[...], ib_ref[...])
    h1 = _ln(DEEPNORM_ALPHA * h + y, l1g_ref[...], l1b_ref[...])
    h1_ref[0] = h1

    logits = jnp.dot(h1, wr_ref[...], preferred_element_type=F32,
                     precision=lax.Precision.HIGHEST) + br_ref[...]
    lane = lax.broadcasted_iota(jnp.int32, (tm, LANES), 1)
    big = jnp.int32(1 << 20)
    is_g = (lane >= N_EXPERTS) & (lane < N_EXPERTS + N_GROUPS)
    gl = jnp.where(is_g, logits, -jnp.inf)
    gmax = jnp.max(gl, axis=1, keepdims=True)
    g_idx = jnp.min(jnp.where(gl == gmax, lane - N_EXPERTS, big), axis=1, keepdims=True)
    g_w = 1.0 / jnp.sum(jnp.exp(gl - gmax), axis=1, keepdims=True)
    in_grp = (lane < N_EXPERTS) & ((lane // EXPERTS_PER_GROUP) == g_idx)
    el = jnp.where(in_grp, logits, -jnp.inf)
    v1 = jnp.max(el, axis=1, keepdims=True)
    i1 = jnp.min(jnp.where(el == v1, lane, big), axis=1, keepdims=True)
    el2 = jnp.where(lane == i1, -jnp.inf, el)
    v2 = jnp.max(el2, axis=1, keepdims=True)
    i2 = jnp.min(jnp.where(el2 == v2, lane, big), axis=1, keepdims=True)
    e2 = jnp.exp(v2 - v1)
    w1 = g_w / (1.0 + e2)
    w2 = g_w * e2 / (1.0 + e2)

    oh1 = lane == i1
    oh2 = lane == i2
    oh = jnp.where(oh1 | oh2, 1.0, 0.0)
    rt = lax.broadcasted_iota(jnp.int32, (tm, tm), 0)
    ct = lax.broadcasted_iota(jnp.int32, (tm, tm), 1)
    ltri = jnp.where(ct < rt, 1.0, 0.0).astype(BF16)
    before = jnp.dot(ltri, oh.astype(BF16), preferred_element_type=F32) + carry_sc[...]
    r1 = jnp.sum(jnp.where(oh1, before, 0.0), axis=1, keepdims=True).astype(jnp.int32)
    r2 = jnp.sum(jnp.where(oh2, before, 0.0), axis=1, keepdims=True).astype(jnp.int32)
    carry_sc[...] = carry_sc[...] + jnp.sum(oh, axis=0, keepdims=True)
    cnt_ref[...] = carry_sc[...]

    ri = jnp.where(lane == 0, i1, jnp.where(lane == 1, i2, jnp.where(lane == 2, r1, r2)))
    rw = jnp.where(lane == 0, w1, w2)
    ri_ref[0] = ri[:, :8]
    rw_ref[0] = rw[:, :8]


def _mix_call(x, ig, ib, u, pre, cw, cb, clg, clb, cog, o, aog, wo, l1g, l1b, wr, br):
    b, seq, _ = x.shape
    tm = MIX_TM
    nt = seq // tm
    full = lambda shape: pl.BlockSpec(shape, lambda i, j: (0,) * len(shape))
    tile = lambda w: pl.BlockSpec((1, tm, w), lambda i, j: (i, j, 0))
    halo = pl.BlockSpec((1, HALO, C_CONV), lambda i, j: (i, jnp.maximum(j * (tm // HALO) - 1, 0), 0))
    return pl.pallas_call(
        _mix_kernel,
        grid=(b, nt),
        in_specs=[tile(D_MODEL), full((1, D_MODEL)), full((1, D_MODEL)),
                  tile(C_CONV), halo, full((HALO, C_CONV)), full((HALO, C_CONV)), full((1, C_CONV)),
                  full((1, C_CONV)), full((1, C_CONV)), full((1, C_CONV)),
                  tile(D_ATTN), full((1, D_ATTN)), full((D_MODEL, D_MODEL)),
                  full((1, D_MODEL)), full((1, D_MODEL)), full((D_MODEL, LANES)), full((1, LANES))],
        out_specs=[tile(D_MODEL), tile(8), tile(8), full((1, LANES))],
        out_shape=[jax.ShapeDtypeStruct((b, seq, D_MODEL), F32),
                   jax.ShapeDtypeStruct((b, seq, 8), jnp.int32),
                   jax.ShapeDtypeStruct((b, seq, 8), F32),
                   jax.ShapeDtypeStruct((1, LANES), F32)],
        scratch_shapes=[pltpu.VMEM((HALO + tm, C_CONV), F32), pltpu.VMEM((1, LANES), F32)],
        compiler_params=pltpu.CompilerParams(dimension_semantics=("arbitrary", "arbitrary"),
                                             vmem_limit_bytes=VMEM_LIMIT),
        name="mix",
    )(x, ig, ib, u, u, pre, cw, cb, clg, clb, cog, o, aog, wo, l1g, l1b, wr, br)


def _dispatch_kernel(dest_ref, h_ref, xz_ref, xb_ref, sem):
    del xz_ref
    tm = DISP_TM

    def issue(t, carry):
        src = h_ref.at[pl.ds(t, 1)]
        pltpu.make_async_copy(src, xb_ref.at[pl.ds(dest_ref[2 * t], 1)], sem).start()
        pltpu.make_async_copy(src, xb_ref.at[pl.ds(dest_ref[2 * t + 1], 1)], sem).start()
        return carry

    lax.fori_loop(0, tm, issue, 0, unroll=8)
    for _ in range(2):
        pltpu.make_async_copy(h_ref, xb_ref.at[pl.ds(0, tm)], sem).wait()


def _dispatch_call(dest, h1, xz):
    t = h1.shape[0]
    tm = DISP_TM
    return pl.pallas_call(
        _dispatch_kernel,
        grid=(t // tm,),
        in_specs=[pl.BlockSpec((2 * tm,), lambda i: (i,), memory_space=pltpu.SMEM),
                  pl.BlockSpec((tm, D_MODEL), lambda i: (i, 0)),
                  pl.BlockSpec(memory_space=pl.ANY)],
        out_specs=pl.BlockSpec(memory_space=pl.ANY),
        out_shape=jax.ShapeDtypeStruct(xz.shape, F32),
        scratch_shapes=[pltpu.SemaphoreType.DMA],
        input_output_aliases={2: 0},
        compiler_params=pltpu.CompilerParams(dimension_semantics=("arbitrary",),
                                             vmem_limit_bytes=VMEM_LIMIT),
        name="dispatch",
    )(dest, h1, xz)


def _expert_kernel(be_ref, nu_ref, x_ref, wgu_ref, wd_ref, y_ref):
    i = pl.program_id(0)

    @pl.when(i < nu_ref[0])
    def _():
        gu = jnp.dot(x_ref[...].astype(BF16), wgu_ref[0], preferred_element_type=F32)
        g = gu[:, :D_EXPERT]
        act = g * _sigmoid(g) * gu[:, D_EXPERT:]
        y_ref[...] = jnp.dot(act.astype(BF16), wd_ref[0], preferred_element_type=F32)

    @pl.when(i >= nu_ref[0])
    def _():
        y_ref[...] = jnp.zeros_like(y_ref)


def _expert_call(blk_e, n_used, xb, wgu, wd):
    r = xb.shape[0]
    bm = MOE_BM
    return pl.pallas_call(
        _expert_kernel,
        grid_spec=pltpu.PrefetchScalarGridSpec(
            num_scalar_prefetch=2,
            grid=(r // bm,),
            in_specs=[pl.BlockSpec((bm, D_MODEL), lambda i, be, nu: (i, 0)),
                      pl.BlockSpec((1, D_MODEL, 2 * D_EXPERT), lambda i, be, nu: (be[i], 0, 0)),
                      pl.BlockSpec((1, D_EXPERT, D_MODEL), lambda i, be, nu: (be[i], 0, 0))],
            out_specs=pl.BlockSpec((bm, D_MODEL), lambda i, be, nu: (i, 0))),
        out_shape=jax.ShapeDtypeStruct((r, D_MODEL), F32),
        compiler_params=pltpu.CompilerParams(dimension_semantics=("arbitrary",),
                                             vmem_limit_bytes=VMEM_LIMIT),
        name="experts",
    )(blk_e, n_used, xb, wgu, wd)


def _combine_kernel(dest_ref, h_ref, w_ref, g_ref, b_ref, yb_ref, o_ref, y0_sc, y1_sc, sem):
    tm = COMB_TM

    def issue(t, carry):
        pltpu.make_async_copy(yb_ref.at[pl.ds(dest_ref[2 * t], 1)], y0_sc.at[pl.ds(t, 1)], sem).start()
        pltpu.make_async_copy(yb_ref.at[pl.ds(dest_ref[2 * t + 1], 1)], y1_sc.at[pl.ds(t, 1)], sem).start()
        return carry

    lax.fori_loop(0, tm, issue, 0, unroll=8)
    pltpu.make_async_copy(yb_ref.at[pl.ds(0, tm)], y0_sc, sem).wait()
    pltpu.make_async_copy(yb_ref.at[pl.ds(0, tm)], y1_sc, sem).wait()
    w = w_ref[...]
    y = w[:, 0:1] * y0_sc[...] + w[:, 1:2] * y1_sc[...]
    o_ref[...] = _ln(DEEPNORM_ALPHA * h_ref[...] + y, g_ref[...], b_ref[...])


def _combine_call(dest, h1, rw, g, b, yb):
    t = h1.shape[0]
    tm = COMB_TM
    return pl.pallas_call(
        _combine_kernel,
        grid=(t // tm,),
        in_specs=[pl.BlockSpec((2 * tm,), lambda i: (i,), memory_space=pltpu.SMEM),
                  pl.BlockSpec((tm, D_MODEL), lambda i: (i, 0)),
                  pl.BlockSpec((tm, 8), lambda i: (i, 0)),
                  pl.BlockSpec((1, D_MODEL), lambda i: (0, 0)),
                  pl.BlockSpec((1, D_MODEL), lambda i: (0, 0)),
                  pl.BlockSpec(memory_space=pl.ANY)],
        out_specs=pl.BlockSpec((tm, D_MODEL), lambda i: (i, 0)),
        out_shape=jax.ShapeDtypeStruct((t, D_MODEL), F32),
        scratch_shapes=[pltpu.VMEM((tm, D_MODEL), F32), pltpu.VMEM((tm, D_MODEL), F32),
                        pltpu.SemaphoreType.DMA],
        compiler_params=pltpu.CompilerParams(dimension_semantics=("arbitrary",),
                                             vmem_limit_bytes=VMEM_LIMIT),
        name="combine",
    )(dest, h1, rw, g, b, yb)


def _take_cols(w, idx):
    wz = jnp.concatenate([w, jnp.zeros((w.shape[0], 1), w.dtype)], axis=1)
    return wz[:, np.where(idx < 0, w.shape[1], idx)]


def _layout_indices():
    lane_src = np.full((HEAD_PAD,), -1, np.int64)
    lane_src[0:ROPE_HALF] = QK_NOPE + np.arange(ROPE_HALF)
    lane_src[ROPE_HALF:X2_LANE] = np.arange(X2_LANE - ROPE_HALF)
    lane_src[X2_LANE:X2_LANE + ROPE_HALF] = QK_NOPE + ROPE_HALF + np.arange(ROPE_HALF)
    n_lo = X2_LANE - ROPE_HALF
    lane_src[X2_LANE + ROPE_HALF:X2_LANE + ROPE_HALF + QK_NOPE - n_lo] = n_lo + np.arange(QK_NOPE - n_lo)
    q_idx = np.concatenate([np.where(lane_src >= 0, lane_src + (QK_NOPE + QK_ROPE) * h, -1)
                            for h in range(N_HEADS)])
    nope_src = np.where(lane_src < QK_NOPE, lane_src, -1)
    k_idx = np.concatenate([np.where(nope_src >= 0, nope_src + (QK_NOPE + V_DIM) * h, -1)
                            for h in range(N_HEADS)])
    v_idx = np.concatenate([QK_NOPE + np.arange(V_DIM) + (QK_NOPE + V_DIM) * h for h in range(N_HEADS)])
    kr_idx = np.full((LANES,), -1, np.int64)
    kr_idx[0:ROPE_HALF] = np.arange(ROPE_HALF)
    kr_idx[X2_LANE:X2_LANE + ROPE_HALF] = ROPE_HALF + np.arange(ROPE_HALF)
    return q_idx, k_idx, v_idx, kr_idx


def _rope_lane_tables(length):
    inv_freq = ROPE_THETA ** (-jnp.arange(ROPE_HALF, dtype=F32) / ROPE_HALF)
    ang = jnp.arange(length, dtype=F32)[:, None] * inv_freq[None, :]
    cos, sin = jnp.cos(ang), jnp.sin(ang)
    ctab = jnp.ones((length, LANES), F32)
    ctab = ctab.at[:, 0:ROPE_HALF].set(cos).at[:, X2_LANE:X2_LANE + ROPE_HALF].set(cos)
    stab = jnp.zeros((length, LANES), F32)
    stab = stab.at[:, 0:ROPE_HALF].set(-sin).at[:, X2_LANE:X2_LANE + ROPE_HALF].set(sin)
    return ctab, stab


def kernel(x, meta_tokens, ln_in_g, ln_in_b, w_in, conv_w, conv_b, conv_ln_g, conv_ln_b, q_norm_g, w_uq,
           kv_norm_g, w_ukv, conv_out_g, attn_out_g, w_o, ln1_g, ln1_b, w_rg, b_rg, w_re, b_re,
           w_gate, w_up, w_down, ln2_g, ln2_b):
    assert w_in.shape[0] == 1, "single layer"
    bsz, seq, d = x.shape
    t = bsz * seq
    row = lambda a: a.reshape(1, -1)

    q_idx, k_idx, v_idx, kr_idx = _layout_indices()
    c4 = 2 * C_CONV + Q_LORA + KV_LORA
    w_in_l = jnp.concatenate([w_in[0][:, :c4], _take_cols(w_in[0][:, c4:], kr_idx)], axis=1).astype(BF16)
    wuq_l = _take_cols(w_uq[0], q_idx).astype(BF16)
    wukv_l = jnp.concatenate([_take_cols(w_ukv[0], k_idx), w_ukv[0][:, v_idx]], axis=1).astype(BF16)
    wo_l = w_o[0].astype(BF16)
    wgu_l = jnp.concatenate([w_gate[0], w_up[0]], axis=2).astype(BF16)
    wd_l = w_down[0].astype(BF16)
    wr_l = jnp.concatenate([w_re[0], w_rg[0], jnp.zeros((d, LANES - N_EXPERTS - N_GROUPS), F32)], axis=1)
    br_l = jnp.concatenate([b_re[0], b_rg[0], jnp.zeros((LANES - N_EXPERTS - N_GROUPS,), F32)]).reshape(1, LANES)
    ctab, stab = _rope_lane_tables(N_META + seq)

    proj_w = (row(ln_in_g), row(ln_in_b), w_in_l, row(q_norm_g[0]), wuq_l, row(kv_norm_g[0]), wukv_l)
    u_m, _, k_m, v_m = _proj_call(meta_tokens.astype(F32), N_META, 1, *proj_w, ctab[:N_META], stab[:N_META])
    u, q, k, v = _proj_call(x.reshape(t, d), PROJ_TM, seq // PROJ_TM, *proj_w, ctab[N_META:], stab[N_META:])

    o = _attn_call(q.reshape(bsz, seq, -1), k.reshape(bsz, seq, -1), v.reshape(bsz, seq, -1), k_m, v_m)

    pre = jnp.concatenate([jnp.zeros((HALO - N_META, C_CONV), F32), u_m], axis=0)
    cw = jnp.concatenate([conv_w[0], jnp.zeros((HALO - CONV_K, C_CONV), F32)], axis=0)
    h1, ri, rw, cnt = _mix_call(x, row(ln_in_g), row(ln_in_b), u.reshape(bsz, seq, C_CONV), pre, cw,
                                row(conv_b[0]), row(conv_ln_g[0]), row(conv_ln_b[0]), row(conv_out_g[0]),
                                o, row(attn_out_g[0]), wo_l, row(ln1_g[0]), row(ln1_b[0]), wr_l, br_l)

    counts = cnt[0, :N_EXPERTS].astype(jnp.int32)
    padded = (counts + MOE_BM - 1) // MOE_BM * MOE_BM
    pad_end = jnp.cumsum(padded)
    pad_start = pad_end - padded
    n_blocks = (2 * t) // MOE_BM + N_EXPERTS
    ri2 = ri.reshape(t, 8)
    dest = (pad_start[ri2[:, 0:2]] + ri2[:, 2:4]).reshape(2 * t)
    blk_e = jnp.clip(jnp.searchsorted(pad_end, jnp.arange(n_blocks, dtype=jnp.int32) * MOE_BM, side='right'),
                     0, N_EXPERTS - 1).astype(jnp.int32)
    n_used = (pad_end[-1:] // MOE_BM).astype(jnp.int32)

    h1f = h1.reshape(t, d)
    xb = _dispatch_call(dest, h1f, jnp.zeros((n_blocks * MOE_BM, d), F32))
    yb = _expert_call(blk_e, n_used, xb, wgu_l, wd_l)
    out = _combine_call(dest, h1f, rw.reshape(t, 8), row(ln2_g[0]), row(ln2_b[0]), yb)
    return out.reshape(bsz, seq, d)
```

```python
import math

import numpy as np
import jax
import jax.numpy as jnp
from jax import lax
from jax.experimental import pallas as pl
from jax.experimental.pallas import tpu as pltpu

F32 = jnp.float32
BF16 = jnp.bfloat16

D_MODEL = 1024
N_META = 16
CHUNK = 64
C_CONV = 512
CONV_K = 31
N_HEADS = 8
QK_NOPE = 64
QK_ROPE = 32
V_DIM = 64
Q_LORA = 384
KV_LORA = 256
ROPE_THETA = 10000.0
D_ATTN = N_HEADS * V_DIM
N_GROUPS = 4
EXPERTS_PER_GROUP = 8
N_EXPERTS = N_GROUPS * EXPERTS_PER_GROUP
D_EXPERT = 256
DEEPNORM_ALPHA = 2.0 ** 0.25
EPS = 1e-5
NEG_INF = -1e30

LANES = 128
HEAD_PAD = LANES
ROPE_HALF = QK_ROPE // 2
X2_LANE = 64
Z_COLS = 2 * C_CONV + Q_LORA + KV_LORA + LANES
Q_SCALE = (QK_NOPE + QK_ROPE) ** -0.5 * math.log2(math.e)

PROJ_TM = 512
ATT_T = 256
ATT_HEADS = 8
MIX_TM = 256
HALO = 32
MOE_BM = 256
DISP_TM = 512
COMB_TM = 512
VMEM_LIMIT = 56 * 1024 * 1024


def _ln(x, g, b):
    mu = jnp.mean(x, axis=-1, keepdims=True)
    xc = x - mu
    var = jnp.mean(xc * xc, axis=-1, keepdims=True)
    return xc * lax.rsqrt(var + EPS) * g + b


def _rms(x, g):
    return x * lax.rsqrt(jnp.mean(x * x, axis=-1, keepdims=True) + EPS) * g


def _sigmoid(x):
    return 1.0 / (1.0 + jnp.exp(-x))


def _proj_kernel(x_ref, g_ref, b_ref, win_ref, qg_ref, wuq_ref, kvg_ref, wuk_ref, wvt_ref, c_ref, s_ref,
                 u_ref, q_ref, k_ref, vt_ref):
    h = _ln(x_ref[...], g_ref[...], b_ref[...])
    z = jnp.dot(h.astype(BF16), win_ref[...], preferred_element_type=F32)
    u_ref[...] = z[:, :C_CONV] * _sigmoid(z[:, C_CONV:2 * C_CONV])
    c0 = 2 * C_CONV
    cq = z[:, c0:c0 + Q_LORA]
    ckv = z[:, c0 + Q_LORA:c0 + Q_LORA + KV_LORA]
    kr = z[:, c0 + Q_LORA + KV_LORA:]
    cs = c_ref[...]
    sn = s_ref[...]
    q = jnp.dot(_rms(cq, qg_ref[...]).astype(BF16), wuq_ref[...], preferred_element_type=F32)
    kvn = _rms(ckv, kvg_ref[...]).astype(BF16)
    kn = jnp.dot(kvn, wuk_ref[...], preferred_element_type=F32)
    kr_rot = kr * cs + pltpu.roll(kr, X2_LANE, 1) * sn
    for hh in range(N_HEADS):
        sl = slice(HEAD_PAD * hh, HEAD_PAD * (hh + 1))
        qh = q[:, sl]
        qh = (qh * cs + pltpu.roll(qh, X2_LANE, 1) * sn) * Q_SCALE
        q_ref[:, sl] = qh.astype(BF16)
        k_ref[:, sl] = (kn[:, sl] + kr_rot).astype(BF16)
    vt = lax.dot_general(wvt_ref[...], kvn, (((1,), (1,)), ((), ())), preferred_element_type=F32)
    n_sub, sub_w = vt_ref.shape[1], vt_ref.shape[3]
    for tt in range(n_sub):
        vt_ref[0, tt] = vt[:, tt * sub_w:(tt + 1) * sub_w].astype(BF16)


def _proj_call(x2d, tm, pos_blocks, ln_g, ln_b, w_in, qg, wuq, kvg, wuk, wvt, ctab, stab):
    t = x2d.shape[0]
    sub_w = min(tm, ATT_T)
    n_sub = tm // sub_w
    full = lambda shape: pl.BlockSpec(shape, lambda i: (0,) * len(shape))
    row = lambda w: pl.BlockSpec((tm, w), lambda i: (i, 0))
    tab = pl.BlockSpec((tm, LANES), lambda i: (i % pos_blocks, 0))
    return pl.pallas_call(
        _proj_kernel,
        grid=(t // tm,),
        in_specs=[row(D_MODEL), full((1, D_MODEL)), full((1, D_MODEL)), full((D_MODEL, Z_COLS)),
                  full((1, Q_LORA)), full((Q_LORA, N_HEADS * HEAD_PAD)),
                  full((1, KV_LORA)), full((KV_LORA, N_HEADS * HEAD_PAD)), full((D_ATTN, KV_LORA)), tab, tab],
        out_specs=[row(C_CONV), row(N_HEADS * HEAD_PAD), row(N_HEADS * HEAD_PAD),
                   pl.BlockSpec((1, n_sub, D_ATTN, sub_w), lambda i: (i, 0, 0, 0))],
        out_shape=[jax.ShapeDtypeStruct((t, C_CONV), F32),
                   jax.ShapeDtypeStruct((t, N_HEADS * HEAD_PAD), BF16),
                   jax.ShapeDtypeStruct((t, N_HEADS * HEAD_PAD), BF16),
                   jax.ShapeDtypeStruct((t // tm, n_sub, D_ATTN, sub_w), BF16)],
        compiler_params=pltpu.CompilerParams(dimension_semantics=("arbitrary",),
                                             vmem_limit_bytes=VMEM_LIMIT),
        name="proj",
    )(x2d, ln_g, ln_b, w_in, qg, wuq, kvg, wuk, wvt, ctab, stab)


def _attn_kernel(q_ref, k_ref, vt_ref, km_ref, vtm_ref, o_ref, s_sc):
    seq = q_ref.shape[1]
    nq = seq // ATT_T
    key_chunk = lax.broadcasted_iota(jnp.int32, (ATT_T, ATT_T), 0) // CHUNK
    qry_chunk = lax.broadcasted_iota(jnp.int32, (ATT_T, ATT_T), 1) // CHUNK
    diag_visible = key_chunk <= qry_chunk
    nt = (((1,), (1,)), ((), ()))
    heads = range(ATT_HEADS)
    hsl = [slice(HEAD_PAD * hh, HEAD_PAD * (hh + 1)) for hh in heads]
    vsl = [slice(V_DIM * hh, V_DIM * (hh + 1)) for hh in heads]

    def update(s, vt, m, l, acc):
        m_new = jnp.maximum(m, jnp.max(s, axis=0, keepdims=True))
        alpha = jnp.exp2(m - m_new)
        p = jnp.exp2(s - m_new)
        l = alpha * l + jnp.sum(p, axis=0, keepdims=True)
        acc = alpha * acc + jnp.dot(vt, p.astype(BF16), preferred_element_type=F32)
        return m_new, l, acc

    def q_body(qi, carry):
        row0 = pl.multiple_of(qi * ATT_T, ATT_T)
        qs = [q_ref[0, pl.ds(row0, ATT_T), hsl[hh]] for hh in heads]
        state = []
        for hh in heads:
            s = lax.dot_general(km_ref[:, hsl[hh]], qs[hh], nt, preferred_element_type=F32)
            m = jnp.max(s, axis=0, keepdims=True)
            p = jnp.exp2(s - m)
            l = jnp.sum(p, axis=0, keepdims=True)
            acc = jnp.dot(vtm_ref[0, 0, vsl[hh], :], p.astype(BF16), preferred_element_type=F32)
            state += [m, l, acc]

        def scores(j, hh):
            k0 = pl.multiple_of(j * ATT_T, ATT_T)
            return lax.dot_general(k_ref[0, pl.ds(k0, ATT_T), hsl[hh]], qs[hh], nt, preferred_element_type=F32)

        for hh in heads:
            s_sc[hh] = scores(0, hh)

        def kv_body(j, st):
            out = []
            for hh in heads:
                s_cur = s_sc[hh]
                s_next = scores(j + 1, hh)
                out += update(s_cur, vt_ref[0, j, vsl[hh], :], *st[3 * hh:3 * hh + 3])
                s_sc[hh] = s_next
            return tuple(out)

        state = lax.fori_loop(0, qi, kv_body, tuple(state))
        outs = []
        for hh in heads:
            s = jnp.where(diag_visible, s_sc[hh], NEG_INF)
            m, l, acc = update(s, vt_ref[0, qi, vsl[hh], :], *state[3 * hh:3 * hh + 3])
            outs.append(acc / l)
        o_ref[0, pl.ds(row0, ATT_T), :] = jnp.concatenate(outs, axis=0).T
        return carry

    lax.fori_loop(0, nq, q_body, 0)


def _attn_call(q, k, vt, k_meta, vt_meta):
    b, seq, _ = q.shape
    nk = seq // ATT_T
    hb = ATT_HEADS
    return pl.pallas_call(
        _attn_kernel,
        grid=(b, N_HEADS // hb),
        in_specs=[pl.BlockSpec((1, seq, hb * HEAD_PAD), lambda i, j: (i, 0, j)),
                  pl.BlockSpec((1, seq, hb * HEAD_PAD), lambda i, j: (i, 0, j)),
                  pl.BlockSpec((1, nk, hb * V_DIM, ATT_T), lambda i, j: (i, 0, j, 0)),
                  pl.BlockSpec((N_META, hb * HEAD_PAD), lambda i, j: (0, j)),
                  pl.BlockSpec((1, 1, hb * V_DIM, N_META), lambda i, j: (0, 0, j, 0))],
        out_specs=pl.BlockSpec((1, seq, hb * V_DIM), lambda i, j: (i, 0, j)),
        out_shape=jax.ShapeDtypeStruct((b, seq, D_ATTN), F32),
        scratch_shapes=[pltpu.VMEM((hb, ATT_T, ATT_T), F32)],
        compiler_params=pltpu.CompilerParams(dimension_semantics=("arbitrary", "arbitrary"),
                                             vmem_limit_bytes=VMEM_LIMIT),
        name="attn",
    )(q, k, vt, k_meta, vt_meta)


def _mix_kernel(x_ref, ig_ref, ib_ref, u_ref, uh_ref, pre_ref, cw_ref, cb_ref, clg_ref, clb_ref,
                cog_ref, o_ref, aog_ref, wo_ref, l1g_ref, l1b_ref, wr_ref, br_ref,
                h1_ref, ri_ref, rw_ref, cnt_ref, win_sc, carry_sc):
    bi = pl.program_id(0)
    ti = pl.program_id(1)
    tm = MIX_TM

    @pl.when((bi == 0) & (ti == 0))
    def _():
        carry_sc[...] = jnp.zeros_like(carry_sc)

    win_sc[0:HALO, :] = jnp.where(ti == 0, pre_ref[...], uh_ref[0])
    win_sc[HALO:, :] = u_ref[0]
    acc = jnp.broadcast_to(cb_ref[...], (tm, C_CONV))
    for kk in range(CONV_K):
        acc = acc + win_sc[pl.ds(HALO - (CONV_K - 1) + kk, tm), :] * cw_ref[kk:kk + 1, :]
    c = _ln(acc, clg_ref[...], clb_ref[...])
    c = c * _sigmoid(c)
    cn = _rms(c, cog_ref[...])
    on = _rms(o_ref[0], aog_ref[...])
    y = (jnp.dot(cn.astype(BF16), wo_ref[0:C_CONV, :], preferred_element_type=F32)
         + jnp.dot(on.astype(BF16), wo_ref[C_CONV:, :], preferred_element_type=F32))
    h = _ln(x_ref[0], ig_ref[...], ib_ref[...])
    h1 = _ln(DEEPNORM_ALPHA * h + y, l1g_ref[...], l1b_ref[...])
    h1_ref[0] = h1

    logits = jnp.dot(h1, wr_ref[...], preferred_element_type=F32,
                     precision=lax.Precision.HIGHEST) + br_ref[...]
    lane = lax.broadcasted_iota(jnp.int32, (tm, LANES), 1)
    big = jnp.int32(1 << 20)
    is_g = (lane >= N_EXPERTS) & (lane < N_EXPERTS + N_GROUPS)
    gl = jnp.where(is_g, logits, -jnp.inf)
    gmax = jnp.max(gl, axis=1, keepdims=True)
    g_idx = jnp.min(jnp.where(gl == gmax, lane - N_EXPERTS, big), axis=1, keepdims=True)
    g_w = 1.0 / jnp.sum(jnp.exp(gl - gmax), axis=1, keepdims=True)
    in_grp = (lane < N_EXPERTS) & ((lane // EXPERTS_PER_GROUP) == g_idx)
    el = jnp.where(in_grp, logits, -jnp.inf)
    v1 = jnp.max(el, axis=1, keepdims=True)
    i1 = jnp.min(jnp.where(el == v1, lane, big), axis=1, keepdims=True)
    el2 = jnp.where(lane == i1, -jnp.inf, el)
    v2 = jnp.max(el2, axis=1, keepdims=True)
    i2 = jnp.min(jnp.where(el2 == v2, lane, big), axis=1, keepdims=True)
    e2 = jnp.exp(v2 - v1)
    w1 = g_w / (1.0 + e2)
    w2 = g_w * e2 / (1.0 + e2)

    oh1 = lane == i1
    oh2 = lane == i2
    oh = jnp.where(oh1 | oh2, 1.0, 0.0)
    rt = lax.broadcasted_iota(jnp.int32, (tm, tm), 0)
    ct = lax.broadcasted_iota(jnp.int32, (tm, tm), 1)
    ltri = jnp.where(ct < rt, 1.0, 0.0).astype(BF16)
    before = jnp.dot(ltri, oh.astype(BF16), preferred_element_type=F32) + carry_sc[...]
    r1 = jnp.sum(jnp.where(oh1, before, 0.0), axis=1, keepdims=True).astype(jnp.int32)
    r2 = jnp.sum(jnp.where(oh2, before, 0.0), axis=1, keepdims=True).astype(jnp.int32)
    carry_sc[...] = carry_sc[...] + jnp.sum(oh, axis=0, keepdims=True)
    cnt_ref[...] = carry_sc[...]

    ri = jnp.where(lane == 0, i1, jnp.where(lane == 1, i2, jnp.where(lane == 2, r1, r2)))
    rw = jnp.where(lane == 0, w1, w2)
    ri_ref[0] = ri[:, :8]
    rw_ref[0] = rw[:, :8]


def _mix_call(x, ig, ib, u, pre, cw, cb, clg, clb, cog, o, aog, wo, l1g, l1b, wr, br):
    b, seq, _ = x.shape
    tm = MIX_TM
    nt = seq // tm
    full = lambda shape: pl.BlockSpec(shape, lambda i, j: (0,) * len(shape))
    tile = lambda w: pl.BlockSpec((1, tm, w), lambda i, j: (i, j, 0))
    halo = pl.BlockSpec((1, HALO, C_CONV), lambda i, j: (i, jnp.maximum(j * (tm // HALO) - 1, 0), 0))
    return pl.pallas_call(
        _mix_kernel,
        grid=(b, nt),
        in_specs=[tile(D_MODEL), full((1, D_MODEL)), full((1, D_MODEL)),
                  tile(C_CONV), halo, full((HALO, C_CONV)), full((HALO, C_CONV)), full((1, C_CONV)),
                  full((1, C_CONV)), full((1, C_CONV)), full((1, C_CONV)),
                  tile(D_ATTN), full((1, D_ATTN)), full((D_MODEL, D_MODEL)),
                  full((1, D_MODEL)), full((1, D_MODEL)), full((D_MODEL, LANES)), full((1, LANES))],
        out_specs=[tile(D_MODEL), tile(8), tile(8), full((1, LANES))],
        out_shape=[jax.ShapeDtypeStruct((b, seq, D_MODEL), F32),
                   jax.ShapeDtypeStruct((b, seq, 8), jnp.int32),
                   jax.ShapeDtypeStruct((b, seq, 8), F32),
                   jax.ShapeDtypeStruct((1, LANES), F32)],
        scratch_shapes=[pltpu.VMEM((HALO + tm, C_CONV), F32), pltpu.VMEM((1, LANES), F32)],
        compiler_params=pltpu.CompilerParams(dimension_semantics=("arbitrary", "arbitrary"),
                                             vmem_limit_bytes=VMEM_LIMIT),
        name="mix",
    )(x, ig, ib, u, u, pre, cw, cb, clg, clb, cog, o, aog, wo, l1g, l1b, wr, br)


def _dispatch_kernel(dest_ref, h_ref, xz_ref, xb_ref, sem):
    del xz_ref
    tm = DISP_TM

    def issue(t, carry):
        src = h_ref.at[pl.ds(t, 1)]
        pltpu.make_async_copy(src, xb_ref.at[pl.ds(dest_ref[2 * t], 1)], sem).start()
        pltpu.make_async_copy(src, xb_ref.at[pl.ds(dest_ref[2 * t + 1], 1)], sem).start()
        return carry

    lax.fori_loop(0, tm, issue, 0, unroll=8)
    for _ in range(2):
        pltpu.make_async_copy(h_ref, xb_ref.at[pl.ds(0, tm)], sem).wait()


def _dispatch_call(dest, h1, xz):
    t = h1.shape[0]
    tm = DISP_TM
    return pl.pallas_call(
        _dispatch_kernel,
        grid=(t // tm,),
        in_specs=[pl.BlockSpec((2 * tm,), lambda i: (i,), memory_space=pltpu.SMEM),
                  pl.BlockSpec((tm, D_MODEL), lambda i: (i, 0)),
                  pl.BlockSpec(memory_space=pl.ANY)],
        out_specs=pl.BlockSpec(memory_space=pl.ANY),
        out_shape=jax.ShapeDtypeStruct(xz.shape, F32),
        scratch_shapes=[pltpu.SemaphoreType.DMA],
        input_output_aliases={2: 0},
        compiler_params=pltpu.CompilerParams(dimension_semantics=("arbitrary",),
                                             vmem_limit_bytes=VMEM_LIMIT),
        name="dispatch",
    )(dest, h1, xz)


def _expert_kernel(be_ref, nu_ref, x_ref, wgu_ref, wd_ref, y_ref):
    i = pl.program_id(0)

    @pl.when(i < nu_ref[0])
    def _():
        gu = jnp.dot(x_ref[...].astype(BF16), wgu_ref[0], preferred_element_type=F32)
        g = gu[:, :D_EXPERT]
        act = g * _sigmoid(g) * gu[:, D_EXPERT:]
        y_ref[...] = jnp.dot(act.astype(BF16), wd_ref[0], preferred_element_type=F32)

    @pl.when(i >= nu_ref[0])
    def _():
        y_ref[...] = jnp.zeros_like(y_ref)


def _expert_call(blk_e, n_used, xb, wgu, wd):
    r = xb.shape[0]
    bm = MOE_BM
    return pl.pallas_call(
        _expert_kernel,
        grid_spec=pltpu.PrefetchScalarGridSpec(
            num_scalar_prefetch=2,
            grid=(r // bm,),
            in_specs=[pl.BlockSpec((bm, D_MODEL), lambda i, be, nu: (i, 0)),
                      pl.BlockSpec((1, D_MODEL, 2 * D_EXPERT), lambda i, be, nu: (be[i], 0, 0)),
                      pl.BlockSpec((1, D_EXPERT, D_MODEL), lambda i, be, nu: (be[i], 0, 0))],
            out_specs=pl.BlockSpec((bm, D_MODEL), lambda i, be, nu: (i, 0))),
        out_shape=jax.ShapeDtypeStruct((r, D_MODEL), F32),
        compiler_params=pltpu.CompilerParams(dimension_semantics=("arbitrary",),
                                             vmem_limit_bytes=VMEM_LIMIT),
        name="experts",
    )(blk_e, n_used, xb, wgu, wd)


def _combine_kernel(dest_ref, h_ref, w_ref, g_ref, b_ref, yb_ref, o_ref, y0_sc, y1_sc, sem):
    tm = COMB_TM

    def issue(t, carry):
        pltpu.make_async_copy(yb_ref.at[pl.ds(dest_ref[2 * t], 1)], y0_sc.at[pl.ds(t, 1)], sem).start()
        pltpu.make_async_copy(yb_ref.at[pl.ds(dest_ref[2 * t + 1], 1)], y1_sc.at[pl.ds(t, 1)], sem).start()
        return carry

    lax.fori_loop(0, tm, issue, 0, unroll=8)
    pltpu.make_async_copy(yb_ref.at[pl.ds(0, tm)], y0_sc, sem).wait()
    pltpu.make_async_copy(yb_ref.at[pl.ds(0, tm)], y1_sc, sem).wait()
    w = w_ref[...]
    y = w[:, 0:1] * y0_sc[...] + w[:, 1:2] * y1_sc[...]
    o_ref[...] = _ln(DEEPNORM_ALPHA * h_ref[...] + y, g_ref[...], b_ref[...])


def _combine_call(dest, h1, rw, g, b, yb):
    t = h1.shape[0]
    tm = COMB_TM
    return pl.pallas_call(
        _combine_kernel,
        grid=(t // tm,),
        in_specs=[pl.BlockSpec((2 * tm,), lambda i: (i,), memory_space=pltpu.SMEM),
                  pl.BlockSpec((tm, D_MODEL), lambda i: (i, 0)),
                  pl.BlockSpec((tm, 8), lambda i: (i, 0)),
                  pl.BlockSpec((1, D_MODEL), lambda i: (0, 0)),
                  pl.BlockSpec((1, D_MODEL), lambda i: (0, 0)),
                  pl.BlockSpec(memory_space=pl.ANY)],
        out_specs=pl.BlockSpec((tm, D_MODEL), lambda i: (i, 0)),
        out_shape=jax.ShapeDtypeStruct((t, D_MODEL), F32),
        scratch_shapes=[pltpu.VMEM((tm, D_MODEL), F32), pltpu.VMEM((tm, D_MODEL), F32),
                        pltpu.SemaphoreType.DMA],
        compiler_params=pltpu.CompilerParams(dimension_semantics=("arbitrary",),
                                             vmem_limit_bytes=VMEM_LIMIT),
        name="combine",
    )(dest, h1, rw, g, b, yb)


def _take_cols(w, idx):
    wz = jnp.concatenate([w, jnp.zeros((w.shape[0], 1), w.dtype)], axis=1)
    return wz[:, np.where(idx < 0, w.shape[1], idx)]


def _layout_indices():
    lane_src = np.full((HEAD_PAD,), -1, np.int64)
    lane_src[0:ROPE_HALF] = QK_NOPE + np.arange(ROPE_HALF)
    lane_src[ROPE_HALF:X2_LANE] = np.arange(X2_LANE - ROPE_HALF)
    lane_src[X2_LANE:X2_LANE + ROPE_HALF] = QK_NOPE + ROPE_HALF + np.arange(ROPE_HALF)
    n_lo = X2_LANE - ROPE_HALF
    lane_src[X2_LANE + ROPE_HALF:X2_LANE + ROPE_HALF + QK_NOPE - n_lo] = n_lo + np.arange(QK_NOPE - n_lo)
    q_idx = np.concatenate([np.where(lane_src >= 0, lane_src + (QK_NOPE + QK_ROPE) * h, -1)
                            for h in range(N_HEADS)])
    nope_src = np.where(lane_src < QK_NOPE, lane_src, -1)
    k_idx = np.concatenate([np.where(nope_src >= 0, nope_src + (QK_NOPE + V_DIM) * h, -1)
                            for h in range(N_HEADS)])
    v_idx = np.concatenate([QK_NOPE + np.arange(V_DIM) + (QK_NOPE + V_DIM) * h for h in range(N_HEADS)])
    kr_idx = np.full((LANES,), -1, np.int64)
    kr_idx[0:ROPE_HALF] = np.arange(ROPE_HALF)
    kr_idx[X2_LANE:X2_LANE + ROPE_HALF] = ROPE_HALF + np.arange(ROPE_HALF)
    return q_idx, k_idx, v_idx, kr_idx


def _rope_lane_tables(length):
    inv_freq = ROPE_THETA ** (-jnp.arange(ROPE_HALF, dtype=F32) / ROPE_HALF)
    ang = jnp.arange(length, dtype=F32)[:, None] * inv_freq[None, :]
    cos, sin = jnp.cos(ang), jnp.sin(ang)
    ctab = jnp.ones((length, LANES), F32)
    ctab = ctab.at[:, 0:ROPE_HALF].set(cos).at[:, X2_LANE:X2_LANE + ROPE_HALF].set(cos)
    stab = jnp.zeros((length, LANES), F32)
    stab = stab.at[:, 0:ROPE_HALF].set(-sin).at[:, X2_LANE:X2_LANE + ROPE_HALF].set(sin)
    return ctab, stab


def kernel(x, meta_tokens, ln_in_g, ln_in_b, w_in, conv_w, conv_b, conv_ln_g, conv_ln_b, q_norm_g, w_uq,
           kv_norm_g, w_ukv, conv_out_g, attn_out_g, w_o, ln1_g, ln1_b, w_rg, b_rg, w_re, b_re,
           w_gate, w_up, w_down, ln2_g, ln2_b):
    assert w_in.shape[0] == 1, "single layer"
    bsz, seq, d = x.shape
    t = bsz * seq
    row = lambda a: a.reshape(1, -1)

    q_idx, k_idx, v_idx, kr_idx = _layout_indices()
    c4 = 2 * C_CONV + Q_LORA + KV_LORA
    w_in_l = jnp.concatenate([w_in[0][:, :c4], _take_cols(w_in[0][:, c4:], kr_idx)], axis=1).astype(BF16)
    wuq_l = _take_cols(w_uq[0], q_idx).astype(BF16)
    wuk_l = _take_cols(w_ukv[0], k_idx).astype(BF16)
    wvt_l = w_ukv[0][:, v_idx].T.astype(BF16)
    wo_l = w_o[0].astype(BF16)
    wgu_l = jnp.concatenate([w_gate[0], w_up[0]], axis=2).astype(BF16)
    wd_l = w_down[0].astype(BF16)
    wr_l = jnp.concatenate([w_re[0], w_rg[0], jnp.zeros((d, LANES - N_EXPERTS - N_GROUPS), F32)], axis=1)
    br_l = jnp.concatenate([b_re[0], b_rg[0], jnp.zeros((LANES - N_EXPERTS - N_GROUPS,), F32)]).reshape(1, LANES)
    ctab, stab = _rope_lane_tables(N_META + seq)

    proj_w = (row(ln_in_g), row(ln_in_b), w_in_l, row(q_norm_g[0]), wuq_l, row(kv_norm_g[0]), wuk_l, wvt_l)
    u_m, _, k_m, vt_m = _proj_call(meta_tokens.astype(F32), N_META, 1, *proj_w, ctab[:N_META], stab[:N_META])
    u, q, k, vt = _proj_call(x.reshape(t, d), PROJ_TM, seq // PROJ_TM, *proj_w, ctab[N_META:], stab[N_META:])

    o = _attn_call(q.reshape(bsz, seq, -1), k.reshape(bsz, seq, -1),
                   vt.reshape(bsz, seq // ATT_T, D_ATTN, ATT_T), k_m, vt_m)

    pre = jnp.concatenate([jnp.zeros((HALO - N_META, C_CONV), F32), u_m], axis=0)
    cw = jnp.concatenate([conv_w[0], jnp.zeros((HALO - CONV_K, C_CONV), F32)], axis=0)
    h1, ri, rw, cnt = _mix_call(x, row(ln_in_g), row(ln_in_b), u.reshape(bsz, seq, C_CONV), pre, cw,
                                row(conv_b[0]), row(conv_ln_g[0]), row(conv_ln_b[0]), row(conv_out_g[0]),
                                o, row(attn_out_g[0]), wo_l, row(ln1_g[0]), row(ln1_b[0]), wr_l, br_l)

    counts = cnt[0, :N_EXPERTS].astype(jnp.int32)
    padded = (counts + MOE_BM - 1) // MOE_BM * MOE_BM
    pad_end = jnp.cumsum(padded)
    pad_start = pad_end - padded
    n_blocks = (2 * t) // MOE_BM + N_EXPERTS
    ri2 = ri.reshape(t, 8)
    dest = (pad_start[ri2[:, 0:2]] + ri2[:, 2:4]).reshape(2 * t)
    blk_e = jnp.clip(jnp.searchsorted(pad_end, jnp.arange(n_blocks, dtype=jnp.int32) * MOE_BM, side='right'),
                     0, N_EXPERTS - 1).astype(jnp.int32)
    n_used = (pad_end[-1:] // MOE_BM).astype(jnp.int32)

    h1f = h1.reshape(t, d)
    xb = _dispatch_call(dest, h1f, jnp.zeros((n_blocks * MOE_BM, d), F32))
    yb = _expert_call(blk_e, n_used, xb, wgu_l, wd_l)
    out = _combine_call(dest, h1f, rw.reshape(t, 8), row(ln2_g[0]), row(ln2_b[0]), yb)
    return out.reshape(bsz, seq, d)
```

```python
import math

import numpy as np
import jax
import jax.numpy as jnp
from jax import lax
from jax.experimental import pallas as pl
from jax.experimental.pallas import tpu as pltpu

F32 = jnp.float32
BF16 = jnp.bfloat16

D_MODEL = 1024
N_META = 16
CHUNK = 64
C_CONV = 512
CONV_K = 31
N_HEADS = 8
QK_NOPE = 64
QK_ROPE = 32
V_DIM = 64
Q_LORA = 384
KV_LORA = 256
ROPE_THETA = 10000.0
D_ATTN = N_HEADS * V_DIM
N_GROUPS = 4
EXPERTS_PER_GROUP = 8
N_EXPERTS = N_GROUPS * EXPERTS_PER_GROUP
D_EXPERT = 256
DEEPNORM_ALPHA = 2.0 ** 0.25
EPS = 1e-5
NEG_INF = -1e30

LANES = 128
SUBLANES = 8
HEAD_PAD = LANES
ROPE_HALF = QK_ROPE // 2
X2_LANE = 64
Z_COLS = 2 * C_CONV + Q_LORA + KV_LORA + LANES
Q_SCALE = (QK_NOPE + QK_ROPE) ** -0.5 * math.log2(math.e)

PROJ_TM = 512
ATT_T = 256
ATT_HEADS = 8
MIX_TM = 256
HALO = 32
MOE_BM = 256
DISP_TM = 512
COMB_TM = 512
VMEM_LIMIT = 56 * 1024 * 1024


def _ln(x, g, b):
    mu = jnp.mean(x, axis=-1, keepdims=True)
    xc = x - mu
    var = jnp.mean(xc * xc, axis=-1, keepdims=True)
    return xc * lax.rsqrt(var + EPS) * g + b


def _rms(x, g):
    return x * lax.rsqrt(jnp.mean(x * x, axis=-1, keepdims=True) + EPS) * g


def _sigmoid(x):
    return 1.0 / (1.0 + jnp.exp(-x))


def _proj_kernel(x_ref, g_ref, b_ref, win_ref, qg_ref, wuq_ref, kvg_ref, wuk_ref, wvt_ref, c_ref, s_ref,
                 u_ref, q_ref, k_ref, vt_ref):
    h = _ln(x_ref[...], g_ref[...], b_ref[...])
    z = jnp.dot(h.astype(BF16), win_ref[...], preferred_element_type=F32)
    u_ref[...] = z[:, :C_CONV] * _sigmoid(z[:, C_CONV:2 * C_CONV])
    c0 = 2 * C_CONV
    cq = z[:, c0:c0 + Q_LORA]
    ckv = z[:, c0 + Q_LORA:c0 + Q_LORA + KV_LORA]
    kr = z[:, c0 + Q_LORA + KV_LORA:]
    cs = c_ref[...]
    sn = s_ref[...]
    q = jnp.dot(_rms(cq, qg_ref[...]).astype(BF16), wuq_ref[...], preferred_element_type=F32)
    kvn = _rms(ckv, kvg_ref[...]).astype(BF16)
    kn = jnp.dot(kvn, wuk_ref[...], preferred_element_type=F32)
    kr_rot = kr * cs + pltpu.roll(kr, X2_LANE, 1) * sn
    for hh in range(N_HEADS):
        sl = slice(HEAD_PAD * hh, HEAD_PAD * (hh + 1))
        qh = q[:, sl]
        qh = (qh * cs + pltpu.roll(qh, X2_LANE, 1) * sn) * Q_SCALE
        q_ref[:, sl] = qh.astype(BF16)
        k_ref[:, sl] = (kn[:, sl] + kr_rot).astype(BF16)
    vt = lax.dot_general(wvt_ref[...], kvn, (((1,), (1,)), ((), ())), preferred_element_type=F32)
    n_sub, sub_w = vt_ref.shape[1], vt_ref.shape[3]
    for tt in range(n_sub):
        vt_ref[0, tt] = vt[:, tt * sub_w:(tt + 1) * sub_w].astype(BF16)


def _proj_call(x2d, tm, pos_blocks, ln_g, ln_b, w_in, qg, wuq, kvg, wuk, wvt, ctab, stab):
    t = x2d.shape[0]
    sub_w = min(tm, ATT_T)
    n_sub = tm // sub_w
    full = lambda shape: pl.BlockSpec(shape, lambda i: (0,) * len(shape))
    row = lambda w: pl.BlockSpec((tm, w), lambda i: (i, 0))
    tab = pl.BlockSpec((tm, LANES), lambda i: (i % pos_blocks, 0))
    return pl.pallas_call(
        _proj_kernel,
        grid=(t // tm,),
        in_specs=[row(D_MODEL), full((1, D_MODEL)), full((1, D_MODEL)), full((D_MODEL, Z_COLS)),
                  full((1, Q_LORA)), full((Q_LORA, N_HEADS * HEAD_PAD)),
                  full((1, KV_LORA)), full((KV_LORA, N_HEADS * HEAD_PAD)), full((D_ATTN, KV_LORA)), tab, tab],
        out_specs=[row(C_CONV), row(N_HEADS * HEAD_PAD), row(N_HEADS * HEAD_PAD),
                   pl.BlockSpec((1, n_sub, D_ATTN, sub_w), lambda i: (i, 0, 0, 0))],
        out_shape=[jax.ShapeDtypeStruct((t, C_CONV), F32),
                   jax.ShapeDtypeStruct((t, N_HEADS * HEAD_PAD), BF16),
                   jax.ShapeDtypeStruct((t, N_HEADS * HEAD_PAD), BF16),
                   jax.ShapeDtypeStruct((t // tm, n_sub, D_ATTN, sub_w), BF16)],
        compiler_params=pltpu.CompilerParams(dimension_semantics=("arbitrary",),
                                             vmem_limit_bytes=VMEM_LIMIT),
        name="proj",
    )(x2d, ln_g, ln_b, w_in, qg, wuq, kvg, wuk, wvt, ctab, stab)


def _attn_kernel(q_ref, k_ref, vt_ref, km_ref, vtm_ref, o_ref, s_sc):
    seq = q_ref.shape[1]
    nq = seq // ATT_T
    key_chunk = lax.broadcasted_iota(jnp.int32, (ATT_T, ATT_T), 0) // CHUNK
    qry_chunk = lax.broadcasted_iota(jnp.int32, (ATT_T, ATT_T), 1) // CHUNK
    diag_visible = key_chunk <= qry_chunk
    nt = (((1,), (1,)), ((), ()))
    heads = range(ATT_HEADS)
    hsl = [slice(HEAD_PAD * hh, HEAD_PAD * (hh + 1)) for hh in heads]
    vsl = [slice(V_DIM * hh, V_DIM * (hh + 1)) for hh in heads]

    def update(s, vt, m, l, acc):
        m_new = jnp.maximum(m, jnp.max(s, axis=0, keepdims=True))
        alpha = jnp.exp2(m - m_new)
        p = jnp.exp2(s - m_new)
        l = alpha * l + jnp.sum(p, axis=0, keepdims=True)
        acc = alpha * acc + jnp.dot(vt, p.astype(BF16), preferred_element_type=F32)
        return m_new, l, acc

    def q_body(qi, carry):
        row0 = pl.multiple_of(qi * ATT_T, ATT_T)
        qs = [q_ref[0, pl.ds(row0, ATT_T), hsl[hh]] for hh in heads]
        state = []
        for hh in heads:
            s = lax.dot_general(km_ref[:, hsl[hh]], qs[hh], nt, preferred_element_type=F32)
            m = jnp.max(s, axis=0, keepdims=True)
            p = jnp.exp2(s - m)
            l = jnp.sum(p, axis=0, keepdims=True)
            acc = jnp.dot(vtm_ref[0, 0, vsl[hh], :], p.astype(BF16), preferred_element_type=F32)
            state += [m, l, acc]

        def scores(j, hh):
            k0 = pl.multiple_of(j * ATT_T, ATT_T)
            return lax.dot_general(k_ref[0, pl.ds(k0, ATT_T), hsl[hh]], qs[hh], nt, preferred_element_type=F32)

        for hh in heads:
            s_sc[hh] = scores(0, hh)

        def kv_body(j, st):
            out = []
            for hh in heads:
                s_cur = s_sc[hh]
                s_next = scores(j + 1, hh)
                out += update(s_cur, vt_ref[0, j, vsl[hh], :], *st[3 * hh:3 * hh + 3])
                s_sc[hh] = s_next
            return tuple(out)

        state = lax.fori_loop(0, qi, kv_body, tuple(state))
        outs = []
        for hh in heads:
            s = jnp.where(diag_visible, s_sc[hh], NEG_INF)
            m, l, acc = update(s, vt_ref[0, qi, vsl[hh], :], *state[3 * hh:3 * hh + 3])
            outs.append(acc / l)
        o_ref[0, pl.ds(row0, ATT_T), :] = jnp.concatenate(outs, axis=0).T
        return carry

    lax.fori_loop(0, nq, q_body, 0)


def _attn_call(q, k, vt, k_meta, vt_meta):
    b, seq, _ = q.shape
    nk = seq // ATT_T
    hb = ATT_HEADS
    return pl.pallas_call(
        _attn_kernel,
        grid=(b, N_HEADS // hb),
        in_specs=[pl.BlockSpec((1, seq, hb * HEAD_PAD), lambda i, j: (i, 0, j)),
                  pl.BlockSpec((1, seq, hb * HEAD_PAD), lambda i, j: (i, 0, j)),
                  pl.BlockSpec((1, nk, hb * V_DIM, ATT_T), lambda i, j: (i, 0, j, 0)),
                  pl.BlockSpec((N_META, hb * HEAD_PAD), lambda i, j: (0, j)),
                  pl.BlockSpec((1, 1, hb * V_DIM, N_META), lambda i, j: (0, 0, j, 0))],
        out_specs=pl.BlockSpec((1, seq, hb * V_DIM), lambda i, j: (i, 0, j)),
        out_shape=jax.ShapeDtypeStruct((b, seq, D_ATTN), F32),
        scratch_shapes=[pltpu.VMEM((hb, ATT_T, ATT_T), F32)],
        compiler_params=pltpu.CompilerParams(dimension_semantics=("arbitrary", "arbitrary"),
                                             vmem_limit_bytes=VMEM_LIMIT),
        name="attn",
    )(q, k, vt, k_meta, vt_meta)


def _mix_kernel(x_ref, ig_ref, ib_ref, u_ref, uh_ref, pre_ref, cw_ref, cb_ref, clg_ref, clb_ref,
                cog_ref, o_ref, aog_ref, wo_ref, l1g_ref, l1b_ref, wr_ref, br_ref,
                h1_ref, ri_ref, rw_ref, cnt_ref, win_sc, sh_sc, carry_sc):
    bi = pl.program_id(0)
    ti = pl.program_id(1)
    tm = MIX_TM

    @pl.when((bi == 0) & (ti == 0))
    def _():
        carry_sc[...] = jnp.zeros_like(carry_sc)

    win_sc[0:HALO, :] = jnp.where(ti == 0, pre_ref[...], uh_ref[0])
    win_sc[HALO:, :] = u_ref[0]
    first = HALO - (CONV_K - 1)
    acc = jnp.broadcast_to(cb_ref[...], (tm, C_CONV))
    for res in range(SUBLANES):
        offs = [o for o in range(first, first + CONV_K) if o % SUBLANES == res]
        rows = tm + offs[-1] - res
        if res:
            sh_sc[res - 1, 0:rows, :] = win_sc[pl.ds(res, rows), :]
        for o in offs:
            src = sh_sc[res - 1, o - res:o - res + tm, :] if res else win_sc[o:o + tm, :]
            acc = acc + src * cw_ref[o - first:o - first + 1, :]
    c = _ln(acc, clg_ref[...], clb_ref[...])
    c = c * _sigmoid(c)
    cn = _rms(c, cog_ref[...])
    on = _rms(o_ref[0], aog_ref[...])
    y = (jnp.dot(cn.astype(BF16), wo_ref[0:C_CONV, :], preferred_element_type=F32)
         + jnp.dot(on.astype(BF16), wo_ref[C_CONV:, :], preferred_element_type=F32))
    h = _ln(x_ref[0], ig_ref[...], ib_ref[...])
    h1 = _ln(DEEPNORM_ALPHA * h + y, l1g_ref[...], l1b_ref[...])
    h1_ref[0] = h1

    logits = jnp.dot(h1, wr_ref[...], preferred_element_type=F32,
                     precision=lax.Precision.HIGHEST) + br_ref[...]
    lane = lax.broadcasted_iota(jnp.int32, (tm, LANES), 1)
    big = jnp.int32(1 << 20)
    is_g = (lane >= N_EXPERTS) & (lane < N_EXPERTS + N_GROUPS)
    gl = jnp.where(is_g, logits, -jnp.inf)
    gmax = jnp.max(gl, axis=1, keepdims=True)
    g_idx = jnp.min(jnp.where(gl == gmax, lane - N_EXPERTS, big), axis=1, keepdims=True)
    g_w = 1.0 / jnp.sum(jnp.exp(gl - gmax), axis=1, keepdims=True)
    in_grp = (lane < N_EXPERTS) & ((lane // EXPERTS_PER_GROUP) == g_idx)
    el = jnp.where(in_grp, logits, -jnp.inf)
    v1 = jnp.max(el, axis=1, keepdims=True)
    i1 = jnp.min(jnp.where(el == v1, lane, big), axis=1, keepdims=True)
    el2 = jnp.where(lane == i1, -jnp.inf, el)
    v2 = jnp.max(el2, axis=1, keepdims=True)
    i2 = jnp.min(jnp.where(el2 == v2, lane, big), axis=1, keepdims=True)
    e2 = jnp.exp(v2 - v1)
    w1 = g_w / (1.0 + e2)
    w2 = g_w * e2 / (1.0 + e2)

    oh1 = lane == i1
    oh2 = lane == i2
    oh = jnp.where(oh1 | oh2, 1.0, 0.0)
    rt = lax.broadcasted_iota(jnp.int32, (tm, tm), 0)
    ct = lax.broadcasted_iota(jnp.int32, (tm, tm), 1)
    ltri = jnp.where(ct < rt, 1.0, 0.0).astype(BF16)
    before = jnp.dot(ltri, oh.astype(BF16), preferred_element_type=F32) + carry_sc[...]
    r1 = jnp.sum(jnp.where(oh1, before, 0.0), axis=1, keepdims=True).astype(jnp.int32)
    r2 = jnp.sum(jnp.where(oh2, before, 0.0), axis=1, keepdims=True).astype(jnp.int32)
    carry_sc[...] = carry_sc[...] + jnp.sum(oh, axis=0, keepdims=True)
    cnt_ref[...] = carry_sc[...]

    ri = jnp.where(lane == 0, i1, jnp.where(lane == 1, i2, jnp.where(lane == 2, r1, r2)))
    rw = jnp.where(lane == 0, w1, w2)
    ri_ref[...] = ri.T[:8, :]
    rw_ref[0] = rw[:, :8]


def _mix_call(x, ig, ib, u, pre, cw, cb, clg, clb, cog, o, aog, wo, l1g, l1b, wr, br):
    b, seq, _ = x.shape
    tm = MIX_TM
    nt = seq // tm
    full = lambda shape: pl.BlockSpec(shape, lambda i, j: (0,) * len(shape))
    tile = lambda w: pl.BlockSpec((1, tm, w), lambda i, j: (i, j, 0))
    halo = pl.BlockSpec((1, HALO, C_CONV), lambda i, j: (i, jnp.maximum(j * (tm // HALO) - 1, 0), 0))
    return pl.pallas_call(
        _mix_kernel,
        grid=(b, nt),
        in_specs=[tile(D_MODEL), full((1, D_MODEL)), full((1, D_MODEL)),
                  tile(C_CONV), halo, full((HALO, C_CONV)), full((HALO, C_CONV)), full((1, C_CONV)),
                  full((1, C_CONV)), full((1, C_CONV)), full((1, C_CONV)),
                  tile(D_ATTN), full((1, D_ATTN)), full((D_MODEL, D_MODEL)),
                  full((1, D_MODEL)), full((1, D_MODEL)), full((D_MODEL, LANES)), full((1, LANES))],
        out_specs=[tile(D_MODEL), pl.BlockSpec((8, tm), lambda i, j: (0, i * nt + j)), tile(8),
                   full((1, LANES))],
        out_shape=[jax.ShapeDtypeStruct((b, seq, D_MODEL), F32),
                   jax.ShapeDtypeStruct((8, b * seq), jnp.int32),
                   jax.ShapeDtypeStruct((b, seq, 8), F32),
                   jax.ShapeDtypeStruct((1, LANES), F32)],
        scratch_shapes=[pltpu.VMEM((HALO + tm, C_CONV), F32),
                        pltpu.VMEM((SUBLANES - 1, HALO + tm, C_CONV), F32),
                        pltpu.VMEM((1, LANES), F32)],
        compiler_params=pltpu.CompilerParams(dimension_semantics=("arbitrary", "arbitrary"),
                                             vmem_limit_bytes=VMEM_LIMIT),
        name="mix",
    )(x, ig, ib, u, u, pre, cw, cb, clg, clb, cog, o, aog, wo, l1g, l1b, wr, br)


def _dispatch_kernel(nv_ref, dest_ref, h_ref, xb_ref, zero_sc, sem):
    tm = DISP_TM
    n_blocks = nv_ref.shape[0]

    @pl.when(pl.program_id(0) == 0)
    def _():
        zero_sc[...] = jnp.zeros_like(zero_sc)

        def zero_copy(i):
            return pltpu.make_async_copy(zero_sc, xb_ref.at[pl.ds(i * MOE_BM, MOE_BM)], sem)

        def start(i, carry):
            @pl.when(nv_ref[i] < MOE_BM)
            def _():
                zero_copy(i).start()
            return carry

        def wait(i, carry):
            @pl.when(nv_ref[i] < MOE_BM)
            def _():
                zero_copy(i).wait()
            return carry

        lax.fori_loop(0, n_blocks, start, 0)
        lax.fori_loop(0, n_blocks, wait, 0)

    def issue(t, carry):
        src = h_ref.at[pl.ds(t, 1)]
        pltpu.make_async_copy(src, xb_ref.at[pl.ds(dest_ref[2 * t], 1)], sem).start()
        pltpu.make_async_copy(src, xb_ref.at[pl.ds(dest_ref[2 * t + 1], 1)], sem).start()
        return carry

    lax.fori_loop(0, tm, issue, 0, unroll=8)
    for _ in range(2):
        pltpu.make_async_copy(h_ref, xb_ref.at[pl.ds(0, tm)], sem).wait()


def _dispatch_call(blk_valid, dest, h1):
    t = h1.shape[0]
    tm = DISP_TM
    return pl.pallas_call(
        _dispatch_kernel,
        grid_spec=pltpu.PrefetchScalarGridSpec(
            num_scalar_prefetch=1,
            grid=(t // tm,),
            in_specs=[pl.BlockSpec((2 * tm,), lambda i, nv: (i,), memory_space=pltpu.SMEM),
                      pl.BlockSpec((tm, D_MODEL), lambda i, nv: (i, 0))],
            out_specs=pl.BlockSpec(memory_space=pl.ANY),
            scratch_shapes=[pltpu.VMEM((MOE_BM, D_MODEL), F32), pltpu.SemaphoreType.DMA]),
        out_shape=jax.ShapeDtypeStruct((blk_valid.shape[0] * MOE_BM, D_MODEL), F32),
        compiler_params=pltpu.CompilerParams(dimension_semantics=("arbitrary",),
                                             vmem_limit_bytes=VMEM_LIMIT),
        name="dispatch",
    )(blk_valid, dest, h1)


def _expert_kernel(be_ref, nv_ref, x_ref, wgu_ref, wd_ref, y_ref):
    nv = nv_ref[pl.program_id(0)]

    @pl.when(nv > 0)
    def _():
        gu = jnp.dot(x_ref[...].astype(BF16), wgu_ref[0], preferred_element_type=F32)
        g = gu[:, :D_EXPERT]
        act = g * _sigmoid(g) * gu[:, D_EXPERT:]
        y_ref[...] = jnp.dot(act.astype(BF16), wd_ref[0], preferred_element_type=F32)

    @pl.when(nv == 0)
    def _():
        y_ref[...] = jnp.zeros_like(y_ref)


def _expert_call(blk_e, n_used, xb, wgu, wd):
    r = xb.shape[0]
    bm = MOE_BM
    return pl.pallas_call(
        _expert_kernel,
        grid_spec=pltpu.PrefetchScalarGridSpec(
            num_scalar_prefetch=2,
            grid=(r // bm,),
            in_specs=[pl.BlockSpec((bm, D_MODEL), lambda i, be, nu: (i, 0)),
                      pl.BlockSpec((1, D_MODEL, 2 * D_EXPERT), lambda i, be, nu: (be[i], 0, 0)),
                      pl.BlockSpec((1, D_EXPERT, D_MODEL), lambda i, be, nu: (be[i], 0, 0))],
            out_specs=pl.BlockSpec((bm, D_MODEL), lambda i, be, nu: (i, 0))),
        out_shape=jax.ShapeDtypeStruct((r, D_MODEL), F32),
        compiler_params=pltpu.CompilerParams(dimension_semantics=("arbitrary",),
                                             vmem_limit_bytes=VMEM_LIMIT),
        name="experts",
    )(blk_e, n_used, xb, wgu, wd)


def _combine_kernel(dest_ref, h_ref, w_ref, g_ref, b_ref, yb_ref, o_ref, y0_sc, y1_sc, sem):
    tm = COMB_TM

    def issue(t, carry):
        pltpu.make_async_copy(yb_ref.at[pl.ds(dest_ref[2 * t], 1)], y0_sc.at[pl.ds(t, 1)], sem).start()
        pltpu.make_async_copy(yb_ref.at[pl.ds(dest_ref[2 * t + 1], 1)], y1_sc.at[pl.ds(t, 1)], sem).start()
        return carry

    lax.fori_loop(0, tm, issue, 0, unroll=8)
    pltpu.make_async_copy(yb_ref.at[pl.ds(0, tm)], y0_sc, sem).wait()
    pltpu.make_async_copy(yb_ref.at[pl.ds(0, tm)], y1_sc, sem).wait()
    w = w_ref[...]
    y = w[:, 0:1] * y0_sc[...] + w[:, 1:2] * y1_sc[...]
    o_ref[...] = _ln(DEEPNORM_ALPHA * h_ref[...] + y, g_ref[...], b_ref[...])


def _combine_call(dest, h1, rw, g, b, yb):
    t = h1.shape[0]
    tm = COMB_TM
    return pl.pallas_call(
        _combine_kernel,
        grid=(t // tm,),
        in_specs=[pl.BlockSpec((2 * tm,), lambda i: (i,), memory_space=pltpu.SMEM),
                  pl.BlockSpec((tm, D_MODEL), lambda i: (i, 0)),
                  pl.BlockSpec((tm, 8), lambda i: (i, 0)),
                  pl.BlockSpec((1, D_MODEL), lambda i: (0, 0)),
                  pl.BlockSpec((1, D_MODEL), lambda i: (0, 0)),
                  pl.BlockSpec(memory_space=pl.ANY)],
        out_specs=pl.BlockSpec((tm, D_MODEL), lambda i: (i, 0)),
        out_shape=jax.ShapeDtypeStruct((t, D_MODEL), F32),
        scratch_shapes=[pltpu.VMEM((tm, D_MODEL), F32), pltpu.VMEM((tm, D_MODEL), F32),
                        pltpu.SemaphoreType.DMA],
        compiler_params=pltpu.CompilerParams(dimension_semantics=("arbitrary",),
                                             vmem_limit_bytes=VMEM_LIMIT),
        name="combine",
    )(dest, h1, rw, g, b, yb)


def _take_cols(w, idx):
    wz = jnp.concatenate([w, jnp.zeros((w.shape[0], 1), w.dtype)], axis=1)
    return wz[:, np.where(idx < 0, w.shape[1], idx)]


def _layout_indices():
    lane_src = np.full((HEAD_PAD,), -1, np.int64)
    lane_src[0:ROPE_HALF] = QK_NOPE + np.arange(ROPE_HALF)
    lane_src[ROPE_HALF:X2_LANE] = np.arange(X2_LANE - ROPE_HALF)
    lane_src[X2_LANE:X2_LANE + ROPE_HALF] = QK_NOPE + ROPE_HALF + np.arange(ROPE_HALF)
    n_lo = X2_LANE - ROPE_HALF
    lane_src[X2_LANE + ROPE_HALF:X2_LANE + ROPE_HALF + QK_NOPE - n_lo] = n_lo + np.arange(QK_NOPE - n_lo)
    q_idx = np.concatenate([np.where(lane_src >= 0, lane_src + (QK_NOPE + QK_ROPE) * h, -1)
                            for h in range(N_HEADS)])
    nope_src = np.where(lane_src < QK_NOPE, lane_src, -1)
    k_idx = np.concatenate([np.where(nope_src >= 0, nope_src + (QK_NOPE + V_DIM) * h, -1)
                            for h in range(N_HEADS)])
    v_idx = np.concatenate([QK_NOPE + np.arange(V_DIM) + (QK_NOPE + V_DIM) * h for h in range(N_HEADS)])
    kr_idx = np.full((LANES,), -1, np.int64)
    kr_idx[0:ROPE_HALF] = np.arange(ROPE_HALF)
    kr_idx[X2_LANE:X2_LANE + ROPE_HALF] = ROPE_HALF + np.arange(ROPE_HALF)
    return q_idx, k_idx, v_idx, kr_idx


def _rope_lane_tables(length):
    inv_freq = np.float32(ROPE_THETA) ** (-np.arange(ROPE_HALF, dtype=np.float32) / np.float32(ROPE_HALF))
    ang = np.arange(length, dtype=np.float32)[:, None] * inv_freq[None, :].astype(np.float32)
    cos, sin = np.cos(ang).astype(np.float32), np.sin(ang).astype(np.float32)
    ctab = np.ones((length, LANES), np.float32)
    stab = np.zeros((length, LANES), np.float32)
    for lo, sign in ((0, -1.0), (X2_LANE, 1.0)):
        ctab[:, lo:lo + ROPE_HALF] = cos
        stab[:, lo:lo + ROPE_HALF] = sign * sin
    return jnp.asarray(ctab), jnp.asarray(stab)


def kernel(x, meta_tokens, ln_in_g, ln_in_b, w_in, conv_w, conv_b, conv_ln_g, conv_ln_b, q_norm_g, w_uq,
           kv_norm_g, w_ukv, conv_out_g, attn_out_g, w_o, ln1_g, ln1_b, w_rg, b_rg, w_re, b_re,
           w_gate, w_up, w_down, ln2_g, ln2_b):
    assert w_in.shape[0] == 1, "single layer"
    bsz, seq, d = x.shape
    t = bsz * seq
    row = lambda a: a.reshape(1, -1)

    q_idx, k_idx, v_idx, kr_idx = _layout_indices()
    c4 = 2 * C_CONV + Q_LORA + KV_LORA
    w_in_l = jnp.concatenate([w_in[0][:, :c4], _take_cols(w_in[0][:, c4:], kr_idx)], axis=1).astype(BF16)
    wuq_l = _take_cols(w_uq[0], q_idx).astype(BF16)
    wuk_l = _take_cols(w_ukv[0], k_idx).astype(BF16)
    wvt_l = w_ukv[0][:, v_idx].T.astype(BF16)
    wo_l = w_o[0].astype(BF16)
    wgu_l = jnp.concatenate([w_gate[0], w_up[0]], axis=2).astype(BF16)
    wd_l = w_down[0].astype(BF16)
    wr_l = jnp.concatenate([w_re[0], w_rg[0], jnp.zeros((d, LANES - N_EXPERTS - N_GROUPS), F32)], axis=1)
    br_l = jnp.concatenate([b_re[0], b_rg[0], jnp.zeros((LANES - N_EXPERTS - N_GROUPS,), F32)]).reshape(1, LANES)
    ctab, stab = _rope_lane_tables(N_META + seq)

    proj_w = (row(ln_in_g), row(ln_in_b), w_in_l, row(q_norm_g[0]), wuq_l, row(kv_norm_g[0]), wuk_l, wvt_l)
    u_m, _, k_m, vt_m = _proj_call(meta_tokens.astype(F32), N_META, 1, *proj_w, ctab[:N_META], stab[:N_META])
    u, q, k, vt = _proj_call(x.reshape(t, d), PROJ_TM, seq // PROJ_TM, *proj_w, ctab[N_META:], stab[N_META:])

    o = _attn_call(q.reshape(bsz, seq, -1), k.reshape(bsz, seq, -1),
                   vt.reshape(bsz, seq // ATT_T, D_ATTN, ATT_T), k_m, vt_m)

    pre = jnp.concatenate([jnp.zeros((HALO - N_META, C_CONV), F32), u_m], axis=0)
    cw = jnp.concatenate([conv_w[0], jnp.zeros((HALO - CONV_K, C_CONV), F32)], axis=0)
    h1, ri, rw, cnt = _mix_call(x, row(ln_in_g), row(ln_in_b), u.reshape(bsz, seq, C_CONV), pre, cw,
                                row(conv_b[0]), row(conv_ln_g[0]), row(conv_ln_b[0]), row(conv_out_g[0]),
                                o, row(attn_out_g[0]), wo_l, row(ln1_g[0]), row(ln1_b[0]), wr_l, br_l)

    counts = cnt[0, :N_EXPERTS].astype(jnp.int32)
    padded = (counts + MOE_BM - 1) // MOE_BM * MOE_BM
    pad_end = jnp.cumsum(padded)
    pad_start = pad_end - padded
    n_blocks = (2 * t) // MOE_BM + N_EXPERTS
    dest = jnp.stack([pad_start[ri[0]] + ri[2], pad_start[ri[1]] + ri[3]], axis=1).reshape(2 * t)
    blk_start = jnp.arange(n_blocks, dtype=jnp.int32) * MOE_BM
    blk_e = jnp.minimum(jnp.sum((pad_end[None, :] <= blk_start[:, None]).astype(jnp.int32), axis=1),
                        N_EXPERTS - 1)
    blk_valid = jnp.clip(pad_start[blk_e] + counts[blk_e] - blk_start, 0, MOE_BM).astype(jnp.int32)

    h1f = h1.reshape(t, d)
    xb = _dispatch_call(blk_valid, dest, h1f)
    yb = _expert_call(blk_e, blk_valid, xb, wgu_l, wd_l)
    out = _combine_call(dest, h1f, rw.reshape(t, 8), row(ln2_g[0]), row(ln2_b[0]), yb)
    return out.reshape(bsz, seq, d)
```

```python
import math

import numpy as np
import jax
import jax.numpy as jnp
from jax import lax
from jax.experimental import pallas as pl
from jax.experimental.pallas import tpu as pltpu

F32 = jnp.float32
BF16 = jnp.bfloat16

D_MODEL = 1024
HALF_D = D_MODEL // 2
N_META = 16
CHUNK = 64
C_CONV = 512
CONV_K = 31
N_HEADS = 8
QK_NOPE = 64
QK_ROPE = 32
V_DIM = 64
Q_LORA = 384
KV_LORA = 256
ROPE_THETA = 10000.0
D_ATTN = N_HEADS * V_DIM
N_GROUPS = 4
EXPERTS_PER_GROUP = 8
N_EXPERTS = N_GROUPS * EXPERTS_PER_GROUP
D_EXPERT = 256
DEEPNORM_ALPHA = 2.0 ** 0.25
EPS = 1e-5
NEG_INF = -1e30

LANES = 128
SUBLANES = 8
HEAD_PAD = LANES
ROPE_HALF = QK_ROPE // 2
X2_LANE = 64
Z_COLS = 2 * C_CONV + Q_LORA + KV_LORA + LANES
Q_SCALE = (QK_NOPE + QK_ROPE) ** -0.5 * math.log2(math.e)

PROJ_TM = 512
ATT_T = 256
ATT_HEADS = 8
MIX_TM = 256
HALO = 32
MOE_BM = 256
DISP_TM = 512
COMB_TM = 512
VMEM_LIMIT = 56 * 1024 * 1024


def _ln(x, g, b):
    mu = jnp.mean(x, axis=-1, keepdims=True)
    xc = x - mu
    var = jnp.mean(xc * xc, axis=-1, keepdims=True)
    return xc * lax.rsqrt(var + EPS) * g + b


def _rms(x, g):
    return x * lax.rsqrt(jnp.mean(x * x, axis=-1, keepdims=True) + EPS) * g


def _sigmoid(x):
    return 1.0 / (1.0 + jnp.exp(-x))


def _pack_bf16_pair(lo, hi):
    lo_bits = lax.bitcast_convert_type(lo.astype(BF16).astype(F32), jnp.uint32)
    hi_bits = lax.bitcast_convert_type(hi.astype(BF16).astype(F32), jnp.uint32)
    return (lo_bits >> 16) | (hi_bits & jnp.uint32(0xFFFF0000))


def _unpack_bf16_pair(packed):
    lo = lax.bitcast_convert_type(packed << 16, F32)
    hi = lax.bitcast_convert_type(packed & jnp.uint32(0xFFFF0000), F32)
    return lo, hi


def _proj_kernel(x_ref, g_ref, b_ref, win_ref, qg_ref, wuq_ref, kvg_ref, wuk_ref, wvt_ref, c_ref, s_ref,
                 u_ref, q_ref, k_ref, vt_ref):
    h = _ln(x_ref[...], g_ref[...], b_ref[...])
    z = jnp.dot(h.astype(BF16), win_ref[...], preferred_element_type=F32)
    u_ref[...] = z[:, :C_CONV] * _sigmoid(z[:, C_CONV:2 * C_CONV])
    c0 = 2 * C_CONV
    cq = z[:, c0:c0 + Q_LORA]
    ckv = z[:, c0 + Q_LORA:c0 + Q_LORA + KV_LORA]
    kr = z[:, c0 + Q_LORA + KV_LORA:]
    cs = c_ref[...]
    sn = s_ref[...]
    q = jnp.dot(_rms(cq, qg_ref[...]).astype(BF16), wuq_ref[...], preferred_element_type=F32)
    kvn = _rms(ckv, kvg_ref[...]).astype(BF16)
    kn = jnp.dot(kvn, wuk_ref[...], preferred_element_type=F32)
    kr_rot = kr * cs + pltpu.roll(kr, X2_LANE, 1) * sn
    for hh in range(N_HEADS):
        sl = slice(HEAD_PAD * hh, HEAD_PAD * (hh + 1))
        qh = q[:, sl]
        qh = (qh * cs + pltpu.roll(qh, X2_LANE, 1) * sn) * Q_SCALE
        q_ref[:, sl] = qh.astype(BF16)
        k_ref[:, sl] = (kn[:, sl] + kr_rot).astype(BF16)
    vt = lax.dot_general(wvt_ref[...], kvn, (((1,), (1,)), ((), ())), preferred_element_type=F32)
    n_sub, sub_w = vt_ref.shape[1], vt_ref.shape[3]
    for tt in range(n_sub):
        vt_ref[0, tt] = vt[:, tt * sub_w:(tt + 1) * sub_w].astype(BF16)


def _proj_call(x2d, tm, pos_blocks, ln_g, ln_b, w_in, qg, wuq, kvg, wuk, wvt, ctab, stab):
    t = x2d.shape[0]
    sub_w = min(tm, ATT_T)
    n_sub = tm // sub_w
    full = lambda shape: pl.BlockSpec(shape, lambda i: (0,) * len(shape))
    row = lambda w: pl.BlockSpec((tm, w), lambda i: (i, 0))
    tab = pl.BlockSpec((tm, LANES), lambda i: (i % pos_blocks, 0))
    return pl.pallas_call(
        _proj_kernel,
        grid=(t // tm,),
        in_specs=[row(D_MODEL), full((1, D_MODEL)), full((1, D_MODEL)), full((D_MODEL, Z_COLS)),
                  full((1, Q_LORA)), full((Q_LORA, N_HEADS * HEAD_PAD)),
                  full((1, KV_LORA)), full((KV_LORA, N_HEADS * HEAD_PAD)), full((D_ATTN, KV_LORA)), tab, tab],
        out_specs=[row(C_CONV), row(N_HEADS * HEAD_PAD), row(N_HEADS * HEAD_PAD),
                   pl.BlockSpec((1, n_sub, D_ATTN, sub_w), lambda i: (i, 0, 0, 0))],
        out_shape=[jax.ShapeDtypeStruct((t, C_CONV), F32),
                   jax.ShapeDtypeStruct((t, N_HEADS * HEAD_PAD), BF16),
                   jax.ShapeDtypeStruct((t, N_HEADS * HEAD_PAD), BF16),
                   jax.ShapeDtypeStruct((t // tm, n_sub, D_ATTN, sub_w), BF16)],
        compiler_params=pltpu.CompilerParams(dimension_semantics=("arbitrary",),
                                             vmem_limit_bytes=VMEM_LIMIT),
        name="proj",
    )(x2d, ln_g, ln_b, w_in, qg, wuq, kvg, wuk, wvt, ctab, stab)


def _attn_kernel(q_ref, k_ref, vt_ref, km_ref, vtm_ref, o_ref, s_sc):
    seq = q_ref.shape[1]
    nq = seq // ATT_T
    key_chunk = lax.broadcasted_iota(jnp.int32, (ATT_T, ATT_T), 0) // CHUNK
    qry_chunk = lax.broadcasted_iota(jnp.int32, (ATT_T, ATT_T), 1) // CHUNK
    diag_visible = key_chunk <= qry_chunk
    nt = (((1,), (1,)), ((), ()))
    heads = range(ATT_HEADS)
    hsl = [slice(HEAD_PAD * hh, HEAD_PAD * (hh + 1)) for hh in heads]
    vsl = [slice(V_DIM * hh, V_DIM * (hh + 1)) for hh in heads]

    def update(s, vt, m, l, acc):
        m_new = jnp.maximum(m, jnp.max(s, axis=0, keepdims=True))
        alpha = jnp.exp2(m - m_new)
        p = jnp.exp2(s - m_new)
        l = alpha * l + jnp.sum(p, axis=0, keepdims=True)
        acc = alpha * acc + jnp.dot(vt, p.astype(BF16), preferred_element_type=F32)
        return m_new, l, acc

    def q_body(qi, carry):
        row0 = pl.multiple_of(qi * ATT_T, ATT_T)
        qs = [q_ref[0, pl.ds(row0, ATT_T), hsl[hh]] for hh in heads]
        state = []
        for hh in heads:
            s = lax.dot_general(km_ref[:, hsl[hh]], qs[hh], nt, preferred_element_type=F32)
            m = jnp.max(s, axis=0, keepdims=True)
            p = jnp.exp2(s - m)
            l = jnp.sum(p, axis=0, keepdims=True)
            acc = jnp.dot(vtm_ref[0, 0, vsl[hh], :], p.astype(BF16), preferred_element_type=F32)
            state += [m, l, acc]

        def scores(j, hh):
            k0 = pl.multiple_of(j * ATT_T, ATT_T)
            return lax.dot_general(k_ref[0, pl.ds(k0, ATT_T), hsl[hh]], qs[hh], nt, preferred_element_type=F32)

        for hh in heads:
            s_sc[hh] = scores(0, hh)

        def kv_body(j, st):
            out = []
            for hh in heads:
                s_cur = s_sc[hh]
                s_next = scores(j + 1, hh)
                out += update(s_cur, vt_ref[0, j, vsl[hh], :], *st[3 * hh:3 * hh + 3])
                s_sc[hh] = s_next
            return tuple(out)

        state = lax.fori_loop(0, qi, kv_body, tuple(state))
        outs = []
        for hh in heads:
            s = jnp.where(diag_visible, s_sc[hh], NEG_INF)
            m, l, acc = update(s, vt_ref[0, qi, vsl[hh], :], *state[3 * hh:3 * hh + 3])
            outs.append(acc / l)
        o_ref[0, pl.ds(row0, ATT_T), :] = jnp.concatenate(outs, axis=0).T
        return carry

    lax.fori_loop(0, nq, q_body, 0)


def _attn_call(q, k, vt, k_meta, vt_meta):
    b, seq, _ = q.shape
    nk = seq // ATT_T
    hb = ATT_HEADS
    return pl.pallas_call(
        _attn_kernel,
        grid=(b, N_HEADS // hb),
        in_specs=[pl.BlockSpec((1, seq, hb * HEAD_PAD), lambda i, j: (i, 0, j)),
                  pl.BlockSpec((1, seq, hb * HEAD_PAD), lambda i, j: (i, 0, j)),
                  pl.BlockSpec((1, nk, hb * V_DIM, ATT_T), lambda i, j: (i, 0, j, 0)),
                  pl.BlockSpec((N_META, hb * HEAD_PAD), lambda i, j: (0, j)),
                  pl.BlockSpec((1, 1, hb * V_DIM, N_META), lambda i, j: (0, 0, j, 0))],
        out_specs=pl.BlockSpec((1, seq, hb * V_DIM), lambda i, j: (i, 0, j)),
        out_shape=jax.ShapeDtypeStruct((b, seq, D_ATTN), F32),
        scratch_shapes=[pltpu.VMEM((hb, ATT_T, ATT_T), F32)],
        compiler_params=pltpu.CompilerParams(dimension_semantics=("arbitrary", "arbitrary"),
                                             vmem_limit_bytes=VMEM_LIMIT),
        name="attn",
    )(q, k, vt, k_meta, vt_meta)


def _mix_kernel(x_ref, ig_ref, ib_ref, u_ref, uh_ref, pre_ref, cw_ref, cb_ref, clg_ref, clb_ref,
                cog_ref, o_ref, aog_ref, wo_ref, l1g_ref, l1b_ref, wr_ref, br_ref,
                h1_ref, hp_ref, ri_ref, rw_ref, cnt_ref, win_sc, sh_sc, carry_sc):
    bi = pl.program_id(0)
    ti = pl.program_id(1)
    tm = MIX_TM

    @pl.when((bi == 0) & (ti == 0))
    def _():
        carry_sc[...] = jnp.zeros_like(carry_sc)

    win_sc[0:HALO, :] = jnp.where(ti == 0, pre_ref[...], uh_ref[0])
    win_sc[HALO:, :] = u_ref[0]
    first = HALO - (CONV_K - 1)
    acc = jnp.broadcast_to(cb_ref[...], (tm, C_CONV))
    for res in range(SUBLANES):
        offs = [o for o in range(first, first + CONV_K) if o % SUBLANES == res]
        rows = tm + offs[-1] - res
        if res:
            sh_sc[res - 1, 0:rows, :] = win_sc[pl.ds(res, rows), :]
        for o in offs:
            src = sh_sc[res - 1, o - res:o - res + tm, :] if res else win_sc[o:o + tm, :]
            acc = acc + src * cw_ref[o - first:o - first + 1, :]
    c = _ln(acc, clg_ref[...], clb_ref[...])
    c = c * _sigmoid(c)
    cn = _rms(c, cog_ref[...])
    on = _rms(o_ref[0], aog_ref[...])
    y = (jnp.dot(cn.astype(BF16), wo_ref[0:C_CONV, :], preferred_element_type=F32)
         + jnp.dot(on.astype(BF16), wo_ref[C_CONV:, :], preferred_element_type=F32))
    h = _ln(x_ref[0], ig_ref[...], ib_ref[...])
    h1 = _ln(DEEPNORM_ALPHA * h + y, l1g_ref[...], l1b_ref[...])
    h1_ref[0] = h1
    hp_ref[0] = _pack_bf16_pair(h1[:, :HALF_D], h1[:, HALF_D:])

    logits = jnp.dot(h1, wr_ref[...], preferred_element_type=F32,
                     precision=lax.Precision.HIGHEST) + br_ref[...]
    lane = lax.broadcasted_iota(jnp.int32, (tm, LANES), 1)
    big = jnp.int32(1 << 20)
    is_g = (lane >= N_EXPERTS) & (lane < N_EXPERTS + N_GROUPS)
    gl = jnp.where(is_g, logits, -jnp.inf)
    gmax = jnp.max(gl, axis=1, keepdims=True)
    g_idx = jnp.min(jnp.where(gl == gmax, lane - N_EXPERTS, big), axis=1, keepdims=True)
    g_w = 1.0 / jnp.sum(jnp.exp(gl - gmax), axis=1, keepdims=True)
    in_grp = (lane < N_EXPERTS) & ((lane // EXPERTS_PER_GROUP) == g_idx)
    el = jnp.where(in_grp, logits, -jnp.inf)
    v1 = jnp.max(el, axis=1, keepdims=True)
    i1 = jnp.min(jnp.where(el == v1, lane, big), axis=1, keepdims=True)
    el2 = jnp.where(lane == i1, -jnp.inf, el)
    v2 = jnp.max(el2, axis=1, keepdims=True)
    i2 = jnp.min(jnp.where(el2 == v2, lane, big), axis=1, keepdims=True)
    e2 = jnp.exp(v2 - v1)
    w1 = g_w / (1.0 + e2)
    w2 = g_w * e2 / (1.0 + e2)

    oh1 = lane == i1
    oh2 = lane == i2
    oh = jnp.where(oh1 | oh2, 1.0, 0.0)
    rt = lax.broadcasted_iota(jnp.int32, (tm, tm), 0)
    ct = lax.broadcasted_iota(jnp.int32, (tm, tm), 1)
    ltri = jnp.where(ct < rt, 1.0, 0.0).astype(BF16)
    before = jnp.dot(ltri, oh.astype(BF16), preferred_element_type=F32) + carry_sc[...]
    r1 = jnp.sum(jnp.where(oh1, before, 0.0), axis=1, keepdims=True).astype(jnp.int32)
    r2 = jnp.sum(jnp.where(oh2, before, 0.0), axis=1, keepdims=True).astype(jnp.int32)
    carry_sc[...] = carry_sc[...] + jnp.sum(oh, axis=0, keepdims=True)
    cnt_ref[...] = carry_sc[...]

    ri = jnp.where(lane == 0, i1, jnp.where(lane == 1, i2, jnp.where(lane == 2, r1, r2)))
    rw = jnp.where(lane == 0, w1, w2)
    ri_ref[...] = ri.T[:8, :]
    rw_ref[0] = rw[:, :8]


def _mix_call(x, ig, ib, u, pre, cw, cb, clg, clb, cog, o, aog, wo, l1g, l1b, wr, br):
    b, seq, _ = x.shape
    tm = MIX_TM
    nt = seq // tm
    full = lambda shape: pl.BlockSpec(shape, lambda i, j: (0,) * len(shape))
    tile = lambda w: pl.BlockSpec((1, tm, w), lambda i, j: (i, j, 0))
    halo = pl.BlockSpec((1, HALO, C_CONV), lambda i, j: (i, jnp.maximum(j * (tm // HALO) - 1, 0), 0))
    return pl.pallas_call(
        _mix_kernel,
        grid=(b, nt),
        in_specs=[tile(D_MODEL), full((1, D_MODEL)), full((1, D_MODEL)),
                  tile(C_CONV), halo, full((HALO, C_CONV)), full((HALO, C_CONV)), full((1, C_CONV)),
                  full((1, C_CONV)), full((1, C_CONV)), full((1, C_CONV)),
                  tile(D_ATTN), full((1, D_ATTN)), full((D_MODEL, D_MODEL)),
                  full((1, D_MODEL)), full((1, D_MODEL)), full((D_MODEL, LANES)), full((1, LANES))],
        out_specs=[tile(D_MODEL), tile(HALF_D), pl.BlockSpec((8, tm), lambda i, j: (0, i * nt + j)), tile(8),
                   full((1, LANES))],
        out_shape=[jax.ShapeDtypeStruct((b, seq, D_MODEL), F32),
                   jax.ShapeDtypeStruct((b, seq, HALF_D), jnp.uint32),
                   jax.ShapeDtypeStruct((8, b * seq), jnp.int32),
                   jax.ShapeDtypeStruct((b, seq, 8), F32),
                   jax.ShapeDtypeStruct((1, LANES), F32)],
        scratch_shapes=[pltpu.VMEM((HALO + tm, C_CONV), F32),
                        pltpu.VMEM((SUBLANES - 1, HALO + tm, C_CONV), F32),
                        pltpu.VMEM((1, LANES), F32)],
        compiler_params=pltpu.CompilerParams(dimension_semantics=("arbitrary", "arbitrary"),
                                             vmem_limit_bytes=VMEM_LIMIT),
        name="mix",
    )(x, ig, ib, u, u, pre, cw, cb, clg, clb, cog, o, aog, wo, l1g, l1b, wr, br)


def _dispatch_kernel(nv_ref, dest_ref, h_ref, xb_ref, zero_sc, sem):
    tm = DISP_TM
    n_blocks = nv_ref.shape[0]

    @pl.when(pl.program_id(0) == 0)
    def _():
        zero_sc[...] = jnp.zeros_like(zero_sc)

        def zero_copy(i):
            return pltpu.make_async_copy(zero_sc, xb_ref.at[pl.ds(i * MOE_BM, MOE_BM)], sem)

        def start(i, carry):
            @pl.when(nv_ref[i] < MOE_BM)
            def _():
                zero_copy(i).start()
            return carry

        def wait(i, carry):
            @pl.when(nv_ref[i] < MOE_BM)
            def _():
                zero_copy(i).wait()
            return carry

        lax.fori_loop(0, n_blocks, start, 0)
        lax.fori_loop(0, n_blocks, wait, 0)

    def issue(t, carry):
        src = h_ref.at[pl.ds(t, 1)]
        pltpu.make_async_copy(src, xb_ref.at[pl.ds(dest_ref[2 * t], 1)], sem).start(priority=0)
        pltpu.make_async_copy(src, xb_ref.at[pl.ds(dest_ref[2 * t + 1], 1)], sem).start(priority=1)
        return carry

    lax.fori_loop(0, tm, issue, 0, unroll=8)
    for _ in range(2):
        pltpu.make_async_copy(h_ref, xb_ref.at[pl.ds(0, tm)], sem).wait()


def _dispatch_call(blk_valid, dest, h1):
    t, w = h1.shape
    tm = DISP_TM
    return pl.pallas_call(
        _dispatch_kernel,
        grid_spec=pltpu.PrefetchScalarGridSpec(
            num_scalar_prefetch=1,
            grid=(t // tm,),
            in_specs=[pl.BlockSpec((2 * tm,), lambda i, nv: (i,), memory_space=pltpu.SMEM),
                      pl.BlockSpec((tm, w), lambda i, nv: (i, 0))],
            out_specs=pl.BlockSpec(memory_space=pl.ANY),
            scratch_shapes=[pltpu.VMEM((MOE_BM, w), h1.dtype), pltpu.SemaphoreType.DMA]),
        out_shape=jax.ShapeDtypeStruct((blk_valid.shape[0] * MOE_BM, w), h1.dtype),
        compiler_params=pltpu.CompilerParams(dimension_semantics=("arbitrary",),
                                             vmem_limit_bytes=VMEM_LIMIT),
        name="dispatch",
    )(blk_valid, dest, h1)


def _expert_kernel(be_ref, nv_ref, x_ref, wgu_ref, wd_ref, y_ref):
    nv = nv_ref[pl.program_id(0)]

    @pl.when(nv > 0)
    def _():
        x_lo, x_hi = _unpack_bf16_pair(x_ref[...])
        gu = (jnp.dot(x_lo.astype(BF16), wgu_ref[0, :HALF_D, :], preferred_element_type=F32)
              + jnp.dot(x_hi.astype(BF16), wgu_ref[0, HALF_D:, :], preferred_element_type=F32))
        g = gu[:, :D_EXPERT]
        act = g * _sigmoid(g) * gu[:, D_EXPERT:]
        y = jnp.dot(act.astype(BF16), wd_ref[0], preferred_element_type=F32)
        y_ref[...] = _pack_bf16_pair(y[:, :HALF_D], y[:, HALF_D:])

    @pl.when(nv == 0)
    def _():
        y_ref[...] = jnp.zeros_like(y_ref)


def _expert_call(blk_e, n_used, xb, wgu, wd):
    r = xb.shape[0]
    bm = MOE_BM
    return pl.pallas_call(
        _expert_kernel,
        grid_spec=pltpu.PrefetchScalarGridSpec(
            num_scalar_prefetch=2,
            grid=(r // bm,),
            in_specs=[pl.BlockSpec((bm, HALF_D), lambda i, be, nu: (i, 0)),
                      pl.BlockSpec((1, D_MODEL, 2 * D_EXPERT), lambda i, be, nu: (be[i], 0, 0)),
                      pl.BlockSpec((1, D_EXPERT, D_MODEL), lambda i, be, nu: (be[i], 0, 0))],
            out_specs=pl.BlockSpec((bm, HALF_D), lambda i, be, nu: (i, 0))),
        out_shape=jax.ShapeDtypeStruct((r, HALF_D), jnp.uint32),
        compiler_params=pltpu.CompilerParams(dimension_semantics=("arbitrary",),
                                             vmem_limit_bytes=VMEM_LIMIT),
        name="experts",
    )(blk_e, n_used, xb, wgu, wd)


def _combine_kernel(dest_ref, h_ref, w_ref, g_ref, b_ref, yb_ref, o_ref, y0_sc, y1_sc, sem):
    tm = COMB_TM

    def issue(t, carry):
        pltpu.make_async_copy(yb_ref.at[pl.ds(dest_ref[2 * t], 1)], y0_sc.at[pl.ds(t, 1)],
                              sem).start(priority=0)
        pltpu.make_async_copy(yb_ref.at[pl.ds(dest_ref[2 * t + 1], 1)], y1_sc.at[pl.ds(t, 1)],
                              sem).start(priority=1)
        return carry

    lax.fori_loop(0, tm, issue, 0, unroll=8)
    pltpu.make_async_copy(yb_ref.at[pl.ds(0, tm)], y0_sc, sem).wait()
    pltpu.make_async_copy(yb_ref.at[pl.ds(0, tm)], y1_sc, sem).wait()
    w = w_ref[...]
    y0_lo, y0_hi = _unpack_bf16_pair(y0_sc[...])
    y1_lo, y1_hi = _unpack_bf16_pair(y1_sc[...])
    y = jnp.concatenate([w[:, 0:1] * y0_lo + w[:, 1:2] * y1_lo, w[:, 0:1] * y0_hi + w[:, 1:2] * y1_hi], axis=1)
    o_ref[...] = _ln(DEEPNORM_ALPHA * h_ref[...] + y, g_ref[...], b_ref[...])


def _combine_call(dest, h1, rw, g, b, yb):
    t = h1.shape[0]
    tm = COMB_TM
    return pl.pallas_call(
        _combine_kernel,
        grid=(t // tm,),
        in_specs=[pl.BlockSpec((2 * tm,), lambda i: (i,), memory_space=pltpu.SMEM),
                  pl.BlockSpec((tm, D_MODEL), lambda i: (i, 0)),
                  pl.BlockSpec((tm, 8), lambda i: (i, 0)),
                  pl.BlockSpec((1, D_MODEL), lambda i: (0, 0)),
                  pl.BlockSpec((1, D_MODEL), lambda i: (0, 0)),
                  pl.BlockSpec(memory_space=pl.ANY)],
        out_specs=pl.BlockSpec((tm, D_MODEL), lambda i: (i, 0)),
        out_shape=jax.ShapeDtypeStruct((t, D_MODEL), F32),
        scratch_shapes=[pltpu.VMEM((tm, HALF_D), jnp.uint32), pltpu.VMEM((tm, HALF_D), jnp.uint32),
                        pltpu.SemaphoreType.DMA],
        compiler_params=pltpu.CompilerParams(dimension_semantics=("arbitrary",),
                                             vmem_limit_bytes=VMEM_LIMIT),
        name="combine",
    )(dest, h1, rw, g, b, yb)


def _take_cols(w, idx):
    wz = jnp.concatenate([w, jnp.zeros((w.shape[0], 1), w.dtype)], axis=1)
    return wz[:, np.where(idx < 0, w.shape[1], idx)]


def _layout_indices():
    lane_src = np.full((HEAD_PAD,), -1, np.int64)
    lane_src[0:ROPE_HALF] = QK_NOPE + np.arange(ROPE_HALF)
    lane_src[ROPE_HALF:X2_LANE] = np.arange(X2_LANE - ROPE_HALF)
    lane_src[X2_LANE:X2_LANE + ROPE_HALF] = QK_NOPE + ROPE_HALF + np.arange(ROPE_HALF)
    n_lo = X2_LANE - ROPE_HALF
    lane_src[X2_LANE + ROPE_HALF:X2_LANE + ROPE_HALF + QK_NOPE - n_lo] = n_lo + np.arange(QK_NOPE - n_lo)
    q_idx = np.concatenate([np.where(lane_src >= 0, lane_src + (QK_NOPE + QK_ROPE) * h, -1)
                            for h in range(N_HEADS)])
    nope_src = np.where(lane_src < QK_NOPE, lane_src, -1)
    k_idx = np.concatenate([np.where(nope_src >= 0, nope_src + (QK_NOPE + V_DIM) * h, -1)
                            for h in range(N_HEADS)])
    v_idx = np.concatenate([QK_NOPE + np.arange(V_DIM) + (QK_NOPE + V_DIM) * h for h in range(N_HEADS)])
    kr_idx = np.full((LANES,), -1, np.int64)
    kr_idx[0:ROPE_HALF] = np.arange(ROPE_HALF)
    kr_idx[X2_LANE:X2_LANE + ROPE_HALF] = ROPE_HALF + np.arange(ROPE_HALF)
    return q_idx, k_idx, v_idx, kr_idx


def _rope_lane_tables(length):
    inv_freq = np.float32(ROPE_THETA) ** (-np.arange(ROPE_HALF, dtype=np.float32) / np.float32(ROPE_HALF))
    ang = np.arange(length, dtype=np.float32)[:, None] * inv_freq[None, :].astype(np.float32)
    cos, sin = np.cos(ang).astype(np.float32), np.sin(ang).astype(np.float32)
    ctab = np.ones((length, LANES), np.float32)
    stab = np.zeros((length, LANES), np.float32)
    for lo, sign in ((0, -1.0), (X2_LANE, 1.0)):
        ctab[:, lo:lo + ROPE_HALF] = cos
        stab[:, lo:lo + ROPE_HALF] = sign * sin
    return jnp.asarray(ctab), jnp.asarray(stab)


def kernel(x, meta_tokens, ln_in_g, ln_in_b, w_in, conv_w, conv_b, conv_ln_g, conv_ln_b, q_norm_g, w_uq,
           kv_norm_g, w_ukv, conv_out_g, attn_out_g, w_o, ln1_g, ln1_b, w_rg, b_rg, w_re, b_re,
           w_gate, w_up, w_down, ln2_g, ln2_b):
    assert w_in.shape[0] == 1, "single layer"
    bsz, seq, d = x.shape
    t = bsz * seq
    row = lambda a: a.reshape(1, -1)

    q_idx, k_idx, v_idx, kr_idx = _layout_indices()
    c4 = 2 * C_CONV + Q_LORA + KV_LORA
    w_in_l = jnp.concatenate([w_in[0][:, :c4], _take_cols(w_in[0][:, c4:], kr_idx)], axis=1).astype(BF16)
    wuq_l = _take_cols(w_uq[0], q_idx).astype(BF16)
    wuk_l = _take_cols(w_ukv[0], k_idx).astype(BF16)
    wvt_l = w_ukv[0][:, v_idx].T.astype(BF16)
    wo_l = w_o[0].astype(BF16)
    wgu_l = jnp.concatenate([w_gate[0], w_up[0]], axis=2).astype(BF16)
    wd_l = w_down[0].astype(BF16)
    wr_l = jnp.concatenate([w_re[0], w_rg[0], jnp.zeros((d, LANES - N_EXPERTS - N_GROUPS), F32)], axis=1)
    br_l = jnp.concatenate([b_re[0], b_rg[0], jnp.zeros((LANES - N_EXPERTS - N_GROUPS,), F32)]).reshape(1, LANES)
    ctab, stab = _rope_lane_tables(N_META + seq)

    proj_w = (row(ln_in_g), row(ln_in_b), w_in_l, row(q_norm_g[0]), wuq_l, row(kv_norm_g[0]), wuk_l, wvt_l)
    u_m, _, k_m, vt_m = _proj_call(meta_tokens.astype(F32), N_META, 1, *proj_w, ctab[:N_META], stab[:N_META])
    u, q, k, vt = _proj_call(x.reshape(t, d), PROJ_TM, seq // PROJ_TM, *proj_w, ctab[N_META:], stab[N_META:])

    o = _attn_call(q.reshape(bsz, seq, -1), k.reshape(bsz, seq, -1),
                   vt.reshape(bsz, seq // ATT_T, D_ATTN, ATT_T), k_m, vt_m)

    pre = jnp.concatenate([jnp.zeros((HALO - N_META, C_CONV), F32), u_m], axis=0)
    cw = jnp.concatenate([conv_w[0], jnp.zeros((HALO - CONV_K, C_CONV), F32)], axis=0)
    h1, hp, ri, rw, cnt = _mix_call(x, row(ln_in_g), row(ln_in_b), u.reshape(bsz, seq, C_CONV), pre, cw,
                                row(conv_b[0]), row(conv_ln_g[0]), row(conv_ln_b[0]), row(conv_out_g[0]),
                                o, row(attn_out_g[0]), wo_l, row(ln1_g[0]), row(ln1_b[0]), wr_l, br_l)

    counts = cnt[0, :N_EXPERTS].astype(jnp.int32)
    padded = (counts + MOE_BM - 1) // MOE_BM * MOE_BM
    pad_end = jnp.cumsum(padded)
    pad_start = pad_end - padded
    n_blocks = (2 * t) // MOE_BM + N_EXPERTS
    dest = jnp.stack([pad_start[ri[0]] + ri[2], pad_start[ri[1]] + ri[3]], axis=1).reshape(2 * t)
    blk_start = jnp.arange(n_blocks, dtype=jnp.int32) * MOE_BM
    blk_e = jnp.minimum(jnp.sum((pad_end[None, :] <= blk_start[:, None]).astype(jnp.int32), axis=1),
                        N_EXPERTS - 1)
    blk_valid = jnp.clip(pad_start[blk_e] + counts[blk_e] - blk_start, 0, MOE_BM).astype(jnp.int32)

    h1f = h1.reshape(t, d)
    xb = _dispatch_call(blk_valid, dest, hp.reshape(t, HALF_D))
    yb = _expert_call(blk_e, blk_valid, xb, wgu_l, wd_l)
    out = _combine_call(dest, h1f, rw.reshape(t, 8), row(ln2_g[0]), row(ln2_b[0]), yb)
    return out.reshape(bsz, seq, d)
```

```python
import math

import numpy as np
import jax
import jax.numpy as jnp
from jax import lax
from jax.experimental import pallas as pl
from jax.experimental.pallas import tpu as pltpu

F32 = jnp.float32
BF16 = jnp.bfloat16

D_MODEL = 1024
HALF_D = D_MODEL // 2
N_META = 16
CHUNK = 64
C_CONV = 512
CONV_K = 31
N_HEADS = 8
QK_NOPE = 64
QK_ROPE = 32
V_DIM = 64
Q_LORA = 384
KV_LORA = 256
ROPE_THETA = 10000.0
D_ATTN = N_HEADS * V_DIM
N_GROUPS = 4
EXPERTS_PER_GROUP = 8
N_EXPERTS = N_GROUPS * EXPERTS_PER_GROUP
D_EXPERT = 256
DEEPNORM_ALPHA = 2.0 ** 0.25
EPS = 1e-5
NEG_INF = -1e30

LANES = 128
SUBLANES = 8
HEAD_PAD = LANES
ROPE_HALF = QK_ROPE // 2
X2_LANE = 64
Z_COLS = 2 * C_CONV + Q_LORA + KV_LORA + LANES
Q_SCALE = (QK_NOPE + QK_ROPE) ** -0.5 * math.log2(math.e)

PROJ_TM = 512
ATT_T = 256
ATT_HEADS = 8
MIX_TM = 256
HALO = 32
MOE_BM = 256
DISP_TM = 512
COMB_TM = 512
VMEM_LIMIT = 56 * 1024 * 1024


def _ln(x, g, b):
    mu = jnp.mean(x, axis=-1, keepdims=True)
    xc = x - mu
    var = jnp.mean(xc * xc, axis=-1, keepdims=True)
    return xc * lax.rsqrt(var + EPS) * g + b


def _rms(x, g):
    return x * lax.rsqrt(jnp.mean(x * x, axis=-1, keepdims=True) + EPS) * g


def _sigmoid(x):
    return 1.0 / (1.0 + jnp.exp(-x))


def _pack_bf16_pair(lo, hi):
    lo_bits = lax.bitcast_convert_type(lo.astype(BF16).astype(F32), jnp.uint32)
    hi_bits = lax.bitcast_convert_type(hi.astype(BF16).astype(F32), jnp.uint32)
    return (lo_bits >> 16) | (hi_bits & jnp.uint32(0xFFFF0000))


def _unpack_bf16_pair(packed):
    lo = lax.bitcast_convert_type(packed << 16, F32)
    hi = lax.bitcast_convert_type(packed & jnp.uint32(0xFFFF0000), F32)
    return lo, hi


def _proj_kernel(x_ref, g_ref, b_ref, win_ref, qg_ref, wuq_ref, kvg_ref, wuk_ref, wvt_ref, c_ref, s_ref,
                 u_ref, q_ref, k_ref, vt_ref):
    h = _ln(x_ref[...], g_ref[...], b_ref[...])
    z = jnp.dot(h.astype(BF16), win_ref[...], preferred_element_type=F32)
    u_ref[...] = z[:, :C_CONV] * _sigmoid(z[:, C_CONV:2 * C_CONV])
    c0 = 2 * C_CONV
    cq = z[:, c0:c0 + Q_LORA]
    ckv = z[:, c0 + Q_LORA:c0 + Q_LORA + KV_LORA]
    kr = z[:, c0 + Q_LORA + KV_LORA:]
    cs = c_ref[...]
    sn = s_ref[...]
    q = jnp.dot(_rms(cq, qg_ref[...]).astype(BF16), wuq_ref[...], preferred_element_type=F32)
    kvn = _rms(ckv, kvg_ref[...]).astype(BF16)
    kn = jnp.dot(kvn, wuk_ref[...], preferred_element_type=F32)
    kr_rot = kr * cs + pltpu.roll(kr, X2_LANE, 1) * sn
    for hh in range(N_HEADS):
        sl = slice(HEAD_PAD * hh, HEAD_PAD * (hh + 1))
        qh = q[:, sl]
        qh = (qh * cs + pltpu.roll(qh, X2_LANE, 1) * sn) * Q_SCALE
        q_ref[:, sl] = qh.astype(BF16)
        k_ref[:, sl] = (kn[:, sl] + kr_rot).astype(BF16)
    vt = lax.dot_general(wvt_ref[...], kvn, (((1,), (1,)), ((), ())), preferred_element_type=F32)
    n_sub, sub_w = vt_ref.shape[1], vt_ref.shape[3]
    for tt in range(n_sub):
        vt_ref[0, tt] = vt[:, tt * sub_w:(tt + 1) * sub_w].astype(BF16)


def _proj_call(x2d, tm, pos_blocks, ln_g, ln_b, w_in, qg, wuq, kvg, wuk, wvt, ctab, stab):
    t = x2d.shape[0]
    sub_w = min(tm, ATT_T)
    n_sub = tm // sub_w
    full = lambda shape: pl.BlockSpec(shape, lambda i: (0,) * len(shape))
    row = lambda w: pl.BlockSpec((tm, w), lambda i: (i, 0))
    tab = pl.BlockSpec((tm, LANES), lambda i: (i % pos_blocks, 0))
    return pl.pallas_call(
        _proj_kernel,
        grid=(t // tm,),
        in_specs=[row(D_MODEL), full((1, D_MODEL)), full((1, D_MODEL)), full((D_MODEL, Z_COLS)),
                  full((1, Q_LORA)), full((Q_LORA, N_HEADS * HEAD_PAD)),
                  full((1, KV_LORA)), full((KV_LORA, N_HEADS * HEAD_PAD)), full((D_ATTN, KV_LORA)), tab, tab],
        out_specs=[row(C_CONV), row(N_HEADS * HEAD_PAD), row(N_HEADS * HEAD_PAD),
                   pl.BlockSpec((1, n_sub, D_ATTN, sub_w), lambda i: (i, 0, 0, 0))],
        out_shape=[jax.ShapeDtypeStruct((t, C_CONV), F32),
                   jax.ShapeDtypeStruct((t, N_HEADS * HEAD_PAD), BF16),
                   jax.ShapeDtypeStruct((t, N_HEADS * HEAD_PAD), BF16),
                   jax.ShapeDtypeStruct((t // tm, n_sub, D_ATTN, sub_w), BF16)],
        compiler_params=pltpu.CompilerParams(dimension_semantics=("arbitrary",),
                                             vmem_limit_bytes=VMEM_LIMIT),
        name="proj",
    )(x2d, ln_g, ln_b, w_in, qg, wuq, kvg, wuk, wvt, ctab, stab)


def _attn_kernel(q_ref, k_ref, vt_ref, km_ref, vtm_ref, o_ref, s_sc):
    seq = q_ref.shape[1]
    nq = seq // ATT_T
    key_chunk = lax.broadcasted_iota(jnp.int32, (ATT_T, ATT_T), 0) // CHUNK
    qry_chunk = lax.broadcasted_iota(jnp.int32, (ATT_T, ATT_T), 1) // CHUNK
    diag_visible = key_chunk <= qry_chunk
    nt = (((1,), (1,)), ((), ()))
    heads = range(ATT_HEADS)
    hsl = [slice(HEAD_PAD * hh, HEAD_PAD * (hh + 1)) for hh in heads]
    vsl = [slice(V_DIM * hh, V_DIM * (hh + 1)) for hh in heads]

    def update(s, vt, m, l, acc):
        m_new = jnp.maximum(m, jnp.max(s, axis=0, keepdims=True))
        alpha = jnp.exp2(m - m_new)
        p = jnp.exp2(s - m_new)
        l = alpha * l + jnp.sum(p, axis=0, keepdims=True)
        acc = alpha * acc + jnp.dot(vt, p.astype(BF16), preferred_element_type=F32)
        return m_new, l, acc

    def q_body(qi, carry):
        row0 = pl.multiple_of(qi * ATT_T, ATT_T)
        qs = [q_ref[0, pl.ds(row0, ATT_T), hsl[hh]] for hh in heads]
        state = []
        for hh in heads:
            s = lax.dot_general(km_ref[:, hsl[hh]], qs[hh], nt, preferred_element_type=F32)
            m = jnp.max(s, axis=0, keepdims=True)
            p = jnp.exp2(s - m)
            l = jnp.sum(p, axis=0, keepdims=True)
            acc = jnp.dot(vtm_ref[0, 0, vsl[hh], :], p.astype(BF16), preferred_element_type=F32)
            state += [m, l, acc]

        def scores(j, hh):
            k0 = pl.multiple_of(j * ATT_T, ATT_T)
            return lax.dot_general(k_ref[0, pl.ds(k0, ATT_T), hsl[hh]], qs[hh], nt, preferred_element_type=F32)

        for hh in heads:
            s_sc[hh] = scores(0, hh)

        def kv_body(j, st):
            out = []
            for hh in heads:
                s_cur = s_sc[hh]
                s_next = scores(j + 1, hh)
                out += update(s_cur, vt_ref[0, j, vsl[hh], :], *st[3 * hh:3 * hh + 3])
                s_sc[hh] = s_next
            return tuple(out)

        state = lax.fori_loop(0, qi, kv_body, tuple(state))
        outs = []
        for hh in heads:
            s = jnp.where(diag_visible, s_sc[hh], NEG_INF)
            m, l, acc = update(s, vt_ref[0, qi, vsl[hh], :], *state[3 * hh:3 * hh + 3])
            outs.append(acc / l)
        o_ref[0, pl.ds(row0, ATT_T), :] = jnp.concatenate(outs, axis=0).T
        return carry

    lax.fori_loop(0, nq, q_body, 0)


def _attn_call(q, k, vt, k_meta, vt_meta):
    b, seq, _ = q.shape
    nk = seq // ATT_T
    hb = ATT_HEADS
    return pl.pallas_call(
        _attn_kernel,
        grid=(b, N_HEADS // hb),
        in_specs=[pl.BlockSpec((1, seq, hb * HEAD_PAD), lambda i, j: (i, 0, j)),
                  pl.BlockSpec((1, seq, hb * HEAD_PAD), lambda i, j: (i, 0, j)),
                  pl.BlockSpec((1, nk, hb * V_DIM, ATT_T), lambda i, j: (i, 0, j, 0)),
                  pl.BlockSpec((N_META, hb * HEAD_PAD), lambda i, j: (0, j)),
                  pl.BlockSpec((1, 1, hb * V_DIM, N_META), lambda i, j: (0, 0, j, 0))],
        out_specs=pl.BlockSpec((1, seq, hb * V_DIM), lambda i, j: (i, 0, j)),
        out_shape=jax.ShapeDtypeStruct((b, seq, D_ATTN), F32),
        scratch_shapes=[pltpu.VMEM((hb, ATT_T, ATT_T), F32)],
        compiler_params=pltpu.CompilerParams(dimension_semantics=("arbitrary", "arbitrary"),
                                             vmem_limit_bytes=VMEM_LIMIT),
        name="attn",
    )(q, k, vt, k_meta, vt_meta)


def _mix_kernel(x_ref, ig_ref, ib_ref, u_ref, uh_ref, pre_ref, cw_ref, cb_ref, clg_ref, clb_ref,
                cog_ref, o_ref, aog_ref, wo_ref, l1g_ref, l1b_ref, wr_ref, br_ref,
                h1_ref, hp_ref, ri_ref, rw_ref, cnt_ref, win_sc, sh_sc, carry_sc):
    bi = pl.program_id(0)
    ti = pl.program_id(1)
    tm = MIX_TM

    @pl.when((bi == 0) & (ti == 0))
    def _():
        carry_sc[...] = jnp.zeros_like(carry_sc)

    win_sc[0:HALO, :] = jnp.where(ti == 0, pre_ref[...], uh_ref[0])
    win_sc[HALO:, :] = u_ref[0]
    first = HALO - (CONV_K - 1)
    acc = jnp.broadcast_to(cb_ref[...], (tm, C_CONV))
    for res in range(SUBLANES):
        offs = [o for o in range(first, first + CONV_K) if o % SUBLANES == res]
        rows = tm + offs[-1] - res
        if res:
            sh_sc[res - 1, 0:rows, :] = win_sc[pl.ds(res, rows), :]
        for o in offs:
            src = sh_sc[res - 1, o - res:o - res + tm, :] if res else win_sc[o:o + tm, :]
            acc = acc + src * cw_ref[o - first:o - first + 1, :]
    c = _ln(acc, clg_ref[...], clb_ref[...])
    c = c * _sigmoid(c)
    cn = _rms(c, cog_ref[...])
    on = _rms(o_ref[0], aog_ref[...])
    y = (jnp.dot(cn.astype(BF16), wo_ref[0:C_CONV, :], preferred_element_type=F32)
         + jnp.dot(on.astype(BF16), wo_ref[C_CONV:, :], preferred_element_type=F32))
    h = _ln(x_ref[0], ig_ref[...], ib_ref[...])
    h1 = _ln(DEEPNORM_ALPHA * h + y, l1g_ref[...], l1b_ref[...])
    h1_ref[0] = h1
    hp_ref[0] = _pack_bf16_pair(h1[:, :HALF_D], h1[:, HALF_D:])

    logits = jnp.dot(h1, wr_ref[...], preferred_element_type=F32,
                     precision=lax.Precision.HIGHEST) + br_ref[...]
    lane = lax.broadcasted_iota(jnp.int32, (tm, LANES), 1)
    big = jnp.int32(1 << 20)
    is_g = (lane >= N_EXPERTS) & (lane < N_EXPERTS + N_GROUPS)
    gl = jnp.where(is_g, logits, -jnp.inf)
    gmax = jnp.max(gl, axis=1, keepdims=True)
    g_idx = jnp.min(jnp.where(gl == gmax, lane - N_EXPERTS, big), axis=1, keepdims=True)
    g_w = 1.0 / jnp.sum(jnp.exp(gl - gmax), axis=1, keepdims=True)
    in_grp = (lane < N_EXPERTS) & ((lane // EXPERTS_PER_GROUP) == g_idx)
    el = jnp.where(in_grp, logits, -jnp.inf)
    v1 = jnp.max(el, axis=1, keepdims=True)
    i1 = jnp.min(jnp.where(el == v1, lane, big), axis=1, keepdims=True)
    el2 = jnp.where(lane == i1, -jnp.inf, el)
    v2 = jnp.max(el2, axis=1, keepdims=True)
    i2 = jnp.min(jnp.where(el2 == v2, lane, big), axis=1, keepdims=True)
    e2 = jnp.exp(v2 - v1)
    w1 = g_w / (1.0 + e2)
    w2 = g_w * e2 / (1.0 + e2)

    oh1 = lane == i1
    oh2 = lane == i2
    oh = jnp.where(oh1 | oh2, 1.0, 0.0)
    rt = lax.broadcasted_iota(jnp.int32, (tm, tm), 0)
    ct = lax.broadcasted_iota(jnp.int32, (tm, tm), 1)
    ltri = jnp.where(ct < rt, 1.0, 0.0).astype(BF16)
    before = jnp.dot(ltri, oh.astype(BF16), preferred_element_type=F32) + carry_sc[...]
    r1 = jnp.sum(jnp.where(oh1, before, 0.0), axis=1, keepdims=True).astype(jnp.int32)
    r2 = jnp.sum(jnp.where(oh2, before, 0.0), axis=1, keepdims=True).astype(jnp.int32)
    carry_sc[...] = carry_sc[...] + jnp.sum(oh, axis=0, keepdims=True)
    cnt_ref[...] = carry_sc[...]

    ri = jnp.where(lane == 0, i1, jnp.where(lane == 1, i2, jnp.where(lane == 2, r1, r2)))
    rw = jnp.where(lane == 0, w1, w2)
    ri_ref[...] = ri.T[:8, :]
    rw_ref[0] = rw[:, :8]


def _mix_call(x, ig, ib, u, pre, cw, cb, clg, clb, cog, o, aog, wo, l1g, l1b, wr, br):
    b, seq, _ = x.shape
    tm = MIX_TM
    nt = seq // tm
    full = lambda shape: pl.BlockSpec(shape, lambda i, j: (0,) * len(shape))
    tile = lambda w: pl.BlockSpec((1, tm, w), lambda i, j: (i, j, 0))
    halo = pl.BlockSpec((1, HALO, C_CONV), lambda i, j: (i, jnp.maximum(j * (tm // HALO) - 1, 0), 0))
    return pl.pallas_call(
        _mix_kernel,
        grid=(b, nt),
        in_specs=[tile(D_MODEL), full((1, D_MODEL)), full((1, D_MODEL)),
                  tile(C_CONV), halo, full((HALO, C_CONV)), full((HALO, C_CONV)), full((1, C_CONV)),
                  full((1, C_CONV)), full((1, C_CONV)), full((1, C_CONV)),
                  tile(D_ATTN), full((1, D_ATTN)), full((D_MODEL, D_MODEL)),
                  full((1, D_MODEL)), full((1, D_MODEL)), full((D_MODEL, LANES)), full((1, LANES))],
        out_specs=[tile(D_MODEL), tile(HALF_D), pl.BlockSpec((8, tm), lambda i, j: (0, i * nt + j)), tile(8),
                   full((1, LANES))],
        out_shape=[jax.ShapeDtypeStruct((b, seq, D_MODEL), F32),
                   jax.ShapeDtypeStruct((b, seq, HALF_D), jnp.uint32),
                   jax.ShapeDtypeStruct((8, b * seq), jnp.int32),
                   jax.ShapeDtypeStruct((b, seq, 8), F32),
                   jax.ShapeDtypeStruct((1, LANES), F32)],
        scratch_shapes=[pltpu.VMEM((HALO + tm, C_CONV), F32),
                        pltpu.VMEM((SUBLANES - 1, HALO + tm, C_CONV), F32),
                        pltpu.VMEM((1, LANES), F32)],
        compiler_params=pltpu.CompilerParams(dimension_semantics=("arbitrary", "arbitrary"),
                                             vmem_limit_bytes=VMEM_LIMIT),
        name="mix",
    )(x, ig, ib, u, u, pre, cw, cb, clg, clb, cog, o, aog, wo, l1g, l1b, wr, br)


def _tables_kernel(ps_ref, pe_ref, cn_ref, ri_ref, dest_ref, blk_ref):
    ri = ri_ref[...]
    start = jnp.zeros_like(ri)
    for e in range(N_EXPERTS):
        start = jnp.where(ri == e, ps_ref[e], start)
    dest_ref[...] = jnp.zeros_like(dest_ref)
    dest_ref[0:2, :] = start[0:2, :] + ri_ref[2:4, :]
    blk_start = lax.broadcasted_iota(jnp.int32, blk_ref.shape, 1) * MOE_BM
    be = jnp.zeros(blk_ref.shape, jnp.int32)
    for e in range(N_EXPERTS):
        be = be + jnp.where(pe_ref[e] <= blk_start, 1, 0)
    be = jnp.minimum(be, N_EXPERTS - 1)
    end = jnp.zeros(blk_ref.shape, jnp.int32)
    for e in range(N_EXPERTS):
        end = jnp.where(be == e, ps_ref[e] + cn_ref[e], end)
    valid = jnp.clip(end - blk_start, 0, MOE_BM)
    row = lax.broadcasted_iota(jnp.int32, blk_ref.shape, 0)
    blk_ref[...] = jnp.where(row == 0, be, valid)


def _tables_call(pad_start, pad_end, counts, ri, n_blocks):
    t = ri.shape[1]
    blk_w = -(-n_blocks // LANES) * LANES
    return pl.pallas_call(
        _tables_kernel,
        grid_spec=pltpu.PrefetchScalarGridSpec(
            num_scalar_prefetch=3,
            grid=(1,),
            in_specs=[pl.BlockSpec((SUBLANES, t), lambda i, *_: (0, 0))],
            out_specs=[pl.BlockSpec((SUBLANES, t), lambda i, *_: (0, 0)),
                       pl.BlockSpec((SUBLANES, blk_w), lambda i, *_: (0, 0))]),
        out_shape=[jax.ShapeDtypeStruct((SUBLANES, t), jnp.int32),
                   jax.ShapeDtypeStruct((SUBLANES, blk_w), jnp.int32)],
        compiler_params=pltpu.CompilerParams(dimension_semantics=("arbitrary",),
                                             vmem_limit_bytes=VMEM_LIMIT),
        name="tables",
    )(pad_start, pad_end, counts, ri)


def _dispatch_kernel(nv_ref, dest_ref, h_ref, xb_ref, zero_sc, sem):
    tm = DISP_TM
    n_blocks = nv_ref.shape[0]

    @pl.when(pl.program_id(0) == 0)
    def _():
        zero_sc[...] = jnp.zeros_like(zero_sc)

        def zero_copy(i):
            return pltpu.make_async_copy(zero_sc, xb_ref.at[pl.ds(i * MOE_BM, MOE_BM)], sem)

        def start(i, carry):
            @pl.when(nv_ref[i] < MOE_BM)
            def _():
                zero_copy(i).start()
            return carry

        def wait(i, carry):
            @pl.when(nv_ref[i] < MOE_BM)
            def _():
                zero_copy(i).wait()
            return carry

        lax.fori_loop(0, n_blocks, start, 0)
        lax.fori_loop(0, n_blocks, wait, 0)

    def issue(t, carry):
        src = h_ref.at[pl.ds(t, 1)]
        pltpu.make_async_copy(src, xb_ref.at[pl.ds(dest_ref[0, t], 1)], sem).start(priority=0)
        pltpu.make_async_copy(src, xb_ref.at[pl.ds(dest_ref[1, t], 1)], sem).start(priority=1)
        return carry

    lax.fori_loop(0, tm, issue, 0, unroll=8)
    for _ in range(2):
        pltpu.make_async_copy(h_ref, xb_ref.at[pl.ds(0, tm)], sem).wait()


def _dispatch_call(blk_valid, dest, h1):
    t, w = h1.shape
    tm = DISP_TM
    return pl.pallas_call(
        _dispatch_kernel,
        grid_spec=pltpu.PrefetchScalarGridSpec(
            num_scalar_prefetch=1,
            grid=(t // tm,),
            in_specs=[pl.BlockSpec((SUBLANES, tm), lambda i, nv: (0, i), memory_space=pltpu.SMEM),
                      pl.BlockSpec((tm, w), lambda i, nv: (i, 0))],
            out_specs=pl.BlockSpec(memory_space=pl.ANY),
            scratch_shapes=[pltpu.VMEM((MOE_BM, w), h1.dtype), pltpu.SemaphoreType.DMA]),
        out_shape=jax.ShapeDtypeStruct((blk_valid.shape[0] * MOE_BM, w), h1.dtype),
        compiler_params=pltpu.CompilerParams(dimension_semantics=("arbitrary",),
                                             vmem_limit_bytes=VMEM_LIMIT),
        name="dispatch",
    )(blk_valid, dest, h1)


def _expert_kernel(be_ref, nv_ref, x_ref, wg_ref, wu_ref, wd_ref, y_ref, wg_sc, wu_sc, wd_sc):
    i = pl.program_id(0)
    nv = nv_ref[i]

    @pl.when((i == 0) | (be_ref[i] != be_ref[jnp.maximum(i - 1, 0)]))
    def _():
        wg_sc[...] = wg_ref[0, 0].astype(BF16)
        wu_sc[...] = wu_ref[0, 0].astype(BF16)
        wd_sc[...] = wd_ref[0, 0].astype(BF16)

    @pl.when(nv > 0)
    def _():
        x_lo, x_hi = _unpack_bf16_pair(x_ref[...])
        x_lo = x_lo.astype(BF16)
        x_hi = x_hi.astype(BF16)
        g = (jnp.dot(x_lo, wg_sc[:HALF_D, :], preferred_element_type=F32)
             + jnp.dot(x_hi, wg_sc[HALF_D:, :], preferred_element_type=F32))
        up = (jnp.dot(x_lo, wu_sc[:HALF_D, :], preferred_element_type=F32)
              + jnp.dot(x_hi, wu_sc[HALF_D:, :], preferred_element_type=F32))
        act = g * _sigmoid(g) * up
        y = jnp.dot(act.astype(BF16), wd_sc[...], preferred_element_type=F32)
        y_ref[...] = _pack_bf16_pair(y[:, :HALF_D], y[:, HALF_D:])

    @pl.when(nv == 0)
    def _():
        y_ref[...] = jnp.zeros_like(y_ref)


def _expert_call(blk_e, blk_valid, xb, w_gate, w_up, w_down):
    r = xb.shape[0]
    bm = MOE_BM
    wspec = lambda a, b: pl.BlockSpec((1, 1, a, b), lambda i, be, nv: (0, be[i], 0, 0))
    return pl.pallas_call(
        _expert_kernel,
        grid_spec=pltpu.PrefetchScalarGridSpec(
            num_scalar_prefetch=2,
            grid=(r // bm,),
            in_specs=[pl.BlockSpec((bm, HALF_D), lambda i, be, nv: (i, 0)),
                      wspec(D_MODEL, D_EXPERT), wspec(D_MODEL, D_EXPERT), wspec(D_EXPERT, D_MODEL)],
            out_specs=pl.BlockSpec((bm, HALF_D), lambda i, be, nv: (i, 0)),
            scratch_shapes=[pltpu.VMEM((D_MODEL, D_EXPERT), BF16), pltpu.VMEM((D_MODEL, D_EXPERT), BF16),
                            pltpu.VMEM((D_EXPERT, D_MODEL), BF16)]),
        out_shape=jax.ShapeDtypeStruct((r, HALF_D), jnp.uint32),
        compiler_params=pltpu.CompilerParams(dimension_semantics=("arbitrary",),
                                             vmem_limit_bytes=VMEM_LIMIT),
        name="experts",
    )(blk_e, blk_valid, xb, w_gate, w_up, w_down)


def _combine_kernel(dest_ref, h_ref, w_ref, g_ref, b_ref, yb_ref, o_ref, y0_sc, y1_sc, sem):
    tm = COMB_TM

    def issue(t, carry):
        pltpu.make_async_copy(yb_ref.at[pl.ds(dest_ref[0, t], 1)], y0_sc.at[pl.ds(t, 1)],
                              sem).start(priority=0)
        pltpu.make_async_copy(yb_ref.at[pl.ds(dest_ref[1, t], 1)], y1_sc.at[pl.ds(t, 1)],
                              sem).start(priority=1)
        return carry

    lax.fori_loop(0, tm, issue, 0, unroll=8)
    pltpu.make_async_copy(yb_ref.at[pl.ds(0, tm)], y0_sc, sem).wait()
    pltpu.make_async_copy(yb_ref.at[pl.ds(0, tm)], y1_sc, sem).wait()
    w = w_ref[...]
    y0_lo, y0_hi = _unpack_bf16_pair(y0_sc[...])
    y1_lo, y1_hi = _unpack_bf16_pair(y1_sc[...])
    y = jnp.concatenate([w[:, 0:1] * y0_lo + w[:, 1:2] * y1_lo, w[:, 0:1] * y0_hi + w[:, 1:2] * y1_hi], axis=1)
    o_ref[...] = _ln(DEEPNORM_ALPHA * h_ref[...] + y, g_ref[...], b_ref[...])


def _combine_call(dest, h1, rw, g, b, yb):
    t = h1.shape[0]
    tm = COMB_TM
    return pl.pallas_call(
        _combine_kernel,
        grid=(t // tm,),
        in_specs=[pl.BlockSpec((SUBLANES, tm), lambda i: (0, i), memory_space=pltpu.SMEM),
                  pl.BlockSpec((tm, D_MODEL), lambda i: (i, 0)),
                  pl.BlockSpec((tm, 8), lambda i: (i, 0)),
                  pl.BlockSpec((1, D_MODEL), lambda i: (0, 0)),
                  pl.BlockSpec((1, D_MODEL), lambda i: (0, 0)),
                  pl.BlockSpec(memory_space=pl.ANY)],
        out_specs=pl.BlockSpec((tm, D_MODEL), lambda i: (i, 0)),
        out_shape=jax.ShapeDtypeStruct((t, D_MODEL), F32),
        scratch_shapes=[pltpu.VMEM((tm, HALF_D), jnp.uint32), pltpu.VMEM((tm, HALF_D), jnp.uint32),
                        pltpu.SemaphoreType.DMA],
        compiler_params=pltpu.CompilerParams(dimension_semantics=("arbitrary",),
                                             vmem_limit_bytes=VMEM_LIMIT),
        name="combine",
    )(dest, h1, rw, g, b, yb)


def _take_cols(w, idx):
    wz = jnp.concatenate([w, jnp.zeros((w.shape[0], 1), w.dtype)], axis=1)
    return wz[:, np.where(idx < 0, w.shape[1], idx)]


def _layout_indices():
    lane_src = np.full((HEAD_PAD,), -1, np.int64)
    lane_src[0:ROPE_HALF] = QK_NOPE + np.arange(ROPE_HALF)
    lane_src[ROPE_HALF:X2_LANE] = np.arange(X2_LANE - ROPE_HALF)
    lane_src[X2_LANE:X2_LANE + ROPE_HALF] = QK_NOPE + ROPE_HALF + np.arange(ROPE_HALF)
    n_lo = X2_LANE - ROPE_HALF
    lane_src[X2_LANE + ROPE_HALF:X2_LANE + ROPE_HALF + QK_NOPE - n_lo] = n_lo + np.arange(QK_NOPE - n_lo)
    q_idx = np.concatenate([np.where(lane_src >= 0, lane_src + (QK_NOPE + QK_ROPE) * h, -1)
                            for h in range(N_HEADS)])
    nope_src = np.where(lane_src < QK_NOPE, lane_src, -1)
    k_idx = np.concatenate([np.where(nope_src >= 0, nope_src + (QK_NOPE + V_DIM) * h, -1)
                            for h in range(N_HEADS)])
    v_idx = np.concatenate([QK_NOPE + np.arange(V_DIM) + (QK_NOPE + V_DIM) * h for h in range(N_HEADS)])
    kr_idx = np.full((LANES,), -1, np.int64)
    kr_idx[0:ROPE_HALF] = np.arange(ROPE_HALF)
    kr_idx[X2_LANE:X2_LANE + ROPE_HALF] = ROPE_HALF + np.arange(ROPE_HALF)
    return q_idx, k_idx, v_idx, kr_idx


def _rope_lane_tables(length):
    inv_freq = np.float32(ROPE_THETA) ** (-np.arange(ROPE_HALF, dtype=np.float32) / np.float32(ROPE_HALF))
    ang = np.arange(length, dtype=np.float32)[:, None] * inv_freq[None, :].astype(np.float32)
    cos, sin = np.cos(ang).astype(np.float32), np.sin(ang).astype(np.float32)
    ctab = np.ones((length, LANES), np.float32)
    stab = np.zeros((length, LANES), np.float32)
    for lo, sign in ((0, -1.0), (X2_LANE, 1.0)):
        ctab[:, lo:lo + ROPE_HALF] = cos
        stab[:, lo:lo + ROPE_HALF] = sign * sin
    return jnp.asarray(ctab), jnp.asarray(stab)


def kernel(x, meta_tokens, ln_in_g, ln_in_b, w_in, conv_w, conv_b, conv_ln_g, conv_ln_b, q_norm_g, w_uq,
           kv_norm_g, w_ukv, conv_out_g, attn_out_g, w_o, ln1_g, ln1_b, w_rg, b_rg, w_re, b_re,
           w_gate, w_up, w_down, ln2_g, ln2_b):
    assert w_in.shape[0] == 1, "single layer"
    bsz, seq, d = x.shape
    t = bsz * seq
    row = lambda a: a.reshape(1, -1)

    q_idx, k_idx, v_idx, kr_idx = _layout_indices()
    c4 = 2 * C_CONV + Q_LORA + KV_LORA
    w_in_l = jnp.concatenate([w_in[0][:, :c4], _take_cols(w_in[0][:, c4:], kr_idx)], axis=1).astype(BF16)
    wuq_l = _take_cols(w_uq[0], q_idx).astype(BF16)
    wuk_l = _take_cols(w_ukv[0], k_idx).astype(BF16)
    wvt_l = w_ukv[0][:, v_idx].T.astype(BF16)
    wo_l = w_o[0].astype(BF16)
    wr_l = jnp.concatenate([w_re[0], w_rg[0], jnp.zeros((d, LANES - N_EXPERTS - N_GROUPS), F32)], axis=1)
    br_l = jnp.concatenate([b_re[0], b_rg[0], jnp.zeros((LANES - N_EXPERTS - N_GROUPS,), F32)]).reshape(1, LANES)
    ctab, stab = _rope_lane_tables(N_META + seq)

    proj_w = (row(ln_in_g), row(ln_in_b), w_in_l, row(q_norm_g[0]), wuq_l, row(kv_norm_g[0]), wuk_l, wvt_l)
    u_m, _, k_m, vt_m = _proj_call(meta_tokens.astype(F32), N_META, 1, *proj_w, ctab[:N_META], stab[:N_META])
    u, q, k, vt = _proj_call(x.reshape(t, d), PROJ_TM, seq // PROJ_TM, *proj_w, ctab[N_META:], stab[N_META:])

    o = _attn_call(q.reshape(bsz, seq, -1), k.reshape(bsz, seq, -1),
                   vt.reshape(bsz, seq // ATT_T, D_ATTN, ATT_T), k_m, vt_m)

    pre = jnp.concatenate([jnp.zeros((HALO - N_META, C_CONV), F32), u_m], axis=0)
    cw = jnp.concatenate([conv_w[0], jnp.zeros((HALO - CONV_K, C_CONV), F32)], axis=0)
    h1, hp, ri, rw, cnt = _mix_call(x, row(ln_in_g), row(ln_in_b), u.reshape(bsz, seq, C_CONV), pre, cw,
                                row(conv_b[0]), row(conv_ln_g[0]), row(conv_ln_b[0]), row(conv_out_g[0]),
                                o, row(attn_out_g[0]), wo_l, row(ln1_g[0]), row(ln1_b[0]), wr_l, br_l)

    counts = cnt[0, :N_EXPERTS].astype(jnp.int32)
    padded = (counts + MOE_BM - 1) // MOE_BM * MOE_BM
    pad_end = jnp.cumsum(padded)
    pad_start = pad_end - padded
    n_blocks = (2 * t) // MOE_BM + N_EXPERTS
    dest, blk = _tables_call(pad_start, pad_end, counts, ri, n_blocks)
    blk_e, blk_valid = blk[0, :n_blocks], blk[1, :n_blocks]

    h1f = h1.reshape(t, d)
    xb = _dispatch_call(blk_valid, dest, hp.reshape(t, HALF_D))
    yb = _expert_call(blk_e, blk_valid, xb, w_gate, w_up, w_down)
    out = _combine_call(dest, h1f, rw.reshape(t, 8), row(ln2_g[0]), row(ln2_b[0]), yb)
    return out.reshape(bsz, seq, d)
```

```python
import math

import numpy as np
import jax
import jax.numpy as jnp
from jax import lax
from jax.experimental import pallas as pl
from jax.experimental.pallas import tpu as pltpu

F32 = jnp.float32
BF16 = jnp.bfloat16

D_MODEL = 1024
HALF_D = D_MODEL // 2
N_META = 16
CHUNK = 64
C_CONV = 512
CONV_K = 31
N_HEADS = 8
QK_NOPE = 64
QK_ROPE = 32
V_DIM = 64
Q_LORA = 384
KV_LORA = 256
ROPE_THETA = 10000.0
D_ATTN = N_HEADS * V_DIM
N_GROUPS = 4
EXPERTS_PER_GROUP = 8
N_EXPERTS = N_GROUPS * EXPERTS_PER_GROUP
D_EXPERT = 256
DEEPNORM_ALPHA = 2.0 ** 0.25
EPS = 1e-5
NEG_INF = -1e30

LANES = 128
SUBLANES = 8
HEAD_PAD = LANES
ROPE_HALF = QK_ROPE // 2
X2_LANE = 64
Z_COLS = 2 * C_CONV + Q_LORA + KV_LORA + LANES
Q_SCALE = (QK_NOPE + QK_ROPE) ** -0.5 * math.log2(math.e)

PROJ_TM = 512
ATT_T = 256
ATT_HEADS = 8
MIX_TM = 256
HALO = 32
MOE_BM = 256
DISP_TM = 1024
COMB_TM = 1024
VMEM_LIMIT = 56 * 1024 * 1024


def _ln(x, g, b):
    mu = jnp.mean(x, axis=-1, keepdims=True)
    xc = x - mu
    var = jnp.mean(xc * xc, axis=-1, keepdims=True)
    return xc * lax.rsqrt(var + EPS) * g + b


def _rms(x, g):
    return x * lax.rsqrt(jnp.mean(x * x, axis=-1, keepdims=True) + EPS) * g


def _sigmoid(x):
    return 1.0 / (1.0 + jnp.exp(-x))


def _pack_bf16_pair(lo, hi):
    lo_bits = lax.bitcast_convert_type(lo.astype(BF16).astype(F32), jnp.uint32)
    hi_bits = lax.bitcast_convert_type(hi.astype(BF16).astype(F32), jnp.uint32)
    return (lo_bits >> 16) | (hi_bits & jnp.uint32(0xFFFF0000))


def _unpack_bf16_pair(packed):
    lo = lax.bitcast_convert_type(packed << 16, F32)
    hi = lax.bitcast_convert_type(packed & jnp.uint32(0xFFFF0000), F32)
    return lo, hi


def _proj_kernel(x_ref, g_ref, b_ref, win_ref, qg_ref, wuq_ref, kvg_ref, wuk_ref, wvt_ref, c_ref, s_ref,
                 u_ref, q_ref, k_ref, vt_ref):
    h = _ln(x_ref[...], g_ref[...], b_ref[...])
    z = jnp.dot(h.astype(BF16), win_ref[...], preferred_element_type=F32)
    u_ref[...] = z[:, :C_CONV] * _sigmoid(z[:, C_CONV:2 * C_CONV])
    c0 = 2 * C_CONV
    cq = z[:, c0:c0 + Q_LORA]
    ckv = z[:, c0 + Q_LORA:c0 + Q_LORA + KV_LORA]
    kr = z[:, c0 + Q_LORA + KV_LORA:]
    cs = c_ref[...]
    sn = s_ref[...]
    q = jnp.dot(_rms(cq, qg_ref[...]).astype(BF16), wuq_ref[...], preferred_element_type=F32)
    kvn = _rms(ckv, kvg_ref[...]).astype(BF16)
    kn = jnp.dot(kvn, wuk_ref[...], preferred_element_type=F32)
    kr_rot = kr * cs + pltpu.roll(kr, X2_LANE, 1) * sn
    for hh in range(N_HEADS):
        sl = slice(HEAD_PAD * hh, HEAD_PAD * (hh + 1))
        qh = q[:, sl]
        qh = (qh * cs + pltpu.roll(qh, X2_LANE, 1) * sn) * Q_SCALE
        q_ref[:, sl] = qh.astype(BF16)
        k_ref[:, sl] = (kn[:, sl] + kr_rot).astype(BF16)
    vt = lax.dot_general(wvt_ref[...], kvn, (((1,), (1,)), ((), ())), preferred_element_type=F32)
    n_sub, sub_w = vt_ref.shape[1], vt_ref.shape[3]
    for tt in range(n_sub):
        vt_ref[0, tt] = vt[:, tt * sub_w:(tt + 1) * sub_w].astype(BF16)


def _proj_call(x2d, tm, pos_blocks, ln_g, ln_b, w_in, qg, wuq, kvg, wuk, wvt, ctab, stab):
    t = x2d.shape[0]
    sub_w = min(tm, ATT_T)
    n_sub = tm // sub_w
    full = lambda shape: pl.BlockSpec(shape, lambda i: (0,) * len(shape))
    row = lambda w: pl.BlockSpec((tm, w), lambda i: (i, 0))
    tab = pl.BlockSpec((tm, LANES), lambda i: (i % pos_blocks, 0))
    return pl.pallas_call(
        _proj_kernel,
        grid=(t // tm,),
        in_specs=[row(D_MODEL), full((1, D_MODEL)), full((1, D_MODEL)), full((D_MODEL, Z_COLS)),
                  full((1, Q_LORA)), full((Q_LORA, N_HEADS * HEAD_PAD)),
                  full((1, KV_LORA)), full((KV_LORA, N_HEADS * HEAD_PAD)), full((D_ATTN, KV_LORA)), tab, tab],
        out_specs=[row(C_CONV), row(N_HEADS * HEAD_PAD), row(N_HEADS * HEAD_PAD),
                   pl.BlockSpec((1, n_sub, D_ATTN, sub_w), lambda i: (i, 0, 0, 0))],
        out_shape=[jax.ShapeDtypeStruct((t, C_CONV), F32),
                   jax.ShapeDtypeStruct((t, N_HEADS * HEAD_PAD), BF16),
                   jax.ShapeDtypeStruct((t, N_HEADS * HEAD_PAD), BF16),
                   jax.ShapeDtypeStruct((t // tm, n_sub, D_ATTN, sub_w), BF16)],
        compiler_params=pltpu.CompilerParams(dimension_semantics=("arbitrary",),
                                             vmem_limit_bytes=VMEM_LIMIT),
        name="proj",
    )(x2d, ln_g, ln_b, w_in, qg, wuq, kvg, wuk, wvt, ctab, stab)


def _attn_kernel(q_ref, k_ref, vt_ref, km_ref, vtm_ref, o_ref, s_sc):
    seq = q_ref.shape[1]
    nq = seq // ATT_T
    key_chunk = lax.broadcasted_iota(jnp.int32, (ATT_T, ATT_T), 0) // CHUNK
    qry_chunk = lax.broadcasted_iota(jnp.int32, (ATT_T, ATT_T), 1) // CHUNK
    diag_visible = key_chunk <= qry_chunk
    nt = (((1,), (1,)), ((), ()))
    heads = range(ATT_HEADS)
    hsl = [slice(HEAD_PAD * hh, HEAD_PAD * (hh + 1)) for hh in heads]
    vsl = [slice(V_DIM * hh, V_DIM * (hh + 1)) for hh in heads]

    def update(s, vt, m, l, acc):
        m_new = jnp.maximum(m, jnp.max(s, axis=0, keepdims=True))
        alpha = jnp.exp2(m - m_new)
        p = jnp.exp2(s - m_new)
        l = alpha * l + jnp.sum(p, axis=0, keepdims=True)
        acc = alpha * acc + jnp.dot(vt, p.astype(BF16), preferred_element_type=F32)
        return m_new, l, acc

    def q_body(qi, carry):
        row0 = pl.multiple_of(qi * ATT_T, ATT_T)
        qs = [q_ref[0, pl.ds(row0, ATT_T), hsl[hh]] for hh in heads]
        state = []
        for hh in heads:
            s = lax.dot_general(km_ref[:, hsl[hh]], qs[hh], nt, preferred_element_type=F32)
            m = jnp.max(s, axis=0, keepdims=True)
            p = jnp.exp2(s - m)
            l = jnp.sum(p, axis=0, keepdims=True)
            acc = jnp.dot(vtm_ref[0, 0, vsl[hh], :], p.astype(BF16), preferred_element_type=F32)
            state += [m, l, acc]

        def scores(j, hh):
            k0 = pl.multiple_of(j * ATT_T, ATT_T)
            return lax.dot_general(k_ref[0, pl.ds(k0, ATT_T), hsl[hh]], qs[hh], nt, preferred_element_type=F32)

        for hh in heads:
            s_sc[hh] = scores(0, hh)

        def kv_body(j, st):
            out = []
            for hh in heads:
                s_cur = s_sc[hh]
                s_next = scores(j + 1, hh)
                out += update(s_cur, vt_ref[0, j, vsl[hh], :], *st[3 * hh:3 * hh + 3])
                s_sc[hh] = s_next
            return tuple(out)

        state = lax.fori_loop(0, qi, kv_body, tuple(state))
        outs = []
        for hh in heads:
            s = jnp.where(diag_visible, s_sc[hh], NEG_INF)
            m, l, acc = update(s, vt_ref[0, qi, vsl[hh], :], *state[3 * hh:3 * hh + 3])
            outs.append(acc / l)
        o_ref[0, pl.ds(row0, ATT_T), :] = jnp.concatenate(outs, axis=0).T
        return carry

    lax.fori_loop(0, nq, q_body, 0)


def _attn_call(q, k, vt, k_meta, vt_meta):
    b, seq, _ = q.shape
    nk = seq // ATT_T
    hb = ATT_HEADS
    return pl.pallas_call(
        _attn_kernel,
        grid=(b, N_HEADS // hb),
        in_specs=[pl.BlockSpec((1, seq, hb * HEAD_PAD), lambda i, j: (i, 0, j)),
                  pl.BlockSpec((1, seq, hb * HEAD_PAD), lambda i, j: (i, 0, j)),
                  pl.BlockSpec((1, nk, hb * V_DIM, ATT_T), lambda i, j: (i, 0, j, 0)),
                  pl.BlockSpec((N_META, hb * HEAD_PAD), lambda i, j: (0, j)),
                  pl.BlockSpec((1, 1, hb * V_DIM, N_META), lambda i, j: (0, 0, j, 0))],
        out_specs=pl.BlockSpec((1, seq, hb * V_DIM), lambda i, j: (i, 0, j)),
        out_shape=jax.ShapeDtypeStruct((b, seq, D_ATTN), F32),
        scratch_shapes=[pltpu.VMEM((hb, ATT_T, ATT_T), F32)],
        compiler_params=pltpu.CompilerParams(dimension_semantics=("arbitrary", "arbitrary"),
                                             vmem_limit_bytes=VMEM_LIMIT),
        name="attn",
    )(q, k, vt, k_meta, vt_meta)


def _mix_kernel(x_ref, ig_ref, ib_ref, u_ref, uh_ref, pre_ref, cw_ref, cb_ref, clg_ref, clb_ref,
                cog_ref, o_ref, aog_ref, wo_ref, l1g_ref, l1b_ref, wr_ref, br_ref,
                h1_ref, hp_ref, ri_ref, rw_ref, cnt_ref, win_sc, sh_sc, carry_sc):
    bi = pl.program_id(0)
    ti = pl.program_id(1)
    tm = MIX_TM

    @pl.when((bi == 0) & (ti == 0))
    def _():
        carry_sc[...] = jnp.zeros_like(carry_sc)

    win_sc[0:HALO, :] = jnp.where(ti == 0, pre_ref[...], uh_ref[0])
    win_sc[HALO:, :] = u_ref[0]
    first = HALO - (CONV_K - 1)
    acc = jnp.broadcast_to(cb_ref[...], (tm, C_CONV))
    for res in range(SUBLANES):
        offs = [o for o in range(first, first + CONV_K) if o % SUBLANES == res]
        rows = tm + offs[-1] - res
        if res:
            sh_sc[res - 1, 0:rows, :] = win_sc[pl.ds(res, rows), :]
        for o in offs:
            src = sh_sc[res - 1, o - res:o - res + tm, :] if res else win_sc[o:o + tm, :]
            acc = acc + src * cw_ref[o - first:o - first + 1, :]
    c = _ln(acc, clg_ref[...], clb_ref[...])
    c = c * _sigmoid(c)
    cn = _rms(c, cog_ref[...])
    on = _rms(o_ref[0], aog_ref[...])
    y = (jnp.dot(cn.astype(BF16), wo_ref[0:C_CONV, :], preferred_element_type=F32)
         + jnp.dot(on.astype(BF16), wo_ref[C_CONV:, :], preferred_element_type=F32))
    h = _ln(x_ref[0], ig_ref[...], ib_ref[...])
    h1 = _ln(DEEPNORM_ALPHA * h + y, l1g_ref[...], l1b_ref[...])
    h1_ref[0] = h1
    hp_ref[0] = _pack_bf16_pair(h1[:, :HALF_D], h1[:, HALF_D:])

    logits = jnp.dot(h1, wr_ref[...], preferred_element_type=F32,
                     precision=lax.Precision.HIGHEST) + br_ref[...]
    lane = lax.broadcasted_iota(jnp.int32, (tm, LANES), 1)
    big = jnp.int32(1 << 20)
    is_g = (lane >= N_EXPERTS) & (lane < N_EXPERTS + N_GROUPS)
    gl = jnp.where(is_g, logits, -jnp.inf)
    gmax = jnp.max(gl, axis=1, keepdims=True)
    g_idx = jnp.min(jnp.where(gl == gmax, lane - N_EXPERTS, big), axis=1, keepdims=True)
    g_w = 1.0 / jnp.sum(jnp.exp(gl - gmax), axis=1, keepdims=True)
    in_grp = (lane < N_EXPERTS) & ((lane // EXPERTS_PER_GROUP) == g_idx)
    el = jnp.where(in_grp, logits, -jnp.inf)
    v1 = jnp.max(el, axis=1, keepdims=True)
    i1 = jnp.min(jnp.where(el == v1, lane, big), axis=1, keepdims=True)
    el2 = jnp.where(lane == i1, -jnp.inf, el)
    v2 = jnp.max(el2, axis=1, keepdims=True)
    i2 = jnp.min(jnp.where(el2 == v2, lane, big), axis=1, keepdims=True)
    e2 = jnp.exp(v2 - v1)
    w1 = g_w / (1.0 + e2)
    w2 = g_w * e2 / (1.0 + e2)

    oh1 = lane == i1
    oh2 = lane == i2
    oh = jnp.where(oh1 | oh2, 1.0, 0.0)
    rt = lax.broadcasted_iota(jnp.int32, (tm, tm), 0)
    ct = lax.broadcasted_iota(jnp.int32, (tm, tm), 1)
    ltri = jnp.where(ct < rt, 1.0, 0.0).astype(BF16)
    before = jnp.dot(ltri, oh.astype(BF16), preferred_element_type=F32) + carry_sc[...]
    r1 = jnp.sum(jnp.where(oh1, before, 0.0), axis=1, keepdims=True).astype(jnp.int32)
    r2 = jnp.sum(jnp.where(oh2, before, 0.0), axis=1, keepdims=True).astype(jnp.int32)
    carry_sc[...] = carry_sc[...] + jnp.sum(oh, axis=0, keepdims=True)
    cnt_ref[...] = carry_sc[...]

    ri = jnp.where(lane == 0, i1, jnp.where(lane == 1, i2, jnp.where(lane == 2, r1, r2)))
    rw = jnp.where(lane == 0, w1, w2)
    ri_ref[...] = ri.T[:8, :]
    rw_ref[0] = rw[:, :8]


def _mix_call(x, ig, ib, u, pre, cw, cb, clg, clb, cog, o, aog, wo, l1g, l1b, wr, br):
    b, seq, _ = x.shape
    tm = MIX_TM
    nt = seq // tm
    full = lambda shape: pl.BlockSpec(shape, lambda i, j: (0,) * len(shape))
    tile = lambda w: pl.BlockSpec((1, tm, w), lambda i, j: (i, j, 0))
    halo = pl.BlockSpec((1, HALO, C_CONV), lambda i, j: (i, jnp.maximum(j * (tm // HALO) - 1, 0), 0))
    return pl.pallas_call(
        _mix_kernel,
        grid=(b, nt),
        in_specs=[tile(D_MODEL), full((1, D_MODEL)), full((1, D_MODEL)),
                  tile(C_CONV), halo, full((HALO, C_CONV)), full((HALO, C_CONV)), full((1, C_CONV)),
                  full((1, C_CONV)), full((1, C_CONV)), full((1, C_CONV)),
                  tile(D_ATTN), full((1, D_ATTN)), full((D_MODEL, D_MODEL)),
                  full((1, D_MODEL)), full((1, D_MODEL)), full((D_MODEL, LANES)), full((1, LANES))],
        out_specs=[tile(D_MODEL), tile(HALF_D), pl.BlockSpec((8, tm), lambda i, j: (0, i * nt + j)), tile(8),
                   full((1, LANES))],
        out_shape=[jax.ShapeDtypeStruct((b, seq, D_MODEL), F32),
                   jax.ShapeDtypeStruct((b, seq, HALF_D), jnp.uint32),
                   jax.ShapeDtypeStruct((8, b * seq), jnp.int32),
                   jax.ShapeDtypeStruct((b, seq, 8), F32),
                   jax.ShapeDtypeStruct((1, LANES), F32)],
        scratch_shapes=[pltpu.VMEM((HALO + tm, C_CONV), F32),
                        pltpu.VMEM((SUBLANES - 1, HALO + tm, C_CONV), F32),
                        pltpu.VMEM((1, LANES), F32)],
        compiler_params=pltpu.CompilerParams(dimension_semantics=("arbitrary", "arbitrary"),
                                             vmem_limit_bytes=VMEM_LIMIT),
        name="mix",
    )(x, ig, ib, u, u, pre, cw, cb, clg, clb, cog, o, aog, wo, l1g, l1b, wr, br)


def _tables_kernel(ps_ref, pe_ref, cn_ref, ri_ref, dest_ref, blk_ref):
    ri = ri_ref[...]
    start = jnp.zeros_like(ri)
    for e in range(N_EXPERTS):
        start = jnp.where(ri == e, ps_ref[e], start)
    dest_ref[...] = jnp.zeros_like(dest_ref)
    dest_ref[0:2, :] = start[0:2, :] + ri_ref[2:4, :]
    blk_start = lax.broadcasted_iota(jnp.int32, blk_ref.shape, 1) * MOE_BM
    be = jnp.zeros(blk_ref.shape, jnp.int32)
    for e in range(N_EXPERTS):
        be = be + jnp.where(pe_ref[e] <= blk_start, 1, 0)
    be = jnp.minimum(be, N_EXPERTS - 1)
    end = jnp.zeros(blk_ref.shape, jnp.int32)
    for e in range(N_EXPERTS):
        end = jnp.where(be == e, ps_ref[e] + cn_ref[e], end)
    valid = jnp.clip(end - blk_start, 0, MOE_BM)
    row = lax.broadcasted_iota(jnp.int32, blk_ref.shape, 0)
    blk_ref[...] = jnp.where(row == 0, be, valid)


def _tables_call(pad_start, pad_end, counts, ri, n_blocks):
    t = ri.shape[1]
    blk_w = -(-n_blocks // LANES) * LANES
    return pl.pallas_call(
        _tables_kernel,
        grid_spec=pltpu.PrefetchScalarGridSpec(
            num_scalar_prefetch=3,
            grid=(1,),
            in_specs=[pl.BlockSpec((SUBLANES, t), lambda i, *_: (0, 0))],
            out_specs=[pl.BlockSpec((SUBLANES, t), lambda i, *_: (0, 0)),
                       pl.BlockSpec((SUBLANES, blk_w), lambda i, *_: (0, 0))]),
        out_shape=[jax.ShapeDtypeStruct((SUBLANES, t), jnp.int32),
                   jax.ShapeDtypeStruct((SUBLANES, blk_w), jnp.int32)],
        compiler_params=pltpu.CompilerParams(dimension_semantics=("arbitrary",),
                                             vmem_limit_bytes=VMEM_LIMIT),
        name="tables",
    )(pad_start, pad_end, counts, ri)


def _dispatch_kernel(nv_ref, d0_ref, d1_ref, h_ref, xb_ref, zero_sc, sem):
    tm = DISP_TM
    n_blocks = nv_ref.shape[0]

    @pl.when(pl.program_id(0) == 0)
    def _():
        zero_sc[...] = jnp.zeros_like(zero_sc)

        def zero_copy(i):
            return pltpu.make_async_copy(zero_sc, xb_ref.at[pl.ds(i * MOE_BM, MOE_BM)], sem)

        def start(i, carry):
            @pl.when(nv_ref[i] < MOE_BM)
            def _():
                zero_copy(i).start()
            return carry

        def wait(i, carry):
            @pl.when(nv_ref[i] < MOE_BM)
            def _():
                zero_copy(i).wait()
            return carry

        lax.fori_loop(0, n_blocks, start, 0)
        lax.fori_loop(0, n_blocks, wait, 0)

    def issue(t, carry):
        src = h_ref.at[pl.ds(t, 1)]
        pltpu.make_async_copy(src, xb_ref.at[pl.ds(d0_ref[t], 1)], sem).start(priority=0)
        pltpu.make_async_copy(src, xb_ref.at[pl.ds(d1_ref[t], 1)], sem).start(priority=1)
        return carry

    lax.fori_loop(0, tm, issue, 0, unroll=8)
    for _ in range(2):
        pltpu.make_async_copy(h_ref, xb_ref.at[pl.ds(0, tm)], sem).wait()


def _dispatch_call(blk_valid, dest0, dest1, h1):
    t, w = h1.shape
    tm = DISP_TM
    return pl.pallas_call(
        _dispatch_kernel,
        grid_spec=pltpu.PrefetchScalarGridSpec(
            num_scalar_prefetch=1,
            grid=(t // tm,),
            in_specs=[pl.BlockSpec((tm,), lambda i, nv: (i,), memory_space=pltpu.SMEM),
                      pl.BlockSpec((tm,), lambda i, nv: (i,), memory_space=pltpu.SMEM),
                      pl.BlockSpec((tm, w), lambda i, nv: (i, 0))],
            out_specs=pl.BlockSpec(memory_space=pl.ANY),
            scratch_shapes=[pltpu.VMEM((MOE_BM, w), h1.dtype), pltpu.SemaphoreType.DMA]),
        out_shape=jax.ShapeDtypeStruct((blk_valid.shape[0] * MOE_BM, w), h1.dtype),
        compiler_params=pltpu.CompilerParams(dimension_semantics=("arbitrary",),
                                             vmem_limit_bytes=VMEM_LIMIT),
        name="dispatch",
    )(blk_valid, dest0, dest1, h1)


def _expert_kernel(be_ref, nv_ref, x_ref, wg_ref, wu_ref, wd_ref, y_ref, wg_sc, wu_sc, wd_sc):
    i = pl.program_id(0)
    nv = nv_ref[i]

    @pl.when((i == 0) | (be_ref[i] != be_ref[jnp.maximum(i - 1, 0)]))
    def _():
        wg_sc[...] = wg_ref[0, 0].astype(BF16)
        wu_sc[...] = wu_ref[0, 0].astype(BF16)
        wd_sc[...] = wd_ref[0, 0].astype(BF16)

    @pl.when(nv > 0)
    def _():
        x_lo, x_hi = _unpack_bf16_pair(x_ref[...])
        x_lo = x_lo.astype(BF16)
        x_hi = x_hi.astype(BF16)
        g = (jnp.dot(x_lo, wg_sc[:HALF_D, :], preferred_element_type=F32)
             + jnp.dot(x_hi, wg_sc[HALF_D:, :], preferred_element_type=F32))
        up = (jnp.dot(x_lo, wu_sc[:HALF_D, :], preferred_element_type=F32)
              + jnp.dot(x_hi, wu_sc[HALF_D:, :], preferred_element_type=F32))
        act = g * _sigmoid(g) * up
        y = jnp.dot(act.astype(BF16), wd_sc[...], preferred_element_type=F32)
        y_ref[...] = _pack_bf16_pair(y[:, :HALF_D], y[:, HALF_D:])

    @pl.when(nv == 0)
    def _():
        y_ref[...] = jnp.zeros_like(y_ref)


def _expert_call(blk_e, blk_valid, xb, w_gate, w_up, w_down):
    r = xb.shape[0]
    bm = MOE_BM
    wspec = lambda a, b: pl.BlockSpec((1, 1, a, b), lambda i, be, nv: (0, be[i], 0, 0))
    return pl.pallas_call(
        _expert_kernel,
        grid_spec=pltpu.PrefetchScalarGridSpec(
            num_scalar_prefetch=2,
            grid=(r // bm,),
            in_specs=[pl.BlockSpec((bm, HALF_D), lambda i, be, nv: (i, 0)),
                      wspec(D_MODEL, D_EXPERT), wspec(D_MODEL, D_EXPERT), wspec(D_EXPERT, D_MODEL)],
            out_specs=pl.BlockSpec((bm, HALF_D), lambda i, be, nv: (i, 0)),
            scratch_shapes=[pltpu.VMEM((D_MODEL, D_EXPERT), BF16), pltpu.VMEM((D_MODEL, D_EXPERT), BF16),
                            pltpu.VMEM((D_EXPERT, D_MODEL), BF16)]),
        out_shape=jax.ShapeDtypeStruct((r, HALF_D), jnp.uint32),
        compiler_params=pltpu.CompilerParams(dimension_semantics=("arbitrary",),
                                             vmem_limit_bytes=VMEM_LIMIT),
        name="experts",
    )(blk_e, blk_valid, xb, w_gate, w_up, w_down)


def _combine_kernel(d0_ref, d1_ref, n0_ref, n1_ref, h_ref, w_ref, g_ref, b_ref, yb_ref, o_ref, y_sc, sem):
    tm = COMB_TM
    i = pl.program_id(0)
    slot = i % 2

    def gather(dst0_ref, dst1_ref, s):
        def issue(t, carry):
            pltpu.make_async_copy(yb_ref.at[pl.ds(dst0_ref[t], 1)], y_sc.at[s, 0, pl.ds(t, 1)],
                                  sem.at[s]).start(priority=0)
            pltpu.make_async_copy(yb_ref.at[pl.ds(dst1_ref[t], 1)], y_sc.at[s, 1, pl.ds(t, 1)],
                                  sem.at[s]).start(priority=1)
            return carry
        lax.fori_loop(0, tm, issue, 0, unroll=8)

    @pl.when(i == 0)
    def _():
        gather(d0_ref, d1_ref, 0)

    @pl.when(i + 1 < pl.num_programs(0))
    def _():
        gather(n0_ref, n1_ref, 1 - slot)

    for kk in range(2):
        pltpu.make_async_copy(yb_ref.at[pl.ds(0, tm)], y_sc.at[slot, kk], sem.at[slot]).wait()
    w = w_ref[...]
    y0_lo, y0_hi = _unpack_bf16_pair(y_sc[slot, 0])
    y1_lo, y1_hi = _unpack_bf16_pair(y_sc[slot, 1])
    y = jnp.concatenate([w[:, 0:1] * y0_lo + w[:, 1:2] * y1_lo, w[:, 0:1] * y0_hi + w[:, 1:2] * y1_hi], axis=1)
    o_ref[...] = _ln(DEEPNORM_ALPHA * h_ref[...] + y, g_ref[...], b_ref[...])


def _combine_call(dest0, dest1, h1, rw, g, b, yb):
    t = h1.shape[0]
    tm = COMB_TM
    n = t // tm
    cur = pl.BlockSpec((tm,), lambda i: (i,), memory_space=pltpu.SMEM)
    nxt = pl.BlockSpec((tm,), lambda i: (jnp.minimum(i + 1, n - 1),), memory_space=pltpu.SMEM)
    return pl.pallas_call(
        _combine_kernel,
        grid=(n,),
        in_specs=[cur, cur, nxt, nxt,
                  pl.BlockSpec((tm, D_MODEL), lambda i: (i, 0)),
                  pl.BlockSpec((tm, 8), lambda i: (i, 0)),
                  pl.BlockSpec((1, D_MODEL), lambda i: (0, 0)),
                  pl.BlockSpec((1, D_MODEL), lambda i: (0, 0)),
                  pl.BlockSpec(memory_space=pl.ANY)],
        out_specs=pl.BlockSpec((tm, D_MODEL), lambda i: (i, 0)),
        out_shape=jax.ShapeDtypeStruct((t, D_MODEL), F32),
        scratch_shapes=[pltpu.VMEM((2, 2, tm, HALF_D), jnp.uint32), pltpu.SemaphoreType.DMA((2,))],
        compiler_params=pltpu.CompilerParams(dimension_semantics=("arbitrary",),
                                             vmem_limit_bytes=VMEM_LIMIT),
        name="combine",
    )(dest0, dest1, dest0, dest1, h1, rw, g, b, yb)


def _take_cols(w, idx):
    wz = jnp.concatenate([w, jnp.zeros((w.shape[0], 1), w.dtype)], axis=1)
    return wz[:, np.where(idx < 0, w.shape[1], idx)]


def _layout_indices():
    lane_src = np.full((HEAD_PAD,), -1, np.int64)
    lane_src[0:ROPE_HALF] = QK_NOPE + np.arange(ROPE_HALF)
    lane_src[ROPE_HALF:X2_LANE] = np.arange(X2_LANE - ROPE_HALF)
    lane_src[X2_LANE:X2_LANE + ROPE_HALF] = QK_NOPE + ROPE_HALF + np.arange(ROPE_HALF)
    n_lo = X2_LANE - ROPE_HALF
    lane_src[X2_LANE + ROPE_HALF:X2_LANE + ROPE_HALF + QK_NOPE - n_lo] = n_lo + np.arange(QK_NOPE - n_lo)
    q_idx = np.concatenate([np.where(lane_src >= 0, lane_src + (QK_NOPE + QK_ROPE) * h, -1)
                            for h in range(N_HEADS)])
    nope_src = np.where(lane_src < QK_NOPE, lane_src, -1)
    k_idx = np.concatenate([np.where(nope_src >= 0, nope_src + (QK_NOPE + V_DIM) * h, -1)
                            for h in range(N_HEADS)])
    v_idx = np.concatenate([QK_NOPE + np.arange(V_DIM) + (QK_NOPE + V_DIM) * h for h in range(N_HEADS)])
    kr_idx = np.full((LANES,), -1, np.int64)
    kr_idx[0:ROPE_HALF] = np.arange(ROPE_HALF)
    kr_idx[X2_LANE:X2_LANE + ROPE_HALF] = ROPE_HALF + np.arange(ROPE_HALF)
    return q_idx, k_idx, v_idx, kr_idx


def _rope_lane_tables(length):
    inv_freq = np.float32(ROPE_THETA) ** (-np.arange(ROPE_HALF, dtype=np.float32) / np.float32(ROPE_HALF))
    ang = np.arange(length, dtype=np.float32)[:, None] * inv_freq[None, :].astype(np.float32)
    cos, sin = np.cos(ang).astype(np.float32), np.sin(ang).astype(np.float32)
    ctab = np.ones((length, LANES), np.float32)
    stab = np.zeros((length, LANES), np.float32)
    for lo, sign in ((0, -1.0), (X2_LANE, 1.0)):
        ctab[:, lo:lo + ROPE_HALF] = cos
        stab[:, lo:lo + ROPE_HALF] = sign * sin
    return jnp.asarray(ctab), jnp.asarray(stab)


def kernel(x, meta_tokens, ln_in_g, ln_in_b, w_in, conv_w, conv_b, conv_ln_g, conv_ln_b, q_norm_g, w_uq,
           kv_norm_g, w_ukv, conv_out_g, attn_out_g, w_o, ln1_g, ln1_b, w_rg, b_rg, w_re, b_re,
           w_gate, w_up, w_down, ln2_g, ln2_b):
    assert w_in.shape[0] == 1, "single layer"
    bsz, seq, d = x.shape
    t = bsz * seq
    row = lambda a: a.reshape(1, -1)

    q_idx, k_idx, v_idx, kr_idx = _layout_indices()
    c4 = 2 * C_CONV + Q_LORA + KV_LORA
    w_in_l = jnp.concatenate([w_in[0][:, :c4], _take_cols(w_in[0][:, c4:], kr_idx)], axis=1).astype(BF16)
    wuq_l = _take_cols(w_uq[0], q_idx).astype(BF16)
    wuk_l = _take_cols(w_ukv[0], k_idx).astype(BF16)
    wvt_l = w_ukv[0][:, v_idx].T.astype(BF16)
    wo_l = w_o[0].astype(BF16)
    wr_l = jnp.concatenate([w_re[0], w_rg[0], jnp.zeros((d, LANES - N_EXPERTS - N_GROUPS), F32)], axis=1)
    br_l = jnp.concatenate([b_re[0], b_rg[0], jnp.zeros((LANES - N_EXPERTS - N_GROUPS,), F32)]).reshape(1, LANES)
    ctab, stab = _rope_lane_tables(N_META + seq)

    proj_w = (row(ln_in_g), row(ln_in_b), w_in_l, row(q_norm_g[0]), wuq_l, row(kv_norm_g[0]), wuk_l, wvt_l)
    u_m, _, k_m, vt_m = _proj_call(meta_tokens.astype(F32), N_META, 1, *proj_w, ctab[:N_META], stab[:N_META])
    u, q, k, vt = _proj_call(x.reshape(t, d), PROJ_TM, seq // PROJ_TM, *proj_w, ctab[N_META:], stab[N_META:])

    o = _attn_call(q.reshape(bsz, seq, -1), k.reshape(bsz, seq, -1),
                   vt.reshape(bsz, seq // ATT_T, D_ATTN, ATT_T), k_m, vt_m)

    pre = jnp.concatenate([jnp.zeros((HALO - N_META, C_CONV), F32), u_m], axis=0)
    cw = jnp.concatenate([conv_w[0], jnp.zeros((HALO - CONV_K, C_CONV), F32)], axis=0)
    h1, hp, ri, rw, cnt = _mix_call(x, row(ln_in_g), row(ln_in_b), u.reshape(bsz, seq, C_CONV), pre, cw,
                                    row(conv_b[0]), row(conv_ln_g[0]), row(conv_ln_b[0]), row(conv_out_g[0]),
                                    o, row(attn_out_g[0]), wo_l, row(ln1_g[0]), row(ln1_b[0]), wr_l, br_l)

    counts = cnt[0, :N_EXPERTS].astype(jnp.int32)
    padded = (counts + MOE_BM - 1) // MOE_BM * MOE_BM
    pad_end = jnp.cumsum(padded)
    pad_start = pad_end - padded
    n_blocks = (2 * t) // MOE_BM + N_EXPERTS
    dest, blk = _tables_call(pad_start, pad_end, counts, ri, n_blocks)
    blk_e, blk_valid = blk[0, :n_blocks], blk[1, :n_blocks]

    h1f = h1.reshape(t, d)
    dest0, dest1 = dest[0], dest[1]
    xb = _dispatch_call(blk_valid, dest0, dest1, hp.reshape(t, HALF_D))
    yb = _expert_call(blk_e, blk_valid, xb, w_gate, w_up, w_down)
    out = _combine_call(dest0, dest1, h1f, rw.reshape(t, 8), row(ln2_g[0]), row(ln2_b[0]), yb)
    return out.reshape(bsz, seq, d)
```

```python
import math

import numpy as np
import jax
import jax.numpy as jnp
from jax import lax
from jax.experimental import pallas as pl
from jax.experimental.pallas import tpu as pltpu

F32 = jnp.float32
BF16 = jnp.bfloat16

D_MODEL = 1024
HALF_D = D_MODEL // 2
N_META = 16
CHUNK = 64
C_CONV = 512
CONV_K = 31
N_HEADS = 8
QK_NOPE = 64
QK_ROPE = 32
V_DIM = 64
Q_LORA = 384
KV_LORA = 256
ROPE_THETA = 10000.0
D_ATTN = N_HEADS * V_DIM
N_GROUPS = 4
EXPERTS_PER_GROUP = 8
N_EXPERTS = N_GROUPS * EXPERTS_PER_GROUP
D_EXPERT = 256
DEEPNORM_ALPHA = 2.0 ** 0.25
EPS = 1e-5
NEG_INF = -1e30

LANES = 128
SUBLANES = 8
HEAD_PAD = LANES
ROPE_HALF = QK_ROPE // 2
X2_LANE = 64
Z_COLS = 2 * C_CONV + Q_LORA + KV_LORA + LANES
Q_SCALE = (QK_NOPE + QK_ROPE) ** -0.5 * math.log2(math.e)

PROJ_TM = 512
ATT_T = 256
ATT_HEADS = 8
MIX_TM = 256
ROUTE_ROWS = 48
HALO = 32
MOE_BM = 256
DISP_TM = 1024
COMB_TM = 1024
VMEM_LIMIT = 56 * 1024 * 1024


def _ln(x, g, b):
    mu = jnp.mean(x, axis=-1, keepdims=True)
    xc = x - mu
    var = jnp.mean(xc * xc, axis=-1, keepdims=True)
    return xc * lax.rsqrt(var + EPS) * g + b


def _rms(x, g):
    return x * lax.rsqrt(jnp.mean(x * x, axis=-1, keepdims=True) + EPS) * g


def _sigmoid(x):
    return 1.0 / (1.0 + jnp.exp(-x))


def _pack_bf16_pair(lo, hi):
    lo_bits = lax.bitcast_convert_type(lo.astype(BF16).astype(F32), jnp.uint32)
    hi_bits = lax.bitcast_convert_type(hi.astype(BF16).astype(F32), jnp.uint32)
    return (lo_bits >> 16) | (hi_bits & jnp.uint32(0xFFFF0000))


def _unpack_bf16_pair(packed):
    lo = lax.bitcast_convert_type(packed << 16, F32)
    hi = lax.bitcast_convert_type(packed & jnp.uint32(0xFFFF0000), F32)
    return lo, hi


def _proj_kernel(x_ref, g_ref, b_ref, win_ref, qg_ref, wuq_ref, kvg_ref, wuk_ref, wvt_ref, c_ref, s_ref,
                 u_ref, q_ref, k_ref, vt_ref):
    h = _ln(x_ref[...], g_ref[...], b_ref[...])
    z = jnp.dot(h.astype(BF16), win_ref[...], preferred_element_type=F32)
    u_ref[...] = z[:, :C_CONV] * _sigmoid(z[:, C_CONV:2 * C_CONV])
    c0 = 2 * C_CONV
    cq = z[:, c0:c0 + Q_LORA]
    ckv = z[:, c0 + Q_LORA:c0 + Q_LORA + KV_LORA]
    kr = z[:, c0 + Q_LORA + KV_LORA:]
    cs = c_ref[...]
    sn = s_ref[...]
    q = jnp.dot(_rms(cq, qg_ref[...]).astype(BF16), wuq_ref[...], preferred_element_type=F32)
    kvn = _rms(ckv, kvg_ref[...]).astype(BF16)
    kn = jnp.dot(kvn, wuk_ref[...], preferred_element_type=F32)
    kr_rot = kr * cs + pltpu.roll(kr, X2_LANE, 1) * sn
    for hh in range(N_HEADS):
        sl = slice(HEAD_PAD * hh, HEAD_PAD * (hh + 1))
        qh = q[:, sl]
        qh = (qh * cs + pltpu.roll(qh, X2_LANE, 1) * sn) * Q_SCALE
        q_ref[:, sl] = qh.astype(BF16)
        k_ref[:, sl] = (kn[:, sl] + kr_rot).astype(BF16)
    vt = lax.dot_general(wvt_ref[...], kvn, (((1,), (1,)), ((), ())), preferred_element_type=F32)
    n_sub, sub_w = vt_ref.shape[1], vt_ref.shape[3]
    for tt in range(n_sub):
        vt_ref[0, tt] = vt[:, tt * sub_w:(tt + 1) * sub_w].astype(BF16)


def _proj_call(x2d, tm, pos_blocks, ln_g, ln_b, w_in, qg, wuq, kvg, wuk, wvt, ctab, stab):
    t = x2d.shape[0]
    sub_w = min(tm, ATT_T)
    n_sub = tm // sub_w
    full = lambda shape: pl.BlockSpec(shape, lambda i: (0,) * len(shape))
    row = lambda w: pl.BlockSpec((tm, w), lambda i: (i, 0))
    tab = pl.BlockSpec((tm, LANES), lambda i: (i % pos_blocks, 0))
    return pl.pallas_call(
        _proj_kernel,
        grid=(t // tm,),
        in_specs=[row(D_MODEL), full((1, D_MODEL)), full((1, D_MODEL)), full((D_MODEL, Z_COLS)),
                  full((1, Q_LORA)), full((Q_LORA, N_HEADS * HEAD_PAD)),
                  full((1, KV_LORA)), full((KV_LORA, N_HEADS * HEAD_PAD)), full((D_ATTN, KV_LORA)), tab, tab],
        out_specs=[row(C_CONV), row(N_HEADS * HEAD_PAD), row(N_HEADS * HEAD_PAD),
                   pl.BlockSpec((1, n_sub, D_ATTN, sub_w), lambda i: (i, 0, 0, 0))],
        out_shape=[jax.ShapeDtypeStruct((t, C_CONV), F32),
                   jax.ShapeDtypeStruct((t, N_HEADS * HEAD_PAD), BF16),
                   jax.ShapeDtypeStruct((t, N_HEADS * HEAD_PAD), BF16),
                   jax.ShapeDtypeStruct((t // tm, n_sub, D_ATTN, sub_w), BF16)],
        compiler_params=pltpu.CompilerParams(dimension_semantics=("arbitrary",),
                                             vmem_limit_bytes=VMEM_LIMIT),
        name="proj",
    )(x2d, ln_g, ln_b, w_in, qg, wuq, kvg, wuk, wvt, ctab, stab)


def _attn_kernel(q_ref, k_ref, vt_ref, km_ref, vtm_ref, o_ref, s_sc):
    seq = q_ref.shape[1]
    nq = seq // ATT_T
    key_chunk = lax.broadcasted_iota(jnp.int32, (ATT_T, ATT_T), 0) // CHUNK
    qry_chunk = lax.broadcasted_iota(jnp.int32, (ATT_T, ATT_T), 1) // CHUNK
    diag_visible = key_chunk <= qry_chunk
    nt = (((1,), (1,)), ((), ()))
    heads = range(ATT_HEADS)
    hsl = [slice(HEAD_PAD * hh, HEAD_PAD * (hh + 1)) for hh in heads]
    vsl = [slice(V_DIM * hh, V_DIM * (hh + 1)) for hh in heads]

    def update(s, vt, m, l, acc):
        m_new = jnp.maximum(m, jnp.max(s, axis=0, keepdims=True))
        alpha = jnp.exp2(m - m_new)
        p = jnp.exp2(s - m_new)
        l = alpha * l + jnp.sum(p, axis=0, keepdims=True)
        acc = alpha * acc + jnp.dot(vt, p.astype(BF16), preferred_element_type=F32)
        return m_new, l, acc

    def q_body(qi, carry):
        row0 = pl.multiple_of(qi * ATT_T, ATT_T)
        qs = [q_ref[0, pl.ds(row0, ATT_T), hsl[hh]] for hh in heads]
        state = []
        for hh in heads:
            s = lax.dot_general(km_ref[:, hsl[hh]], qs[hh], nt, preferred_element_type=F32)
            m = jnp.max(s, axis=0, keepdims=True)
            p = jnp.exp2(s - m)
            l = jnp.sum(p, axis=0, keepdims=True)
            acc = jnp.dot(vtm_ref[0, 0, vsl[hh], :], p.astype(BF16), preferred_element_type=F32)
            state += [m, l, acc]

        def scores(j, hh):
            k0 = pl.multiple_of(j * ATT_T, ATT_T)
            return lax.dot_general(k_ref[0, pl.ds(k0, ATT_T), hsl[hh]], qs[hh], nt, preferred_element_type=F32)

        for hh in heads:
            s_sc[hh] = scores(0, hh)

        def kv_body(j, st):
            out = []
            for hh in heads:
                s_cur = s_sc[hh]
                s_next = scores(j + 1, hh)
                out += update(s_cur, vt_ref[0, j, vsl[hh], :], *st[3 * hh:3 * hh + 3])
                s_sc[hh] = s_next
            return tuple(out)

        state = lax.fori_loop(0, qi, kv_body, tuple(state))
        outs = []
        for hh in heads:
            s = jnp.where(diag_visible, s_sc[hh], NEG_INF)
            m, l, acc = update(s, vt_ref[0, qi, vsl[hh], :], *state[3 * hh:3 * hh + 3])
            outs.append(acc / l)
        o_ref[0, pl.ds(row0, ATT_T), :] = jnp.concatenate(outs, axis=0).T
        return carry

    lax.fori_loop(0, nq, q_body, 0)


def _attn_call(q, k, vt, k_meta, vt_meta):
    b, seq, _ = q.shape
    nk = seq // ATT_T
    hb = ATT_HEADS
    return pl.pallas_call(
        _attn_kernel,
        grid=(b, N_HEADS // hb),
        in_specs=[pl.BlockSpec((1, seq, hb * HEAD_PAD), lambda i, j: (i, 0, j)),
                  pl.BlockSpec((1, seq, hb * HEAD_PAD), lambda i, j: (i, 0, j)),
                  pl.BlockSpec((1, nk, hb * V_DIM, ATT_T), lambda i, j: (i, 0, j, 0)),
                  pl.BlockSpec((N_META, hb * HEAD_PAD), lambda i, j: (0, j)),
                  pl.BlockSpec((1, 1, hb * V_DIM, N_META), lambda i, j: (0, 0, j, 0))],
        out_specs=pl.BlockSpec((1, seq, hb * V_DIM), lambda i, j: (i, 0, j)),
        out_shape=jax.ShapeDtypeStruct((b, seq, D_ATTN), F32),
        scratch_shapes=[pltpu.VMEM((hb, ATT_T, ATT_T), F32)],
        compiler_params=pltpu.CompilerParams(dimension_semantics=("arbitrary", "arbitrary"),
                                             vmem_limit_bytes=VMEM_LIMIT),
        name="attn",
    )(q, k, vt, k_meta, vt_meta)


def _mix_kernel(x_ref, ig_ref, ib_ref, u_ref, uh_ref, pre_ref, cw_ref, cb_ref, clg_ref, clb_ref,
                cog_ref, o_ref, aog_ref, wo_ref, l1g_ref, l1b_ref, wrh_ref, wrl_ref, br_ref,
                h1_ref, hp_ref, ri_ref, rw_ref, cnt_ref, win_sc, sh_sc, carry_sc):
    bi = pl.program_id(0)
    ti = pl.program_id(1)
    tm = MIX_TM

    @pl.when((bi == 0) & (ti == 0))
    def _():
        carry_sc[...] = jnp.zeros_like(carry_sc)

    win_sc[0:HALO, :] = jnp.where(ti == 0, pre_ref[...], uh_ref[0])
    win_sc[HALO:, :] = u_ref[0]
    first = HALO - (CONV_K - 1)
    acc = jnp.broadcast_to(cb_ref[...], (tm, C_CONV))
    for res in range(SUBLANES):
        offs = [o for o in range(first, first + CONV_K) if o % SUBLANES == res]
        rows = tm + offs[-1] - res
        if res:
            sh_sc[res - 1, 0:rows, :] = win_sc[pl.ds(res, rows), :]
        for o in offs:
            src = sh_sc[res - 1, o - res:o - res + tm, :] if res else win_sc[o:o + tm, :]
            acc = acc + src * cw_ref[o - first:o - first + 1, :]
    c = _ln(acc, clg_ref[...], clb_ref[...])
    c = c * _sigmoid(c)
    cn = _rms(c, cog_ref[...])
    on = _rms(o_ref[0], aog_ref[...])
    y = (jnp.dot(cn.astype(BF16), wo_ref[0:C_CONV, :], preferred_element_type=F32)
         + jnp.dot(on.astype(BF16), wo_ref[C_CONV:, :], preferred_element_type=F32))
    h = _ln(x_ref[0], ig_ref[...], ib_ref[...])
    h1 = _ln(DEEPNORM_ALPHA * h + y, l1g_ref[...], l1b_ref[...])
    h1_ref[0] = h1
    hp_ref[0] = _pack_bf16_pair(h1[:, :HALF_D], h1[:, HALF_D:])

    nt = (((1,), (1,)), ((), ()))
    h_hi = h1.astype(BF16)
    h_lo = (h1 - h_hi.astype(F32)).astype(BF16)
    lt = (lax.dot_general(wrh_ref[...], h_hi, nt, preferred_element_type=F32)
          + lax.dot_general(wrl_ref[...], h_hi, nt, preferred_element_type=F32)
          + lax.dot_general(wrh_ref[...], h_lo, nt, preferred_element_type=F32)) + br_ref[:, 0:1]
    row = lax.broadcasted_iota(jnp.int32, (ROUTE_ROWS, tm), 0)
    big = jnp.int32(1 << 20)
    is_g = (row >= N_EXPERTS) & (row < N_EXPERTS + N_GROUPS)
    gl = jnp.where(is_g, lt, -jnp.inf)
    gmax = jnp.max(gl, axis=0, keepdims=True)
    g_idx = jnp.min(jnp.where(gl == gmax, row - N_EXPERTS, big), axis=0, keepdims=True)
    g_w = 1.0 / jnp.sum(jnp.exp(gl - gmax), axis=0, keepdims=True)
    in_grp = (row < N_EXPERTS) & ((row // EXPERTS_PER_GROUP) == g_idx)
    el = jnp.where(in_grp, lt, -jnp.inf)
    v1 = jnp.max(el, axis=0, keepdims=True)
    i1 = jnp.min(jnp.where(el == v1, row, big), axis=0, keepdims=True)
    el2 = jnp.where(row == i1, -jnp.inf, el)
    v2 = jnp.max(el2, axis=0, keepdims=True)
    i2 = jnp.min(jnp.where(el2 == v2, row, big), axis=0, keepdims=True)
    e2 = jnp.exp(v2 - v1)
    w1 = g_w / (1.0 + e2)
    w2 = g_w * e2 / (1.0 + e2)

    oh1 = row == i1
    oh2 = row == i2
    oh = jnp.where(oh1 | oh2, 1.0, 0.0)
    earlier = (lax.broadcasted_iota(jnp.int32, (tm, tm), 0) < lax.broadcasted_iota(jnp.int32, (tm, tm), 1))
    before = (jnp.dot(oh.astype(BF16), jnp.where(earlier, 1.0, 0.0).astype(BF16), preferred_element_type=F32)
              + carry_sc[:, 0:1])
    r1 = jnp.sum(jnp.where(oh1, before, 0.0), axis=0, keepdims=True).astype(jnp.int32)
    r2 = jnp.sum(jnp.where(oh2, before, 0.0), axis=0, keepdims=True).astype(jnp.int32)
    carry_sc[...] = carry_sc[...] + jnp.sum(oh, axis=1, keepdims=True)
    cnt_ref[...] = carry_sc[...]

    sub = lax.broadcasted_iota(jnp.int32, (SUBLANES, tm), 0)
    ri_ref[...] = jnp.where(sub == 0, i1, jnp.where(sub == 1, i2, jnp.where(sub == 2, r1,
                            jnp.where(sub == 3, r2, 0))))
    lrow = lax.broadcasted_iota(jnp.int32, (LANES, tm), 0)
    rw_ref[0] = jnp.where(lrow == 0, w1, jnp.where(lrow == 1, w2, 0.0)).T[:, :8]


def _mix_call(x, ig, ib, u, pre, cw, cb, clg, clb, cog, o, aog, wo, l1g, l1b, wrh, wrl, br):
    b, seq, _ = x.shape
    tm = MIX_TM
    nt = seq // tm
    full = lambda shape: pl.BlockSpec(shape, lambda i, j: (0,) * len(shape))
    tile = lambda w: pl.BlockSpec((1, tm, w), lambda i, j: (i, j, 0))
    halo = pl.BlockSpec((1, HALO, C_CONV), lambda i, j: (i, jnp.maximum(j * (tm // HALO) - 1, 0), 0))
    return pl.pallas_call(
        _mix_kernel,
        grid=(b, nt),
        in_specs=[tile(D_MODEL), full((1, D_MODEL)), full((1, D_MODEL)),
                  tile(C_CONV), halo, full((HALO, C_CONV)), full((HALO, C_CONV)), full((1, C_CONV)),
                  full((1, C_CONV)), full((1, C_CONV)), full((1, C_CONV)),
                  tile(D_ATTN), full((1, D_ATTN)), full((D_MODEL, D_MODEL)),
                  full((1, D_MODEL)), full((1, D_MODEL)), full((ROUTE_ROWS, D_MODEL)),
                  full((ROUTE_ROWS, D_MODEL)), full((ROUTE_ROWS, LANES))],
        out_specs=[tile(D_MODEL), tile(HALF_D), pl.BlockSpec((8, tm), lambda i, j: (0, i * nt + j)), tile(8),
                   full((ROUTE_ROWS, LANES))],
        out_shape=[jax.ShapeDtypeStruct((b, seq, D_MODEL), F32),
                   jax.ShapeDtypeStruct((b, seq, HALF_D), jnp.uint32),
                   jax.ShapeDtypeStruct((8, b * seq), jnp.int32),
                   jax.ShapeDtypeStruct((b, seq, 8), F32),
                   jax.ShapeDtypeStruct((ROUTE_ROWS, LANES), F32)],
        scratch_shapes=[pltpu.VMEM((HALO + tm, C_CONV), F32),
                        pltpu.VMEM((SUBLANES - 1, HALO + tm, C_CONV), F32),
                        pltpu.VMEM((ROUTE_ROWS, LANES), F32)],
        compiler_params=pltpu.CompilerParams(dimension_semantics=("arbitrary", "arbitrary"),
                                             vmem_limit_bytes=VMEM_LIMIT),
        name="mix",
    )(x, ig, ib, u, u, pre, cw, cb, clg, clb, cog, o, aog, wo, l1g, l1b, wrh, wrl, br)


def _tables_kernel(ps_ref, pe_ref, cn_ref, ri_ref, dest_ref, blk_ref):
    ri = ri_ref[...]
    start = jnp.zeros_like(ri)
    for e in range(N_EXPERTS):
        start = jnp.where(ri == e, ps_ref[e], start)
    dest_ref[...] = jnp.zeros_like(dest_ref)
    dest_ref[0:2, :] = start[0:2, :] + ri_ref[2:4, :]
    blk_start = lax.broadcasted_iota(jnp.int32, blk_ref.shape, 1) * MOE_BM
    be = jnp.zeros(blk_ref.shape, jnp.int32)
    for e in range(N_EXPERTS):
        be = be + jnp.where(pe_ref[e] <= blk_start, 1, 0)
    be = jnp.minimum(be, N_EXPERTS - 1)
    end = jnp.zeros(blk_ref.shape, jnp.int32)
    for e in range(N_EXPERTS):
        end = jnp.where(be == e, ps_ref[e] + cn_ref[e], end)
    valid = jnp.clip(end - blk_start, 0, MOE_BM)
    row = lax.broadcasted_iota(jnp.int32, blk_ref.shape, 0)
    blk_ref[...] = jnp.where(row == 0, be, valid)


def _tables_call(pad_start, pad_end, counts, ri, n_blocks):
    t = ri.shape[1]
    blk_w = -(-n_blocks // LANES) * LANES
    return pl.pallas_call(
        _tables_kernel,
        grid_spec=pltpu.PrefetchScalarGridSpec(
            num_scalar_prefetch=3,
            grid=(1,),
            in_specs=[pl.BlockSpec((SUBLANES, t), lambda i, *_: (0, 0))],
            out_specs=[pl.BlockSpec((SUBLANES, t), lambda i, *_: (0, 0)),
                       pl.BlockSpec((SUBLANES, blk_w), lambda i, *_: (0, 0))]),
        out_shape=[jax.ShapeDtypeStruct((SUBLANES, t), jnp.int32),
                   jax.ShapeDtypeStruct((SUBLANES, blk_w), jnp.int32)],
        compiler_params=pltpu.CompilerParams(dimension_semantics=("arbitrary",),
                                             vmem_limit_bytes=VMEM_LIMIT),
        name="tables",
    )(pad_start, pad_end, counts, ri)


def _dispatch_kernel(nv_ref, d0_ref, d1_ref, h_ref, xb_ref, zero_sc, sem):
    tm = DISP_TM
    n_blocks = nv_ref.shape[0]

    @pl.when(pl.program_id(0) == 0)
    def _():
        zero_sc[...] = jnp.zeros_like(zero_sc)

        def zero_copy(i):
            return pltpu.make_async_copy(zero_sc, xb_ref.at[pl.ds(i * MOE_BM, MOE_BM)], sem)

        def start(i, carry):
            @pl.when(nv_ref[i] < MOE_BM)
            def _():
                zero_copy(i).start()
            return carry

        def wait(i, carry):
            @pl.when(nv_ref[i] < MOE_BM)
            def _():
                zero_copy(i).wait()
            return carry

        lax.fori_loop(0, n_blocks, start, 0)
        lax.fori_loop(0, n_blocks, wait, 0)

    def issue(t, carry):
        src = h_ref.at[pl.ds(t, 1)]
        pltpu.make_async_copy(src, xb_ref.at[pl.ds(d0_ref[t], 1)], sem).start(priority=0)
        pltpu.make_async_copy(src, xb_ref.at[pl.ds(d1_ref[t], 1)], sem).start(priority=1)
        return carry

    lax.fori_loop(0, tm, issue, 0, unroll=8)
    for _ in range(2):
        pltpu.make_async_copy(h_ref, xb_ref.at[pl.ds(0, tm)], sem).wait()


def _dispatch_call(blk_valid, dest0, dest1, h1):
    t, w = h1.shape
    tm = DISP_TM
    return pl.pallas_call(
        _dispatch_kernel,
        grid_spec=pltpu.PrefetchScalarGridSpec(
            num_scalar_prefetch=1,
            grid=(t // tm,),
            in_specs=[pl.BlockSpec((tm,), lambda i, nv: (i,), memory_space=pltpu.SMEM),
                      pl.BlockSpec((tm,), lambda i, nv: (i,), memory_space=pltpu.SMEM),
                      pl.BlockSpec((tm, w), lambda i, nv: (i, 0))],
            out_specs=pl.BlockSpec(memory_space=pl.ANY),
            scratch_shapes=[pltpu.VMEM((MOE_BM, w), h1.dtype), pltpu.SemaphoreType.DMA]),
        out_shape=jax.ShapeDtypeStruct((blk_valid.shape[0] * MOE_BM, w), h1.dtype),
        compiler_params=pltpu.CompilerParams(dimension_semantics=("arbitrary",),
                                             vmem_limit_bytes=VMEM_LIMIT),
        name="dispatch",
    )(blk_valid, dest0, dest1, h1)


def _expert_kernel(be_ref, nv_ref, x_ref, wg_ref, wu_ref, wd_ref, y_ref, wg_sc, wu_sc, wd_sc):
    i = pl.program_id(0)
    nv = nv_ref[i]

    @pl.when((i == 0) | (be_ref[i] != be_ref[jnp.maximum(i - 1, 0)]))
    def _():
        wg_sc[...] = wg_ref[0, 0].astype(BF16)
        wu_sc[...] = wu_ref[0, 0].astype(BF16)
        wd_sc[...] = wd_ref[0, 0].astype(BF16)

    @pl.when(nv > 0)
    def _():
        x_lo, x_hi = _unpack_bf16_pair(x_ref[...])
        x_lo = x_lo.astype(BF16)
        x_hi = x_hi.astype(BF16)
        g = (jnp.dot(x_lo, wg_sc[:HALF_D, :], preferred_element_type=F32)
             + jnp.dot(x_hi, wg_sc[HALF_D:, :], preferred_element_type=F32))
        up = (jnp.dot(x_lo, wu_sc[:HALF_D, :], preferred_element_type=F32)
              + jnp.dot(x_hi, wu_sc[HALF_D:, :], preferred_element_type=F32))
        act = g * _sigmoid(g) * up
        y = jnp.dot(act.astype(BF16), wd_sc[...], preferred_element_type=F32)
        y_ref[...] = _pack_bf16_pair(y[:, :HALF_D], y[:, HALF_D:])

    @pl.when(nv == 0)
    def _():
        y_ref[...] = jnp.zeros_like(y_ref)


def _expert_call(blk_e, blk_valid, xb, w_gate, w_up, w_down):
    r = xb.shape[0]
    bm = MOE_BM
    wspec = lambda a, b: pl.BlockSpec((1, 1, a, b), lambda i, be, nv: (0, be[i], 0, 0))
    return pl.pallas_call(
        _expert_kernel,
        grid_spec=pltpu.PrefetchScalarGridSpec(
            num_scalar_prefetch=2,
            grid=(r // bm,),
            in_specs=[pl.BlockSpec((bm, HALF_D), lambda i, be, nv: (i, 0)),
                      wspec(D_MODEL, D_EXPERT), wspec(D_MODEL, D_EXPERT), wspec(D_EXPERT, D_MODEL)],
            out_specs=pl.BlockSpec((bm, HALF_D), lambda i, be, nv: (i, 0)),
            scratch_shapes=[pltpu.VMEM((D_MODEL, D_EXPERT), BF16), pltpu.VMEM((D_MODEL, D_EXPERT), BF16),
                            pltpu.VMEM((D_EXPERT, D_MODEL), BF16)]),
        out_shape=jax.ShapeDtypeStruct((r, HALF_D), jnp.uint32),
        compiler_params=pltpu.CompilerParams(dimension_semantics=("arbitrary",),
                                             vmem_limit_bytes=VMEM_LIMIT),
        name="experts",
    )(blk_e, blk_valid, xb, w_gate, w_up, w_down)


def _combine_kernel(d0_ref, d1_ref, n0_ref, n1_ref, h_ref, w_ref, g_ref, b_ref, yb_ref, o_ref, y_sc, sem):
    tm = COMB_TM
    i = pl.program_id(0)
    slot = i % 2

    def gather(dst0_ref, dst1_ref, s):
        def issue(t, carry):
            pltpu.make_async_copy(yb_ref.at[pl.ds(dst0_ref[t], 1)], y_sc.at[s, 0, pl.ds(t, 1)],
                                  sem.at[s]).start(priority=0)
            pltpu.make_async_copy(yb_ref.at[pl.ds(dst1_ref[t], 1)], y_sc.at[s, 1, pl.ds(t, 1)],
                                  sem.at[s]).start(priority=1)
            return carry
        lax.fori_loop(0, tm, issue, 0, unroll=8)

    @pl.when(i == 0)
    def _():
        gather(d0_ref, d1_ref, 0)

    @pl.when(i + 1 < pl.num_programs(0))
    def _():
        gather(n0_ref, n1_ref, 1 - slot)

    for kk in range(2):
        pltpu.make_async_copy(yb_ref.at[pl.ds(0, tm)], y_sc.at[slot, kk], sem.at[slot]).wait()
    w = w_ref[...]
    y0_lo, y0_hi = _unpack_bf16_pair(y_sc[slot, 0])
    y1_lo, y1_hi = _unpack_bf16_pair(y_sc[slot, 1])
    y = jnp.concatenate([w[:, 0:1] * y0_lo + w[:, 1:2] * y1_lo, w[:, 0:1] * y0_hi + w[:, 1:2] * y1_hi], axis=1)
    o_ref[...] = _ln(DEEPNORM_ALPHA * h_ref[...] + y, g_ref[...], b_ref[...])


def _combine_call(dest0, dest1, h1, rw, g, b, yb):
    t = h1.shape[0]
    tm = COMB_TM
    n = t // tm
    cur = pl.BlockSpec((tm,), lambda i: (i,), memory_space=pltpu.SMEM)
    nxt = pl.BlockSpec((tm,), lambda i: (jnp.minimum(i + 1, n - 1),), memory_space=pltpu.SMEM)
    return pl.pallas_call(
        _combine_kernel,
        grid=(n,),
        in_specs=[cur, cur, nxt, nxt,
                  pl.BlockSpec((tm, D_MODEL), lambda i: (i, 0)),
                  pl.BlockSpec((tm, 8), lambda i: (i, 0)),
                  pl.BlockSpec((1, D_MODEL), lambda i: (0, 0)),
                  pl.BlockSpec((1, D_MODEL), lambda i: (0, 0)),
                  pl.BlockSpec(memory_space=pl.ANY)],
        out_specs=pl.BlockSpec((tm, D_MODEL), lambda i: (i, 0)),
        out_shape=jax.ShapeDtypeStruct((t, D_MODEL), F32),
        scratch_shapes=[pltpu.VMEM((2, 2, tm, HALF_D), jnp.uint32), pltpu.SemaphoreType.DMA((2,))],
        compiler_params=pltpu.CompilerParams(dimension_semantics=("arbitrary",),
                                             vmem_limit_bytes=VMEM_LIMIT),
        name="combine",
    )(dest0, dest1, dest0, dest1, h1, rw, g, b, yb)


def _take_cols(w, idx):
    wz = jnp.concatenate([w, jnp.zeros((w.shape[0], 1), w.dtype)], axis=1)
    return wz[:, np.where(idx < 0, w.shape[1], idx)]


def _layout_indices():
    lane_src = np.full((HEAD_PAD,), -1, np.int64)
    lane_src[0:ROPE_HALF] = QK_NOPE + np.arange(ROPE_HALF)
    lane_src[ROPE_HALF:X2_LANE] = np.arange(X2_LANE - ROPE_HALF)
    lane_src[X2_LANE:X2_LANE + ROPE_HALF] = QK_NOPE + ROPE_HALF + np.arange(ROPE_HALF)
    n_lo = X2_LANE - ROPE_HALF
    lane_src[X2_LANE + ROPE_HALF:X2_LANE + ROPE_HALF + QK_NOPE - n_lo] = n_lo + np.arange(QK_NOPE - n_lo)
    q_idx = np.concatenate([np.where(lane_src >= 0, lane_src + (QK_NOPE + QK_ROPE) * h, -1)
                            for h in range(N_HEADS)])
    nope_src = np.where(lane_src < QK_NOPE, lane_src, -1)
    k_idx = np.concatenate([np.where(nope_src >= 0, nope_src + (QK_NOPE + V_DIM) * h, -1)
                            for h in range(N_HEADS)])
    v_idx = np.concatenate([QK_NOPE + np.arange(V_DIM) + (QK_NOPE + V_DIM) * h for h in range(N_HEADS)])
    kr_idx = np.full((LANES,), -1, np.int64)
    kr_idx[0:ROPE_HALF] = np.arange(ROPE_HALF)
    kr_idx[X2_LANE:X2_LANE + ROPE_HALF] = ROPE_HALF + np.arange(ROPE_HALF)
    return q_idx, k_idx, v_idx, kr_idx


def _rope_lane_tables(length):
    inv_freq = np.float32(ROPE_THETA) ** (-np.arange(ROPE_HALF, dtype=np.float32) / np.float32(ROPE_HALF))
    ang = np.arange(length, dtype=np.float32)[:, None] * inv_freq[None, :].astype(np.float32)
    cos, sin = np.cos(ang).astype(np.float32), np.sin(ang).astype(np.float32)
    ctab = np.ones((length, LANES), np.float32)
    stab = np.zeros((length, LANES), np.float32)
    for lo, sign in ((0, -1.0), (X2_LANE, 1.0)):
        ctab[:, lo:lo + ROPE_HALF] = cos
        stab[:, lo:lo + ROPE_HALF] = sign * sin
    return jnp.asarray(ctab), jnp.asarray(stab)


def kernel(x, meta_tokens, ln_in_g, ln_in_b, w_in, conv_w, conv_b, conv_ln_g, conv_ln_b, q_norm_g, w_uq,
           kv_norm_g, w_ukv, conv_out_g, attn_out_g, w_o, ln1_g, ln1_b, w_rg, b_rg, w_re, b_re,
           w_gate, w_up, w_down, ln2_g, ln2_b):
    assert w_in.shape[0] == 1, "single layer"
    bsz, seq, d = x.shape
    t = bsz * seq
    row = lambda a: a.reshape(1, -1)

    q_idx, k_idx, v_idx, kr_idx = _layout_indices()
    c4 = 2 * C_CONV + Q_LORA + KV_LORA
    w_in_l = jnp.concatenate([w_in[0][:, :c4], _take_cols(w_in[0][:, c4:], kr_idx)], axis=1).astype(BF16)
    wuq_l = _take_cols(w_uq[0], q_idx).astype(BF16)
    wuk_l = _take_cols(w_ukv[0], k_idx).astype(BF16)
    wvt_l = w_ukv[0][:, v_idx].T.astype(BF16)
    wo_l = w_o[0].astype(BF16)
    n_route_pad = ROUTE_ROWS - N_EXPERTS - N_GROUPS
    wr_t = jnp.concatenate([w_re[0].T, w_rg[0].T, jnp.zeros((n_route_pad, d), F32)], axis=0)
    wrh_l = wr_t.astype(BF16)
    wrl_l = (wr_t - wrh_l.astype(F32)).astype(BF16)
    br_l = jnp.broadcast_to(jnp.concatenate([b_re[0], b_rg[0], jnp.zeros((n_route_pad,), F32)])[:, None],
                            (ROUTE_ROWS, LANES))
    ctab, stab = _rope_lane_tables(N_META + seq)

    proj_w = (row(ln_in_g), row(ln_in_b), w_in_l, row(q_norm_g[0]), wuq_l, row(kv_norm_g[0]), wuk_l, wvt_l)
    u_m, _, k_m, vt_m = _proj_call(meta_tokens.astype(F32), N_META, 1, *proj_w, ctab[:N_META], stab[:N_META])
    u, q, k, vt = _proj_call(x.reshape(t, d), PROJ_TM, seq // PROJ_TM, *proj_w, ctab[N_META:], stab[N_META:])

    o = _attn_call(q.reshape(bsz, seq, -1), k.reshape(bsz, seq, -1),
                   vt.reshape(bsz, seq // ATT_T, D_ATTN, ATT_T), k_m, vt_m)

    pre = jnp.concatenate([jnp.zeros((HALO - N_META, C_CONV), F32), u_m], axis=0)
    cw = jnp.concatenate([conv_w[0], jnp.zeros((HALO - CONV_K, C_CONV), F32)], axis=0)
    h1, hp, ri, rw, cnt = _mix_call(x, row(ln_in_g), row(ln_in_b), u.reshape(bsz, seq, C_CONV), pre, cw,
                                    row(conv_b[0]), row(conv_ln_g[0]), row(conv_ln_b[0]), row(conv_out_g[0]),
                                    o, row(attn_out_g[0]), wo_l, row(ln1_g[0]), row(ln1_b[0]), wrh_l, wrl_l, br_l)

    counts = cnt[:N_EXPERTS, 0].astype(jnp.int32)
    padded = (counts + MOE_BM - 1) // MOE_BM * MOE_BM
    pad_end = jnp.cumsum(padded)
    pad_start = pad_end - padded
    n_blocks = (2 * t) // MOE_BM + N_EXPERTS
    dest, blk = _tables_call(pad_start, pad_end, counts, ri, n_blocks)
    blk_e, blk_valid = blk[0, :n_blocks], blk[1, :n_blocks]

    h1f = h1.reshape(t, d)
    dest0, dest1 = dest[0], dest[1]
    xb = _dispatch_call(blk_valid, dest0, dest1, hp.reshape(t, HALF_D))
    yb = _expert_call(blk_e, blk_valid, xb, w_gate, w_up, w_down)
    out = _combine_call(dest0, dest1, h1f, rw.reshape(t, 8), row(ln2_g[0]), row(ln2_b[0]), yb)
    return out.reshape(bsz, seq, d)
```

```python
import math

import numpy as np
import jax
import jax.numpy as jnp
from jax import lax
from jax.experimental import pallas as pl
from jax.experimental.pallas import tpu as pltpu

F32 = jnp.float32
BF16 = jnp.bfloat16

D_MODEL = 1024
HALF_D = D_MODEL // 2
N_META = 16
CHUNK = 64
C_CONV = 512
CONV_K = 31
N_HEADS = 8
QK_NOPE = 64
QK_ROPE = 32
V_DIM = 64
Q_LORA = 384
KV_LORA = 256
ROPE_THETA = 10000.0
D_ATTN = N_HEADS * V_DIM
N_GROUPS = 4
EXPERTS_PER_GROUP = 8
N_EXPERTS = N_GROUPS * EXPERTS_PER_GROUP
D_EXPERT = 256
DEEPNORM_ALPHA = 2.0 ** 0.25
EPS = 1e-5
NEG_INF = -1e30

LANES = 128
SUBLANES = 8
HEAD_PAD = LANES
ROPE_HALF = QK_ROPE // 2
X2_LANE = 64
Z_COLS = 2 * C_CONV + Q_LORA + KV_LORA + LANES
Q_SCALE = (QK_NOPE + QK_ROPE) ** -0.5 * math.log2(math.e)

PROJ_TM = 512
ATT_T = 256
ATT_HEADS = 8
MIX_TM = 256
ROUTE_ROWS = 48
HALO = 32
MOE_BM = 512
DISP_TM = 1024
COMB_TM = 1024
VMEM_LIMIT = 56 * 1024 * 1024


def _ln(x, g, b):
    mu = jnp.mean(x, axis=-1, keepdims=True)
    xc = x - mu
    var = jnp.mean(xc * xc, axis=-1, keepdims=True)
    return xc * lax.rsqrt(var + EPS) * g + b


def _rms(x, g):
    return x * lax.rsqrt(jnp.mean(x * x, axis=-1, keepdims=True) + EPS) * g


def _sigmoid(x):
    return 1.0 / (1.0 + jnp.exp(-x))


def _pack_bf16_pair(lo, hi):
    lo_bits = lax.bitcast_convert_type(lo.astype(BF16).astype(F32), jnp.uint32)
    hi_bits = lax.bitcast_convert_type(hi.astype(BF16).astype(F32), jnp.uint32)
    return (lo_bits >> 16) | (hi_bits & jnp.uint32(0xFFFF0000))


def _unpack_bf16_pair(packed):
    lo = lax.bitcast_convert_type(packed << 16, F32)
    hi = lax.bitcast_convert_type(packed & jnp.uint32(0xFFFF0000), F32)
    return lo, hi


def _proj_kernel(x_ref, g_ref, b_ref, win_ref, qg_ref, wuq_ref, kvg_ref, wuk_ref, wvt_ref, c_ref, s_ref,
                 u_ref, q_ref, k_ref, vt_ref):
    h = _ln(x_ref[...], g_ref[...], b_ref[...])
    z = jnp.dot(h.astype(BF16), win_ref[...], preferred_element_type=F32)
    u_ref[...] = z[:, :C_CONV] * _sigmoid(z[:, C_CONV:2 * C_CONV])
    c0 = 2 * C_CONV
    cq = z[:, c0:c0 + Q_LORA]
    ckv = z[:, c0 + Q_LORA:c0 + Q_LORA + KV_LORA]
    kr = z[:, c0 + Q_LORA + KV_LORA:]
    cs = c_ref[...]
    sn = s_ref[...]
    q = jnp.dot(_rms(cq, qg_ref[...]).astype(BF16), wuq_ref[...], preferred_element_type=F32)
    kvn = _rms(ckv, kvg_ref[...]).astype(BF16)
    kn = jnp.dot(kvn, wuk_ref[...], preferred_element_type=F32)
    kr_rot = kr * cs + pltpu.roll(kr, X2_LANE, 1) * sn
    for hh in range(N_HEADS):
        sl = slice(HEAD_PAD * hh, HEAD_PAD * (hh + 1))
        qh = q[:, sl]
        qh = (qh * cs + pltpu.roll(qh, X2_LANE, 1) * sn) * Q_SCALE
        q_ref[:, sl] = qh.astype(BF16)
        k_ref[:, sl] = (kn[:, sl] + kr_rot).astype(BF16)
    vt = lax.dot_general(wvt_ref[...], kvn, (((1,), (1,)), ((), ())), preferred_element_type=F32)
    n_sub, sub_w = vt_ref.shape[1], vt_ref.shape[3]
    for tt in range(n_sub):
        vt_ref[0, tt] = vt[:, tt * sub_w:(tt + 1) * sub_w].astype(BF16)


def _proj_call(x2d, tm, pos_blocks, ln_g, ln_b, w_in, qg, wuq, kvg, wuk, wvt, ctab, stab):
    t = x2d.shape[0]
    sub_w = min(tm, ATT_T)
    n_sub = tm // sub_w
    full = lambda shape: pl.BlockSpec(shape, lambda i: (0,) * len(shape))
    row = lambda w: pl.BlockSpec((tm, w), lambda i: (i, 0))
    tab = pl.BlockSpec((tm, LANES), lambda i: (i % pos_blocks, 0))
    return pl.pallas_call(
        _proj_kernel,
        grid=(t // tm,),
        in_specs=[row(D_MODEL), full((1, D_MODEL)), full((1, D_MODEL)), full((D_MODEL, Z_COLS)),
                  full((1, Q_LORA)), full((Q_LORA, N_HEADS * HEAD_PAD)),
                  full((1, KV_LORA)), full((KV_LORA, N_HEADS * HEAD_PAD)), full((D_ATTN, KV_LORA)), tab, tab],
        out_specs=[row(C_CONV), row(N_HEADS * HEAD_PAD), row(N_HEADS * HEAD_PAD),
                   pl.BlockSpec((1, n_sub, D_ATTN, sub_w), lambda i: (i, 0, 0, 0))],
        out_shape=[jax.ShapeDtypeStruct((t, C_CONV), F32),
                   jax.ShapeDtypeStruct((t, N_HEADS * HEAD_PAD), BF16),
                   jax.ShapeDtypeStruct((t, N_HEADS * HEAD_PAD), BF16),
                   jax.ShapeDtypeStruct((t // tm, n_sub, D_ATTN, sub_w), BF16)],
        compiler_params=pltpu.CompilerParams(dimension_semantics=("arbitrary",),
                                             vmem_limit_bytes=VMEM_LIMIT),
        name="proj",
    )(x2d, ln_g, ln_b, w_in, qg, wuq, kvg, wuk, wvt, ctab, stab)


def _attn_kernel(q_ref, k_ref, vt_ref, km_ref, vtm_ref, o_ref, s_sc):
    seq = q_ref.shape[1]
    nq = seq // ATT_T
    key_chunk = lax.broadcasted_iota(jnp.int32, (ATT_T, ATT_T), 0) // CHUNK
    qry_chunk = lax.broadcasted_iota(jnp.int32, (ATT_T, ATT_T), 1) // CHUNK
    diag_visible = key_chunk <= qry_chunk
    nt = (((1,), (1,)), ((), ()))
    heads = range(ATT_HEADS)
    hsl = [slice(HEAD_PAD * hh, HEAD_PAD * (hh + 1)) for hh in heads]
    vsl = [slice(V_DIM * hh, V_DIM * (hh + 1)) for hh in heads]

    def update(s, vt, m, l, acc):
        m_new = jnp.maximum(m, jnp.max(s, axis=0, keepdims=True))
        alpha = jnp.exp2(m - m_new)
        p = jnp.exp2(s - m_new)
        l = alpha * l + jnp.sum(p, axis=0, keepdims=True)
        acc = alpha * acc + jnp.dot(vt, p.astype(BF16), preferred_element_type=F32)
        return m_new, l, acc

    def q_body(qi, carry):
        row0 = pl.multiple_of(qi * ATT_T, ATT_T)
        qs = [q_ref[0, pl.ds(row0, ATT_T), hsl[hh]] for hh in heads]
        state = []
        for hh in heads:
            s = lax.dot_general(km_ref[:, hsl[hh]], qs[hh], nt, preferred_element_type=F32)
            m = jnp.max(s, axis=0, keepdims=True)
            p = jnp.exp2(s - m)
            l = jnp.sum(p, axis=0, keepdims=True)
            acc = jnp.dot(vtm_ref[0, 0, vsl[hh], :], p.astype(BF16), preferred_element_type=F32)
            state += [m, l, acc]

        def scores(j, hh):
            k0 = pl.multiple_of(j * ATT_T, ATT_T)
            return lax.dot_general(k_ref[0, pl.ds(k0, ATT_T), hsl[hh]], qs[hh], nt, preferred_element_type=F32)

        for hh in heads:
            s_sc[hh] = scores(0, hh)

        def kv_body(j, st):
            out = []
            for hh in heads:
                s_cur = s_sc[hh]
                s_next = scores(j + 1, hh)
                out += update(s_cur, vt_ref[0, j, vsl[hh], :], *st[3 * hh:3 * hh + 3])
                s_sc[hh] = s_next
            return tuple(out)

        state = lax.fori_loop(0, qi, kv_body, tuple(state))
        outs = []
        for hh in heads:
            s = jnp.where(diag_visible, s_sc[hh], NEG_INF)
            m, l, acc = update(s, vt_ref[0, qi, vsl[hh], :], *state[3 * hh:3 * hh + 3])
            outs.append(acc / l)
        o_ref[0, pl.ds(row0, ATT_T), :] = jnp.concatenate(outs, axis=0).T
        return carry

    lax.fori_loop(0, nq, q_body, 0)


def _attn_call(q, k, vt, k_meta, vt_meta):
    b, seq, _ = q.shape
    nk = seq // ATT_T
    hb = ATT_HEADS
    return pl.pallas_call(
        _attn_kernel,
        grid=(b, N_HEADS // hb),
        in_specs=[pl.BlockSpec((1, seq, hb * HEAD_PAD), lambda i, j: (i, 0, j)),
                  pl.BlockSpec((1, seq, hb * HEAD_PAD), lambda i, j: (i, 0, j)),
                  pl.BlockSpec((1, nk, hb * V_DIM, ATT_T), lambda i, j: (i, 0, j, 0)),
                  pl.BlockSpec((N_META, hb * HEAD_PAD), lambda i, j: (0, j)),
                  pl.BlockSpec((1, 1, hb * V_DIM, N_META), lambda i, j: (0, 0, j, 0))],
        out_specs=pl.BlockSpec((1, seq, hb * V_DIM), lambda i, j: (i, 0, j)),
        out_shape=jax.ShapeDtypeStruct((b, seq, D_ATTN), F32),
        scratch_shapes=[pltpu.VMEM((hb, ATT_T, ATT_T), F32)],
        compiler_params=pltpu.CompilerParams(dimension_semantics=("arbitrary", "arbitrary"),
                                             vmem_limit_bytes=VMEM_LIMIT),
        name="attn",
    )(q, k, vt, k_meta, vt_meta)


def _mix_kernel(x_ref, ig_ref, ib_ref, u_ref, uh_ref, pre_ref, cw_ref, cb_ref, clg_ref, clb_ref,
                cog_ref, o_ref, aog_ref, wo_ref, l1g_ref, l1b_ref, wrh_ref, wrl_ref, br_ref,
                h1_ref, hp_ref, ri_ref, rw_ref, cnt_ref, win_sc, sh_sc, carry_sc):
    bi = pl.program_id(0)
    ti = pl.program_id(1)
    tm = MIX_TM

    @pl.when((bi == 0) & (ti == 0))
    def _():
        carry_sc[...] = jnp.zeros_like(carry_sc)

    win_sc[0:HALO, :] = jnp.where(ti == 0, pre_ref[...], uh_ref[0])
    win_sc[HALO:, :] = u_ref[0]
    first = HALO - (CONV_K - 1)
    acc = jnp.broadcast_to(cb_ref[...], (tm, C_CONV))
    for res in range(SUBLANES):
        offs = [o for o in range(first, first + CONV_K) if o % SUBLANES == res]
        rows = tm + offs[-1] - res
        if res:
            sh_sc[res - 1, 0:rows, :] = win_sc[pl.ds(res, rows), :]
        for o in offs:
            src = sh_sc[res - 1, o - res:o - res + tm, :] if res else win_sc[o:o + tm, :]
            acc = acc + src * cw_ref[o - first:o - first + 1, :]
    c = _ln(acc, clg_ref[...], clb_ref[...])
    c = c * _sigmoid(c)
    cn = _rms(c, cog_ref[...])
    on = _rms(o_ref[0], aog_ref[...])
    y = (jnp.dot(cn.astype(BF16), wo_ref[0:C_CONV, :], preferred_element_type=F32)
         + jnp.dot(on.astype(BF16), wo_ref[C_CONV:, :], preferred_element_type=F32))
    h = _ln(x_ref[0], ig_ref[...], ib_ref[...])
    h1 = _ln(DEEPNORM_ALPHA * h + y, l1g_ref[...], l1b_ref[...])
    h1_ref[0] = h1
    hp_ref[0] = _pack_bf16_pair(h1[:, :HALF_D], h1[:, HALF_D:])

    nt = (((1,), (1,)), ((), ()))
    h_hi = h1.astype(BF16)
    h_lo = (h1 - h_hi.astype(F32)).astype(BF16)
    lt = (lax.dot_general(wrh_ref[...], h_hi, nt, preferred_element_type=F32)
          + lax.dot_general(wrl_ref[...], h_hi, nt, preferred_element_type=F32)
          + lax.dot_general(wrh_ref[...], h_lo, nt, preferred_element_type=F32)) + br_ref[:, 0:1]
    row = lax.broadcasted_iota(jnp.int32, (ROUTE_ROWS, tm), 0)
    big = jnp.int32(1 << 20)
    is_g = (row >= N_EXPERTS) & (row < N_EXPERTS + N_GROUPS)
    gl = jnp.where(is_g, lt, -jnp.inf)
    gmax = jnp.max(gl, axis=0, keepdims=True)
    g_idx = jnp.min(jnp.where(gl == gmax, row - N_EXPERTS, big), axis=0, keepdims=True)
    g_w = 1.0 / jnp.sum(jnp.exp(gl - gmax), axis=0, keepdims=True)
    in_grp = (row < N_EXPERTS) & ((row // EXPERTS_PER_GROUP) == g_idx)
    el = jnp.where(in_grp, lt, -jnp.inf)
    v1 = jnp.max(el, axis=0, keepdims=True)
    i1 = jnp.min(jnp.where(el == v1, row, big), axis=0, keepdims=True)
    el2 = jnp.where(row == i1, -jnp.inf, el)
    v2 = jnp.max(el2, axis=0, keepdims=True)
    i2 = jnp.min(jnp.where(el2 == v2, row, big), axis=0, keepdims=True)
    e2 = jnp.exp(v2 - v1)
    w1 = g_w / (1.0 + e2)
    w2 = g_w * e2 / (1.0 + e2)

    oh1 = row == i1
    oh2 = row == i2
    oh = jnp.where(oh1 | oh2, 1.0, 0.0)
    earlier = (lax.broadcasted_iota(jnp.int32, (tm, tm), 0) < lax.broadcasted_iota(jnp.int32, (tm, tm), 1))
    before = (jnp.dot(oh.astype(BF16), jnp.where(earlier, 1.0, 0.0).astype(BF16), preferred_element_type=F32)
              + carry_sc[:, 0:1])
    r1 = jnp.sum(jnp.where(oh1, before, 0.0), axis=0, keepdims=True).astype(jnp.int32)
    r2 = jnp.sum(jnp.where(oh2, before, 0.0), axis=0, keepdims=True).astype(jnp.int32)
    carry_sc[...] = carry_sc[...] + jnp.sum(oh, axis=1, keepdims=True)
    cnt_ref[...] = carry_sc[...]

    sub = lax.broadcasted_iota(jnp.int32, (SUBLANES, tm), 0)
    ri_ref[...] = jnp.where(sub == 0, i1, jnp.where(sub == 1, i2, jnp.where(sub == 2, r1,
                            jnp.where(sub == 3, r2, 0))))
    lrow = lax.broadcasted_iota(jnp.int32, (LANES, tm), 0)
    rw_ref[0] = jnp.where(lrow == 0, w1, jnp.where(lrow == 1, w2, 0.0)).T[:, :8]


def _mix_call(x, ig, ib, u, pre, cw, cb, clg, clb, cog, o, aog, wo, l1g, l1b, wrh, wrl, br):
    b, seq, _ = x.shape
    tm = MIX_TM
    nt = seq // tm
    full = lambda shape: pl.BlockSpec(shape, lambda i, j: (0,) * len(shape))
    tile = lambda w: pl.BlockSpec((1, tm, w), lambda i, j: (i, j, 0))
    halo = pl.BlockSpec((1, HALO, C_CONV), lambda i, j: (i, jnp.maximum(j * (tm // HALO) - 1, 0), 0))
    return pl.pallas_call(
        _mix_kernel,
        grid=(b, nt),
        in_specs=[tile(D_MODEL), full((1, D_MODEL)), full((1, D_MODEL)),
                  tile(C_CONV), halo, full((HALO, C_CONV)), full((HALO, C_CONV)), full((1, C_CONV)),
                  full((1, C_CONV)), full((1, C_CONV)), full((1, C_CONV)),
                  tile(D_ATTN), full((1, D_ATTN)), full((D_MODEL, D_MODEL)),
                  full((1, D_MODEL)), full((1, D_MODEL)), full((ROUTE_ROWS, D_MODEL)),
                  full((ROUTE_ROWS, D_MODEL)), full((ROUTE_ROWS, LANES))],
        out_specs=[tile(D_MODEL), tile(HALF_D), pl.BlockSpec((8, tm), lambda i, j: (0, i * nt + j)), tile(8),
                   full((ROUTE_ROWS, LANES))],
        out_shape=[jax.ShapeDtypeStruct((b, seq, D_MODEL), F32),
                   jax.ShapeDtypeStruct((b, seq, HALF_D), jnp.uint32),
                   jax.ShapeDtypeStruct((8, b * seq), jnp.int32),
                   jax.ShapeDtypeStruct((b, seq, 8), F32),
                   jax.ShapeDtypeStruct((ROUTE_ROWS, LANES), F32)],
        scratch_shapes=[pltpu.VMEM((HALO + tm, C_CONV), F32),
                        pltpu.VMEM((SUBLANES - 1, HALO + tm, C_CONV), F32),
                        pltpu.VMEM((ROUTE_ROWS, LANES), F32)],
        compiler_params=pltpu.CompilerParams(dimension_semantics=("arbitrary", "arbitrary"),
                                             vmem_limit_bytes=VMEM_LIMIT),
        name="mix",
    )(x, ig, ib, u, u, pre, cw, cb, clg, clb, cog, o, aog, wo, l1g, l1b, wrh, wrl, br)


def _tables_kernel(ps_ref, pe_ref, cn_ref, ri_ref, dest_ref, blk_ref):
    ri = ri_ref[...]
    start = jnp.zeros_like(ri)
    for e in range(N_EXPERTS):
        start = jnp.where(ri == e, ps_ref[e], start)
    dest_ref[...] = jnp.zeros_like(dest_ref)
    dest_ref[0:2, :] = start[0:2, :] + ri_ref[2:4, :]
    blk_start = lax.broadcasted_iota(jnp.int32, blk_ref.shape, 1) * MOE_BM
    be = jnp.zeros(blk_ref.shape, jnp.int32)
    for e in range(N_EXPERTS):
        be = be + jnp.where(pe_ref[e] <= blk_start, 1, 0)
    be = jnp.minimum(be, N_EXPERTS - 1)
    end = jnp.zeros(blk_ref.shape, jnp.int32)
    for e in range(N_EXPERTS):
        end = jnp.where(be == e, ps_ref[e] + cn_ref[e], end)
    valid = jnp.clip(end - blk_start, 0, MOE_BM)
    row = lax.broadcasted_iota(jnp.int32, blk_ref.shape, 0)
    blk_ref[...] = jnp.where(row == 0, be, valid)


def _tables_call(pad_start, pad_end, counts, ri, n_blocks):
    t = ri.shape[1]
    blk_w = -(-n_blocks // LANES) * LANES
    return pl.pallas_call(
        _tables_kernel,
        grid_spec=pltpu.PrefetchScalarGridSpec(
            num_scalar_prefetch=3,
            grid=(1,),
            in_specs=[pl.BlockSpec((SUBLANES, t), lambda i, *_: (0, 0))],
            out_specs=[pl.BlockSpec((SUBLANES, t), lambda i, *_: (0, 0)),
                       pl.BlockSpec((SUBLANES, blk_w), lambda i, *_: (0, 0))]),
        out_shape=[jax.ShapeDtypeStruct((SUBLANES, t), jnp.int32),
                   jax.ShapeDtypeStruct((SUBLANES, blk_w), jnp.int32)],
        compiler_params=pltpu.CompilerParams(dimension_semantics=("arbitrary",),
                                             vmem_limit_bytes=VMEM_LIMIT),
        name="tables",
    )(pad_start, pad_end, counts, ri)


def _dispatch_kernel(nv_ref, d0_ref, d1_ref, h_ref, xb_ref, zero_sc, sem):
    tm = DISP_TM
    n_blocks = nv_ref.shape[0]

    @pl.when(pl.program_id(0) == 0)
    def _():
        zero_sc[...] = jnp.zeros_like(zero_sc)

        def zero_copy(i):
            return pltpu.make_async_copy(zero_sc, xb_ref.at[pl.ds(i * MOE_BM, MOE_BM)], sem)

        def start(i, carry):
            @pl.when(nv_ref[i] < MOE_BM)
            def _():
                zero_copy(i).start()
            return carry

        def wait(i, carry):
            @pl.when(nv_ref[i] < MOE_BM)
            def _():
                zero_copy(i).wait()
            return carry

        lax.fori_loop(0, n_blocks, start, 0)
        lax.fori_loop(0, n_blocks, wait, 0)

    def issue(t, carry):
        src = h_ref.at[pl.ds(t, 1)]
        pltpu.make_async_copy(src, xb_ref.at[pl.ds(d0_ref[t], 1)], sem).start(priority=0)
        pltpu.make_async_copy(src, xb_ref.at[pl.ds(d1_ref[t], 1)], sem).start(priority=1)
        return carry

    lax.fori_loop(0, tm, issue, 0, unroll=8)
    for _ in range(2):
        pltpu.make_async_copy(h_ref, xb_ref.at[pl.ds(0, tm)], sem).wait()


def _dispatch_call(blk_valid, dest0, dest1, h1):
    t, w = h1.shape
    tm = DISP_TM
    return pl.pallas_call(
        _dispatch_kernel,
        grid_spec=pltpu.PrefetchScalarGridSpec(
            num_scalar_prefetch=1,
            grid=(t // tm,),
            in_specs=[pl.BlockSpec((tm,), lambda i, nv: (i,), memory_space=pltpu.SMEM),
                      pl.BlockSpec((tm,), lambda i, nv: (i,), memory_space=pltpu.SMEM),
                      pl.BlockSpec((tm, w), lambda i, nv: (i, 0))],
            out_specs=pl.BlockSpec(memory_space=pl.ANY),
            scratch_shapes=[pltpu.VMEM((MOE_BM, w), h1.dtype), pltpu.SemaphoreType.DMA]),
        out_shape=jax.ShapeDtypeStruct((blk_valid.shape[0] * MOE_BM, w), h1.dtype),
        compiler_params=pltpu.CompilerParams(dimension_semantics=("arbitrary",),
                                             vmem_limit_bytes=VMEM_LIMIT),
        name="dispatch",
    )(blk_valid, dest0, dest1, h1)


def _expert_kernel(be_ref, nv_ref, x_ref, wg_ref, wu_ref, wd_ref, y_ref, wg_sc, wu_sc, wd_sc):
    i = pl.program_id(0)
    nv = nv_ref[i]

    @pl.when((i == 0) | (be_ref[i] != be_ref[jnp.maximum(i - 1, 0)]))
    def _():
        wg_sc[...] = wg_ref[0, 0].astype(BF16)
        wu_sc[...] = wu_ref[0, 0].astype(BF16)
        wd_sc[...] = wd_ref[0, 0].astype(BF16)

    @pl.when(nv > 0)
    def _():
        x_lo, x_hi = _unpack_bf16_pair(x_ref[...])
        x_lo = x_lo.astype(BF16)
        x_hi = x_hi.astype(BF16)
        g = (jnp.dot(x_lo, wg_sc[:HALF_D, :], preferred_element_type=F32)
             + jnp.dot(x_hi, wg_sc[HALF_D:, :], preferred_element_type=F32))
        up = (jnp.dot(x_lo, wu_sc[:HALF_D, :], preferred_element_type=F32)
              + jnp.dot(x_hi, wu_sc[HALF_D:, :], preferred_element_type=F32))
        act = g * _sigmoid(g) * up
        y = jnp.dot(act.astype(BF16), wd_sc[...], preferred_element_type=F32)
        y_ref[...] = _pack_bf16_pair(y[:, :HALF_D], y[:, HALF_D:])

    @pl.when(nv == 0)
    def _():
        y_ref[...] = jnp.zeros_like(y_ref)


def _expert_call(blk_e, blk_valid, xb, w_gate, w_up, w_down):
    r = xb.shape[0]
    bm = MOE_BM
    wspec = lambda a, b: pl.BlockSpec((1, 1, a, b), lambda i, be, nv: (0, be[i], 0, 0))
    return pl.pallas_call(
        _expert_kernel,
        grid_spec=pltpu.PrefetchScalarGridSpec(
            num_scalar_prefetch=2,
            grid=(r // bm,),
            in_specs=[pl.BlockSpec((bm, HALF_D), lambda i, be, nv: (i, 0)),
                      wspec(D_MODEL, D_EXPERT), wspec(D_MODEL, D_EXPERT), wspec(D_EXPERT, D_MODEL)],
            out_specs=pl.BlockSpec((bm, HALF_D), lambda i, be, nv: (i, 0)),
            scratch_shapes=[pltpu.VMEM((D_MODEL, D_EXPERT), BF16), pltpu.VMEM((D_MODEL, D_EXPERT), BF16),
                            pltpu.VMEM((D_EXPERT, D_MODEL), BF16)]),
        out_shape=jax.ShapeDtypeStruct((r, HALF_D), jnp.uint32),
        compiler_params=pltpu.CompilerParams(dimension_semantics=("arbitrary",),
                                             vmem_limit_bytes=VMEM_LIMIT),
        name="experts",
    )(blk_e, blk_valid, xb, w_gate, w_up, w_down)


def _combine_kernel(d0_ref, d1_ref, n0_ref, n1_ref, h_ref, w_ref, g_ref, b_ref, yb_ref, o_ref, y_sc, sem):
    tm = COMB_TM
    i = pl.program_id(0)
    slot = i % 2

    def gather(dst0_ref, dst1_ref, s):
        def issue(t, carry):
            pltpu.make_async_copy(yb_ref.at[pl.ds(dst0_ref[t], 1)], y_sc.at[s, 0, pl.ds(t, 1)],
                                  sem.at[s]).start(priority=0)
            pltpu.make_async_copy(yb_ref.at[pl.ds(dst1_ref[t], 1)], y_sc.at[s, 1, pl.ds(t, 1)],
                                  sem.at[s]).start(priority=1)
            return carry
        lax.fori_loop(0, tm, issue, 0, unroll=8)

    @pl.when(i == 0)
    def _():
        gather(d0_ref, d1_ref, 0)

    @pl.when(i + 1 < pl.num_programs(0))
    def _():
        gather(n0_ref, n1_ref, 1 - slot)

    for kk in range(2):
        pltpu.make_async_copy(yb_ref.at[pl.ds(0, tm)], y_sc.at[slot, kk], sem.at[slot]).wait()
    w = w_ref[...]
    y0_lo, y0_hi = _unpack_bf16_pair(y_sc[slot, 0])
    y1_lo, y1_hi = _unpack_bf16_pair(y_sc[slot, 1])
    y = jnp.concatenate([w[:, 0:1] * y0_lo + w[:, 1:2] * y1_lo, w[:, 0:1] * y0_hi + w[:, 1:2] * y1_hi], axis=1)
    o_ref[...] = _ln(DEEPNORM_ALPHA * h_ref[...] + y, g_ref[...], b_ref[...])


def _combine_call(dest0, dest1, h1, rw, g, b, yb):
    t = h1.shape[0]
    tm = COMB_TM
    n = t // tm
    cur = pl.BlockSpec((tm,), lambda i: (i,), memory_space=pltpu.SMEM)
    nxt = pl.BlockSpec((tm,), lambda i: (jnp.minimum(i + 1, n - 1),), memory_space=pltpu.SMEM)
    return pl.pallas_call(
        _combine_kernel,
        grid=(n,),
        in_specs=[cur, cur, nxt, nxt,
                  pl.BlockSpec((tm, D_MODEL), lambda i: (i, 0)),
                  pl.BlockSpec((tm, 8), lambda i: (i, 0)),
                  pl.BlockSpec((1, D_MODEL), lambda i: (0, 0)),
                  pl.BlockSpec((1, D_MODEL), lambda i: (0, 0)),
                  pl.BlockSpec(memory_space=pl.ANY)],
        out_specs=pl.BlockSpec((tm, D_MODEL), lambda i: (i, 0)),
        out_shape=jax.ShapeDtypeStruct((t, D_MODEL), F32),
        scratch_shapes=[pltpu.VMEM((2, 2, tm, HALF_D), jnp.uint32), pltpu.SemaphoreType.DMA((2,))],
        compiler_params=pltpu.CompilerParams(dimension_semantics=("arbitrary",),
                                             vmem_limit_bytes=VMEM_LIMIT),
        name="combine",
    )(dest0, dest1, dest0, dest1, h1, rw, g, b, yb)


def _take_cols(w, idx):
    wz = jnp.concatenate([w, jnp.zeros((w.shape[0], 1), w.dtype)], axis=1)
    return wz[:, np.where(idx < 0, w.shape[1], idx)]


def _layout_indices():
    lane_src = np.full((HEAD_PAD,), -1, np.int64)
    lane_src[0:ROPE_HALF] = QK_NOPE + np.arange(ROPE_HALF)
    lane_src[ROPE_HALF:X2_LANE] = np.arange(X2_LANE - ROPE_HALF)
    lane_src[X2_LANE:X2_LANE + ROPE_HALF] = QK_NOPE + ROPE_HALF + np.arange(ROPE_HALF)
    n_lo = X2_LANE - ROPE_HALF
    lane_src[X2_LANE + ROPE_HALF:X2_LANE + ROPE_HALF + QK_NOPE - n_lo] = n_lo + np.arange(QK_NOPE - n_lo)
    q_idx = np.concatenate([np.where(lane_src >= 0, lane_src + (QK_NOPE + QK_ROPE) * h, -1)
                            for h in range(N_HEADS)])
    nope_src = np.where(lane_src < QK_NOPE, lane_src, -1)
    k_idx = np.concatenate([np.where(nope_src >= 0, nope_src + (QK_NOPE + V_DIM) * h, -1)
                            for h in range(N_HEADS)])
    v_idx = np.concatenate([QK_NOPE + np.arange(V_DIM) + (QK_NOPE + V_DIM) * h for h in range(N_HEADS)])
    kr_idx = np.full((LANES,), -1, np.int64)
    kr_idx[0:ROPE_HALF] = np.arange(ROPE_HALF)
    kr_idx[X2_LANE:X2_LANE + ROPE_HALF] = ROPE_HALF + np.arange(ROPE_HALF)
    return q_idx, k_idx, v_idx, kr_idx


def _rope_lane_tables(length):
    inv_freq = np.float32(ROPE_THETA) ** (-np.arange(ROPE_HALF, dtype=np.float32) / np.float32(ROPE_HALF))
    ang = np.arange(length, dtype=np.float32)[:, None] * inv_freq[None, :].astype(np.float32)
    cos, sin = np.cos(ang).astype(np.float32), np.sin(ang).astype(np.float32)
    ctab = np.ones((length, LANES), np.float32)
    stab = np.zeros((length, LANES), np.float32)
    for lo, sign in ((0, -1.0), (X2_LANE, 1.0)):
        ctab[:, lo:lo + ROPE_HALF] = cos
        stab[:, lo:lo + ROPE_HALF] = sign * sin
    return jnp.asarray(ctab), jnp.asarray(stab)


def kernel(x, meta_tokens, ln_in_g, ln_in_b, w_in, conv_w, conv_b, conv_ln_g, conv_ln_b, q_norm_g, w_uq,
           kv_norm_g, w_ukv, conv_out_g, attn_out_g, w_o, ln1_g, ln1_b, w_rg, b_rg, w_re, b_re,
           w_gate, w_up, w_down, ln2_g, ln2_b):
    assert w_in.shape[0] == 1, "single layer"
    bsz, seq, d = x.shape
    t = bsz * seq
    row = lambda a: a.reshape(1, -1)

    q_idx, k_idx, v_idx, kr_idx = _layout_indices()
    c4 = 2 * C_CONV + Q_LORA + KV_LORA
    w_in_l = jnp.concatenate([w_in[0][:, :c4], _take_cols(w_in[0][:, c4:], kr_idx)], axis=1).astype(BF16)
    wuq_l = _take_cols(w_uq[0], q_idx).astype(BF16)
    wuk_l = _take_cols(w_ukv[0], k_idx).astype(BF16)
    wvt_l = w_ukv[0][:, v_idx].T.astype(BF16)
    wo_l = w_o[0].astype(BF16)
    n_route_pad = ROUTE_ROWS - N_EXPERTS - N_GROUPS
    wr_t = jnp.concatenate([w_re[0].T, w_rg[0].T, jnp.zeros((n_route_pad, d), F32)], axis=0)
    wrh_l = wr_t.astype(BF16)
    wrl_l = (wr_t - wrh_l.astype(F32)).astype(BF16)
    br_l = jnp.broadcast_to(jnp.concatenate([b_re[0], b_rg[0], jnp.zeros((n_route_pad,), F32)])[:, None],
                            (ROUTE_ROWS, LANES))
    ctab, stab = _rope_lane_tables(N_META + seq)

    proj_w = (row(ln_in_g), row(ln_in_b), w_in_l, row(q_norm_g[0]), wuq_l, row(kv_norm_g[0]), wuk_l, wvt_l)
    u_m, _, k_m, vt_m = _proj_call(meta_tokens.astype(F32), N_META, 1, *proj_w, ctab[:N_META], stab[:N_META])
    u, q, k, vt = _proj_call(x.reshape(t, d), PROJ_TM, seq // PROJ_TM, *proj_w, ctab[N_META:], stab[N_META:])

    o = _attn_call(q.reshape(bsz, seq, -1), k.reshape(bsz, seq, -1),
                   vt.reshape(bsz, seq // ATT_T, D_ATTN, ATT_T), k_m, vt_m)

    pre = jnp.concatenate([jnp.zeros((HALO - N_META, C_CONV), F32), u_m], axis=0)
    cw = jnp.concatenate([conv_w[0], jnp.zeros((HALO - CONV_K, C_CONV), F32)], axis=0)
    h1, hp, ri, rw, cnt = _mix_call(x, row(ln_in_g), row(ln_in_b), u.reshape(bsz, seq, C_CONV), pre, cw,
                                    row(conv_b[0]), row(conv_ln_g[0]), row(conv_ln_b[0]), row(conv_out_g[0]),
                                    o, row(attn_out_g[0]), wo_l, row(ln1_g[0]), row(ln1_b[0]), wrh_l, wrl_l, br_l)

    counts = cnt[:N_EXPERTS, 0].astype(jnp.int32)
    padded = (counts + MOE_BM - 1) // MOE_BM * MOE_BM
    pad_end = jnp.cumsum(padded)
    pad_start = pad_end - padded
    n_blocks = (2 * t) // MOE_BM + N_EXPERTS
    dest, blk = _tables_call(pad_start, pad_end, counts, ri, n_blocks)
    blk_e, blk_valid = blk[0, :n_blocks], blk[1, :n_blocks]

    h1f = h1.reshape(t, d)
    dest0, dest1 = dest[0], dest[1]
    xb = _dispatch_call(blk_valid, dest0, dest1, hp.reshape(t, HALF_D))
    yb = _expert_call(blk_e, blk_valid, xb, w_gate, w_up, w_down)
    out = _combine_call(dest0, dest1, h1f, rw.reshape(t, 8), row(ln2_g[0]), row(ln2_b[0]), yb)
    return out.reshape(bsz, seq, d)
```

```python
import math

import numpy as np
import jax
import jax.numpy as jnp
from jax import lax
from jax.experimental import pallas as pl
from jax.experimental.pallas import tpu as pltpu

F32 = jnp.float32
BF16 = jnp.bfloat16

D_MODEL = 1024
HALF_D = D_MODEL // 2
N_META = 16
CHUNK = 64
C_CONV = 512
CONV_K = 31
N_HEADS = 8
QK_NOPE = 64
QK_ROPE = 32
V_DIM = 64
Q_LORA = 384
KV_LORA = 256
ROPE_THETA = 10000.0
D_ATTN = N_HEADS * V_DIM
N_GROUPS = 4
EXPERTS_PER_GROUP = 8
N_EXPERTS = N_GROUPS * EXPERTS_PER_GROUP
D_EXPERT = 256
DEEPNORM_ALPHA = 2.0 ** 0.25
EPS = 1e-5
NEG_INF = -1e30

LANES = 128
SUBLANES = 8
HEAD_PAD = LANES
ROPE_HALF = QK_ROPE // 2
X2_LANE = 64
Z_COLS = 2 * C_CONV + Q_LORA + KV_LORA + LANES
Q_SCALE = (QK_NOPE + QK_ROPE) ** -0.5 * math.log2(math.e)

PROJ_TM = 512
ATT_T = 256
ATT_HEADS = 8
MIX_TM = 256
ROUTE_ROWS = 48
HALO = 32
MOE_BM = 512
DISP_TM = 1024
COMB_TM = 1024
VMEM_LIMIT = 56 * 1024 * 1024


def _ln(x, g, b):
    mu = jnp.mean(x, axis=-1, keepdims=True)
    xc = x - mu
    var = jnp.mean(xc * xc, axis=-1, keepdims=True)
    return xc * lax.rsqrt(var + EPS) * g + b


def _rms(x, g):
    return x * lax.rsqrt(jnp.mean(x * x, axis=-1, keepdims=True) + EPS) * g


def _sigmoid(x):
    return 1.0 / (1.0 + jnp.exp(-x))


def _pack_bf16_pair(lo, hi):
    lo_bits = lax.bitcast_convert_type(lo.astype(BF16).astype(F32), jnp.uint32)
    hi_bits = lax.bitcast_convert_type(hi.astype(BF16).astype(F32), jnp.uint32)
    return (lo_bits >> 16) | (hi_bits & jnp.uint32(0xFFFF0000))


def _unpack_bf16_pair(packed):
    lo = lax.bitcast_convert_type(packed << 16, F32)
    hi = lax.bitcast_convert_type(packed & jnp.uint32(0xFFFF0000), F32)
    return lo, hi


def _proj_kernel(x_ref, g_ref, b_ref, win_ref, qg_ref, wuq_ref, kvg_ref, wuk_ref, wvt_ref, c_ref, s_ref,
                 u_ref, q_ref, k_ref, vt_ref):
    h = _ln(x_ref[...], g_ref[...], b_ref[...])
    z = jnp.dot(h.astype(BF16), win_ref[...], preferred_element_type=F32)
    u_ref[...] = z[:, :C_CONV] * _sigmoid(z[:, C_CONV:2 * C_CONV])
    c0 = 2 * C_CONV
    cq = z[:, c0:c0 + Q_LORA]
    ckv = z[:, c0 + Q_LORA:c0 + Q_LORA + KV_LORA]
    kr = z[:, c0 + Q_LORA + KV_LORA:]
    cs = c_ref[...]
    sn = s_ref[...]
    q = jnp.dot(_rms(cq, qg_ref[...]).astype(BF16), wuq_ref[...], preferred_element_type=F32)
    kvn = _rms(ckv, kvg_ref[...]).astype(BF16)
    kn = jnp.dot(kvn, wuk_ref[...], preferred_element_type=F32)
    kr_rot = kr * cs + pltpu.roll(kr, X2_LANE, 1) * sn
    for hh in range(N_HEADS):
        sl = slice(HEAD_PAD * hh, HEAD_PAD * (hh + 1))
        qh = q[:, sl]
        qh = (qh * cs + pltpu.roll(qh, X2_LANE, 1) * sn) * Q_SCALE
        q_ref[:, sl] = qh.astype(BF16)
        k_ref[:, sl] = (kn[:, sl] + kr_rot).astype(BF16)
    vt = lax.dot_general(wvt_ref[...], kvn, (((1,), (1,)), ((), ())), preferred_element_type=F32)
    n_sub, sub_w = vt_ref.shape[1], vt_ref.shape[3]
    for tt in range(n_sub):
        vt_ref[0, tt] = vt[:, tt * sub_w:(tt + 1) * sub_w].astype(BF16)


def _proj_call(x2d, tm, pos_blocks, ln_g, ln_b, w_in, qg, wuq, kvg, wuk, wvt, ctab, stab):
    t = x2d.shape[0]
    sub_w = min(tm, ATT_T)
    n_sub = tm // sub_w
    full = lambda shape: pl.BlockSpec(shape, lambda i: (0,) * len(shape))
    row = lambda w: pl.BlockSpec((tm, w), lambda i: (i, 0))
    tab = pl.BlockSpec((tm, LANES), lambda i: (i % pos_blocks, 0))
    return pl.pallas_call(
        _proj_kernel,
        grid=(t // tm,),
        in_specs=[row(D_MODEL), full((1, D_MODEL)), full((1, D_MODEL)), full((D_MODEL, Z_COLS)),
                  full((1, Q_LORA)), full((Q_LORA, N_HEADS * HEAD_PAD)),
                  full((1, KV_LORA)), full((KV_LORA, N_HEADS * HEAD_PAD)), full((D_ATTN, KV_LORA)), tab, tab],
        out_specs=[row(C_CONV), row(N_HEADS * HEAD_PAD), row(N_HEADS * HEAD_PAD),
                   pl.BlockSpec((1, n_sub, D_ATTN, sub_w), lambda i: (i, 0, 0, 0))],
        out_shape=[jax.ShapeDtypeStruct((t, C_CONV), F32),
                   jax.ShapeDtypeStruct((t, N_HEADS * HEAD_PAD), BF16),
                   jax.ShapeDtypeStruct((t, N_HEADS * HEAD_PAD), BF16),
                   jax.ShapeDtypeStruct((t // tm, n_sub, D_ATTN, sub_w), BF16)],
        compiler_params=pltpu.CompilerParams(dimension_semantics=("arbitrary",),
                                             vmem_limit_bytes=VMEM_LIMIT),
        name="proj",
    )(x2d, ln_g, ln_b, w_in, qg, wuq, kvg, wuk, wvt, ctab, stab)


def _attn_kernel(q_ref, k_ref, vt_ref, km_ref, vtm_ref, o_ref, s_sc):
    seq = q_ref.shape[1]
    nq = seq // ATT_T
    key_chunk = lax.broadcasted_iota(jnp.int32, (ATT_T, ATT_T), 0) // CHUNK
    qry_chunk = lax.broadcasted_iota(jnp.int32, (ATT_T, ATT_T), 1) // CHUNK
    diag_visible = key_chunk <= qry_chunk
    nt = (((1,), (1,)), ((), ()))
    heads = range(ATT_HEADS)
    hsl = [slice(HEAD_PAD * hh, HEAD_PAD * (hh + 1)) for hh in heads]
    vsl = [slice(V_DIM * hh, V_DIM * (hh + 1)) for hh in heads]

    def update(s, vt, m, l, acc):
        m_new = jnp.maximum(m, jnp.max(s, axis=0, keepdims=True))
        alpha = jnp.exp2(m - m_new)
        p = jnp.exp2(s - m_new)
        l = alpha * l + jnp.sum(p, axis=0, keepdims=True)
        acc = alpha * acc + jnp.dot(vt, p.astype(BF16), preferred_element_type=F32)
        return m_new, l, acc

    def scores(qi, j, hh):
        q0 = pl.multiple_of(qi * ATT_T, ATT_T)
        k0 = pl.multiple_of(j * ATT_T, ATT_T)
        return lax.dot_general(k_ref[0, pl.ds(k0, ATT_T), hsl[hh]], q_ref[0, pl.ds(q0, ATT_T), hsl[hh]], nt,
                               preferred_element_type=F32)

    for hh in heads:
        s_sc[0, hh] = scores(0, 0, hh)

    def q_body(qi, carry):
        row0 = pl.multiple_of(qi * ATT_T, ATT_T)
        slot = qi % 2
        state = []
        for hh in heads:
            state += [jnp.full((1, ATT_T), NEG_INF, F32), jnp.zeros((1, ATT_T), F32),
                      jnp.zeros((V_DIM, ATT_T), F32)]

        def kv_body(j, st):
            out = []
            for hh in heads:
                s_cur = s_sc[slot, hh]
                s_next = scores(qi, j + 1, hh)
                out += update(s_cur, vt_ref[0, j, vsl[hh], :], *st[3 * hh:3 * hh + 3])
                s_sc[slot, hh] = s_next
            return tuple(out)

        state = lax.fori_loop(0, qi, kv_body, tuple(state))
        q_next = jnp.minimum(qi + 1, nq - 1)
        sms = [lax.dot_general(km_ref[:, hsl[hh]], q_ref[0, pl.ds(row0, ATT_T), hsl[hh]], nt,
                               preferred_element_type=F32) for hh in heads]
        for hh in heads:
            s_sc[1 - slot, hh] = scores(q_next, 0, hh)
        outs = []
        for hh in heads:
            m, l, acc = state[3 * hh:3 * hh + 3]
            s = jnp.where(diag_visible, s_sc[slot, hh], NEG_INF)
            sm = sms[hh]
            m_new = jnp.maximum(m, jnp.maximum(jnp.max(s, axis=0, keepdims=True),
                                               jnp.max(sm, axis=0, keepdims=True)))
            alpha = jnp.exp2(m - m_new)
            p = jnp.exp2(s - m_new)
            pm = jnp.exp2(sm - m_new)
            l = alpha * l + jnp.sum(p, axis=0, keepdims=True) + jnp.sum(pm, axis=0, keepdims=True)
            acc = (alpha * acc + jnp.dot(vt_ref[0, qi, vsl[hh], :], p.astype(BF16), preferred_element_type=F32)
                   + jnp.dot(vtm_ref[0, 0, vsl[hh], :], pm.astype(BF16), preferred_element_type=F32))
            outs.append(acc / l)
        o_ref[0, pl.ds(row0, ATT_T), :] = jnp.concatenate(outs, axis=0).T
        return carry

    lax.fori_loop(0, nq, q_body, 0)


def _attn_call(q, k, vt, k_meta, vt_meta):
    b, seq, _ = q.shape
    nk = seq // ATT_T
    hb = ATT_HEADS
    return pl.pallas_call(
        _attn_kernel,
        grid=(b, N_HEADS // hb),
        in_specs=[pl.BlockSpec((1, seq, hb * HEAD_PAD), lambda i, j: (i, 0, j)),
                  pl.BlockSpec((1, seq, hb * HEAD_PAD), lambda i, j: (i, 0, j)),
                  pl.BlockSpec((1, nk, hb * V_DIM, ATT_T), lambda i, j: (i, 0, j, 0)),
                  pl.BlockSpec((N_META, hb * HEAD_PAD), lambda i, j: (0, j)),
                  pl.BlockSpec((1, 1, hb * V_DIM, N_META), lambda i, j: (0, 0, j, 0))],
        out_specs=pl.BlockSpec((1, seq, hb * V_DIM), lambda i, j: (i, 0, j)),
        out_shape=jax.ShapeDtypeStruct((b, seq, D_ATTN), F32),
        scratch_shapes=[pltpu.VMEM((2, hb, ATT_T, ATT_T), F32)],
        compiler_params=pltpu.CompilerParams(dimension_semantics=("arbitrary", "arbitrary"),
                                             vmem_limit_bytes=VMEM_LIMIT),
        name="attn",
    )(q, k, vt, k_meta, vt_meta)


def _mix_kernel(x_ref, ig_ref, ib_ref, u_ref, uh_ref, pre_ref, cw_ref, cb_ref, clg_ref, clb_ref,
                cog_ref, o_ref, aog_ref, wo_ref, l1g_ref, l1b_ref, wrh_ref, wrl_ref, br_ref,
                h1_ref, hp_ref, ri_ref, rw_ref, cnt_ref, win_sc, sh_sc, carry_sc):
    bi = pl.program_id(0)
    ti = pl.program_id(1)
    tm = MIX_TM

    @pl.when((bi == 0) & (ti == 0))
    def _():
        carry_sc[...] = jnp.zeros_like(carry_sc)

    win_sc[0:HALO, :] = jnp.where(ti == 0, pre_ref[...], uh_ref[0])
    win_sc[HALO:, :] = u_ref[0]
    first = HALO - (CONV_K - 1)
    acc = jnp.broadcast_to(cb_ref[...], (tm, C_CONV))
    for res in range(SUBLANES):
        offs = [o for o in range(first, first + CONV_K) if o % SUBLANES == res]
        rows = tm + offs[-1] - res
        if res:
            sh_sc[res - 1, 0:rows, :] = win_sc[pl.ds(res, rows), :]
        for o in offs:
            src = sh_sc[res - 1, o - res:o - res + tm, :] if res else win_sc[o:o + tm, :]
            acc = acc + src * cw_ref[o - first:o - first + 1, :]
    c = _ln(acc, clg_ref[...], clb_ref[...])
    c = c * _sigmoid(c)
    cn = _rms(c, cog_ref[...])
    on = _rms(o_ref[0], aog_ref[...])
    y = (jnp.dot(cn.astype(BF16), wo_ref[0:C_CONV, :], preferred_element_type=F32)
         + jnp.dot(on.astype(BF16), wo_ref[C_CONV:, :], preferred_element_type=F32))
    h = _ln(x_ref[0], ig_ref[...], ib_ref[...])
    h1 = _ln(DEEPNORM_ALPHA * h + y, l1g_ref[...], l1b_ref[...])
    h1_ref[0] = h1
    hp_ref[0] = _pack_bf16_pair(h1[:, :HALF_D], h1[:, HALF_D:])

    nt = (((1,), (1,)), ((), ()))
    h_hi = h1.astype(BF16)
    h_lo = (h1 - h_hi.astype(F32)).astype(BF16)
    lt = (lax.dot_general(wrh_ref[...], h_hi, nt, preferred_element_type=F32)
          + lax.dot_general(wrl_ref[...], h_hi, nt, preferred_element_type=F32)
          + lax.dot_general(wrh_ref[...], h_lo, nt, preferred_element_type=F32)) + br_ref[:, 0:1]
    row = lax.broadcasted_iota(jnp.int32, (ROUTE_ROWS, tm), 0)
    big = jnp.int32(1 << 20)
    is_g = (row >= N_EXPERTS) & (row < N_EXPERTS + N_GROUPS)
    gl = jnp.where(is_g, lt, -jnp.inf)
    gmax = jnp.max(gl, axis=0, keepdims=True)
    g_idx = jnp.min(jnp.where(gl == gmax, row - N_EXPERTS, big), axis=0, keepdims=True)
    g_w = 1.0 / jnp.sum(jnp.exp(gl - gmax), axis=0, keepdims=True)
    in_grp = (row < N_EXPERTS) & ((row // EXPERTS_PER_GROUP) == g_idx)
    el = jnp.where(in_grp, lt, -jnp.inf)
    v1 = jnp.max(el, axis=0, keepdims=True)
    i1 = jnp.min(jnp.where(el == v1, row, big), axis=0, keepdims=True)
    el2 = jnp.where(row == i1, -jnp.inf, el)
    v2 = jnp.max(el2, axis=0, keepdims=True)
    i2 = jnp.min(jnp.where(el2 == v2, row, big), axis=0, keepdims=True)
    e2 = jnp.exp(v2 - v1)
    w1 = g_w / (1.0 + e2)
    w2 = g_w * e2 / (1.0 + e2)

    oh1 = row == i1
    oh2 = row == i2
    oh = jnp.where(oh1 | oh2, 1.0, 0.0)
    earlier = (lax.broadcasted_iota(jnp.int32, (tm, tm), 0) < lax.broadcasted_iota(jnp.int32, (tm, tm), 1))
    before = (jnp.dot(oh.astype(BF16), jnp.where(earlier, 1.0, 0.0).astype(BF16), preferred_element_type=F32)
              + carry_sc[:, 0:1])
    r1 = jnp.sum(jnp.where(oh1, before, 0.0), axis=0, keepdims=True).astype(jnp.int32)
    r2 = jnp.sum(jnp.where(oh2, before, 0.0), axis=0, keepdims=True).astype(jnp.int32)
    carry_sc[...] = carry_sc[...] + jnp.sum(oh, axis=1, keepdims=True)
    cnt_ref[...] = carry_sc[...]

    sub = lax.broadcasted_iota(jnp.int32, (SUBLANES, tm), 0)
    ri_ref[...] = jnp.where(sub == 0, i1, jnp.where(sub == 1, i2, jnp.where(sub == 2, r1,
                            jnp.where(sub == 3, r2, 0))))
    lrow = lax.broadcasted_iota(jnp.int32, (LANES, tm), 0)
    rw_ref[0] = jnp.where(lrow == 0, w1, jnp.where(lrow == 1, w2, 0.0)).T[:, :8]


def _mix_call(x, ig, ib, u, pre, cw, cb, clg, clb, cog, o, aog, wo, l1g, l1b, wrh, wrl, br):
    b, seq, _ = x.shape
    tm = MIX_TM
    nt = seq // tm
    full = lambda shape: pl.BlockSpec(shape, lambda i, j: (0,) * len(shape))
    tile = lambda w: pl.BlockSpec((1, tm, w), lambda i, j: (i, j, 0))
    halo = pl.BlockSpec((1, HALO, C_CONV), lambda i, j: (i, jnp.maximum(j * (tm // HALO) - 1, 0), 0))
    return pl.pallas_call(
        _mix_kernel,
        grid=(b, nt),
        in_specs=[tile(D_MODEL), full((1, D_MODEL)), full((1, D_MODEL)),
                  tile(C_CONV), halo, full((HALO, C_CONV)), full((HALO, C_CONV)), full((1, C_CONV)),
                  full((1, C_CONV)), full((1, C_CONV)), full((1, C_CONV)),
                  tile(D_ATTN), full((1, D_ATTN)), full((D_MODEL, D_MODEL)),
                  full((1, D_MODEL)), full((1, D_MODEL)), full((ROUTE_ROWS, D_MODEL)),
                  full((ROUTE_ROWS, D_MODEL)), full((ROUTE_ROWS, LANES))],
        out_specs=[tile(D_MODEL), tile(HALF_D), pl.BlockSpec((8, tm), lambda i, j: (0, i * nt + j)), tile(8),
                   full((ROUTE_ROWS, LANES))],
        out_shape=[jax.ShapeDtypeStruct((b, seq, D_MODEL), F32),
                   jax.ShapeDtypeStruct((b, seq, HALF_D), jnp.uint32),
                   jax.ShapeDtypeStruct((8, b * seq), jnp.int32),
                   jax.ShapeDtypeStruct((b, seq, 8), F32),
                   jax.ShapeDtypeStruct((ROUTE_ROWS, LANES), F32)],
        scratch_shapes=[pltpu.VMEM((HALO + tm, C_CONV), F32),
                        pltpu.VMEM((SUBLANES - 1, HALO + tm, C_CONV), F32),
                        pltpu.VMEM((ROUTE_ROWS, LANES), F32)],
        compiler_params=pltpu.CompilerParams(dimension_semantics=("arbitrary", "arbitrary"),
                                             vmem_limit_bytes=VMEM_LIMIT),
        name="mix",
    )(x, ig, ib, u, u, pre, cw, cb, clg, clb, cog, o, aog, wo, l1g, l1b, wrh, wrl, br)


def _tables_kernel(ps_ref, pe_ref, cn_ref, ri_ref, dest_ref, blk_ref):
    ri = ri_ref[...]
    start = jnp.zeros_like(ri)
    for e in range(N_EXPERTS):
        start = jnp.where(ri == e, ps_ref[e], start)
    dest_ref[...] = jnp.zeros_like(dest_ref)
    dest_ref[0:2, :] = start[0:2, :] + ri_ref[2:4, :]
    blk_start = lax.broadcasted_iota(jnp.int32, blk_ref.shape, 1) * MOE_BM
    be = jnp.zeros(blk_ref.shape, jnp.int32)
    for e in range(N_EXPERTS):
        be = be + jnp.where(pe_ref[e] <= blk_start, 1, 0)
    be = jnp.minimum(be, N_EXPERTS - 1)
    end = jnp.zeros(blk_ref.shape, jnp.int32)
    for e in range(N_EXPERTS):
        end = jnp.where(be == e, ps_ref[e] + cn_ref[e], end)
    valid = jnp.clip(end - blk_start, 0, MOE_BM)
    row = lax.broadcasted_iota(jnp.int32, blk_ref.shape, 0)
    blk_ref[...] = jnp.where(row == 0, be, valid)


def _tables_call(pad_start, pad_end, counts, ri, n_blocks):
    t = ri.shape[1]
    blk_w = -(-n_blocks // LANES) * LANES
    return pl.pallas_call(
        _tables_kernel,
        grid_spec=pltpu.PrefetchScalarGridSpec(
            num_scalar_prefetch=3,
            grid=(1,),
            in_specs=[pl.BlockSpec((SUBLANES, t), lambda i, *_: (0, 0))],
            out_specs=[pl.BlockSpec((SUBLANES, t), lambda i, *_: (0, 0)),
                       pl.BlockSpec((SUBLANES, blk_w), lambda i, *_: (0, 0))]),
        out_shape=[jax.ShapeDtypeStruct((SUBLANES, t), jnp.int32),
                   jax.ShapeDtypeStruct((SUBLANES, blk_w), jnp.int32)],
        compiler_params=pltpu.CompilerParams(dimension_semantics=("arbitrary",),
                                             vmem_limit_bytes=VMEM_LIMIT),
        name="tables",
    )(pad_start, pad_end, counts, ri)


def _dispatch_kernel(nv_ref, d0_ref, d1_ref, h_ref, xb_ref, zero_sc, sem):
    tm = DISP_TM
    n_blocks = nv_ref.shape[0]

    @pl.when(pl.program_id(0) == 0)
    def _():
        zero_sc[...] = jnp.zeros_like(zero_sc)

        def zero_copy(i):
            return pltpu.make_async_copy(zero_sc, xb_ref.at[pl.ds(i * MOE_BM, MOE_BM)], sem)

        def start(i, carry):
            @pl.when(nv_ref[i] < MOE_BM)
            def _():
                zero_copy(i).start()
            return carry

        def wait(i, carry):
            @pl.when(nv_ref[i] < MOE_BM)
            def _():
                zero_copy(i).wait()
            return carry

        lax.fori_loop(0, n_blocks, start, 0)
        lax.fori_loop(0, n_blocks, wait, 0)

    def issue(t, carry):
        src = h_ref.at[pl.ds(t, 1)]
        pltpu.make_async_copy(src, xb_ref.at[pl.ds(d0_ref[t], 1)], sem).start(priority=0)
        pltpu.make_async_copy(src, xb_ref.at[pl.ds(d1_ref[t], 1)], sem).start(priority=1)
        return carry

    lax.fori_loop(0, tm, issue, 0, unroll=8)
    for _ in range(2):
        pltpu.make_async_copy(h_ref, xb_ref.at[pl.ds(0, tm)], sem).wait()


def _dispatch_call(blk_valid, dest0, dest1, h1):
    t, w = h1.shape
    tm = DISP_TM
    return pl.pallas_call(
        _dispatch_kernel,
        grid_spec=pltpu.PrefetchScalarGridSpec(
            num_scalar_prefetch=1,
            grid=(t // tm,),
            in_specs=[pl.BlockSpec((tm,), lambda i, nv: (i,), memory_space=pltpu.SMEM),
                      pl.BlockSpec((tm,), lambda i, nv: (i,), memory_space=pltpu.SMEM),
                      pl.BlockSpec((tm, w), lambda i, nv: (i, 0))],
            out_specs=pl.BlockSpec(memory_space=pl.ANY),
            scratch_shapes=[pltpu.VMEM((MOE_BM, w), h1.dtype), pltpu.SemaphoreType.DMA]),
        out_shape=jax.ShapeDtypeStruct((blk_valid.shape[0] * MOE_BM, w), h1.dtype),
        compiler_params=pltpu.CompilerParams(dimension_semantics=("arbitrary",),
                                             vmem_limit_bytes=VMEM_LIMIT),
        name="dispatch",
    )(blk_valid, dest0, dest1, h1)


def _expert_kernel(be_ref, nv_ref, x_ref, wg_ref, wu_ref, wd_ref, y_ref, wg_sc, wu_sc, wd_sc):
    i = pl.program_id(0)
    nv = nv_ref[i]

    @pl.when((i == 0) | (be_ref[i] != be_ref[jnp.maximum(i - 1, 0)]))
    def _():
        wg_sc[...] = wg_ref[0, 0].astype(BF16)
        wu_sc[...] = wu_ref[0, 0].astype(BF16)
        wd_sc[...] = wd_ref[0, 0].astype(BF16)

    @pl.when(nv > 0)
    def _():
        x_lo, x_hi = _unpack_bf16_pair(x_ref[...])
        x_lo = x_lo.astype(BF16)
        x_hi = x_hi.astype(BF16)
        g = (jnp.dot(x_lo, wg_sc[:HALF_D, :], preferred_element_type=F32)
             + jnp.dot(x_hi, wg_sc[HALF_D:, :], preferred_element_type=F32))
        up = (jnp.dot(x_lo, wu_sc[:HALF_D, :], preferred_element_type=F32)
              + jnp.dot(x_hi, wu_sc[HALF_D:, :], preferred_element_type=F32))
        act = g * _sigmoid(g) * up
        y = jnp.dot(act.astype(BF16), wd_sc[...], preferred_element_type=F32)
        y_ref[...] = _pack_bf16_pair(y[:, :HALF_D], y[:, HALF_D:])

    @pl.when(nv == 0)
    def _():
        y_ref[...] = jnp.zeros_like(y_ref)


def _expert_call(blk_e, blk_valid, xb, w_gate, w_up, w_down):
    r = xb.shape[0]
    bm = MOE_BM
    wspec = lambda a, b: pl.BlockSpec((1, 1, a, b), lambda i, be, nv: (0, be[i], 0, 0))
    return pl.pallas_call(
        _expert_kernel,
        grid_spec=pltpu.PrefetchScalarGridSpec(
            num_scalar_prefetch=2,
            grid=(r // bm,),
            in_specs=[pl.BlockSpec((bm, HALF_D), lambda i, be, nv: (i, 0)),
                      wspec(D_MODEL, D_EXPERT), wspec(D_MODEL, D_EXPERT), wspec(D_EXPERT, D_MODEL)],
            out_specs=pl.BlockSpec((bm, HALF_D), lambda i, be, nv: (i, 0)),
            scratch_shapes=[pltpu.VMEM((D_MODEL, D_EXPERT), BF16), pltpu.VMEM((D_MODEL, D_EXPERT), BF16),
                            pltpu.VMEM((D_EXPERT, D_MODEL), BF16)]),
        out_shape=jax.ShapeDtypeStruct((r, HALF_D), jnp.uint32),
        compiler_params=pltpu.CompilerParams(dimension_semantics=("arbitrary",),
                                             vmem_limit_bytes=VMEM_LIMIT),
        name="experts",
    )(blk_e, blk_valid, xb, w_gate, w_up, w_down)


def _combine_kernel(d0_ref, d1_ref, n0_ref, n1_ref, h_ref, w_ref, g_ref, b_ref, yb_ref, o_ref, y_sc, sem):
    tm = COMB_TM
    i = pl.program_id(0)
    slot = i % 2

    def gather(dst0_ref, dst1_ref, s):
        def issue(t, carry):
            pltpu.make_async_copy(yb_ref.at[pl.ds(dst0_ref[t], 1)], y_sc.at[s, 0, pl.ds(t, 1)],
                                  sem.at[s]).start(priority=0)
            pltpu.make_async_copy(yb_ref.at[pl.ds(dst1_ref[t], 1)], y_sc.at[s, 1, pl.ds(t, 1)],
                                  sem.at[s]).start(priority=1)
            return carry
        lax.fori_loop(0, tm, issue, 0, unroll=8)

    @pl.when(i == 0)
    def _():
        gather(d0_ref, d1_ref, 0)

    @pl.when(i + 1 < pl.num_programs(0))
    def _():
        gather(n0_ref, n1_ref, 1 - slot)

    for kk in range(2):
        pltpu.make_async_copy(yb_ref.at[pl.ds(0, tm)], y_sc.at[slot, kk], sem.at[slot]).wait()
    w = w_ref[...]
    y0_lo, y0_hi = _unpack_bf16_pair(y_sc[slot, 0])
    y1_lo, y1_hi = _unpack_bf16_pair(y_sc[slot, 1])
    y = jnp.concatenate([w[:, 0:1] * y0_lo + w[:, 1:2] * y1_lo, w[:, 0:1] * y0_hi + w[:, 1:2] * y1_hi], axis=1)
    o_ref[...] = _ln(DEEPNORM_ALPHA * h_ref[...] + y, g_ref[...], b_ref[...])


def _combine_call(dest0, dest1, h1, rw, g, b, yb):
    t = h1.shape[0]
    tm = COMB_TM
    n = t // tm
    cur = pl.BlockSpec((tm,), lambda i: (i,), memory_space=pltpu.SMEM)
    nxt = pl.BlockSpec((tm,), lambda i: (jnp.minimum(i + 1, n - 1),), memory_space=pltpu.SMEM)
    return pl.pallas_call(
        _combine_kernel,
        grid=(n,),
        in_specs=[cur, cur, nxt, nxt,
                  pl.BlockSpec((tm, D_MODEL), lambda i: (i, 0)),
                  pl.BlockSpec((tm, 8), lambda i: (i, 0)),
                  pl.BlockSpec((1, D_MODEL), lambda i: (0, 0)),
                  pl.BlockSpec((1, D_MODEL), lambda i: (0, 0)),
                  pl.BlockSpec(memory_space=pl.ANY)],
        out_specs=pl.BlockSpec((tm, D_MODEL), lambda i: (i, 0)),
        out_shape=jax.ShapeDtypeStruct((t, D_MODEL), F32),
        scratch_shapes=[pltpu.VMEM((2, 2, tm, HALF_D), jnp.uint32), pltpu.SemaphoreType.DMA((2,))],
        compiler_params=pltpu.CompilerParams(dimension_semantics=("arbitrary",),
                                             vmem_limit_bytes=VMEM_LIMIT),
        name="combine",
    )(dest0, dest1, dest0, dest1, h1, rw, g, b, yb)


def _take_cols(w, idx):
    wz = jnp.concatenate([w, jnp.zeros((w.shape[0], 1), w.dtype)], axis=1)
    return wz[:, np.where(idx < 0, w.shape[1], idx)]


def _layout_indices():
    lane_src = np.full((HEAD_PAD,), -1, np.int64)
    lane_src[0:ROPE_HALF] = QK_NOPE + np.arange(ROPE_HALF)
    lane_src[ROPE_HALF:X2_LANE] = np.arange(X2_LANE - ROPE_HALF)
    lane_src[X2_LANE:X2_LANE + ROPE_HALF] = QK_NOPE + ROPE_HALF + np.arange(ROPE_HALF)
    n_lo = X2_LANE - ROPE_HALF
    lane_src[X2_LANE + ROPE_HALF:X2_LANE + ROPE_HALF + QK_NOPE - n_lo] = n_lo + np.arange(QK_NOPE - n_lo)
    q_idx = np.concatenate([np.where(lane_src >= 0, lane_src + (QK_NOPE + QK_ROPE) * h, -1)
                            for h in range(N_HEADS)])
    nope_src = np.where(lane_src < QK_NOPE, lane_src, -1)
    k_idx = np.concatenate([np.where(nope_src >= 0, nope_src + (QK_NOPE + V_DIM) * h, -1)
                            for h in range(N_HEADS)])
    v_idx = np.concatenate([QK_NOPE + np.arange(V_DIM) + (QK_NOPE + V_DIM) * h for h in range(N_HEADS)])
    kr_idx = np.full((LANES,), -1, np.int64)
    kr_idx[0:ROPE_HALF] = np.arange(ROPE_HALF)
    kr_idx[X2_LANE:X2_LANE + ROPE_HALF] = ROPE_HALF + np.arange(ROPE_HALF)
    return q_idx, k_idx, v_idx, kr_idx


def _rope_lane_tables(length):
    inv_freq = np.float32(ROPE_THETA) ** (-np.arange(ROPE_HALF, dtype=np.float32) / np.float32(ROPE_HALF))
    ang = np.arange(length, dtype=np.float32)[:, None] * inv_freq[None, :].astype(np.float32)
    cos, sin = np.cos(ang).astype(np.float32), np.sin(ang).astype(np.float32)
    ctab = np.ones((length, LANES), np.float32)
    stab = np.zeros((length, LANES), np.float32)
    for lo, sign in ((0, -1.0), (X2_LANE, 1.0)):
        ctab[:, lo:lo + ROPE_HALF] = cos
        stab[:, lo:lo + ROPE_HALF] = sign * sin
    return jnp.asarray(ctab), jnp.asarray(stab)


def kernel(x, meta_tokens, ln_in_g, ln_in_b, w_in, conv_w, conv_b, conv_ln_g, conv_ln_b, q_norm_g, w_uq,
           kv_norm_g, w_ukv, conv_out_g, attn_out_g, w_o, ln1_g, ln1_b, w_rg, b_rg, w_re, b_re,
           w_gate, w_up, w_down, ln2_g, ln2_b):
    assert w_in.shape[0] == 1, "single layer"
    bsz, seq, d = x.shape
    t = bsz * seq
    row = lambda a: a.reshape(1, -1)

    q_idx, k_idx, v_idx, kr_idx = _layout_indices()
    c4 = 2 * C_CONV + Q_LORA + KV_LORA
    w_in_l = jnp.concatenate([w_in[0][:, :c4], _take_cols(w_in[0][:, c4:], kr_idx)], axis=1).astype(BF16)
    wuq_l = _take_cols(w_uq[0], q_idx).astype(BF16)
    wuk_l = _take_cols(w_ukv[0], k_idx).astype(BF16)
    wvt_l = w_ukv[0][:, v_idx].T.astype(BF16)
    wo_l = w_o[0].astype(BF16)
    n_route_pad = ROUTE_ROWS - N_EXPERTS - N_GROUPS
    wr_t = jnp.concatenate([w_re[0].T, w_rg[0].T, jnp.zeros((n_route_pad, d), F32)], axis=0)
    wrh_l = wr_t.astype(BF16)
    wrl_l = (wr_t - wrh_l.astype(F32)).astype(BF16)
    br_l = jnp.broadcast_to(jnp.concatenate([b_re[0], b_rg[0], jnp.zeros((n_route_pad,), F32)])[:, None],
                            (ROUTE_ROWS, LANES))
    ctab, stab = _rope_lane_tables(N_META + seq)

    proj_w = (row(ln_in_g), row(ln_in_b), w_in_l, row(q_norm_g[0]), wuq_l, row(kv_norm_g[0]), wuk_l, wvt_l)
    u_m, _, k_m, vt_m = _proj_call(meta_tokens.astype(F32), N_META, 1, *proj_w, ctab[:N_META], stab[:N_META])
    u, q, k, vt = _proj_call(x.reshape(t, d), PROJ_TM, seq // PROJ_TM, *proj_w, ctab[N_META:], stab[N_META:])

    o = _attn_call(q.reshape(bsz, seq, -1), k.reshape(bsz, seq, -1),
                   vt.reshape(bsz, seq // ATT_T, D_ATTN, ATT_T), k_m, vt_m)

    pre = jnp.concatenate([jnp.zeros((HALO - N_META, C_CONV), F32), u_m], axis=0)
    cw = jnp.concatenate([conv_w[0], jnp.zeros((HALO - CONV_K, C_CONV), F32)], axis=0)
    h1, hp, ri, rw, cnt = _mix_call(x, row(ln_in_g), row(ln_in_b), u.reshape(bsz, seq, C_CONV), pre, cw,
                                    row(conv_b[0]), row(conv_ln_g[0]), row(conv_ln_b[0]), row(conv_out_g[0]),
                                    o, row(attn_out_g[0]), wo_l, row(ln1_g[0]), row(ln1_b[0]), wrh_l, wrl_l, br_l)

    counts = cnt[:N_EXPERTS, 0].astype(jnp.int32)
    padded = (counts + MOE_BM - 1) // MOE_BM * MOE_BM
    pad_end = jnp.cumsum(padded)
    pad_start = pad_end - padded
    n_blocks = (2 * t) // MOE_BM + N_EXPERTS
    dest, blk = _tables_call(pad_start, pad_end, counts, ri, n_blocks)
    blk_e, blk_valid = blk[0, :n_blocks], blk[1, :n_blocks]

    h1f = h1.reshape(t, d)
    dest0, dest1 = dest[0], dest[1]
    xb = _dispatch_call(blk_valid, dest0, dest1, hp.reshape(t, HALF_D))
    yb = _expert_call(blk_e, blk_valid, xb, w_gate, w_up, w_down)
    out = _combine_call(dest0, dest1, h1f, rw.reshape(t, 8), row(ln2_g[0]), row(ln2_b[0]), yb)
    return out.reshape(bsz, seq, d)
```

```python
import functools
import math

import numpy as np
import jax
import jax.numpy as jnp
from jax import lax
from jax.experimental import pallas as pl
from jax.experimental.pallas import tpu as pltpu
from jax.experimental.pallas import tpu_sc as plsc

F32 = jnp.float32
BF16 = jnp.bfloat16

D_MODEL = 1024
HALF_D = D_MODEL // 2
N_META = 16
CHUNK = 64
C_CONV = 512
CONV_K = 31
N_HEADS = 8
QK_NOPE = 64
QK_ROPE = 32
V_DIM = 64
Q_LORA = 384
KV_LORA = 256
ROPE_THETA = 10000.0
D_ATTN = N_HEADS * V_DIM
N_GROUPS = 4
EXPERTS_PER_GROUP = 8
N_EXPERTS = N_GROUPS * EXPERTS_PER_GROUP
D_EXPERT = 256
DEEPNORM_ALPHA = 2.0 ** 0.25
EPS = 1e-5
NEG_INF = -1e30

LANES = 128
SUBLANES = 8
HEAD_PAD = LANES
ROPE_HALF = QK_ROPE // 2
X2_LANE = 64
Z_COLS = 2 * C_CONV + Q_LORA + KV_LORA + LANES
Q_SCALE = (QK_NOPE + QK_ROPE) ** -0.5 * math.log2(math.e)

PROJ_TM = 512
ATT_T = 256
ATT_HEADS = 8
MIX_TM = 256
ROUTE_ROWS = 48
HALO = 32
MOE_BM = 512
DISP_TM = 1024
COMB_TM = 1024
SC_WINDOW = 128
VMEM_LIMIT = 56 * 1024 * 1024


def _ln(x, g, b):
    mu = jnp.mean(x, axis=-1, keepdims=True)
    xc = x - mu
    var = jnp.mean(xc * xc, axis=-1, keepdims=True)
    return xc * lax.rsqrt(var + EPS) * g + b


def _rms(x, g):
    return x * lax.rsqrt(jnp.mean(x * x, axis=-1, keepdims=True) + EPS) * g


def _sigmoid(x):
    return 1.0 / (1.0 + jnp.exp(-x))


def _pack_bf16_pair(lo, hi):
    lo_bits = lax.bitcast_convert_type(lo.astype(BF16).astype(F32), jnp.uint32)
    hi_bits = lax.bitcast_convert_type(hi.astype(BF16).astype(F32), jnp.uint32)
    return (lo_bits >> 16) | (hi_bits & jnp.uint32(0xFFFF0000))


def _unpack_bf16_pair(packed):
    lo = lax.bitcast_convert_type(packed << 16, F32)
    hi = lax.bitcast_convert_type(packed & jnp.uint32(0xFFFF0000), F32)
    return lo, hi


def _proj_kernel(x_ref, g_ref, b_ref, win_ref, qg_ref, wuq_ref, kvg_ref, wuk_ref, wvt_ref, c_ref, s_ref,
                 u_ref, q_ref, k_ref, vt_ref):
    h = _ln(x_ref[...], g_ref[...], b_ref[...])
    z = jnp.dot(h.astype(BF16), win_ref[...], preferred_element_type=F32)
    u_ref[...] = z[:, :C_CONV] * _sigmoid(z[:, C_CONV:2 * C_CONV])
    c0 = 2 * C_CONV
    cq = z[:, c0:c0 + Q_LORA]
    ckv = z[:, c0 + Q_LORA:c0 + Q_LORA + KV_LORA]
    kr = z[:, c0 + Q_LORA + KV_LORA:]
    cs = c_ref[...]
    sn = s_ref[...]
    q = jnp.dot(_rms(cq, qg_ref[...]).astype(BF16), wuq_ref[...], preferred_element_type=F32)
    kvn = _rms(ckv, kvg_ref[...]).astype(BF16)
    kn = jnp.dot(kvn, wuk_ref[...], preferred_element_type=F32)
    kr_rot = kr * cs + pltpu.roll(kr, X2_LANE, 1) * sn
    for hh in range(N_HEADS):
        sl = slice(HEAD_PAD * hh, HEAD_PAD * (hh + 1))
        qh = q[:, sl]
        qh = (qh * cs + pltpu.roll(qh, X2_LANE, 1) * sn) * Q_SCALE
        q_ref[:, sl] = qh.astype(BF16)
        k_ref[:, sl] = (kn[:, sl] + kr_rot).astype(BF16)
    vt = lax.dot_general(wvt_ref[...], kvn, (((1,), (1,)), ((), ())), preferred_element_type=F32)
    n_sub, sub_w = vt_ref.shape[1], vt_ref.shape[3]
    for tt in range(n_sub):
        vt_ref[0, tt] = vt[:, tt * sub_w:(tt + 1) * sub_w].astype(BF16)


def _proj_call(x2d, tm, pos_blocks, ln_g, ln_b, w_in, qg, wuq, kvg, wuk, wvt, ctab, stab):
    t = x2d.shape[0]
    sub_w = min(tm, ATT_T)
    n_sub = tm // sub_w
    full = lambda shape: pl.BlockSpec(shape, lambda i: (0,) * len(shape))
    row = lambda w: pl.BlockSpec((tm, w), lambda i: (i, 0))
    tab = pl.BlockSpec((tm, LANES), lambda i: (i % pos_blocks, 0))
    return pl.pallas_call(
        _proj_kernel,
        grid=(t // tm,),
        in_specs=[row(D_MODEL), full((1, D_MODEL)), full((1, D_MODEL)), full((D_MODEL, Z_COLS)),
                  full((1, Q_LORA)), full((Q_LORA, N_HEADS * HEAD_PAD)),
                  full((1, KV_LORA)), full((KV_LORA, N_HEADS * HEAD_PAD)), full((D_ATTN, KV_LORA)), tab, tab],
        out_specs=[row(C_CONV), row(N_HEADS * HEAD_PAD), row(N_HEADS * HEAD_PAD),
                   pl.BlockSpec((1, n_sub, D_ATTN, sub_w), lambda i: (i, 0, 0, 0))],
        out_shape=[jax.ShapeDtypeStruct((t, C_CONV), F32),
                   jax.ShapeDtypeStruct((t, N_HEADS * HEAD_PAD), BF16),
                   jax.ShapeDtypeStruct((t, N_HEADS * HEAD_PAD), BF16),
                   jax.ShapeDtypeStruct((t // tm, n_sub, D_ATTN, sub_w), BF16)],
        compiler_params=pltpu.CompilerParams(dimension_semantics=("arbitrary",),
                                             vmem_limit_bytes=VMEM_LIMIT),
        name="proj",
    )(x2d, ln_g, ln_b, w_in, qg, wuq, kvg, wuk, wvt, ctab, stab)


def _attn_kernel(q_ref, k_ref, vt_ref, km_ref, vtm_ref, o_ref, s_sc):
    seq = q_ref.shape[1]
    nq = seq // ATT_T
    key_chunk = lax.broadcasted_iota(jnp.int32, (ATT_T, ATT_T), 0) // CHUNK
    qry_chunk = lax.broadcasted_iota(jnp.int32, (ATT_T, ATT_T), 1) // CHUNK
    diag_visible = key_chunk <= qry_chunk
    nt = (((1,), (1,)), ((), ()))
    heads = range(ATT_HEADS)
    hsl = [slice(HEAD_PAD * hh, HEAD_PAD * (hh + 1)) for hh in heads]
    vsl = [slice(V_DIM * hh, V_DIM * (hh + 1)) for hh in heads]

    def update(s, vt, m, l, acc):
        m_new = jnp.maximum(m, jnp.max(s, axis=0, keepdims=True))
        alpha = jnp.exp2(m - m_new)
        p = jnp.exp2(s - m_new)
        l = alpha * l + jnp.sum(p, axis=0, keepdims=True)
        acc = alpha * acc + jnp.dot(vt, p.astype(BF16), preferred_element_type=F32)
        return m_new, l, acc

    def scores(qi, j, hh):
        q0 = pl.multiple_of(qi * ATT_T, ATT_T)
        k0 = pl.multiple_of(j * ATT_T, ATT_T)
        return lax.dot_general(k_ref[0, pl.ds(k0, ATT_T), hsl[hh]], q_ref[0, pl.ds(q0, ATT_T), hsl[hh]], nt,
                               preferred_element_type=F32)

    for hh in heads:
        s_sc[0, hh] = scores(0, 0, hh)

    def q_body(qi, carry):
        row0 = pl.multiple_of(qi * ATT_T, ATT_T)
        slot = qi % 2
        state = []
        for hh in heads:
            state += [jnp.full((1, ATT_T), NEG_INF, F32), jnp.zeros((1, ATT_T), F32),
                      jnp.zeros((V_DIM, ATT_T), F32)]

        def kv_body(j, st):
            out = []
            for hh in heads:
                s_cur = s_sc[slot, hh]
                s_next = scores(qi, j + 1, hh)
                out += update(s_cur, vt_ref[0, j, vsl[hh], :], *st[3 * hh:3 * hh + 3])
                s_sc[slot, hh] = s_next
            return tuple(out)

        state = lax.fori_loop(0, qi, kv_body, tuple(state))
        q_next = jnp.minimum(qi + 1, nq - 1)
        sms = [lax.dot_general(km_ref[:, hsl[hh]], q_ref[0, pl.ds(row0, ATT_T), hsl[hh]], nt,
                               preferred_element_type=F32) for hh in heads]
        for hh in heads:
            s_sc[1 - slot, hh] = scores(q_next, 0, hh)
        outs = []
        for hh in heads:
            m, l, acc = state[3 * hh:3 * hh + 3]
            s = jnp.where(diag_visible, s_sc[slot, hh], NEG_INF)
            sm = sms[hh]
            m_new = jnp.maximum(m, jnp.maximum(jnp.max(s, axis=0, keepdims=True),
                                               jnp.max(sm, axis=0, keepdims=True)))
            alpha = jnp.exp2(m - m_new)
            p = jnp.exp2(s - m_new)
            pm = jnp.exp2(sm - m_new)
            l = alpha * l + jnp.sum(p, axis=0, keepdims=True) + jnp.sum(pm, axis=0, keepdims=True)
            acc = (alpha * acc + jnp.dot(vt_ref[0, qi, vsl[hh], :], p.astype(BF16), preferred_element_type=F32)
                   + jnp.dot(vtm_ref[0, 0, vsl[hh], :], pm.astype(BF16), preferred_element_type=F32))
            outs.append(acc / l)
        o_ref[0, pl.ds(row0, ATT_T), :] = jnp.concatenate(outs, axis=0).T
        return carry

    lax.fori_loop(0, nq, q_body, 0)


def _attn_call(q, k, vt, k_meta, vt_meta):
    b, seq, _ = q.shape
    nk = seq // ATT_T
    hb = ATT_HEADS
    return pl.pallas_call(
        _attn_kernel,
        grid=(b, N_HEADS // hb),
        in_specs=[pl.BlockSpec((1, seq, hb * HEAD_PAD), lambda i, j: (i, 0, j)),
                  pl.BlockSpec((1, seq, hb * HEAD_PAD), lambda i, j: (i, 0, j)),
                  pl.BlockSpec((1, nk, hb * V_DIM, ATT_T), lambda i, j: (i, 0, j, 0)),
                  pl.BlockSpec((N_META, hb * HEAD_PAD), lambda i, j: (0, j)),
                  pl.BlockSpec((1, 1, hb * V_DIM, N_META), lambda i, j: (0, 0, j, 0))],
        out_specs=pl.BlockSpec((1, seq, hb * V_DIM), lambda i, j: (i, 0, j)),
        out_shape=jax.ShapeDtypeStruct((b, seq, D_ATTN), F32),
        scratch_shapes=[pltpu.VMEM((2, hb, ATT_T, ATT_T), F32)],
        compiler_params=pltpu.CompilerParams(dimension_semantics=("arbitrary", "arbitrary"),
                                             vmem_limit_bytes=VMEM_LIMIT),
        name="attn",
    )(q, k, vt, k_meta, vt_meta)


def _mix_kernel(x_ref, ig_ref, ib_ref, u_ref, uh_ref, pre_ref, cw_ref, cb_ref, clg_ref, clb_ref,
                cog_ref, o_ref, aog_ref, wo_ref, l1g_ref, l1b_ref, wrh_ref, wrl_ref, br_ref,
                h1_ref, hp_ref, ri_ref, rw_ref, cnt_ref, win_sc, sh_sc, carry_sc):
    bi = pl.program_id(0)
    ti = pl.program_id(1)
    tm = MIX_TM

    @pl.when((bi == 0) & (ti == 0))
    def _():
        carry_sc[...] = jnp.zeros_like(carry_sc)

    win_sc[0:HALO, :] = jnp.where(ti == 0, pre_ref[...], uh_ref[0])
    win_sc[HALO:, :] = u_ref[0]
    first = HALO - (CONV_K - 1)
    acc = jnp.broadcast_to(cb_ref[...], (tm, C_CONV))
    for res in range(SUBLANES):
        offs = [o for o in range(first, first + CONV_K) if o % SUBLANES == res]
        rows = tm + offs[-1] - res
        if res:
            sh_sc[res - 1, 0:rows, :] = win_sc[pl.ds(res, rows), :]
        for o in offs:
            src = sh_sc[res - 1, o - res:o - res + tm, :] if res else win_sc[o:o + tm, :]
            acc = acc + src * cw_ref[o - first:o - first + 1, :]
    c = _ln(acc, clg_ref[...], clb_ref[...])
    c = c * _sigmoid(c)
    cn = _rms(c, cog_ref[...])
    on = _rms(o_ref[0], aog_ref[...])
    y = (jnp.dot(cn.astype(BF16), wo_ref[0:C_CONV, :], preferred_element_type=F32)
         + jnp.dot(on.astype(BF16), wo_ref[C_CONV:, :], preferred_element_type=F32))
    h = _ln(x_ref[0], ig_ref[...], ib_ref[...])
    h1 = _ln(DEEPNORM_ALPHA * h + y, l1g_ref[...], l1b_ref[...])
    h1_ref[0] = h1
    hp_ref[0] = _pack_bf16_pair(h1[:, :HALF_D], h1[:, HALF_D:])

    nt = (((1,), (1,)), ((), ()))
    h_hi = h1.astype(BF16)
    h_lo = (h1 - h_hi.astype(F32)).astype(BF16)
    lt = (lax.dot_general(wrh_ref[...], h_hi, nt, preferred_element_type=F32)
          + lax.dot_general(wrl_ref[...], h_hi, nt, preferred_element_type=F32)
          + lax.dot_general(wrh_ref[...], h_lo, nt, preferred_element_type=F32)) + br_ref[:, 0:1]
    row = lax.broadcasted_iota(jnp.int32, (ROUTE_ROWS, tm), 0)
    big = jnp.int32(1 << 20)
    is_g = (row >= N_EXPERTS) & (row < N_EXPERTS + N_GROUPS)
    gl = jnp.where(is_g, lt, -jnp.inf)
    gmax = jnp.max(gl, axis=0, keepdims=True)
    g_idx = jnp.min(jnp.where(gl == gmax, row - N_EXPERTS, big), axis=0, keepdims=True)
    g_w = 1.0 / jnp.sum(jnp.exp(gl - gmax), axis=0, keepdims=True)
    in_grp = (row < N_EXPERTS) & ((row // EXPERTS_PER_GROUP) == g_idx)
    el = jnp.where(in_grp, lt, -jnp.inf)
    v1 = jnp.max(el, axis=0, keepdims=True)
    i1 = jnp.min(jnp.where(el == v1, row, big), axis=0, keepdims=True)
    el2 = jnp.where(row == i1, -jnp.inf, el)
    v2 = jnp.max(el2, axis=0, keepdims=True)
    i2 = jnp.min(jnp.where(el2 == v2, row, big), axis=0, keepdims=True)
    e2 = jnp.exp(v2 - v1)
    w1 = g_w / (1.0 + e2)
    w2 = g_w * e2 / (1.0 + e2)

    oh1 = row == i1
    oh2 = row == i2
    oh = jnp.where(oh1 | oh2, 1.0, 0.0)
    earlier = (lax.broadcasted_iota(jnp.int32, (tm, tm), 0) < lax.broadcasted_iota(jnp.int32, (tm, tm), 1))
    before = (jnp.dot(oh.astype(BF16), jnp.where(earlier, 1.0, 0.0).astype(BF16), preferred_element_type=F32)
              + carry_sc[:, 0:1])
    r1 = jnp.sum(jnp.where(oh1, before, 0.0), axis=0, keepdims=True).astype(jnp.int32)
    r2 = jnp.sum(jnp.where(oh2, before, 0.0), axis=0, keepdims=True).astype(jnp.int32)
    carry_sc[...] = carry_sc[...] + jnp.sum(oh, axis=1, keepdims=True)
    cnt_ref[...] = carry_sc[...]

    sub = lax.broadcasted_iota(jnp.int32, (SUBLANES, tm), 0)
    ri_ref[...] = jnp.where(sub == 0, i1, jnp.where(sub == 1, i2, jnp.where(sub == 2, r1,
                            jnp.where(sub == 3, r2, 0))))
    lrow = lax.broadcasted_iota(jnp.int32, (LANES, tm), 0)
    rw_ref[0] = jnp.where(lrow == 0, w1, jnp.where(lrow == 1, w2, 0.0)).T[:, :8]


def _mix_call(x, ig, ib, u, pre, cw, cb, clg, clb, cog, o, aog, wo, l1g, l1b, wrh, wrl, br):
    b, seq, _ = x.shape
    tm = MIX_TM
    nt = seq // tm
    full = lambda shape: pl.BlockSpec(shape, lambda i, j: (0,) * len(shape))
    tile = lambda w: pl.BlockSpec((1, tm, w), lambda i, j: (i, j, 0))
    halo = pl.BlockSpec((1, HALO, C_CONV), lambda i, j: (i, jnp.maximum(j * (tm // HALO) - 1, 0), 0))
    return pl.pallas_call(
        _mix_kernel,
        grid=(b, nt),
        in_specs=[tile(D_MODEL), full((1, D_MODEL)), full((1, D_MODEL)),
                  tile(C_CONV), halo, full((HALO, C_CONV)), full((HALO, C_CONV)), full((1, C_CONV)),
                  full((1, C_CONV)), full((1, C_CONV)), full((1, C_CONV)),
                  tile(D_ATTN), full((1, D_ATTN)), full((D_MODEL, D_MODEL)),
                  full((1, D_MODEL)), full((1, D_MODEL)), full((ROUTE_ROWS, D_MODEL)),
                  full((ROUTE_ROWS, D_MODEL)), full((ROUTE_ROWS, LANES))],
        out_specs=[tile(D_MODEL), tile(HALF_D), pl.BlockSpec((8, tm), lambda i, j: (0, i * nt + j)), tile(8),
                   full((ROUTE_ROWS, LANES))],
        out_shape=[jax.ShapeDtypeStruct((b, seq, D_MODEL), F32),
                   jax.ShapeDtypeStruct((b, seq, HALF_D), jnp.uint32),
                   jax.ShapeDtypeStruct((8, b * seq), jnp.int32),
                   jax.ShapeDtypeStruct((b, seq, 8), F32),
                   jax.ShapeDtypeStruct((ROUTE_ROWS, LANES), F32)],
        scratch_shapes=[pltpu.VMEM((HALO + tm, C_CONV), F32),
                        pltpu.VMEM((SUBLANES - 1, HALO + tm, C_CONV), F32),
                        pltpu.VMEM((ROUTE_ROWS, LANES), F32)],
        compiler_params=pltpu.CompilerParams(dimension_semantics=("arbitrary", "arbitrary"),
                                             vmem_limit_bytes=VMEM_LIMIT),
        name="mix",
    )(x, ig, ib, u, u, pre, cw, cb, clg, clb, cog, o, aog, wo, l1g, l1b, wrh, wrl, br)


def _tables_kernel(ps_ref, pe_ref, cn_ref, ri_ref, dest_ref, blk_ref):
    ri = ri_ref[...]
    start = jnp.zeros_like(ri)
    for e in range(N_EXPERTS):
        start = jnp.where(ri == e, ps_ref[e], start)
    dest_ref[...] = jnp.zeros_like(dest_ref)
    dest_ref[0:2, :] = start[0:2, :] + ri_ref[2:4, :]
    blk_start = lax.broadcasted_iota(jnp.int32, blk_ref.shape, 1) * MOE_BM
    be = jnp.zeros(blk_ref.shape, jnp.int32)
    for e in range(N_EXPERTS):
        be = be + jnp.where(pe_ref[e] <= blk_start, 1, 0)
    be = jnp.minimum(be, N_EXPERTS - 1)
    end = jnp.zeros(blk_ref.shape, jnp.int32)
    for e in range(N_EXPERTS):
        end = jnp.where(be == e, ps_ref[e] + cn_ref[e], end)
    valid = jnp.clip(end - blk_start, 0, MOE_BM)
    row = lax.broadcasted_iota(jnp.int32, blk_ref.shape, 0)
    blk_ref[...] = jnp.where(row == 0, be, valid)


def _tables_call(pad_start, pad_end, counts, ri, n_blocks):
    t = ri.shape[1]
    blk_w = -(-n_blocks // LANES) * LANES
    return pl.pallas_call(
        _tables_kernel,
        grid_spec=pltpu.PrefetchScalarGridSpec(
            num_scalar_prefetch=3,
            grid=(1,),
            in_specs=[pl.BlockSpec((SUBLANES, t), lambda i, *_: (0, 0))],
            out_specs=[pl.BlockSpec((SUBLANES, t), lambda i, *_: (0, 0)),
                       pl.BlockSpec((SUBLANES, blk_w), lambda i, *_: (0, 0))]),
        out_shape=[jax.ShapeDtypeStruct((SUBLANES, t), jnp.int32),
                   jax.ShapeDtypeStruct((SUBLANES, blk_w), jnp.int32)],
        compiler_params=pltpu.CompilerParams(dimension_semantics=("arbitrary",),
                                             vmem_limit_bytes=VMEM_LIMIT),
        name="tables",
    )(pad_start, pad_end, counts, ri)


def _dispatch_kernel(nv_ref, d0_ref, d1_ref, h_ref, xb_ref, zero_sc, sem):
    tm = DISP_TM
    n_blocks = nv_ref.shape[0]

    @pl.when(pl.program_id(0) == 0)
    def _():
        zero_sc[...] = jnp.zeros_like(zero_sc)

        def zero_copy(i):
            return pltpu.make_async_copy(zero_sc, xb_ref.at[pl.ds(i * MOE_BM, MOE_BM)], sem)

        def start(i, carry):
            @pl.when(nv_ref[i] < MOE_BM)
            def _():
                zero_copy(i).start()
            return carry

        def wait(i, carry):
            @pl.when(nv_ref[i] < MOE_BM)
            def _():
                zero_copy(i).wait()
            return carry

        lax.fori_loop(0, n_blocks, start, 0)
        lax.fori_loop(0, n_blocks, wait, 0)

    def issue(t, carry):
        src = h_ref.at[pl.ds(t, 1)]
        pltpu.make_async_copy(src, xb_ref.at[pl.ds(d0_ref[t], 1)], sem).start(priority=0)
        pltpu.make_async_copy(src, xb_ref.at[pl.ds(d1_ref[t], 1)], sem).start(priority=1)
        return carry

    lax.fori_loop(0, tm, issue, 0, unroll=8)
    for _ in range(2):
        pltpu.make_async_copy(h_ref, xb_ref.at[pl.ds(0, tm)], sem).wait()


def _dispatch_call(blk_valid, dest0, dest1, h1):
    t, w = h1.shape
    tm = DISP_TM
    return pl.pallas_call(
        _dispatch_kernel,
        grid_spec=pltpu.PrefetchScalarGridSpec(
            num_scalar_prefetch=1,
            grid=(t // tm,),
            in_specs=[pl.BlockSpec((tm,), lambda i, nv: (i,), memory_space=pltpu.SMEM),
                      pl.BlockSpec((tm,), lambda i, nv: (i,), memory_space=pltpu.SMEM),
                      pl.BlockSpec((tm, w), lambda i, nv: (i, 0))],
            out_specs=pl.BlockSpec(memory_space=pl.ANY),
            scratch_shapes=[pltpu.VMEM((MOE_BM, w), h1.dtype), pltpu.SemaphoreType.DMA]),
        out_shape=jax.ShapeDtypeStruct((blk_valid.shape[0] * MOE_BM, w), h1.dtype),
        compiler_params=pltpu.CompilerParams(dimension_semantics=("arbitrary",),
                                             vmem_limit_bytes=VMEM_LIMIT),
        name="dispatch",
    )(blk_valid, dest0, dest1, h1)


def _expert_kernel(be_ref, nv_ref, x_ref, wg_ref, wu_ref, wd_ref, y_ref, wg_sc, wu_sc, wd_sc):
    i = pl.program_id(0)
    nv = nv_ref[i]

    @pl.when((i == 0) | (be_ref[i] != be_ref[jnp.maximum(i - 1, 0)]))
    def _():
        wg_sc[...] = wg_ref[0, 0].astype(BF16)
        wu_sc[...] = wu_ref[0, 0].astype(BF16)
        wd_sc[...] = wd_ref[0, 0].astype(BF16)

    @pl.when(nv > 0)
    def _():
        x_lo, x_hi = _unpack_bf16_pair(x_ref[...])
        x_lo = x_lo.astype(BF16)
        x_hi = x_hi.astype(BF16)
        g = (jnp.dot(x_lo, wg_sc[:HALF_D, :], preferred_element_type=F32)
             + jnp.dot(x_hi, wg_sc[HALF_D:, :], preferred_element_type=F32))
        up = (jnp.dot(x_lo, wu_sc[:HALF_D, :], preferred_element_type=F32)
              + jnp.dot(x_hi, wu_sc[HALF_D:, :], preferred_element_type=F32))
        act = g * _sigmoid(g) * up
        y = jnp.dot(act.astype(BF16), wd_sc[...], preferred_element_type=F32)
        y_ref[...] = _pack_bf16_pair(y[:, :HALF_D], y[:, HALF_D:])

    @pl.when(nv == 0)
    def _():
        y_ref[...] = jnp.zeros_like(y_ref)


def _expert_call(blk_e, blk_valid, xb, w_gate, w_up, w_down):
    r = xb.shape[0]
    bm = MOE_BM
    wspec = lambda a, b: pl.BlockSpec((1, 1, a, b), lambda i, be, nv: (0, be[i], 0, 0))
    return pl.pallas_call(
        _expert_kernel,
        grid_spec=pltpu.PrefetchScalarGridSpec(
            num_scalar_prefetch=2,
            grid=(r // bm,),
            in_specs=[pl.BlockSpec((bm, HALF_D), lambda i, be, nv: (i, 0)),
                      wspec(D_MODEL, D_EXPERT), wspec(D_MODEL, D_EXPERT), wspec(D_EXPERT, D_MODEL)],
            out_specs=pl.BlockSpec((bm, HALF_D), lambda i, be, nv: (i, 0)),
            scratch_shapes=[pltpu.VMEM((D_MODEL, D_EXPERT), BF16), pltpu.VMEM((D_MODEL, D_EXPERT), BF16),
                            pltpu.VMEM((D_EXPERT, D_MODEL), BF16)]),
        out_shape=jax.ShapeDtypeStruct((r, HALF_D), jnp.uint32),
        compiler_params=pltpu.CompilerParams(dimension_semantics=("arbitrary",),
                                             vmem_limit_bytes=VMEM_LIMIT),
        name="experts",
    )(blk_e, blk_valid, xb, w_gate, w_up, w_down)


def _combine_kernel(d0_ref, d1_ref, n0_ref, n1_ref, h_ref, w_ref, g_ref, b_ref, yb_ref, o_ref, y_sc, sem):
    tm = COMB_TM
    i = pl.program_id(0)
    slot = i % 2

    def gather(dst0_ref, dst1_ref, s):
        def issue(t, carry):
            pltpu.make_async_copy(yb_ref.at[pl.ds(dst0_ref[t], 1)], y_sc.at[s, 0, pl.ds(t, 1)],
                                  sem.at[s]).start(priority=0)
            pltpu.make_async_copy(yb_ref.at[pl.ds(dst1_ref[t], 1)], y_sc.at[s, 1, pl.ds(t, 1)],
                                  sem.at[s]).start(priority=1)
            return carry
        lax.fori_loop(0, tm, issue, 0, unroll=8)

    @pl.when(i == 0)
    def _():
        gather(d0_ref, d1_ref, 0)

    @pl.when(i + 1 < pl.num_programs(0))
    def _():
        gather(n0_ref, n1_ref, 1 - slot)

    for kk in range(2):
        pltpu.make_async_copy(yb_ref.at[pl.ds(0, tm)], y_sc.at[slot, kk], sem.at[slot]).wait()
    w = w_ref[...]
    y0_lo, y0_hi = _unpack_bf16_pair(y_sc[slot, 0])
    y1_lo, y1_hi = _unpack_bf16_pair(y_sc[slot, 1])
    y = jnp.concatenate([w[:, 0:1] * y0_lo + w[:, 1:2] * y1_lo, w[:, 0:1] * y0_hi + w[:, 1:2] * y1_hi], axis=1)
    o_ref[...] = _ln(DEEPNORM_ALPHA * h_ref[...] + y, g_ref[...], b_ref[...])


def _combine_call(dest0, dest1, h1, rw, g, b, yb):
    t = h1.shape[0]
    tm = COMB_TM
    n = t // tm
    cur = pl.BlockSpec((tm,), lambda i: (i,), memory_space=pltpu.SMEM)
    nxt = pl.BlockSpec((tm,), lambda i: (jnp.minimum(i + 1, n - 1),), memory_space=pltpu.SMEM)
    return pl.pallas_call(
        _combine_kernel,
        grid=(n,),
        in_specs=[cur, cur, nxt, nxt,
                  pl.BlockSpec((tm, D_MODEL), lambda i: (i, 0)),
                  pl.BlockSpec((tm, 8), lambda i: (i, 0)),
                  pl.BlockSpec((1, D_MODEL), lambda i: (0, 0)),
                  pl.BlockSpec((1, D_MODEL), lambda i: (0, 0)),
                  pl.BlockSpec(memory_space=pl.ANY)],
        out_specs=pl.BlockSpec((tm, D_MODEL), lambda i: (i, 0)),
        out_shape=jax.ShapeDtypeStruct((t, D_MODEL), F32),
        scratch_shapes=[pltpu.VMEM((2, 2, tm, HALF_D), jnp.uint32), pltpu.SemaphoreType.DMA((2,))],
        compiler_params=pltpu.CompilerParams(dimension_semantics=("arbitrary",),
                                             vmem_limit_bytes=VMEM_LIMIT),
        name="combine",
    )(dest0, dest1, dest0, dest1, h1, rw, g, b, yb)


def _sc_gather_rows(table, idx):
    m = idx.shape[0]
    width = table.shape[1]
    sc = plsc.get_sparse_core_info()
    workers = sc.num_cores * sc.num_subcores
    per_worker = m // workers
    mesh = plsc.VectorSubcoreMesh(core_axis_name="c", subcore_axis_name="s")

    @functools.partial(pl.kernel, mesh=mesh,
                       out_type=jax.ShapeDtypeStruct((m, width), table.dtype),
                       scratch_types=[pltpu.VMEM((SC_WINDOW,), jnp.int32),
                                      pltpu.VMEM((SC_WINDOW, width), table.dtype),
                                      pltpu.SemaphoreType.DMA])
    def gather(table_hbm, idx_hbm, out_hbm, idx_v, rows_v, sem):
        base = (lax.axis_index("s") * sc.num_cores + lax.axis_index("c")) * per_worker

        @pl.loop(0, per_worker // SC_WINDOW)
        def _(step):
            off = base + step * SC_WINDOW
            pltpu.sync_copy(idx_hbm.at[pl.ds(off, SC_WINDOW)], idx_v)
            pltpu.async_copy(table_hbm.at[idx_v], rows_v, sem).wait()
            pltpu.sync_copy(rows_v, out_hbm.at[pl.ds(off, SC_WINDOW)])

    return gather(table, idx)


def _combine_rows_kernel(h_ref, w_ref, g_ref, b_ref, y0_ref, y1_ref, o_ref):
    w = w_ref[...]
    y0_lo, y0_hi = _unpack_bf16_pair(y0_ref[...])
    y1_lo, y1_hi = _unpack_bf16_pair(y1_ref[...])
    y = jnp.concatenate([w[:, 0:1] * y0_lo + w[:, 1:2] * y1_lo, w[:, 0:1] * y0_hi + w[:, 1:2] * y1_hi], axis=1)
    o_ref[...] = _ln(DEEPNORM_ALPHA * h_ref[...] + y, g_ref[...], b_ref[...])


def _combine_rows_call(h1, rw, g, b, yg):
    t = h1.shape[0]
    tm = COMB_TM
    n = t // tm
    return pl.pallas_call(
        _combine_rows_kernel,
        grid=(n,),
        in_specs=[pl.BlockSpec((tm, D_MODEL), lambda i: (i, 0)),
                  pl.BlockSpec((tm, 8), lambda i: (i, 0)),
                  pl.BlockSpec((1, D_MODEL), lambda i: (0, 0)),
                  pl.BlockSpec((1, D_MODEL), lambda i: (0, 0)),
                  pl.BlockSpec((tm, HALF_D), lambda i: (i, 0)),
                  pl.BlockSpec((tm, HALF_D), lambda i: (n + i, 0))],
        out_specs=pl.BlockSpec((tm, D_MODEL), lambda i: (i, 0)),
        out_shape=jax.ShapeDtypeStruct((t, D_MODEL), F32),
        compiler_params=pltpu.CompilerParams(dimension_semantics=("arbitrary",),
                                             vmem_limit_bytes=VMEM_LIMIT),
        name="combine_rows",
    )(h1, rw, g, b, yg, yg)


def _take_cols(w, idx):
    wz = jnp.concatenate([w, jnp.zeros((w.shape[0], 1), w.dtype)], axis=1)
    return wz[:, np.where(idx < 0, w.shape[1], idx)]


def _layout_indices():
    lane_src = np.full((HEAD_PAD,), -1, np.int64)
    lane_src[0:ROPE_HALF] = QK_NOPE + np.arange(ROPE_HALF)
    lane_src[ROPE_HALF:X2_LANE] = np.arange(X2_LANE - ROPE_HALF)
    lane_src[X2_LANE:X2_LANE + ROPE_HALF] = QK_NOPE + ROPE_HALF + np.arange(ROPE_HALF)
    n_lo = X2_LANE - ROPE_HALF
    lane_src[X2_LANE + ROPE_HALF:X2_LANE + ROPE_HALF + QK_NOPE - n_lo] = n_lo + np.arange(QK_NOPE - n_lo)
    q_idx = np.concatenate([np.where(lane_src >= 0, lane_src + (QK_NOPE + QK_ROPE) * h, -1)
                            for h in range(N_HEADS)])
    nope_src = np.where(lane_src < QK_NOPE, lane_src, -1)
    k_idx = np.concatenate([np.where(nope_src >= 0, nope_src + (QK_NOPE + V_DIM) * h, -1)
                            for h in range(N_HEADS)])
    v_idx = np.concatenate([QK_NOPE + np.arange(V_DIM) + (QK_NOPE + V_DIM) * h for h in range(N_HEADS)])
    kr_idx = np.full((LANES,), -1, np.int64)
    kr_idx[0:ROPE_HALF] = np.arange(ROPE_HALF)
    kr_idx[X2_LANE:X2_LANE + ROPE_HALF] = ROPE_HALF + np.arange(ROPE_HALF)
    return q_idx, k_idx, v_idx, kr_idx


def _rope_lane_tables(length):
    inv_freq = np.float32(ROPE_THETA) ** (-np.arange(ROPE_HALF, dtype=np.float32) / np.float32(ROPE_HALF))
    ang = np.arange(length, dtype=np.float32)[:, None] * inv_freq[None, :].astype(np.float32)
    cos, sin = np.cos(ang).astype(np.float32), np.sin(ang).astype(np.float32)
    ctab = np.ones((length, LANES), np.float32)
    stab = np.zeros((length, LANES), np.float32)
    for lo, sign in ((0, -1.0), (X2_LANE, 1.0)):
        ctab[:, lo:lo + ROPE_HALF] = cos
        stab[:, lo:lo + ROPE_HALF] = sign * sin
    return jnp.asarray(ctab), jnp.asarray(stab)


def kernel(x, meta_tokens, ln_in_g, ln_in_b, w_in, conv_w, conv_b, conv_ln_g, conv_ln_b, q_norm_g, w_uq,
           kv_norm_g, w_ukv, conv_out_g, attn_out_g, w_o, ln1_g, ln1_b, w_rg, b_rg, w_re, b_re,
           w_gate, w_up, w_down, ln2_g, ln2_b):
    assert w_in.shape[0] == 1, "single layer"
    bsz, seq, d = x.shape
    t = bsz * seq
    row = lambda a: a.reshape(1, -1)

    q_idx, k_idx, v_idx, kr_idx = _layout_indices()
    c4 = 2 * C_CONV + Q_LORA + KV_LORA
    w_in_l = jnp.concatenate([w_in[0][:, :c4], _take_cols(w_in[0][:, c4:], kr_idx)], axis=1).astype(BF16)
    wuq_l = _take_cols(w_uq[0], q_idx).astype(BF16)
    wuk_l = _take_cols(w_ukv[0], k_idx).astype(BF16)
    wvt_l = w_ukv[0][:, v_idx].T.astype(BF16)
    wo_l = w_o[0].astype(BF16)
    n_route_pad = ROUTE_ROWS - N_EXPERTS - N_GROUPS
    wr_t = jnp.concatenate([w_re[0].T, w_rg[0].T, jnp.zeros((n_route_pad, d), F32)], axis=0)
    wrh_l = wr_t.astype(BF16)
    wrl_l = (wr_t - wrh_l.astype(F32)).astype(BF16)
    br_l = jnp.broadcast_to(jnp.concatenate([b_re[0], b_rg[0], jnp.zeros((n_route_pad,), F32)])[:, None],
                            (ROUTE_ROWS, LANES))
    ctab, stab = _rope_lane_tables(N_META + seq)

    proj_w = (row(ln_in_g), row(ln_in_b), w_in_l, row(q_norm_g[0]), wuq_l, row(kv_norm_g[0]), wuk_l, wvt_l)
    u_m, _, k_m, vt_m = _proj_call(meta_tokens.astype(F32), N_META, 1, *proj_w, ctab[:N_META], stab[:N_META])
    u, q, k, vt = _proj_call(x.reshape(t, d), PROJ_TM, seq // PROJ_TM, *proj_w, ctab[N_META:], stab[N_META:])

    o = _attn_call(q.reshape(bsz, seq, -1), k.reshape(bsz, seq, -1),
                   vt.reshape(bsz, seq // ATT_T, D_ATTN, ATT_T), k_m, vt_m)

    pre = jnp.concatenate([jnp.zeros((HALO - N_META, C_CONV), F32), u_m], axis=0)
    cw = jnp.concatenate([conv_w[0], jnp.zeros((HALO - CONV_K, C_CONV), F32)], axis=0)
    h1, hp, ri, rw, cnt = _mix_call(x, row(ln_in_g), row(ln_in_b), u.reshape(bsz, seq, C_CONV), pre, cw,
                                    row(conv_b[0]), row(conv_ln_g[0]), row(conv_ln_b[0]), row(conv_out_g[0]),
                                    o, row(attn_out_g[0]), wo_l, row(ln1_g[0]), row(ln1_b[0]), wrh_l, wrl_l, br_l)

    counts = cnt[:N_EXPERTS, 0].astype(jnp.int32)
    padded = (counts + MOE_BM - 1) // MOE_BM * MOE_BM
    pad_end = jnp.cumsum(padded)
    pad_start = pad_end - padded
    n_blocks = (2 * t) // MOE_BM + N_EXPERTS
    dest, blk = _tables_call(pad_start, pad_end, counts, ri, n_blocks)
    blk_e, blk_valid = blk[0, :n_blocks], blk[1, :n_blocks]

    h1f = h1.reshape(t, d)
    dest0, dest1 = dest[0], dest[1]
    xb = _dispatch_call(blk_valid, dest0, dest1, hp.reshape(t, HALF_D))
    yb = _expert_call(blk_e, blk_valid, xb, w_gate, w_up, w_down)
    yg = _sc_gather_rows(yb, jnp.concatenate([dest0, dest1]))
    out = _combine_rows_call(h1f, rw.reshape(t, 8), row(ln2_g[0]), row(ln2_b[0]), yg)
    return out.reshape(bsz, seq, d)
```

```python
import functools
import math

import numpy as np
import jax
import jax.numpy as jnp
from jax import lax
from jax.experimental import pallas as pl
from jax.experimental.pallas import tpu as pltpu
from jax.experimental.pallas import tpu_sc as plsc

F32 = jnp.float32
BF16 = jnp.bfloat16

D_MODEL = 1024
HALF_D = D_MODEL // 2
N_META = 16
CHUNK = 64
C_CONV = 512
CONV_K = 31
N_HEADS = 8
QK_NOPE = 64
QK_ROPE = 32
V_DIM = 64
Q_LORA = 384
KV_LORA = 256
ROPE_THETA = 10000.0
D_ATTN = N_HEADS * V_DIM
N_GROUPS = 4
EXPERTS_PER_GROUP = 8
N_EXPERTS = N_GROUPS * EXPERTS_PER_GROUP
D_EXPERT = 256
DEEPNORM_ALPHA = 2.0 ** 0.25
EPS = 1e-5
NEG_INF = -1e30

LANES = 128
SUBLANES = 8
HEAD_PAD = LANES
ROPE_HALF = QK_ROPE // 2
X2_LANE = 64
Z_COLS = 2 * C_CONV + Q_LORA + KV_LORA + LANES
Q_SCALE = (QK_NOPE + QK_ROPE) ** -0.5 * math.log2(math.e)

PROJ_TM = 512
ATT_T = 256
ATT_HEADS = 8
MIX_TM = 256
ROUTE_ROWS = 48
HALO = 32
MOE_BM = 512
DISP_TM = 1024
COMB_TM = 1024
SC_WINDOW = 128
VMEM_LIMIT = 56 * 1024 * 1024


def _ln(x, g, b):
    mu = jnp.mean(x, axis=-1, keepdims=True)
    xc = x - mu
    var = jnp.mean(xc * xc, axis=-1, keepdims=True)
    return xc * lax.rsqrt(var + EPS) * g + b


def _rms(x, g):
    return x * lax.rsqrt(jnp.mean(x * x, axis=-1, keepdims=True) + EPS) * g


def _sigmoid(x):
    return 1.0 / (1.0 + jnp.exp(-x))


def _pack_bf16_pair(lo, hi):
    lo_bits = lax.bitcast_convert_type(lo.astype(BF16).astype(F32), jnp.uint32)
    hi_bits = lax.bitcast_convert_type(hi.astype(BF16).astype(F32), jnp.uint32)
    return (lo_bits >> 16) | (hi_bits & jnp.uint32(0xFFFF0000))


def _unpack_bf16_pair(packed):
    lo = lax.bitcast_convert_type(packed << 16, F32)
    hi = lax.bitcast_convert_type(packed & jnp.uint32(0xFFFF0000), F32)
    return lo, hi


def _proj_kernel(x_ref, g_ref, b_ref, win_ref, qg_ref, wuq_ref, kvg_ref, wuk_ref, wvt_ref, c_ref, s_ref,
                 u_ref, q_ref, k_ref, vt_ref):
    h = _ln(x_ref[...], g_ref[...], b_ref[...])
    z = jnp.dot(h.astype(BF16), win_ref[...], preferred_element_type=F32)
    u_ref[...] = z[:, :C_CONV] * _sigmoid(z[:, C_CONV:2 * C_CONV])
    c0 = 2 * C_CONV
    cq = z[:, c0:c0 + Q_LORA]
    ckv = z[:, c0 + Q_LORA:c0 + Q_LORA + KV_LORA]
    kr = z[:, c0 + Q_LORA + KV_LORA:]
    cs = c_ref[...]
    sn = s_ref[...]
    q = jnp.dot(_rms(cq, qg_ref[...]).astype(BF16), wuq_ref[...], preferred_element_type=F32)
    kvn = _rms(ckv, kvg_ref[...]).astype(BF16)
    kn = jnp.dot(kvn, wuk_ref[...], preferred_element_type=F32)
    kr_rot = kr * cs + pltpu.roll(kr, X2_LANE, 1) * sn
    for hh in range(N_HEADS):
        sl = slice(HEAD_PAD * hh, HEAD_PAD * (hh + 1))
        qh = q[:, sl]
        qh = (qh * cs + pltpu.roll(qh, X2_LANE, 1) * sn) * Q_SCALE
        q_ref[:, sl] = qh.astype(BF16)
        k_ref[:, sl] = (kn[:, sl] + kr_rot).astype(BF16)
    vt = lax.dot_general(wvt_ref[...], kvn, (((1,), (1,)), ((), ())), preferred_element_type=F32)
    n_sub, sub_w = vt_ref.shape[1], vt_ref.shape[3]
    for tt in range(n_sub):
        vt_ref[0, tt] = vt[:, tt * sub_w:(tt + 1) * sub_w].astype(BF16)


def _proj_call(x2d, tm, pos_blocks, ln_g, ln_b, w_in, qg, wuq, kvg, wuk, wvt, ctab, stab):
    t = x2d.shape[0]
    sub_w = min(tm, ATT_T)
    n_sub = tm // sub_w
    full = lambda shape: pl.BlockSpec(shape, lambda i: (0,) * len(shape))
    row = lambda w: pl.BlockSpec((tm, w), lambda i: (i, 0))
    tab = pl.BlockSpec((tm, LANES), lambda i: (i % pos_blocks, 0))
    return pl.pallas_call(
        _proj_kernel,
        grid=(t // tm,),
        in_specs=[row(D_MODEL), full((1, D_MODEL)), full((1, D_MODEL)), full((D_MODEL, Z_COLS)),
                  full((1, Q_LORA)), full((Q_LORA, N_HEADS * HEAD_PAD)),
                  full((1, KV_LORA)), full((KV_LORA, N_HEADS * HEAD_PAD)), full((D_ATTN, KV_LORA)), tab, tab],
        out_specs=[row(C_CONV), row(N_HEADS * HEAD_PAD), row(N_HEADS * HEAD_PAD),
                   pl.BlockSpec((1, n_sub, D_ATTN, sub_w), lambda i: (i, 0, 0, 0))],
        out_shape=[jax.ShapeDtypeStruct((t, C_CONV), F32),
                   jax.ShapeDtypeStruct((t, N_HEADS * HEAD_PAD), BF16),
                   jax.ShapeDtypeStruct((t, N_HEADS * HEAD_PAD), BF16),
                   jax.ShapeDtypeStruct((t // tm, n_sub, D_ATTN, sub_w), BF16)],
        compiler_params=pltpu.CompilerParams(dimension_semantics=("arbitrary",),
                                             vmem_limit_bytes=VMEM_LIMIT),
        name="proj",
    )(x2d, ln_g, ln_b, w_in, qg, wuq, kvg, wuk, wvt, ctab, stab)


def _attn_kernel(q_ref, k_ref, vt_ref, km_ref, vtm_ref, o_ref, s_sc):
    seq = q_ref.shape[1]
    nq = seq // ATT_T
    key_chunk = lax.broadcasted_iota(jnp.int32, (ATT_T, ATT_T), 0) // CHUNK
    qry_chunk = lax.broadcasted_iota(jnp.int32, (ATT_T, ATT_T), 1) // CHUNK
    diag_visible = key_chunk <= qry_chunk
    nt = (((1,), (1,)), ((), ()))
    heads = range(ATT_HEADS)
    hsl = [slice(HEAD_PAD * hh, HEAD_PAD * (hh + 1)) for hh in heads]
    vsl = [slice(V_DIM * hh, V_DIM * (hh + 1)) for hh in heads]

    def update(s, vt, m, l, acc):
        m_new = jnp.maximum(m, jnp.max(s, axis=0, keepdims=True))
        alpha = jnp.exp2(m - m_new)
        p = jnp.exp2(s - m_new)
        l = alpha * l + jnp.sum(p, axis=0, keepdims=True)
        acc = alpha * acc + jnp.dot(vt, p.astype(BF16), preferred_element_type=F32)
        return m_new, l, acc

    def scores(qi, j, hh):
        q0 = pl.multiple_of(qi * ATT_T, ATT_T)
        k0 = pl.multiple_of(j * ATT_T, ATT_T)
        return lax.dot_general(k_ref[0, pl.ds(k0, ATT_T), hsl[hh]], q_ref[0, pl.ds(q0, ATT_T), hsl[hh]], nt,
                               preferred_element_type=F32)

    for hh in heads:
        s_sc[0, hh] = scores(0, 0, hh)

    def q_body(qi, carry):
        row0 = pl.multiple_of(qi * ATT_T, ATT_T)
        slot = qi % 2
        state = []
        for hh in heads:
            state += [jnp.full((1, ATT_T), NEG_INF, F32), jnp.zeros((1, ATT_T), F32),
                      jnp.zeros((V_DIM, ATT_T), F32)]

        def kv_body(j, st):
            out = []
            for hh in heads:
                s_cur = s_sc[slot, hh]
                s_next = scores(qi, j + 1, hh)
                out += update(s_cur, vt_ref[0, j, vsl[hh], :], *st[3 * hh:3 * hh + 3])
                s_sc[slot, hh] = s_next
            return tuple(out)

        state = lax.fori_loop(0, qi, kv_body, tuple(state))
        q_next = jnp.minimum(qi + 1, nq - 1)
        sms = [lax.dot_general(km_ref[:, hsl[hh]], q_ref[0, pl.ds(row0, ATT_T), hsl[hh]], nt,
                               preferred_element_type=F32) for hh in heads]
        for hh in heads:
            s_sc[1 - slot, hh] = scores(q_next, 0, hh)
        outs = []
        for hh in heads:
            m, l, acc = state[3 * hh:3 * hh + 3]
            s = jnp.where(diag_visible, s_sc[slot, hh], NEG_INF)
            sm = sms[hh]
            m_new = jnp.maximum(m, jnp.maximum(jnp.max(s, axis=0, keepdims=True),
                                               jnp.max(sm, axis=0, keepdims=True)))
            alpha = jnp.exp2(m - m_new)
            p = jnp.exp2(s - m_new)
            pm = jnp.exp2(sm - m_new)
            l = alpha * l + jnp.sum(p, axis=0, keepdims=True) + jnp.sum(pm, axis=0, keepdims=True)
            acc = (alpha * acc + jnp.dot(vt_ref[0, qi, vsl[hh], :], p.astype(BF16), preferred_element_type=F32)
                   + jnp.dot(vtm_ref[0, 0, vsl[hh], :], pm.astype(BF16), preferred_element_type=F32))
            outs.append(acc / l)
        o_ref[0, pl.ds(row0, ATT_T), :] = jnp.concatenate(outs, axis=0).T
        return carry

    lax.fori_loop(0, nq, q_body, 0)


def _attn_call(q, k, vt, k_meta, vt_meta):
    b, seq, _ = q.shape
    nk = seq // ATT_T
    hb = ATT_HEADS
    return pl.pallas_call(
        _attn_kernel,
        grid=(b, N_HEADS // hb),
        in_specs=[pl.BlockSpec((1, seq, hb * HEAD_PAD), lambda i, j: (i, 0, j)),
                  pl.BlockSpec((1, seq, hb * HEAD_PAD), lambda i, j: (i, 0, j)),
                  pl.BlockSpec((1, nk, hb * V_DIM, ATT_T), lambda i, j: (i, 0, j, 0)),
                  pl.BlockSpec((N_META, hb * HEAD_PAD), lambda i, j: (0, j)),
                  pl.BlockSpec((1, 1, hb * V_DIM, N_META), lambda i, j: (0, 0, j, 0))],
        out_specs=pl.BlockSpec((1, seq, hb * V_DIM), lambda i, j: (i, 0, j)),
        out_shape=jax.ShapeDtypeStruct((b, seq, D_ATTN), F32),
        scratch_shapes=[pltpu.VMEM((2, hb, ATT_T, ATT_T), F32)],
        compiler_params=pltpu.CompilerParams(dimension_semantics=("arbitrary", "arbitrary"),
                                             vmem_limit_bytes=VMEM_LIMIT),
        name="attn",
    )(q, k, vt, k_meta, vt_meta)


def _mix_kernel(x_ref, ig_ref, ib_ref, u_ref, uh_ref, pre_ref, cw_ref, cb_ref, clg_ref, clb_ref,
                cog_ref, o_ref, aog_ref, wo_ref, l1g_ref, l1b_ref, wrh_ref, wrl_ref, br_ref,
                h1_ref, hp_ref, ri_ref, rw_ref, cnt_ref, win_sc, sh_sc, carry_sc):
    bi = pl.program_id(0)
    ti = pl.program_id(1)
    tm = MIX_TM

    @pl.when((bi == 0) & (ti == 0))
    def _():
        carry_sc[...] = jnp.zeros_like(carry_sc)

    win_sc[0:HALO, :] = jnp.where(ti == 0, pre_ref[...], uh_ref[0])
    win_sc[HALO:, :] = u_ref[0]
    first = HALO - (CONV_K - 1)
    acc = jnp.broadcast_to(cb_ref[...], (tm, C_CONV))
    for res in range(SUBLANES):
        offs = [o for o in range(first, first + CONV_K) if o % SUBLANES == res]
        rows = tm + offs[-1] - res
        if res:
            sh_sc[res - 1, 0:rows, :] = win_sc[pl.ds(res, rows), :]
        for o in offs:
            src = sh_sc[res - 1, o - res:o - res + tm, :] if res else win_sc[o:o + tm, :]
            acc = acc + src * cw_ref[o - first:o - first + 1, :]
    c = _ln(acc, clg_ref[...], clb_ref[...])
    c = c * _sigmoid(c)
    cn = _rms(c, cog_ref[...])
    on = _rms(o_ref[0], aog_ref[...])
    y = (jnp.dot(cn.astype(BF16), wo_ref[0:C_CONV, :], preferred_element_type=F32)
         + jnp.dot(on.astype(BF16), wo_ref[C_CONV:, :], preferred_element_type=F32))
    h = _ln(x_ref[0], ig_ref[...], ib_ref[...])
    h1 = _ln(DEEPNORM_ALPHA * h + y, l1g_ref[...], l1b_ref[...])
    h1_ref[0] = h1
    hp_ref[0] = _pack_bf16_pair(h1[:, :HALF_D], h1[:, HALF_D:])

    nt = (((1,), (1,)), ((), ()))
    h_hi = h1.astype(BF16)
    h_lo = (h1 - h_hi.astype(F32)).astype(BF16)
    lt = (lax.dot_general(wrh_ref[...], h_hi, nt, preferred_element_type=F32)
          + lax.dot_general(wrl_ref[...], h_hi, nt, preferred_element_type=F32)
          + lax.dot_general(wrh_ref[...], h_lo, nt, preferred_element_type=F32)) + br_ref[:, 0:1]
    row = lax.broadcasted_iota(jnp.int32, (ROUTE_ROWS, tm), 0)
    big = jnp.int32(1 << 20)
    is_g = (row >= N_EXPERTS) & (row < N_EXPERTS + N_GROUPS)
    gl = jnp.where(is_g, lt, -jnp.inf)
    gmax = jnp.max(gl, axis=0, keepdims=True)
    g_idx = jnp.min(jnp.where(gl == gmax, row - N_EXPERTS, big), axis=0, keepdims=True)
    g_w = 1.0 / jnp.sum(jnp.exp(gl - gmax), axis=0, keepdims=True)
    in_grp = (row < N_EXPERTS) & ((row // EXPERTS_PER_GROUP) == g_idx)
    el = jnp.where(in_grp, lt, -jnp.inf)
    v1 = jnp.max(el, axis=0, keepdims=True)
    i1 = jnp.min(jnp.where(el == v1, row, big), axis=0, keepdims=True)
    el2 = jnp.where(row == i1, -jnp.inf, el)
    v2 = jnp.max(el2, axis=0, keepdims=True)
    i2 = jnp.min(jnp.where(el2 == v2, row, big), axis=0, keepdims=True)
    e2 = jnp.exp(v2 - v1)
    w1 = g_w / (1.0 + e2)
    w2 = g_w * e2 / (1.0 + e2)

    oh1 = row == i1
    oh2 = row == i2
    oh = jnp.where(oh1 | oh2, 1.0, 0.0)
    earlier = (lax.broadcasted_iota(jnp.int32, (tm, tm), 0) < lax.broadcasted_iota(jnp.int32, (tm, tm), 1))
    before = (jnp.dot(oh.astype(BF16), jnp.where(earlier, 1.0, 0.0).astype(BF16), preferred_element_type=F32)
              + carry_sc[:, 0:1])
    r1 = jnp.sum(jnp.where(oh1, before, 0.0), axis=0, keepdims=True).astype(jnp.int32)
    r2 = jnp.sum(jnp.where(oh2, before, 0.0), axis=0, keepdims=True).astype(jnp.int32)
    carry_sc[...] = carry_sc[...] + jnp.sum(oh, axis=1, keepdims=True)
    cnt_ref[...] = carry_sc[...]

    sub = lax.broadcasted_iota(jnp.int32, (SUBLANES, tm), 0)
    ri_ref[...] = jnp.where(sub == 0, i1, jnp.where(sub == 1, i2, jnp.where(sub == 2, r1,
                            jnp.where(sub == 3, r2, 0))))
    lrow = lax.broadcasted_iota(jnp.int32, (LANES, tm), 0)
    rw_ref[0] = jnp.where(lrow == 0, w1, jnp.where(lrow == 1, w2, 0.0)).T[:, :8]


def _mix_call(x, ig, ib, u, pre, cw, cb, clg, clb, cog, o, aog, wo, l1g, l1b, wrh, wrl, br):
    b, seq, _ = x.shape
    tm = MIX_TM
    nt = seq // tm
    full = lambda shape: pl.BlockSpec(shape, lambda i, j: (0,) * len(shape))
    tile = lambda w: pl.BlockSpec((1, tm, w), lambda i, j: (i, j, 0))
    halo = pl.BlockSpec((1, HALO, C_CONV), lambda i, j: (i, jnp.maximum(j * (tm // HALO) - 1, 0), 0))
    return pl.pallas_call(
        _mix_kernel,
        grid=(b, nt),
        in_specs=[tile(D_MODEL), full((1, D_MODEL)), full((1, D_MODEL)),
                  tile(C_CONV), halo, full((HALO, C_CONV)), full((HALO, C_CONV)), full((1, C_CONV)),
                  full((1, C_CONV)), full((1, C_CONV)), full((1, C_CONV)),
                  tile(D_ATTN), full((1, D_ATTN)), full((D_MODEL, D_MODEL)),
                  full((1, D_MODEL)), full((1, D_MODEL)), full((ROUTE_ROWS, D_MODEL)),
                  full((ROUTE_ROWS, D_MODEL)), full((ROUTE_ROWS, LANES))],
        out_specs=[tile(D_MODEL), tile(HALF_D), pl.BlockSpec((8, tm), lambda i, j: (0, i * nt + j)), tile(8),
                   full((ROUTE_ROWS, LANES))],
        out_shape=[jax.ShapeDtypeStruct((b, seq, D_MODEL), F32),
                   jax.ShapeDtypeStruct((b, seq, HALF_D), jnp.uint32),
                   jax.ShapeDtypeStruct((8, b * seq), jnp.int32),
                   jax.ShapeDtypeStruct((b, seq, 8), F32),
                   jax.ShapeDtypeStruct((ROUTE_ROWS, LANES), F32)],
        scratch_shapes=[pltpu.VMEM((HALO + tm, C_CONV), F32),
                        pltpu.VMEM((SUBLANES - 1, HALO + tm, C_CONV), F32),
                        pltpu.VMEM((ROUTE_ROWS, LANES), F32)],
        compiler_params=pltpu.CompilerParams(dimension_semantics=("arbitrary", "arbitrary"),
                                             vmem_limit_bytes=VMEM_LIMIT),
        name="mix",
    )(x, ig, ib, u, u, pre, cw, cb, clg, clb, cog, o, aog, wo, l1g, l1b, wrh, wrl, br)


def _tables_kernel(ps_ref, pe_ref, cn_ref, ri_ref, dest_ref, blk_ref):
    ri = ri_ref[...]
    start = jnp.zeros_like(ri)
    for e in range(N_EXPERTS):
        start = jnp.where(ri == e, ps_ref[e], start)
    dest_ref[...] = jnp.zeros_like(dest_ref)
    dest_ref[0:2, :] = start[0:2, :] + ri_ref[2:4, :]
    blk_start = lax.broadcasted_iota(jnp.int32, blk_ref.shape, 1) * MOE_BM
    be = jnp.zeros(blk_ref.shape, jnp.int32)
    for e in range(N_EXPERTS):
        be = be + jnp.where(pe_ref[e] <= blk_start, 1, 0)
    be = jnp.minimum(be, N_EXPERTS - 1)
    end = jnp.zeros(blk_ref.shape, jnp.int32)
    for e in range(N_EXPERTS):
        end = jnp.where(be == e, ps_ref[e] + cn_ref[e], end)
    valid = jnp.clip(end - blk_start, 0, MOE_BM)
    row = lax.broadcasted_iota(jnp.int32, blk_ref.shape, 0)
    blk_ref[...] = jnp.where(row == 0, be, valid)


def _tables_call(pad_start, pad_end, counts, ri, n_blocks):
    t = ri.shape[1]
    blk_w = -(-n_blocks // LANES) * LANES
    return pl.pallas_call(
        _tables_kernel,
        grid_spec=pltpu.PrefetchScalarGridSpec(
            num_scalar_prefetch=3,
            grid=(1,),
            in_specs=[pl.BlockSpec((SUBLANES, t), lambda i, *_: (0, 0))],
            out_specs=[pl.BlockSpec((SUBLANES, t), lambda i, *_: (0, 0)),
                       pl.BlockSpec((SUBLANES, blk_w), lambda i, *_: (0, 0))]),
        out_shape=[jax.ShapeDtypeStruct((SUBLANES, t), jnp.int32),
                   jax.ShapeDtypeStruct((SUBLANES, blk_w), jnp.int32)],
        compiler_params=pltpu.CompilerParams(dimension_semantics=("arbitrary",),
                                             vmem_limit_bytes=VMEM_LIMIT),
        name="tables",
    )(pad_start, pad_end, counts, ri)


def _dispatch_kernel(nv_ref, d0_ref, d1_ref, h_ref, xb_ref, zero_sc, sem):
    tm = DISP_TM
    n_blocks = nv_ref.shape[0]

    @pl.when(pl.program_id(0) == 0)
    def _():
        zero_sc[...] = jnp.zeros_like(zero_sc)

        def zero_copy(i):
            return pltpu.make_async_copy(zero_sc, xb_ref.at[pl.ds(i * MOE_BM, MOE_BM)], sem)

        def start(i, carry):
            @pl.when(nv_ref[i] < MOE_BM)
            def _():
                zero_copy(i).start()
            return carry

        def wait(i, carry):
            @pl.when(nv_ref[i] < MOE_BM)
            def _():
                zero_copy(i).wait()
            return carry

        lax.fori_loop(0, n_blocks, start, 0)
        lax.fori_loop(0, n_blocks, wait, 0)

    def issue(t, carry):
        src = h_ref.at[pl.ds(t, 1)]
        pltpu.make_async_copy(src, xb_ref.at[pl.ds(d0_ref[t], 1)], sem).start(priority=0)
        pltpu.make_async_copy(src, xb_ref.at[pl.ds(d1_ref[t], 1)], sem).start(priority=1)
        return carry

    lax.fori_loop(0, tm, issue, 0, unroll=8)
    for _ in range(2):
        pltpu.make_async_copy(h_ref, xb_ref.at[pl.ds(0, tm)], sem).wait()


def _dispatch_call(blk_valid, dest0, dest1, h1):
    t, w = h1.shape
    tm = DISP_TM
    return pl.pallas_call(
        _dispatch_kernel,
        grid_spec=pltpu.PrefetchScalarGridSpec(
            num_scalar_prefetch=1,
            grid=(t // tm,),
            in_specs=[pl.BlockSpec((tm,), lambda i, nv: (i,), memory_space=pltpu.SMEM),
                      pl.BlockSpec((tm,), lambda i, nv: (i,), memory_space=pltpu.SMEM),
                      pl.BlockSpec((tm, w), lambda i, nv: (i, 0))],
            out_specs=pl.BlockSpec(memory_space=pl.ANY),
            scratch_shapes=[pltpu.VMEM((MOE_BM, w), h1.dtype), pltpu.SemaphoreType.DMA]),
        out_shape=jax.ShapeDtypeStruct((blk_valid.shape[0] * MOE_BM, w), h1.dtype),
        compiler_params=pltpu.CompilerParams(dimension_semantics=("arbitrary",),
                                             vmem_limit_bytes=VMEM_LIMIT),
        name="dispatch",
    )(blk_valid, dest0, dest1, h1)


def _expert_kernel(be_ref, nv_ref, x_ref, wg_ref, wu_ref, wd_ref, y_ref, wg_sc, wu_sc, wd_sc):
    i = pl.program_id(0)
    nv = nv_ref[i]

    @pl.when((i == 0) | (be_ref[i] != be_ref[jnp.maximum(i - 1, 0)]))
    def _():
        wg_sc[...] = wg_ref[0, 0].astype(BF16)
        wu_sc[...] = wu_ref[0, 0].astype(BF16)
        wd_sc[...] = wd_ref[0, 0].astype(BF16)

    @pl.when(nv > 0)
    def _():
        rowid = lax.broadcasted_iota(jnp.int32, x_ref.shape, 0)
        x_lo, x_hi = _unpack_bf16_pair(jnp.where(rowid < nv, x_ref[...], jnp.uint32(0)))
        x_lo = x_lo.astype(BF16)
        x_hi = x_hi.astype(BF16)
        g = (jnp.dot(x_lo, wg_sc[:HALF_D, :], preferred_element_type=F32)
             + jnp.dot(x_hi, wg_sc[HALF_D:, :], preferred_element_type=F32))
        up = (jnp.dot(x_lo, wu_sc[:HALF_D, :], preferred_element_type=F32)
              + jnp.dot(x_hi, wu_sc[HALF_D:, :], preferred_element_type=F32))
        act = g * _sigmoid(g) * up
        y = jnp.dot(act.astype(BF16), wd_sc[...], preferred_element_type=F32)
        y_ref[...] = _pack_bf16_pair(y[:, :HALF_D], y[:, HALF_D:])

    @pl.when(nv == 0)
    def _():
        y_ref[...] = jnp.zeros_like(y_ref)


def _expert_call(blk_e, blk_valid, xb, w_gate, w_up, w_down):
    r = xb.shape[0]
    bm = MOE_BM
    wspec = lambda a, b: pl.BlockSpec((1, 1, a, b), lambda i, be, nv: (0, be[i], 0, 0))
    return pl.pallas_call(
        _expert_kernel,
        grid_spec=pltpu.PrefetchScalarGridSpec(
            num_scalar_prefetch=2,
            grid=(r // bm,),
            in_specs=[pl.BlockSpec((bm, HALF_D), lambda i, be, nv: (i, 0)),
                      wspec(D_MODEL, D_EXPERT), wspec(D_MODEL, D_EXPERT), wspec(D_EXPERT, D_MODEL)],
            out_specs=pl.BlockSpec((bm, HALF_D), lambda i, be, nv: (i, 0)),
            scratch_shapes=[pltpu.VMEM((D_MODEL, D_EXPERT), BF16), pltpu.VMEM((D_MODEL, D_EXPERT), BF16),
                            pltpu.VMEM((D_EXPERT, D_MODEL), BF16)]),
        out_shape=jax.ShapeDtypeStruct((r, HALF_D), jnp.uint32),
        compiler_params=pltpu.CompilerParams(dimension_semantics=("arbitrary",),
                                             vmem_limit_bytes=VMEM_LIMIT),
        name="experts",
    )(blk_e, blk_valid, xb, w_gate, w_up, w_down)


def _combine_kernel(d0_ref, d1_ref, n0_ref, n1_ref, h_ref, w_ref, g_ref, b_ref, yb_ref, o_ref, y_sc, sem):
    tm = COMB_TM
    i = pl.program_id(0)
    slot = i % 2

    def gather(dst0_ref, dst1_ref, s):
        def issue(t, carry):
            pltpu.make_async_copy(yb_ref.at[pl.ds(dst0_ref[t], 1)], y_sc.at[s, 0, pl.ds(t, 1)],
                                  sem.at[s]).start(priority=0)
            pltpu.make_async_copy(yb_ref.at[pl.ds(dst1_ref[t], 1)], y_sc.at[s, 1, pl.ds(t, 1)],
                                  sem.at[s]).start(priority=1)
            return carry
        lax.fori_loop(0, tm, issue, 0, unroll=8)

    @pl.when(i == 0)
    def _():
        gather(d0_ref, d1_ref, 0)

    @pl.when(i + 1 < pl.num_programs(0))
    def _():
        gather(n0_ref, n1_ref, 1 - slot)

    for kk in range(2):
        pltpu.make_async_copy(yb_ref.at[pl.ds(0, tm)], y_sc.at[slot, kk], sem.at[slot]).wait()
    w = w_ref[...]
    y0_lo, y0_hi = _unpack_bf16_pair(y_sc[slot, 0])
    y1_lo, y1_hi = _unpack_bf16_pair(y_sc[slot, 1])
    y = jnp.concatenate([w[:, 0:1] * y0_lo + w[:, 1:2] * y1_lo, w[:, 0:1] * y0_hi + w[:, 1:2] * y1_hi], axis=1)
    o_ref[...] = _ln(DEEPNORM_ALPHA * h_ref[...] + y, g_ref[...], b_ref[...])


def _combine_call(dest0, dest1, h1, rw, g, b, yb):
    t = h1.shape[0]
    tm = COMB_TM
    n = t // tm
    cur = pl.BlockSpec((tm,), lambda i: (i,), memory_space=pltpu.SMEM)
    nxt = pl.BlockSpec((tm,), lambda i: (jnp.minimum(i + 1, n - 1),), memory_space=pltpu.SMEM)
    return pl.pallas_call(
        _combine_kernel,
        grid=(n,),
        in_specs=[cur, cur, nxt, nxt,
                  pl.BlockSpec((tm, D_MODEL), lambda i: (i, 0)),
                  pl.BlockSpec((tm, 8), lambda i: (i, 0)),
                  pl.BlockSpec((1, D_MODEL), lambda i: (0, 0)),
                  pl.BlockSpec((1, D_MODEL), lambda i: (0, 0)),
                  pl.BlockSpec(memory_space=pl.ANY)],
        out_specs=pl.BlockSpec((tm, D_MODEL), lambda i: (i, 0)),
        out_shape=jax.ShapeDtypeStruct((t, D_MODEL), F32),
        scratch_shapes=[pltpu.VMEM((2, 2, tm, HALF_D), jnp.uint32), pltpu.SemaphoreType.DMA((2,))],
        compiler_params=pltpu.CompilerParams(dimension_semantics=("arbitrary",),
                                             vmem_limit_bytes=VMEM_LIMIT),
        name="combine",
    )(dest0, dest1, dest0, dest1, h1, rw, g, b, yb)


def _sc_gather_rows(table, idx):
    m = idx.shape[0]
    width = table.shape[1]
    sc = plsc.get_sparse_core_info()
    workers = sc.num_cores * sc.num_subcores
    per_worker = m // workers
    mesh = plsc.VectorSubcoreMesh(core_axis_name="c", subcore_axis_name="s")

    @functools.partial(pl.kernel, mesh=mesh,
                       out_type=jax.ShapeDtypeStruct((m, width), table.dtype),
                       scratch_types=[pltpu.VMEM((SC_WINDOW,), jnp.int32),
                                      pltpu.VMEM((SC_WINDOW, width), table.dtype),
                                      pltpu.SemaphoreType.DMA])
    def gather(table_hbm, idx_hbm, out_hbm, idx_v, rows_v, sem):
        base = (lax.axis_index("s") * sc.num_cores + lax.axis_index("c")) * per_worker

        @pl.loop(0, per_worker // SC_WINDOW)
        def _(step):
            off = base + step * SC_WINDOW
            pltpu.sync_copy(idx_hbm.at[pl.ds(off, SC_WINDOW)], idx_v)
            pltpu.async_copy(table_hbm.at[idx_v], rows_v, sem).wait()
            pltpu.sync_copy(rows_v, out_hbm.at[pl.ds(off, SC_WINDOW)])

    return gather(table, idx)


def _sc_scatter_rows(rows, idx0, idx1, n_out):
    m, width = rows.shape
    sc = plsc.get_sparse_core_info()
    workers = sc.num_cores * sc.num_subcores
    per_worker = m // workers
    mesh = plsc.VectorSubcoreMesh(core_axis_name="c", subcore_axis_name="s")

    @functools.partial(pl.kernel, mesh=mesh,
                       out_type=jax.ShapeDtypeStruct((n_out, width), rows.dtype),
                       scratch_types=[pltpu.VMEM((SC_WINDOW,), jnp.int32), pltpu.VMEM((SC_WINDOW,), jnp.int32),
                                      pltpu.VMEM((SC_WINDOW, width), rows.dtype),
                                      pltpu.SemaphoreType.DMA, pltpu.SemaphoreType.DMA])
    def scatter(rows_hbm, idx0_hbm, idx1_hbm, out_hbm, idx0_v, idx1_v, rows_v, sem0, sem1):
        base = (lax.axis_index("s") * sc.num_cores + lax.axis_index("c")) * per_worker

        @pl.loop(0, per_worker // SC_WINDOW)
        def _(step):
            off = base + step * SC_WINDOW
            pltpu.sync_copy(rows_hbm.at[pl.ds(off, SC_WINDOW)], rows_v)
            pltpu.sync_copy(idx0_hbm.at[pl.ds(off, SC_WINDOW)], idx0_v)
            pltpu.sync_copy(idx1_hbm.at[pl.ds(off, SC_WINDOW)], idx1_v)
            first = pltpu.async_copy(rows_v, out_hbm.at[idx0_v], sem0)
            second = pltpu.async_copy(rows_v, out_hbm.at[idx1_v], sem1)
            first.wait()
            second.wait()

    return scatter(rows, idx0, idx1)


def _combine_rows_kernel(h_ref, w_ref, g_ref, b_ref, y0_ref, y1_ref, o_ref):
    w = w_ref[...]
    y0_lo, y0_hi = _unpack_bf16_pair(y0_ref[...])
    y1_lo, y1_hi = _unpack_bf16_pair(y1_ref[...])
    y = jnp.concatenate([w[:, 0:1] * y0_lo + w[:, 1:2] * y1_lo, w[:, 0:1] * y0_hi + w[:, 1:2] * y1_hi], axis=1)
    o_ref[...] = _ln(DEEPNORM_ALPHA * h_ref[...] + y, g_ref[...], b_ref[...])


def _combine_rows_call(h1, rw, g, b, yg):
    t = h1.shape[0]
    tm = COMB_TM
    n = t // tm
    return pl.pallas_call(
        _combine_rows_kernel,
        grid=(n,),
        in_specs=[pl.BlockSpec((tm, D_MODEL), lambda i: (i, 0)),
                  pl.BlockSpec((tm, 8), lambda i: (i, 0)),
                  pl.BlockSpec((1, D_MODEL), lambda i: (0, 0)),
                  pl.BlockSpec((1, D_MODEL), lambda i: (0, 0)),
                  pl.BlockSpec((tm, HALF_D), lambda i: (i, 0)),
                  pl.BlockSpec((tm, HALF_D), lambda i: (n + i, 0))],
        out_specs=pl.BlockSpec((tm, D_MODEL), lambda i: (i, 0)),
        out_shape=jax.ShapeDtypeStruct((t, D_MODEL), F32),
        compiler_params=pltpu.CompilerParams(dimension_semantics=("arbitrary",),
                                             vmem_limit_bytes=VMEM_LIMIT),
        name="combine_rows",
    )(h1, rw, g, b, yg, yg)


def _take_cols(w, idx):
    wz = jnp.concatenate([w, jnp.zeros((w.shape[0], 1), w.dtype)], axis=1)
    return wz[:, np.where(idx < 0, w.shape[1], idx)]


def _layout_indices():
    lane_src = np.full((HEAD_PAD,), -1, np.int64)
    lane_src[0:ROPE_HALF] = QK_NOPE + np.arange(ROPE_HALF)
    lane_src[ROPE_HALF:X2_LANE] = np.arange(X2_LANE - ROPE_HALF)
    lane_src[X2_LANE:X2_LANE + ROPE_HALF] = QK_NOPE + ROPE_HALF + np.arange(ROPE_HALF)
    n_lo = X2_LANE - ROPE_HALF
    lane_src[X2_LANE + ROPE_HALF:X2_LANE + ROPE_HALF + QK_NOPE - n_lo] = n_lo + np.arange(QK_NOPE - n_lo)
    q_idx = np.concatenate([np.where(lane_src >= 0, lane_src + (QK_NOPE + QK_ROPE) * h, -1)
                            for h in range(N_HEADS)])
    nope_src = np.where(lane_src < QK_NOPE, lane_src, -1)
    k_idx = np.concatenate([np.where(nope_src >= 0, nope_src + (QK_NOPE + V_DIM) * h, -1)
                            for h in range(N_HEADS)])
    v_idx = np.concatenate([QK_NOPE + np.arange(V_DIM) + (QK_NOPE + V_DIM) * h for h in range(N_HEADS)])
    kr_idx = np.full((LANES,), -1, np.int64)
    kr_idx[0:ROPE_HALF] = np.arange(ROPE_HALF)
    kr_idx[X2_LANE:X2_LANE + ROPE_HALF] = ROPE_HALF + np.arange(ROPE_HALF)
    return q_idx, k_idx, v_idx, kr_idx


def _rope_lane_tables(length):
    inv_freq = np.float32(ROPE_THETA) ** (-np.arange(ROPE_HALF, dtype=np.float32) / np.float32(ROPE_HALF))
    ang = np.arange(length, dtype=np.float32)[:, None] * inv_freq[None, :].astype(np.float32)
    cos, sin = np.cos(ang).astype(np.float32), np.sin(ang).astype(np.float32)
    ctab = np.ones((length, LANES), np.float32)
    stab = np.zeros((length, LANES), np.float32)
    for lo, sign in ((0, -1.0), (X2_LANE, 1.0)):
        ctab[:, lo:lo + ROPE_HALF] = cos
        stab[:, lo:lo + ROPE_HALF] = sign * sin
    return jnp.asarray(ctab), jnp.asarray(stab)


def kernel(x, meta_tokens, ln_in_g, ln_in_b, w_in, conv_w, conv_b, conv_ln_g, conv_ln_b, q_norm_g, w_uq,
           kv_norm_g, w_ukv, conv_out_g, attn_out_g, w_o, ln1_g, ln1_b, w_rg, b_rg, w_re, b_re,
           w_gate, w_up, w_down, ln2_g, ln2_b):
    assert w_in.shape[0] == 1, "single layer"
    bsz, seq, d = x.shape
    t = bsz * seq
    row = lambda a: a.reshape(1, -1)

    q_idx, k_idx, v_idx, kr_idx = _layout_indices()
    c4 = 2 * C_CONV + Q_LORA + KV_LORA
    w_in_l = jnp.concatenate([w_in[0][:, :c4], _take_cols(w_in[0][:, c4:], kr_idx)], axis=1).astype(BF16)
    wuq_l = _take_cols(w_uq[0], q_idx).astype(BF16)
    wuk_l = _take_cols(w_ukv[0], k_idx).astype(BF16)
    wvt_l = w_ukv[0][:, v_idx].T.astype(BF16)
    wo_l = w_o[0].astype(BF16)
    n_route_pad = ROUTE_ROWS - N_EXPERTS - N_GROUPS
    wr_t = jnp.concatenate([w_re[0].T, w_rg[0].T, jnp.zeros((n_route_pad, d), F32)], axis=0)
    wrh_l = wr_t.astype(BF16)
    wrl_l = (wr_t - wrh_l.astype(F32)).astype(BF16)
    br_l = jnp.broadcast_to(jnp.concatenate([b_re[0], b_rg[0], jnp.zeros((n_route_pad,), F32)])[:, None],
                            (ROUTE_ROWS, LANES))
    ctab, stab = _rope_lane_tables(N_META + seq)

    proj_w = (row(ln_in_g), row(ln_in_b), w_in_l, row(q_norm_g[0]), wuq_l, row(kv_norm_g[0]), wuk_l, wvt_l)
    u_m, _, k_m, vt_m = _proj_call(meta_tokens.astype(F32), N_META, 1, *proj_w, ctab[:N_META], stab[:N_META])
    u, q, k, vt = _proj_call(x.reshape(t, d), PROJ_TM, seq // PROJ_TM, *proj_w, ctab[N_META:], stab[N_META:])

    o = _attn_call(q.reshape(bsz, seq, -1), k.reshape(bsz, seq, -1),
                   vt.reshape(bsz, seq // ATT_T, D_ATTN, ATT_T), k_m, vt_m)

    pre = jnp.concatenate([jnp.zeros((HALO - N_META, C_CONV), F32), u_m], axis=0)
    cw = jnp.concatenate([conv_w[0], jnp.zeros((HALO - CONV_K, C_CONV), F32)], axis=0)
    h1, hp, ri, rw, cnt = _mix_call(x, row(ln_in_g), row(ln_in_b), u.reshape(bsz, seq, C_CONV), pre, cw,
                                    row(conv_b[0]), row(conv_ln_g[0]), row(conv_ln_b[0]), row(conv_out_g[0]),
                                    o, row(attn_out_g[0]), wo_l, row(ln1_g[0]), row(ln1_b[0]), wrh_l, wrl_l, br_l)

    counts = cnt[:N_EXPERTS, 0].astype(jnp.int32)
    padded = (counts + MOE_BM - 1) // MOE_BM * MOE_BM
    pad_end = jnp.cumsum(padded)
    pad_start = pad_end - padded
    n_blocks = (2 * t) // MOE_BM + N_EXPERTS
    dest, blk = _tables_call(pad_start, pad_end, counts, ri, n_blocks)
    blk_e, blk_valid = blk[0, :n_blocks], blk[1, :n_blocks]

    h1f = h1.reshape(t, d)
    dest0, dest1 = dest[0], dest[1]
    xb = _sc_scatter_rows(hp.reshape(t, HALF_D), dest0, dest1, n_blocks * MOE_BM)
    yb = _expert_call(blk_e, blk_valid, xb, w_gate, w_up, w_down)
    yg = _sc_gather_rows(yb, jnp.concatenate([dest0, dest1]))
    out = _combine_rows_call(h1f, rw.reshape(t, 8), row(ln2_g[0]), row(ln2_b[0]), yg)
    return out.reshape(bsz, seq, d)
```

```python
import functools
import math

import numpy as np
import jax
import jax.numpy as jnp
from jax import lax
from jax.experimental import pallas as pl
from jax.experimental.pallas import tpu as pltpu
from jax.experimental.pallas import tpu_sc as plsc

F32 = jnp.float32
BF16 = jnp.bfloat16

D_MODEL = 1024
HALF_D = D_MODEL // 2
N_META = 16
CHUNK = 64
C_CONV = 512
CONV_K = 31
N_HEADS = 8
QK_NOPE = 64
QK_ROPE = 32
V_DIM = 64
Q_LORA = 384
KV_LORA = 256
ROPE_THETA = 10000.0
D_ATTN = N_HEADS * V_DIM
N_GROUPS = 4
EXPERTS_PER_GROUP = 8
N_EXPERTS = N_GROUPS * EXPERTS_PER_GROUP
D_EXPERT = 256
DEEPNORM_ALPHA = 2.0 ** 0.25
EPS = 1e-5
NEG_INF = -1e30

LANES = 128
SUBLANES = 8
HEAD_PAD = LANES
ROPE_HALF = QK_ROPE // 2
X2_LANE = 64
Z_COLS = 2 * C_CONV + Q_LORA + KV_LORA + LANES
Q_SCALE = (QK_NOPE + QK_ROPE) ** -0.5 * math.log2(math.e)

PROJ_TM = 1024
ATT_T = 256
ATT_HEADS = 8
MIX_TM = 512
ROUTE_ROWS = 48
HALO = 32
MOE_BM = 512
COMB_TM = 1024
SC_WINDOW = 128
VMEM_LIMIT = 56 * 1024 * 1024


def _ln(x, g, b):
    mu = jnp.mean(x, axis=-1, keepdims=True)
    xc = x - mu
    var = jnp.mean(xc * xc, axis=-1, keepdims=True)
    return xc * lax.rsqrt(var + EPS) * g + b


def _rms(x, g):
    return x * lax.rsqrt(jnp.mean(x * x, axis=-1, keepdims=True) + EPS) * g


def _sigmoid(x):
    return 1.0 / (1.0 + jnp.exp(-x))


def _pack_bf16_pair(lo, hi):
    lo_bits = lax.bitcast_convert_type(lo.astype(BF16).astype(F32), jnp.uint32)
    hi_bits = lax.bitcast_convert_type(hi.astype(BF16).astype(F32), jnp.uint32)
    return (lo_bits >> 16) | (hi_bits & jnp.uint32(0xFFFF0000))


def _unpack_bf16_pair(packed):
    lo = lax.bitcast_convert_type(packed << 16, F32)
    hi = lax.bitcast_convert_type(packed & jnp.uint32(0xFFFF0000), F32)
    return lo, hi


def _proj_kernel(x_ref, g_ref, b_ref, win_ref, qg_ref, wuq_ref, kvg_ref, wuk_ref, wvt_ref, c_ref, s_ref,
                 u_ref, q_ref, k_ref, vt_ref):
    h = _ln(x_ref[...], g_ref[...], b_ref[...])
    z = jnp.dot(h.astype(BF16), win_ref[...], preferred_element_type=F32)
    u_ref[...] = z[:, :C_CONV] * _sigmoid(z[:, C_CONV:2 * C_CONV])
    c0 = 2 * C_CONV
    cq = z[:, c0:c0 + Q_LORA]
    ckv = z[:, c0 + Q_LORA:c0 + Q_LORA + KV_LORA]
    kr = z[:, c0 + Q_LORA + KV_LORA:]
    cs = c_ref[...]
    sn = s_ref[...]
    q = jnp.dot(_rms(cq, qg_ref[...]).astype(BF16), wuq_ref[...], preferred_element_type=F32)
    kvn = _rms(ckv, kvg_ref[...]).astype(BF16)
    kn = jnp.dot(kvn, wuk_ref[...], preferred_element_type=F32)
    kr_rot = kr * cs + pltpu.roll(kr, X2_LANE, 1) * sn
    for hh in range(N_HEADS):
        sl = slice(HEAD_PAD * hh, HEAD_PAD * (hh + 1))
        qh = q[:, sl]
        qh = (qh * cs + pltpu.roll(qh, X2_LANE, 1) * sn) * Q_SCALE
        q_ref[:, sl] = qh.astype(BF16)
        k_ref[:, sl] = (kn[:, sl] + kr_rot).astype(BF16)
    vt = lax.dot_general(wvt_ref[...], kvn, (((1,), (1,)), ((), ())), preferred_element_type=F32)
    n_sub, sub_w = vt_ref.shape[1], vt_ref.shape[3]
    for tt in range(n_sub):
        vt_ref[0, tt] = vt[:, tt * sub_w:(tt + 1) * sub_w].astype(BF16)


def _proj_call(x2d, tm, pos_blocks, ln_g, ln_b, w_in, qg, wuq, kvg, wuk, wvt, ctab, stab):
    t = x2d.shape[0]
    sub_w = min(tm, ATT_T)
    n_sub = tm // sub_w
    full = lambda shape: pl.BlockSpec(shape, lambda i: (0,) * len(shape))
    row = lambda w: pl.BlockSpec((tm, w), lambda i: (i, 0))
    tab = pl.BlockSpec((tm, LANES), lambda i: (i % pos_blocks, 0))
    return pl.pallas_call(
        _proj_kernel,
        grid=(t // tm,),
        in_specs=[row(D_MODEL), full((1, D_MODEL)), full((1, D_MODEL)), full((D_MODEL, Z_COLS)),
                  full((1, Q_LORA)), full((Q_LORA, N_HEADS * HEAD_PAD)),
                  full((1, KV_LORA)), full((KV_LORA, N_HEADS * HEAD_PAD)), full((D_ATTN, KV_LORA)), tab, tab],
        out_specs=[row(C_CONV), row(N_HEADS * HEAD_PAD), row(N_HEADS * HEAD_PAD),
                   pl.BlockSpec((1, n_sub, D_ATTN, sub_w), lambda i: (i, 0, 0, 0))],
        out_shape=[jax.ShapeDtypeStruct((t, C_CONV), F32),
                   jax.ShapeDtypeStruct((t, N_HEADS * HEAD_PAD), BF16),
                   jax.ShapeDtypeStruct((t, N_HEADS * HEAD_PAD), BF16),
                   jax.ShapeDtypeStruct((t // tm, n_sub, D_ATTN, sub_w), BF16)],
        compiler_params=pltpu.CompilerParams(dimension_semantics=("arbitrary",),
                                             vmem_limit_bytes=VMEM_LIMIT),
        name="proj",
    )(x2d, ln_g, ln_b, w_in, qg, wuq, kvg, wuk, wvt, ctab, stab)


def _attn_kernel(q_ref, k_ref, vt_ref, km_ref, vtm_ref, o_ref, s_sc):
    seq = q_ref.shape[1]
    nq = seq // ATT_T
    key_chunk = lax.broadcasted_iota(jnp.int32, (ATT_T, ATT_T), 0) // CHUNK
    qry_chunk = lax.broadcasted_iota(jnp.int32, (ATT_T, ATT_T), 1) // CHUNK
    diag_visible = key_chunk <= qry_chunk
    nt = (((1,), (1,)), ((), ()))
    heads = range(ATT_HEADS)
    hsl = [slice(HEAD_PAD * hh, HEAD_PAD * (hh + 1)) for hh in heads]
    vsl = [slice(V_DIM * hh, V_DIM * (hh + 1)) for hh in heads]

    def update(s, vt, m, l, acc):
        m_new = jnp.maximum(m, jnp.max(s, axis=0, keepdims=True))
        alpha = jnp.exp2(m - m_new)
        p = jnp.exp2(s - m_new)
        l = alpha * l + jnp.sum(p, axis=0, keepdims=True)
        acc = alpha * acc + jnp.dot(vt, p.astype(BF16), preferred_element_type=F32)
        return m_new, l, acc

    def scores(qi, j, hh):
        q0 = pl.multiple_of(qi * ATT_T, ATT_T)
        k0 = pl.multiple_of(j * ATT_T, ATT_T)
        return lax.dot_general(k_ref[0, pl.ds(k0, ATT_T), hsl[hh]], q_ref[0, pl.ds(q0, ATT_T), hsl[hh]], nt,
                               preferred_element_type=F32)

    for hh in heads:
        s_sc[0, hh] = scores(0, 0, hh)

    def q_body(qi, carry):
        row0 = pl.multiple_of(qi * ATT_T, ATT_T)
        slot = qi % 2
        state = []
        for hh in heads:
            state += [jnp.full((1, ATT_T), NEG_INF, F32), jnp.zeros((1, ATT_T), F32),
                      jnp.zeros((V_DIM, ATT_T), F32)]

        def kv_body(j, st):
            out = []
            for hh in heads:
                s_cur = s_sc[slot, hh]
                s_next = scores(qi, j + 1, hh)
                out += update(s_cur, vt_ref[0, j, vsl[hh], :], *st[3 * hh:3 * hh + 3])
                s_sc[slot, hh] = s_next
            return tuple(out)

        state = lax.fori_loop(0, qi, kv_body, tuple(state))
        q_next = jnp.minimum(qi + 1, nq - 1)
        sms = [lax.dot_general(km_ref[:, hsl[hh]], q_ref[0, pl.ds(row0, ATT_T), hsl[hh]], nt,
                               preferred_element_type=F32) for hh in heads]
        for hh in heads:
            s_sc[1 - slot, hh] = scores(q_next, 0, hh)
        outs = []
        for hh in heads:
            m, l, acc = state[3 * hh:3 * hh + 3]
            s = jnp.where(diag_visible, s_sc[slot, hh], NEG_INF)
            sm = sms[hh]
            m_new = jnp.maximum(m, jnp.maximum(jnp.max(s, axis=0, keepdims=True),
                                               jnp.max(sm, axis=0, keepdims=True)))
            alpha = jnp.exp2(m - m_new)
            p = jnp.exp2(s - m_new)
            pm = jnp.exp2(sm - m_new)
            l = alpha * l + jnp.sum(p, axis=0, keepdims=True) + jnp.sum(pm, axis=0, keepdims=True)
            acc = (alpha * acc + jnp.dot(vt_ref[0, qi, vsl[hh], :], p.astype(BF16), preferred_element_type=F32)
                   + jnp.dot(vtm_ref[0, 0, vsl[hh], :], pm.astype(BF16), preferred_element_type=F32))
            outs.append(acc / l)
        o_ref[0, pl.ds(row0, ATT_T), :] = jnp.concatenate(outs, axis=0).T
        return carry

    lax.fori_loop(0, nq, q_body, 0)


def _attn_call(q, k, vt, k_meta, vt_meta):
    b, seq, _ = q.shape
    nk = seq // ATT_T
    hb = ATT_HEADS
    return pl.pallas_call(
        _attn_kernel,
        grid=(b, N_HEADS // hb),
        in_specs=[pl.BlockSpec((1, seq, hb * HEAD_PAD), lambda i, j: (i, 0, j)),
                  pl.BlockSpec((1, seq, hb * HEAD_PAD), lambda i, j: (i, 0, j)),
                  pl.BlockSpec((1, nk, hb * V_DIM, ATT_T), lambda i, j: (i, 0, j, 0)),
                  pl.BlockSpec((N_META, hb * HEAD_PAD), lambda i, j: (0, j)),
                  pl.BlockSpec((1, 1, hb * V_DIM, N_META), lambda i, j: (0, 0, j, 0))],
        out_specs=pl.BlockSpec((1, seq, hb * V_DIM), lambda i, j: (i, 0, j)),
        out_shape=jax.ShapeDtypeStruct((b, seq, D_ATTN), F32),
        scratch_shapes=[pltpu.VMEM((2, hb, ATT_T, ATT_T), F32)],
        compiler_params=pltpu.CompilerParams(dimension_semantics=("arbitrary", "arbitrary"),
                                             vmem_limit_bytes=VMEM_LIMIT),
        name="attn",
    )(q, k, vt, k_meta, vt_meta)


def _mix_kernel(x_ref, ig_ref, ib_ref, u_ref, uh_ref, pre_ref, cw_ref, cb_ref, clg_ref, clb_ref,
                cog_ref, o_ref, aog_ref, wo_ref, l1g_ref, l1b_ref, wrh_ref, wrl_ref, br_ref,
                h1_ref, hp_ref, ri_ref, rw_ref, cnt_ref, win_sc, sh_sc, carry_sc):
    bi = pl.program_id(0)
    ti = pl.program_id(1)
    tm = MIX_TM

    @pl.when((bi == 0) & (ti == 0))
    def _():
        carry_sc[...] = jnp.zeros_like(carry_sc)

    win_sc[0:HALO, :] = jnp.where(ti == 0, pre_ref[...], uh_ref[0])
    win_sc[HALO:, :] = u_ref[0]
    first = HALO - (CONV_K - 1)
    acc = jnp.broadcast_to(cb_ref[...], (tm, C_CONV))
    for res in range(SUBLANES):
        offs = [o for o in range(first, first + CONV_K) if o % SUBLANES == res]
        rows = tm + offs[-1] - res
        if res:
            sh_sc[res - 1, 0:rows, :] = win_sc[pl.ds(res, rows), :]
        for o in offs:
            src = sh_sc[res - 1, o - res:o - res + tm, :] if res else win_sc[o:o + tm, :]
            acc = acc + src * cw_ref[o - first:o - first + 1, :]
    c = _ln(acc, clg_ref[...], clb_ref[...])
    c = c * _sigmoid(c)
    cn = _rms(c, cog_ref[...])
    on = _rms(o_ref[0], aog_ref[...])
    y = (jnp.dot(cn.astype(BF16), wo_ref[0:C_CONV, :], preferred_element_type=F32)
         + jnp.dot(on.astype(BF16), wo_ref[C_CONV:, :], preferred_element_type=F32))
    h = _ln(x_ref[0], ig_ref[...], ib_ref[...])
    h1 = _ln(DEEPNORM_ALPHA * h + y, l1g_ref[...], l1b_ref[...])
    h1_ref[0] = h1
    hp_ref[0] = _pack_bf16_pair(h1[:, :HALF_D], h1[:, HALF_D:])

    nt = (((1,), (1,)), ((), ()))
    h_hi = h1.astype(BF16)
    h_lo = (h1 - h_hi.astype(F32)).astype(BF16)
    lt = (lax.dot_general(wrh_ref[...], h_hi, nt, preferred_element_type=F32)
          + lax.dot_general(wrl_ref[...], h_hi, nt, preferred_element_type=F32)
          + lax.dot_general(wrh_ref[...], h_lo, nt, preferred_element_type=F32)) + br_ref[:, 0:1]
    row = lax.broadcasted_iota(jnp.int32, (ROUTE_ROWS, tm), 0)
    big = jnp.int32(1 << 20)
    is_g = (row >= N_EXPERTS) & (row < N_EXPERTS + N_GROUPS)
    gl = jnp.where(is_g, lt, -jnp.inf)
    gmax = jnp.max(gl, axis=0, keepdims=True)
    g_idx = jnp.min(jnp.where(gl == gmax, row - N_EXPERTS, big), axis=0, keepdims=True)
    g_w = 1.0 / jnp.sum(jnp.exp(gl - gmax), axis=0, keepdims=True)
    in_grp = (row < N_EXPERTS) & ((row // EXPERTS_PER_GROUP) == g_idx)
    el = jnp.where(in_grp, lt, -jnp.inf)
    v1 = jnp.max(el, axis=0, keepdims=True)
    i1 = jnp.min(jnp.where(el == v1, row, big), axis=0, keepdims=True)
    el2 = jnp.where(row == i1, -jnp.inf, el)
    v2 = jnp.max(el2, axis=0, keepdims=True)
    i2 = jnp.min(jnp.where(el2 == v2, row, big), axis=0, keepdims=True)
    e2 = jnp.exp(v2 - v1)
    w1 = g_w / (1.0 + e2)
    w2 = g_w * e2 / (1.0 + e2)

    oh1 = row == i1
    oh2 = row == i2
    oh = jnp.where(oh1 | oh2, 1.0, 0.0)
    earlier = (lax.broadcasted_iota(jnp.int32, (tm, tm), 0) < lax.broadcasted_iota(jnp.int32, (tm, tm), 1))
    before = (jnp.dot(oh.astype(BF16), jnp.where(earlier, 1.0, 0.0).astype(BF16), preferred_element_type=F32)
              + carry_sc[:, 0:1])
    r1 = jnp.sum(jnp.where(oh1, before, 0.0), axis=0, keepdims=True).astype(jnp.int32)
    r2 = jnp.sum(jnp.where(oh2, before, 0.0), axis=0, keepdims=True).astype(jnp.int32)
    carry_sc[...] = carry_sc[...] + jnp.sum(oh, axis=1, keepdims=True)
    cnt_ref[...] = carry_sc[...]

    sub = lax.broadcasted_iota(jnp.int32, (SUBLANES, tm), 0)
    ri_ref[...] = jnp.where(sub == 0, i1, jnp.where(sub == 1, i2, jnp.where(sub == 2, r1,
                            jnp.where(sub == 3, r2, 0))))
    lrow = lax.broadcasted_iota(jnp.int32, (LANES, tm), 0)
    rw_ref[0] = jnp.where(lrow == 0, w1, jnp.where(lrow == 1, w2, 0.0)).T[:, :8]


def _mix_call(x, ig, ib, u, pre, cw, cb, clg, clb, cog, o, aog, wo, l1g, l1b, wrh, wrl, br):
    b, seq, _ = x.shape
    tm = MIX_TM
    nt = seq // tm
    full = lambda shape: pl.BlockSpec(shape, lambda i, j: (0,) * len(shape))
    tile = lambda w: pl.BlockSpec((1, tm, w), lambda i, j: (i, j, 0))
    halo = pl.BlockSpec((1, HALO, C_CONV), lambda i, j: (i, jnp.maximum(j * (tm // HALO) - 1, 0), 0))
    return pl.pallas_call(
        _mix_kernel,
        grid=(b, nt),
        in_specs=[tile(D_MODEL), full((1, D_MODEL)), full((1, D_MODEL)),
                  tile(C_CONV), halo, full((HALO, C_CONV)), full((HALO, C_CONV)), full((1, C_CONV)),
                  full((1, C_CONV)), full((1, C_CONV)), full((1, C_CONV)),
                  tile(D_ATTN), full((1, D_ATTN)), full((D_MODEL, D_MODEL)),
                  full((1, D_MODEL)), full((1, D_MODEL)), full((ROUTE_ROWS, D_MODEL)),
                  full((ROUTE_ROWS, D_MODEL)), full((ROUTE_ROWS, LANES))],
        out_specs=[tile(D_MODEL), tile(HALF_D), pl.BlockSpec((8, tm), lambda i, j: (0, i * nt + j)), tile(8),
                   full((ROUTE_ROWS, LANES))],
        out_shape=[jax.ShapeDtypeStruct((b, seq, D_MODEL), F32),
                   jax.ShapeDtypeStruct((b, seq, HALF_D), jnp.uint32),
                   jax.ShapeDtypeStruct((8, b * seq), jnp.int32),
                   jax.ShapeDtypeStruct((b, seq, 8), F32),
                   jax.ShapeDtypeStruct((ROUTE_ROWS, LANES), F32)],
        scratch_shapes=[pltpu.VMEM((HALO + tm, C_CONV), F32),
                        pltpu.VMEM((SUBLANES - 1, HALO + tm, C_CONV), F32),
                        pltpu.VMEM((ROUTE_ROWS, LANES), F32)],
        compiler_params=pltpu.CompilerParams(dimension_semantics=("arbitrary", "arbitrary"),
                                             vmem_limit_bytes=VMEM_LIMIT),
        name="mix",
    )(x, ig, ib, u, u, pre, cw, cb, clg, clb, cog, o, aog, wo, l1g, l1b, wrh, wrl, br)


def _tables_kernel(ps_ref, pe_ref, cn_ref, ri_ref, dest_ref, blk_ref):
    ri = ri_ref[...]
    start = jnp.zeros_like(ri)
    for e in range(N_EXPERTS):
        start = jnp.where(ri == e, ps_ref[e], start)
    dest_ref[...] = jnp.zeros_like(dest_ref)
    dest_ref[0:2, :] = start[0:2, :] + ri_ref[2:4, :]
    blk_start = lax.broadcasted_iota(jnp.int32, blk_ref.shape, 1) * MOE_BM
    be = jnp.zeros(blk_ref.shape, jnp.int32)
    for e in range(N_EXPERTS):
        be = be + jnp.where(pe_ref[e] <= blk_start, 1, 0)
    be = jnp.minimum(be, N_EXPERTS - 1)
    end = jnp.zeros(blk_ref.shape, jnp.int32)
    for e in range(N_EXPERTS):
        end = jnp.where(be == e, ps_ref[e] + cn_ref[e], end)
    valid = jnp.clip(end - blk_start, 0, MOE_BM)
    row = lax.broadcasted_iota(jnp.int32, blk_ref.shape, 0)
    blk_ref[...] = jnp.where(row == 0, be, valid)


def _tables_call(pad_start, pad_end, counts, ri, n_blocks):
    t = ri.shape[1]
    blk_w = -(-n_blocks // LANES) * LANES
    return pl.pallas_call(
        _tables_kernel,
        grid_spec=pltpu.PrefetchScalarGridSpec(
            num_scalar_prefetch=3,
            grid=(1,),
            in_specs=[pl.BlockSpec((SUBLANES, t), lambda i, *_: (0, 0))],
            out_specs=[pl.BlockSpec((SUBLANES, t), lambda i, *_: (0, 0)),
                       pl.BlockSpec((SUBLANES, blk_w), lambda i, *_: (0, 0))]),
        out_shape=[jax.ShapeDtypeStruct((SUBLANES, t), jnp.int32),
                   jax.ShapeDtypeStruct((SUBLANES, blk_w), jnp.int32)],
        compiler_params=pltpu.CompilerParams(dimension_semantics=("arbitrary",),
                                             vmem_limit_bytes=VMEM_LIMIT),
        name="tables",
    )(pad_start, pad_end, counts, ri)


def _expert_kernel(be_ref, nv_ref, x_ref, wg_ref, wu_ref, wd_ref, y_ref, wg_sc, wu_sc, wd_sc):
    i = pl.program_id(0)
    nv = nv_ref[i]

    @pl.when((i == 0) | (be_ref[i] != be_ref[jnp.maximum(i - 1, 0)]))
    def _():
        wg_sc[...] = wg_ref[0, 0].astype(BF16)
        wu_sc[...] = wu_ref[0, 0].astype(BF16)
        wd_sc[...] = wd_ref[0, 0].astype(BF16)

    @pl.when(nv > 0)
    def _():
        rowid = lax.broadcasted_iota(jnp.int32, x_ref.shape, 0)
        x_lo, x_hi = _unpack_bf16_pair(jnp.where(rowid < nv, x_ref[...], jnp.uint32(0)))
        x_lo = x_lo.astype(BF16)
        x_hi = x_hi.astype(BF16)
        g = (jnp.dot(x_lo, wg_sc[:HALF_D, :], preferred_element_type=F32)
             + jnp.dot(x_hi, wg_sc[HALF_D:, :], preferred_element_type=F32))
        up = (jnp.dot(x_lo, wu_sc[:HALF_D, :], preferred_element_type=F32)
              + jnp.dot(x_hi, wu_sc[HALF_D:, :], preferred_element_type=F32))
        act = g * _sigmoid(g) * up
        y = jnp.dot(act.astype(BF16), wd_sc[...], preferred_element_type=F32)
        y_ref[...] = _pack_bf16_pair(y[:, :HALF_D], y[:, HALF_D:])

    @pl.when(nv == 0)
    def _():
        y_ref[...] = jnp.zeros_like(y_ref)


def _expert_call(blk_e, blk_valid, xb, w_gate, w_up, w_down):
    r = xb.shape[0]
    bm = MOE_BM
    wspec = lambda a, b: pl.BlockSpec((1, 1, a, b), lambda i, be, nv: (0, be[i], 0, 0))
    return pl.pallas_call(
        _expert_kernel,
        grid_spec=pltpu.PrefetchScalarGridSpec(
            num_scalar_prefetch=2,
            grid=(r // bm,),
            in_specs=[pl.BlockSpec((bm, HALF_D), lambda i, be, nv: (i, 0)),
                      wspec(D_MODEL, D_EXPERT), wspec(D_MODEL, D_EXPERT), wspec(D_EXPERT, D_MODEL)],
            out_specs=pl.BlockSpec((bm, HALF_D), lambda i, be, nv: (i, 0)),
            scratch_shapes=[pltpu.VMEM((D_MODEL, D_EXPERT), BF16), pltpu.VMEM((D_MODEL, D_EXPERT), BF16),
                            pltpu.VMEM((D_EXPERT, D_MODEL), BF16)]),
        out_shape=jax.ShapeDtypeStruct((r, HALF_D), jnp.uint32),
        compiler_params=pltpu.CompilerParams(dimension_semantics=("arbitrary",),
                                             vmem_limit_bytes=VMEM_LIMIT),
        name="experts",
    )(blk_e, blk_valid, xb, w_gate, w_up, w_down)


def _sc_gather_rows(table, idx):
    m = idx.shape[0]
    width = table.shape[1]
    sc = plsc.get_sparse_core_info()
    workers = sc.num_cores * sc.num_subcores
    per_worker = m // workers
    mesh = plsc.VectorSubcoreMesh(core_axis_name="c", subcore_axis_name="s")

    @functools.partial(pl.kernel, mesh=mesh,
                       out_type=jax.ShapeDtypeStruct((m, width), table.dtype),
                       scratch_types=[pltpu.VMEM((SC_WINDOW,), jnp.int32),
                                      pltpu.VMEM((SC_WINDOW, width), table.dtype),
                                      pltpu.SemaphoreType.DMA])
    def gather(table_hbm, idx_hbm, out_hbm, idx_v, rows_v, sem):
        base = (lax.axis_index("s") * sc.num_cores + lax.axis_index("c")) * per_worker

        @pl.loop(0, per_worker // SC_WINDOW)
        def _(step):
            off = base + step * SC_WINDOW
            pltpu.sync_copy(idx_hbm.at[pl.ds(off, SC_WINDOW)], idx_v)
            pltpu.async_copy(table_hbm.at[idx_v], rows_v, sem).wait()
            pltpu.sync_copy(rows_v, out_hbm.at[pl.ds(off, SC_WINDOW)])

    return gather(table, idx)


def _sc_scatter_rows(rows, idx0, idx1, n_out):
    m, width = rows.shape
    sc = plsc.get_sparse_core_info()
    workers = sc.num_cores * sc.num_subcores
    per_worker = m // workers
    mesh = plsc.VectorSubcoreMesh(core_axis_name="c", subcore_axis_name="s")

    @functools.partial(pl.kernel, mesh=mesh,
                       out_type=jax.ShapeDtypeStruct((n_out, width), rows.dtype),
                       scratch_types=[pltpu.VMEM((SC_WINDOW,), jnp.int32), pltpu.VMEM((SC_WINDOW,), jnp.int32),
                                      pltpu.VMEM((SC_WINDOW, width), rows.dtype),
                                      pltpu.SemaphoreType.DMA, pltpu.SemaphoreType.DMA])
    def scatter(rows_hbm, idx0_hbm, idx1_hbm, out_hbm, idx0_v, idx1_v, rows_v, sem0, sem1):
        base = (lax.axis_index("s") * sc.num_cores + lax.axis_index("c")) * per_worker

        @pl.loop(0, per_worker // SC_WINDOW)
        def _(step):
            off = base + step * SC_WINDOW
            pltpu.sync_copy(rows_hbm.at[pl.ds(off, SC_WINDOW)], rows_v)
            pltpu.sync_copy(idx0_hbm.at[pl.ds(off, SC_WINDOW)], idx0_v)
            pltpu.sync_copy(idx1_hbm.at[pl.ds(off, SC_WINDOW)], idx1_v)
            first = pltpu.async_copy(rows_v, out_hbm.at[idx0_v], sem0)
            second = pltpu.async_copy(rows_v, out_hbm.at[idx1_v], sem1)
            first.wait()
            second.wait()

    return scatter(rows, idx0, idx1)


def _combine_rows_kernel(h_ref, w_ref, g_ref, b_ref, y0_ref, y1_ref, o_ref):
    w = w_ref[...]
    y0_lo, y0_hi = _unpack_bf16_pair(y0_ref[...])
    y1_lo, y1_hi = _unpack_bf16_pair(y1_ref[...])
    y = jnp.concatenate([w[:, 0:1] * y0_lo + w[:, 1:2] * y1_lo, w[:, 0:1] * y0_hi + w[:, 1:2] * y1_hi], axis=1)
    o_ref[...] = _ln(DEEPNORM_ALPHA * h_ref[...] + y, g_ref[...], b_ref[...])


def _combine_rows_call(h1, rw, g, b, yg):
    t = h1.shape[0]
    tm = COMB_TM
    n = t // tm
    return pl.pallas_call(
        _combine_rows_kernel,
        grid=(n,),
        in_specs=[pl.BlockSpec((tm, D_MODEL), lambda i: (i, 0)),
                  pl.BlockSpec((tm, 8), lambda i: (i, 0)),
                  pl.BlockSpec((1, D_MODEL), lambda i: (0, 0)),
                  pl.BlockSpec((1, D_MODEL), lambda i: (0, 0)),
                  pl.BlockSpec((tm, HALF_D), lambda i: (i, 0)),
                  pl.BlockSpec((tm, HALF_D), lambda i: (n + i, 0))],
        out_specs=pl.BlockSpec((tm, D_MODEL), lambda i: (i, 0)),
        out_shape=jax.ShapeDtypeStruct((t, D_MODEL), F32),
        compiler_params=pltpu.CompilerParams(dimension_semantics=("arbitrary",),
                                             vmem_limit_bytes=VMEM_LIMIT),
        name="combine_rows",
    )(h1, rw, g, b, yg, yg)


def _take_cols(w, idx):
    wz = jnp.concatenate([w, jnp.zeros((w.shape[0], 1), w.dtype)], axis=1)
    return wz[:, np.where(idx < 0, w.shape[1], idx)]


def _layout_indices():
    lane_src = np.full((HEAD_PAD,), -1, np.int64)
    lane_src[0:ROPE_HALF] = QK_NOPE + np.arange(ROPE_HALF)
    lane_src[ROPE_HALF:X2_LANE] = np.arange(X2_LANE - ROPE_HALF)
    lane_src[X2_LANE:X2_LANE + ROPE_HALF] = QK_NOPE + ROPE_HALF + np.arange(ROPE_HALF)
    n_lo = X2_LANE - ROPE_HALF
    lane_src[X2_LANE + ROPE_HALF:X2_LANE + ROPE_HALF + QK_NOPE - n_lo] = n_lo + np.arange(QK_NOPE - n_lo)
    q_idx = np.concatenate([np.where(lane_src >= 0, lane_src + (QK_NOPE + QK_ROPE) * h, -1)
                            for h in range(N_HEADS)])
    nope_src = np.where(lane_src < QK_NOPE, lane_src, -1)
    k_idx = np.concatenate([np.where(nope_src >= 0, nope_src + (QK_NOPE + V_DIM) * h, -1)
                            for h in range(N_HEADS)])
    v_idx = np.concatenate([QK_NOPE + np.arange(V_DIM) + (QK_NOPE + V_DIM) * h for h in range(N_HEADS)])
    kr_idx = np.full((LANES,), -1, np.int64)
    kr_idx[0:ROPE_HALF] = np.arange(ROPE_HALF)
    kr_idx[X2_LANE:X2_LANE + ROPE_HALF] = ROPE_HALF + np.arange(ROPE_HALF)
    return q_idx, k_idx, v_idx, kr_idx


def _rope_lane_tables(length):
    inv_freq = np.float32(ROPE_THETA) ** (-np.arange(ROPE_HALF, dtype=np.float32) / np.float32(ROPE_HALF))
    ang = np.arange(length, dtype=np.float32)[:, None] * inv_freq[None, :].astype(np.float32)
    cos, sin = np.cos(ang).astype(np.float32), np.sin(ang).astype(np.float32)
    ctab = np.ones((length, LANES), np.float32)
    stab = np.zeros((length, LANES), np.float32)
    for lo, sign in ((0, -1.0), (X2_LANE, 1.0)):
        ctab[:, lo:lo + ROPE_HALF] = cos
        stab[:, lo:lo + ROPE_HALF] = sign * sin
    return jnp.asarray(ctab), jnp.asarray(stab)


def kernel(x, meta_tokens, ln_in_g, ln_in_b, w_in, conv_w, conv_b, conv_ln_g, conv_ln_b, q_norm_g, w_uq,
           kv_norm_g, w_ukv, conv_out_g, attn_out_g, w_o, ln1_g, ln1_b, w_rg, b_rg, w_re, b_re,
           w_gate, w_up, w_down, ln2_g, ln2_b):
    assert w_in.shape[0] == 1, "single layer"
    bsz, seq, d = x.shape
    t = bsz * seq
    row = lambda a: a.reshape(1, -1)

    q_idx, k_idx, v_idx, kr_idx = _layout_indices()
    c4 = 2 * C_CONV + Q_LORA + KV_LORA
    w_in_l = jnp.concatenate([w_in[0][:, :c4], _take_cols(w_in[0][:, c4:], kr_idx)], axis=1).astype(BF16)
    wuq_l = _take_cols(w_uq[0], q_idx).astype(BF16)
    wuk_l = _take_cols(w_ukv[0], k_idx).astype(BF16)
    wvt_l = w_ukv[0][:, v_idx].T.astype(BF16)
    wo_l = w_o[0].astype(BF16)
    n_route_pad = ROUTE_ROWS - N_EXPERTS - N_GROUPS
    wr_t = jnp.concatenate([w_re[0].T, w_rg[0].T, jnp.zeros((n_route_pad, d), F32)], axis=0)
    wrh_l = wr_t.astype(BF16)
    wrl_l = (wr_t - wrh_l.astype(F32)).astype(BF16)
    br_l = jnp.broadcast_to(jnp.concatenate([b_re[0], b_rg[0], jnp.zeros((n_route_pad,), F32)])[:, None],
                            (ROUTE_ROWS, LANES))
    ctab, stab = _rope_lane_tables(N_META + seq)

    proj_w = (row(ln_in_g), row(ln_in_b), w_in_l, row(q_norm_g[0]), wuq_l, row(kv_norm_g[0]), wuk_l, wvt_l)
    u_m, _, k_m, vt_m = _proj_call(meta_tokens.astype(F32), N_META, 1, *proj_w, ctab[:N_META], stab[:N_META])
    u, q, k, vt = _proj_call(x.reshape(t, d), PROJ_TM, seq // PROJ_TM, *proj_w, ctab[N_META:], stab[N_META:])

    o = _attn_call(q.reshape(bsz, seq, -1), k.reshape(bsz, seq, -1),
                   vt.reshape(bsz, seq // ATT_T, D_ATTN, ATT_T), k_m, vt_m)

    pre = jnp.concatenate([jnp.zeros((HALO - N_META, C_CONV), F32), u_m], axis=0)
    cw = jnp.concatenate([conv_w[0], jnp.zeros((HALO - CONV_K, C_CONV), F32)], axis=0)
    h1, hp, ri, rw, cnt = _mix_call(x, row(ln_in_g), row(ln_in_b), u.reshape(bsz, seq, C_CONV), pre, cw,
                                    row(conv_b[0]), row(conv_ln_g[0]), row(conv_ln_b[0]), row(conv_out_g[0]),
                                    o, row(attn_out_g[0]), wo_l, row(ln1_g[0]), row(ln1_b[0]), wrh_l, wrl_l, br_l)

    counts = cnt[:N_EXPERTS, 0].astype(jnp.int32)
    padded = (counts + MOE_BM - 1) // MOE_BM * MOE_BM
    pad_end = jnp.cumsum(padded)
    pad_start = pad_end - padded
    n_blocks = (2 * t) // MOE_BM + N_EXPERTS
    dest, blk = _tables_call(pad_start, pad_end, counts, ri, n_blocks)
    blk_e, blk_valid = blk[0, :n_blocks], blk[1, :n_blocks]

    h1f = h1.reshape(t, d)
    dest0, dest1 = dest[0], dest[1]
    xb = _sc_scatter_rows(hp.reshape(t, HALF_D), dest0, dest1, n_blocks * MOE_BM)
    yb = _expert_call(blk_e, blk_valid, xb, w_gate, w_up, w_down)
    yg = _sc_gather_rows(yb, jnp.concatenate([dest0, dest1]))
    out = _combine_rows_call(h1f, rw.reshape(t, 8), row(ln2_g[0]), row(ln2_b[0]), yg)
    return out.reshape(bsz, seq, d)
```

```python
import functools
import math

import numpy as np
import jax
import jax.numpy as jnp
from jax import lax
from jax.experimental import pallas as pl
from jax.experimental.pallas import tpu as pltpu
from jax.experimental.pallas import tpu_sc as plsc

F32 = jnp.float32
BF16 = jnp.bfloat16

D_MODEL = 1024
HALF_D = D_MODEL // 2
N_META = 16
CHUNK = 64
C_CONV = 512
CONV_K = 31
N_HEADS = 8
QK_NOPE = 64
QK_ROPE = 32
V_DIM = 64
Q_LORA = 384
KV_LORA = 256
ROPE_THETA = 10000.0
D_ATTN = N_HEADS * V_DIM
N_GROUPS = 4
EXPERTS_PER_GROUP = 8
N_EXPERTS = N_GROUPS * EXPERTS_PER_GROUP
D_EXPERT = 256
DEEPNORM_ALPHA = 2.0 ** 0.25
EPS = 1e-5
NEG_INF = -1e30

LANES = 128
SUBLANES = 8
HEAD_PAD = LANES
ROPE_HALF = QK_ROPE // 2
X2_LANE = 64
Z_COLS = 2 * C_CONV + Q_LORA + KV_LORA + LANES
Q_SCALE = (QK_NOPE + QK_ROPE) ** -0.5 * math.log2(math.e)

PROJ_TM = 1024
ATT_T = 256
ATT_HEADS = 8
DEN_ROWS = 16
MIX_TM = 512
ROUTE_ROWS = 48
HALO = 32
MOE_BM = 512
COMB_TM = 1024
SC_WINDOW = 64
VMEM_LIMIT = 56 * 1024 * 1024


def _ln(x, g, b):
    mu = jnp.mean(x, axis=-1, keepdims=True)
    xc = x - mu
    var = jnp.mean(xc * xc, axis=-1, keepdims=True)
    return xc * lax.rsqrt(var + EPS) * g + b


def _rms(x, g):
    return x * lax.rsqrt(jnp.mean(x * x, axis=-1, keepdims=True) + EPS) * g


def _sigmoid(x):
    return 1.0 / (1.0 + jnp.exp(-x))


def _pack_bf16_pair(lo, hi):
    lo_bits = lax.bitcast_convert_type(lo.astype(BF16).astype(F32), jnp.uint32)
    hi_bits = lax.bitcast_convert_type(hi.astype(BF16).astype(F32), jnp.uint32)
    return (lo_bits >> 16) | (hi_bits & jnp.uint32(0xFFFF0000))


def _unpack_bf16_pair(packed):
    lo = lax.bitcast_convert_type(packed << 16, F32)
    hi = lax.bitcast_convert_type(packed & jnp.uint32(0xFFFF0000), F32)
    return lo, hi


def _proj_kernel(x_ref, g_ref, b_ref, win_ref, qg_ref, wuq_ref, kvg_ref, wuk_ref, wvt_ref, c_ref, s_ref,
                 u_ref, q_ref, k_ref, vt_ref):
    h = _ln(x_ref[...], g_ref[...], b_ref[...])
    z = jnp.dot(h.astype(BF16), win_ref[...], preferred_element_type=F32)
    u_ref[...] = z[:, :C_CONV] * _sigmoid(z[:, C_CONV:2 * C_CONV])
    c0 = 2 * C_CONV
    cq = z[:, c0:c0 + Q_LORA]
    ckv = z[:, c0 + Q_LORA:c0 + Q_LORA + KV_LORA]
    kr = z[:, c0 + Q_LORA + KV_LORA:]
    cs = c_ref[...]
    sn = s_ref[...]
    q = jnp.dot(_rms(cq, qg_ref[...]).astype(BF16), wuq_ref[...], preferred_element_type=F32)
    kvn = _rms(ckv, kvg_ref[...]).astype(BF16)
    kn = jnp.dot(kvn, wuk_ref[...], preferred_element_type=F32)
    kr_rot = kr * cs + pltpu.roll(kr, X2_LANE, 1) * sn
    for hh in range(N_HEADS):
        sl = slice(HEAD_PAD * hh, HEAD_PAD * (hh + 1))
        qh = q[:, sl]
        qh = (qh * cs + pltpu.roll(qh, X2_LANE, 1) * sn) * Q_SCALE
        q_ref[:, sl] = qh.astype(BF16)
        k_ref[:, sl] = (kn[:, sl] + kr_rot).astype(BF16)
    vt = lax.dot_general(wvt_ref[...], kvn, (((1,), (1,)), ((), ())), preferred_element_type=F32)
    n_sub, sub_w = vt_ref.shape[1], vt_ref.shape[3]
    for tt in range(n_sub):
        vt_ref[0, tt] = vt[:, tt * sub_w:(tt + 1) * sub_w].astype(BF16)


def _proj_call(x2d, tm, pos_blocks, ln_g, ln_b, w_in, qg, wuq, kvg, wuk, wvt, ctab, stab):
    t = x2d.shape[0]
    sub_w = min(tm, ATT_T)
    n_sub = tm // sub_w
    full = lambda shape: pl.BlockSpec(shape, lambda i: (0,) * len(shape))
    row = lambda w: pl.BlockSpec((tm, w), lambda i: (i, 0))
    tab = pl.BlockSpec((tm, LANES), lambda i: (i % pos_blocks, 0))
    return pl.pallas_call(
        _proj_kernel,
        grid=(t // tm,),
        in_specs=[row(D_MODEL), full((1, D_MODEL)), full((1, D_MODEL)), full((D_MODEL, Z_COLS)),
                  full((1, Q_LORA)), full((Q_LORA, N_HEADS * HEAD_PAD)),
                  full((1, KV_LORA)), full((KV_LORA, N_HEADS * HEAD_PAD)), full((D_ATTN, KV_LORA)), tab, tab],
        out_specs=[row(C_CONV), row(N_HEADS * HEAD_PAD), row(N_HEADS * HEAD_PAD),
                   pl.BlockSpec((1, n_sub, D_ATTN, sub_w), lambda i: (i, 0, 0, 0))],
        out_shape=[jax.ShapeDtypeStruct((t, C_CONV), F32),
                   jax.ShapeDtypeStruct((t, N_HEADS * HEAD_PAD), BF16),
                   jax.ShapeDtypeStruct((t, N_HEADS * HEAD_PAD), BF16),
                   jax.ShapeDtypeStruct((t // tm, n_sub, D_ATTN, sub_w), BF16)],
        compiler_params=pltpu.CompilerParams(dimension_semantics=("arbitrary",),
                                             vmem_limit_bytes=VMEM_LIMIT),
        name="proj",
    )(x2d, ln_g, ln_b, w_in, qg, wuq, kvg, wuk, wvt, ctab, stab)


def _attn_kernel(q_ref, k_ref, vt_ref, km_ref, vtm_ref, o_ref, s_sc):
    seq = q_ref.shape[1]
    nq = seq // ATT_T
    key_chunk = lax.broadcasted_iota(jnp.int32, (ATT_T, ATT_T), 0) // CHUNK
    qry_chunk = lax.broadcasted_iota(jnp.int32, (ATT_T, ATT_T), 1) // CHUNK
    diag_visible = key_chunk <= qry_chunk
    nt = (((1,), (1,)), ((), ()))
    heads = range(ATT_HEADS)
    hsl = [slice(HEAD_PAD * hh, HEAD_PAD * (hh + 1)) for hh in heads]
    vsl = [slice(V_DIM * hh, V_DIM * (hh + 1)) for hh in heads]

    def with_ones(vt):
        return jnp.concatenate([vt, jnp.ones((DEN_ROWS, vt.shape[1]), BF16)], axis=0)

    def update(s, vt, m, acc):
        m_new = jnp.maximum(m, jnp.max(s, axis=0, keepdims=True))
        alpha = jnp.exp2(m - m_new)
        p = jnp.exp2(s - m_new)
        acc = alpha * acc + jnp.dot(with_ones(vt), p.astype(BF16), preferred_element_type=F32)
        return m_new, acc

    def scores(qi, j, hh):
        q0 = pl.multiple_of(qi * ATT_T, ATT_T)
        k0 = pl.multiple_of(j * ATT_T, ATT_T)
        return lax.dot_general(k_ref[0, pl.ds(k0, ATT_T), hsl[hh]], q_ref[0, pl.ds(q0, ATT_T), hsl[hh]], nt,
                               preferred_element_type=F32)

    for hh in heads:
        s_sc[0, hh] = scores(0, 0, hh)

    def q_body(qi, carry):
        row0 = pl.multiple_of(qi * ATT_T, ATT_T)
        slot = qi % 2
        state = []
        for hh in heads:
            state += [jnp.full((1, ATT_T), NEG_INF, F32), jnp.zeros((V_DIM + DEN_ROWS, ATT_T), F32)]

        def kv_body(j, st):
            out = []
            for hh in heads:
                s_cur = s_sc[slot, hh]
                s_next = scores(qi, j + 1, hh)
                out += update(s_cur, vt_ref[0, j, vsl[hh], :], *st[2 * hh:2 * hh + 2])
                s_sc[slot, hh] = s_next
            return tuple(out)

        state = lax.fori_loop(0, qi, kv_body, tuple(state))
        q_next = jnp.minimum(qi + 1, nq - 1)
        sms = [lax.dot_general(km_ref[:, hsl[hh]], q_ref[0, pl.ds(row0, ATT_T), hsl[hh]], nt,
                               preferred_element_type=F32) for hh in heads]
        for hh in heads:
            s_sc[1 - slot, hh] = scores(q_next, 0, hh)
        outs = []
        for hh in heads:
            m, acc = state[2 * hh:2 * hh + 2]
            s = jnp.where(diag_visible, s_sc[slot, hh], NEG_INF)
            sm = sms[hh]
            m_new = jnp.maximum(m, jnp.maximum(jnp.max(s, axis=0, keepdims=True),
                                               jnp.max(sm, axis=0, keepdims=True)))
            alpha = jnp.exp2(m - m_new)
            p = jnp.exp2(s - m_new)
            pm = jnp.exp2(sm - m_new)
            acc = (alpha * acc
                   + jnp.dot(with_ones(vt_ref[0, qi, vsl[hh], :]), p.astype(BF16), preferred_element_type=F32)
                   + jnp.dot(with_ones(vtm_ref[0, 0, vsl[hh], :]), pm.astype(BF16), preferred_element_type=F32))
            outs.append(acc[:V_DIM] / acc[V_DIM:V_DIM + 1])
        o_ref[0, pl.ds(row0, ATT_T), :] = jnp.concatenate(outs, axis=0).T
        return carry

    lax.fori_loop(0, nq, q_body, 0)


def _attn_call(q, k, vt, k_meta, vt_meta):
    b, seq, _ = q.shape
    nk = seq // ATT_T
    hb = ATT_HEADS
    return pl.pallas_call(
        _attn_kernel,
        grid=(b, N_HEADS // hb),
        in_specs=[pl.BlockSpec((1, seq, hb * HEAD_PAD), lambda i, j: (i, 0, j)),
                  pl.BlockSpec((1, seq, hb * HEAD_PAD), lambda i, j: (i, 0, j)),
                  pl.BlockSpec((1, nk, hb * V_DIM, ATT_T), lambda i, j: (i, 0, j, 0)),
                  pl.BlockSpec((N_META, hb * HEAD_PAD), lambda i, j: (0, j)),
                  pl.BlockSpec((1, 1, hb * V_DIM, N_META), lambda i, j: (0, 0, j, 0))],
        out_specs=pl.BlockSpec((1, seq, hb * V_DIM), lambda i, j: (i, 0, j)),
        out_shape=jax.ShapeDtypeStruct((b, seq, D_ATTN), F32),
        scratch_shapes=[pltpu.VMEM((2, hb, ATT_T, ATT_T), F32)],
        compiler_params=pltpu.CompilerParams(dimension_semantics=("arbitrary", "arbitrary"),
                                             vmem_limit_bytes=VMEM_LIMIT),
        name="attn",
    )(q, k, vt, k_meta, vt_meta)


def _mix_kernel(x_ref, ig_ref, ib_ref, u_ref, uh_ref, pre_ref, cw_ref, cb_ref, clg_ref, clb_ref,
                cog_ref, o_ref, aog_ref, wo_ref, l1g_ref, l1b_ref, wrh_ref, wrl_ref, br_ref,
                h1_ref, hp_ref, ri_ref, rw_ref, cnt_ref, win_sc, sh_sc, carry_sc):
    bi = pl.program_id(0)
    ti = pl.program_id(1)
    tm = MIX_TM

    @pl.when((bi == 0) & (ti == 0))
    def _():
        carry_sc[...] = jnp.zeros_like(carry_sc)

    win_sc[0:HALO, :] = jnp.where(ti == 0, pre_ref[...], uh_ref[0])
    win_sc[HALO:, :] = u_ref[0]
    first = HALO - (CONV_K - 1)
    acc = jnp.broadcast_to(cb_ref[...], (tm, C_CONV))
    for res in range(SUBLANES):
        offs = [o for o in range(first, first + CONV_K) if o % SUBLANES == res]
        rows = tm + offs[-1] - res
        if res:
            sh_sc[res - 1, 0:rows, :] = win_sc[pl.ds(res, rows), :]
        for o in offs:
            src = sh_sc[res - 1, o - res:o - res + tm, :] if res else win_sc[o:o + tm, :]
            acc = acc + src * cw_ref[o - first:o - first + 1, :]
    c = _ln(acc, clg_ref[...], clb_ref[...])
    c = c * _sigmoid(c)
    cn = _rms(c, cog_ref[...])
    on = _rms(o_ref[0], aog_ref[...])
    y = (jnp.dot(cn.astype(BF16), wo_ref[0:C_CONV, :], preferred_element_type=F32)
         + jnp.dot(on.astype(BF16), wo_ref[C_CONV:, :], preferred_element_type=F32))
    h = _ln(x_ref[0], ig_ref[...], ib_ref[...])
    h1 = _ln(DEEPNORM_ALPHA * h + y, l1g_ref[...], l1b_ref[...])
    h1_ref[0] = h1
    hp_ref[0] = _pack_bf16_pair(h1[:, :HALF_D], h1[:, HALF_D:])

    nt = (((1,), (1,)), ((), ()))
    h_hi = h1.astype(BF16)
    h_lo = (h1 - h_hi.astype(F32)).astype(BF16)
    lt = (lax.dot_general(wrh_ref[...], h_hi, nt, preferred_element_type=F32)
          + lax.dot_general(wrl_ref[...], h_hi, nt, preferred_element_type=F32)
          + lax.dot_general(wrh_ref[...], h_lo, nt, preferred_element_type=F32)) + br_ref[:, 0:1]
    row = lax.broadcasted_iota(jnp.int32, (ROUTE_ROWS, tm), 0)
    big = jnp.int32(1 << 20)
    is_g = (row >= N_EXPERTS) & (row < N_EXPERTS + N_GROUPS)
    gl = jnp.where(is_g, lt, -jnp.inf)
    gmax = jnp.max(gl, axis=0, keepdims=True)
    g_idx = jnp.min(jnp.where(gl == gmax, row - N_EXPERTS, big), axis=0, keepdims=True)
    g_w = 1.0 / jnp.sum(jnp.exp(gl - gmax), axis=0, keepdims=True)
    in_grp = (row < N_EXPERTS) & ((row // EXPERTS_PER_GROUP) == g_idx)
    el = jnp.where(in_grp, lt, -jnp.inf)
    v1 = jnp.max(el, axis=0, keepdims=True)
    i1 = jnp.min(jnp.where(el == v1, row, big), axis=0, keepdims=True)
    el2 = jnp.where(row == i1, -jnp.inf, el)
    v2 = jnp.max(el2, axis=0, keepdims=True)
    i2 = jnp.min(jnp.where(el2 == v2, row, big), axis=0, keepdims=True)
    e2 = jnp.exp(v2 - v1)
    w1 = g_w / (1.0 + e2)
    w2 = g_w * e2 / (1.0 + e2)

    oh1 = row == i1
    oh2 = row == i2
    oh = jnp.where(oh1 | oh2, 1.0, 0.0)
    earlier = (lax.broadcasted_iota(jnp.int32, (tm, tm), 0) < lax.broadcasted_iota(jnp.int32, (tm, tm), 1))
    before = (jnp.dot(oh.astype(BF16), jnp.where(earlier, 1.0, 0.0).astype(BF16), preferred_element_type=F32)
              + carry_sc[:, 0:1])
    r1 = jnp.sum(jnp.where(oh1, before, 0.0), axis=0, keepdims=True).astype(jnp.int32)
    r2 = jnp.sum(jnp.where(oh2, before, 0.0), axis=0, keepdims=True).astype(jnp.int32)
    carry_sc[...] = carry_sc[...] + jnp.sum(oh, axis=1, keepdims=True)
    cnt_ref[...] = carry_sc[...]

    sub = lax.broadcasted_iota(jnp.int32, (SUBLANES, tm), 0)
    ri_ref[...] = jnp.where(sub == 0, i1, jnp.where(sub == 1, i2, jnp.where(sub == 2, r1,
                            jnp.where(sub == 3, r2, 0))))
    lrow = lax.broadcasted_iota(jnp.int32, (LANES, tm), 0)
    rw_ref[0] = jnp.where(lrow == 0, w1, jnp.where(lrow == 1, w2, 0.0)).T[:, :8]


def _mix_call(x, ig, ib, u, pre, cw, cb, clg, clb, cog, o, aog, wo, l1g, l1b, wrh, wrl, br):
    b, seq, _ = x.shape
    tm = MIX_TM
    nt = seq // tm
    full = lambda shape: pl.BlockSpec(shape, lambda i, j: (0,) * len(shape))
    tile = lambda w: pl.BlockSpec((1, tm, w), lambda i, j: (i, j, 0))
    halo = pl.BlockSpec((1, HALO, C_CONV), lambda i, j: (i, jnp.maximum(j * (tm // HALO) - 1, 0), 0))
    return pl.pallas_call(
        _mix_kernel,
        grid=(b, nt),
        in_specs=[tile(D_MODEL), full((1, D_MODEL)), full((1, D_MODEL)),
                  tile(C_CONV), halo, full((HALO, C_CONV)), full((HALO, C_CONV)), full((1, C_CONV)),
                  full((1, C_CONV)), full((1, C_CONV)), full((1, C_CONV)),
                  tile(D_ATTN), full((1, D_ATTN)), full((D_MODEL, D_MODEL)),
                  full((1, D_MODEL)), full((1, D_MODEL)), full((ROUTE_ROWS, D_MODEL)),
                  full((ROUTE_ROWS, D_MODEL)), full((ROUTE_ROWS, LANES))],
        out_specs=[tile(D_MODEL), tile(HALF_D), pl.BlockSpec((8, tm), lambda i, j: (0, i * nt + j)), tile(8),
                   full((ROUTE_ROWS, LANES))],
        out_shape=[jax.ShapeDtypeStruct((b, seq, D_MODEL), F32),
                   jax.ShapeDtypeStruct((b, seq, HALF_D), jnp.uint32),
                   jax.ShapeDtypeStruct((8, b * seq), jnp.int32),
                   jax.ShapeDtypeStruct((b, seq, 8), F32),
                   jax.ShapeDtypeStruct((ROUTE_ROWS, LANES), F32)],
        scratch_shapes=[pltpu.VMEM((HALO + tm, C_CONV), F32),
                        pltpu.VMEM((SUBLANES - 1, HALO + tm, C_CONV), F32),
                        pltpu.VMEM((ROUTE_ROWS, LANES), F32)],
        compiler_params=pltpu.CompilerParams(dimension_semantics=("arbitrary", "arbitrary"),
                                             vmem_limit_bytes=VMEM_LIMIT),
        name="mix",
    )(x, ig, ib, u, u, pre, cw, cb, clg, clb, cog, o, aog, wo, l1g, l1b, wrh, wrl, br)


def _tables_kernel(ps_ref, pe_ref, cn_ref, ri_ref, dest_ref, blk_ref):
    ri = ri_ref[...]
    start = jnp.zeros_like(ri)
    for e in range(N_EXPERTS):
        start = jnp.where(ri == e, ps_ref[e], start)
    dest_ref[...] = jnp.zeros_like(dest_ref)
    dest_ref[0:2, :] = start[0:2, :] + ri_ref[2:4, :]
    blk_start = lax.broadcasted_iota(jnp.int32, blk_ref.shape, 1) * MOE_BM
    be = jnp.zeros(blk_ref.shape, jnp.int32)
    for e in range(N_EXPERTS):
        be = be + jnp.where(pe_ref[e] <= blk_start, 1, 0)
    be = jnp.minimum(be, N_EXPERTS - 1)
    end = jnp.zeros(blk_ref.shape, jnp.int32)
    for e in range(N_EXPERTS):
        end = jnp.where(be == e, ps_ref[e] + cn_ref[e], end)
    valid = jnp.clip(end - blk_start, 0, MOE_BM)
    row = lax.broadcasted_iota(jnp.int32, blk_ref.shape, 0)
    blk_ref[...] = jnp.where(row == 0, be, valid)


def _tables_call(pad_start, pad_end, counts, ri, n_blocks):
    t = ri.shape[1]
    blk_w = -(-n_blocks // LANES) * LANES
    return pl.pallas_call(
        _tables_kernel,
        grid_spec=pltpu.PrefetchScalarGridSpec(
            num_scalar_prefetch=3,
            grid=(1,),
            in_specs=[pl.BlockSpec((SUBLANES, t), lambda i, *_: (0, 0))],
            out_specs=[pl.BlockSpec((SUBLANES, t), lambda i, *_: (0, 0)),
                       pl.BlockSpec((SUBLANES, blk_w), lambda i, *_: (0, 0))]),
        out_shape=[jax.ShapeDtypeStruct((SUBLANES, t), jnp.int32),
                   jax.ShapeDtypeStruct((SUBLANES, blk_w), jnp.int32)],
        compiler_params=pltpu.CompilerParams(dimension_semantics=("arbitrary",),
                                             vmem_limit_bytes=VMEM_LIMIT),
        name="tables",
    )(pad_start, pad_end, counts, ri)


def _expert_kernel(be_ref, nv_ref, x_ref, wg_ref, wu_ref, wd_ref, y_ref, wg_sc, wu_sc, wd_sc):
    i = pl.program_id(0)
    nv = nv_ref[i]

    @pl.when((i == 0) | (be_ref[i] != be_ref[jnp.maximum(i - 1, 0)]))
    def _():
        wg_sc[...] = wg_ref[0, 0].astype(BF16)
        wu_sc[...] = wu_ref[0, 0].astype(BF16)
        wd_sc[...] = wd_ref[0, 0].astype(BF16)

    @pl.when(nv > 0)
    def _():
        rowid = lax.broadcasted_iota(jnp.int32, x_ref.shape, 0)
        x_lo, x_hi = _unpack_bf16_pair(jnp.where(rowid < nv, x_ref[...], jnp.uint32(0)))
        x_lo = x_lo.astype(BF16)
        x_hi = x_hi.astype(BF16)
        g = (jnp.dot(x_lo, wg_sc[:HALF_D, :], preferred_element_type=F32)
             + jnp.dot(x_hi, wg_sc[HALF_D:, :], preferred_element_type=F32))
        up = (jnp.dot(x_lo, wu_sc[:HALF_D, :], preferred_element_type=F32)
              + jnp.dot(x_hi, wu_sc[HALF_D:, :], preferred_element_type=F32))
        act = g * _sigmoid(g) * up
        y = jnp.dot(act.astype(BF16), wd_sc[...], preferred_element_type=F32)
        y_ref[...] = _pack_bf16_pair(y[:, :HALF_D], y[:, HALF_D:])

    @pl.when(nv == 0)
    def _():
        y_ref[...] = jnp.zeros_like(y_ref)


def _expert_call(blk_e, blk_valid, xb, w_gate, w_up, w_down):
    r = xb.shape[0]
    bm = MOE_BM
    wspec = lambda a, b: pl.BlockSpec((1, 1, a, b), lambda i, be, nv: (0, be[i], 0, 0))
    return pl.pallas_call(
        _expert_kernel,
        grid_spec=pltpu.PrefetchScalarGridSpec(
            num_scalar_prefetch=2,
            grid=(r // bm,),
            in_specs=[pl.BlockSpec((bm, HALF_D), lambda i, be, nv: (i, 0)),
                      wspec(D_MODEL, D_EXPERT), wspec(D_MODEL, D_EXPERT), wspec(D_EXPERT, D_MODEL)],
            out_specs=pl.BlockSpec((bm, HALF_D), lambda i, be, nv: (i, 0)),
            scratch_shapes=[pltpu.VMEM((D_MODEL, D_EXPERT), BF16), pltpu.VMEM((D_MODEL, D_EXPERT), BF16),
                            pltpu.VMEM((D_EXPERT, D_MODEL), BF16)]),
        out_shape=jax.ShapeDtypeStruct((r, HALF_D), jnp.uint32),
        compiler_params=pltpu.CompilerParams(dimension_semantics=("arbitrary",),
                                             vmem_limit_bytes=VMEM_LIMIT),
        name="experts",
    )(blk_e, blk_valid, xb, w_gate, w_up, w_down)


def _sc_gather_rows(table, idx):
    m = idx.shape[0]
    width = table.shape[1]
    sc = plsc.get_sparse_core_info()
    workers = sc.num_cores * sc.num_subcores
    per_worker = m // workers
    mesh = plsc.VectorSubcoreMesh(core_axis_name="c", subcore_axis_name="s")

    n_win = per_worker // SC_WINDOW
    assert n_win % 2 == 0 and n_win >= 2
    buf = lambda: [pltpu.VMEM((SC_WINDOW,), jnp.int32), pltpu.VMEM((SC_WINDOW, width), table.dtype),
                   pltpu.SemaphoreType.DMA]

    @functools.partial(pl.kernel, mesh=mesh, out_type=jax.ShapeDtypeStruct((m, width), table.dtype),
                       scratch_types=buf() + buf())
    def gather(table_hbm, idx_hbm, out_hbm, idx_a, rows_a, sem_a, idx_b, rows_b, sem_b):
        base = (lax.axis_index("s") * sc.num_cores + lax.axis_index("c")) * per_worker
        bufs = ((idx_a, rows_a, sem_a), (idx_b, rows_b, sem_b))

        def fetch(win, which):
            idx_v, rows_v, sem = bufs[which]
            pltpu.sync_copy(idx_hbm.at[pl.ds(base + win * SC_WINDOW, SC_WINDOW)], idx_v)
            pltpu.async_copy(table_hbm.at[idx_v], rows_v, sem)

        def flush(win, which):
            idx_v, rows_v, sem = bufs[which]
            pltpu.make_async_copy(table_hbm.at[idx_v], rows_v, sem).wait()
            pltpu.sync_copy(rows_v, out_hbm.at[pl.ds(base + win * SC_WINDOW, SC_WINDOW)])

        fetch(0, 0)

        @pl.loop(0, n_win // 2 - 1)
        def _(pair):
            win = 2 * pair
            fetch(win + 1, 1)
            flush(win, 0)
            fetch(win + 2, 0)
            flush(win + 1, 1)

        fetch(n_win - 1, 1)
        flush(n_win - 2, 0)
        flush(n_win - 1, 1)

    return gather(table, idx)


def _sc_scatter_rows(rows, idx0, idx1, n_out):
    m, width = rows.shape
    sc = plsc.get_sparse_core_info()
    workers = sc.num_cores * sc.num_subcores
    per_worker = m // workers
    mesh = plsc.VectorSubcoreMesh(core_axis_name="c", subcore_axis_name="s")

    n_win = per_worker // SC_WINDOW
    assert n_win % 2 == 0 and n_win >= 2
    buf = lambda: [pltpu.VMEM((SC_WINDOW,), jnp.int32), pltpu.VMEM((SC_WINDOW,), jnp.int32),
                   pltpu.VMEM((SC_WINDOW, width), rows.dtype),
                   pltpu.SemaphoreType.DMA, pltpu.SemaphoreType.DMA, pltpu.SemaphoreType.DMA]

    @functools.partial(pl.kernel, mesh=mesh, out_type=jax.ShapeDtypeStruct((n_out, width), rows.dtype),
                       scratch_types=buf() + buf())
    def scatter(rows_hbm, idx0_hbm, idx1_hbm, out_hbm, *scratch):
        base = (lax.axis_index("s") * sc.num_cores + lax.axis_index("c")) * per_worker
        bufs = (scratch[:6], scratch[6:])

        def load(win, which):
            idx0_v, idx1_v, rows_v, lsem, _, _ = bufs[which]
            off = base + win * SC_WINDOW
            pltpu.sync_copy(idx0_hbm.at[pl.ds(off, SC_WINDOW)], idx0_v)
            pltpu.sync_copy(idx1_hbm.at[pl.ds(off, SC_WINDOW)], idx1_v)
            pltpu.async_copy(rows_hbm.at[pl.ds(off, SC_WINDOW)], rows_v, lsem)

        def send(which):
            idx0_v, idx1_v, rows_v, lsem, sem0, sem1 = bufs[which]
            pltpu.make_async_copy(rows_hbm.at[pl.ds(0, SC_WINDOW)], rows_v, lsem).wait()
            first = pltpu.async_copy(rows_v, out_hbm.at[idx0_v], sem0)
            second = pltpu.async_copy(rows_v, out_hbm.at[idx1_v], sem1)
            first.wait()
            second.wait()

        load(0, 0)

        @pl.loop(0, n_win // 2 - 1)
        def _(pair):
            win = 2 * pair
            load(win + 1, 1)
            send(0)
            load(win + 2, 0)
            send(1)

        load(n_win - 1, 1)
        send(0)
        send(1)

    return scatter(rows, idx0, idx1)


def _combine_rows_kernel(h_ref, w_ref, g_ref, b_ref, y0_ref, y1_ref, o_ref):
    w = w_ref[...]
    y0_lo, y0_hi = _unpack_bf16_pair(y0_ref[...])
    y1_lo, y1_hi = _unpack_bf16_pair(y1_ref[...])
    y = jnp.concatenate([w[:, 0:1] * y0_lo + w[:, 1:2] * y1_lo, w[:, 0:1] * y0_hi + w[:, 1:2] * y1_hi], axis=1)
    o_ref[...] = _ln(DEEPNORM_ALPHA * h_ref[...] + y, g_ref[...], b_ref[...])


def _combine_rows_call(h1, rw, g, b, yg):
    t = h1.shape[0]
    tm = COMB_TM
    n = t // tm
    return pl.pallas_call(
        _combine_rows_kernel,
        grid=(n,),
        in_specs=[pl.BlockSpec((tm, D_MODEL), lambda i: (i, 0)),
                  pl.BlockSpec((tm, 8), lambda i: (i, 0)),
                  pl.BlockSpec((1, D_MODEL), lambda i: (0, 0)),
                  pl.BlockSpec((1, D_MODEL), lambda i: (0, 0)),
                  pl.BlockSpec((tm, HALF_D), lambda i: (i, 0)),
                  pl.BlockSpec((tm, HALF_D), lambda i: (n + i, 0))],
        out_specs=pl.BlockSpec((tm, D_MODEL), lambda i: (i, 0)),
        out_shape=jax.ShapeDtypeStruct((t, D_MODEL), F32),
        compiler_params=pltpu.CompilerParams(dimension_semantics=("arbitrary",),
                                             vmem_limit_bytes=VMEM_LIMIT),
        name="combine_rows",
    )(h1, rw, g, b, yg, yg)


def _take_cols(w, idx):
    wz = jnp.concatenate([w, jnp.zeros((w.shape[0], 1), w.dtype)], axis=1)
    return wz[:, np.where(idx < 0, w.shape[1], idx)]


def _layout_indices():
    lane_src = np.full((HEAD_PAD,), -1, np.int64)
    lane_src[0:ROPE_HALF] = QK_NOPE + np.arange(ROPE_HALF)
    lane_src[ROPE_HALF:X2_LANE] = np.arange(X2_LANE - ROPE_HALF)
    lane_src[X2_LANE:X2_LANE + ROPE_HALF] = QK_NOPE + ROPE_HALF + np.arange(ROPE_HALF)
    n_lo = X2_LANE - ROPE_HALF
    lane_src[X2_LANE + ROPE_HALF:X2_LANE + ROPE_HALF + QK_NOPE - n_lo] = n_lo + np.arange(QK_NOPE - n_lo)
    q_idx = np.concatenate([np.where(lane_src >= 0, lane_src + (QK_NOPE + QK_ROPE) * h, -1)
                            for h in range(N_HEADS)])
    nope_src = np.where(lane_src < QK_NOPE, lane_src, -1)
    k_idx = np.concatenate([np.where(nope_src >= 0, nope_src + (QK_NOPE + V_DIM) * h, -1)
                            for h in range(N_HEADS)])
    v_idx = np.concatenate([QK_NOPE + np.arange(V_DIM) + (QK_NOPE + V_DIM) * h for h in range(N_HEADS)])
    kr_idx = np.full((LANES,), -1, np.int64)
    kr_idx[0:ROPE_HALF] = np.arange(ROPE_HALF)
    kr_idx[X2_LANE:X2_LANE + ROPE_HALF] = ROPE_HALF + np.arange(ROPE_HALF)
    return q_idx, k_idx, v_idx, kr_idx


def _rope_lane_tables(length):
    inv_freq = np.float32(ROPE_THETA) ** (-np.arange(ROPE_HALF, dtype=np.float32) / np.float32(ROPE_HALF))
    ang = np.arange(length, dtype=np.float32)[:, None] * inv_freq[None, :].astype(np.float32)
    cos, sin = np.cos(ang).astype(np.float32), np.sin(ang).astype(np.float32)
    ctab = np.ones((length, LANES), np.float32)
    stab = np.zeros((length, LANES), np.float32)
    for lo, sign in ((0, -1.0), (X2_LANE, 1.0)):
        ctab[:, lo:lo + ROPE_HALF] = cos
        stab[:, lo:lo + ROPE_HALF] = sign * sin
    return jnp.asarray(ctab), jnp.asarray(stab)


def kernel(x, meta_tokens, ln_in_g, ln_in_b, w_in, conv_w, conv_b, conv_ln_g, conv_ln_b, q_norm_g, w_uq,
           kv_norm_g, w_ukv, conv_out_g, attn_out_g, w_o, ln1_g, ln1_b, w_rg, b_rg, w_re, b_re,
           w_gate, w_up, w_down, ln2_g, ln2_b):
    assert w_in.shape[0] == 1, "single layer"
    bsz, seq, d = x.shape
    t = bsz * seq
    row = lambda a: a.reshape(1, -1)

    q_idx, k_idx, v_idx, kr_idx = _layout_indices()
    c4 = 2 * C_CONV + Q_LORA + KV_LORA
    w_in_l = jnp.concatenate([w_in[0][:, :c4], _take_cols(w_in[0][:, c4:], kr_idx)], axis=1).astype(BF16)
    wuq_l = _take_cols(w_uq[0], q_idx).astype(BF16)
    wuk_l = _take_cols(w_ukv[0], k_idx).astype(BF16)
    wvt_l = w_ukv[0][:, v_idx].T.astype(BF16)
    wo_l = w_o[0].astype(BF16)
    n_route_pad = ROUTE_ROWS - N_EXPERTS - N_GROUPS
    wr_t = jnp.concatenate([w_re[0].T, w_rg[0].T, jnp.zeros((n_route_pad, d), F32)], axis=0)
    wrh_l = wr_t.astype(BF16)
    wrl_l = (wr_t - wrh_l.astype(F32)).astype(BF16)
    br_l = jnp.broadcast_to(jnp.concatenate([b_re[0], b_rg[0], jnp.zeros((n_route_pad,), F32)])[:, None],
                            (ROUTE_ROWS, LANES))
    ctab, stab = _rope_lane_tables(N_META + seq)

    proj_w = (row(ln_in_g), row(ln_in_b), w_in_l, row(q_norm_g[0]), wuq_l, row(kv_norm_g[0]), wuk_l, wvt_l)
    u_m, _, k_m, vt_m = _proj_call(meta_tokens.astype(F32), N_META, 1, *proj_w, ctab[:N_META], stab[:N_META])
    u, q, k, vt = _proj_call(x.reshape(t, d), PROJ_TM, seq // PROJ_TM, *proj_w, ctab[N_META:], stab[N_META:])

    o = _attn_call(q.reshape(bsz, seq, -1), k.reshape(bsz, seq, -1),
                   vt.reshape(bsz, seq // ATT_T, D_ATTN, ATT_T), k_m, vt_m)

    pre = jnp.concatenate([jnp.zeros((HALO - N_META, C_CONV), F32), u_m], axis=0)
    cw = jnp.concatenate([conv_w[0], jnp.zeros((HALO - CONV_K, C_CONV), F32)], axis=0)
    h1, hp, ri, rw, cnt = _mix_call(x, row(ln_in_g), row(ln_in_b), u.reshape(bsz, seq, C_CONV), pre, cw,
                                    row(conv_b[0]), row(conv_ln_g[0]), row(conv_ln_b[0]), row(conv_out_g[0]),
                                    o, row(attn_out_g[0]), wo_l, row(ln1_g[0]), row(ln1_b[0]), wrh_l, wrl_l, br_l)

    counts = cnt[:N_EXPERTS, 0].astype(jnp.int32)
    padded = (counts + MOE_BM - 1) // MOE_BM * MOE_BM
    pad_end = jnp.cumsum(padded)
    pad_start = pad_end - padded
    n_blocks = (2 * t) // MOE_BM + N_EXPERTS
    dest, blk = _tables_call(pad_start, pad_end, counts, ri, n_blocks)
    blk_e, blk_valid = blk[0, :n_blocks], blk[1, :n_blocks]

    h1f = h1.reshape(t, d)
    dest0, dest1 = dest[0], dest[1]
    xb = _sc_scatter_rows(hp.reshape(t, HALF_D), dest0, dest1, n_blocks * MOE_BM)
    yb = _expert_call(blk_e, blk_valid, xb, w_gate, w_up, w_down)
    yg = _sc_gather_rows(yb, jnp.concatenate([dest0, dest1]))
    out = _combine_rows_call(h1f, rw.reshape(t, 8), row(ln2_g[0]), row(ln2_b[0]), yg)
    return out.reshape(bsz, seq, d)
```

```python
import functools
import math

import numpy as np
import jax
import jax.numpy as jnp
from jax import lax
from jax.experimental import pallas as pl
from jax.experimental.pallas import tpu as pltpu
from jax.experimental.pallas import tpu_sc as plsc

F32 = jnp.float32
BF16 = jnp.bfloat16

D_MODEL = 1024
HALF_D = D_MODEL // 2
N_META = 16
CHUNK = 64
C_CONV = 512
CONV_K = 31
N_HEADS = 8
QK_NOPE = 64
QK_ROPE = 32
V_DIM = 64
Q_LORA = 384
KV_LORA = 256
ROPE_THETA = 10000.0
D_ATTN = N_HEADS * V_DIM
N_GROUPS = 4
EXPERTS_PER_GROUP = 8
N_EXPERTS = N_GROUPS * EXPERTS_PER_GROUP
D_EXPERT = 256
DEEPNORM_ALPHA = 2.0 ** 0.25
EPS = 1e-5
NEG_INF = -1e30

LANES = 128
SUBLANES = 8
HEAD_PAD = LANES
ROPE_HALF = QK_ROPE // 2
X2_LANE = 64
Z_COLS = 2 * C_CONV + Q_LORA + KV_LORA + LANES
Q_SCALE = (QK_NOPE + QK_ROPE) ** -0.5 * math.log2(math.e)

PROJ_TM = 1024
ATT_T = 256
ATT_HEADS = 8
MIX_TM = 512
ROUTE_ROWS = 48
HALO = 32
MOE_BM = 512
COMB_TM = 1024
COMBINE_PARTS = 4
SC_WINDOW = 64
VMEM_LIMIT = 56 * 1024 * 1024


def _ln(x, g, b):
    mu = jnp.mean(x, axis=-1, keepdims=True)
    xc = x - mu
    var = jnp.mean(xc * xc, axis=-1, keepdims=True)
    return xc * lax.rsqrt(var + EPS) * g + b


def _rms(x, g):
    return x * lax.rsqrt(jnp.mean(x * x, axis=-1, keepdims=True) + EPS) * g


def _sigmoid(x):
    return 1.0 / (1.0 + jnp.exp(-x))


def _pack_bf16_pair(lo, hi):
    lo_bits = lax.bitcast_convert_type(lo.astype(BF16).astype(F32), jnp.uint32)
    hi_bits = lax.bitcast_convert_type(hi.astype(BF16).astype(F32), jnp.uint32)
    return (lo_bits >> 16) | (hi_bits & jnp.uint32(0xFFFF0000))


def _unpack_bf16_pair(packed):
    lo = lax.bitcast_convert_type(packed << 16, F32)
    hi = lax.bitcast_convert_type(packed & jnp.uint32(0xFFFF0000), F32)
    return lo, hi


def _proj_kernel(x_ref, g_ref, b_ref, win_ref, qg_ref, wuq_ref, kvg_ref, wuk_ref, wvt_ref, c_ref, s_ref,
                 u_ref, q_ref, k_ref, vt_ref):
    h = _ln(x_ref[...], g_ref[...], b_ref[...])
    z = jnp.dot(h.astype(BF16), win_ref[...], preferred_element_type=F32)
    u_ref[...] = z[:, :C_CONV] * _sigmoid(z[:, C_CONV:2 * C_CONV])
    c0 = 2 * C_CONV
    cq = z[:, c0:c0 + Q_LORA]
    ckv = z[:, c0 + Q_LORA:c0 + Q_LORA + KV_LORA]
    kr = z[:, c0 + Q_LORA + KV_LORA:]
    cs = c_ref[...]
    sn = s_ref[...]
    q = jnp.dot(_rms(cq, qg_ref[...]).astype(BF16), wuq_ref[...], preferred_element_type=F32)
    kvn = _rms(ckv, kvg_ref[...]).astype(BF16)
    kn = jnp.dot(kvn, wuk_ref[...], preferred_element_type=F32)
    kr_rot = kr * cs + pltpu.roll(kr, X2_LANE, 1) * sn
    for hh in range(N_HEADS):
        sl = slice(HEAD_PAD * hh, HEAD_PAD * (hh + 1))
        qh = q[:, sl]
        qh = (qh * cs + pltpu.roll(qh, X2_LANE, 1) * sn) * Q_SCALE
        q_ref[:, sl] = qh.astype(BF16)
        k_ref[:, sl] = (kn[:, sl] + kr_rot).astype(BF16)
    vt = lax.dot_general(wvt_ref[...], kvn, (((1,), (1,)), ((), ())), preferred_element_type=F32)
    n_sub, sub_w = vt_ref.shape[1], vt_ref.shape[3]
    for tt in range(n_sub):
        vt_ref[0, tt] = vt[:, tt * sub_w:(tt + 1) * sub_w].astype(BF16)


def _proj_call(x2d, tm, pos_blocks, ln_g, ln_b, w_in, qg, wuq, kvg, wuk, wvt, ctab, stab):
    t = x2d.shape[0]
    sub_w = min(tm, ATT_T)
    n_sub = tm // sub_w
    full = lambda shape: pl.BlockSpec(shape, lambda i: (0,) * len(shape))
    row = lambda w: pl.BlockSpec((tm, w), lambda i: (i, 0))
    tab = pl.BlockSpec((tm, LANES), lambda i: (i % pos_blocks, 0))
    return pl.pallas_call(
        _proj_kernel,
        grid=(t // tm,),
        in_specs=[row(D_MODEL), full((1, D_MODEL)), full((1, D_MODEL)), full((D_MODEL, Z_COLS)),
                  full((1, Q_LORA)), full((Q_LORA, N_HEADS * HEAD_PAD)),
                  full((1, KV_LORA)), full((KV_LORA, N_HEADS * HEAD_PAD)), full((D_ATTN, KV_LORA)), tab, tab],
        out_specs=[row(C_CONV), row(N_HEADS * HEAD_PAD), row(N_HEADS * HEAD_PAD),
                   pl.BlockSpec((1, n_sub, D_ATTN, sub_w), lambda i: (i, 0, 0, 0))],
        out_shape=[jax.ShapeDtypeStruct((t, C_CONV), F32),
                   jax.ShapeDtypeStruct((t, N_HEADS * HEAD_PAD), BF16),
                   jax.ShapeDtypeStruct((t, N_HEADS * HEAD_PAD), BF16),
                   jax.ShapeDtypeStruct((t // tm, n_sub, D_ATTN, sub_w), BF16)],
        compiler_params=pltpu.CompilerParams(dimension_semantics=("arbitrary",),
                                             vmem_limit_bytes=VMEM_LIMIT),
        name="proj",
    )(x2d, ln_g, ln_b, w_in, qg, wuq, kvg, wuk, wvt, ctab, stab)


def _attn_kernel(q_ref, k_ref, vt_ref, km_ref, vtm_ref, o_ref, s_sc):
    seq = q_ref.shape[1]
    nq = seq // ATT_T
    key_chunk = lax.broadcasted_iota(jnp.int32, (ATT_T, ATT_T), 0) // CHUNK
    qry_chunk = lax.broadcasted_iota(jnp.int32, (ATT_T, ATT_T), 1) // CHUNK
    diag_visible = key_chunk <= qry_chunk
    nt = (((1,), (1,)), ((), ()))
    heads = range(ATT_HEADS)
    hsl = [slice(HEAD_PAD * hh, HEAD_PAD * (hh + 1)) for hh in heads]
    vsl = [slice(V_DIM * hh, V_DIM * (hh + 1)) for hh in heads]

    def update(s, vt, m, l, acc):
        m_new = jnp.maximum(m, jnp.max(s, axis=0, keepdims=True))
        alpha = jnp.exp2(m - m_new)
        p = jnp.exp2(s - m_new)
        l = alpha * l + jnp.sum(p, axis=0, keepdims=True)
        acc = alpha * acc + jnp.dot(vt, p.astype(BF16), preferred_element_type=F32)
        return m_new, l, acc

    def scores(qi, j, hh):
        q0 = pl.multiple_of(qi * ATT_T, ATT_T)
        k0 = pl.multiple_of(j * ATT_T, ATT_T)
        return lax.dot_general(k_ref[0, pl.ds(k0, ATT_T), hsl[hh]], q_ref[0, pl.ds(q0, ATT_T), hsl[hh]], nt,
                               preferred_element_type=F32)

    for hh in heads:
        s_sc[0, hh] = scores(0, 0, hh)

    def q_body(qi, carry):
        row0 = pl.multiple_of(qi * ATT_T, ATT_T)
        slot = qi % 2
        state = []
        for hh in heads:
            state += [jnp.full((1, ATT_T), NEG_INF, F32), jnp.zeros((1, ATT_T), F32),
                      jnp.zeros((V_DIM, ATT_T), F32)]

        def kv_body(j, st):
            out = []
            for hh in heads:
                s_cur = s_sc[slot, hh]
                s_next = scores(qi, j + 1, hh)
                out += update(s_cur, vt_ref[0, j, vsl[hh], :], *st[3 * hh:3 * hh + 3])
                s_sc[slot, hh] = s_next
            return tuple(out)

        state = lax.fori_loop(0, qi, kv_body, tuple(state))
        q_next = jnp.minimum(qi + 1, nq - 1)
        sms = [lax.dot_general(km_ref[:, hsl[hh]], q_ref[0, pl.ds(row0, ATT_T), hsl[hh]], nt,
                               preferred_element_type=F32) for hh in heads]
        for hh in heads:
            s_sc[1 - slot, hh] = scores(q_next, 0, hh)
        outs = []
        for hh in heads:
            m, l, acc = state[3 * hh:3 * hh + 3]
            s = jnp.where(diag_visible, s_sc[slot, hh], NEG_INF)
            sm = sms[hh]
            m_new = jnp.maximum(m, jnp.maximum(jnp.max(s, axis=0, keepdims=True),
                                               jnp.max(sm, axis=0, keepdims=True)))
            alpha = jnp.exp2(m - m_new)
            p = jnp.exp2(s - m_new)
            pm = jnp.exp2(sm - m_new)
            l = alpha * l + jnp.sum(p, axis=0, keepdims=True) + jnp.sum(pm, axis=0, keepdims=True)
            acc = (alpha * acc + jnp.dot(vt_ref[0, qi, vsl[hh], :], p.astype(BF16), preferred_element_type=F32)
                   + jnp.dot(vtm_ref[0, 0, vsl[hh], :], pm.astype(BF16), preferred_element_type=F32))
            outs.append(acc / l)
        o_ref[0, pl.ds(row0, ATT_T), :] = jnp.concatenate(outs, axis=0).T
        return carry

    lax.fori_loop(0, nq, q_body, 0)


def _attn_call(q, k, vt, k_meta, vt_meta):
    b, seq, _ = q.shape
    nk = seq // ATT_T
    hb = ATT_HEADS
    return pl.pallas_call(
        _attn_kernel,
        grid=(b, N_HEADS // hb),
        in_specs=[pl.BlockSpec((1, seq, hb * HEAD_PAD), lambda i, j: (i, 0, j)),
                  pl.BlockSpec((1, seq, hb * HEAD_PAD), lambda i, j: (i, 0, j)),
                  pl.BlockSpec((1, nk, hb * V_DIM, ATT_T), lambda i, j: (i, 0, j, 0)),
                  pl.BlockSpec((N_META, hb * HEAD_PAD), lambda i, j: (0, j)),
                  pl.BlockSpec((1, 1, hb * V_DIM, N_META), lambda i, j: (0, 0, j, 0))],
        out_specs=pl.BlockSpec((1, seq, hb * V_DIM), lambda i, j: (i, 0, j)),
        out_shape=jax.ShapeDtypeStruct((b, seq, D_ATTN), F32),
        scratch_shapes=[pltpu.VMEM((2, hb, ATT_T, ATT_T), F32)],
        compiler_params=pltpu.CompilerParams(dimension_semantics=("arbitrary", "arbitrary"),
                                             vmem_limit_bytes=VMEM_LIMIT),
        name="attn",
    )(q, k, vt, k_meta, vt_meta)


def _mix_kernel(x_ref, ig_ref, ib_ref, u_ref, uh_ref, pre_ref, cw_ref, cb_ref, clg_ref, clb_ref,
                cog_ref, o_ref, aog_ref, wo_ref, l1g_ref, l1b_ref, wrh_ref, wrl_ref, br_ref,
                h1_ref, hp_ref, ri_ref, rw_ref, cnt_ref, win_sc, sh_sc, carry_sc):
    bi = pl.program_id(0)
    ti = pl.program_id(1)
    tm = MIX_TM

    @pl.when((bi == 0) & (ti == 0))
    def _():
        carry_sc[...] = jnp.zeros_like(carry_sc)

    win_sc[0:HALO, :] = jnp.where(ti == 0, pre_ref[...], uh_ref[0])
    win_sc[HALO:, :] = u_ref[0]
    first = HALO - (CONV_K - 1)
    acc = jnp.broadcast_to(cb_ref[...], (tm, C_CONV))
    for res in range(SUBLANES):
        offs = [o for o in range(first, first + CONV_K) if o % SUBLANES == res]
        rows = tm + offs[-1] - res
        if res:
            sh_sc[res - 1, 0:rows, :] = win_sc[pl.ds(res, rows), :]
        for o in offs:
            src = sh_sc[res - 1, o - res:o - res + tm, :] if res else win_sc[o:o + tm, :]
            acc = acc + src * cw_ref[o - first:o - first + 1, :]
    c = _ln(acc, clg_ref[...], clb_ref[...])
    c = c * _sigmoid(c)
    cn = _rms(c, cog_ref[...])
    on = _rms(o_ref[0], aog_ref[...])
    y = (jnp.dot(cn.astype(BF16), wo_ref[0:C_CONV, :], preferred_element_type=F32)
         + jnp.dot(on.astype(BF16), wo_ref[C_CONV:, :], preferred_element_type=F32))
    h = _ln(x_ref[0], ig_ref[...], ib_ref[...])
    h1 = _ln(DEEPNORM_ALPHA * h + y, l1g_ref[...], l1b_ref[...])
    h1_ref[0] = h1
    hp_ref[0] = _pack_bf16_pair(h1[:, :HALF_D], h1[:, HALF_D:])

    nt = (((1,), (1,)), ((), ()))
    h_hi = h1.astype(BF16)
    h_lo = (h1 - h_hi.astype(F32)).astype(BF16)
    lt = (lax.dot_general(wrh_ref[...], h_hi, nt, preferred_element_type=F32)
          + lax.dot_general(wrl_ref[...], h_hi, nt, preferred_element_type=F32)
          + lax.dot_general(wrh_ref[...], h_lo, nt, preferred_element_type=F32)) + br_ref[:, 0:1]
    row = lax.broadcasted_iota(jnp.int32, (ROUTE_ROWS, tm), 0)
    big = jnp.int32(1 << 20)
    is_g = (row >= N_EXPERTS) & (row < N_EXPERTS + N_GROUPS)
    gl = jnp.where(is_g, lt, -jnp.inf)
    gmax = jnp.max(gl, axis=0, keepdims=True)
    g_idx = jnp.min(jnp.where(gl == gmax, row - N_EXPERTS, big), axis=0, keepdims=True)
    g_w = 1.0 / jnp.sum(jnp.exp(gl - gmax), axis=0, keepdims=True)
    in_grp = (row < N_EXPERTS) & ((row // EXPERTS_PER_GROUP) == g_idx)
    el = jnp.where(in_grp, lt, -jnp.inf)
    v1 = jnp.max(el, axis=0, keepdims=True)
    i1 = jnp.min(jnp.where(el == v1, row, big), axis=0, keepdims=True)
    el2 = jnp.where(row == i1, -jnp.inf, el)
    v2 = jnp.max(el2, axis=0, keepdims=True)
    i2 = jnp.min(jnp.where(el2 == v2, row, big), axis=0, keepdims=True)
    e2 = jnp.exp(v2 - v1)
    w1 = g_w / (1.0 + e2)
    w2 = g_w * e2 / (1.0 + e2)

    oh1 = row == i1
    oh2 = row == i2
    oh = jnp.where(oh1 | oh2, 1.0, 0.0)
    earlier = (lax.broadcasted_iota(jnp.int32, (tm, tm), 0) < lax.broadcasted_iota(jnp.int32, (tm, tm), 1))
    before = (jnp.dot(oh.astype(BF16), jnp.where(earlier, 1.0, 0.0).astype(BF16), preferred_element_type=F32)
              + carry_sc[:, 0:1])
    r1 = jnp.sum(jnp.where(oh1, before, 0.0), axis=0, keepdims=True).astype(jnp.int32)
    r2 = jnp.sum(jnp.where(oh2, before, 0.0), axis=0, keepdims=True).astype(jnp.int32)
    carry_sc[...] = carry_sc[...] + jnp.sum(oh, axis=1, keepdims=True)
    cnt_ref[...] = carry_sc[...]

    sub = lax.broadcasted_iota(jnp.int32, (SUBLANES, tm), 0)
    ri_ref[...] = jnp.where(sub == 0, i1, jnp.where(sub == 1, i2, jnp.where(sub == 2, r1,
                            jnp.where(sub == 3, r2, 0))))
    lrow = lax.broadcasted_iota(jnp.int32, (LANES, tm), 0)
    rw_ref[0] = jnp.where(lrow == 0, w1, jnp.where(lrow == 1, w2, 0.0)).T[:, :8]


def _mix_call(x, ig, ib, u, pre, cw, cb, clg, clb, cog, o, aog, wo, l1g, l1b, wrh, wrl, br):
    b, seq, _ = x.shape
    tm = MIX_TM
    nt = seq // tm
    full = lambda shape: pl.BlockSpec(shape, lambda i, j: (0,) * len(shape))
    tile = lambda w: pl.BlockSpec((1, tm, w), lambda i, j: (i, j, 0))
    halo = pl.BlockSpec((1, HALO, C_CONV), lambda i, j: (i, jnp.maximum(j * (tm // HALO) - 1, 0), 0))
    return pl.pallas_call(
        _mix_kernel,
        grid=(b, nt),
        in_specs=[tile(D_MODEL), full((1, D_MODEL)), full((1, D_MODEL)),
                  tile(C_CONV), halo, full((HALO, C_CONV)), full((HALO, C_CONV)), full((1, C_CONV)),
                  full((1, C_CONV)), full((1, C_CONV)), full((1, C_CONV)),
                  tile(D_ATTN), full((1, D_ATTN)), full((D_MODEL, D_MODEL)),
                  full((1, D_MODEL)), full((1, D_MODEL)), full((ROUTE_ROWS, D_MODEL)),
                  full((ROUTE_ROWS, D_MODEL)), full((ROUTE_ROWS, LANES))],
        out_specs=[tile(D_MODEL), tile(HALF_D), pl.BlockSpec((8, tm), lambda i, j: (0, i * nt + j)), tile(8),
                   full((ROUTE_ROWS, LANES))],
        out_shape=[jax.ShapeDtypeStruct((b, seq, D_MODEL), F32),
                   jax.ShapeDtypeStruct((b, seq, HALF_D), jnp.uint32),
                   jax.ShapeDtypeStruct((8, b * seq), jnp.int32),
                   jax.ShapeDtypeStruct((b, seq, 8), F32),
                   jax.ShapeDtypeStruct((ROUTE_ROWS, LANES), F32)],
        scratch_shapes=[pltpu.VMEM((HALO + tm, C_CONV), F32),
                        pltpu.VMEM((SUBLANES - 1, HALO + tm, C_CONV), F32),
                        pltpu.VMEM((ROUTE_ROWS, LANES), F32)],
        compiler_params=pltpu.CompilerParams(dimension_semantics=("arbitrary", "arbitrary"),
                                             vmem_limit_bytes=VMEM_LIMIT),
        name="mix",
    )(x, ig, ib, u, u, pre, cw, cb, clg, clb, cog, o, aog, wo, l1g, l1b, wrh, wrl, br)


def _tables_kernel(ps_ref, pe_ref, cn_ref, ri_ref, dest_ref, blk_ref):
    ri = ri_ref[...]
    start = jnp.zeros_like(ri)
    for e in range(N_EXPERTS):
        start = jnp.where(ri == e, ps_ref[e], start)
    dest_ref[...] = jnp.zeros_like(dest_ref)
    dest_ref[0:2, :] = start[0:2, :] + ri_ref[2:4, :]
    blk_start = lax.broadcasted_iota(jnp.int32, blk_ref.shape, 1) * MOE_BM
    be = jnp.zeros(blk_ref.shape, jnp.int32)
    for e in range(N_EXPERTS):
        be = be + jnp.where(pe_ref[e] <= blk_start, 1, 0)
    be = jnp.minimum(be, N_EXPERTS - 1)
    end = jnp.zeros(blk_ref.shape, jnp.int32)
    for e in range(N_EXPERTS):
        end = jnp.where(be == e, ps_ref[e] + cn_ref[e], end)
    valid = jnp.clip(end - blk_start, 0, MOE_BM)
    row = lax.broadcasted_iota(jnp.int32, blk_ref.shape, 0)
    blk_ref[...] = jnp.where(row == 0, be, valid)


def _tables_call(pad_start, pad_end, counts, ri, n_blocks):
    t = ri.shape[1]
    blk_w = -(-n_blocks // LANES) * LANES
    return pl.pallas_call(
        _tables_kernel,
        grid_spec=pltpu.PrefetchScalarGridSpec(
            num_scalar_prefetch=3,
            grid=(1,),
            in_specs=[pl.BlockSpec((SUBLANES, t), lambda i, *_: (0, 0))],
            out_specs=[pl.BlockSpec((SUBLANES, t), lambda i, *_: (0, 0)),
                       pl.BlockSpec((SUBLANES, blk_w), lambda i, *_: (0, 0))]),
        out_shape=[jax.ShapeDtypeStruct((SUBLANES, t), jnp.int32),
                   jax.ShapeDtypeStruct((SUBLANES, blk_w), jnp.int32)],
        compiler_params=pltpu.CompilerParams(dimension_semantics=("arbitrary",),
                                             vmem_limit_bytes=VMEM_LIMIT),
        name="tables",
    )(pad_start, pad_end, counts, ri)


def _expert_kernel(be_ref, nv_ref, x_ref, wg_ref, wu_ref, wd_ref, y_ref, wg_sc, wu_sc, wd_sc):
    i = pl.program_id(0)
    nv = nv_ref[i]

    @pl.when((i == 0) | (be_ref[i] != be_ref[jnp.maximum(i - 1, 0)]))
    def _():
        wg_sc[...] = wg_ref[0, 0].astype(BF16)
        wu_sc[...] = wu_ref[0, 0].astype(BF16)
        wd_sc[...] = wd_ref[0, 0].astype(BF16)

    @pl.when(nv > 0)
    def _():
        rowid = lax.broadcasted_iota(jnp.int32, x_ref.shape, 0)
        x_lo, x_hi = _unpack_bf16_pair(jnp.where(rowid < nv, x_ref[...], jnp.uint32(0)))
        x_lo = x_lo.astype(BF16)
        x_hi = x_hi.astype(BF16)
        g = (jnp.dot(x_lo, wg_sc[:HALF_D, :], preferred_element_type=F32)
             + jnp.dot(x_hi, wg_sc[HALF_D:, :], preferred_element_type=F32))
        up = (jnp.dot(x_lo, wu_sc[:HALF_D, :], preferred_element_type=F32)
              + jnp.dot(x_hi, wu_sc[HALF_D:, :], preferred_element_type=F32))
        act = g * _sigmoid(g) * up
        y = jnp.dot(act.astype(BF16), wd_sc[...], preferred_element_type=F32)
        y_ref[...] = _pack_bf16_pair(y[:, :HALF_D], y[:, HALF_D:])

    @pl.when(nv == 0)
    def _():
        y_ref[...] = jnp.zeros_like(y_ref)


def _expert_call(blk_e, blk_valid, xb, w_gate, w_up, w_down):
    r = xb.shape[0]
    bm = MOE_BM
    wspec = lambda a, b: pl.BlockSpec((1, 1, a, b), lambda i, be, nv: (0, be[i], 0, 0))
    return pl.pallas_call(
        _expert_kernel,
        grid_spec=pltpu.PrefetchScalarGridSpec(
            num_scalar_prefetch=2,
            grid=(r // bm,),
            in_specs=[pl.BlockSpec((bm, HALF_D), lambda i, be, nv: (i, 0)),
                      wspec(D_MODEL, D_EXPERT), wspec(D_MODEL, D_EXPERT), wspec(D_EXPERT, D_MODEL)],
            out_specs=pl.BlockSpec((bm, HALF_D), lambda i, be, nv: (i, 0)),
            scratch_shapes=[pltpu.VMEM((D_MODEL, D_EXPERT), BF16), pltpu.VMEM((D_MODEL, D_EXPERT), BF16),
                            pltpu.VMEM((D_EXPERT, D_MODEL), BF16)]),
        out_shape=jax.ShapeDtypeStruct((r, HALF_D), jnp.uint32),
        compiler_params=pltpu.CompilerParams(dimension_semantics=("arbitrary",),
                                             vmem_limit_bytes=VMEM_LIMIT),
        name="experts",
    )(blk_e, blk_valid, xb, w_gate, w_up, w_down)


def _sc_gather_rows(table, idx):
    m = idx.shape[0]
    width = table.shape[1]
    sc = plsc.get_sparse_core_info()
    workers = sc.num_cores * sc.num_subcores
    per_worker = m // workers
    mesh = plsc.VectorSubcoreMesh(core_axis_name="c", subcore_axis_name="s")

    n_win = per_worker // SC_WINDOW
    assert n_win % 2 == 0 and n_win >= 2
    buf = lambda: [pltpu.VMEM((SC_WINDOW,), jnp.int32), pltpu.VMEM((SC_WINDOW, width), table.dtype),
                   pltpu.SemaphoreType.DMA]

    @functools.partial(pl.kernel, mesh=mesh, out_type=jax.ShapeDtypeStruct((m, width), table.dtype),
                       scratch_types=buf() + buf())
    def gather(table_hbm, idx_hbm, out_hbm, idx_a, rows_a, sem_a, idx_b, rows_b, sem_b):
        base = (lax.axis_index("s") * sc.num_cores + lax.axis_index("c")) * per_worker
        bufs = ((idx_a, rows_a, sem_a), (idx_b, rows_b, sem_b))

        def fetch(win, which):
            idx_v, rows_v, sem = bufs[which]
            pltpu.sync_copy(idx_hbm.at[pl.ds(base + win * SC_WINDOW, SC_WINDOW)], idx_v)
            pltpu.async_copy(table_hbm.at[idx_v], rows_v, sem)

        def flush(win, which):
            idx_v, rows_v, sem = bufs[which]
            pltpu.make_async_copy(table_hbm.at[idx_v], rows_v, sem).wait()
            pltpu.sync_copy(rows_v, out_hbm.at[pl.ds(base + win * SC_WINDOW, SC_WINDOW)])

        fetch(0, 0)

        @pl.loop(0, n_win // 2 - 1)
        def _(pair):
            win = 2 * pair
            fetch(win + 1, 1)
            flush(win, 0)
            fetch(win + 2, 0)
            flush(win + 1, 1)

        fetch(n_win - 1, 1)
        flush(n_win - 2, 0)
        flush(n_win - 1, 1)

    return gather(table, idx)


def _sc_scatter_rows(rows, idx0, idx1, n_out):
    m, width = rows.shape
    sc = plsc.get_sparse_core_info()
    workers = sc.num_cores * sc.num_subcores
    per_worker = m // workers
    mesh = plsc.VectorSubcoreMesh(core_axis_name="c", subcore_axis_name="s")

    n_win = per_worker // SC_WINDOW
    assert n_win % 2 == 0 and n_win >= 2
    buf = lambda: [pltpu.VMEM((SC_WINDOW,), jnp.int32), pltpu.VMEM((SC_WINDOW,), jnp.int32),
                   pltpu.VMEM((SC_WINDOW, width), rows.dtype),
                   pltpu.SemaphoreType.DMA, pltpu.SemaphoreType.DMA, pltpu.SemaphoreType.DMA]

    @functools.partial(pl.kernel, mesh=mesh, out_type=jax.ShapeDtypeStruct((n_out, width), rows.dtype),
                       scratch_types=buf() + buf())
    def scatter(rows_hbm, idx0_hbm, idx1_hbm, out_hbm, *scratch):
        base = (lax.axis_index("s") * sc.num_cores + lax.axis_index("c")) * per_worker
        bufs = (scratch[:6], scratch[6:])

        def load(win, which):
            idx0_v, idx1_v, rows_v, lsem, _, _ = bufs[which]
            off = base + win * SC_WINDOW
            pltpu.sync_copy(idx0_hbm.at[pl.ds(off, SC_WINDOW)], idx0_v)
            pltpu.sync_copy(idx1_hbm.at[pl.ds(off, SC_WINDOW)], idx1_v)
            pltpu.async_copy(rows_hbm.at[pl.ds(off, SC_WINDOW)], rows_v, lsem)

        def send(which):
            idx0_v, idx1_v, rows_v, lsem, sem0, sem1 = bufs[which]
            pltpu.make_async_copy(rows_hbm.at[pl.ds(0, SC_WINDOW)], rows_v, lsem).wait()
            first = pltpu.async_copy(rows_v, out_hbm.at[idx0_v], sem0)
            second = pltpu.async_copy(rows_v, out_hbm.at[idx1_v], sem1)
            first.wait()
            second.wait()

        load(0, 0)

        @pl.loop(0, n_win // 2 - 1)
        def _(pair):
            win = 2 * pair
            load(win + 1, 1)
            send(0)
            load(win + 2, 0)
            send(1)

        load(n_win - 1, 1)
        send(0)
        send(1)

    return scatter(rows, idx0, idx1)


def _combine_rows_kernel(h_ref, w_ref, g_ref, b_ref, y0_ref, y1_ref, *rest):
    o_ref = rest[-1]
    w = w_ref[...]
    y0_lo, y0_hi = _unpack_bf16_pair(y0_ref[...])
    y1_lo, y1_hi = _unpack_bf16_pair(y1_ref[...])
    y = jnp.concatenate([w[:, 0:1] * y0_lo + w[:, 1:2] * y1_lo, w[:, 0:1] * y0_hi + w[:, 1:2] * y1_hi], axis=1)
    o_ref[...] = _ln(DEEPNORM_ALPHA * h_ref[...] + y, g_ref[...], b_ref[...])


def _combine_rows_call(h1, rw, g, b, yg, part, n_parts, prev=None):
    t = h1.shape[0]
    tm = COMB_TM
    n = t // tm // n_parts
    first = part * n
    in_specs = [pl.BlockSpec((tm, D_MODEL), lambda i: (first + i, 0)),
                pl.BlockSpec((tm, 8), lambda i: (first + i, 0)),
                pl.BlockSpec((1, D_MODEL), lambda i: (0, 0)),
                pl.BlockSpec((1, D_MODEL), lambda i: (0, 0)),
                pl.BlockSpec((tm, HALF_D), lambda i: (i, 0)),
                pl.BlockSpec((tm, HALF_D), lambda i: (n + i, 0))]
    args = [h1, rw, g, b, yg, yg]
    aliases = {}
    if prev is not None:
        in_specs.append(pl.BlockSpec(memory_space=pl.ANY))
        args.append(prev)
        aliases = {len(args) - 1: 0}
    return pl.pallas_call(
        _combine_rows_kernel,
        grid=(n,),
        in_specs=in_specs,
        out_specs=pl.BlockSpec((tm, D_MODEL), lambda i: (first + i, 0)),
        out_shape=jax.ShapeDtypeStruct((t, D_MODEL), F32),
        input_output_aliases=aliases,
        compiler_params=pltpu.CompilerParams(dimension_semantics=("arbitrary",),
                                             vmem_limit_bytes=VMEM_LIMIT),
        name="combine_rows",
    )(*args)


def _take_cols(w, idx):
    wz = jnp.concatenate([w, jnp.zeros((w.shape[0], 1), w.dtype)], axis=1)
    return wz[:, np.where(idx < 0, w.shape[1], idx)]


def _layout_indices():
    lane_src = np.full((HEAD_PAD,), -1, np.int64)
    lane_src[0:ROPE_HALF] = QK_NOPE + np.arange(ROPE_HALF)
    lane_src[ROPE_HALF:X2_LANE] = np.arange(X2_LANE - ROPE_HALF)
    lane_src[X2_LANE:X2_LANE + ROPE_HALF] = QK_NOPE + ROPE_HALF + np.arange(ROPE_HALF)
    n_lo = X2_LANE - ROPE_HALF
    lane_src[X2_LANE + ROPE_HALF:X2_LANE + ROPE_HALF + QK_NOPE - n_lo] = n_lo + np.arange(QK_NOPE - n_lo)
    q_idx = np.concatenate([np.where(lane_src >= 0, lane_src + (QK_NOPE + QK_ROPE) * h, -1)
                            for h in range(N_HEADS)])
    nope_src = np.where(lane_src < QK_NOPE, lane_src, -1)
    k_idx = np.concatenate([np.where(nope_src >= 0, nope_src + (QK_NOPE + V_DIM) * h, -1)
                            for h in range(N_HEADS)])
    v_idx = np.concatenate([QK_NOPE + np.arange(V_DIM) + (QK_NOPE + V_DIM) * h for h in range(N_HEADS)])
    kr_idx = np.full((LANES,), -1, np.int64)
    kr_idx[0:ROPE_HALF] = np.arange(ROPE_HALF)
    kr_idx[X2_LANE:X2_LANE + ROPE_HALF] = ROPE_HALF + np.arange(ROPE_HALF)
    return q_idx, k_idx, v_idx, kr_idx


def _rope_lane_tables(length):
    inv_freq = np.float32(ROPE_THETA) ** (-np.arange(ROPE_HALF, dtype=np.float32) / np.float32(ROPE_HALF))
    ang = np.arange(length, dtype=np.float32)[:, None] * inv_freq[None, :].astype(np.float32)
    cos, sin = np.cos(ang).astype(np.float32), np.sin(ang).astype(np.float32)
    ctab = np.ones((length, LANES), np.float32)
    stab = np.zeros((length, LANES), np.float32)
    for lo, sign in ((0, -1.0), (X2_LANE, 1.0)):
        ctab[:, lo:lo + ROPE_HALF] = cos
        stab[:, lo:lo + ROPE_HALF] = sign * sin
    return jnp.asarray(ctab), jnp.asarray(stab)


def kernel(x, meta_tokens, ln_in_g, ln_in_b, w_in, conv_w, conv_b, conv_ln_g, conv_ln_b, q_norm_g, w_uq,
           kv_norm_g, w_ukv, conv_out_g, attn_out_g, w_o, ln1_g, ln1_b, w_rg, b_rg, w_re, b_re,
           w_gate, w_up, w_down, ln2_g, ln2_b):
    assert w_in.shape[0] == 1, "single layer"
    bsz, seq, d = x.shape
    t = bsz * seq
    row = lambda a: a.reshape(1, -1)

    q_idx, k_idx, v_idx, kr_idx = _layout_indices()
    c4 = 2 * C_CONV + Q_LORA + KV_LORA
    w_in_l = jnp.concatenate([w_in[0][:, :c4], _take_cols(w_in[0][:, c4:], kr_idx)], axis=1).astype(BF16)
    wuq_l = _take_cols(w_uq[0], q_idx).astype(BF16)
    wuk_l = _take_cols(w_ukv[0], k_idx).astype(BF16)
    wvt_l = w_ukv[0][:, v_idx].T.astype(BF16)
    wo_l = w_o[0].astype(BF16)
    n_route_pad = ROUTE_ROWS - N_EXPERTS - N_GROUPS
    wr_t = jnp.concatenate([w_re[0].T, w_rg[0].T, jnp.zeros((n_route_pad, d), F32)], axis=0)
    wrh_l = wr_t.astype(BF16)
    wrl_l = (wr_t - wrh_l.astype(F32)).astype(BF16)
    br_l = jnp.broadcast_to(jnp.concatenate([b_re[0], b_rg[0], jnp.zeros((n_route_pad,), F32)])[:, None],
                            (ROUTE_ROWS, LANES))
    ctab, stab = _rope_lane_tables(N_META + seq)

    proj_w = (row(ln_in_g), row(ln_in_b), w_in_l, row(q_norm_g[0]), wuq_l, row(kv_norm_g[0]), wuk_l, wvt_l)
    u_m, _, k_m, vt_m = _proj_call(meta_tokens.astype(F32), N_META, 1, *proj_w, ctab[:N_META], stab[:N_META])
    u, q, k, vt = _proj_call(x.reshape(t, d), PROJ_TM, seq // PROJ_TM, *proj_w, ctab[N_META:], stab[N_META:])

    o = _attn_call(q.reshape(bsz, seq, -1), k.reshape(bsz, seq, -1),
                   vt.reshape(bsz, seq // ATT_T, D_ATTN, ATT_T), k_m, vt_m)

    pre = jnp.concatenate([jnp.zeros((HALO - N_META, C_CONV), F32), u_m], axis=0)
    cw = jnp.concatenate([conv_w[0], jnp.zeros((HALO - CONV_K, C_CONV), F32)], axis=0)
    h1, hp, ri, rw, cnt = _mix_call(x, row(ln_in_g), row(ln_in_b), u.reshape(bsz, seq, C_CONV), pre, cw,
                                    row(conv_b[0]), row(conv_ln_g[0]), row(conv_ln_b[0]), row(conv_out_g[0]),
                                    o, row(attn_out_g[0]), wo_l, row(ln1_g[0]), row(ln1_b[0]), wrh_l, wrl_l, br_l)

    counts = cnt[:N_EXPERTS, 0].astype(jnp.int32)
    padded = (counts + MOE_BM - 1) // MOE_BM * MOE_BM
    pad_end = jnp.cumsum(padded)
    pad_start = pad_end - padded
    n_blocks = (2 * t) // MOE_BM + N_EXPERTS
    dest, blk = _tables_call(pad_start, pad_end, counts, ri, n_blocks)
    blk_e, blk_valid = blk[0, :n_blocks], blk[1, :n_blocks]

    h1f = h1.reshape(t, d)
    dest0, dest1 = dest[0], dest[1]
    xb = _sc_scatter_rows(hp.reshape(t, HALF_D), dest0, dest1, n_blocks * MOE_BM)
    yb = _expert_call(blk_e, blk_valid, xb, w_gate, w_up, w_down)
    tp = t // COMBINE_PARTS
    out = None
    for part in range(COMBINE_PARTS):
        sl = slice(part * tp, (part + 1) * tp)
        yg = _sc_gather_rows(yb, jnp.concatenate([dest0[sl], dest1[sl]]))
        out = _combine_rows_call(h1f, rw.reshape(t, 8), row(ln2_g[0]), row(ln2_b[0]), yg, part, COMBINE_PARTS,
                                 prev=out)
    return out.reshape(bsz, seq, d)
```

```python
import functools
import math

import numpy as np
import jax
import jax.numpy as jnp
from jax import lax
from jax.experimental import pallas as pl
from jax.experimental.pallas import tpu as pltpu
from jax.experimental.pallas import tpu_sc as plsc

F32 = jnp.float32
BF16 = jnp.bfloat16

D_MODEL = 1024
HALF_D = D_MODEL // 2
N_META = 16
CHUNK = 64
C_CONV = 512
CONV_K = 31
N_HEADS = 8
QK_NOPE = 64
QK_ROPE = 32
V_DIM = 64
Q_LORA = 384
KV_LORA = 256
ROPE_THETA = 10000.0
D_ATTN = N_HEADS * V_DIM
N_GROUPS = 4
EXPERTS_PER_GROUP = 8
N_EXPERTS = N_GROUPS * EXPERTS_PER_GROUP
D_EXPERT = 256
DEEPNORM_ALPHA = 2.0 ** 0.25
EPS = 1e-5
NEG_INF = -1e30

LANES = 128
SUBLANES = 8
HEAD_PAD = LANES
ROPE_HALF = QK_ROPE // 2
X2_LANE = 64
Z_COLS = 2 * C_CONV + Q_LORA + KV_LORA + LANES
Q_SCALE = (QK_NOPE + QK_ROPE) ** -0.5 * math.log2(math.e)

PROJ_TM = 1024
ATT_T = 256
ATT_HEADS = 8
MIX_TM = 512
ROUTE_ROWS = 48
HALO = 32
MOE_BM = 512
COMB_TM = 1024
COMBINE_PARTS = 4
SC_WINDOW = 64
VMEM_LIMIT = 56 * 1024 * 1024


def _ln(x, g, b):
    mu = jnp.mean(x, axis=-1, keepdims=True)
    xc = x - mu
    var = jnp.mean(xc * xc, axis=-1, keepdims=True)
    return xc * lax.rsqrt(var + EPS) * g + b


def _rms(x, g):
    return x * lax.rsqrt(jnp.mean(x * x, axis=-1, keepdims=True) + EPS) * g


def _sigmoid(x):
    return 1.0 / (1.0 + jnp.exp(-x))


def _pack_bf16_pair(lo, hi):
    lo_bits = lax.bitcast_convert_type(lo.astype(BF16).astype(F32), jnp.uint32)
    hi_bits = lax.bitcast_convert_type(hi.astype(BF16).astype(F32), jnp.uint32)
    return (lo_bits >> 16) | (hi_bits & jnp.uint32(0xFFFF0000))


def _unpack_bf16_pair(packed):
    lo = lax.bitcast_convert_type(packed << 16, F32)
    hi = lax.bitcast_convert_type(packed & jnp.uint32(0xFFFF0000), F32)
    return lo, hi


def _proj_kernel(x_ref, g_ref, b_ref, win_ref, qg_ref, wuq_ref, kvg_ref, wuk_ref, wvt_ref, c_ref, s_ref,
                 u_ref, q_ref, k_ref, vt_ref):
    h = _ln(x_ref[...], g_ref[...], b_ref[...])
    z = jnp.dot(h.astype(BF16), win_ref[...], preferred_element_type=F32)
    u_ref[...] = z[:, :C_CONV] * _sigmoid(z[:, C_CONV:2 * C_CONV])
    c0 = 2 * C_CONV
    cq = z[:, c0:c0 + Q_LORA]
    ckv = z[:, c0 + Q_LORA:c0 + Q_LORA + KV_LORA]
    kr = z[:, c0 + Q_LORA + KV_LORA:]
    cs = c_ref[...]
    sn = s_ref[...]
    q = jnp.dot(_rms(cq, qg_ref[...]).astype(BF16), wuq_ref[...], preferred_element_type=F32)
    kvn = _rms(ckv, kvg_ref[...]).astype(BF16)
    kn = jnp.dot(kvn, wuk_ref[...], preferred_element_type=F32)
    kr_rot = kr * cs + pltpu.roll(kr, X2_LANE, 1) * sn
    for hh in range(N_HEADS):
        sl = slice(HEAD_PAD * hh, HEAD_PAD * (hh + 1))
        qh = q[:, sl]
        qh = (qh * cs + pltpu.roll(qh, X2_LANE, 1) * sn) * Q_SCALE
        q_ref[:, sl] = qh.astype(BF16)
        k_ref[:, sl] = (kn[:, sl] + kr_rot).astype(BF16)
    vt = lax.dot_general(wvt_ref[...], kvn, (((1,), (1,)), ((), ())), preferred_element_type=F32)
    n_sub, sub_w = vt_ref.shape[1], vt_ref.shape[3]
    for tt in range(n_sub):
        vt_ref[0, tt] = vt[:, tt * sub_w:(tt + 1) * sub_w].astype(BF16)


def _proj_call(x2d, tm, pos_blocks, ln_g, ln_b, w_in, qg, wuq, kvg, wuk, wvt, ctab, stab):
    t = x2d.shape[0]
    sub_w = min(tm, ATT_T)
    n_sub = tm // sub_w
    full = lambda shape: pl.BlockSpec(shape, lambda i: (0,) * len(shape))
    row = lambda w: pl.BlockSpec((tm, w), lambda i: (i, 0))
    tab = pl.BlockSpec((tm, LANES), lambda i: (i % pos_blocks, 0))
    return pl.pallas_call(
        _proj_kernel,
        grid=(t // tm,),
        in_specs=[row(D_MODEL), full((1, D_MODEL)), full((1, D_MODEL)), full((D_MODEL, Z_COLS)),
                  full((1, Q_LORA)), full((Q_LORA, N_HEADS * HEAD_PAD)),
                  full((1, KV_LORA)), full((KV_LORA, N_HEADS * HEAD_PAD)), full((D_ATTN, KV_LORA)), tab, tab],
        out_specs=[row(C_CONV), row(N_HEADS * HEAD_PAD), row(N_HEADS * HEAD_PAD),
                   pl.BlockSpec((1, n_sub, D_ATTN, sub_w), lambda i: (i, 0, 0, 0))],
        out_shape=[jax.ShapeDtypeStruct((t, C_CONV), F32),
                   jax.ShapeDtypeStruct((t, N_HEADS * HEAD_PAD), BF16),
                   jax.ShapeDtypeStruct((t, N_HEADS * HEAD_PAD), BF16),
                   jax.ShapeDtypeStruct((t // tm, n_sub, D_ATTN, sub_w), BF16)],
        compiler_params=pltpu.CompilerParams(dimension_semantics=("arbitrary",),
                                             vmem_limit_bytes=VMEM_LIMIT),
        name="proj",
    )(x2d, ln_g, ln_b, w_in, qg, wuq, kvg, wuk, wvt, ctab, stab)


def _attn_kernel(q_ref, k_ref, vt_ref, km_ref, vtm_ref, o_ref, s_sc):
    seq = q_ref.shape[1]
    nq = seq // ATT_T
    key_chunk = lax.broadcasted_iota(jnp.int32, (ATT_T, ATT_T), 0) // CHUNK
    qry_chunk = lax.broadcasted_iota(jnp.int32, (ATT_T, ATT_T), 1) // CHUNK
    diag_visible = key_chunk <= qry_chunk
    nt = (((1,), (1,)), ((), ()))
    heads = range(ATT_HEADS)
    hsl = [slice(HEAD_PAD * hh, HEAD_PAD * (hh + 1)) for hh in heads]
    vsl = [slice(V_DIM * hh, V_DIM * (hh + 1)) for hh in heads]

    def update(s, vt, m, l, acc):
        m_new = jnp.maximum(m, jnp.max(s, axis=0, keepdims=True))
        alpha = jnp.exp2(m - m_new)
        p = jnp.exp2(s - m_new)
        l = alpha * l + jnp.sum(p, axis=0, keepdims=True)
        acc = alpha * acc + jnp.dot(vt, p.astype(BF16), preferred_element_type=F32)
        return m_new, l, acc

    def scores(qi, j, hh):
        q0 = pl.multiple_of(qi * ATT_T, ATT_T)
        k0 = pl.multiple_of(j * ATT_T, ATT_T)
        return lax.dot_general(k_ref[0, pl.ds(k0, ATT_T), hsl[hh]], q_ref[0, pl.ds(q0, ATT_T), hsl[hh]], nt,
                               preferred_element_type=F32)

    for hh in heads:
        s_sc[0, hh] = scores(0, 0, hh)

    def q_body(qi, carry):
        row0 = pl.multiple_of(qi * ATT_T, ATT_T)
        slot = qi % 2
        state = []
        for hh in heads:
            state += [jnp.full((1, ATT_T), NEG_INF, F32), jnp.zeros((1, ATT_T), F32),
                      jnp.zeros((V_DIM, ATT_T), F32)]

        def kv_body(j, st):
            out = []
            s_next = scores(qi, j + 1, 0)
            for hh in heads:
                s_cur = s_sc[slot, hh]
                s_after = scores(qi, j + 1, hh + 1) if hh + 1 < ATT_HEADS else None
                out += update(s_cur, vt_ref[0, j, vsl[hh], :], *st[3 * hh:3 * hh + 3])
                s_sc[slot, hh] = s_next
                s_next = s_after
            return tuple(out)

        state = lax.fori_loop(0, qi, kv_body, tuple(state))
        q_next = jnp.minimum(qi + 1, nq - 1)
        sms = [lax.dot_general(km_ref[:, hsl[hh]], q_ref[0, pl.ds(row0, ATT_T), hsl[hh]], nt,
                               preferred_element_type=F32) for hh in heads]
        outs = []
        for hh in heads:
            m, l, acc = state[3 * hh:3 * hh + 3]
            s = jnp.where(diag_visible, s_sc[slot, hh], NEG_INF)
            sm = sms[hh]
            m_new = jnp.maximum(m, jnp.maximum(jnp.max(s, axis=0, keepdims=True),
                                               jnp.max(sm, axis=0, keepdims=True)))
            alpha = jnp.exp2(m - m_new)
            p = jnp.exp2(s - m_new)
            pm = jnp.exp2(sm - m_new)
            l = alpha * l + jnp.sum(p, axis=0, keepdims=True) + jnp.sum(pm, axis=0, keepdims=True)
            acc = (alpha * acc + jnp.dot(vt_ref[0, qi, vsl[hh], :], p.astype(BF16), preferred_element_type=F32)
                   + jnp.dot(vtm_ref[0, 0, vsl[hh], :], pm.astype(BF16), preferred_element_type=F32))
            outs.append(acc / l)
        for hh in heads:
            s_sc[1 - slot, hh] = scores(q_next, 0, hh)
        o_ref[0, pl.ds(row0, ATT_T), :] = jnp.concatenate(outs, axis=0).T
        return carry

    lax.fori_loop(0, nq, q_body, 0)


def _attn_call(q, k, vt, k_meta, vt_meta):
    b, seq, _ = q.shape
    nk = seq // ATT_T
    hb = ATT_HEADS
    return pl.pallas_call(
        _attn_kernel,
        grid=(b, N_HEADS // hb),
        in_specs=[pl.BlockSpec((1, seq, hb * HEAD_PAD), lambda i, j: (i, 0, j)),
                  pl.BlockSpec((1, seq, hb * HEAD_PAD), lambda i, j: (i, 0, j)),
                  pl.BlockSpec((1, nk, hb * V_DIM, ATT_T), lambda i, j: (i, 0, j, 0)),
                  pl.BlockSpec((N_META, hb * HEAD_PAD), lambda i, j: (0, j)),
                  pl.BlockSpec((1, 1, hb * V_DIM, N_META), lambda i, j: (0, 0, j, 0))],
        out_specs=pl.BlockSpec((1, seq, hb * V_DIM), lambda i, j: (i, 0, j)),
        out_shape=jax.ShapeDtypeStruct((b, seq, D_ATTN), F32),
        scratch_shapes=[pltpu.VMEM((2, hb, ATT_T, ATT_T), F32)],
        compiler_params=pltpu.CompilerParams(dimension_semantics=("arbitrary", "arbitrary"),
                                             vmem_limit_bytes=VMEM_LIMIT),
        name="attn",
    )(q, k, vt, k_meta, vt_meta)


def _mix_kernel(x_ref, ig_ref, ib_ref, u_ref, uh_ref, pre_ref, cw_ref, cb_ref, clg_ref, clb_ref,
                cog_ref, o_ref, aog_ref, wo_ref, l1g_ref, l1b_ref, wrh_ref, wrl_ref, br_ref,
                h1_ref, hp_ref, ri_ref, rw_ref, cnt_ref, win_sc, sh_sc, carry_sc):
    bi = pl.program_id(0)
    ti = pl.program_id(1)
    tm = MIX_TM

    @pl.when((bi == 0) & (ti == 0))
    def _():
        carry_sc[...] = jnp.zeros_like(carry_sc)

    win_sc[0:HALO, :] = jnp.where(ti == 0, pre_ref[...], uh_ref[0])
    win_sc[HALO:, :] = u_ref[0]
    first = HALO - (CONV_K - 1)
    acc = jnp.broadcast_to(cb_ref[...], (tm, C_CONV))
    for res in range(SUBLANES):
        offs = [o for o in range(first, first + CONV_K) if o % SUBLANES == res]
        rows = tm + offs[-1] - res
        if res:
            sh_sc[res - 1, 0:rows, :] = win_sc[pl.ds(res, rows), :]
        for o in offs:
            src = sh_sc[res - 1, o - res:o - res + tm, :] if res else win_sc[o:o + tm, :]
            acc = acc + src * cw_ref[o - first:o - first + 1, :]
    c = _ln(acc, clg_ref[...], clb_ref[...])
    c = c * _sigmoid(c)
    cn = _rms(c, cog_ref[...])
    on = _rms(o_ref[0], aog_ref[...])
    y = (jnp.dot(cn.astype(BF16), wo_ref[0:C_CONV, :], preferred_element_type=F32)
         + jnp.dot(on.astype(BF16), wo_ref[C_CONV:, :], preferred_element_type=F32))
    h = _ln(x_ref[0], ig_ref[...], ib_ref[...])
    h1 = _ln(DEEPNORM_ALPHA * h + y, l1g_ref[...], l1b_ref[...])
    h1_ref[0] = h1
    hp_ref[0] = _pack_bf16_pair(h1[:, :HALF_D], h1[:, HALF_D:])

    nt = (((1,), (1,)), ((), ()))
    h_hi = h1.astype(BF16)
    h_lo = (h1 - h_hi.astype(F32)).astype(BF16)
    lt = (lax.dot_general(wrh_ref[...], h_hi, nt, preferred_element_type=F32)
          + lax.dot_general(wrl_ref[...], h_hi, nt, preferred_element_type=F32)
          + lax.dot_general(wrh_ref[...], h_lo, nt, preferred_element_type=F32)) + br_ref[:, 0:1]
    row = lax.broadcasted_iota(jnp.int32, (ROUTE_ROWS, tm), 0)
    big = jnp.int32(1 << 20)
    is_g = (row >= N_EXPERTS) & (row < N_EXPERTS + N_GROUPS)
    gl = jnp.where(is_g, lt, -jnp.inf)
    gmax = jnp.max(gl, axis=0, keepdims=True)
    g_idx = jnp.min(jnp.where(gl == gmax, row - N_EXPERTS, big), axis=0, keepdims=True)
    g_w = 1.0 / jnp.sum(jnp.exp(gl - gmax), axis=0, keepdims=True)
    in_grp = (row < N_EXPERTS) & ((row // EXPERTS_PER_GROUP) == g_idx)
    el = jnp.where(in_grp, lt, -jnp.inf)
    v1 = jnp.max(el, axis=0, keepdims=True)
    i1 = jnp.min(jnp.where(el == v1, row, big), axis=0, keepdims=True)
    el2 = jnp.where(row == i1, -jnp.inf, el)
    v2 = jnp.max(el2, axis=0, keepdims=True)
    i2 = jnp.min(jnp.where(el2 == v2, row, big), axis=0, keepdims=True)
    e2 = jnp.exp(v2 - v1)
    w1 = g_w / (1.0 + e2)
    w2 = g_w * e2 / (1.0 + e2)

    oh1 = row == i1
    oh2 = row == i2
    oh = jnp.where(oh1 | oh2, 1.0, 0.0)
    earlier = (lax.broadcasted_iota(jnp.int32, (tm, tm), 0) < lax.broadcasted_iota(jnp.int32, (tm, tm), 1))
    before = (jnp.dot(oh.astype(BF16), jnp.where(earlier, 1.0, 0.0).astype(BF16), preferred_element_type=F32)
              + carry_sc[:, 0:1])
    r1 = jnp.sum(jnp.where(oh1, before, 0.0), axis=0, keepdims=True).astype(jnp.int32)
    r2 = jnp.sum(jnp.where(oh2, before, 0.0), axis=0, keepdims=True).astype(jnp.int32)
    carry_sc[...] = carry_sc[...] + jnp.sum(oh, axis=1, keepdims=True)
    cnt_ref[...] = carry_sc[...]

    sub = lax.broadcasted_iota(jnp.int32, (SUBLANES, tm), 0)
    ri_ref[...] = jnp.where(sub == 0, i1, jnp.where(sub == 1, i2, jnp.where(sub == 2, r1,
                            jnp.where(sub == 3, r2, 0))))
    lrow = lax.broadcasted_iota(jnp.int32, (LANES, tm), 0)
    rw_ref[0] = jnp.where(lrow == 0, w1, jnp.where(lrow == 1, w2, 0.0)).T[:, :8]


def _mix_call(x, ig, ib, u, pre, cw, cb, clg, clb, cog, o, aog, wo, l1g, l1b, wrh, wrl, br):
    b, seq, _ = x.shape
    tm = MIX_TM
    nt = seq // tm
    full = lambda shape: pl.BlockSpec(shape, lambda i, j: (0,) * len(shape))
    tile = lambda w: pl.BlockSpec((1, tm, w), lambda i, j: (i, j, 0))
    halo = pl.BlockSpec((1, HALO, C_CONV), lambda i, j: (i, jnp.maximum(j * (tm // HALO) - 1, 0), 0))
    return pl.pallas_call(
        _mix_kernel,
        grid=(b, nt),
        in_specs=[tile(D_MODEL), full((1, D_MODEL)), full((1, D_MODEL)),
                  tile(C_CONV), halo, full((HALO, C_CONV)), full((HALO, C_CONV)), full((1, C_CONV)),
                  full((1, C_CONV)), full((1, C_CONV)), full((1, C_CONV)),
                  tile(D_ATTN), full((1, D_ATTN)), full((D_MODEL, D_MODEL)),
                  full((1, D_MODEL)), full((1, D_MODEL)), full((ROUTE_ROWS, D_MODEL)),
                  full((ROUTE_ROWS, D_MODEL)), full((ROUTE_ROWS, LANES))],
        out_specs=[tile(D_MODEL), tile(HALF_D), pl.BlockSpec((8, tm), lambda i, j: (0, i * nt + j)), tile(8),
                   full((ROUTE_ROWS, LANES))],
        out_shape=[jax.ShapeDtypeStruct((b, seq, D_MODEL), F32),
                   jax.ShapeDtypeStruct((b, seq, HALF_D), jnp.uint32),
                   jax.ShapeDtypeStruct((8, b * seq), jnp.int32),
                   jax.ShapeDtypeStruct((b, seq, 8), F32),
                   jax.ShapeDtypeStruct((ROUTE_ROWS, LANES), F32)],
        scratch_shapes=[pltpu.VMEM((HALO + tm, C_CONV), F32),
                        pltpu.VMEM((SUBLANES - 1, HALO + tm, C_CONV), F32),
                        pltpu.VMEM((ROUTE_ROWS, LANES), F32)],
        compiler_params=pltpu.CompilerParams(dimension_semantics=("arbitrary", "arbitrary"),
                                             vmem_limit_bytes=VMEM_LIMIT),
        name="mix",
    )(x, ig, ib, u, u, pre, cw, cb, clg, clb, cog, o, aog, wo, l1g, l1b, wrh, wrl, br)


def _tables_kernel(ps_ref, pe_ref, cn_ref, ri_ref, dest_ref, blk_ref):
    ri = ri_ref[...]
    start = jnp.zeros_like(ri)
    for e in range(N_EXPERTS):
        start = jnp.where(ri == e, ps_ref[e], start)
    dest_ref[...] = jnp.zeros_like(dest_ref)
    dest_ref[0:2, :] = start[0:2, :] + ri_ref[2:4, :]
    blk_start = lax.broadcasted_iota(jnp.int32, blk_ref.shape, 1) * MOE_BM
    be = jnp.zeros(blk_ref.shape, jnp.int32)
    for e in range(N_EXPERTS):
        be = be + jnp.where(pe_ref[e] <= blk_start, 1, 0)
    be = jnp.minimum(be, N_EXPERTS - 1)
    end = jnp.zeros(blk_ref.shape, jnp.int32)
    for e in range(N_EXPERTS):
        end = jnp.where(be == e, ps_ref[e] + cn_ref[e], end)
    valid = jnp.clip(end - blk_start, 0, MOE_BM)
    row = lax.broadcasted_iota(jnp.int32, blk_ref.shape, 0)
    blk_ref[...] = jnp.where(row == 0, be, valid)


def _tables_call(pad_start, pad_end, counts, ri, n_blocks):
    t = ri.shape[1]
    blk_w = -(-n_blocks // LANES) * LANES
    return pl.pallas_call(
        _tables_kernel,
        grid_spec=pltpu.PrefetchScalarGridSpec(
            num_scalar_prefetch=3,
            grid=(1,),
            in_specs=[pl.BlockSpec((SUBLANES, t), lambda i, *_: (0, 0))],
            out_specs=[pl.BlockSpec((SUBLANES, t), lambda i, *_: (0, 0)),
                       pl.BlockSpec((SUBLANES, blk_w), lambda i, *_: (0, 0))]),
        out_shape=[jax.ShapeDtypeStruct((SUBLANES, t), jnp.int32),
                   jax.ShapeDtypeStruct((SUBLANES, blk_w), jnp.int32)],
        compiler_params=pltpu.CompilerParams(dimension_semantics=("arbitrary",),
                                             vmem_limit_bytes=VMEM_LIMIT),
        name="tables",
    )(pad_start, pad_end, counts, ri)


def _expert_kernel(be_ref, nv_ref, x_ref, wg_ref, wu_ref, wd_ref, y_ref, wg_sc, wu_sc, wd_sc):
    i = pl.program_id(0)
    nv = nv_ref[i]

    @pl.when((i == 0) | (be_ref[i] != be_ref[jnp.maximum(i - 1, 0)]))
    def _():
        wg_sc[...] = wg_ref[0, 0].astype(BF16)
        wu_sc[...] = wu_ref[0, 0].astype(BF16)
        wd_sc[...] = wd_ref[0, 0].astype(BF16)

    @pl.when(nv > 0)
    def _():
        rowid = lax.broadcasted_iota(jnp.int32, x_ref.shape, 0)
        x_lo, x_hi = _unpack_bf16_pair(jnp.where(rowid < nv, x_ref[...], jnp.uint32(0)))
        x_lo = x_lo.astype(BF16)
        x_hi = x_hi.astype(BF16)
        g = (jnp.dot(x_lo, wg_sc[:HALF_D, :], preferred_element_type=F32)
             + jnp.dot(x_hi, wg_sc[HALF_D:, :], preferred_element_type=F32))
        up = (jnp.dot(x_lo, wu_sc[:HALF_D, :], preferred_element_type=F32)
              + jnp.dot(x_hi, wu_sc[HALF_D:, :], preferred_element_type=F32))
        act = g * _sigmoid(g) * up
        y = jnp.dot(act.astype(BF16), wd_sc[...], preferred_element_type=F32)
        y_ref[...] = _pack_bf16_pair(y[:, :HALF_D], y[:, HALF_D:])

    @pl.when(nv == 0)
    def _():
        y_ref[...] = jnp.zeros_like(y_ref)


def _expert_call(blk_e, blk_valid, xb, w_gate, w_up, w_down):
    r = xb.shape[0]
    bm = MOE_BM
    wspec = lambda a, b: pl.BlockSpec((1, 1, a, b), lambda i, be, nv: (0, be[i], 0, 0))
    return pl.pallas_call(
        _expert_kernel,
        grid_spec=pltpu.PrefetchScalarGridSpec(
            num_scalar_prefetch=2,
            grid=(r // bm,),
            in_specs=[pl.BlockSpec((bm, HALF_D), lambda i, be, nv: (i, 0)),
                      wspec(D_MODEL, D_EXPERT), wspec(D_MODEL, D_EXPERT), wspec(D_EXPERT, D_MODEL)],
            out_specs=pl.BlockSpec((bm, HALF_D), lambda i, be, nv: (i, 0)),
            scratch_shapes=[pltpu.VMEM((D_MODEL, D_EXPERT), BF16), pltpu.VMEM((D_MODEL, D_EXPERT), BF16),
                            pltpu.VMEM((D_EXPERT, D_MODEL), BF16)]),
        out_shape=jax.ShapeDtypeStruct((r, HALF_D), jnp.uint32),
        compiler_params=pltpu.CompilerParams(dimension_semantics=("arbitrary",),
                                             vmem_limit_bytes=VMEM_LIMIT),
        name="experts",
    )(blk_e, blk_valid, xb, w_gate, w_up, w_down)


def _sc_gather_rows(table, idx):
    m = idx.shape[0]
    width = table.shape[1]
    sc = plsc.get_sparse_core_info()
    workers = sc.num_cores * sc.num_subcores
    per_worker = m // workers
    mesh = plsc.VectorSubcoreMesh(core_axis_name="c", subcore_axis_name="s")

    n_win = per_worker // SC_WINDOW
    assert n_win % 2 == 0 and n_win >= 2
    buf = lambda: [pltpu.VMEM((SC_WINDOW,), jnp.int32), pltpu.VMEM((SC_WINDOW, width), table.dtype),
                   pltpu.SemaphoreType.DMA]

    @functools.partial(pl.kernel, mesh=mesh, out_type=jax.ShapeDtypeStruct((m, width), table.dtype),
                       scratch_types=buf() + buf())
    def gather(table_hbm, idx_hbm, out_hbm, idx_a, rows_a, sem_a, idx_b, rows_b, sem_b):
        base = (lax.axis_index("s") * sc.num_cores + lax.axis_index("c")) * per_worker
        bufs = ((idx_a, rows_a, sem_a), (idx_b, rows_b, sem_b))

        def fetch(win, which):
            idx_v, rows_v, sem = bufs[which]
            pltpu.sync_copy(idx_hbm.at[pl.ds(base + win * SC_WINDOW, SC_WINDOW)], idx_v)
            pltpu.async_copy(table_hbm.at[idx_v], rows_v, sem)

        def flush(win, which):
            idx_v, rows_v, sem = bufs[which]
            pltpu.make_async_copy(table_hbm.at[idx_v], rows_v, sem).wait()
            pltpu.sync_copy(rows_v, out_hbm.at[pl.ds(base + win * SC_WINDOW, SC_WINDOW)])

        fetch(0, 0)

        @pl.loop(0, n_win // 2 - 1)
        def _(pair):
            win = 2 * pair
            fetch(win + 1, 1)
            flush(win, 0)
            fetch(win + 2, 0)
            flush(win + 1, 1)

        fetch(n_win - 1, 1)
        flush(n_win - 2, 0)
        flush(n_win - 1, 1)

    return gather(table, idx)


def _sc_scatter_rows(rows, idx0, idx1, n_out):
    m, width = rows.shape
    sc = plsc.get_sparse_core_info()
    workers = sc.num_cores * sc.num_subcores
    per_worker = m // workers
    mesh = plsc.VectorSubcoreMesh(core_axis_name="c", subcore_axis_name="s")

    n_win = per_worker // SC_WINDOW
    assert n_win % 2 == 0 and n_win >= 2
    buf = lambda: [pltpu.VMEM((SC_WINDOW,), jnp.int32), pltpu.VMEM((SC_WINDOW,), jnp.int32),
                   pltpu.VMEM((SC_WINDOW, width), rows.dtype),
                   pltpu.SemaphoreType.DMA, pltpu.SemaphoreType.DMA, pltpu.SemaphoreType.DMA]

    @functools.partial(pl.kernel, mesh=mesh, out_type=jax.ShapeDtypeStruct((n_out, width), rows.dtype),
                       scratch_types=buf() + buf())
    def scatter(rows_hbm, idx0_hbm, idx1_hbm, out_hbm, *scratch):
        base = (lax.axis_index("s") * sc.num_cores + lax.axis_index("c")) * per_worker
        bufs = (scratch[:6], scratch[6:])

        def load(win, which):
            idx0_v, idx1_v, rows_v, lsem, _, _ = bufs[which]
            off = base + win * SC_WINDOW
            pltpu.sync_copy(idx0_hbm.at[pl.ds(off, SC_WINDOW)], idx0_v)
            pltpu.sync_copy(idx1_hbm.at[pl.ds(off, SC_WINDOW)], idx1_v)
            pltpu.async_copy(rows_hbm.at[pl.ds(off, SC_WINDOW)], rows_v, lsem)

        def send(which):
            idx0_v, idx1_v, rows_v, lsem, sem0, sem1 = bufs[which]
            pltpu.make_async_copy(rows_hbm.at[pl.ds(0, SC_WINDOW)], rows_v, lsem).wait()
            first = pltpu.async_copy(rows_v, out_hbm.at[idx0_v], sem0)
            second = pltpu.async_copy(rows_v, out_hbm.at[idx1_v], sem1)
            first.wait()
            second.wait()

        load(0, 0)

        @pl.loop(0, n_win // 2 - 1)
        def _(pair):
            win = 2 * pair
            load(win + 1, 1)
            send(0)
            load(win + 2, 0)
            send(1)

        load(n_win - 1, 1)
        send(0)
        send(1)

    return scatter(rows, idx0, idx1)


def _combine_rows_kernel(h_ref, w_ref, g_ref, b_ref, y0_ref, y1_ref, *rest):
    o_ref = rest[-1]
    w = w_ref[...]
    y0_lo, y0_hi = _unpack_bf16_pair(y0_ref[...])
    y1_lo, y1_hi = _unpack_bf16_pair(y1_ref[...])
    y = jnp.concatenate([w[:, 0:1] * y0_lo + w[:, 1:2] * y1_lo, w[:, 0:1] * y0_hi + w[:, 1:2] * y1_hi], axis=1)
    o_ref[...] = _ln(DEEPNORM_ALPHA * h_ref[...] + y, g_ref[...], b_ref[...])


def _combine_rows_call(h1, rw, g, b, yg, part, n_parts, prev=None):
    t = h1.shape[0]
    tm = COMB_TM
    n = t // tm // n_parts
    first = part * n
    in_specs = [pl.BlockSpec((tm, D_MODEL), lambda i: (first + i, 0)),
                pl.BlockSpec((tm, 8), lambda i: (first + i, 0)),
                pl.BlockSpec((1, D_MODEL), lambda i: (0, 0)),
                pl.BlockSpec((1, D_MODEL), lambda i: (0, 0)),
                pl.BlockSpec((tm, HALF_D), lambda i: (i, 0)),
                pl.BlockSpec((tm, HALF_D), lambda i: (n + i, 0))]
    args = [h1, rw, g, b, yg, yg]
    aliases = {}
    if prev is not None:
        in_specs.append(pl.BlockSpec(memory_space=pl.ANY))
        args.append(prev)
        aliases = {len(args) - 1: 0}
    return pl.pallas_call(
        _combine_rows_kernel,
        grid=(n,),
        in_specs=in_specs,
        out_specs=pl.BlockSpec((tm, D_MODEL), lambda i: (first + i, 0)),
        out_shape=jax.ShapeDtypeStruct((t, D_MODEL), F32),
        input_output_aliases=aliases,
        compiler_params=pltpu.CompilerParams(dimension_semantics=("arbitrary",),
                                             vmem_limit_bytes=VMEM_LIMIT),
        name="combine_rows",
    )(*args)


def _take_cols(w, idx):
    wz = jnp.concatenate([w, jnp.zeros((w.shape[0], 1), w.dtype)], axis=1)
    return wz[:, np.where(idx < 0, w.shape[1], idx)]


def _layout_indices():
    lane_src = np.full((HEAD_PAD,), -1, np.int64)
    lane_src[0:ROPE_HALF] = QK_NOPE + np.arange(ROPE_HALF)
    lane_src[ROPE_HALF:X2_LANE] = np.arange(X2_LANE - ROPE_HALF)
    lane_src[X2_LANE:X2_LANE + ROPE_HALF] = QK_NOPE + ROPE_HALF + np.arange(ROPE_HALF)
    n_lo = X2_LANE - ROPE_HALF
    lane_src[X2_LANE + ROPE_HALF:X2_LANE + ROPE_HALF + QK_NOPE - n_lo] = n_lo + np.arange(QK_NOPE - n_lo)
    q_idx = np.concatenate([np.where(lane_src >= 0, lane_src + (QK_NOPE + QK_ROPE) * h, -1)
                            for h in range(N_HEADS)])
    nope_src = np.where(lane_src < QK_NOPE, lane_src, -1)
    k_idx = np.concatenate([np.where(nope_src >= 0, nope_src + (QK_NOPE + V_DIM) * h, -1)
                            for h in range(N_HEADS)])
    v_idx = np.concatenate([QK_NOPE + np.arange(V_DIM) + (QK_NOPE + V_DIM) * h for h in range(N_HEADS)])
    kr_idx = np.full((LANES,), -1, np.int64)
    kr_idx[0:ROPE_HALF] = np.arange(ROPE_HALF)
    kr_idx[X2_LANE:X2_LANE + ROPE_HALF] = ROPE_HALF + np.arange(ROPE_HALF)
    return q_idx, k_idx, v_idx, kr_idx


def _rope_lane_tables(length):
    inv_freq = np.float32(ROPE_THETA) ** (-np.arange(ROPE_HALF, dtype=np.float32) / np.float32(ROPE_HALF))
    ang = np.arange(length, dtype=np.float32)[:, None] * inv_freq[None, :].astype(np.float32)
    cos, sin = np.cos(ang).astype(np.float32), np.sin(ang).astype(np.float32)
    ctab = np.ones((length, LANES), np.float32)
    stab = np.zeros((length, LANES), np.float32)
    for lo, sign in ((0, -1.0), (X2_LANE, 1.0)):
        ctab[:, lo:lo + ROPE_HALF] = cos
        stab[:, lo:lo + ROPE_HALF] = sign * sin
    return jnp.asarray(ctab), jnp.asarray(stab)


def kernel(x, meta_tokens, ln_in_g, ln_in_b, w_in, conv_w, conv_b, conv_ln_g, conv_ln_b, q_norm_g, w_uq,
           kv_norm_g, w_ukv, conv_out_g, attn_out_g, w_o, ln1_g, ln1_b, w_rg, b_rg, w_re, b_re,
           w_gate, w_up, w_down, ln2_g, ln2_b):
    assert w_in.shape[0] == 1, "single layer"
    bsz, seq, d = x.shape
    t = bsz * seq
    row = lambda a: a.reshape(1, -1)

    q_idx, k_idx, v_idx, kr_idx = _layout_indices()
    c4 = 2 * C_CONV + Q_LORA + KV_LORA
    w_in_l = jnp.concatenate([w_in[0][:, :c4], _take_cols(w_in[0][:, c4:], kr_idx)], axis=1).astype(BF16)
    wuq_l = _take_cols(w_uq[0], q_idx).astype(BF16)
    wuk_l = _take_cols(w_ukv[0], k_idx).astype(BF16)
    wvt_l = w_ukv[0][:, v_idx].T.astype(BF16)
    wo_l = w_o[0].astype(BF16)
    n_route_pad = ROUTE_ROWS - N_EXPERTS - N_GROUPS
    wr_t = jnp.concatenate([w_re[0].T, w_rg[0].T, jnp.zeros((n_route_pad, d), F32)], axis=0)
    wrh_l = wr_t.astype(BF16)
    wrl_l = (wr_t - wrh_l.astype(F32)).astype(BF16)
    br_l = jnp.broadcast_to(jnp.concatenate([b_re[0], b_rg[0], jnp.zeros((n_route_pad,), F32)])[:, None],
                            (ROUTE_ROWS, LANES))
    ctab, stab = _rope_lane_tables(N_META + seq)

    proj_w = (row(ln_in_g), row(ln_in_b), w_in_l, row(q_norm_g[0]), wuq_l, row(kv_norm_g[0]), wuk_l, wvt_l)
    u_m, _, k_m, vt_m = _proj_call(meta_tokens.astype(F32), N_META, 1, *proj_w, ctab[:N_META], stab[:N_META])
    u, q, k, vt = _proj_call(x.reshape(t, d), PROJ_TM, seq // PROJ_TM, *proj_w, ctab[N_META:], stab[N_META:])

    o = _attn_call(q.reshape(bsz, seq, -1), k.reshape(bsz, seq, -1),
                   vt.reshape(bsz, seq // ATT_T, D_ATTN, ATT_T), k_m, vt_m)

    pre = jnp.concatenate([jnp.zeros((HALO - N_META, C_CONV), F32), u_m], axis=0)
    cw = jnp.concatenate([conv_w[0], jnp.zeros((HALO - CONV_K, C_CONV), F32)], axis=0)
    h1, hp, ri, rw, cnt = _mix_call(x, row(ln_in_g), row(ln_in_b), u.reshape(bsz, seq, C_CONV), pre, cw,
                                    row(conv_b[0]), row(conv_ln_g[0]), row(conv_ln_b[0]), row(conv_out_g[0]),
                                    o, row(attn_out_g[0]), wo_l, row(ln1_g[0]), row(ln1_b[0]), wrh_l, wrl_l, br_l)

    counts = cnt[:N_EXPERTS, 0].astype(jnp.int32)
    padded = (counts + MOE_BM - 1) // MOE_BM * MOE_BM
    pad_end = jnp.cumsum(padded)
    pad_start = pad_end - padded
    n_blocks = (2 * t) // MOE_BM + N_EXPERTS
    dest, blk = _tables_call(pad_start, pad_end, counts, ri, n_blocks)
    blk_e, blk_valid = blk[0, :n_blocks], blk[1, :n_blocks]

    h1f = h1.reshape(t, d)
    dest0, dest1 = dest[0], dest[1]
    xb = _sc_scatter_rows(hp.reshape(t, HALF_D), dest0, dest1, n_blocks * MOE_BM)
    yb = _expert_call(blk_e, blk_valid, xb, w_gate, w_up, w_down)
    tp = t // COMBINE_PARTS
    out = None
    for part in range(COMBINE_PARTS):
        sl = slice(part * tp, (part + 1) * tp)
        yg = _sc_gather_rows(yb, jnp.concatenate([dest0[sl], dest1[sl]]))
        out = _combine_rows_call(h1f, rw.reshape(t, 8), row(ln2_g[0]), row(ln2_b[0]), yg, part, COMBINE_PARTS,
                                 prev=out)
    return out.reshape(bsz, seq, d)
```

```python
import functools
import math

import numpy as np
import jax
import jax.numpy as jnp
from jax import lax
from jax.experimental import pallas as pl
from jax.experimental.pallas import tpu as pltpu
from jax.experimental.pallas import tpu_sc as plsc

F32 = jnp.float32
BF16 = jnp.bfloat16

D_MODEL = 1024
HALF_D = D_MODEL // 2
N_META = 16
CHUNK = 64
C_CONV = 512
CONV_K = 31
N_HEADS = 8
QK_NOPE = 64
QK_ROPE = 32
V_DIM = 64
Q_LORA = 384
KV_LORA = 256
ROPE_THETA = 10000.0
D_ATTN = N_HEADS * V_DIM
N_GROUPS = 4
EXPERTS_PER_GROUP = 8
N_EXPERTS = N_GROUPS * EXPERTS_PER_GROUP
D_EXPERT = 256
DEEPNORM_ALPHA = 2.0 ** 0.25
EPS = 1e-5
NEG_INF = -1e30

LANES = 128
SUBLANES = 8
HEAD_PAD = LANES
ROPE_HALF = QK_ROPE // 2
X2_LANE = 64
Z_COLS = 2 * C_CONV + Q_LORA + KV_LORA + LANES
Q_SCALE = (QK_NOPE + QK_ROPE) ** -0.5 * math.log2(math.e)

PROJ_TM = 1024
ATT_T = 256
ATT_HEADS = 8
MIX_TM = 512
ROUTE_ROWS = 48
HALO = 32
MOE_BM = 512
COMB_TM = 1024
SC_WINDOW = 64
VMEM_LIMIT = 56 * 1024 * 1024


def _ln(x, g, b):
    mu = jnp.mean(x, axis=-1, keepdims=True)
    xc = x - mu
    var = jnp.mean(xc * xc, axis=-1, keepdims=True)
    return xc * lax.rsqrt(var + EPS) * g + b


def _rms(x, g):
    return x * lax.rsqrt(jnp.mean(x * x, axis=-1, keepdims=True) + EPS) * g


def _sigmoid(x):
    return 1.0 / (1.0 + jnp.exp(-x))


def _pack_bf16_pair(lo, hi):
    lo_bits = lax.bitcast_convert_type(lo.astype(BF16).astype(F32), jnp.uint32)
    hi_bits = lax.bitcast_convert_type(hi.astype(BF16).astype(F32), jnp.uint32)
    return (lo_bits >> 16) | (hi_bits & jnp.uint32(0xFFFF0000))


def _unpack_bf16_pair(packed):
    lo = lax.bitcast_convert_type(packed << 16, F32)
    hi = lax.bitcast_convert_type(packed & jnp.uint32(0xFFFF0000), F32)
    return lo, hi


def _proj_kernel(x_ref, g_ref, b_ref, win_ref, qg_ref, wuq_ref, kvg_ref, wuk_ref, wvt_ref, c_ref, s_ref,
                 u_ref, q_ref, k_ref, vt_ref):
    n_part, part = vt_ref.shape[1], vt_ref.shape[3]
    zs = []
    for r in range(n_part):
        h = _ln(x_ref[r * part:(r + 1) * part, :], g_ref[...], b_ref[...])
        zs.append(jnp.dot(h.astype(BF16), win_ref[...], preferred_element_type=F32))
    c0 = 2 * C_CONV
    for r, z in enumerate(zs):
        rows = slice(r * part, (r + 1) * part)
        u_ref[rows, :] = z[:, :C_CONV] * _sigmoid(z[:, C_CONV:c0])
        cq = z[:, c0:c0 + Q_LORA]
        ckv = z[:, c0 + Q_LORA:c0 + Q_LORA + KV_LORA]
        kr = z[:, c0 + Q_LORA + KV_LORA:]
        cs = c_ref[rows, :]
        sn = s_ref[rows, :]
        q = jnp.dot(_rms(cq, qg_ref[...]).astype(BF16), wuq_ref[...], preferred_element_type=F32)
        kvn = _rms(ckv, kvg_ref[...]).astype(BF16)
        kn = jnp.dot(kvn, wuk_ref[...], preferred_element_type=F32)
        kr_rot = kr * cs + pltpu.roll(kr, X2_LANE, 1) * sn
        for hh in range(N_HEADS):
            sl = slice(HEAD_PAD * hh, HEAD_PAD * (hh + 1))
            qh = q[:, sl]
            qh = (qh * cs + pltpu.roll(qh, X2_LANE, 1) * sn) * Q_SCALE
            q_ref[rows, sl] = qh.astype(BF16)
            k_ref[rows, sl] = (kn[:, sl] + kr_rot).astype(BF16)
        vt = lax.dot_general(wvt_ref[...], kvn, (((1,), (1,)), ((), ())), preferred_element_type=F32)
        vt_ref[0, r] = vt.astype(BF16)


def _proj_call(x2d, tm, pos_blocks, ln_g, ln_b, w_in, qg, wuq, kvg, wuk, wvt, ctab, stab):
    t = x2d.shape[0]
    sub_w = min(tm, ATT_T)
    n_sub = tm // sub_w
    full = lambda shape: pl.BlockSpec(shape, lambda i: (0,) * len(shape))
    row = lambda w: pl.BlockSpec((tm, w), lambda i: (i, 0))
    tab = pl.BlockSpec((tm, LANES), lambda i: (i % pos_blocks, 0))
    return pl.pallas_call(
        _proj_kernel,
        grid=(t // tm,),
        in_specs=[row(D_MODEL), full((1, D_MODEL)), full((1, D_MODEL)), full((D_MODEL, Z_COLS)),
                  full((1, Q_LORA)), full((Q_LORA, N_HEADS * HEAD_PAD)),
                  full((1, KV_LORA)), full((KV_LORA, N_HEADS * HEAD_PAD)), full((D_ATTN, KV_LORA)), tab, tab],
        out_specs=[row(C_CONV), row(N_HEADS * HEAD_PAD), row(N_HEADS * HEAD_PAD),
                   pl.BlockSpec((1, n_sub, D_ATTN, sub_w), lambda i: (i, 0, 0, 0))],
        out_shape=[jax.ShapeDtypeStruct((t, C_CONV), F32),
                   jax.ShapeDtypeStruct((t, N_HEADS * HEAD_PAD), BF16),
                   jax.ShapeDtypeStruct((t, N_HEADS * HEAD_PAD), BF16),
                   jax.ShapeDtypeStruct((t // tm, n_sub, D_ATTN, sub_w), BF16)],
        compiler_params=pltpu.CompilerParams(dimension_semantics=("arbitrary",),
                                             vmem_limit_bytes=VMEM_LIMIT),
        name="proj",
    )(x2d, ln_g, ln_b, w_in, qg, wuq, kvg, wuk, wvt, ctab, stab)


def _attn_kernel(q_ref, k_ref, vt_ref, km_ref, vtm_ref, o_ref, s_sc):
    seq = q_ref.shape[1]
    nq = seq // ATT_T
    key_chunk = lax.broadcasted_iota(jnp.int32, (ATT_T, ATT_T), 0) // CHUNK
    qry_chunk = lax.broadcasted_iota(jnp.int32, (ATT_T, ATT_T), 1) // CHUNK
    diag_visible = key_chunk <= qry_chunk
    nt = (((1,), (1,)), ((), ()))
    heads = range(ATT_HEADS)
    hsl = [slice(HEAD_PAD * hh, HEAD_PAD * (hh + 1)) for hh in heads]
    vsl = [slice(V_DIM * hh, V_DIM * (hh + 1)) for hh in heads]

    def update(s, vt, m, l, acc):
        m_new = jnp.maximum(m, jnp.max(s, axis=0, keepdims=True))
        alpha = jnp.exp2(m - m_new)
        p = jnp.exp2(s - m_new)
        l = alpha * l + jnp.sum(p, axis=0, keepdims=True)
        acc = alpha * acc + jnp.dot(vt, p.astype(BF16), preferred_element_type=F32)
        return m_new, l, acc

    def scores(qi, j, hh):
        q0 = pl.multiple_of(qi * ATT_T, ATT_T)
        k0 = pl.multiple_of(j * ATT_T, ATT_T)
        return lax.dot_general(k_ref[0, pl.ds(k0, ATT_T), hsl[hh]], q_ref[0, pl.ds(q0, ATT_T), hsl[hh]], nt,
                               preferred_element_type=F32)

    for hh in heads:
        s_sc[0, hh] = scores(0, 0, hh)

    def q_body(qi, carry):
        row0 = pl.multiple_of(qi * ATT_T, ATT_T)
        slot = qi % 2
        state = []
        for hh in heads:
            state += [jnp.full((1, ATT_T), NEG_INF, F32), jnp.zeros((1, ATT_T), F32),
                      jnp.zeros((V_DIM, ATT_T), F32)]

        def kv_body(j, st):
            out = []
            s_next = scores(qi, j + 1, 0)
            for hh in heads:
                s_cur = s_sc[slot, hh]
                s_after = scores(qi, j + 1, hh + 1) if hh + 1 < ATT_HEADS else None
                out += update(s_cur, vt_ref[0, j, vsl[hh], :], *st[3 * hh:3 * hh + 3])
                s_sc[slot, hh] = s_next
                s_next = s_after
            return tuple(out)

        state = lax.fori_loop(0, qi, kv_body, tuple(state))
        q_next = jnp.minimum(qi + 1, nq - 1)
        sms = [lax.dot_general(km_ref[:, hsl[hh]], q_ref[0, pl.ds(row0, ATT_T), hsl[hh]], nt,
                               preferred_element_type=F32) for hh in heads]
        outs = []
        for hh in heads:
            m, l, acc = state[3 * hh:3 * hh + 3]
            s = jnp.where(diag_visible, s_sc[slot, hh], NEG_INF)
            sm = sms[hh]
            m_new = jnp.maximum(m, jnp.maximum(jnp.max(s, axis=0, keepdims=True),
                                               jnp.max(sm, axis=0, keepdims=True)))
            alpha = jnp.exp2(m - m_new)
            p = jnp.exp2(s - m_new)
            pm = jnp.exp2(sm - m_new)
            l = alpha * l + jnp.sum(p, axis=0, keepdims=True) + jnp.sum(pm, axis=0, keepdims=True)
            acc = (alpha * acc + jnp.dot(vt_ref[0, qi, vsl[hh], :], p.astype(BF16), preferred_element_type=F32)
                   + jnp.dot(vtm_ref[0, 0, vsl[hh], :], pm.astype(BF16), preferred_element_type=F32))
            outs.append(acc / l)
        for hh in heads:
            s_sc[1 - slot, hh] = scores(q_next, 0, hh)
        o_ref[0, pl.ds(row0, ATT_T), :] = jnp.concatenate(outs, axis=0).T
        return carry

    lax.fori_loop(0, nq, q_body, 0)


def _attn_call(q, k, vt, k_meta, vt_meta):
    b, seq, _ = q.shape
    nk = seq // ATT_T
    hb = ATT_HEADS
    return pl.pallas_call(
        _attn_kernel,
        grid=(b, N_HEADS // hb),
        in_specs=[pl.BlockSpec((1, seq, hb * HEAD_PAD), lambda i, j: (i, 0, j)),
                  pl.BlockSpec((1, seq, hb * HEAD_PAD), lambda i, j: (i, 0, j)),
                  pl.BlockSpec((1, nk, hb * V_DIM, ATT_T), lambda i, j: (i, 0, j, 0)),
                  pl.BlockSpec((N_META, hb * HEAD_PAD), lambda i, j: (0, j)),
                  pl.BlockSpec((1, 1, hb * V_DIM, N_META), lambda i, j: (0, 0, j, 0))],
        out_specs=pl.BlockSpec((1, seq, hb * V_DIM), lambda i, j: (i, 0, j)),
        out_shape=jax.ShapeDtypeStruct((b, seq, D_ATTN), F32),
        scratch_shapes=[pltpu.VMEM((2, hb, ATT_T, ATT_T), F32)],
        compiler_params=pltpu.CompilerParams(dimension_semantics=("arbitrary", "arbitrary"),
                                             vmem_limit_bytes=VMEM_LIMIT),
        name="attn",
    )(q, k, vt, k_meta, vt_meta)


def _mix_kernel(x_ref, ig_ref, ib_ref, u_ref, uh_ref, pre_ref, cw_ref, cb_ref, clg_ref, clb_ref,
                cog_ref, o_ref, aog_ref, wo_ref, l1g_ref, l1b_ref, wrh_ref, wrl_ref, br_ref,
                h1_ref, hp_ref, ri_ref, rw_ref, cnt_ref, win_sc, sh_sc, carry_sc):
    bi = pl.program_id(0)
    ti = pl.program_id(1)
    tm = MIX_TM

    @pl.when((bi == 0) & (ti == 0))
    def _():
        carry_sc[...] = jnp.zeros_like(carry_sc)

    win_sc[0:HALO, :] = jnp.where(ti == 0, pre_ref[...], uh_ref[0])
    win_sc[HALO:, :] = u_ref[0]
    first = HALO - (CONV_K - 1)
    acc = jnp.broadcast_to(cb_ref[...], (tm, C_CONV))
    for res in range(SUBLANES):
        offs = [o for o in range(first, first + CONV_K) if o % SUBLANES == res]
        rows = tm + offs[-1] - res
        if res:
            sh_sc[res - 1, 0:rows, :] = win_sc[pl.ds(res, rows), :]
        for o in offs:
            src = sh_sc[res - 1, o - res:o - res + tm, :] if res else win_sc[o:o + tm, :]
            acc = acc + src * cw_ref[o - first:o - first + 1, :]
    c = _ln(acc, clg_ref[...], clb_ref[...])
    c = c * _sigmoid(c)
    cn = _rms(c, cog_ref[...])
    on = _rms(o_ref[0], aog_ref[...])
    y = (jnp.dot(cn.astype(BF16), wo_ref[0:C_CONV, :], preferred_element_type=F32)
         + jnp.dot(on.astype(BF16), wo_ref[C_CONV:, :], preferred_element_type=F32))
    h = _ln(x_ref[0], ig_ref[...], ib_ref[...])
    h1 = _ln(DEEPNORM_ALPHA * h + y, l1g_ref[...], l1b_ref[...])
    h1_ref[0] = h1
    hp_ref[0] = _pack_bf16_pair(h1[:, :HALF_D], h1[:, HALF_D:])

    nt = (((1,), (1,)), ((), ()))
    h_hi = h1.astype(BF16)
    h_lo = (h1 - h_hi.astype(F32)).astype(BF16)
    lt = (lax.dot_general(wrh_ref[...], h_hi, nt, preferred_element_type=F32)
          + lax.dot_general(wrl_ref[...], h_hi, nt, preferred_element_type=F32)
          + lax.dot_general(wrh_ref[...], h_lo, nt, preferred_element_type=F32)) + br_ref[:, 0:1]
    row = lax.broadcasted_iota(jnp.int32, (ROUTE_ROWS, tm), 0)
    big = jnp.int32(1 << 20)
    is_g = (row >= N_EXPERTS) & (row < N_EXPERTS + N_GROUPS)
    gl = jnp.where(is_g, lt, -jnp.inf)
    gmax = jnp.max(gl, axis=0, keepdims=True)
    g_idx = jnp.min(jnp.where(gl == gmax, row - N_EXPERTS, big), axis=0, keepdims=True)
    g_w = 1.0 / jnp.sum(jnp.exp(gl - gmax), axis=0, keepdims=True)
    in_grp = (row < N_EXPERTS) & ((row // EXPERTS_PER_GROUP) == g_idx)
    el = jnp.where(in_grp, lt, -jnp.inf)
    v1 = jnp.max(el, axis=0, keepdims=True)
    i1 = jnp.min(jnp.where(el == v1, row, big), axis=0, keepdims=True)
    el2 = jnp.where(row == i1, -jnp.inf, el)
    v2 = jnp.max(el2, axis=0, keepdims=True)
    i2 = jnp.min(jnp.where(el2 == v2, row, big), axis=0, keepdims=True)
    e2 = jnp.exp(v2 - v1)
    w1 = g_w / (1.0 + e2)
    w2 = g_w * e2 / (1.0 + e2)

    oh1 = row == i1
    oh2 = row == i2
    oh = jnp.where(oh1 | oh2, 1.0, 0.0)
    earlier = (lax.broadcasted_iota(jnp.int32, (tm, tm), 0) < lax.broadcasted_iota(jnp.int32, (tm, tm), 1))
    before = (jnp.dot(oh.astype(BF16), jnp.where(earlier, 1.0, 0.0).astype(BF16), preferred_element_type=F32)
              + carry_sc[:, 0:1])
    r1 = jnp.sum(jnp.where(oh1, before, 0.0), axis=0, keepdims=True).astype(jnp.int32)
    r2 = jnp.sum(jnp.where(oh2, before, 0.0), axis=0, keepdims=True).astype(jnp.int32)
    carry_sc[...] = carry_sc[...] + jnp.sum(oh, axis=1, keepdims=True)
    cnt_ref[...] = carry_sc[...]

    sub = lax.broadcasted_iota(jnp.int32, (SUBLANES, tm), 0)
    ri_ref[...] = jnp.where(sub == 0, i1, jnp.where(sub == 1, i2, jnp.where(sub == 2, r1,
                            jnp.where(sub == 3, r2, 0))))
    lrow = lax.broadcasted_iota(jnp.int32, (LANES, tm), 0)
    rw_ref[0] = jnp.where(lrow == 0, w1, jnp.where(lrow == 1, w2, 0.0)).T[:, :8]


def _mix_call(x, ig, ib, u, pre, cw, cb, clg, clb, cog, o, aog, wo, l1g, l1b, wrh, wrl, br):
    b, seq, _ = x.shape
    tm = MIX_TM
    nt = seq // tm
    full = lambda shape: pl.BlockSpec(shape, lambda i, j: (0,) * len(shape))
    tile = lambda w: pl.BlockSpec((1, tm, w), lambda i, j: (i, j, 0))
    halo = pl.BlockSpec((1, HALO, C_CONV), lambda i, j: (i, jnp.maximum(j * (tm // HALO) - 1, 0), 0))
    return pl.pallas_call(
        _mix_kernel,
        grid=(b, nt),
        in_specs=[tile(D_MODEL), full((1, D_MODEL)), full((1, D_MODEL)),
                  tile(C_CONV), halo, full((HALO, C_CONV)), full((HALO, C_CONV)), full((1, C_CONV)),
                  full((1, C_CONV)), full((1, C_CONV)), full((1, C_CONV)),
                  tile(D_ATTN), full((1, D_ATTN)), full((D_MODEL, D_MODEL)),
                  full((1, D_MODEL)), full((1, D_MODEL)), full((ROUTE_ROWS, D_MODEL)),
                  full((ROUTE_ROWS, D_MODEL)), full((ROUTE_ROWS, LANES))],
        out_specs=[tile(D_MODEL), tile(HALF_D), pl.BlockSpec((8, tm), lambda i, j: (0, i * nt + j)), tile(8),
                   full((ROUTE_ROWS, LANES))],
        out_shape=[jax.ShapeDtypeStruct((b, seq, D_MODEL), F32),
                   jax.ShapeDtypeStruct((b, seq, HALF_D), jnp.uint32),
                   jax.ShapeDtypeStruct((8, b * seq), jnp.int32),
                   jax.ShapeDtypeStruct((b, seq, 8), F32),
                   jax.ShapeDtypeStruct((ROUTE_ROWS, LANES), F32)],
        scratch_shapes=[pltpu.VMEM((HALO + tm, C_CONV), F32),
                        pltpu.VMEM((SUBLANES - 1, HALO + tm, C_CONV), F32),
                        pltpu.VMEM((ROUTE_ROWS, LANES), F32)],
        compiler_params=pltpu.CompilerParams(dimension_semantics=("arbitrary", "arbitrary"),
                                             vmem_limit_bytes=VMEM_LIMIT),
        name="mix",
    )(x, ig, ib, u, u, pre, cw, cb, clg, clb, cog, o, aog, wo, l1g, l1b, wrh, wrl, br)


def _tables_kernel(ps_ref, pe_ref, cn_ref, ri_ref, dest_ref, blk_ref):
    ri = ri_ref[...]
    start = jnp.zeros_like(ri)
    for e in range(N_EXPERTS):
        start = jnp.where(ri == e, ps_ref[e], start)
    dest_ref[...] = jnp.zeros_like(dest_ref)
    dest_ref[0:2, :] = start[0:2, :] + ri_ref[2:4, :]
    blk_start = lax.broadcasted_iota(jnp.int32, blk_ref.shape, 1) * MOE_BM
    be = jnp.zeros(blk_ref.shape, jnp.int32)
    for e in range(N_EXPERTS):
        be = be + jnp.where(pe_ref[e] <= blk_start, 1, 0)
    be = jnp.minimum(be, N_EXPERTS - 1)
    end = jnp.zeros(blk_ref.shape, jnp.int32)
    for e in range(N_EXPERTS):
        end = jnp.where(be == e, ps_ref[e] + cn_ref[e], end)
    valid = jnp.clip(end - blk_start, 0, MOE_BM)
    row = lax.broadcasted_iota(jnp.int32, blk_ref.shape, 0)
    blk_ref[...] = jnp.where(row == 0, be, valid)


def _tables_call(pad_start, pad_end, counts, ri, n_blocks):
    t = ri.shape[1]
    blk_w = -(-n_blocks // LANES) * LANES
    return pl.pallas_call(
        _tables_kernel,
        grid_spec=pltpu.PrefetchScalarGridSpec(
            num_scalar_prefetch=3,
            grid=(1,),
            in_specs=[pl.BlockSpec((SUBLANES, t), lambda i, *_: (0, 0))],
            out_specs=[pl.BlockSpec((SUBLANES, t), lambda i, *_: (0, 0)),
                       pl.BlockSpec((SUBLANES, blk_w), lambda i, *_: (0, 0))]),
        out_shape=[jax.ShapeDtypeStruct((SUBLANES, t), jnp.int32),
                   jax.ShapeDtypeStruct((SUBLANES, blk_w), jnp.int32)],
        compiler_params=pltpu.CompilerParams(dimension_semantics=("arbitrary",),
                                             vmem_limit_bytes=VMEM_LIMIT),
        name="tables",
    )(pad_start, pad_end, counts, ri)


def _expert_kernel(be_ref, nv_ref, x_ref, wg_ref, wu_ref, wd_ref, y_ref, wg_sc, wu_sc, wd_sc):
    i = pl.program_id(0)
    nv = nv_ref[i]

    @pl.when((i == 0) | (be_ref[i] != be_ref[jnp.maximum(i - 1, 0)]))
    def _():
        wg_sc[...] = wg_ref[0, 0].astype(BF16)
        wu_sc[...] = wu_ref[0, 0].astype(BF16)
        wd_sc[...] = wd_ref[0, 0].astype(BF16)

    @pl.when(nv > 0)
    def _():
        rowid = lax.broadcasted_iota(jnp.int32, x_ref.shape, 0)
        x_lo, x_hi = _unpack_bf16_pair(jnp.where(rowid < nv, x_ref[...], jnp.uint32(0)))
        x_lo = x_lo.astype(BF16)
        x_hi = x_hi.astype(BF16)
        g = (jnp.dot(x_lo, wg_sc[:HALF_D, :], preferred_element_type=F32)
             + jnp.dot(x_hi, wg_sc[HALF_D:, :], preferred_element_type=F32))
        up = (jnp.dot(x_lo, wu_sc[:HALF_D, :], preferred_element_type=F32)
              + jnp.dot(x_hi, wu_sc[HALF_D:, :], preferred_element_type=F32))
        act = g * _sigmoid(g) * up
        y = jnp.dot(act.astype(BF16), wd_sc[...], preferred_element_type=F32)
        y_ref[...] = _pack_bf16_pair(y[:, :HALF_D], y[:, HALF_D:])

    @pl.when(nv == 0)
    def _():
        y_ref[...] = jnp.zeros_like(y_ref)


def _expert_call(blk_e, blk_valid, xb, w_gate, w_up, w_down):
    r = xb.shape[0]
    bm = MOE_BM
    wspec = lambda a, b: pl.BlockSpec((1, 1, a, b), lambda i, be, nv: (0, be[i], 0, 0))
    return pl.pallas_call(
        _expert_kernel,
        grid_spec=pltpu.PrefetchScalarGridSpec(
            num_scalar_prefetch=2,
            grid=(r // bm,),
            in_specs=[pl.BlockSpec((bm, HALF_D), lambda i, be, nv: (i, 0)),
                      wspec(D_MODEL, D_EXPERT), wspec(D_MODEL, D_EXPERT), wspec(D_EXPERT, D_MODEL)],
            out_specs=pl.BlockSpec((bm, HALF_D), lambda i, be, nv: (i, 0)),
            scratch_shapes=[pltpu.VMEM((D_MODEL, D_EXPERT), BF16), pltpu.VMEM((D_MODEL, D_EXPERT), BF16),
                            pltpu.VMEM((D_EXPERT, D_MODEL), BF16)]),
        out_shape=jax.ShapeDtypeStruct((r, HALF_D), jnp.uint32),
        compiler_params=pltpu.CompilerParams(dimension_semantics=("arbitrary",),
                                             vmem_limit_bytes=VMEM_LIMIT),
        name="experts",
    )(blk_e, blk_valid, xb, w_gate, w_up, w_down)


def _sc_gather_rows(table, idx):
    m = idx.shape[0]
    width = table.shape[1]
    sc = plsc.get_sparse_core_info()
    workers = sc.num_cores * sc.num_subcores
    per_worker = m // workers
    mesh = plsc.VectorSubcoreMesh(core_axis_name="c", subcore_axis_name="s")

    n_win = per_worker // SC_WINDOW
    assert n_win % 2 == 0 and n_win >= 2
    buf = lambda: [pltpu.VMEM((SC_WINDOW,), jnp.int32), pltpu.VMEM((SC_WINDOW, width), table.dtype),
                   pltpu.SemaphoreType.DMA]

    @functools.partial(pl.kernel, mesh=mesh, out_type=jax.ShapeDtypeStruct((m, width), table.dtype),
                       scratch_types=buf() + buf())
    def gather(table_hbm, idx_hbm, out_hbm, idx_a, rows_a, sem_a, idx_b, rows_b, sem_b):
        base = (lax.axis_index("s") * sc.num_cores + lax.axis_index("c")) * per_worker
        bufs = ((idx_a, rows_a, sem_a), (idx_b, rows_b, sem_b))

        def fetch(win, which):
            idx_v, rows_v, sem = bufs[which]
            pltpu.sync_copy(idx_hbm.at[pl.ds(base + win * SC_WINDOW, SC_WINDOW)], idx_v)
            pltpu.async_copy(table_hbm.at[idx_v], rows_v, sem)

        def flush(win, which):
            idx_v, rows_v, sem = bufs[which]
            pltpu.make_async_copy(table_hbm.at[idx_v], rows_v, sem).wait()
            pltpu.sync_copy(rows_v, out_hbm.at[pl.ds(base + win * SC_WINDOW, SC_WINDOW)])

        fetch(0, 0)

        @pl.loop(0, n_win // 2 - 1)
        def _(pair):
            win = 2 * pair
            fetch(win + 1, 1)
            flush(win, 0)
            fetch(win + 2, 0)
            flush(win + 1, 1)

        fetch(n_win - 1, 1)
        flush(n_win - 2, 0)
        flush(n_win - 1, 1)

    return gather(table, idx)


def _sc_scatter_rows(rows, idx0, idx1, n_out):
    m, width = rows.shape
    sc = plsc.get_sparse_core_info()
    workers = sc.num_cores * sc.num_subcores
    per_worker = m // workers
    mesh = plsc.VectorSubcoreMesh(core_axis_name="c", subcore_axis_name="s")

    n_win = per_worker // SC_WINDOW
    assert n_win % 2 == 0 and n_win >= 2
    buf = lambda: [pltpu.VMEM((SC_WINDOW,), jnp.int32), pltpu.VMEM((SC_WINDOW,), jnp.int32),
                   pltpu.VMEM((SC_WINDOW, width), rows.dtype),
                   pltpu.SemaphoreType.DMA, pltpu.SemaphoreType.DMA, pltpu.SemaphoreType.DMA]

    @functools.partial(pl.kernel, mesh=mesh, out_type=jax.ShapeDtypeStruct((n_out, width), rows.dtype),
                       scratch_types=buf() + buf())
    def scatter(rows_hbm, idx0_hbm, idx1_hbm, out_hbm, *scratch):
        base = (lax.axis_index("s") * sc.num_cores + lax.axis_index("c")) * per_worker
        bufs = (scratch[:6], scratch[6:])

        def load(win, which):
            idx0_v, idx1_v, rows_v, lsem, _, _ = bufs[which]
            off = base + win * SC_WINDOW
            pltpu.sync_copy(idx0_hbm.at[pl.ds(off, SC_WINDOW)], idx0_v)
            pltpu.sync_copy(idx1_hbm.at[pl.ds(off, SC_WINDOW)], idx1_v)
            pltpu.async_copy(rows_hbm.at[pl.ds(off, SC_WINDOW)], rows_v, lsem)

        def send(which):
            idx0_v, idx1_v, rows_v, lsem, sem0, sem1 = bufs[which]
            pltpu.make_async_copy(rows_hbm.at[pl.ds(0, SC_WINDOW)], rows_v, lsem).wait()
            first = pltpu.async_copy(rows_v, out_hbm.at[idx0_v], sem0)
            second = pltpu.async_copy(rows_v, out_hbm.at[idx1_v], sem1)
            first.wait()
            second.wait()

        load(0, 0)

        @pl.loop(0, n_win // 2 - 1)
        def _(pair):
            win = 2 * pair
            load(win + 1, 1)
            send(0)
            load(win + 2, 0)
            send(1)

        load(n_win - 1, 1)
        send(0)
        send(1)

    return scatter(rows, idx0, idx1)


def _combine_rows_kernel(h_ref, w_ref, g_ref, b_ref, y0_ref, y1_ref, o_ref):
    w = w_ref[...]
    y0_lo, y0_hi = _unpack_bf16_pair(y0_ref[...])
    y1_lo, y1_hi = _unpack_bf16_pair(y1_ref[...])
    y = jnp.concatenate([w[:, 0:1] * y0_lo + w[:, 1:2] * y1_lo, w[:, 0:1] * y0_hi + w[:, 1:2] * y1_hi], axis=1)
    o_ref[...] = _ln(DEEPNORM_ALPHA * h_ref[...] + y, g_ref[...], b_ref[...])


def _combine_rows_call(h1, rw, g, b, yg):
    t = h1.shape[0]
    tm = COMB_TM
    n = t // tm
    return pl.pallas_call(
        _combine_rows_kernel,
        grid=(n,),
        in_specs=[pl.BlockSpec((tm, D_MODEL), lambda i: (i, 0)),
                  pl.BlockSpec((tm, 8), lambda i: (i, 0)),
                  pl.BlockSpec((1, D_MODEL), lambda i: (0, 0)),
                  pl.BlockSpec((1, D_MODEL), lambda i: (0, 0)),
                  pl.BlockSpec((tm, HALF_D), lambda i: (i, 0)),
                  pl.BlockSpec((tm, HALF_D), lambda i: (n + i, 0))],
        out_specs=pl.BlockSpec((tm, D_MODEL), lambda i: (i, 0)),
        out_shape=jax.ShapeDtypeStruct((t, D_MODEL), F32),
        compiler_params=pltpu.CompilerParams(dimension_semantics=("arbitrary",),
                                             vmem_limit_bytes=VMEM_LIMIT),
        name="combine_rows",
    )(h1, rw, g, b, yg, yg)


def _take_cols(w, idx):
    wz = jnp.concatenate([w, jnp.zeros((w.shape[0], 1), w.dtype)], axis=1)
    return wz[:, np.where(idx < 0, w.shape[1], idx)]


def _layout_indices():
    lane_src = np.full((HEAD_PAD,), -1, np.int64)
    lane_src[0:ROPE_HALF] = QK_NOPE + np.arange(ROPE_HALF)
    lane_src[ROPE_HALF:X2_LANE] = np.arange(X2_LANE - ROPE_HALF)
    lane_src[X2_LANE:X2_LANE + ROPE_HALF] = QK_NOPE + ROPE_HALF + np.arange(ROPE_HALF)
    n_lo = X2_LANE - ROPE_HALF
    lane_src[X2_LANE + ROPE_HALF:X2_LANE + ROPE_HALF + QK_NOPE - n_lo] = n_lo + np.arange(QK_NOPE - n_lo)
    q_idx = np.concatenate([np.where(lane_src >= 0, lane_src + (QK_NOPE + QK_ROPE) * h, -1)
                            for h in range(N_HEADS)])
    nope_src = np.where(lane_src < QK_NOPE, lane_src, -1)
    k_idx = np.concatenate([np.where(nope_src >= 0, nope_src + (QK_NOPE + V_DIM) * h, -1)
                            for h in range(N_HEADS)])
    v_idx = np.concatenate([QK_NOPE + np.arange(V_DIM) + (QK_NOPE + V_DIM) * h for h in range(N_HEADS)])
    kr_idx = np.full((LANES,), -1, np.int64)
    kr_idx[0:ROPE_HALF] = np.arange(ROPE_HALF)
    kr_idx[X2_LANE:X2_LANE + ROPE_HALF] = ROPE_HALF + np.arange(ROPE_HALF)
    return q_idx, k_idx, v_idx, kr_idx


def _rope_lane_tables(length):
    inv_freq = np.float32(ROPE_THETA) ** (-np.arange(ROPE_HALF, dtype=np.float32) / np.float32(ROPE_HALF))
    ang = np.arange(length, dtype=np.float32)[:, None] * inv_freq[None, :].astype(np.float32)
    cos, sin = np.cos(ang).astype(np.float32), np.sin(ang).astype(np.float32)
    ctab = np.ones((length, LANES), np.float32)
    stab = np.zeros((length, LANES), np.float32)
    for lo, sign in ((0, -1.0), (X2_LANE, 1.0)):
        ctab[:, lo:lo + ROPE_HALF] = cos
        stab[:, lo:lo + ROPE_HALF] = sign * sin
    return jnp.asarray(ctab), jnp.asarray(stab)


def kernel(x, meta_tokens, ln_in_g, ln_in_b, w_in, conv_w, conv_b, conv_ln_g, conv_ln_b, q_norm_g, w_uq,
           kv_norm_g, w_ukv, conv_out_g, attn_out_g, w_o, ln1_g, ln1_b, w_rg, b_rg, w_re, b_re,
           w_gate, w_up, w_down, ln2_g, ln2_b):
    assert w_in.shape[0] == 1, "single layer"
    bsz, seq, d = x.shape
    t = bsz * seq
    row = lambda a: a.reshape(1, -1)

    q_idx, k_idx, v_idx, kr_idx = _layout_indices()
    c4 = 2 * C_CONV + Q_LORA + KV_LORA
    w_in_l = jnp.concatenate([w_in[0][:, :c4], _take_cols(w_in[0][:, c4:], kr_idx)], axis=1).astype(BF16)
    wuq_l = _take_cols(w_uq[0], q_idx).astype(BF16)
    wuk_l = _take_cols(w_ukv[0], k_idx).astype(BF16)
    wvt_l = w_ukv[0][:, v_idx].T.astype(BF16)
    wo_l = w_o[0].astype(BF16)
    n_route_pad = ROUTE_ROWS - N_EXPERTS - N_GROUPS
    wr_t = jnp.concatenate([w_re[0].T, w_rg[0].T, jnp.zeros((n_route_pad, d), F32)], axis=0)
    wrh_l = wr_t.astype(BF16)
    wrl_l = (wr_t - wrh_l.astype(F32)).astype(BF16)
    br_l = jnp.broadcast_to(jnp.concatenate([b_re[0], b_rg[0], jnp.zeros((n_route_pad,), F32)])[:, None],
                            (ROUTE_ROWS, LANES))
    ctab, stab = _rope_lane_tables(N_META + seq)

    proj_w = (row(ln_in_g), row(ln_in_b), w_in_l, row(q_norm_g[0]), wuq_l, row(kv_norm_g[0]), wuk_l, wvt_l)
    u_m, _, k_m, vt_m = _proj_call(meta_tokens.astype(F32), N_META, 1, *proj_w, ctab[:N_META], stab[:N_META])
    u, q, k, vt = _proj_call(x.reshape(t, d), PROJ_TM, seq // PROJ_TM, *proj_w, ctab[N_META:], stab[N_META:])

    o = _attn_call(q.reshape(bsz, seq, -1), k.reshape(bsz, seq, -1),
                   vt.reshape(bsz, seq // ATT_T, D_ATTN, ATT_T), k_m, vt_m)

    pre = jnp.concatenate([jnp.zeros((HALO - N_META, C_CONV), F32), u_m], axis=0)
    cw = jnp.concatenate([conv_w[0], jnp.zeros((HALO - CONV_K, C_CONV), F32)], axis=0)
    h1, hp, ri, rw, cnt = _mix_call(x, row(ln_in_g), row(ln_in_b), u.reshape(bsz, seq, C_CONV), pre, cw,
                                    row(conv_b[0]), row(conv_ln_g[0]), row(conv_ln_b[0]), row(conv_out_g[0]),
                                    o, row(attn_out_g[0]), wo_l, row(ln1_g[0]), row(ln1_b[0]), wrh_l, wrl_l, br_l)

    counts = cnt[:N_EXPERTS, 0].astype(jnp.int32)
    padded = (counts + MOE_BM - 1) // MOE_BM * MOE_BM
    pad_end = jnp.cumsum(padded)
    pad_start = pad_end - padded
    n_blocks = (2 * t) // MOE_BM + N_EXPERTS
    dest, blk = _tables_call(pad_start, pad_end, counts, ri, n_blocks)
    blk_e, blk_valid = blk[0, :n_blocks], blk[1, :n_blocks]

    h1f = h1.reshape(t, d)
    dest0, dest1 = dest[0], dest[1]
    xb = _sc_scatter_rows(hp.reshape(t, HALF_D), dest0, dest1, n_blocks * MOE_BM)
    yb = _expert_call(blk_e, blk_valid, xb, w_gate, w_up, w_down)
    yg = _sc_gather_rows(yb, jnp.concatenate([dest0, dest1]))
    out = _combine_rows_call(h1f, rw.reshape(t, 8), row(ln2_g[0]), row(ln2_b[0]), yg)
    return out.reshape(bsz, seq, d)
```

```python
import functools
import math

import numpy as np
import jax
import jax.numpy as jnp
from jax import lax
from jax.experimental import pallas as pl
from jax.experimental.pallas import tpu as pltpu
from jax.experimental.pallas import tpu_sc as plsc

F32 = jnp.float32
BF16 = jnp.bfloat16

D_MODEL = 1024
HALF_D = D_MODEL // 2
N_META = 16
CHUNK = 64
C_CONV = 512
CONV_K = 31
N_HEADS = 8
QK_NOPE = 64
QK_ROPE = 32
V_DIM = 64
Q_LORA = 384
KV_LORA = 256
ROPE_THETA = 10000.0
D_ATTN = N_HEADS * V_DIM
N_GROUPS = 4
EXPERTS_PER_GROUP = 8
N_EXPERTS = N_GROUPS * EXPERTS_PER_GROUP
D_EXPERT = 256
DEEPNORM_ALPHA = 2.0 ** 0.25
EPS = 1e-5
NEG_INF = -1e30

LANES = 128
SUBLANES = 8
HEAD_PAD = LANES
ROPE_HALF = QK_ROPE // 2
X2_LANE = 64
Z_COLS = 2 * C_CONV + Q_LORA + KV_LORA + LANES
Q_SCALE = (QK_NOPE + QK_ROPE) ** -0.5 * math.log2(math.e)

PROJ_TM = 1024
ATT_T = 256
ATT_HEADS = 8
MIX_TM = 512
ROUTE_ROWS = 48
HALO = 32
MOE_BM = 512
COMB_TM = 1024
SC_WINDOW = 64
VMEM_LIMIT = 56 * 1024 * 1024


def _ln(x, g, b):
    mu = jnp.mean(x, axis=-1, keepdims=True)
    xc = x - mu
    var = jnp.mean(xc * xc, axis=-1, keepdims=True)
    return xc * lax.rsqrt(var + EPS) * g + b


def _rms(x, g):
    return x * lax.rsqrt(jnp.mean(x * x, axis=-1, keepdims=True) + EPS) * g


def _sigmoid(x):
    return 1.0 / (1.0 + jnp.exp(-x))


def _pack_bf16_pair(lo, hi):
    lo_bits = lax.bitcast_convert_type(lo.astype(BF16).astype(F32), jnp.uint32)
    hi_bits = lax.bitcast_convert_type(hi.astype(BF16).astype(F32), jnp.uint32)
    return (lo_bits >> 16) | (hi_bits & jnp.uint32(0xFFFF0000))


def _unpack_bf16_pair(packed):
    lo = lax.bitcast_convert_type(packed << 16, F32)
    hi = lax.bitcast_convert_type(packed & jnp.uint32(0xFFFF0000), F32)
    return lo, hi


def _proj_kernel(x_ref, g_ref, b_ref, win_ref, qg_ref, wuq_ref, kvg_ref, wuk_ref, wvt_ref, c_ref, s_ref,
                 u_ref, q_ref, k_ref, vt_ref):
    n_part, part = vt_ref.shape[1], vt_ref.shape[3]
    zs = []
    for r in range(n_part):
        h = _ln(x_ref[r * part:(r + 1) * part, :], g_ref[...], b_ref[...])
        zs.append(jnp.dot(h.astype(BF16), win_ref[...], preferred_element_type=F32))
    c0 = 2 * C_CONV
    for r, z in enumerate(zs):
        rows = slice(r * part, (r + 1) * part)
        u_ref[rows, :] = z[:, :C_CONV] * _sigmoid(z[:, C_CONV:c0])
        cq = z[:, c0:c0 + Q_LORA]
        ckv = z[:, c0 + Q_LORA:c0 + Q_LORA + KV_LORA]
        kr = z[:, c0 + Q_LORA + KV_LORA:]
        cs = c_ref[rows, :]
        sn = s_ref[rows, :]
        q = jnp.dot(_rms(cq, qg_ref[...]).astype(BF16), wuq_ref[...], preferred_element_type=F32)
        kvn = _rms(ckv, kvg_ref[...]).astype(BF16)
        kn = jnp.dot(kvn, wuk_ref[...], preferred_element_type=F32)
        kr_rot = kr * cs + pltpu.roll(kr, X2_LANE, 1) * sn
        for hh in range(N_HEADS):
            sl = slice(HEAD_PAD * hh, HEAD_PAD * (hh + 1))
            qh = q[:, sl]
            qh = (qh * cs + pltpu.roll(qh, X2_LANE, 1) * sn) * Q_SCALE
            q_ref[rows, sl] = qh.astype(BF16)
            k_ref[rows, sl] = (kn[:, sl] + kr_rot).astype(BF16)
        vt = lax.dot_general(wvt_ref[...], kvn, (((1,), (1,)), ((), ())), preferred_element_type=F32)
        vt_ref[0, r] = vt.astype(BF16)


def _proj_call(x2d, tm, pos_blocks, ln_g, ln_b, w_in, qg, wuq, kvg, wuk, wvt, ctab, stab):
    t = x2d.shape[0]
    sub_w = min(tm, ATT_T)
    n_sub = tm // sub_w
    full = lambda shape: pl.BlockSpec(shape, lambda i: (0,) * len(shape))
    row = lambda w: pl.BlockSpec((tm, w), lambda i: (i, 0))
    tab = pl.BlockSpec((tm, LANES), lambda i: (i % pos_blocks, 0))
    return pl.pallas_call(
        _proj_kernel,
        grid=(t // tm,),
        in_specs=[row(D_MODEL), full((1, D_MODEL)), full((1, D_MODEL)), full((D_MODEL, Z_COLS)),
                  full((1, Q_LORA)), full((Q_LORA, N_HEADS * HEAD_PAD)),
                  full((1, KV_LORA)), full((KV_LORA, N_HEADS * HEAD_PAD)), full((D_ATTN, KV_LORA)), tab, tab],
        out_specs=[row(C_CONV), row(N_HEADS * HEAD_PAD), row(N_HEADS * HEAD_PAD),
                   pl.BlockSpec((1, n_sub, D_ATTN, sub_w), lambda i: (i, 0, 0, 0))],
        out_shape=[jax.ShapeDtypeStruct((t, C_CONV), F32),
                   jax.ShapeDtypeStruct((t, N_HEADS * HEAD_PAD), BF16),
                   jax.ShapeDtypeStruct((t, N_HEADS * HEAD_PAD), BF16),
                   jax.ShapeDtypeStruct((t // tm, n_sub, D_ATTN, sub_w), BF16)],
        compiler_params=pltpu.CompilerParams(dimension_semantics=("arbitrary",),
                                             vmem_limit_bytes=VMEM_LIMIT),
        name="proj",
    )(x2d, ln_g, ln_b, w_in, qg, wuq, kvg, wuk, wvt, ctab, stab)


def _attn_kernel(q_ref, k_ref, vt_ref, km_ref, vtm_ref, o_ref, s_sc):
    seq = q_ref.shape[1]
    nq = seq // ATT_T
    key_chunk = lax.broadcasted_iota(jnp.int32, (ATT_T, ATT_T), 0) // CHUNK
    qry_chunk = lax.broadcasted_iota(jnp.int32, (ATT_T, ATT_T), 1) // CHUNK
    diag_visible = key_chunk <= qry_chunk
    nt = (((1,), (1,)), ((), ()))
    heads = range(ATT_HEADS)
    hsl = [slice(HEAD_PAD * hh, HEAD_PAD * (hh + 1)) for hh in heads]
    vsl = [slice(V_DIM * hh, V_DIM * (hh + 1)) for hh in heads]

    def update(s, vt, m, l, acc):
        m_new = jnp.maximum(m, jnp.max(s, axis=0, keepdims=True))
        alpha = jnp.exp2(m - m_new)
        p = jnp.exp2(s - m_new)
        l = alpha * l + jnp.sum(p, axis=0, keepdims=True)
        acc = alpha * acc + jnp.dot(vt, p.astype(BF16), preferred_element_type=F32)
        return m_new, l, acc

    def scores(qi, j, hh):
        q0 = pl.multiple_of(qi * ATT_T, ATT_T)
        k0 = pl.multiple_of(j * ATT_T, ATT_T)
        return lax.dot_general(k_ref[0, pl.ds(k0, ATT_T), hsl[hh]], q_ref[0, pl.ds(q0, ATT_T), hsl[hh]], nt,
                               preferred_element_type=F32)

    for hh in heads:
        s_sc[0, hh] = scores(0, 0, hh)

    def q_body(qi, carry):
        row0 = pl.multiple_of(qi * ATT_T, ATT_T)
        slot = qi % 2
        state = []
        for hh in heads:
            state += [jnp.full((1, ATT_T), NEG_INF, F32), jnp.zeros((1, ATT_T), F32),
                      jnp.zeros((V_DIM, ATT_T), F32)]

        def kv_body(j, st):
            out = []
            s_next = scores(qi, j + 1, 0)
            for hh in heads:
                s_cur = s_sc[slot, hh]
                s_after = scores(qi, j + 1, hh + 1) if hh + 1 < ATT_HEADS else None
                out += update(s_cur, vt_ref[0, j, vsl[hh], :], *st[3 * hh:3 * hh + 3])
                s_sc[slot, hh] = s_next
                s_next = s_after
            return tuple(out)

        state = lax.fori_loop(0, qi, kv_body, tuple(state))
        q_next = jnp.minimum(qi + 1, nq - 1)
        sms = [lax.dot_general(km_ref[:, hsl[hh]], q_ref[0, pl.ds(row0, ATT_T), hsl[hh]], nt,
                               preferred_element_type=F32) for hh in heads]
        outs = []
        for hh in heads:
            m, l, acc = state[3 * hh:3 * hh + 3]
            s = jnp.where(diag_visible, s_sc[slot, hh], NEG_INF)
            sm = sms[hh]
            m_new = jnp.maximum(m, jnp.maximum(jnp.max(s, axis=0, keepdims=True),
                                               jnp.max(sm, axis=0, keepdims=True)))
            alpha = jnp.exp2(m - m_new)
            p = jnp.exp2(s - m_new)
            pm = jnp.exp2(sm - m_new)
            l = alpha * l + jnp.sum(p, axis=0, keepdims=True) + jnp.sum(pm, axis=0, keepdims=True)
            acc = (alpha * acc + jnp.dot(vt_ref[0, qi, vsl[hh], :], p.astype(BF16), preferred_element_type=F32)
                   + jnp.dot(vtm_ref[0, 0, vsl[hh], :], pm.astype(BF16), preferred_element_type=F32))
            outs.append(acc / l)
        for hh in heads:
            s_sc[1 - slot, hh] = scores(q_next, 0, hh)
        o_ref[0, pl.ds(row0, ATT_T), :] = jnp.concatenate(outs, axis=0).T
        return carry

    lax.fori_loop(0, nq, q_body, 0)


def _attn_call(q, k, vt, k_meta, vt_meta):
    b, seq, _ = q.shape
    nk = seq // ATT_T
    hb = ATT_HEADS
    return pl.pallas_call(
        _attn_kernel,
        grid=(b, N_HEADS // hb),
        in_specs=[pl.BlockSpec((1, seq, hb * HEAD_PAD), lambda i, j: (i, 0, j)),
                  pl.BlockSpec((1, seq, hb * HEAD_PAD), lambda i, j: (i, 0, j)),
                  pl.BlockSpec((1, nk, hb * V_DIM, ATT_T), lambda i, j: (i, 0, j, 0)),
                  pl.BlockSpec((N_META, hb * HEAD_PAD), lambda i, j: (0, j)),
                  pl.BlockSpec((1, 1, hb * V_DIM, N_META), lambda i, j: (0, 0, j, 0))],
        out_specs=pl.BlockSpec((1, seq, hb * V_DIM), lambda i, j: (i, 0, j)),
        out_shape=jax.ShapeDtypeStruct((b, seq, D_ATTN), F32),
        scratch_shapes=[pltpu.VMEM((2, hb, ATT_T, ATT_T), F32)],
        compiler_params=pltpu.CompilerParams(dimension_semantics=("arbitrary", "arbitrary"),
                                             vmem_limit_bytes=VMEM_LIMIT),
        name="attn",
    )(q, k, vt, k_meta, vt_meta)


def _mix_kernel(x_ref, ig_ref, ib_ref, u_ref, uh_ref, pre_ref, cw_ref, cb_ref, clg_ref, clb_ref,
                cog_ref, o_ref, aog_ref, wo_ref, l1g_ref, l1b_ref, wrh_ref, wrl_ref, br_ref, tri_ref,
                h1_ref, hp_ref, ri_ref, rw_ref, cnt_ref, win_sc, sh_sc, carry_sc):
    bi = pl.program_id(0)
    ti = pl.program_id(1)
    tm = MIX_TM

    @pl.when((bi == 0) & (ti == 0))
    def _():
        carry_sc[...] = jnp.zeros_like(carry_sc)

    win_sc[0:HALO, :] = jnp.where(ti == 0, pre_ref[...], uh_ref[0])
    win_sc[HALO:, :] = u_ref[0]
    first = HALO - (CONV_K - 1)
    for res in range(1, SUBLANES):
        rows = tm + HALO - SUBLANES
        sh_sc[res - 1, 0:rows, :] = win_sc[pl.ds(res, rows), :]

    acc = jnp.broadcast_to(cb_ref[...], (tm, C_CONV))
    for o in range(first, first + CONV_K):
        res, lo = o % SUBLANES, o - o % SUBLANES
        src = sh_sc[res - 1, lo:lo + tm, :] if res else win_sc[lo:lo + tm, :]
        acc = acc + src * cw_ref[o - first:o - first + 1, :]
    c = _ln(acc, clg_ref[...], clb_ref[...])
    c = c * _sigmoid(c)
    cn = _rms(c, cog_ref[...])
    on = _rms(o_ref[0], aog_ref[...])
    y = (jnp.dot(cn.astype(BF16), wo_ref[0:C_CONV, :], preferred_element_type=F32)
         + jnp.dot(on.astype(BF16), wo_ref[C_CONV:, :], preferred_element_type=F32))
    h = _ln(x_ref[0], ig_ref[...], ib_ref[...])
    h1 = _ln(DEEPNORM_ALPHA * h + y, l1g_ref[...], l1b_ref[...])
    h1_ref[0] = h1
    hp_ref[0] = _pack_bf16_pair(h1[:, :HALF_D], h1[:, HALF_D:])

    nt = (((1,), (1,)), ((), ()))
    h_hi = h1.astype(BF16)
    h_lo = (h1 - h_hi.astype(F32)).astype(BF16)
    lt = (lax.dot_general(wrh_ref[...], h_hi, nt, preferred_element_type=F32)
          + lax.dot_general(wrl_ref[...], h_hi, nt, preferred_element_type=F32)
          + lax.dot_general(wrh_ref[...], h_lo, nt, preferred_element_type=F32)) + br_ref[:, 0:1]
    row = lax.broadcasted_iota(jnp.int32, (ROUTE_ROWS, tm), 0)
    big = jnp.int32(1 << 20)
    is_g = (row >= N_EXPERTS) & (row < N_EXPERTS + N_GROUPS)
    gl = jnp.where(is_g, lt, -jnp.inf)
    gmax = jnp.max(gl, axis=0, keepdims=True)
    g_idx = jnp.min(jnp.where(gl == gmax, row - N_EXPERTS, big), axis=0, keepdims=True)
    g_w = 1.0 / jnp.sum(jnp.exp(gl - gmax), axis=0, keepdims=True)
    in_grp = (row < N_EXPERTS) & ((row // EXPERTS_PER_GROUP) == g_idx)
    el = jnp.where(in_grp, lt, -jnp.inf)
    v1 = jnp.max(el, axis=0, keepdims=True)
    i1 = jnp.min(jnp.where(el == v1, row, big), axis=0, keepdims=True)
    el2 = jnp.where(row == i1, -jnp.inf, el)
    v2 = jnp.max(el2, axis=0, keepdims=True)
    i2 = jnp.min(jnp.where(el2 == v2, row, big), axis=0, keepdims=True)
    e2 = jnp.exp(v2 - v1)
    w1 = g_w / (1.0 + e2)
    w2 = g_w * e2 / (1.0 + e2)

    oh1 = row == i1
    oh2 = row == i2
    oh = jnp.where(oh1 | oh2, 1.0, 0.0)
    before = jnp.dot(oh.astype(BF16), tri_ref[...], preferred_element_type=F32) + carry_sc[:, 0:1]
    r1 = jnp.sum(jnp.where(oh1, before, 0.0), axis=0, keepdims=True).astype(jnp.int32)
    r2 = jnp.sum(jnp.where(oh2, before, 0.0), axis=0, keepdims=True).astype(jnp.int32)
    carry_sc[...] = carry_sc[...] + jnp.sum(oh, axis=1, keepdims=True)
    cnt_ref[...] = carry_sc[...]

    sub = lax.broadcasted_iota(jnp.int32, (SUBLANES, tm), 0)
    ri_ref[...] = jnp.where(sub == 0, i1, jnp.where(sub == 1, i2, jnp.where(sub == 2, r1,
                            jnp.where(sub == 3, r2, 0))))
    lrow = lax.broadcasted_iota(jnp.int32, (LANES, tm), 0)
    rw_ref[0] = jnp.where(lrow == 0, w1, jnp.where(lrow == 1, w2, 0.0)).T[:, :8]


def _mix_call(x, ig, ib, u, pre, cw, cb, clg, clb, cog, o, aog, wo, l1g, l1b, wrh, wrl, br):
    b, seq, _ = x.shape
    tm = MIX_TM
    nt = seq // tm
    full = lambda shape: pl.BlockSpec(shape, lambda i, j: (0,) * len(shape))
    tile = lambda w: pl.BlockSpec((1, tm, w), lambda i, j: (i, j, 0))
    halo = pl.BlockSpec((1, HALO, C_CONV), lambda i, j: (i, jnp.maximum(j * (tm // HALO) - 1, 0), 0))
    earlier = jnp.asarray(np.triu(np.ones((tm, tm), np.float32), k=1), BF16)
    return pl.pallas_call(
        _mix_kernel,
        grid=(b, nt),
        in_specs=[tile(D_MODEL), full((1, D_MODEL)), full((1, D_MODEL)),
                  tile(C_CONV), halo, full((HALO, C_CONV)), full((HALO, C_CONV)), full((1, C_CONV)),
                  full((1, C_CONV)), full((1, C_CONV)), full((1, C_CONV)),
                  tile(D_ATTN), full((1, D_ATTN)), full((D_MODEL, D_MODEL)),
                  full((1, D_MODEL)), full((1, D_MODEL)), full((ROUTE_ROWS, D_MODEL)),
                  full((ROUTE_ROWS, D_MODEL)), full((ROUTE_ROWS, LANES)), full((tm, tm))],
        out_specs=[tile(D_MODEL), tile(HALF_D), pl.BlockSpec((8, tm), lambda i, j: (0, i * nt + j)), tile(8),
                   full((ROUTE_ROWS, LANES))],
        out_shape=[jax.ShapeDtypeStruct((b, seq, D_MODEL), F32),
                   jax.ShapeDtypeStruct((b, seq, HALF_D), jnp.uint32),
                   jax.ShapeDtypeStruct((8, b * seq), jnp.int32),
                   jax.ShapeDtypeStruct((b, seq, 8), F32),
                   jax.ShapeDtypeStruct((ROUTE_ROWS, LANES), F32)],
        scratch_shapes=[pltpu.VMEM((HALO + tm, C_CONV), F32),
                        pltpu.VMEM((SUBLANES - 1, HALO + tm, C_CONV), F32),
                        pltpu.VMEM((ROUTE_ROWS, LANES), F32)],
        compiler_params=pltpu.CompilerParams(dimension_semantics=("arbitrary", "arbitrary"),
                                             vmem_limit_bytes=VMEM_LIMIT),
        name="mix",
    )(x, ig, ib, u, u, pre, cw, cb, clg, clb, cog, o, aog, wo, l1g, l1b, wrh, wrl, br, earlier)


def _tables_kernel(ps_ref, pe_ref, cn_ref, ri_ref, dest_ref, blk_ref):
    ri = ri_ref[...]
    start = jnp.zeros_like(ri)
    for e in range(N_EXPERTS):
        start = jnp.where(ri == e, ps_ref[e], start)
    dest_ref[...] = jnp.zeros_like(dest_ref)
    dest_ref[0:2, :] = start[0:2, :] + ri_ref[2:4, :]
    blk_start = lax.broadcasted_iota(jnp.int32, blk_ref.shape, 1) * MOE_BM
    be = jnp.zeros(blk_ref.shape, jnp.int32)
    for e in range(N_EXPERTS):
        be = be + jnp.where(pe_ref[e] <= blk_start, 1, 0)
    be = jnp.minimum(be, N_EXPERTS - 1)
    end = jnp.zeros(blk_ref.shape, jnp.int32)
    for e in range(N_EXPERTS):
        end = jnp.where(be == e, ps_ref[e] + cn_ref[e], end)
    valid = jnp.clip(end - blk_start, 0, MOE_BM)
    row = lax.broadcasted_iota(jnp.int32, blk_ref.shape, 0)
    blk_ref[...] = jnp.where(row == 0, be, valid)


def _tables_call(pad_start, pad_end, counts, ri, n_blocks):
    t = ri.shape[1]
    blk_w = -(-n_blocks // LANES) * LANES
    return pl.pallas_call(
        _tables_kernel,
        grid_spec=pltpu.PrefetchScalarGridSpec(
            num_scalar_prefetch=3,
            grid=(1,),
            in_specs=[pl.BlockSpec((SUBLANES, t), lambda i, *_: (0, 0))],
            out_specs=[pl.BlockSpec((SUBLANES, t), lambda i, *_: (0, 0)),
                       pl.BlockSpec((SUBLANES, blk_w), lambda i, *_: (0, 0))]),
        out_shape=[jax.ShapeDtypeStruct((SUBLANES, t), jnp.int32),
                   jax.ShapeDtypeStruct((SUBLANES, blk_w), jnp.int32)],
        compiler_params=pltpu.CompilerParams(dimension_semantics=("arbitrary",),
                                             vmem_limit_bytes=VMEM_LIMIT),
        name="tables",
    )(pad_start, pad_end, counts, ri)


def _expert_kernel(be_ref, nv_ref, x_ref, wg_ref, wu_ref, wd_ref, y_ref, wg_sc, wu_sc, wd_sc):
    i = pl.program_id(0)
    nv = nv_ref[i]

    @pl.when((i == 0) | (be_ref[i] != be_ref[jnp.maximum(i - 1, 0)]))
    def _():
        wg_sc[...] = wg_ref[0, 0].astype(BF16)
        wu_sc[...] = wu_ref[0, 0].astype(BF16)
        wd_sc[...] = wd_ref[0, 0].astype(BF16)

    @pl.when(nv > 0)
    def _():
        rowid = lax.broadcasted_iota(jnp.int32, x_ref.shape, 0)
        x_lo, x_hi = _unpack_bf16_pair(jnp.where(rowid < nv, x_ref[...], jnp.uint32(0)))
        x_lo = x_lo.astype(BF16)
        x_hi = x_hi.astype(BF16)
        g = (jnp.dot(x_lo, wg_sc[:HALF_D, :], preferred_element_type=F32)
             + jnp.dot(x_hi, wg_sc[HALF_D:, :], preferred_element_type=F32))
        up = (jnp.dot(x_lo, wu_sc[:HALF_D, :], preferred_element_type=F32)
              + jnp.dot(x_hi, wu_sc[HALF_D:, :], preferred_element_type=F32))
        act = g * _sigmoid(g) * up
        y = jnp.dot(act.astype(BF16), wd_sc[...], preferred_element_type=F32)
        y_ref[...] = _pack_bf16_pair(y[:, :HALF_D], y[:, HALF_D:])

    @pl.when(nv == 0)
    def _():
        y_ref[...] = jnp.zeros_like(y_ref)


def _expert_call(blk_e, blk_valid, xb, w_gate, w_up, w_down):
    r = xb.shape[0]
    bm = MOE_BM
    wspec = lambda a, b: pl.BlockSpec((1, 1, a, b), lambda i, be, nv: (0, be[i], 0, 0))
    return pl.pallas_call(
        _expert_kernel,
        grid_spec=pltpu.PrefetchScalarGridSpec(
            num_scalar_prefetch=2,
            grid=(r // bm,),
            in_specs=[pl.BlockSpec((bm, HALF_D), lambda i, be, nv: (i, 0)),
                      wspec(D_MODEL, D_EXPERT), wspec(D_MODEL, D_EXPERT), wspec(D_EXPERT, D_MODEL)],
            out_specs=pl.BlockSpec((bm, HALF_D), lambda i, be, nv: (i, 0)),
            scratch_shapes=[pltpu.VMEM((D_MODEL, D_EXPERT), BF16), pltpu.VMEM((D_MODEL, D_EXPERT), BF16),
                            pltpu.VMEM((D_EXPERT, D_MODEL), BF16)]),
        out_shape=jax.ShapeDtypeStruct((r, HALF_D), jnp.uint32),
        compiler_params=pltpu.CompilerParams(dimension_semantics=("arbitrary",),
                                             vmem_limit_bytes=VMEM_LIMIT),
        name="experts",
    )(blk_e, blk_valid, xb, w_gate, w_up, w_down)


def _sc_gather_rows(table, idx):
    m = idx.shape[0]
    width = table.shape[1]
    sc = plsc.get_sparse_core_info()
    workers = sc.num_cores * sc.num_subcores
    per_worker = m // workers
    mesh = plsc.VectorSubcoreMesh(core_axis_name="c", subcore_axis_name="s")

    n_win = per_worker // SC_WINDOW
    assert n_win % 2 == 0 and n_win >= 2
    buf = lambda: [pltpu.VMEM((SC_WINDOW,), jnp.int32), pltpu.VMEM((SC_WINDOW, width), table.dtype),
                   pltpu.SemaphoreType.DMA]

    @functools.partial(pl.kernel, mesh=mesh, out_type=jax.ShapeDtypeStruct((m, width), table.dtype),
                       scratch_types=buf() + buf())
    def gather(table_hbm, idx_hbm, out_hbm, idx_a, rows_a, sem_a, idx_b, rows_b, sem_b):
        base = (lax.axis_index("s") * sc.num_cores + lax.axis_index("c")) * per_worker
        bufs = ((idx_a, rows_a, sem_a), (idx_b, rows_b, sem_b))

        def fetch(win, which):
            idx_v, rows_v, sem = bufs[which]
            pltpu.sync_copy(idx_hbm.at[pl.ds(base + win * SC_WINDOW, SC_WINDOW)], idx_v)
            pltpu.async_copy(table_hbm.at[idx_v], rows_v, sem)

        def flush(win, which):
            idx_v, rows_v, sem = bufs[which]
            pltpu.make_async_copy(table_hbm.at[idx_v], rows_v, sem).wait()
            pltpu.sync_copy(rows_v, out_hbm.at[pl.ds(base + win * SC_WINDOW, SC_WINDOW)])

        fetch(0, 0)

        @pl.loop(0, n_win // 2 - 1)
        def _(pair):
            win = 2 * pair
            fetch(win + 1, 1)
            flush(win, 0)
            fetch(win + 2, 0)
            flush(win + 1, 1)

        fetch(n_win - 1, 1)
        flush(n_win - 2, 0)
        flush(n_win - 1, 1)

    return gather(table, idx)


def _sc_scatter_rows(rows, idx0, idx1, n_out):
    m, width = rows.shape
    sc = plsc.get_sparse_core_info()
    workers = sc.num_cores * sc.num_subcores
    per_worker = m // workers
    mesh = plsc.VectorSubcoreMesh(core_axis_name="c", subcore_axis_name="s")

    n_win = per_worker // SC_WINDOW
    assert n_win % 2 == 0 and n_win >= 2
    buf = lambda: [pltpu.VMEM((SC_WINDOW,), jnp.int32), pltpu.VMEM((SC_WINDOW,), jnp.int32),
                   pltpu.VMEM((SC_WINDOW, width), rows.dtype),
                   pltpu.SemaphoreType.DMA, pltpu.SemaphoreType.DMA, pltpu.SemaphoreType.DMA]

    @functools.partial(pl.kernel, mesh=mesh, out_type=jax.ShapeDtypeStruct((n_out, width), rows.dtype),
                       scratch_types=buf() + buf())
    def scatter(rows_hbm, idx0_hbm, idx1_hbm, out_hbm, *scratch):
        base = (lax.axis_index("s") * sc.num_cores + lax.axis_index("c")) * per_worker
        bufs = (scratch[:6], scratch[6:])

        def load(win, which):
            idx0_v, idx1_v, rows_v, lsem, _, _ = bufs[which]
            off = base + win * SC_WINDOW
            pltpu.sync_copy(idx0_hbm.at[pl.ds(off, SC_WINDOW)], idx0_v)
            pltpu.sync_copy(idx1_hbm.at[pl.ds(off, SC_WINDOW)], idx1_v)
            pltpu.async_copy(rows_hbm.at[pl.ds(off, SC_WINDOW)], rows_v, lsem)

        def send(which):
            idx0_v, idx1_v, rows_v, lsem, sem0, sem1 = bufs[which]
            pltpu.make_async_copy(rows_hbm.at[pl.ds(0, SC_WINDOW)], rows_v, lsem).wait()
            first = pltpu.async_copy(rows_v, out_hbm.at[idx0_v], sem0)
            second = pltpu.async_copy(rows_v, out_hbm.at[idx1_v], sem1)
            first.wait()
            second.wait()

        load(0, 0)

        @pl.loop(0, n_win // 2 - 1)
        def _(pair):
            win = 2 * pair
            load(win + 1, 1)
            send(0)
            load(win + 2, 0)
            send(1)

        load(n_win - 1, 1)
        send(0)
        send(1)

    return scatter(rows, idx0, idx1)


def _combine_rows_kernel(h_ref, w_ref, g_ref, b_ref, y0_ref, y1_ref, o_ref):
    w = w_ref[...]
    y0_lo, y0_hi = _unpack_bf16_pair(y0_ref[...])
    y1_lo, y1_hi = _unpack_bf16_pair(y1_ref[...])
    y = jnp.concatenate([w[:, 0:1] * y0_lo + w[:, 1:2] * y1_lo, w[:, 0:1] * y0_hi + w[:, 1:2] * y1_hi], axis=1)
    o_ref[...] = _ln(DEEPNORM_ALPHA * h_ref[...] + y, g_ref[...], b_ref[...])


def _combine_rows_call(h1, rw, g, b, yg):
    t = h1.shape[0]
    tm = COMB_TM
    n = t // tm
    return pl.pallas_call(
        _combine_rows_kernel,
        grid=(n,),
        in_specs=[pl.BlockSpec((tm, D_MODEL), lambda i: (i, 0)),
                  pl.BlockSpec((tm, 8), lambda i: (i, 0)),
                  pl.BlockSpec((1, D_MODEL), lambda i: (0, 0)),
                  pl.BlockSpec((1, D_MODEL), lambda i: (0, 0)),
                  pl.BlockSpec((tm, HALF_D), lambda i: (i, 0)),
                  pl.BlockSpec((tm, HALF_D), lambda i: (n + i, 0))],
        out_specs=pl.BlockSpec((tm, D_MODEL), lambda i: (i, 0)),
        out_shape=jax.ShapeDtypeStruct((t, D_MODEL), F32),
        compiler_params=pltpu.CompilerParams(dimension_semantics=("arbitrary",),
                                             vmem_limit_bytes=VMEM_LIMIT),
        name="combine_rows",
    )(h1, rw, g, b, yg, yg)


def _take_cols(w, idx):
    wz = jnp.concatenate([w, jnp.zeros((w.shape[0], 1), w.dtype)], axis=1)
    return wz[:, np.where(idx < 0, w.shape[1], idx)]


def _layout_indices():
    lane_src = np.full((HEAD_PAD,), -1, np.int64)
    lane_src[0:ROPE_HALF] = QK_NOPE + np.arange(ROPE_HALF)
    lane_src[ROPE_HALF:X2_LANE] = np.arange(X2_LANE - ROPE_HALF)
    lane_src[X2_LANE:X2_LANE + ROPE_HALF] = QK_NOPE + ROPE_HALF + np.arange(ROPE_HALF)
    n_lo = X2_LANE - ROPE_HALF
    lane_src[X2_LANE + ROPE_HALF:X2_LANE + ROPE_HALF + QK_NOPE - n_lo] = n_lo + np.arange(QK_NOPE - n_lo)
    q_idx = np.concatenate([np.where(lane_src >= 0, lane_src + (QK_NOPE + QK_ROPE) * h, -1)
                            for h in range(N_HEADS)])
    nope_src = np.where(lane_src < QK_NOPE, lane_src, -1)
    k_idx = np.concatenate([np.where(nope_src >= 0, nope_src + (QK_NOPE + V_DIM) * h, -1)
                            for h in range(N_HEADS)])
    v_idx = np.concatenate([QK_NOPE + np.arange(V_DIM) + (QK_NOPE + V_DIM) * h for h in range(N_HEADS)])
    kr_idx = np.full((LANES,), -1, np.int64)
    kr_idx[0:ROPE_HALF] = np.arange(ROPE_HALF)
    kr_idx[X2_LANE:X2_LANE + ROPE_HALF] = ROPE_HALF + np.arange(ROPE_HALF)
    return q_idx, k_idx, v_idx, kr_idx


def _rope_lane_tables(length):
    inv_freq = np.float32(ROPE_THETA) ** (-np.arange(ROPE_HALF, dtype=np.float32) / np.float32(ROPE_HALF))
    ang = np.arange(length, dtype=np.float32)[:, None] * inv_freq[None, :].astype(np.float32)
    cos, sin = np.cos(ang).astype(np.float32), np.sin(ang).astype(np.float32)
    ctab = np.ones((length, LANES), np.float32)
    stab = np.zeros((length, LANES), np.float32)
    for lo, sign in ((0, -1.0), (X2_LANE, 1.0)):
        ctab[:, lo:lo + ROPE_HALF] = cos
        stab[:, lo:lo + ROPE_HALF] = sign * sin
    return jnp.asarray(ctab), jnp.asarray(stab)


def kernel(x, meta_tokens, ln_in_g, ln_in_b, w_in, conv_w, conv_b, conv_ln_g, conv_ln_b, q_norm_g, w_uq,
           kv_norm_g, w_ukv, conv_out_g, attn_out_g, w_o, ln1_g, ln1_b, w_rg, b_rg, w_re, b_re,
           w_gate, w_up, w_down, ln2_g, ln2_b):
    assert w_in.shape[0] == 1, "single layer"
    bsz, seq, d = x.shape
    t = bsz * seq
    row = lambda a: a.reshape(1, -1)

    q_idx, k_idx, v_idx, kr_idx = _layout_indices()
    c4 = 2 * C_CONV + Q_LORA + KV_LORA
    w_in_l = jnp.concatenate([w_in[0][:, :c4], _take_cols(w_in[0][:, c4:], kr_idx)], axis=1).astype(BF16)
    wuq_l = _take_cols(w_uq[0], q_idx).astype(BF16)
    wuk_l = _take_cols(w_ukv[0], k_idx).astype(BF16)
    wvt_l = w_ukv[0][:, v_idx].T.astype(BF16)
    wo_l = w_o[0].astype(BF16)
    n_route_pad = ROUTE_ROWS - N_EXPERTS - N_GROUPS
    wr_t = jnp.concatenate([w_re[0].T, w_rg[0].T, jnp.zeros((n_route_pad, d), F32)], axis=0)
    wrh_l = wr_t.astype(BF16)
    wrl_l = (wr_t - wrh_l.astype(F32)).astype(BF16)
    br_l = jnp.broadcast_to(jnp.concatenate([b_re[0], b_rg[0], jnp.zeros((n_route_pad,), F32)])[:, None],
                            (ROUTE_ROWS, LANES))
    ctab, stab = _rope_lane_tables(N_META + seq)

    proj_w = (row(ln_in_g), row(ln_in_b), w_in_l, row(q_norm_g[0]), wuq_l, row(kv_norm_g[0]), wuk_l, wvt_l)
    u_m, _, k_m, vt_m = _proj_call(meta_tokens.astype(F32), N_META, 1, *proj_w, ctab[:N_META], stab[:N_META])
    u, q, k, vt = _proj_call(x.reshape(t, d), PROJ_TM, seq // PROJ_TM, *proj_w, ctab[N_META:], stab[N_META:])

    o = _attn_call(q.reshape(bsz, seq, -1), k.reshape(bsz, seq, -1),
                   vt.reshape(bsz, seq // ATT_T, D_ATTN, ATT_T), k_m, vt_m)

    pre = jnp.concatenate([jnp.zeros((HALO - N_META, C_CONV), F32), u_m], axis=0)
    cw = jnp.concatenate([conv_w[0], jnp.zeros((HALO - CONV_K, C_CONV), F32)], axis=0)
    h1, hp, ri, rw, cnt = _mix_call(x, row(ln_in_g), row(ln_in_b), u.reshape(bsz, seq, C_CONV), pre, cw,
                                    row(conv_b[0]), row(conv_ln_g[0]), row(conv_ln_b[0]), row(conv_out_g[0]),
                                    o, row(attn_out_g[0]), wo_l, row(ln1_g[0]), row(ln1_b[0]), wrh_l, wrl_l, br_l)

    counts = cnt[:N_EXPERTS, 0].astype(jnp.int32)
    padded = (counts + MOE_BM - 1) // MOE_BM * MOE_BM
    pad_end = jnp.cumsum(padded)
    pad_start = pad_end - padded
    n_blocks = (2 * t) // MOE_BM + N_EXPERTS
    dest, blk = _tables_call(pad_start, pad_end, counts, ri, n_blocks)
    blk_e, blk_valid = blk[0, :n_blocks], blk[1, :n_blocks]

    h1f = h1.reshape(t, d)
    dest0, dest1 = dest[0], dest[1]
    xb = _sc_scatter_rows(hp.reshape(t, HALF_D), dest0, dest1, n_blocks * MOE_BM)
    yb = _expert_call(blk_e, blk_valid, xb, w_gate, w_up, w_down)
    yg = _sc_gather_rows(yb, jnp.concatenate([dest0, dest1]))
    out = _combine_rows_call(h1f, rw.reshape(t, 8), row(ln2_g[0]), row(ln2_b[0]), yg)
    return out.reshape(bsz, seq, d)
```

```python
import functools
import math

import numpy as np
import jax
import jax.numpy as jnp
from jax import lax
from jax.experimental import pallas as pl
from jax.experimental.pallas import tpu as pltpu
from jax.experimental.pallas import tpu_sc as plsc

F32 = jnp.float32
BF16 = jnp.bfloat16

D_MODEL = 1024
HALF_D = D_MODEL // 2
N_META = 16
CHUNK = 64
C_CONV = 512
CONV_K = 31
N_HEADS = 8
QK_NOPE = 64
QK_ROPE = 32
V_DIM = 64
Q_LORA = 384
KV_LORA = 256
ROPE_THETA = 10000.0
D_ATTN = N_HEADS * V_DIM
N_GROUPS = 4
EXPERTS_PER_GROUP = 8
N_EXPERTS = N_GROUPS * EXPERTS_PER_GROUP
D_EXPERT = 256
DEEPNORM_ALPHA = 2.0 ** 0.25
EPS = 1e-5
NEG_INF = -1e30

LANES = 128
SUBLANES = 8
HEAD_PAD = LANES
ROPE_HALF = QK_ROPE // 2
X2_LANE = 64
Z_COLS = 2 * C_CONV + Q_LORA + KV_LORA + LANES
Q_SCALE = (QK_NOPE + QK_ROPE) ** -0.5 * math.log2(math.e)

PROJ_TM = 1024
ATT_T = 256
ATT_HEADS = 8
MIX_TM = 512
ROUTE_ROWS = 48
HALO = 32
MOE_BM = 512
MOE_PARTS = 2
COMB_TM = 1024
SC_WINDOW = 64
VMEM_LIMIT = 56 * 1024 * 1024


def _ln(x, g, b):
    mu = jnp.mean(x, axis=-1, keepdims=True)
    xc = x - mu
    var = jnp.mean(xc * xc, axis=-1, keepdims=True)
    return xc * lax.rsqrt(var + EPS) * g + b


def _rms(x, g):
    return x * lax.rsqrt(jnp.mean(x * x, axis=-1, keepdims=True) + EPS) * g


def _sigmoid(x):
    return 1.0 / (1.0 + jnp.exp(-x))


def _pack_bf16_pair(lo, hi):
    lo_bits = lax.bitcast_convert_type(lo.astype(BF16).astype(F32), jnp.uint32)
    hi_bits = lax.bitcast_convert_type(hi.astype(BF16).astype(F32), jnp.uint32)
    return (lo_bits >> 16) | (hi_bits & jnp.uint32(0xFFFF0000))


def _unpack_bf16_pair(packed):
    lo = lax.bitcast_convert_type(packed << 16, F32)
    hi = lax.bitcast_convert_type(packed & jnp.uint32(0xFFFF0000), F32)
    return lo, hi


def _proj_kernel(x_ref, g_ref, b_ref, win_ref, qg_ref, wuq_ref, kvg_ref, wuk_ref, wvt_ref, c_ref, s_ref,
                 u_ref, q_ref, k_ref, vt_ref):
    n_part, part = vt_ref.shape[1], vt_ref.shape[3]
    zs = []
    for r in range(n_part):
        h = _ln(x_ref[r * part:(r + 1) * part, :], g_ref[...], b_ref[...])
        zs.append(jnp.dot(h.astype(BF16), win_ref[...], preferred_element_type=F32))
    c0 = 2 * C_CONV
    for r, z in enumerate(zs):
        rows = slice(r * part, (r + 1) * part)
        u_ref[rows, :] = z[:, :C_CONV] * _sigmoid(z[:, C_CONV:c0])
        cq = z[:, c0:c0 + Q_LORA]
        ckv = z[:, c0 + Q_LORA:c0 + Q_LORA + KV_LORA]
        kr = z[:, c0 + Q_LORA + KV_LORA:]
        cs = c_ref[rows, :]
        sn = s_ref[rows, :]
        q = jnp.dot(_rms(cq, qg_ref[...]).astype(BF16), wuq_ref[...], preferred_element_type=F32)
        kvn = _rms(ckv, kvg_ref[...]).astype(BF16)
        kn = jnp.dot(kvn, wuk_ref[...], preferred_element_type=F32)
        kr_rot = kr * cs + pltpu.roll(kr, X2_LANE, 1) * sn
        for hh in range(N_HEADS):
            sl = slice(HEAD_PAD * hh, HEAD_PAD * (hh + 1))
            qh = q[:, sl]
            qh = (qh * cs + pltpu.roll(qh, X2_LANE, 1) * sn) * Q_SCALE
            q_ref[rows, sl] = qh.astype(BF16)
            k_ref[rows, sl] = (kn[:, sl] + kr_rot).astype(BF16)
        vt = lax.dot_general(wvt_ref[...], kvn, (((1,), (1,)), ((), ())), preferred_element_type=F32)
        vt_ref[0, r] = vt.astype(BF16)


def _proj_call(x2d, tm, pos_blocks, ln_g, ln_b, w_in, qg, wuq, kvg, wuk, wvt, ctab, stab):
    t = x2d.shape[0]
    sub_w = min(tm, ATT_T)
    n_sub = tm // sub_w
    full = lambda shape: pl.BlockSpec(shape, lambda i: (0,) * len(shape))
    row = lambda w: pl.BlockSpec((tm, w), lambda i: (i, 0))
    tab = pl.BlockSpec((tm, LANES), lambda i: (i % pos_blocks, 0))
    return pl.pallas_call(
        _proj_kernel,
        grid=(t // tm,),
        in_specs=[row(D_MODEL), full((1, D_MODEL)), full((1, D_MODEL)), full((D_MODEL, Z_COLS)),
                  full((1, Q_LORA)), full((Q_LORA, N_HEADS * HEAD_PAD)),
                  full((1, KV_LORA)), full((KV_LORA, N_HEADS * HEAD_PAD)), full((D_ATTN, KV_LORA)), tab, tab],
        out_specs=[row(C_CONV), row(N_HEADS * HEAD_PAD), row(N_HEADS * HEAD_PAD),
                   pl.BlockSpec((1, n_sub, D_ATTN, sub_w), lambda i: (i, 0, 0, 0))],
        out_shape=[jax.ShapeDtypeStruct((t, C_CONV), F32),
                   jax.ShapeDtypeStruct((t, N_HEADS * HEAD_PAD), BF16),
                   jax.ShapeDtypeStruct((t, N_HEADS * HEAD_PAD), BF16),
                   jax.ShapeDtypeStruct((t // tm, n_sub, D_ATTN, sub_w), BF16)],
        compiler_params=pltpu.CompilerParams(dimension_semantics=("arbitrary",),
                                             vmem_limit_bytes=VMEM_LIMIT),
        name="proj",
    )(x2d, ln_g, ln_b, w_in, qg, wuq, kvg, wuk, wvt, ctab, stab)


def _attn_kernel(q_ref, k_ref, vt_ref, km_ref, vtm_ref, o_ref, s_sc):
    seq = q_ref.shape[1]
    nq = seq // ATT_T
    key_chunk = lax.broadcasted_iota(jnp.int32, (ATT_T, ATT_T), 0) // CHUNK
    qry_chunk = lax.broadcasted_iota(jnp.int32, (ATT_T, ATT_T), 1) // CHUNK
    diag_visible = key_chunk <= qry_chunk
    nt = (((1,), (1,)), ((), ()))
    heads = range(ATT_HEADS)
    hsl = [slice(HEAD_PAD * hh, HEAD_PAD * (hh + 1)) for hh in heads]
    vsl = [slice(V_DIM * hh, V_DIM * (hh + 1)) for hh in heads]

    def update(s, vt, m, l, acc):
        m_new = jnp.maximum(m, jnp.max(s, axis=0, keepdims=True))
        alpha = jnp.exp2(m - m_new)
        p = jnp.exp2(s - m_new)
        l = alpha * l + jnp.sum(p, axis=0, keepdims=True)
        acc = alpha * acc + jnp.dot(vt, p.astype(BF16), preferred_element_type=F32)
        return m_new, l, acc

    def scores(qi, j, hh):
        q0 = pl.multiple_of(qi * ATT_T, ATT_T)
        k0 = pl.multiple_of(j * ATT_T, ATT_T)
        return lax.dot_general(k_ref[0, pl.ds(k0, ATT_T), hsl[hh]], q_ref[0, pl.ds(q0, ATT_T), hsl[hh]], nt,
                               preferred_element_type=F32)

    for hh in heads:
        s_sc[0, hh] = scores(0, 0, hh)

    def q_body(qi, carry):
        row0 = pl.multiple_of(qi * ATT_T, ATT_T)
        slot = qi % 2
        state = []
        for hh in heads:
            state += [jnp.full((1, ATT_T), NEG_INF, F32), jnp.zeros((1, ATT_T), F32),
                      jnp.zeros((V_DIM, ATT_T), F32)]

        def kv_body(j, st):
            out = []
            s_next = scores(qi, j + 1, 0)
            for hh in heads:
                s_cur = s_sc[slot, hh]
                s_after = scores(qi, j + 1, hh + 1) if hh + 1 < ATT_HEADS else None
                out += update(s_cur, vt_ref[0, j, vsl[hh], :], *st[3 * hh:3 * hh + 3])
                s_sc[slot, hh] = s_next
                s_next = s_after
            return tuple(out)

        state = lax.fori_loop(0, qi, kv_body, tuple(state))
        q_next = jnp.minimum(qi + 1, nq - 1)
        sms = [lax.dot_general(km_ref[:, hsl[hh]], q_ref[0, pl.ds(row0, ATT_T), hsl[hh]], nt,
                               preferred_element_type=F32) for hh in heads]
        outs = []
        for hh in heads:
            m, l, acc = state[3 * hh:3 * hh + 3]
            s = jnp.where(diag_visible, s_sc[slot, hh], NEG_INF)
            sm = sms[hh]
            m_new = jnp.maximum(m, jnp.maximum(jnp.max(s, axis=0, keepdims=True),
                                               jnp.max(sm, axis=0, keepdims=True)))
            alpha = jnp.exp2(m - m_new)
            p = jnp.exp2(s - m_new)
            pm = jnp.exp2(sm - m_new)
            l = alpha * l + jnp.sum(p, axis=0, keepdims=True) + jnp.sum(pm, axis=0, keepdims=True)
            acc = (alpha * acc + jnp.dot(vt_ref[0, qi, vsl[hh], :], p.astype(BF16), preferred_element_type=F32)
                   + jnp.dot(vtm_ref[0, 0, vsl[hh], :], pm.astype(BF16), preferred_element_type=F32))
            outs.append(acc / l)
        for hh in heads:
            s_sc[1 - slot, hh] = scores(q_next, 0, hh)
        o_ref[0, pl.ds(row0, ATT_T), :] = jnp.concatenate(outs, axis=0).T
        return carry

    lax.fori_loop(0, nq, q_body, 0)


def _attn_call(q, k, vt, k_meta, vt_meta):
    b, seq, _ = q.shape
    nk = seq // ATT_T
    hb = ATT_HEADS
    return pl.pallas_call(
        _attn_kernel,
        grid=(b, N_HEADS // hb),
        in_specs=[pl.BlockSpec((1, seq, hb * HEAD_PAD), lambda i, j: (i, 0, j)),
                  pl.BlockSpec((1, seq, hb * HEAD_PAD), lambda i, j: (i, 0, j)),
                  pl.BlockSpec((1, nk, hb * V_DIM, ATT_T), lambda i, j: (i, 0, j, 0)),
                  pl.BlockSpec((N_META, hb * HEAD_PAD), lambda i, j: (0, j)),
                  pl.BlockSpec((1, 1, hb * V_DIM, N_META), lambda i, j: (0, 0, j, 0))],
        out_specs=pl.BlockSpec((1, seq, hb * V_DIM), lambda i, j: (i, 0, j)),
        out_shape=jax.ShapeDtypeStruct((b, seq, D_ATTN), F32),
        scratch_shapes=[pltpu.VMEM((2, hb, ATT_T, ATT_T), F32)],
        compiler_params=pltpu.CompilerParams(dimension_semantics=("arbitrary", "arbitrary"),
                                             vmem_limit_bytes=VMEM_LIMIT),
        name="attn",
    )(q, k, vt, k_meta, vt_meta)


def _mix_kernel(x_ref, ig_ref, ib_ref, u_ref, uh_ref, pre_ref, cw_ref, cb_ref, clg_ref, clb_ref,
                cog_ref, o_ref, aog_ref, wo_ref, l1g_ref, l1b_ref, wrh_ref, wrl_ref, br_ref, tri_ref,
                h1_ref, hp_ref, ri_ref, rw_ref, cnt_ref, win_sc, sh_sc, carry_sc):
    bi = pl.program_id(0)
    ti = pl.program_id(1)
    tm = MIX_TM

    @pl.when((bi == 0) & (ti == 0))
    def _():
        carry_sc[...] = jnp.zeros_like(carry_sc)

    win_sc[0:HALO, :] = jnp.where(ti == 0, pre_ref[...], uh_ref[0])
    win_sc[HALO:, :] = u_ref[0]
    first = HALO - (CONV_K - 1)
    for res in range(1, SUBLANES):
        rows = tm + HALO - SUBLANES
        sh_sc[res - 1, 0:rows, :] = win_sc[pl.ds(res, rows), :]

    acc = jnp.broadcast_to(cb_ref[...], (tm, C_CONV))
    for o in range(first, first + CONV_K):
        res, lo = o % SUBLANES, o - o % SUBLANES
        src = sh_sc[res - 1, lo:lo + tm, :] if res else win_sc[lo:lo + tm, :]
        acc = acc + src * cw_ref[o - first:o - first + 1, :]
    c = _ln(acc, clg_ref[...], clb_ref[...])
    c = c * _sigmoid(c)
    cn = _rms(c, cog_ref[...])
    on = _rms(o_ref[0], aog_ref[...])
    y = (jnp.dot(cn.astype(BF16), wo_ref[0:C_CONV, :], preferred_element_type=F32)
         + jnp.dot(on.astype(BF16), wo_ref[C_CONV:, :], preferred_element_type=F32))
    h = _ln(x_ref[0], ig_ref[...], ib_ref[...])
    h1 = _ln(DEEPNORM_ALPHA * h + y, l1g_ref[...], l1b_ref[...])
    h1_ref[0] = h1
    hp_ref[0] = _pack_bf16_pair(h1[:, :HALF_D], h1[:, HALF_D:])

    nt = (((1,), (1,)), ((), ()))
    h_hi = h1.astype(BF16)
    h_lo = (h1 - h_hi.astype(F32)).astype(BF16)
    lt = (lax.dot_general(wrh_ref[...], h_hi, nt, preferred_element_type=F32)
          + lax.dot_general(wrl_ref[...], h_hi, nt, preferred_element_type=F32)
          + lax.dot_general(wrh_ref[...], h_lo, nt, preferred_element_type=F32)) + br_ref[:, 0:1]
    row = lax.broadcasted_iota(jnp.int32, (ROUTE_ROWS, tm), 0)
    big = jnp.int32(1 << 20)
    is_g = (row >= N_EXPERTS) & (row < N_EXPERTS + N_GROUPS)
    gl = jnp.where(is_g, lt, -jnp.inf)
    gmax = jnp.max(gl, axis=0, keepdims=True)
    g_idx = jnp.min(jnp.where(gl == gmax, row - N_EXPERTS, big), axis=0, keepdims=True)
    g_w = 1.0 / jnp.sum(jnp.exp(gl - gmax), axis=0, keepdims=True)
    in_grp = (row < N_EXPERTS) & ((row // EXPERTS_PER_GROUP) == g_idx)
    el = jnp.where(in_grp, lt, -jnp.inf)
    v1 = jnp.max(el, axis=0, keepdims=True)
    i1 = jnp.min(jnp.where(el == v1, row, big), axis=0, keepdims=True)
    el2 = jnp.where(row == i1, -jnp.inf, el)
    v2 = jnp.max(el2, axis=0, keepdims=True)
    i2 = jnp.min(jnp.where(el2 == v2, row, big), axis=0, keepdims=True)
    e2 = jnp.exp(v2 - v1)
    w1 = g_w / (1.0 + e2)
    w2 = g_w * e2 / (1.0 + e2)

    oh1 = row == i1
    oh2 = row == i2
    oh = jnp.where(oh1 | oh2, 1.0, 0.0)
    before = jnp.dot(oh.astype(BF16), tri_ref[...], preferred_element_type=F32) + carry_sc[:, 0:1]
    r1 = jnp.sum(jnp.where(oh1, before, 0.0), axis=0, keepdims=True).astype(jnp.int32)
    r2 = jnp.sum(jnp.where(oh2, before, 0.0), axis=0, keepdims=True).astype(jnp.int32)
    carry_sc[...] = carry_sc[...] + jnp.sum(oh, axis=1, keepdims=True)
    cnt_ref[...] = carry_sc[...]

    sub = lax.broadcasted_iota(jnp.int32, (SUBLANES, tm), 0)
    ri_ref[...] = jnp.where(sub == 0, i1, jnp.where(sub == 1, i2, jnp.where(sub == 2, r1,
                            jnp.where(sub == 3, r2, 0))))
    lrow = lax.broadcasted_iota(jnp.int32, (LANES, tm), 0)
    rw_ref[0] = jnp.where(lrow == 0, w1, jnp.where(lrow == 1, w2, 0.0)).T[:, :8]


def _mix_call(b0, b, x, ig, ib, u, pre, cw, cb, clg, clb, cog, o, aog, wo, l1g, l1b, wrh, wrl, br):
    seq = x.shape[1]
    tm = MIX_TM
    nt = seq // tm
    full = lambda shape: pl.BlockSpec(shape, lambda i, j: (0,) * len(shape))
    src = lambda w: pl.BlockSpec((1, tm, w), lambda i, j: (b0 + i, j, 0))
    tile = lambda w: pl.BlockSpec((1, tm, w), lambda i, j: (i, j, 0))
    halo = pl.BlockSpec((1, HALO, C_CONV), lambda i, j: (b0 + i, jnp.maximum(j * (tm // HALO) - 1, 0), 0))
    earlier = jnp.asarray(np.triu(np.ones((tm, tm), np.float32), k=1), BF16)
    return pl.pallas_call(
        _mix_kernel,
        grid=(b, nt),
        in_specs=[src(D_MODEL), full((1, D_MODEL)), full((1, D_MODEL)),
                  src(C_CONV), halo, full((HALO, C_CONV)), full((HALO, C_CONV)), full((1, C_CONV)),
                  full((1, C_CONV)), full((1, C_CONV)), full((1, C_CONV)),
                  src(D_ATTN), full((1, D_ATTN)), full((D_MODEL, D_MODEL)),
                  full((1, D_MODEL)), full((1, D_MODEL)), full((ROUTE_ROWS, D_MODEL)),
                  full((ROUTE_ROWS, D_MODEL)), full((ROUTE_ROWS, LANES)), full((tm, tm))],
        out_specs=[tile(D_MODEL), tile(HALF_D), pl.BlockSpec((8, tm), lambda i, j: (0, i * nt + j)), tile(8),
                   full((ROUTE_ROWS, LANES))],
        out_shape=[jax.ShapeDtypeStruct((b, seq, D_MODEL), F32),
                   jax.ShapeDtypeStruct((b, seq, HALF_D), jnp.uint32),
                   jax.ShapeDtypeStruct((8, b * seq), jnp.int32),
                   jax.ShapeDtypeStruct((b, seq, 8), F32),
                   jax.ShapeDtypeStruct((ROUTE_ROWS, LANES), F32)],
        scratch_shapes=[pltpu.VMEM((HALO + tm, C_CONV), F32),
                        pltpu.VMEM((SUBLANES - 1, HALO + tm, C_CONV), F32),
                        pltpu.VMEM((ROUTE_ROWS, LANES), F32)],
        compiler_params=pltpu.CompilerParams(dimension_semantics=("arbitrary", "arbitrary"),
                                             vmem_limit_bytes=VMEM_LIMIT),
        name="mix",
    )(x, ig, ib, u, u, pre, cw, cb, clg, clb, cog, o, aog, wo, l1g, l1b, wrh, wrl, br, earlier)


def _tables_kernel(ps_ref, pe_ref, cn_ref, ri_ref, dest_ref, blk_ref):
    ri = ri_ref[...]
    start = jnp.zeros_like(ri)
    for e in range(N_EXPERTS):
        start = jnp.where(ri == e, ps_ref[e], start)
    dest_ref[...] = jnp.zeros_like(dest_ref)
    dest_ref[0:2, :] = start[0:2, :] + ri_ref[2:4, :]
    blk_start = lax.broadcasted_iota(jnp.int32, blk_ref.shape, 1) * MOE_BM
    be = jnp.zeros(blk_ref.shape, jnp.int32)
    for e in range(N_EXPERTS):
        be = be + jnp.where(pe_ref[e] <= blk_start, 1, 0)
    be = jnp.minimum(be, N_EXPERTS - 1)
    end = jnp.zeros(blk_ref.shape, jnp.int32)
    for e in range(N_EXPERTS):
        end = jnp.where(be == e, ps_ref[e] + cn_ref[e], end)
    valid = jnp.clip(end - blk_start, 0, MOE_BM)
    row = lax.broadcasted_iota(jnp.int32, blk_ref.shape, 0)
    blk_ref[...] = jnp.where(row == 0, be, valid)


def _tables_call(pad_start, pad_end, counts, ri, n_blocks):
    t = ri.shape[1]
    blk_w = -(-n_blocks // LANES) * LANES
    return pl.pallas_call(
        _tables_kernel,
        grid_spec=pltpu.PrefetchScalarGridSpec(
            num_scalar_prefetch=3,
            grid=(1,),
            in_specs=[pl.BlockSpec((SUBLANES, t), lambda i, *_: (0, 0))],
            out_specs=[pl.BlockSpec((SUBLANES, t), lambda i, *_: (0, 0)),
                       pl.BlockSpec((SUBLANES, blk_w), lambda i, *_: (0, 0))]),
        out_shape=[jax.ShapeDtypeStruct((SUBLANES, t), jnp.int32),
                   jax.ShapeDtypeStruct((SUBLANES, blk_w), jnp.int32)],
        compiler_params=pltpu.CompilerParams(dimension_semantics=("arbitrary",),
                                             vmem_limit_bytes=VMEM_LIMIT),
        name="tables",
    )(pad_start, pad_end, counts, ri)


def _expert_kernel(be_ref, nv_ref, x_ref, wg_ref, wu_ref, wd_ref, y_ref, wg_sc, wu_sc, wd_sc):
    i = pl.program_id(0)
    nv = nv_ref[i]

    @pl.when((i == 0) | (be_ref[i] != be_ref[jnp.maximum(i - 1, 0)]))
    def _():
        wg_sc[...] = wg_ref[0, 0].astype(BF16)
        wu_sc[...] = wu_ref[0, 0].astype(BF16)
        wd_sc[...] = wd_ref[0, 0].astype(BF16)

    @pl.when(nv > 0)
    def _():
        rowid = lax.broadcasted_iota(jnp.int32, x_ref.shape, 0)
        x_lo, x_hi = _unpack_bf16_pair(jnp.where(rowid < nv, x_ref[...], jnp.uint32(0)))
        x_lo = x_lo.astype(BF16)
        x_hi = x_hi.astype(BF16)
        g = (jnp.dot(x_lo, wg_sc[:HALF_D, :], preferred_element_type=F32)
             + jnp.dot(x_hi, wg_sc[HALF_D:, :], preferred_element_type=F32))
        up = (jnp.dot(x_lo, wu_sc[:HALF_D, :], preferred_element_type=F32)
              + jnp.dot(x_hi, wu_sc[HALF_D:, :], preferred_element_type=F32))
        act = g * _sigmoid(g) * up
        y = jnp.dot(act.astype(BF16), wd_sc[...], preferred_element_type=F32)
        y_ref[...] = _pack_bf16_pair(y[:, :HALF_D], y[:, HALF_D:])

    @pl.when(nv == 0)
    def _():
        y_ref[...] = jnp.zeros_like(y_ref)


def _expert_call(blk_e, blk_valid, xb, w_gate, w_up, w_down):
    r = xb.shape[0]
    bm = MOE_BM
    wspec = lambda a, b: pl.BlockSpec((1, 1, a, b), lambda i, be, nv: (0, be[i], 0, 0))
    return pl.pallas_call(
        _expert_kernel,
        grid_spec=pltpu.PrefetchScalarGridSpec(
            num_scalar_prefetch=2,
            grid=(r // bm,),
            in_specs=[pl.BlockSpec((bm, HALF_D), lambda i, be, nv: (i, 0)),
                      wspec(D_MODEL, D_EXPERT), wspec(D_MODEL, D_EXPERT), wspec(D_EXPERT, D_MODEL)],
            out_specs=pl.BlockSpec((bm, HALF_D), lambda i, be, nv: (i, 0)),
            scratch_shapes=[pltpu.VMEM((D_MODEL, D_EXPERT), BF16), pltpu.VMEM((D_MODEL, D_EXPERT), BF16),
                            pltpu.VMEM((D_EXPERT, D_MODEL), BF16)]),
        out_shape=jax.ShapeDtypeStruct((r, HALF_D), jnp.uint32),
        compiler_params=pltpu.CompilerParams(dimension_semantics=("arbitrary",),
                                             vmem_limit_bytes=VMEM_LIMIT),
        name="experts",
    )(blk_e, blk_valid, xb, w_gate, w_up, w_down)


def _sc_gather_rows(table, idx):
    m = idx.shape[0]
    width = table.shape[1]
    sc = plsc.get_sparse_core_info()
    workers = sc.num_cores * sc.num_subcores
    per_worker = m // workers
    mesh = plsc.VectorSubcoreMesh(core_axis_name="c", subcore_axis_name="s")

    n_win = per_worker // SC_WINDOW
    assert n_win % 2 == 0 and n_win >= 2
    buf = lambda: [pltpu.VMEM((SC_WINDOW,), jnp.int32), pltpu.VMEM((SC_WINDOW, width), table.dtype),
                   pltpu.SemaphoreType.DMA]

    @functools.partial(pl.kernel, mesh=mesh, out_type=jax.ShapeDtypeStruct((m, width), table.dtype),
                       scratch_types=buf() + buf())
    def gather(table_hbm, idx_hbm, out_hbm, idx_a, rows_a, sem_a, idx_b, rows_b, sem_b):
        base = (lax.axis_index("s") * sc.num_cores + lax.axis_index("c")) * per_worker
        bufs = ((idx_a, rows_a, sem_a), (idx_b, rows_b, sem_b))

        def fetch(win, which):
            idx_v, rows_v, sem = bufs[which]
            pltpu.sync_copy(idx_hbm.at[pl.ds(base + win * SC_WINDOW, SC_WINDOW)], idx_v)
            pltpu.async_copy(table_hbm.at[idx_v], rows_v, sem)

        def flush(win, which):
            idx_v, rows_v, sem = bufs[which]
            pltpu.make_async_copy(table_hbm.at[idx_v], rows_v, sem).wait()
            pltpu.sync_copy(rows_v, out_hbm.at[pl.ds(base + win * SC_WINDOW, SC_WINDOW)])

        fetch(0, 0)

        @pl.loop(0, n_win // 2 - 1)
        def _(pair):
            win = 2 * pair
            fetch(win + 1, 1)
            flush(win, 0)
            fetch(win + 2, 0)
            flush(win + 1, 1)

        fetch(n_win - 1, 1)
        flush(n_win - 2, 0)
        flush(n_win - 1, 1)

    return gather(table, idx)


def _sc_scatter_rows(rows, idx0, idx1, n_out):
    m, width = rows.shape
    sc = plsc.get_sparse_core_info()
    workers = sc.num_cores * sc.num_subcores
    per_worker = m // workers
    mesh = plsc.VectorSubcoreMesh(core_axis_name="c", subcore_axis_name="s")

    n_win = per_worker // SC_WINDOW
    assert n_win % 2 == 0 and n_win >= 2
    buf = lambda: [pltpu.VMEM((SC_WINDOW,), jnp.int32), pltpu.VMEM((SC_WINDOW,), jnp.int32),
                   pltpu.VMEM((SC_WINDOW, width), rows.dtype),
                   pltpu.SemaphoreType.DMA, pltpu.SemaphoreType.DMA, pltpu.SemaphoreType.DMA]

    @functools.partial(pl.kernel, mesh=mesh, out_type=jax.ShapeDtypeStruct((n_out, width), rows.dtype),
                       scratch_types=buf() + buf())
    def scatter(rows_hbm, idx0_hbm, idx1_hbm, out_hbm, *scratch):
        base = (lax.axis_index("s") * sc.num_cores + lax.axis_index("c")) * per_worker
        bufs = (scratch[:6], scratch[6:])

        def load(win, which):
            idx0_v, idx1_v, rows_v, lsem, _, _ = bufs[which]
            off = base + win * SC_WINDOW
            pltpu.sync_copy(idx0_hbm.at[pl.ds(off, SC_WINDOW)], idx0_v)
            pltpu.sync_copy(idx1_hbm.at[pl.ds(off, SC_WINDOW)], idx1_v)
            pltpu.async_copy(rows_hbm.at[pl.ds(off, SC_WINDOW)], rows_v, lsem)

        def send(which):
            idx0_v, idx1_v, rows_v, lsem, sem0, sem1 = bufs[which]
            pltpu.make_async_copy(rows_hbm.at[pl.ds(0, SC_WINDOW)], rows_v, lsem).wait()
            first = pltpu.async_copy(rows_v, out_hbm.at[idx0_v], sem0)
            second = pltpu.async_copy(rows_v, out_hbm.at[idx1_v], sem1)
            first.wait()
            second.wait()

        load(0, 0)

        @pl.loop(0, n_win // 2 - 1)
        def _(pair):
            win = 2 * pair
            load(win + 1, 1)
            send(0)
            load(win + 2, 0)
            send(1)

        load(n_win - 1, 1)
        send(0)
        send(1)

    return scatter(rows, idx0, idx1)


def _combine_rows_kernel(h_ref, w_ref, g_ref, b_ref, y0_ref, y1_ref, *rest):
    o_ref = rest[-1]
    w = w_ref[...]
    y0_lo, y0_hi = _unpack_bf16_pair(y0_ref[...])
    y1_lo, y1_hi = _unpack_bf16_pair(y1_ref[...])
    y = jnp.concatenate([w[:, 0:1] * y0_lo + w[:, 1:2] * y1_lo, w[:, 0:1] * y0_hi + w[:, 1:2] * y1_hi], axis=1)
    o_ref[...] = _ln(DEEPNORM_ALPHA * h_ref[...] + y, g_ref[...], b_ref[...])


def _combine_rows_call(h1, rw, g, b, yg, row0, t_all, prev=None):
    t = h1.shape[0]
    tm = COMB_TM
    n = t // tm
    first = row0 // tm
    in_specs = [pl.BlockSpec((tm, D_MODEL), lambda i: (i, 0)),
                pl.BlockSpec((tm, 8), lambda i: (i, 0)),
                pl.BlockSpec((1, D_MODEL), lambda i: (0, 0)),
                pl.BlockSpec((1, D_MODEL), lambda i: (0, 0)),
                pl.BlockSpec((tm, HALF_D), lambda i: (i, 0)),
                pl.BlockSpec((tm, HALF_D), lambda i: (n + i, 0))]
    args = [h1, rw, g, b, yg, yg]
    aliases = {}
    if prev is not None:
        in_specs.append(pl.BlockSpec(memory_space=pl.ANY))
        args.append(prev)
        aliases = {len(args) - 1: 0}
    return pl.pallas_call(
        _combine_rows_kernel,
        grid=(n,),
        in_specs=in_specs,
        out_specs=pl.BlockSpec((tm, D_MODEL), lambda i: (first + i, 0)),
        out_shape=jax.ShapeDtypeStruct((t_all, D_MODEL), F32),
        input_output_aliases=aliases,
        compiler_params=pltpu.CompilerParams(dimension_semantics=("arbitrary",),
                                             vmem_limit_bytes=VMEM_LIMIT),
        name="combine_rows",
    )(*args)


def _take_cols(w, idx):
    wz = jnp.concatenate([w, jnp.zeros((w.shape[0], 1), w.dtype)], axis=1)
    return wz[:, np.where(idx < 0, w.shape[1], idx)]


def _layout_indices():
    lane_src = np.full((HEAD_PAD,), -1, np.int64)
    lane_src[0:ROPE_HALF] = QK_NOPE + np.arange(ROPE_HALF)
    lane_src[ROPE_HALF:X2_LANE] = np.arange(X2_LANE - ROPE_HALF)
    lane_src[X2_LANE:X2_LANE + ROPE_HALF] = QK_NOPE + ROPE_HALF + np.arange(ROPE_HALF)
    n_lo = X2_LANE - ROPE_HALF
    lane_src[X2_LANE + ROPE_HALF:X2_LANE + ROPE_HALF + QK_NOPE - n_lo] = n_lo + np.arange(QK_NOPE - n_lo)
    q_idx = np.concatenate([np.where(lane_src >= 0, lane_src + (QK_NOPE + QK_ROPE) * h, -1)
                            for h in range(N_HEADS)])
    nope_src = np.where(lane_src < QK_NOPE, lane_src, -1)
    k_idx = np.concatenate([np.where(nope_src >= 0, nope_src + (QK_NOPE + V_DIM) * h, -1)
                            for h in range(N_HEADS)])
    v_idx = np.concatenate([QK_NOPE + np.arange(V_DIM) + (QK_NOPE + V_DIM) * h for h in range(N_HEADS)])
    kr_idx = np.full((LANES,), -1, np.int64)
    kr_idx[0:ROPE_HALF] = np.arange(ROPE_HALF)
    kr_idx[X2_LANE:X2_LANE + ROPE_HALF] = ROPE_HALF + np.arange(ROPE_HALF)
    return q_idx, k_idx, v_idx, kr_idx


def _rope_lane_tables(length):
    inv_freq = np.float32(ROPE_THETA) ** (-np.arange(ROPE_HALF, dtype=np.float32) / np.float32(ROPE_HALF))
    ang = np.arange(length, dtype=np.float32)[:, None] * inv_freq[None, :].astype(np.float32)
    cos, sin = np.cos(ang).astype(np.float32), np.sin(ang).astype(np.float32)
    ctab = np.ones((length, LANES), np.float32)
    stab = np.zeros((length, LANES), np.float32)
    for lo, sign in ((0, -1.0), (X2_LANE, 1.0)):
        ctab[:, lo:lo + ROPE_HALF] = cos
        stab[:, lo:lo + ROPE_HALF] = sign * sin
    return jnp.asarray(ctab), jnp.asarray(stab)


def kernel(x, meta_tokens, ln_in_g, ln_in_b, w_in, conv_w, conv_b, conv_ln_g, conv_ln_b, q_norm_g, w_uq,
           kv_norm_g, w_ukv, conv_out_g, attn_out_g, w_o, ln1_g, ln1_b, w_rg, b_rg, w_re, b_re,
           w_gate, w_up, w_down, ln2_g, ln2_b):
    assert w_in.shape[0] == 1, "single layer"
    bsz, seq, d = x.shape
    t = bsz * seq
    row = lambda a: a.reshape(1, -1)

    q_idx, k_idx, v_idx, kr_idx = _layout_indices()
    c4 = 2 * C_CONV + Q_LORA + KV_LORA
    w_in_l = jnp.concatenate([w_in[0][:, :c4], _take_cols(w_in[0][:, c4:], kr_idx)], axis=1).astype(BF16)
    wuq_l = _take_cols(w_uq[0], q_idx).astype(BF16)
    wuk_l = _take_cols(w_ukv[0], k_idx).astype(BF16)
    wvt_l = w_ukv[0][:, v_idx].T.astype(BF16)
    wo_l = w_o[0].astype(BF16)
    n_route_pad = ROUTE_ROWS - N_EXPERTS - N_GROUPS
    wr_t = jnp.concatenate([w_re[0].T, w_rg[0].T, jnp.zeros((n_route_pad, d), F32)], axis=0)
    wrh_l = wr_t.astype(BF16)
    wrl_l = (wr_t - wrh_l.astype(F32)).astype(BF16)
    br_l = jnp.broadcast_to(jnp.concatenate([b_re[0], b_rg[0], jnp.zeros((n_route_pad,), F32)])[:, None],
                            (ROUTE_ROWS, LANES))
    ctab, stab = _rope_lane_tables(N_META + seq)

    proj_w = (row(ln_in_g), row(ln_in_b), w_in_l, row(q_norm_g[0]), wuq_l, row(kv_norm_g[0]), wuk_l, wvt_l)
    u_m, _, k_m, vt_m = _proj_call(meta_tokens.astype(F32), N_META, 1, *proj_w, ctab[:N_META], stab[:N_META])
    u, q, k, vt = _proj_call(x.reshape(t, d), PROJ_TM, seq // PROJ_TM, *proj_w, ctab[N_META:], stab[N_META:])

    o = _attn_call(q.reshape(bsz, seq, -1), k.reshape(bsz, seq, -1),
                   vt.reshape(bsz, seq // ATT_T, D_ATTN, ATT_T), k_m, vt_m)

    pre = jnp.concatenate([jnp.zeros((HALO - N_META, C_CONV), F32), u_m], axis=0)
    cw = jnp.concatenate([conv_w[0], jnp.zeros((HALO - CONV_K, C_CONV), F32)], axis=0)
    u3 = u.reshape(bsz, seq, C_CONV)
    pb = bsz // MOE_PARTS
    tp = pb * seq
    n_blocks = (2 * tp) // MOE_BM + N_EXPERTS
    out = None
    for part in range(MOE_PARTS):
        h1, hp, ri, rw, cnt = _mix_call(part * pb, pb, x, row(ln_in_g), row(ln_in_b), u3, pre, cw,
                                        row(conv_b[0]), row(conv_ln_g[0]), row(conv_ln_b[0]),
                                        row(conv_out_g[0]), o, row(attn_out_g[0]), wo_l, row(ln1_g[0]),
                                        row(ln1_b[0]), wrh_l, wrl_l, br_l)
        counts = cnt[:N_EXPERTS, 0].astype(jnp.int32)
        padded = (counts + MOE_BM - 1) // MOE_BM * MOE_BM
        pad_end = jnp.cumsum(padded)
        pad_start = pad_end - padded
        dest, blk = _tables_call(pad_start, pad_end, counts, ri, n_blocks)
        blk_e, blk_valid = blk[0, :n_blocks], blk[1, :n_blocks]
        dest0, dest1 = dest[0], dest[1]
        xb = _sc_scatter_rows(hp.reshape(tp, HALF_D), dest0, dest1, n_blocks * MOE_BM)
        yb = _expert_call(blk_e, blk_valid, xb, w_gate, w_up, w_down)
        yg = _sc_gather_rows(yb, jnp.concatenate([dest0, dest1]))
        out = _combine_rows_call(h1.reshape(tp, d), rw.reshape(tp, 8), row(ln2_g[0]), row(ln2_b[0]), yg,
                                 part * tp, t, prev=out)
    return out.reshape(bsz, seq, d)
```

```python
import functools
import math

import numpy as np
import jax
import jax.numpy as jnp
from jax import lax
from jax.experimental import pallas as pl
from jax.experimental.pallas import tpu as pltpu
from jax.experimental.pallas import tpu_sc as plsc

F32 = jnp.float32
BF16 = jnp.bfloat16

D_MODEL = 1024
HALF_D = D_MODEL // 2
N_META = 16
CHUNK = 64
C_CONV = 512
CONV_K = 31
N_HEADS = 8
QK_NOPE = 64
QK_ROPE = 32
V_DIM = 64
Q_LORA = 384
KV_LORA = 256
ROPE_THETA = 10000.0
D_ATTN = N_HEADS * V_DIM
N_GROUPS = 4
EXPERTS_PER_GROUP = 8
N_EXPERTS = N_GROUPS * EXPERTS_PER_GROUP
D_EXPERT = 256
DEEPNORM_ALPHA = 2.0 ** 0.25
EPS = 1e-5
NEG_INF = -1e30

LANES = 128
SUBLANES = 8
HEAD_PAD = LANES
ROPE_HALF = QK_ROPE // 2
X2_LANE = 64
Z_COLS = 2 * C_CONV + Q_LORA + KV_LORA + LANES
Q_SCALE = (QK_NOPE + QK_ROPE) ** -0.5 * math.log2(math.e)

PROJ_TM = 1024
ATT_T = 256
ATT_HEADS = 8
MIX_TM = 512
ROUTE_ROWS = 48
HALO = 32
MOE_BM = 512
COMB_TM = 1024
SC_WINDOW = 64
VMEM_LIMIT = 56 * 1024 * 1024


def _ln(x, g, b):
    mu = jnp.mean(x, axis=-1, keepdims=True)
    xc = x - mu
    var = jnp.mean(xc * xc, axis=-1, keepdims=True)
    return xc * lax.rsqrt(var + EPS) * g + b


def _rms(x, g):
    return x * lax.rsqrt(jnp.mean(x * x, axis=-1, keepdims=True) + EPS) * g


def _sigmoid(x):
    return 1.0 / (1.0 + jnp.exp(-x))


def _pack_bf16_pair(lo, hi):
    lo_bits = lax.bitcast_convert_type(lo.astype(BF16).astype(F32), jnp.uint32)
    hi_bits = lax.bitcast_convert_type(hi.astype(BF16).astype(F32), jnp.uint32)
    return (lo_bits >> 16) | (hi_bits & jnp.uint32(0xFFFF0000))


def _unpack_bf16_pair(packed):
    lo = lax.bitcast_convert_type(packed << 16, F32)
    hi = lax.bitcast_convert_type(packed & jnp.uint32(0xFFFF0000), F32)
    return lo, hi


def _proj_kernel(x_ref, g_ref, b_ref, win_ref, qg_ref, wuq_ref, kvg_ref, wuk_ref, wvt_ref, c_ref, s_ref,
                 h_ref, u_ref, q_ref, k_ref, vt_ref):
    n_part, part = vt_ref.shape[1], vt_ref.shape[3]
    zs = []
    for r in range(n_part):
        h = _ln(x_ref[r * part:(r + 1) * part, :], g_ref[...], b_ref[...])
        h_ref[r * part:(r + 1) * part, :] = h
        zs.append(jnp.dot(h.astype(BF16), win_ref[...], preferred_element_type=F32))
    c0 = 2 * C_CONV
    for r, z in enumerate(zs):
        rows = slice(r * part, (r + 1) * part)
        u_ref[rows, :] = z[:, :C_CONV] * _sigmoid(z[:, C_CONV:c0])
        cq = z[:, c0:c0 + Q_LORA]
        ckv = z[:, c0 + Q_LORA:c0 + Q_LORA + KV_LORA]
        kr = z[:, c0 + Q_LORA + KV_LORA:]
        cs = c_ref[rows, :]
        sn = s_ref[rows, :]
        q = jnp.dot(_rms(cq, qg_ref[...]).astype(BF16), wuq_ref[...], preferred_element_type=F32)
        kvn = _rms(ckv, kvg_ref[...]).astype(BF16)
        kn = jnp.dot(kvn, wuk_ref[...], preferred_element_type=F32)
        kr_rot = kr * cs + pltpu.roll(kr, X2_LANE, 1) * sn
        for hh in range(N_HEADS):
            sl = slice(HEAD_PAD * hh, HEAD_PAD * (hh + 1))
            qh = q[:, sl]
            qh = (qh * cs + pltpu.roll(qh, X2_LANE, 1) * sn) * Q_SCALE
            q_ref[rows, sl] = qh.astype(BF16)
            k_ref[rows, sl] = (kn[:, sl] + kr_rot).astype(BF16)
        vt = lax.dot_general(wvt_ref[...], kvn, (((1,), (1,)), ((), ())), preferred_element_type=F32)
        vt_ref[0, r] = vt.astype(BF16)


def _proj_call(x2d, tm, pos_blocks, ln_g, ln_b, w_in, qg, wuq, kvg, wuk, wvt, ctab, stab):
    t = x2d.shape[0]
    sub_w = min(tm, ATT_T)
    n_sub = tm // sub_w
    full = lambda shape: pl.BlockSpec(shape, lambda i: (0,) * len(shape))
    row = lambda w: pl.BlockSpec((tm, w), lambda i: (i, 0))
    tab = pl.BlockSpec((tm, LANES), lambda i: (i % pos_blocks, 0))
    return pl.pallas_call(
        _proj_kernel,
        grid=(t // tm,),
        in_specs=[row(D_MODEL), full((1, D_MODEL)), full((1, D_MODEL)), full((D_MODEL, Z_COLS)),
                  full((1, Q_LORA)), full((Q_LORA, N_HEADS * HEAD_PAD)),
                  full((1, KV_LORA)), full((KV_LORA, N_HEADS * HEAD_PAD)), full((D_ATTN, KV_LORA)), tab, tab],
        out_specs=[row(D_MODEL), row(C_CONV), row(N_HEADS * HEAD_PAD), row(N_HEADS * HEAD_PAD),
                   pl.BlockSpec((1, n_sub, D_ATTN, sub_w), lambda i: (i, 0, 0, 0))],
        out_shape=[jax.ShapeDtypeStruct((t, D_MODEL), F32),
                   jax.ShapeDtypeStruct((t, C_CONV), F32),
                   jax.ShapeDtypeStruct((t, N_HEADS * HEAD_PAD), BF16),
                   jax.ShapeDtypeStruct((t, N_HEADS * HEAD_PAD), BF16),
                   jax.ShapeDtypeStruct((t // tm, n_sub, D_ATTN, sub_w), BF16)],
        compiler_params=pltpu.CompilerParams(dimension_semantics=("arbitrary",),
                                             vmem_limit_bytes=VMEM_LIMIT),
        name="proj",
    )(x2d, ln_g, ln_b, w_in, qg, wuq, kvg, wuk, wvt, ctab, stab)


def _attn_kernel(q_ref, k_ref, vt_ref, km_ref, vtm_ref, o_ref, s_sc):
    seq = q_ref.shape[1]
    nq = seq // ATT_T
    key_chunk = lax.broadcasted_iota(jnp.int32, (ATT_T, ATT_T), 0) // CHUNK
    qry_chunk = lax.broadcasted_iota(jnp.int32, (ATT_T, ATT_T), 1) // CHUNK
    diag_visible = key_chunk <= qry_chunk
    nt = (((1,), (1,)), ((), ()))
    heads = range(ATT_HEADS)
    hsl = [slice(HEAD_PAD * hh, HEAD_PAD * (hh + 1)) for hh in heads]
    vsl = [slice(V_DIM * hh, V_DIM * (hh + 1)) for hh in heads]

    def update(s, vt, m, l, acc):
        m_new = jnp.maximum(m, jnp.max(s, axis=0, keepdims=True))
        alpha = jnp.exp2(m - m_new)
        p = jnp.exp2(s - m_new)
        l = alpha * l + jnp.sum(p, axis=0, keepdims=True)
        acc = alpha * acc + jnp.dot(vt, p.astype(BF16), preferred_element_type=F32)
        return m_new, l, acc

    def scores(qi, j, hh):
        q0 = pl.multiple_of(qi * ATT_T, ATT_T)
        k0 = pl.multiple_of(j * ATT_T, ATT_T)
        return lax.dot_general(k_ref[0, pl.ds(k0, ATT_T), hsl[hh]], q_ref[0, pl.ds(q0, ATT_T), hsl[hh]], nt,
                               preferred_element_type=F32)

    for hh in heads:
        s_sc[0, hh] = scores(0, 0, hh)

    def q_body(qi, carry):
        row0 = pl.multiple_of(qi * ATT_T, ATT_T)
        slot = qi % 2
        state = []
        for hh in heads:
            state += [jnp.full((1, ATT_T), NEG_INF, F32), jnp.zeros((1, ATT_T), F32),
                      jnp.zeros((V_DIM, ATT_T), F32)]

        def kv_body(j, st):
            out = []
            s_next = scores(qi, j + 1, 0)
            for hh in heads:
                s_cur = s_sc[slot, hh]
                s_after = scores(qi, j + 1, hh + 1) if hh + 1 < ATT_HEADS else None
                out += update(s_cur, vt_ref[0, j, vsl[hh], :], *st[3 * hh:3 * hh + 3])
                s_sc[slot, hh] = s_next
                s_next = s_after
            return tuple(out)

        state = lax.fori_loop(0, qi, kv_body, tuple(state))
        q_next = jnp.minimum(qi + 1, nq - 1)
        sms = [lax.dot_general(km_ref[:, hsl[hh]], q_ref[0, pl.ds(row0, ATT_T), hsl[hh]], nt,
                               preferred_element_type=F32) for hh in heads]
        outs = []
        for hh in heads:
            m, l, acc = state[3 * hh:3 * hh + 3]
            s = jnp.where(diag_visible, s_sc[slot, hh], NEG_INF)
            sm = sms[hh]
            m_new = jnp.maximum(m, jnp.maximum(jnp.max(s, axis=0, keepdims=True),
                                               jnp.max(sm, axis=0, keepdims=True)))
            alpha = jnp.exp2(m - m_new)
            p = jnp.exp2(s - m_new)
            pm = jnp.exp2(sm - m_new)
            l = alpha * l + jnp.sum(p, axis=0, keepdims=True) + jnp.sum(pm, axis=0, keepdims=True)
            acc = (alpha * acc + jnp.dot(vt_ref[0, qi, vsl[hh], :], p.astype(BF16), preferred_element_type=F32)
                   + jnp.dot(vtm_ref[0, 0, vsl[hh], :], pm.astype(BF16), preferred_element_type=F32))
            outs.append(acc / l)
        for hh in heads:
            s_sc[1 - slot, hh] = scores(q_next, 0, hh)
        o_ref[0, pl.ds(row0, ATT_T), :] = jnp.concatenate(outs, axis=0).T
        return carry

    lax.fori_loop(0, nq, q_body, 0)


def _attn_call(q, k, vt, k_meta, vt_meta):
    b, seq, _ = q.shape
    nk = seq // ATT_T
    hb = ATT_HEADS
    return pl.pallas_call(
        _attn_kernel,
        grid=(b, N_HEADS // hb),
        in_specs=[pl.BlockSpec((1, seq, hb * HEAD_PAD), lambda i, j: (i, 0, j)),
                  pl.BlockSpec((1, seq, hb * HEAD_PAD), lambda i, j: (i, 0, j)),
                  pl.BlockSpec((1, nk, hb * V_DIM, ATT_T), lambda i, j: (i, 0, j, 0)),
                  pl.BlockSpec((N_META, hb * HEAD_PAD), lambda i, j: (0, j)),
                  pl.BlockSpec((1, 1, hb * V_DIM, N_META), lambda i, j: (0, 0, j, 0))],
        out_specs=pl.BlockSpec((1, seq, hb * V_DIM), lambda i, j: (i, 0, j)),
        out_shape=jax.ShapeDtypeStruct((b, seq, D_ATTN), F32),
        scratch_shapes=[pltpu.VMEM((2, hb, ATT_T, ATT_T), F32)],
        compiler_params=pltpu.CompilerParams(dimension_semantics=("arbitrary", "arbitrary"),
                                             vmem_limit_bytes=VMEM_LIMIT),
        name="attn",
    )(q, k, vt, k_meta, vt_meta)


def _mix_kernel(h_ref, u_ref, uh_ref, pre_ref, cw_ref, cb_ref, clg_ref, clb_ref,
                cog_ref, o_ref, aog_ref, wo_ref, l1g_ref, l1b_ref, wrh_ref, wrl_ref, br_ref, tri_ref,
                h1_ref, hp_ref, ri_ref, rw_ref, cnt_ref, win_sc, sh_sc, carry_sc):
    bi = pl.program_id(0)
    ti = pl.program_id(1)
    tm = MIX_TM

    @pl.when((bi == 0) & (ti == 0))
    def _():
        carry_sc[...] = jnp.zeros_like(carry_sc)

    win_sc[0:HALO, :] = jnp.where(ti == 0, pre_ref[...], uh_ref[0])
    win_sc[HALO:, :] = u_ref[0]
    first = HALO - (CONV_K - 1)
    for res in range(1, SUBLANES):
        rows = tm + HALO - SUBLANES
        sh_sc[res - 1, 0:rows, :] = win_sc[pl.ds(res, rows), :]

    acc = jnp.broadcast_to(cb_ref[...], (tm, C_CONV))
    for o in range(first, first + CONV_K):
        res, lo = o % SUBLANES, o - o % SUBLANES
        src = sh_sc[res - 1, lo:lo + tm, :] if res else win_sc[lo:lo + tm, :]
        acc = acc + src * cw_ref[o - first:o - first + 1, :]
    c = _ln(acc, clg_ref[...], clb_ref[...])
    c = c * _sigmoid(c)
    cn = _rms(c, cog_ref[...])
    on = _rms(o_ref[0], aog_ref[...])
    y = (jnp.dot(cn.astype(BF16), wo_ref[0:C_CONV, :], preferred_element_type=F32)
         + jnp.dot(on.astype(BF16), wo_ref[C_CONV:, :], preferred_element_type=F32))
    h1 = _ln(DEEPNORM_ALPHA * h_ref[0] + y, l1g_ref[...], l1b_ref[...])
    h1_ref[0] = h1
    hp_ref[0] = _pack_bf16_pair(h1[:, :HALF_D], h1[:, HALF_D:])

    nt = (((1,), (1,)), ((), ()))
    h_hi = h1.astype(BF16)
    h_lo = (h1 - h_hi.astype(F32)).astype(BF16)
    lt = (lax.dot_general(wrh_ref[...], h_hi, nt, preferred_element_type=F32)
          + lax.dot_general(wrl_ref[...], h_hi, nt, preferred_element_type=F32)
          + lax.dot_general(wrh_ref[...], h_lo, nt, preferred_element_type=F32)) + br_ref[:, 0:1]
    row = lax.broadcasted_iota(jnp.int32, (ROUTE_ROWS, tm), 0)
    big = jnp.int32(1 << 20)
    is_g = (row >= N_EXPERTS) & (row < N_EXPERTS + N_GROUPS)
    gl = jnp.where(is_g, lt, -jnp.inf)
    gmax = jnp.max(gl, axis=0, keepdims=True)
    g_idx = jnp.min(jnp.where(gl == gmax, row - N_EXPERTS, big), axis=0, keepdims=True)
    g_w = 1.0 / jnp.sum(jnp.exp(gl - gmax), axis=0, keepdims=True)
    in_grp = (row < N_EXPERTS) & ((row // EXPERTS_PER_GROUP) == g_idx)
    el = jnp.where(in_grp, lt, -jnp.inf)
    v1 = jnp.max(el, axis=0, keepdims=True)
    i1 = jnp.min(jnp.where(el == v1, row, big), axis=0, keepdims=True)
    el2 = jnp.where(row == i1, -jnp.inf, el)
    v2 = jnp.max(el2, axis=0, keepdims=True)
    i2 = jnp.min(jnp.where(el2 == v2, row, big), axis=0, keepdims=True)
    e2 = jnp.exp(v2 - v1)
    w1 = g_w / (1.0 + e2)
    w2 = g_w * e2 / (1.0 + e2)

    oh1 = row == i1
    oh2 = row == i2
    oh = jnp.where(oh1 | oh2, 1.0, 0.0)
    before = jnp.dot(oh.astype(BF16), tri_ref[...], preferred_element_type=F32) + carry_sc[:, 0:1]
    r1 = jnp.sum(jnp.where(oh1, before, 0.0), axis=0, keepdims=True).astype(jnp.int32)
    r2 = jnp.sum(jnp.where(oh2, before, 0.0), axis=0, keepdims=True).astype(jnp.int32)
    carry_sc[...] = carry_sc[...] + jnp.sum(oh, axis=1, keepdims=True)
    cnt_ref[...] = carry_sc[...]

    sub = lax.broadcasted_iota(jnp.int32, (SUBLANES, tm), 0)
    ri_ref[...] = jnp.where(sub == 0, i1, jnp.where(sub == 1, i2, jnp.where(sub == 2, r1,
                            jnp.where(sub == 3, r2, 0))))
    lrow = lax.broadcasted_iota(jnp.int32, (LANES, tm), 0)
    rw_ref[0] = jnp.where(lrow == 0, w1, jnp.where(lrow == 1, w2, 0.0)).T[:, :8]


def _mix_call(h, u, pre, cw, cb, clg, clb, cog, o, aog, wo, l1g, l1b, wrh, wrl, br):
    b, seq, _ = h.shape
    tm = MIX_TM
    nt = seq // tm
    full = lambda shape: pl.BlockSpec(shape, lambda i, j: (0,) * len(shape))
    tile = lambda w: pl.BlockSpec((1, tm, w), lambda i, j: (i, j, 0))
    halo = pl.BlockSpec((1, HALO, C_CONV), lambda i, j: (i, jnp.maximum(j * (tm // HALO) - 1, 0), 0))
    earlier = jnp.asarray(np.triu(np.ones((tm, tm), np.float32), k=1), BF16)
    return pl.pallas_call(
        _mix_kernel,
        grid=(b, nt),
        in_specs=[tile(D_MODEL),
                  tile(C_CONV), halo, full((HALO, C_CONV)), full((HALO, C_CONV)), full((1, C_CONV)),
                  full((1, C_CONV)), full((1, C_CONV)), full((1, C_CONV)),
                  tile(D_ATTN), full((1, D_ATTN)), full((D_MODEL, D_MODEL)),
                  full((1, D_MODEL)), full((1, D_MODEL)), full((ROUTE_ROWS, D_MODEL)),
                  full((ROUTE_ROWS, D_MODEL)), full((ROUTE_ROWS, LANES)), full((tm, tm))],
        out_specs=[tile(D_MODEL), tile(HALF_D), pl.BlockSpec((8, tm), lambda i, j: (0, i * nt + j)), tile(8),
                   full((ROUTE_ROWS, LANES))],
        out_shape=[jax.ShapeDtypeStruct((b, seq, D_MODEL), F32),
                   jax.ShapeDtypeStruct((b, seq, HALF_D), jnp.uint32),
                   jax.ShapeDtypeStruct((8, b * seq), jnp.int32),
                   jax.ShapeDtypeStruct((b, seq, 8), F32),
                   jax.ShapeDtypeStruct((ROUTE_ROWS, LANES), F32)],
        scratch_shapes=[pltpu.VMEM((HALO + tm, C_CONV), F32),
                        pltpu.VMEM((SUBLANES - 1, HALO + tm, C_CONV), F32),
                        pltpu.VMEM((ROUTE_ROWS, LANES), F32)],
        compiler_params=pltpu.CompilerParams(dimension_semantics=("arbitrary", "arbitrary"),
                                             vmem_limit_bytes=VMEM_LIMIT),
        name="mix",
    )(h, u, u, pre, cw, cb, clg, clb, cog, o, aog, wo, l1g, l1b, wrh, wrl, br, earlier)


def _tables_kernel(ps_ref, pe_ref, cn_ref, ri_ref, dest_ref, blk_ref):
    ri = ri_ref[...]
    start = jnp.zeros_like(ri)
    for e in range(N_EXPERTS):
        start = jnp.where(ri == e, ps_ref[e], start)
    dest_ref[...] = jnp.zeros_like(dest_ref)
    dest_ref[0:2, :] = start[0:2, :] + ri_ref[2:4, :]
    blk_start = lax.broadcasted_iota(jnp.int32, blk_ref.shape, 1) * MOE_BM
    be = jnp.zeros(blk_ref.shape, jnp.int32)
    for e in range(N_EXPERTS):
        be = be + jnp.where(pe_ref[e] <= blk_start, 1, 0)
    be = jnp.minimum(be, N_EXPERTS - 1)
    end = jnp.zeros(blk_ref.shape, jnp.int32)
    for e in range(N_EXPERTS):
        end = jnp.where(be == e, ps_ref[e] + cn_ref[e], end)
    valid = jnp.clip(end - blk_start, 0, MOE_BM)
    row = lax.broadcasted_iota(jnp.int32, blk_ref.shape, 0)
    blk_ref[...] = jnp.where(row == 0, be, valid)


def _tables_call(pad_start, pad_end, counts, ri, n_blocks):
    t = ri.shape[1]
    blk_w = -(-n_blocks // LANES) * LANES
    return pl.pallas_call(
        _tables_kernel,
        grid_spec=pltpu.PrefetchScalarGridSpec(
            num_scalar_prefetch=3,
            grid=(1,),
            in_specs=[pl.BlockSpec((SUBLANES, t), lambda i, *_: (0, 0))],
            out_specs=[pl.BlockSpec((SUBLANES, t), lambda i, *_: (0, 0)),
                       pl.BlockSpec((SUBLANES, blk_w), lambda i, *_: (0, 0))]),
        out_shape=[jax.ShapeDtypeStruct((SUBLANES, t), jnp.int32),
                   jax.ShapeDtypeStruct((SUBLANES, blk_w), jnp.int32)],
        compiler_params=pltpu.CompilerParams(dimension_semantics=("arbitrary",),
                                             vmem_limit_bytes=VMEM_LIMIT),
        name="tables",
    )(pad_start, pad_end, counts, ri)


def _expert_kernel(be_ref, nv_ref, x_ref, wg_ref, wu_ref, wd_ref, y_ref, wg_sc, wu_sc, wd_sc):
    i = pl.program_id(0)
    nv = nv_ref[i]

    @pl.when((i == 0) | (be_ref[i] != be_ref[jnp.maximum(i - 1, 0)]))
    def _():
        wg_sc[...] = wg_ref[0, 0].astype(BF16)
        wu_sc[...] = wu_ref[0, 0].astype(BF16)
        wd_sc[...] = wd_ref[0, 0].astype(BF16)

    @pl.when(nv > 0)
    def _():
        rowid = lax.broadcasted_iota(jnp.int32, x_ref.shape, 0)
        x_lo, x_hi = _unpack_bf16_pair(jnp.where(rowid < nv, x_ref[...], jnp.uint32(0)))
        x_lo = x_lo.astype(BF16)
        x_hi = x_hi.astype(BF16)
        g = (jnp.dot(x_lo, wg_sc[:HALF_D, :], preferred_element_type=F32)
             + jnp.dot(x_hi, wg_sc[HALF_D:, :], preferred_element_type=F32))
        up = (jnp.dot(x_lo, wu_sc[:HALF_D, :], preferred_element_type=F32)
              + jnp.dot(x_hi, wu_sc[HALF_D:, :], preferred_element_type=F32))
        act = g * _sigmoid(g) * up
        y = jnp.dot(act.astype(BF16), wd_sc[...], preferred_element_type=F32)
        y_ref[...] = _pack_bf16_pair(y[:, :HALF_D], y[:, HALF_D:])

    @pl.when(nv == 0)
    def _():
        y_ref[...] = jnp.zeros_like(y_ref)


def _expert_call(blk_e, blk_valid, xb, w_gate, w_up, w_down):
    r = xb.shape[0]
    bm = MOE_BM
    wspec = lambda a, b: pl.BlockSpec((1, 1, a, b), lambda i, be, nv: (0, be[i], 0, 0))
    return pl.pallas_call(
        _expert_kernel,
        grid_spec=pltpu.PrefetchScalarGridSpec(
            num_scalar_prefetch=2,
            grid=(r // bm,),
            in_specs=[pl.BlockSpec((bm, HALF_D), lambda i, be, nv: (i, 0)),
                      wspec(D_MODEL, D_EXPERT), wspec(D_MODEL, D_EXPERT), wspec(D_EXPERT, D_MODEL)],
            out_specs=pl.BlockSpec((bm, HALF_D), lambda i, be, nv: (i, 0)),
            scratch_shapes=[pltpu.VMEM((D_MODEL, D_EXPERT), BF16), pltpu.VMEM((D_MODEL, D_EXPERT), BF16),
                            pltpu.VMEM((D_EXPERT, D_MODEL), BF16)]),
        out_shape=jax.ShapeDtypeStruct((r, HALF_D), jnp.uint32),
        compiler_params=pltpu.CompilerParams(dimension_semantics=("arbitrary",),
                                             vmem_limit_bytes=VMEM_LIMIT),
        name="experts",
    )(blk_e, blk_valid, xb, w_gate, w_up, w_down)


def _sc_gather_rows(table, idx):
    m = idx.shape[0]
    width = table.shape[1]
    sc = plsc.get_sparse_core_info()
    workers = sc.num_cores * sc.num_subcores
    per_worker = m // workers
    mesh = plsc.VectorSubcoreMesh(core_axis_name="c", subcore_axis_name="s")

    n_win = per_worker // SC_WINDOW
    assert n_win % 2 == 0 and n_win >= 2
    buf = lambda: [pltpu.VMEM((SC_WINDOW,), jnp.int32), pltpu.VMEM((SC_WINDOW, width), table.dtype),
                   pltpu.SemaphoreType.DMA]

    @functools.partial(pl.kernel, mesh=mesh, out_type=jax.ShapeDtypeStruct((m, width), table.dtype),
                       scratch_types=buf() + buf())
    def gather(table_hbm, idx_hbm, out_hbm, idx_a, rows_a, sem_a, idx_b, rows_b, sem_b):
        base = (lax.axis_index("s") * sc.num_cores + lax.axis_index("c")) * per_worker
        bufs = ((idx_a, rows_a, sem_a), (idx_b, rows_b, sem_b))

        def fetch(win, which):
            idx_v, rows_v, sem = bufs[which]
            pltpu.sync_copy(idx_hbm.at[pl.ds(base + win * SC_WINDOW, SC_WINDOW)], idx_v)
            pltpu.async_copy(table_hbm.at[idx_v], rows_v, sem)

        def flush(win, which):
            idx_v, rows_v, sem = bufs[which]
            pltpu.make_async_copy(table_hbm.at[idx_v], rows_v, sem).wait()
            pltpu.sync_copy(rows_v, out_hbm.at[pl.ds(base + win * SC_WINDOW, SC_WINDOW)])

        fetch(0, 0)

        @pl.loop(0, n_win // 2 - 1)
        def _(pair):
            win = 2 * pair
            fetch(win + 1, 1)
            flush(win, 0)
            fetch(win + 2, 0)
            flush(win + 1, 1)

        fetch(n_win - 1, 1)
        flush(n_win - 2, 0)
        flush(n_win - 1, 1)

    return gather(table, idx)


def _sc_scatter_rows(rows, idx0, idx1, n_out):
    m, width = rows.shape
    sc = plsc.get_sparse_core_info()
    workers = sc.num_cores * sc.num_subcores
    per_worker = m // workers
    mesh = plsc.VectorSubcoreMesh(core_axis_name="c", subcore_axis_name="s")

    n_win = per_worker // SC_WINDOW
    assert n_win % 2 == 0 and n_win >= 2
    buf = lambda: [pltpu.VMEM((SC_WINDOW,), jnp.int32), pltpu.VMEM((SC_WINDOW,), jnp.int32),
                   pltpu.VMEM((SC_WINDOW, width), rows.dtype),
                   pltpu.SemaphoreType.DMA, pltpu.SemaphoreType.DMA, pltpu.SemaphoreType.DMA]

    @functools.partial(pl.kernel, mesh=mesh, out_type=jax.ShapeDtypeStruct((n_out, width), rows.dtype),
                       scratch_types=buf() + buf())
    def scatter(rows_hbm, idx0_hbm, idx1_hbm, out_hbm, *scratch):
        base = (lax.axis_index("s") * sc.num_cores + lax.axis_index("c")) * per_worker
        bufs = (scratch[:6], scratch[6:])

        def load(win, which):
            idx0_v, idx1_v, rows_v, lsem, _, _ = bufs[which]
            off = base + win * SC_WINDOW
            pltpu.sync_copy(idx0_hbm.at[pl.ds(off, SC_WINDOW)], idx0_v)
            pltpu.sync_copy(idx1_hbm.at[pl.ds(off, SC_WINDOW)], idx1_v)
            pltpu.async_copy(rows_hbm.at[pl.ds(off, SC_WINDOW)], rows_v, lsem)

        def send(which):
            idx0_v, idx1_v, rows_v, lsem, sem0, sem1 = bufs[which]
            pltpu.make_async_copy(rows_hbm.at[pl.ds(0, SC_WINDOW)], rows_v, lsem).wait()
            first = pltpu.async_copy(rows_v, out_hbm.at[idx0_v], sem0)
            second = pltpu.async_copy(rows_v, out_hbm.at[idx1_v], sem1)
            first.wait()
            second.wait()

        load(0, 0)

        @pl.loop(0, n_win // 2 - 1)
        def _(pair):
            win = 2 * pair
            load(win + 1, 1)
            send(0)
            load(win + 2, 0)
            send(1)

        load(n_win - 1, 1)
        send(0)
        send(1)

    return scatter(rows, idx0, idx1)


def _combine_rows_kernel(h_ref, w_ref, g_ref, b_ref, y0_ref, y1_ref, o_ref):
    w = w_ref[...]
    y0_lo, y0_hi = _unpack_bf16_pair(y0_ref[...])
    y1_lo, y1_hi = _unpack_bf16_pair(y1_ref[...])
    y = jnp.concatenate([w[:, 0:1] * y0_lo + w[:, 1:2] * y1_lo, w[:, 0:1] * y0_hi + w[:, 1:2] * y1_hi], axis=1)
    o_ref[...] = _ln(DEEPNORM_ALPHA * h_ref[...] + y, g_ref[...], b_ref[...])


def _combine_rows_call(h1, rw, g, b, yg):
    t = h1.shape[0]
    tm = COMB_TM
    n = t // tm
    return pl.pallas_call(
        _combine_rows_kernel,
        grid=(n,),
        in_specs=[pl.BlockSpec((tm, D_MODEL), lambda i: (i, 0)),
                  pl.BlockSpec((tm, 8), lambda i: (i, 0)),
                  pl.BlockSpec((1, D_MODEL), lambda i: (0, 0)),
                  pl.BlockSpec((1, D_MODEL), lambda i: (0, 0)),
                  pl.BlockSpec((tm, HALF_D), lambda i: (i, 0)),
                  pl.BlockSpec((tm, HALF_D), lambda i: (n + i, 0))],
        out_specs=pl.BlockSpec((tm, D_MODEL), lambda i: (i, 0)),
        out_shape=jax.ShapeDtypeStruct((t, D_MODEL), F32),
        compiler_params=pltpu.CompilerParams(dimension_semantics=("arbitrary",),
                                             vmem_limit_bytes=VMEM_LIMIT),
        name="combine_rows",
    )(h1, rw, g, b, yg, yg)


def _take_cols(w, idx):
    wz = jnp.concatenate([w, jnp.zeros((w.shape[0], 1), w.dtype)], axis=1)
    return wz[:, np.where(idx < 0, w.shape[1], idx)]


def _layout_indices():
    lane_src = np.full((HEAD_PAD,), -1, np.int64)
    lane_src[0:ROPE_HALF] = QK_NOPE + np.arange(ROPE_HALF)
    lane_src[ROPE_HALF:X2_LANE] = np.arange(X2_LANE - ROPE_HALF)
    lane_src[X2_LANE:X2_LANE + ROPE_HALF] = QK_NOPE + ROPE_HALF + np.arange(ROPE_HALF)
    n_lo = X2_LANE - ROPE_HALF
    lane_src[X2_LANE + ROPE_HALF:X2_LANE + ROPE_HALF + QK_NOPE - n_lo] = n_lo + np.arange(QK_NOPE - n_lo)
    q_idx = np.concatenate([np.where(lane_src >= 0, lane_src + (QK_NOPE + QK_ROPE) * h, -1)
                            for h in range(N_HEADS)])
    nope_src = np.where(lane_src < QK_NOPE, lane_src, -1)
    k_idx = np.concatenate([np.where(nope_src >= 0, nope_src + (QK_NOPE + V_DIM) * h, -1)
                            for h in range(N_HEADS)])
    v_idx = np.concatenate([QK_NOPE + np.arange(V_DIM) + (QK_NOPE + V_DIM) * h for h in range(N_HEADS)])
    kr_idx = np.full((LANES,), -1, np.int64)
    kr_idx[0:ROPE_HALF] = np.arange(ROPE_HALF)
    kr_idx[X2_LANE:X2_LANE + ROPE_HALF] = ROPE_HALF + np.arange(ROPE_HALF)
    return q_idx, k_idx, v_idx, kr_idx


def _rope_lane_tables(length):
    inv_freq = np.float32(ROPE_THETA) ** (-np.arange(ROPE_HALF, dtype=np.float32) / np.float32(ROPE_HALF))
    ang = np.arange(length, dtype=np.float32)[:, None] * inv_freq[None, :].astype(np.float32)
    cos, sin = np.cos(ang).astype(np.float32), np.sin(ang).astype(np.float32)
    ctab = np.ones((length, LANES), np.float32)
    stab = np.zeros((length, LANES), np.float32)
    for lo, sign in ((0, -1.0), (X2_LANE, 1.0)):
        ctab[:, lo:lo + ROPE_HALF] = cos
        stab[:, lo:lo + ROPE_HALF] = sign * sin
    return jnp.asarray(ctab), jnp.asarray(stab)


def kernel(x, meta_tokens, ln_in_g, ln_in_b, w_in, conv_w, conv_b, conv_ln_g, conv_ln_b, q_norm_g, w_uq,
           kv_norm_g, w_ukv, conv_out_g, attn_out_g, w_o, ln1_g, ln1_b, w_rg, b_rg, w_re, b_re,
           w_gate, w_up, w_down, ln2_g, ln2_b):
    assert w_in.shape[0] == 1, "single layer"
    bsz, seq, d = x.shape
    t = bsz * seq
    row = lambda a: a.reshape(1, -1)

    q_idx, k_idx, v_idx, kr_idx = _layout_indices()
    c4 = 2 * C_CONV + Q_LORA + KV_LORA
    w_in_l = jnp.concatenate([w_in[0][:, :c4], _take_cols(w_in[0][:, c4:], kr_idx)], axis=1).astype(BF16)
    wuq_l = _take_cols(w_uq[0], q_idx).astype(BF16)
    wuk_l = _take_cols(w_ukv[0], k_idx).astype(BF16)
    wvt_l = w_ukv[0][:, v_idx].T.astype(BF16)
    wo_l = w_o[0].astype(BF16)
    n_route_pad = ROUTE_ROWS - N_EXPERTS - N_GROUPS
    wr_t = jnp.concatenate([w_re[0].T, w_rg[0].T, jnp.zeros((n_route_pad, d), F32)], axis=0)
    wrh_l = wr_t.astype(BF16)
    wrl_l = (wr_t - wrh_l.astype(F32)).astype(BF16)
    br_l = jnp.broadcast_to(jnp.concatenate([b_re[0], b_rg[0], jnp.zeros((n_route_pad,), F32)])[:, None],
                            (ROUTE_ROWS, LANES))
    ctab, stab = _rope_lane_tables(N_META + seq)

    proj_w = (row(ln_in_g), row(ln_in_b), w_in_l, row(q_norm_g[0]), wuq_l, row(kv_norm_g[0]), wuk_l, wvt_l)
    _, u_m, _, k_m, vt_m = _proj_call(meta_tokens.astype(F32), N_META, 1, *proj_w, ctab[:N_META], stab[:N_META])
    h, u, q, k, vt = _proj_call(x.reshape(t, d), PROJ_TM, seq // PROJ_TM, *proj_w, ctab[N_META:], stab[N_META:])

    o = _attn_call(q.reshape(bsz, seq, -1), k.reshape(bsz, seq, -1),
                   vt.reshape(bsz, seq // ATT_T, D_ATTN, ATT_T), k_m, vt_m)

    pre = jnp.concatenate([jnp.zeros((HALO - N_META, C_CONV), F32), u_m], axis=0)
    cw = jnp.concatenate([conv_w[0], jnp.zeros((HALO - CONV_K, C_CONV), F32)], axis=0)
    h1, hp, ri, rw, cnt = _mix_call(h.reshape(bsz, seq, d), u.reshape(bsz, seq, C_CONV), pre, cw,
                                    row(conv_b[0]), row(conv_ln_g[0]), row(conv_ln_b[0]), row(conv_out_g[0]),
                                    o, row(attn_out_g[0]), wo_l, row(ln1_g[0]), row(ln1_b[0]), wrh_l, wrl_l, br_l)

    counts = cnt[:N_EXPERTS, 0].astype(jnp.int32)
    padded = (counts + MOE_BM - 1) // MOE_BM * MOE_BM
    pad_end = jnp.cumsum(padded)
    pad_start = pad_end - padded
    n_blocks = (2 * t) // MOE_BM + N_EXPERTS
    dest, blk = _tables_call(pad_start, pad_end, counts, ri, n_blocks)
    blk_e, blk_valid = blk[0, :n_blocks], blk[1, :n_blocks]

    h1f = h1.reshape(t, d)
    dest0, dest1 = dest[0], dest[1]
    xb = _sc_scatter_rows(hp.reshape(t, HALF_D), dest0, dest1, n_blocks * MOE_BM)
    yb = _expert_call(blk_e, blk_valid, xb, w_gate, w_up, w_down)
    yg = _sc_gather_rows(yb, jnp.concatenate([dest0, dest1]))
    out = _combine_rows_call(h1f, rw.reshape(t, 8), row(ln2_g[0]), row(ln2_b[0]), yg)
    return out.reshape(bsz, seq, d)
```

```python
import functools
import math

import numpy as np
import jax
import jax.numpy as jnp
from jax import lax
from jax.experimental import pallas as pl
from jax.experimental.pallas import tpu as pltpu
from jax.experimental.pallas import tpu_sc as plsc

F32 = jnp.float32
BF16 = jnp.bfloat16

D_MODEL = 1024
HALF_D = D_MODEL // 2
N_META = 16
CHUNK = 64
C_CONV = 512
CONV_K = 31
N_HEADS = 8
QK_NOPE = 64
QK_ROPE = 32
V_DIM = 64
Q_LORA = 384
KV_LORA = 256
ROPE_THETA = 10000.0
D_ATTN = N_HEADS * V_DIM
N_GROUPS = 4
EXPERTS_PER_GROUP = 8
N_EXPERTS = N_GROUPS * EXPERTS_PER_GROUP
D_EXPERT = 256
DEEPNORM_ALPHA = 2.0 ** 0.25
EPS = 1e-5
NEG_INF = -1e30

LANES = 128
SUBLANES = 8
HEAD_PAD = LANES
ROPE_HALF = QK_ROPE // 2
X2_LANE = 64
Z_COLS = 2 * C_CONV + Q_LORA + KV_LORA + LANES
Q_SCALE = (QK_NOPE + QK_ROPE) ** -0.5 * math.log2(math.e)

PROJ_TM = 1024
ATT_T = 256
ATT_HEADS = 8
MIX_TM = 512
ROUTE_ROWS = 48
HALO = 32
MOE_BM = 1024
COMB_TM = 1024
SC_WINDOW = 64
VMEM_LIMIT = 56 * 1024 * 1024


def _ln(x, g, b):
    mu = jnp.mean(x, axis=-1, keepdims=True)
    xc = x - mu
    var = jnp.mean(xc * xc, axis=-1, keepdims=True)
    return xc * lax.rsqrt(var + EPS) * g + b


def _rms(x, g):
    return x * lax.rsqrt(jnp.mean(x * x, axis=-1, keepdims=True) + EPS) * g


def _sigmoid(x):
    return 1.0 / (1.0 + jnp.exp(-x))


def _pack_bf16_pair(lo, hi):
    lo_bits = lax.bitcast_convert_type(lo.astype(BF16).astype(F32), jnp.uint32)
    hi_bits = lax.bitcast_convert_type(hi.astype(BF16).astype(F32), jnp.uint32)
    return (lo_bits >> 16) | (hi_bits & jnp.uint32(0xFFFF0000))


def _unpack_bf16_pair(packed):
    lo = lax.bitcast_convert_type(packed << 16, F32)
    hi = lax.bitcast_convert_type(packed & jnp.uint32(0xFFFF0000), F32)
    return lo, hi


def _proj_kernel(x_ref, g_ref, b_ref, win_ref, qg_ref, wuq_ref, kvg_ref, wuk_ref, wvt_ref, c_ref, s_ref,
                 u_ref, q_ref, k_ref, vt_ref):
    n_part, part = vt_ref.shape[1], vt_ref.shape[3]
    zs = []
    for r in range(n_part):
        h = _ln(x_ref[r * part:(r + 1) * part, :], g_ref[...], b_ref[...])
        zs.append(jnp.dot(h.astype(BF16), win_ref[...], preferred_element_type=F32))
    c0 = 2 * C_CONV
    for r, z in enumerate(zs):
        rows = slice(r * part, (r + 1) * part)
        u_ref[rows, :] = z[:, :C_CONV] * _sigmoid(z[:, C_CONV:c0])
        cq = z[:, c0:c0 + Q_LORA]
        ckv = z[:, c0 + Q_LORA:c0 + Q_LORA + KV_LORA]
        kr = z[:, c0 + Q_LORA + KV_LORA:]
        cs = c_ref[rows, :]
        sn = s_ref[rows, :]
        q = jnp.dot(_rms(cq, qg_ref[...]).astype(BF16), wuq_ref[...], preferred_element_type=F32)
        kvn = _rms(ckv, kvg_ref[...]).astype(BF16)
        kn = jnp.dot(kvn, wuk_ref[...], preferred_element_type=F32)
        kr_rot = kr * cs + pltpu.roll(kr, X2_LANE, 1) * sn
        for hh in range(N_HEADS):
            sl = slice(HEAD_PAD * hh, HEAD_PAD * (hh + 1))
            qh = q[:, sl]
            qh = (qh * cs + pltpu.roll(qh, X2_LANE, 1) * sn) * Q_SCALE
            q_ref[rows, sl] = qh.astype(BF16)
            k_ref[rows, sl] = (kn[:, sl] + kr_rot).astype(BF16)
        vt = lax.dot_general(wvt_ref[...], kvn, (((1,), (1,)), ((), ())), preferred_element_type=F32)
        vt_ref[0, r] = vt.astype(BF16)


def _proj_call(x2d, tm, pos_blocks, ln_g, ln_b, w_in, qg, wuq, kvg, wuk, wvt, ctab, stab):
    t = x2d.shape[0]
    sub_w = min(tm, ATT_T)
    n_sub = tm // sub_w
    full = lambda shape: pl.BlockSpec(shape, lambda i: (0,) * len(shape))
    row = lambda w: pl.BlockSpec((tm, w), lambda i: (i, 0))
    tab = pl.BlockSpec((tm, LANES), lambda i: (i % pos_blocks, 0))
    return pl.pallas_call(
        _proj_kernel,
        grid=(t // tm,),
        in_specs=[row(D_MODEL), full((1, D_MODEL)), full((1, D_MODEL)), full((D_MODEL, Z_COLS)),
                  full((1, Q_LORA)), full((Q_LORA, N_HEADS * HEAD_PAD)),
                  full((1, KV_LORA)), full((KV_LORA, N_HEADS * HEAD_PAD)), full((D_ATTN, KV_LORA)), tab, tab],
        out_specs=[row(C_CONV), row(N_HEADS * HEAD_PAD), row(N_HEADS * HEAD_PAD),
                   pl.BlockSpec((1, n_sub, D_ATTN, sub_w), lambda i: (i, 0, 0, 0))],
        out_shape=[jax.ShapeDtypeStruct((t, C_CONV), F32),
                   jax.ShapeDtypeStruct((t, N_HEADS * HEAD_PAD), BF16),
                   jax.ShapeDtypeStruct((t, N_HEADS * HEAD_PAD), BF16),
                   jax.ShapeDtypeStruct((t // tm, n_sub, D_ATTN, sub_w), BF16)],
        compiler_params=pltpu.CompilerParams(dimension_semantics=("arbitrary",),
                                             vmem_limit_bytes=VMEM_LIMIT),
        name="proj",
    )(x2d, ln_g, ln_b, w_in, qg, wuq, kvg, wuk, wvt, ctab, stab)


def _attn_kernel(q_ref, k_ref, vt_ref, km_ref, vtm_ref, o_ref, s_sc):
    seq = q_ref.shape[1]
    nq = seq // ATT_T
    key_chunk = lax.broadcasted_iota(jnp.int32, (ATT_T, ATT_T), 0) // CHUNK
    qry_chunk = lax.broadcasted_iota(jnp.int32, (ATT_T, ATT_T), 1) // CHUNK
    diag_visible = key_chunk <= qry_chunk
    nt = (((1,), (1,)), ((), ()))
    heads = range(ATT_HEADS)
    hsl = [slice(HEAD_PAD * hh, HEAD_PAD * (hh + 1)) for hh in heads]
    vsl = [slice(V_DIM * hh, V_DIM * (hh + 1)) for hh in heads]

    def update(s, vt, m, l, acc):
        m_new = jnp.maximum(m, jnp.max(s, axis=0, keepdims=True))
        alpha = jnp.exp2(m - m_new)
        p = jnp.exp2(s - m_new)
        l = alpha * l + jnp.sum(p, axis=0, keepdims=True)
        acc = alpha * acc + jnp.dot(vt, p.astype(BF16), preferred_element_type=F32)
        return m_new, l, acc

    def scores(qi, j, hh):
        q0 = pl.multiple_of(qi * ATT_T, ATT_T)
        k0 = pl.multiple_of(j * ATT_T, ATT_T)
        return lax.dot_general(k_ref[0, pl.ds(k0, ATT_T), hsl[hh]], q_ref[0, pl.ds(q0, ATT_T), hsl[hh]], nt,
                               preferred_element_type=F32)

    for hh in heads:
        s_sc[0, hh] = scores(0, 0, hh)

    def q_body(qi, carry):
        row0 = pl.multiple_of(qi * ATT_T, ATT_T)
        slot = qi % 2
        state = []
        for hh in heads:
            state += [jnp.full((1, ATT_T), NEG_INF, F32), jnp.zeros((1, ATT_T), F32),
                      jnp.zeros((V_DIM, ATT_T), F32)]

        def kv_body(j, st):
            out = []
            s_next = scores(qi, j + 1, 0)
            for hh in heads:
                s_cur = s_sc[slot, hh]
                s_after = scores(qi, j + 1, hh + 1) if hh + 1 < ATT_HEADS else None
                out += update(s_cur, vt_ref[0, j, vsl[hh], :], *st[3 * hh:3 * hh + 3])
                s_sc[slot, hh] = s_next
                s_next = s_after
            return tuple(out)

        state = lax.fori_loop(0, qi, kv_body, tuple(state))
        q_next = jnp.minimum(qi + 1, nq - 1)
        sms = [lax.dot_general(km_ref[:, hsl[hh]], q_ref[0, pl.ds(row0, ATT_T), hsl[hh]], nt,
                               preferred_element_type=F32) for hh in heads]
        outs = []
        for hh in heads:
            m, l, acc = state[3 * hh:3 * hh + 3]
            s = jnp.where(diag_visible, s_sc[slot, hh], NEG_INF)
            sm = sms[hh]
            m_new = jnp.maximum(m, jnp.maximum(jnp.max(s, axis=0, keepdims=True),
                                               jnp.max(sm, axis=0, keepdims=True)))
            alpha = jnp.exp2(m - m_new)
            p = jnp.exp2(s - m_new)
            pm = jnp.exp2(sm - m_new)
            l = alpha * l + jnp.sum(p, axis=0, keepdims=True) + jnp.sum(pm, axis=0, keepdims=True)
            acc = (alpha * acc + jnp.dot(vt_ref[0, qi, vsl[hh], :], p.astype(BF16), preferred_element_type=F32)
                   + jnp.dot(vtm_ref[0, 0, vsl[hh], :], pm.astype(BF16), preferred_element_type=F32))
            outs.append(acc / l)
        for hh in heads:
            s_sc[1 - slot, hh] = scores(q_next, 0, hh)
        o_ref[0, pl.ds(row0, ATT_T), :] = jnp.concatenate(outs, axis=0).T
        return carry

    lax.fori_loop(0, nq, q_body, 0)


def _attn_call(q, k, vt, k_meta, vt_meta):
    b, seq, _ = q.shape
    nk = seq // ATT_T
    hb = ATT_HEADS
    return pl.pallas_call(
        _attn_kernel,
        grid=(b, N_HEADS // hb),
        in_specs=[pl.BlockSpec((1, seq, hb * HEAD_PAD), lambda i, j: (i, 0, j)),
                  pl.BlockSpec((1, seq, hb * HEAD_PAD), lambda i, j: (i, 0, j)),
                  pl.BlockSpec((1, nk, hb * V_DIM, ATT_T), lambda i, j: (i, 0, j, 0)),
                  pl.BlockSpec((N_META, hb * HEAD_PAD), lambda i, j: (0, j)),
                  pl.BlockSpec((1, 1, hb * V_DIM, N_META), lambda i, j: (0, 0, j, 0))],
        out_specs=pl.BlockSpec((1, seq, hb * V_DIM), lambda i, j: (i, 0, j)),
        out_shape=jax.ShapeDtypeStruct((b, seq, D_ATTN), F32),
        scratch_shapes=[pltpu.VMEM((2, hb, ATT_T, ATT_T), F32)],
        compiler_params=pltpu.CompilerParams(dimension_semantics=("arbitrary", "arbitrary"),
                                             vmem_limit_bytes=VMEM_LIMIT),
        name="attn",
    )(q, k, vt, k_meta, vt_meta)


def _mix_kernel(x_ref, ig_ref, ib_ref, u_ref, uh_ref, pre_ref, cw_ref, cb_ref, clg_ref, clb_ref,
                cog_ref, o_ref, aog_ref, wo_ref, l1g_ref, l1b_ref, wrh_ref, wrl_ref, br_ref, tri_ref,
                h1_ref, hp_ref, ri_ref, rw_ref, cnt_ref, win_sc, sh_sc, carry_sc):
    bi = pl.program_id(0)
    ti = pl.program_id(1)
    tm = MIX_TM

    @pl.when((bi == 0) & (ti == 0))
    def _():
        carry_sc[...] = jnp.zeros_like(carry_sc)

    win_sc[0:HALO, :] = jnp.where(ti == 0, pre_ref[...], uh_ref[0])
    win_sc[HALO:, :] = u_ref[0]
    first = HALO - (CONV_K - 1)
    for res in range(1, SUBLANES):
        rows = tm + HALO - SUBLANES
        sh_sc[res - 1, 0:rows, :] = win_sc[pl.ds(res, rows), :]

    acc = jnp.broadcast_to(cb_ref[...], (tm, C_CONV))
    for o in range(first, first + CONV_K):
        res, lo = o % SUBLANES, o - o % SUBLANES
        src = sh_sc[res - 1, lo:lo + tm, :] if res else win_sc[lo:lo + tm, :]
        acc = acc + src * cw_ref[o - first:o - first + 1, :]
    c = _ln(acc, clg_ref[...], clb_ref[...])
    c = c * _sigmoid(c)
    cn = _rms(c, cog_ref[...])
    on = _rms(o_ref[0], aog_ref[...])
    y = (jnp.dot(cn.astype(BF16), wo_ref[0:C_CONV, :], preferred_element_type=F32)
         + jnp.dot(on.astype(BF16), wo_ref[C_CONV:, :], preferred_element_type=F32))
    h = _ln(x_ref[0], ig_ref[...], ib_ref[...])
    h1 = _ln(DEEPNORM_ALPHA * h + y, l1g_ref[...], l1b_ref[...])
    h1_ref[0] = h1
    hp_ref[0] = _pack_bf16_pair(h1[:, :HALF_D], h1[:, HALF_D:])

    nt = (((1,), (1,)), ((), ()))
    h_hi = h1.astype(BF16)
    h_lo = (h1 - h_hi.astype(F32)).astype(BF16)
    lt = (lax.dot_general(wrh_ref[...], h_hi, nt, preferred_element_type=F32)
          + lax.dot_general(wrl_ref[...], h_hi, nt, preferred_element_type=F32)
          + lax.dot_general(wrh_ref[...], h_lo, nt, preferred_element_type=F32)) + br_ref[:, 0:1]
    row = lax.broadcasted_iota(jnp.int32, (ROUTE_ROWS, tm), 0)
    big = jnp.int32(1 << 20)
    is_g = (row >= N_EXPERTS) & (row < N_EXPERTS + N_GROUPS)
    gl = jnp.where(is_g, lt, -jnp.inf)
    gmax = jnp.max(gl, axis=0, keepdims=True)
    g_idx = jnp.min(jnp.where(gl == gmax, row - N_EXPERTS, big), axis=0, keepdims=True)
    g_w = 1.0 / jnp.sum(jnp.exp(gl - gmax), axis=0, keepdims=True)
    in_grp = (row < N_EXPERTS) & ((row // EXPERTS_PER_GROUP) == g_idx)
    el = jnp.where(in_grp, lt, -jnp.inf)
    v1 = jnp.max(el, axis=0, keepdims=True)
    i1 = jnp.min(jnp.where(el == v1, row, big), axis=0, keepdims=True)
    el2 = jnp.where(row == i1, -jnp.inf, el)
    v2 = jnp.max(el2, axis=0, keepdims=True)
    i2 = jnp.min(jnp.where(el2 == v2, row, big), axis=0, keepdims=True)
    e2 = jnp.exp(v2 - v1)
    w1 = g_w / (1.0 + e2)
    w2 = g_w * e2 / (1.0 + e2)

    oh1 = row == i1
    oh2 = row == i2
    oh = jnp.where(oh1 | oh2, 1.0, 0.0)
    before = jnp.dot(oh.astype(BF16), tri_ref[...], preferred_element_type=F32) + carry_sc[:, 0:1]
    r1 = jnp.sum(jnp.where(oh1, before, 0.0), axis=0, keepdims=True).astype(jnp.int32)
    r2 = jnp.sum(jnp.where(oh2, before, 0.0), axis=0, keepdims=True).astype(jnp.int32)
    carry_sc[...] = carry_sc[...] + jnp.sum(oh, axis=1, keepdims=True)
    cnt_ref[...] = carry_sc[...]

    sub = lax.broadcasted_iota(jnp.int32, (SUBLANES, tm), 0)
    ri_ref[...] = jnp.where(sub == 0, i1, jnp.where(sub == 1, i2, jnp.where(sub == 2, r1,
                            jnp.where(sub == 3, r2, 0))))
    lrow = lax.broadcasted_iota(jnp.int32, (LANES, tm), 0)
    rw_ref[0] = jnp.where(lrow == 0, w1, jnp.where(lrow == 1, w2, 0.0)).T[:, :8]


def _mix_call(x, ig, ib, u, pre, cw, cb, clg, clb, cog, o, aog, wo, l1g, l1b, wrh, wrl, br):
    b, seq, _ = x.shape
    tm = MIX_TM
    nt = seq // tm
    full = lambda shape: pl.BlockSpec(shape, lambda i, j: (0,) * len(shape))
    tile = lambda w: pl.BlockSpec((1, tm, w), lambda i, j: (i, j, 0))
    halo = pl.BlockSpec((1, HALO, C_CONV), lambda i, j: (i, jnp.maximum(j * (tm // HALO) - 1, 0), 0))
    earlier = jnp.asarray(np.triu(np.ones((tm, tm), np.float32), k=1), BF16)
    return pl.pallas_call(
        _mix_kernel,
        grid=(b, nt),
        in_specs=[tile(D_MODEL), full((1, D_MODEL)), full((1, D_MODEL)),
                  tile(C_CONV), halo, full((HALO, C_CONV)), full((HALO, C_CONV)), full((1, C_CONV)),
                  full((1, C_CONV)), full((1, C_CONV)), full((1, C_CONV)),
                  tile(D_ATTN), full((1, D_ATTN)), full((D_MODEL, D_MODEL)),
                  full((1, D_MODEL)), full((1, D_MODEL)), full((ROUTE_ROWS, D_MODEL)),
                  full((ROUTE_ROWS, D_MODEL)), full((ROUTE_ROWS, LANES)), full((tm, tm))],
        out_specs=[tile(D_MODEL), tile(HALF_D), pl.BlockSpec((8, tm), lambda i, j: (0, i * nt + j)), tile(8),
                   full((ROUTE_ROWS, LANES))],
        out_shape=[jax.ShapeDtypeStruct((b, seq, D_MODEL), F32),
                   jax.ShapeDtypeStruct((b, seq, HALF_D), jnp.uint32),
                   jax.ShapeDtypeStruct((8, b * seq), jnp.int32),
                   jax.ShapeDtypeStruct((b, seq, 8), F32),
                   jax.ShapeDtypeStruct((ROUTE_ROWS, LANES), F32)],
        scratch_shapes=[pltpu.VMEM((HALO + tm, C_CONV), F32),
                        pltpu.VMEM((SUBLANES - 1, HALO + tm, C_CONV), F32),
                        pltpu.VMEM((ROUTE_ROWS, LANES), F32)],
        compiler_params=pltpu.CompilerParams(dimension_semantics=("arbitrary", "arbitrary"),
                                             vmem_limit_bytes=VMEM_LIMIT),
        name="mix",
    )(x, ig, ib, u, u, pre, cw, cb, clg, clb, cog, o, aog, wo, l1g, l1b, wrh, wrl, br, earlier)


def _tables_kernel(ps_ref, pe_ref, cn_ref, ri_ref, dest_ref, blk_ref):
    ri = ri_ref[...]
    start = jnp.zeros_like(ri)
    for e in range(N_EXPERTS):
        start = jnp.where(ri == e, ps_ref[e], start)
    dest_ref[...] = jnp.zeros_like(dest_ref)
    dest_ref[0:2, :] = start[0:2, :] + ri_ref[2:4, :]
    blk_start = lax.broadcasted_iota(jnp.int32, blk_ref.shape, 1) * MOE_BM
    be = jnp.zeros(blk_ref.shape, jnp.int32)
    for e in range(N_EXPERTS):
        be = be + jnp.where(pe_ref[e] <= blk_start, 1, 0)
    be = jnp.minimum(be, N_EXPERTS - 1)
    end = jnp.zeros(blk_ref.shape, jnp.int32)
    for e in range(N_EXPERTS):
        end = jnp.where(be == e, ps_ref[e] + cn_ref[e], end)
    valid = jnp.clip(end - blk_start, 0, MOE_BM)
    row = lax.broadcasted_iota(jnp.int32, blk_ref.shape, 0)
    blk_ref[...] = jnp.where(row == 0, be, valid)


def _tables_call(pad_start, pad_end, counts, ri, n_blocks):
    t = ri.shape[1]
    blk_w = -(-n_blocks // LANES) * LANES
    return pl.pallas_call(
        _tables_kernel,
        grid_spec=pltpu.PrefetchScalarGridSpec(
            num_scalar_prefetch=3,
            grid=(1,),
            in_specs=[pl.BlockSpec((SUBLANES, t), lambda i, *_: (0, 0))],
            out_specs=[pl.BlockSpec((SUBLANES, t), lambda i, *_: (0, 0)),
                       pl.BlockSpec((SUBLANES, blk_w), lambda i, *_: (0, 0))]),
        out_shape=[jax.ShapeDtypeStruct((SUBLANES, t), jnp.int32),
                   jax.ShapeDtypeStruct((SUBLANES, blk_w), jnp.int32)],
        compiler_params=pltpu.CompilerParams(dimension_semantics=("arbitrary",),
                                             vmem_limit_bytes=VMEM_LIMIT),
        name="tables",
    )(pad_start, pad_end, counts, ri)


def _expert_kernel(be_ref, nv_ref, x_ref, wg_ref, wu_ref, wd_ref, y_ref, wg_sc, wu_sc, wd_sc):
    i = pl.program_id(0)
    nv = nv_ref[i]

    @pl.when((i == 0) | (be_ref[i] != be_ref[jnp.maximum(i - 1, 0)]))
    def _():
        wg_sc[...] = wg_ref[0, 0].astype(BF16)
        wu_sc[...] = wu_ref[0, 0].astype(BF16)
        wd_sc[...] = wd_ref[0, 0].astype(BF16)

    @pl.when(nv > 0)
    def _():
        rowid = lax.broadcasted_iota(jnp.int32, x_ref.shape, 0)
        x_lo, x_hi = _unpack_bf16_pair(jnp.where(rowid < nv, x_ref[...], jnp.uint32(0)))
        x_lo = x_lo.astype(BF16)
        x_hi = x_hi.astype(BF16)
        g = (jnp.dot(x_lo, wg_sc[:HALF_D, :], preferred_element_type=F32)
             + jnp.dot(x_hi, wg_sc[HALF_D:, :], preferred_element_type=F32))
        up = (jnp.dot(x_lo, wu_sc[:HALF_D, :], preferred_element_type=F32)
              + jnp.dot(x_hi, wu_sc[HALF_D:, :], preferred_element_type=F32))
        act = g * _sigmoid(g) * up
        y = jnp.dot(act.astype(BF16), wd_sc[...], preferred_element_type=F32)
        y_ref[...] = _pack_bf16_pair(y[:, :HALF_D], y[:, HALF_D:])

    @pl.when(nv == 0)
    def _():
        y_ref[...] = jnp.zeros_like(y_ref)


def _expert_call(blk_e, blk_valid, xb, w_gate, w_up, w_down):
    r = xb.shape[0]
    bm = MOE_BM
    wspec = lambda a, b: pl.BlockSpec((1, 1, a, b), lambda i, be, nv: (0, be[i], 0, 0))
    return pl.pallas_call(
        _expert_kernel,
        grid_spec=pltpu.PrefetchScalarGridSpec(
            num_scalar_prefetch=2,
            grid=(r // bm,),
            in_specs=[pl.BlockSpec((bm, HALF_D), lambda i, be, nv: (i, 0)),
                      wspec(D_MODEL, D_EXPERT), wspec(D_MODEL, D_EXPERT), wspec(D_EXPERT, D_MODEL)],
            out_specs=pl.BlockSpec((bm, HALF_D), lambda i, be, nv: (i, 0)),
            scratch_shapes=[pltpu.VMEM((D_MODEL, D_EXPERT), BF16), pltpu.VMEM((D_MODEL, D_EXPERT), BF16),
                            pltpu.VMEM((D_EXPERT, D_MODEL), BF16)]),
        out_shape=jax.ShapeDtypeStruct((r, HALF_D), jnp.uint32),
        compiler_params=pltpu.CompilerParams(dimension_semantics=("arbitrary",),
                                             vmem_limit_bytes=VMEM_LIMIT),
        name="experts",
    )(blk_e, blk_valid, xb, w_gate, w_up, w_down)


def _sc_gather_rows(table, idx):
    m = idx.shape[0]
    width = table.shape[1]
    sc = plsc.get_sparse_core_info()
    workers = sc.num_cores * sc.num_subcores
    per_worker = m // workers
    mesh = plsc.VectorSubcoreMesh(core_axis_name="c", subcore_axis_name="s")

    n_win = per_worker // SC_WINDOW
    assert n_win % 2 == 0 and n_win >= 2
    buf = lambda: [pltpu.VMEM((SC_WINDOW,), jnp.int32), pltpu.VMEM((SC_WINDOW, width), table.dtype),
                   pltpu.SemaphoreType.DMA]

    @functools.partial(pl.kernel, mesh=mesh, out_type=jax.ShapeDtypeStruct((m, width), table.dtype),
                       scratch_types=buf() + buf())
    def gather(table_hbm, idx_hbm, out_hbm, idx_a, rows_a, sem_a, idx_b, rows_b, sem_b):
        base = (lax.axis_index("s") * sc.num_cores + lax.axis_index("c")) * per_worker
        bufs = ((idx_a, rows_a, sem_a), (idx_b, rows_b, sem_b))

        def fetch(win, which):
            idx_v, rows_v, sem = bufs[which]
            pltpu.sync_copy(idx_hbm.at[pl.ds(base + win * SC_WINDOW, SC_WINDOW)], idx_v)
            pltpu.async_copy(table_hbm.at[idx_v], rows_v, sem)

        def flush(win, which):
            idx_v, rows_v, sem = bufs[which]
            pltpu.make_async_copy(table_hbm.at[idx_v], rows_v, sem).wait()
            pltpu.sync_copy(rows_v, out_hbm.at[pl.ds(base + win * SC_WINDOW, SC_WINDOW)])

        fetch(0, 0)

        @pl.loop(0, n_win // 2 - 1)
        def _(pair):
            win = 2 * pair
            fetch(win + 1, 1)
            flush(win, 0)
            fetch(win + 2, 0)
            flush(win + 1, 1)

        fetch(n_win - 1, 1)
        flush(n_win - 2, 0)
        flush(n_win - 1, 1)

    return gather(table, idx)


def _sc_scatter_rows(rows, idx0, idx1, n_out):
    m, width = rows.shape
    sc = plsc.get_sparse_core_info()
    workers = sc.num_cores * sc.num_subcores
    per_worker = m // workers
    mesh = plsc.VectorSubcoreMesh(core_axis_name="c", subcore_axis_name="s")

    n_win = per_worker // SC_WINDOW
    assert n_win % 2 == 0 and n_win >= 2
    buf = lambda: [pltpu.VMEM((SC_WINDOW,), jnp.int32), pltpu.VMEM((SC_WINDOW,), jnp.int32),
                   pltpu.VMEM((SC_WINDOW, width), rows.dtype),
                   pltpu.SemaphoreType.DMA, pltpu.SemaphoreType.DMA, pltpu.SemaphoreType.DMA]

    @functools.partial(pl.kernel, mesh=mesh, out_type=jax.ShapeDtypeStruct((n_out, width), rows.dtype),
                       scratch_types=buf() + buf())
    def scatter(rows_hbm, idx0_hbm, idx1_hbm, out_hbm, *scratch):
        base = (lax.axis_index("s") * sc.num_cores + lax.axis_index("c")) * per_worker
        bufs = (scratch[:6], scratch[6:])

        def load(win, which):
            idx0_v, idx1_v, rows_v, lsem, _, _ = bufs[which]
            off = base + win * SC_WINDOW
            pltpu.sync_copy(idx0_hbm.at[pl.ds(off, SC_WINDOW)], idx0_v)
            pltpu.sync_copy(idx1_hbm.at[pl.ds(off, SC_WINDOW)], idx1_v)
            pltpu.async_copy(rows_hbm.at[pl.ds(off, SC_WINDOW)], rows_v, lsem)

        def send(which):
            idx0_v, idx1_v, rows_v, lsem, sem0, sem1 = bufs[which]
            pltpu.make_async_copy(rows_hbm.at[pl.ds(0, SC_WINDOW)], rows_v, lsem).wait()
            first = pltpu.async_copy(rows_v, out_hbm.at[idx0_v], sem0)
            second = pltpu.async_copy(rows_v, out_hbm.at[idx1_v], sem1)
            first.wait()
            second.wait()

        load(0, 0)

        @pl.loop(0, n_win // 2 - 1)
        def _(pair):
            win = 2 * pair
            load(win + 1, 1)
            send(0)
            load(win + 2, 0)
            send(1)

        load(n_win - 1, 1)
        send(0)
        send(1)

    return scatter(rows, idx0, idx1)


def _combine_rows_kernel(h_ref, w_ref, g_ref, b_ref, y0_ref, y1_ref, o_ref):
    w = w_ref[...]
    y0_lo, y0_hi = _unpack_bf16_pair(y0_ref[...])
    y1_lo, y1_hi = _unpack_bf16_pair(y1_ref[...])
    y = jnp.concatenate([w[:, 0:1] * y0_lo + w[:, 1:2] * y1_lo, w[:, 0:1] * y0_hi + w[:, 1:2] * y1_hi], axis=1)
    o_ref[...] = _ln(DEEPNORM_ALPHA * h_ref[...] + y, g_ref[...], b_ref[...])


def _combine_rows_call(h1, rw, g, b, yg):
    t = h1.shape[0]
    tm = COMB_TM
    n = t // tm
    return pl.pallas_call(
        _combine_rows_kernel,
        grid=(n,),
        in_specs=[pl.BlockSpec((tm, D_MODEL), lambda i: (i, 0)),
                  pl.BlockSpec((tm, 8), lambda i: (i, 0)),
                  pl.BlockSpec((1, D_MODEL), lambda i: (0, 0)),
                  pl.BlockSpec((1, D_MODEL), lambda i: (0, 0)),
                  pl.BlockSpec((tm, HALF_D), lambda i: (i, 0)),
                  pl.BlockSpec((tm, HALF_D), lambda i: (n + i, 0))],
        out_specs=pl.BlockSpec((tm, D_MODEL), lambda i: (i, 0)),
        out_shape=jax.ShapeDtypeStruct((t, D_MODEL), F32),
        compiler_params=pltpu.CompilerParams(dimension_semantics=("arbitrary",),
                                             vmem_limit_bytes=VMEM_LIMIT),
        name="combine_rows",
    )(h1, rw, g, b, yg, yg)


def _take_cols(w, idx):
    wz = jnp.concatenate([w, jnp.zeros((w.shape[0], 1), w.dtype)], axis=1)
    return wz[:, np.where(idx < 0, w.shape[1], idx)]


def _layout_indices():
    lane_src = np.full((HEAD_PAD,), -1, np.int64)
    lane_src[0:ROPE_HALF] = QK_NOPE + np.arange(ROPE_HALF)
    lane_src[ROPE_HALF:X2_LANE] = np.arange(X2_LANE - ROPE_HALF)
    lane_src[X2_LANE:X2_LANE + ROPE_HALF] = QK_NOPE + ROPE_HALF + np.arange(ROPE_HALF)
    n_lo = X2_LANE - ROPE_HALF
    lane_src[X2_LANE + ROPE_HALF:X2_LANE + ROPE_HALF + QK_NOPE - n_lo] = n_lo + np.arange(QK_NOPE - n_lo)
    q_idx = np.concatenate([np.where(lane_src >= 0, lane_src + (QK_NOPE + QK_ROPE) * h, -1)
                            for h in range(N_HEADS)])
    nope_src = np.where(lane_src < QK_NOPE, lane_src, -1)
    k_idx = np.concatenate([np.where(nope_src >= 0, nope_src + (QK_NOPE + V_DIM) * h, -1)
                            for h in range(N_HEADS)])
    v_idx = np.concatenate([QK_NOPE + np.arange(V_DIM) + (QK_NOPE + V_DIM) * h for h in range(N_HEADS)])
    kr_idx = np.full((LANES,), -1, np.int64)
    kr_idx[0:ROPE_HALF] = np.arange(ROPE_HALF)
    kr_idx[X2_LANE:X2_LANE + ROPE_HALF] = ROPE_HALF + np.arange(ROPE_HALF)
    return q_idx, k_idx, v_idx, kr_idx


def _rope_lane_tables(length):
    inv_freq = np.float32(ROPE_THETA) ** (-np.arange(ROPE_HALF, dtype=np.float32) / np.float32(ROPE_HALF))
    ang = np.arange(length, dtype=np.float32)[:, None] * inv_freq[None, :].astype(np.float32)
    cos, sin = np.cos(ang).astype(np.float32), np.sin(ang).astype(np.float32)
    ctab = np.ones((length, LANES), np.float32)
    stab = np.zeros((length, LANES), np.float32)
    for lo, sign in ((0, -1.0), (X2_LANE, 1.0)):
        ctab[:, lo:lo + ROPE_HALF] = cos
        stab[:, lo:lo + ROPE_HALF] = sign * sin
    return jnp.asarray(ctab), jnp.asarray(stab)


def kernel(x, meta_tokens, ln_in_g, ln_in_b, w_in, conv_w, conv_b, conv_ln_g, conv_ln_b, q_norm_g, w_uq,
           kv_norm_g, w_ukv, conv_out_g, attn_out_g, w_o, ln1_g, ln1_b, w_rg, b_rg, w_re, b_re,
           w_gate, w_up, w_down, ln2_g, ln2_b):
    assert w_in.shape[0] == 1, "single layer"
    bsz, seq, d = x.shape
    t = bsz * seq
    row = lambda a: a.reshape(1, -1)

    q_idx, k_idx, v_idx, kr_idx = _layout_indices()
    c4 = 2 * C_CONV + Q_LORA + KV_LORA
    w_in_l = jnp.concatenate([w_in[0][:, :c4], _take_cols(w_in[0][:, c4:], kr_idx)], axis=1).astype(BF16)
    wuq_l = _take_cols(w_uq[0], q_idx).astype(BF16)
    wuk_l = _take_cols(w_ukv[0], k_idx).astype(BF16)
    wvt_l = w_ukv[0][:, v_idx].T.astype(BF16)
    wo_l = w_o[0].astype(BF16)
    n_route_pad = ROUTE_ROWS - N_EXPERTS - N_GROUPS
    wr_t = jnp.concatenate([w_re[0].T, w_rg[0].T, jnp.zeros((n_route_pad, d), F32)], axis=0)
    wrh_l = wr_t.astype(BF16)
    wrl_l = (wr_t - wrh_l.astype(F32)).astype(BF16)
    br_l = jnp.broadcast_to(jnp.concatenate([b_re[0], b_rg[0], jnp.zeros((n_route_pad,), F32)])[:, None],
                            (ROUTE_ROWS, LANES))
    ctab, stab = _rope_lane_tables(N_META + seq)

    proj_w = (row(ln_in_g), row(ln_in_b), w_in_l, row(q_norm_g[0]), wuq_l, row(kv_norm_g[0]), wuk_l, wvt_l)
    u_m, _, k_m, vt_m = _proj_call(meta_tokens.astype(F32), N_META, 1, *proj_w, ctab[:N_META], stab[:N_META])
    u, q, k, vt = _proj_call(x.reshape(t, d), PROJ_TM, seq // PROJ_TM, *proj_w, ctab[N_META:], stab[N_META:])

    o = _attn_call(q.reshape(bsz, seq, -1), k.reshape(bsz, seq, -1),
                   vt.reshape(bsz, seq // ATT_T, D_ATTN, ATT_T), k_m, vt_m)

    pre = jnp.concatenate([jnp.zeros((HALO - N_META, C_CONV), F32), u_m], axis=0)
    cw = jnp.concatenate([conv_w[0], jnp.zeros((HALO - CONV_K, C_CONV), F32)], axis=0)
    h1, hp, ri, rw, cnt = _mix_call(x, row(ln_in_g), row(ln_in_b), u.reshape(bsz, seq, C_CONV), pre, cw,
                                    row(conv_b[0]), row(conv_ln_g[0]), row(conv_ln_b[0]), row(conv_out_g[0]),
                                    o, row(attn_out_g[0]), wo_l, row(ln1_g[0]), row(ln1_b[0]), wrh_l, wrl_l, br_l)

    counts = cnt[:N_EXPERTS, 0].astype(jnp.int32)
    padded = (counts + MOE_BM - 1) // MOE_BM * MOE_BM
    pad_end = jnp.cumsum(padded)
    pad_start = pad_end - padded
    n_blocks = (2 * t) // MOE_BM + N_EXPERTS
    dest, blk = _tables_call(pad_start, pad_end, counts, ri, n_blocks)
    blk_e, blk_valid = blk[0, :n_blocks], blk[1, :n_blocks]

    h1f = h1.reshape(t, d)
    dest0, dest1 = dest[0], dest[1]
    xb = _sc_scatter_rows(hp.reshape(t, HALF_D), dest0, dest1, n_blocks * MOE_BM)
    yb = _expert_call(blk_e, blk_valid, xb, w_gate, w_up, w_down)
    yg = _sc_gather_rows(yb, jnp.concatenate([dest0, dest1]))
    out = _combine_rows_call(h1f, rw.reshape(t, 8), row(ln2_g[0]), row(ln2_b[0]), yg)
    return out.reshape(bsz, seq, d)
```

```python
import functools
import math

import numpy as np
import jax
import jax.numpy as jnp
from jax import lax
from jax.experimental import pallas as pl
from jax.experimental.pallas import tpu as pltpu
from jax.experimental.pallas import tpu_sc as plsc

F32 = jnp.float32
BF16 = jnp.bfloat16

D_MODEL = 1024
HALF_D = D_MODEL // 2
N_META = 16
CHUNK = 64
C_CONV = 512
CONV_K = 31
N_HEADS = 8
QK_NOPE = 64
QK_ROPE = 32
V_DIM = 64
Q_LORA = 384
KV_LORA = 256
ROPE_THETA = 10000.0
D_ATTN = N_HEADS * V_DIM
N_GROUPS = 4
EXPERTS_PER_GROUP = 8
N_EXPERTS = N_GROUPS * EXPERTS_PER_GROUP
D_EXPERT = 256
DEEPNORM_ALPHA = 2.0 ** 0.25
EPS = 1e-5
NEG_INF = -1e30

LANES = 128
SUBLANES = 8
HEAD_PAD = LANES
ROPE_HALF = QK_ROPE // 2
X2_LANE = 64
Z_COLS = 2 * C_CONV + Q_LORA + KV_LORA + LANES
Q_SCALE = (QK_NOPE + QK_ROPE) ** -0.5 * math.log2(math.e)

PROJ_TM = 1024
ATT_T = 256
ATT_HEADS = 8
MIX_TM = 512
ROUTE_ROWS = 48
HALO = 32
MOE_BM = 2048
MOE_SUB = 512
COMB_TM = 1024
SC_WINDOW = 64
VMEM_LIMIT = 56 * 1024 * 1024


def _ln(x, g, b):
    mu = jnp.mean(x, axis=-1, keepdims=True)
    xc = x - mu
    var = jnp.mean(xc * xc, axis=-1, keepdims=True)
    return xc * lax.rsqrt(var + EPS) * g + b


def _rms(x, g):
    return x * lax.rsqrt(jnp.mean(x * x, axis=-1, keepdims=True) + EPS) * g


def _sigmoid(x):
    return 1.0 / (1.0 + jnp.exp(-x))


def _pack_bf16_pair(lo, hi):
    lo_bits = lax.bitcast_convert_type(lo.astype(BF16).astype(F32), jnp.uint32)
    hi_bits = lax.bitcast_convert_type(hi.astype(BF16).astype(F32), jnp.uint32)
    return (lo_bits >> 16) | (hi_bits & jnp.uint32(0xFFFF0000))


def _unpack_bf16_pair(packed):
    lo = lax.bitcast_convert_type(packed << 16, F32)
    hi = lax.bitcast_convert_type(packed & jnp.uint32(0xFFFF0000), F32)
    return lo, hi


def _proj_kernel(x_ref, g_ref, b_ref, win_ref, qg_ref, wuq_ref, kvg_ref, wuk_ref, wvt_ref, c_ref, s_ref,
                 u_ref, q_ref, k_ref, vt_ref):
    n_part, part = vt_ref.shape[1], vt_ref.shape[3]
    zs = []
    for r in range(n_part):
        h = _ln(x_ref[r * part:(r + 1) * part, :], g_ref[...], b_ref[...])
        zs.append(jnp.dot(h.astype(BF16), win_ref[...], preferred_element_type=F32))
    c0 = 2 * C_CONV
    for r, z in enumerate(zs):
        rows = slice(r * part, (r + 1) * part)
        u_ref[rows, :] = z[:, :C_CONV] * _sigmoid(z[:, C_CONV:c0])
        cq = z[:, c0:c0 + Q_LORA]
        ckv = z[:, c0 + Q_LORA:c0 + Q_LORA + KV_LORA]
        kr = z[:, c0 + Q_LORA + KV_LORA:]
        cs = c_ref[rows, :]
        sn = s_ref[rows, :]
        q = jnp.dot(_rms(cq, qg_ref[...]).astype(BF16), wuq_ref[...], preferred_element_type=F32)
        kvn = _rms(ckv, kvg_ref[...]).astype(BF16)
        kn = jnp.dot(kvn, wuk_ref[...], preferred_element_type=F32)
        kr_rot = kr * cs + pltpu.roll(kr, X2_LANE, 1) * sn
        for hh in range(N_HEADS):
            sl = slice(HEAD_PAD * hh, HEAD_PAD * (hh + 1))
            qh = q[:, sl]
            qh = (qh * cs + pltpu.roll(qh, X2_LANE, 1) * sn) * Q_SCALE
            q_ref[rows, sl] = qh.astype(BF16)
            k_ref[rows, sl] = (kn[:, sl] + kr_rot).astype(BF16)
        vt = lax.dot_general(wvt_ref[...], kvn, (((1,), (1,)), ((), ())), preferred_element_type=F32)
        vt_ref[0, r] = vt.astype(BF16)


def _proj_call(x2d, tm, pos_blocks, ln_g, ln_b, w_in, qg, wuq, kvg, wuk, wvt, ctab, stab):
    t = x2d.shape[0]
    sub_w = min(tm, ATT_T)
    n_sub = tm // sub_w
    full = lambda shape: pl.BlockSpec(shape, lambda i: (0,) * len(shape))
    row = lambda w: pl.BlockSpec((tm, w), lambda i: (i, 0))
    tab = pl.BlockSpec((tm, LANES), lambda i: (i % pos_blocks, 0))
    return pl.pallas_call(
        _proj_kernel,
        grid=(t // tm,),
        in_specs=[row(D_MODEL), full((1, D_MODEL)), full((1, D_MODEL)), full((D_MODEL, Z_COLS)),
                  full((1, Q_LORA)), full((Q_LORA, N_HEADS * HEAD_PAD)),
                  full((1, KV_LORA)), full((KV_LORA, N_HEADS * HEAD_PAD)), full((D_ATTN, KV_LORA)), tab, tab],
        out_specs=[row(C_CONV), row(N_HEADS * HEAD_PAD), row(N_HEADS * HEAD_PAD),
                   pl.BlockSpec((1, n_sub, D_ATTN, sub_w), lambda i: (i, 0, 0, 0))],
        out_shape=[jax.ShapeDtypeStruct((t, C_CONV), F32),
                   jax.ShapeDtypeStruct((t, N_HEADS * HEAD_PAD), BF16),
                   jax.ShapeDtypeStruct((t, N_HEADS * HEAD_PAD), BF16),
                   jax.ShapeDtypeStruct((t // tm, n_sub, D_ATTN, sub_w), BF16)],
        compiler_params=pltpu.CompilerParams(dimension_semantics=("arbitrary",),
                                             vmem_limit_bytes=VMEM_LIMIT),
        name="proj",
    )(x2d, ln_g, ln_b, w_in, qg, wuq, kvg, wuk, wvt, ctab, stab)


def _attn_kernel(q_ref, k_ref, vt_ref, km_ref, vtm_ref, o_ref, s_sc):
    seq = q_ref.shape[1]
    nq = seq // ATT_T
    key_chunk = lax.broadcasted_iota(jnp.int32, (ATT_T, ATT_T), 0) // CHUNK
    qry_chunk = lax.broadcasted_iota(jnp.int32, (ATT_T, ATT_T), 1) // CHUNK
    diag_visible = key_chunk <= qry_chunk
    nt = (((1,), (1,)), ((), ()))
    heads = range(ATT_HEADS)
    hsl = [slice(HEAD_PAD * hh, HEAD_PAD * (hh + 1)) for hh in heads]
    vsl = [slice(V_DIM * hh, V_DIM * (hh + 1)) for hh in heads]

    def update(s, vt, m, l, acc):
        m_new = jnp.maximum(m, jnp.max(s, axis=0, keepdims=True))
        alpha = jnp.exp2(m - m_new)
        p = jnp.exp2(s - m_new)
        l = alpha * l + jnp.sum(p, axis=0, keepdims=True)
        acc = alpha * acc + jnp.dot(vt, p.astype(BF16), preferred_element_type=F32)
        return m_new, l, acc

    def scores(qi, j, hh):
        q0 = pl.multiple_of(qi * ATT_T, ATT_T)
        k0 = pl.multiple_of(j * ATT_T, ATT_T)
        return lax.dot_general(k_ref[0, pl.ds(k0, ATT_T), hsl[hh]], q_ref[0, pl.ds(q0, ATT_T), hsl[hh]], nt,
                               preferred_element_type=F32)

    for hh in heads:
        s_sc[0, hh] = scores(0, 0, hh)

    def q_body(qi, carry):
        row0 = pl.multiple_of(qi * ATT_T, ATT_T)
        slot = qi % 2
        state = []
        for hh in heads:
            state += [jnp.full((1, ATT_T), NEG_INF, F32), jnp.zeros((1, ATT_T), F32),
                      jnp.zeros((V_DIM, ATT_T), F32)]

        def kv_body(j, st):
            out = []
            s_next = scores(qi, j + 1, 0)
            for hh in heads:
                s_cur = s_sc[slot, hh]
                s_after = scores(qi, j + 1, hh + 1) if hh + 1 < ATT_HEADS else None
                out += update(s_cur, vt_ref[0, j, vsl[hh], :], *st[3 * hh:3 * hh + 3])
                s_sc[slot, hh] = s_next
                s_next = s_after
            return tuple(out)

        state = lax.fori_loop(0, qi, kv_body, tuple(state))
        q_next = jnp.minimum(qi + 1, nq - 1)
        sms = [lax.dot_general(km_ref[:, hsl[hh]], q_ref[0, pl.ds(row0, ATT_T), hsl[hh]], nt,
                               preferred_element_type=F32) for hh in heads]
        outs = []
        for hh in heads:
            m, l, acc = state[3 * hh:3 * hh + 3]
            s = jnp.where(diag_visible, s_sc[slot, hh], NEG_INF)
            sm = sms[hh]
            m_new = jnp.maximum(m, jnp.maximum(jnp.max(s, axis=0, keepdims=True),
                                               jnp.max(sm, axis=0, keepdims=True)))
            alpha = jnp.exp2(m - m_new)
            p = jnp.exp2(s - m_new)
            pm = jnp.exp2(sm - m_new)
            l = alpha * l + jnp.sum(p, axis=0, keepdims=True) + jnp.sum(pm, axis=0, keepdims=True)
            acc = (alpha * acc + jnp.dot(vt_ref[0, qi, vsl[hh], :], p.astype(BF16), preferred_element_type=F32)
                   + jnp.dot(vtm_ref[0, 0, vsl[hh], :], pm.astype(BF16), preferred_element_type=F32))
            outs.append(acc / l)
        for hh in heads:
            s_sc[1 - slot, hh] = scores(q_next, 0, hh)
        o_ref[0, pl.ds(row0, ATT_T), :] = jnp.concatenate(outs, axis=0).T
        return carry

    lax.fori_loop(0, nq, q_body, 0)


def _attn_call(q, k, vt, k_meta, vt_meta):
    b, seq, _ = q.shape
    nk = seq // ATT_T
    hb = ATT_HEADS
    return pl.pallas_call(
        _attn_kernel,
        grid=(b, N_HEADS // hb),
        in_specs=[pl.BlockSpec((1, seq, hb * HEAD_PAD), lambda i, j: (i, 0, j)),
                  pl.BlockSpec((1, seq, hb * HEAD_PAD), lambda i, j: (i, 0, j)),
                  pl.BlockSpec((1, nk, hb * V_DIM, ATT_T), lambda i, j: (i, 0, j, 0)),
                  pl.BlockSpec((N_META, hb * HEAD_PAD), lambda i, j: (0, j)),
                  pl.BlockSpec((1, 1, hb * V_DIM, N_META), lambda i, j: (0, 0, j, 0))],
        out_specs=pl.BlockSpec((1, seq, hb * V_DIM), lambda i, j: (i, 0, j)),
        out_shape=jax.ShapeDtypeStruct((b, seq, D_ATTN), F32),
        scratch_shapes=[pltpu.VMEM((2, hb, ATT_T, ATT_T), F32)],
        compiler_params=pltpu.CompilerParams(dimension_semantics=("arbitrary", "arbitrary"),
                                             vmem_limit_bytes=VMEM_LIMIT),
        name="attn",
    )(q, k, vt, k_meta, vt_meta)


def _mix_kernel(x_ref, ig_ref, ib_ref, u_ref, uh_ref, pre_ref, cw_ref, cb_ref, clg_ref, clb_ref,
                cog_ref, o_ref, aog_ref, wo_ref, l1g_ref, l1b_ref, wrh_ref, wrl_ref, br_ref, tri_ref,
                h1_ref, hp_ref, ri_ref, rw_ref, cnt_ref, win_sc, sh_sc, carry_sc):
    bi = pl.program_id(0)
    ti = pl.program_id(1)
    tm = MIX_TM

    @pl.when((bi == 0) & (ti == 0))
    def _():
        carry_sc[...] = jnp.zeros_like(carry_sc)

    win_sc[0:HALO, :] = jnp.where(ti == 0, pre_ref[...], uh_ref[0])
    win_sc[HALO:, :] = u_ref[0]
    first = HALO - (CONV_K - 1)
    for res in range(1, SUBLANES):
        rows = tm + HALO - SUBLANES
        sh_sc[res - 1, 0:rows, :] = win_sc[pl.ds(res, rows), :]

    acc = jnp.broadcast_to(cb_ref[...], (tm, C_CONV))
    for o in range(first, first + CONV_K):
        res, lo = o % SUBLANES, o - o % SUBLANES
        src = sh_sc[res - 1, lo:lo + tm, :] if res else win_sc[lo:lo + tm, :]
        acc = acc + src * cw_ref[o - first:o - first + 1, :]
    c = _ln(acc, clg_ref[...], clb_ref[...])
    c = c * _sigmoid(c)
    cn = _rms(c, cog_ref[...])
    on = _rms(o_ref[0], aog_ref[...])
    y = (jnp.dot(cn.astype(BF16), wo_ref[0:C_CONV, :], preferred_element_type=F32)
         + jnp.dot(on.astype(BF16), wo_ref[C_CONV:, :], preferred_element_type=F32))
    h = _ln(x_ref[0], ig_ref[...], ib_ref[...])
    h1 = _ln(DEEPNORM_ALPHA * h + y, l1g_ref[...], l1b_ref[...])
    h1_ref[0] = h1
    hp_ref[0] = _pack_bf16_pair(h1[:, :HALF_D], h1[:, HALF_D:])

    nt = (((1,), (1,)), ((), ()))
    h_hi = h1.astype(BF16)
    h_lo = (h1 - h_hi.astype(F32)).astype(BF16)
    lt = (lax.dot_general(wrh_ref[...], h_hi, nt, preferred_element_type=F32)
          + lax.dot_general(wrl_ref[...], h_hi, nt, preferred_element_type=F32)
          + lax.dot_general(wrh_ref[...], h_lo, nt, preferred_element_type=F32)) + br_ref[:, 0:1]
    row = lax.broadcasted_iota(jnp.int32, (ROUTE_ROWS, tm), 0)
    big = jnp.int32(1 << 20)
    is_g = (row >= N_EXPERTS) & (row < N_EXPERTS + N_GROUPS)
    gl = jnp.where(is_g, lt, -jnp.inf)
    gmax = jnp.max(gl, axis=0, keepdims=True)
    g_idx = jnp.min(jnp.where(gl == gmax, row - N_EXPERTS, big), axis=0, keepdims=True)
    g_w = 1.0 / jnp.sum(jnp.exp(gl - gmax), axis=0, keepdims=True)
    in_grp = (row < N_EXPERTS) & ((row // EXPERTS_PER_GROUP) == g_idx)
    el = jnp.where(in_grp, lt, -jnp.inf)
    v1 = jnp.max(el, axis=0, keepdims=True)
    i1 = jnp.min(jnp.where(el == v1, row, big), axis=0, keepdims=True)
    el2 = jnp.where(row == i1, -jnp.inf, el)
    v2 = jnp.max(el2, axis=0, keepdims=True)
    i2 = jnp.min(jnp.where(el2 == v2, row, big), axis=0, keepdims=True)
    e2 = jnp.exp(v2 - v1)
    w1 = g_w / (1.0 + e2)
    w2 = g_w * e2 / (1.0 + e2)

    oh1 = row == i1
    oh2 = row == i2
    oh = jnp.where(oh1 | oh2, 1.0, 0.0)
    before = jnp.dot(oh.astype(BF16), tri_ref[...], preferred_element_type=F32) + carry_sc[:, 0:1]
    r1 = jnp.sum(jnp.where(oh1, before, 0.0), axis=0, keepdims=True).astype(jnp.int32)
    r2 = jnp.sum(jnp.where(oh2, before, 0.0), axis=0, keepdims=True).astype(jnp.int32)
    carry_sc[...] = carry_sc[...] + jnp.sum(oh, axis=1, keepdims=True)
    cnt_ref[...] = carry_sc[...]

    sub = lax.broadcasted_iota(jnp.int32, (SUBLANES, tm), 0)
    ri_ref[...] = jnp.where(sub == 0, i1, jnp.where(sub == 1, i2, jnp.where(sub == 2, r1,
                            jnp.where(sub == 3, r2, 0))))
    lrow = lax.broadcasted_iota(jnp.int32, (LANES, tm), 0)
    rw_ref[0] = jnp.where(lrow == 0, w1, jnp.where(lrow == 1, w2, 0.0)).T[:, :8]


def _mix_call(x, ig, ib, u, pre, cw, cb, clg, clb, cog, o, aog, wo, l1g, l1b, wrh, wrl, br):
    b, seq, _ = x.shape
    tm = MIX_TM
    nt = seq // tm
    full = lambda shape: pl.BlockSpec(shape, lambda i, j: (0,) * len(shape))
    tile = lambda w: pl.BlockSpec((1, tm, w), lambda i, j: (i, j, 0))
    halo = pl.BlockSpec((1, HALO, C_CONV), lambda i, j: (i, jnp.maximum(j * (tm // HALO) - 1, 0), 0))
    earlier = jnp.asarray(np.triu(np.ones((tm, tm), np.float32), k=1), BF16)
    return pl.pallas_call(
        _mix_kernel,
        grid=(b, nt),
        in_specs=[tile(D_MODEL), full((1, D_MODEL)), full((1, D_MODEL)),
                  tile(C_CONV), halo, full((HALO, C_CONV)), full((HALO, C_CONV)), full((1, C_CONV)),
                  full((1, C_CONV)), full((1, C_CONV)), full((1, C_CONV)),
                  tile(D_ATTN), full((1, D_ATTN)), full((D_MODEL, D_MODEL)),
                  full((1, D_MODEL)), full((1, D_MODEL)), full((ROUTE_ROWS, D_MODEL)),
                  full((ROUTE_ROWS, D_MODEL)), full((ROUTE_ROWS, LANES)), full((tm, tm))],
        out_specs=[tile(D_MODEL), tile(HALF_D), pl.BlockSpec((8, tm), lambda i, j: (0, i * nt + j)), tile(8),
                   full((ROUTE_ROWS, LANES))],
        out_shape=[jax.ShapeDtypeStruct((b, seq, D_MODEL), F32),
                   jax.ShapeDtypeStruct((b, seq, HALF_D), jnp.uint32),
                   jax.ShapeDtypeStruct((8, b * seq), jnp.int32),
                   jax.ShapeDtypeStruct((b, seq, 8), F32),
                   jax.ShapeDtypeStruct((ROUTE_ROWS, LANES), F32)],
        scratch_shapes=[pltpu.VMEM((HALO + tm, C_CONV), F32),
                        pltpu.VMEM((SUBLANES - 1, HALO + tm, C_CONV), F32),
                        pltpu.VMEM((ROUTE_ROWS, LANES), F32)],
        compiler_params=pltpu.CompilerParams(dimension_semantics=("arbitrary", "arbitrary"),
                                             vmem_limit_bytes=VMEM_LIMIT),
        name="mix",
    )(x, ig, ib, u, u, pre, cw, cb, clg, clb, cog, o, aog, wo, l1g, l1b, wrh, wrl, br, earlier)


def _tables_kernel(ps_ref, pe_ref, cn_ref, ri_ref, dest_ref, blk_ref):
    ri = ri_ref[...]
    start = jnp.zeros_like(ri)
    for e in range(N_EXPERTS):
        start = jnp.where(ri == e, ps_ref[e], start)
    dest_ref[...] = jnp.zeros_like(dest_ref)
    dest_ref[0:2, :] = start[0:2, :] + ri_ref[2:4, :]
    blk_start = lax.broadcasted_iota(jnp.int32, blk_ref.shape, 1) * MOE_BM
    be = jnp.zeros(blk_ref.shape, jnp.int32)
    for e in range(N_EXPERTS):
        be = be + jnp.where(pe_ref[e] <= blk_start, 1, 0)
    be = jnp.minimum(be, N_EXPERTS - 1)
    end = jnp.zeros(blk_ref.shape, jnp.int32)
    for e in range(N_EXPERTS):
        end = jnp.where(be == e, ps_ref[e] + cn_ref[e], end)
    valid = jnp.clip(end - blk_start, 0, MOE_BM)
    row = lax.broadcasted_iota(jnp.int32, blk_ref.shape, 0)
    blk_ref[...] = jnp.where(row == 0, be, valid)


def _tables_call(pad_start, pad_end, counts, ri, n_blocks):
    t = ri.shape[1]
    blk_w = -(-n_blocks // LANES) * LANES
    return pl.pallas_call(
        _tables_kernel,
        grid_spec=pltpu.PrefetchScalarGridSpec(
            num_scalar_prefetch=3,
            grid=(1,),
            in_specs=[pl.BlockSpec((SUBLANES, t), lambda i, *_: (0, 0))],
            out_specs=[pl.BlockSpec((SUBLANES, t), lambda i, *_: (0, 0)),
                       pl.BlockSpec((SUBLANES, blk_w), lambda i, *_: (0, 0))]),
        out_shape=[jax.ShapeDtypeStruct((SUBLANES, t), jnp.int32),
                   jax.ShapeDtypeStruct((SUBLANES, blk_w), jnp.int32)],
        compiler_params=pltpu.CompilerParams(dimension_semantics=("arbitrary",),
                                             vmem_limit_bytes=VMEM_LIMIT),
        name="tables",
    )(pad_start, pad_end, counts, ri)


def _expert_kernel(be_ref, nv_ref, x_ref, wg_ref, wu_ref, wd_ref, y_ref, wg_sc, wu_sc, wd_sc):
    i = pl.program_id(0)
    nv = nv_ref[i]

    @pl.when((i == 0) | (be_ref[i] != be_ref[jnp.maximum(i - 1, 0)]))
    def _():
        wg_sc[...] = wg_ref[0, 0].astype(BF16)
        wu_sc[...] = wu_ref[0, 0].astype(BF16)
        wd_sc[...] = wd_ref[0, 0].astype(BF16)

    rowid = lax.broadcasted_iota(jnp.int32, (MOE_SUB, HALF_D), 0)
    for r0 in range(0, x_ref.shape[0], MOE_SUB):
        rows = slice(r0, r0 + MOE_SUB)

        @pl.when(nv > r0)
        def _():
            x_lo, x_hi = _unpack_bf16_pair(jnp.where(rowid < nv - r0, x_ref[rows, :], jnp.uint32(0)))
            x_lo = x_lo.astype(BF16)
            x_hi = x_hi.astype(BF16)
            g = (jnp.dot(x_lo, wg_sc[:HALF_D, :], preferred_element_type=F32)
                 + jnp.dot(x_hi, wg_sc[HALF_D:, :], preferred_element_type=F32))
            up = (jnp.dot(x_lo, wu_sc[:HALF_D, :], preferred_element_type=F32)
                  + jnp.dot(x_hi, wu_sc[HALF_D:, :], preferred_element_type=F32))
            act = g * _sigmoid(g) * up
            y = jnp.dot(act.astype(BF16), wd_sc[...], preferred_element_type=F32)
            y_ref[rows, :] = _pack_bf16_pair(y[:, :HALF_D], y[:, HALF_D:])

        @pl.when(nv <= r0)
        def _():
            y_ref[rows, :] = jnp.zeros((MOE_SUB, HALF_D), jnp.uint32)


def _expert_call(blk_e, blk_valid, xb, w_gate, w_up, w_down):
    r = xb.shape[0]
    bm = MOE_BM
    wspec = lambda a, b: pl.BlockSpec((1, 1, a, b), lambda i, be, nv: (0, be[i], 0, 0))
    return pl.pallas_call(
        _expert_kernel,
        grid_spec=pltpu.PrefetchScalarGridSpec(
            num_scalar_prefetch=2,
            grid=(r // bm,),
            in_specs=[pl.BlockSpec((bm, HALF_D), lambda i, be, nv: (i, 0)),
                      wspec(D_MODEL, D_EXPERT), wspec(D_MODEL, D_EXPERT), wspec(D_EXPERT, D_MODEL)],
            out_specs=pl.BlockSpec((bm, HALF_D), lambda i, be, nv: (i, 0)),
            scratch_shapes=[pltpu.VMEM((D_MODEL, D_EXPERT), BF16), pltpu.VMEM((D_MODEL, D_EXPERT), BF16),
                            pltpu.VMEM((D_EXPERT, D_MODEL), BF16)]),
        out_shape=jax.ShapeDtypeStruct((r, HALF_D), jnp.uint32),
        compiler_params=pltpu.CompilerParams(dimension_semantics=("arbitrary",),
                                             vmem_limit_bytes=VMEM_LIMIT),
        name="experts",
    )(blk_e, blk_valid, xb, w_gate, w_up, w_down)


def _sc_gather_rows(table, idx):
    m = idx.shape[0]
    width = table.shape[1]
    sc = plsc.get_sparse_core_info()
    workers = sc.num_cores * sc.num_subcores
    per_worker = m // workers
    mesh = plsc.VectorSubcoreMesh(core_axis_name="c", subcore_axis_name="s")

    n_win = per_worker // SC_WINDOW
    assert n_win % 2 == 0 and n_win >= 2
    buf = lambda: [pltpu.VMEM((SC_WINDOW,), jnp.int32), pltpu.VMEM((SC_WINDOW, width), table.dtype),
                   pltpu.SemaphoreType.DMA]

    @functools.partial(pl.kernel, mesh=mesh, out_type=jax.ShapeDtypeStruct((m, width), table.dtype),
                       scratch_types=buf() + buf())
    def gather(table_hbm, idx_hbm, out_hbm, idx_a, rows_a, sem_a, idx_b, rows_b, sem_b):
        base = (lax.axis_index("s") * sc.num_cores + lax.axis_index("c")) * per_worker
        bufs = ((idx_a, rows_a, sem_a), (idx_b, rows_b, sem_b))

        def fetch(win, which):
            idx_v, rows_v, sem = bufs[which]
            pltpu.sync_copy(idx_hbm.at[pl.ds(base + win * SC_WINDOW, SC_WINDOW)], idx_v)
            pltpu.async_copy(table_hbm.at[idx_v], rows_v, sem)

        def flush(win, which):
            idx_v, rows_v, sem = bufs[which]
            pltpu.make_async_copy(table_hbm.at[idx_v], rows_v, sem).wait()
            pltpu.sync_copy(rows_v, out_hbm.at[pl.ds(base + win * SC_WINDOW, SC_WINDOW)])

        fetch(0, 0)

        @pl.loop(0, n_win // 2 - 1)
        def _(pair):
            win = 2 * pair
            fetch(win + 1, 1)
            flush(win, 0)
            fetch(win + 2, 0)
            flush(win + 1, 1)

        fetch(n_win - 1, 1)
        flush(n_win - 2, 0)
        flush(n_win - 1, 1)

    return gather(table, idx)


def _sc_scatter_rows(rows, idx0, idx1, n_out):
    m, width = rows.shape
    sc = plsc.get_sparse_core_info()
    workers = sc.num_cores * sc.num_subcores
    per_worker = m // workers
    mesh = plsc.VectorSubcoreMesh(core_axis_name="c", subcore_axis_name="s")

    n_win = per_worker // SC_WINDOW
    assert n_win % 2 == 0 and n_win >= 2
    buf = lambda: [pltpu.VMEM((SC_WINDOW,), jnp.int32), pltpu.VMEM((SC_WINDOW,), jnp.int32),
                   pltpu.VMEM((SC_WINDOW, width), rows.dtype),
                   pltpu.SemaphoreType.DMA, pltpu.SemaphoreType.DMA, pltpu.SemaphoreType.DMA]

    @functools.partial(pl.kernel, mesh=mesh, out_type=jax.ShapeDtypeStruct((n_out, width), rows.dtype),
                       scratch_types=buf() + buf())
    def scatter(rows_hbm, idx0_hbm, idx1_hbm, out_hbm, *scratch):
        base = (lax.axis_index("s") * sc.num_cores + lax.axis_index("c")) * per_worker
        bufs = (scratch[:6], scratch[6:])

        def load(win, which):
            idx0_v, idx1_v, rows_v, lsem, _, _ = bufs[which]
            off = base + win * SC_WINDOW
            pltpu.sync_copy(idx0_hbm.at[pl.ds(off, SC_WINDOW)], idx0_v)
            pltpu.sync_copy(idx1_hbm.at[pl.ds(off, SC_WINDOW)], idx1_v)
            pltpu.async_copy(rows_hbm.at[pl.ds(off, SC_WINDOW)], rows_v, lsem)

        def send(which):
            idx0_v, idx1_v, rows_v, lsem, sem0, sem1 = bufs[which]
            pltpu.make_async_copy(rows_hbm.at[pl.ds(0, SC_WINDOW)], rows_v, lsem).wait()
            first = pltpu.async_copy(rows_v, out_hbm.at[idx0_v], sem0)
            second = pltpu.async_copy(rows_v, out_hbm.at[idx1_v], sem1)
            first.wait()
            second.wait()

        load(0, 0)

        @pl.loop(0, n_win // 2 - 1)
        def _(pair):
            win = 2 * pair
            load(win + 1, 1)
            send(0)
            load(win + 2, 0)
            send(1)

        load(n_win - 1, 1)
        send(0)
        send(1)

    return scatter(rows, idx0, idx1)


def _combine_rows_kernel(h_ref, w_ref, g_ref, b_ref, y0_ref, y1_ref, o_ref):
    w = w_ref[...]
    y0_lo, y0_hi = _unpack_bf16_pair(y0_ref[...])
    y1_lo, y1_hi = _unpack_bf16_pair(y1_ref[...])
    y = jnp.concatenate([w[:, 0:1] * y0_lo + w[:, 1:2] * y1_lo, w[:, 0:1] * y0_hi + w[:, 1:2] * y1_hi], axis=1)
    o_ref[...] = _ln(DEEPNORM_ALPHA * h_ref[...] + y, g_ref[...], b_ref[...])


def _combine_rows_call(h1, rw, g, b, yg):
    t = h1.shape[0]
    tm = COMB_TM
    n = t // tm
    return pl.pallas_call(
        _combine_rows_kernel,
        grid=(n,),
        in_specs=[pl.BlockSpec((tm, D_MODEL), lambda i: (i, 0)),
                  pl.BlockSpec((tm, 8), lambda i: (i, 0)),
                  pl.BlockSpec((1, D_MODEL), lambda i: (0, 0)),
                  pl.BlockSpec((1, D_MODEL), lambda i: (0, 0)),
                  pl.BlockSpec((tm, HALF_D), lambda i: (i, 0)),
                  pl.BlockSpec((tm, HALF_D), lambda i: (n + i, 0))],
        out_specs=pl.BlockSpec((tm, D_MODEL), lambda i: (i, 0)),
        out_shape=jax.ShapeDtypeStruct((t, D_MODEL), F32),
        compiler_params=pltpu.CompilerParams(dimension_semantics=("arbitrary",),
                                             vmem_limit_bytes=VMEM_LIMIT),
        name="combine_rows",
    )(h1, rw, g, b, yg, yg)


def _take_cols(w, idx):
    wz = jnp.concatenate([w, jnp.zeros((w.shape[0], 1), w.dtype)], axis=1)
    return wz[:, np.where(idx < 0, w.shape[1], idx)]


def _layout_indices():
    lane_src = np.full((HEAD_PAD,), -1, np.int64)
    lane_src[0:ROPE_HALF] = QK_NOPE + np.arange(ROPE_HALF)
    lane_src[ROPE_HALF:X2_LANE] = np.arange(X2_LANE - ROPE_HALF)
    lane_src[X2_LANE:X2_LANE + ROPE_HALF] = QK_NOPE + ROPE_HALF + np.arange(ROPE_HALF)
    n_lo = X2_LANE - ROPE_HALF
    lane_src[X2_LANE + ROPE_HALF:X2_LANE + ROPE_HALF + QK_NOPE - n_lo] = n_lo + np.arange(QK_NOPE - n_lo)
    q_idx = np.concatenate([np.where(lane_src >= 0, lane_src + (QK_NOPE + QK_ROPE) * h, -1)
                            for h in range(N_HEADS)])
    nope_src = np.where(lane_src < QK_NOPE, lane_src, -1)
    k_idx = np.concatenate([np.where(nope_src >= 0, nope_src + (QK_NOPE + V_DIM) * h, -1)
                            for h in range(N_HEADS)])
    v_idx = np.concatenate([QK_NOPE + np.arange(V_DIM) + (QK_NOPE + V_DIM) * h for h in range(N_HEADS)])
    kr_idx = np.full((LANES,), -1, np.int64)
    kr_idx[0:ROPE_HALF] = np.arange(ROPE_HALF)
    kr_idx[X2_LANE:X2_LANE + ROPE_HALF] = ROPE_HALF + np.arange(ROPE_HALF)
    return q_idx, k_idx, v_idx, kr_idx


def _rope_lane_tables(length):
    inv_freq = np.float32(ROPE_THETA) ** (-np.arange(ROPE_HALF, dtype=np.float32) / np.float32(ROPE_HALF))
    ang = np.arange(length, dtype=np.float32)[:, None] * inv_freq[None, :].astype(np.float32)
    cos, sin = np.cos(ang).astype(np.float32), np.sin(ang).astype(np.float32)
    ctab = np.ones((length, LANES), np.float32)
    stab = np.zeros((length, LANES), np.float32)
    for lo, sign in ((0, -1.0), (X2_LANE, 1.0)):
        ctab[:, lo:lo + ROPE_HALF] = cos
        stab[:, lo:lo + ROPE_HALF] = sign * sin
    return jnp.asarray(ctab), jnp.asarray(stab)


def kernel(x, meta_tokens, ln_in_g, ln_in_b, w_in, conv_w, conv_b, conv_ln_g, conv_ln_b, q_norm_g, w_uq,
           kv_norm_g, w_ukv, conv_out_g, attn_out_g, w_o, ln1_g, ln1_b, w_rg, b_rg, w_re, b_re,
           w_gate, w_up, w_down, ln2_g, ln2_b):
    assert w_in.shape[0] == 1, "single layer"
    bsz, seq, d = x.shape
    t = bsz * seq
    row = lambda a: a.reshape(1, -1)

    q_idx, k_idx, v_idx, kr_idx = _layout_indices()
    c4 = 2 * C_CONV + Q_LORA + KV_LORA
    w_in_l = jnp.concatenate([w_in[0][:, :c4], _take_cols(w_in[0][:, c4:], kr_idx)], axis=1).astype(BF16)
    wuq_l = _take_cols(w_uq[0], q_idx).astype(BF16)
    wuk_l = _take_cols(w_ukv[0], k_idx).astype(BF16)
    wvt_l = w_ukv[0][:, v_idx].T.astype(BF16)
    wo_l = w_o[0].astype(BF16)
    n_route_pad = ROUTE_ROWS - N_EXPERTS - N_GROUPS
    wr_t = jnp.concatenate([w_re[0].T, w_rg[0].T, jnp.zeros((n_route_pad, d), F32)], axis=0)
    wrh_l = wr_t.astype(BF16)
    wrl_l = (wr_t - wrh_l.astype(F32)).astype(BF16)
    br_l = jnp.broadcast_to(jnp.concatenate([b_re[0], b_rg[0], jnp.zeros((n_route_pad,), F32)])[:, None],
                            (ROUTE_ROWS, LANES))
    ctab, stab = _rope_lane_tables(N_META + seq)

    proj_w = (row(ln_in_g), row(ln_in_b), w_in_l, row(q_norm_g[0]), wuq_l, row(kv_norm_g[0]), wuk_l, wvt_l)
    u_m, _, k_m, vt_m = _proj_call(meta_tokens.astype(F32), N_META, 1, *proj_w, ctab[:N_META], stab[:N_META])
    u, q, k, vt = _proj_call(x.reshape(t, d), PROJ_TM, seq // PROJ_TM, *proj_w, ctab[N_META:], stab[N_META:])

    o = _attn_call(q.reshape(bsz, seq, -1), k.reshape(bsz, seq, -1),
                   vt.reshape(bsz, seq // ATT_T, D_ATTN, ATT_T), k_m, vt_m)

    pre = jnp.concatenate([jnp.zeros((HALO - N_META, C_CONV), F32), u_m], axis=0)
    cw = jnp.concatenate([conv_w[0], jnp.zeros((HALO - CONV_K, C_CONV), F32)], axis=0)
    h1, hp, ri, rw, cnt = _mix_call(x, row(ln_in_g), row(ln_in_b), u.reshape(bsz, seq, C_CONV), pre, cw,
                                    row(conv_b[0]), row(conv_ln_g[0]), row(conv_ln_b[0]), row(conv_out_g[0]),
                                    o, row(attn_out_g[0]), wo_l, row(ln1_g[0]), row(ln1_b[0]), wrh_l, wrl_l, br_l)

    counts = cnt[:N_EXPERTS, 0].astype(jnp.int32)
    padded = (counts + MOE_BM - 1) // MOE_BM * MOE_BM
    pad_end = jnp.cumsum(padded)
    pad_start = pad_end - padded
    n_blocks = (2 * t) // MOE_BM + N_EXPERTS
    dest, blk = _tables_call(pad_start, pad_end, counts, ri, n_blocks)
    blk_e, blk_valid = blk[0, :n_blocks], blk[1, :n_blocks]

    h1f = h1.reshape(t, d)
    dest0, dest1 = dest[0], dest[1]
    xb = _sc_scatter_rows(hp.reshape(t, HALF_D), dest0, dest1, n_blocks * MOE_BM)
    yb = _expert_call(blk_e, blk_valid, xb, w_gate, w_up, w_down)
    yg = _sc_gather_rows(yb, jnp.concatenate([dest0, dest1]))
    out = _combine_rows_call(h1f, rw.reshape(t, 8), row(ln2_g[0]), row(ln2_b[0]), yg)
    return out.reshape(bsz, seq, d)
```

```python
import functools
import math

import numpy as np
import jax
import jax.numpy as jnp
from jax import lax
from jax.experimental import pallas as pl
from jax.experimental.pallas import tpu as pltpu
from jax.experimental.pallas import tpu_sc as plsc

F32 = jnp.float32
BF16 = jnp.bfloat16

D_MODEL = 1024
HALF_D = D_MODEL // 2
N_META = 16
CHUNK = 64
C_CONV = 512
CONV_K = 31
N_HEADS = 8
QK_NOPE = 64
QK_ROPE = 32
V_DIM = 64
Q_LORA = 384
KV_LORA = 256
ROPE_THETA = 10000.0
D_ATTN = N_HEADS * V_DIM
N_GROUPS = 4
EXPERTS_PER_GROUP = 8
N_EXPERTS = N_GROUPS * EXPERTS_PER_GROUP
D_EXPERT = 256
DEEPNORM_ALPHA = 2.0 ** 0.25
EPS = 1e-5
NEG_INF = -1e30

LANES = 128
SUBLANES = 8
HEAD_PAD = LANES
ROPE_HALF = QK_ROPE // 2
X2_LANE = 64
Z_COLS = 2 * C_CONV + Q_LORA + KV_LORA + LANES
Q_SCALE = (QK_NOPE + QK_ROPE) ** -0.5 * math.log2(math.e)

PROJ_TM = 1024
ATT_T = 256
ATT_HEADS = 8
MIX_TM = 512
ROUTE_ROWS = 48
HALO = 32
MOE_BM = 1024
MOE_SUB = 512
COMB_TM = 1024
SC_WINDOW = 64
VMEM_LIMIT = 56 * 1024 * 1024


def _ln(x, g, b):
    mu = jnp.mean(x, axis=-1, keepdims=True)
    xc = x - mu
    var = jnp.mean(xc * xc, axis=-1, keepdims=True)
    return xc * lax.rsqrt(var + EPS) * g + b


def _rms(x, g):
    return x * lax.rsqrt(jnp.mean(x * x, axis=-1, keepdims=True) + EPS) * g


def _sigmoid(x):
    return 1.0 / (1.0 + jnp.exp(-x))


def _pack_bf16_pair(lo, hi):
    lo_bits = lax.bitcast_convert_type(lo.astype(BF16).astype(F32), jnp.uint32)
    hi_bits = lax.bitcast_convert_type(hi.astype(BF16).astype(F32), jnp.uint32)
    return (lo_bits >> 16) | (hi_bits & jnp.uint32(0xFFFF0000))


def _unpack_bf16_pair(packed):
    lo = lax.bitcast_convert_type(packed << 16, F32)
    hi = lax.bitcast_convert_type(packed & jnp.uint32(0xFFFF0000), F32)
    return lo, hi


def _proj_kernel(x_ref, g_ref, b_ref, win_ref, qg_ref, wuq_ref, kvg_ref, wuk_ref, wvt_ref, c_ref, s_ref,
                 u_ref, q_ref, k_ref, vt_ref):
    n_part, part = vt_ref.shape[1], vt_ref.shape[3]
    zs = []
    for r in range(n_part):
        h = _ln(x_ref[r * part:(r + 1) * part, :], g_ref[...], b_ref[...])
        zs.append(jnp.dot(h.astype(BF16), win_ref[...], preferred_element_type=F32))
    c0 = 2 * C_CONV
    for r, z in enumerate(zs):
        rows = slice(r * part, (r + 1) * part)
        u_ref[rows, :] = z[:, :C_CONV] * _sigmoid(z[:, C_CONV:c0])
        cq = z[:, c0:c0 + Q_LORA]
        ckv = z[:, c0 + Q_LORA:c0 + Q_LORA + KV_LORA]
        kr = z[:, c0 + Q_LORA + KV_LORA:]
        cs = c_ref[rows, :]
        sn = s_ref[rows, :]
        q = jnp.dot(_rms(cq, qg_ref[...]).astype(BF16), wuq_ref[...], preferred_element_type=F32)
        kvn = _rms(ckv, kvg_ref[...]).astype(BF16)
        kn = jnp.dot(kvn, wuk_ref[...], preferred_element_type=F32)
        kr_rot = kr * cs + pltpu.roll(kr, X2_LANE, 1) * sn
        for hh in range(N_HEADS):
            sl = slice(HEAD_PAD * hh, HEAD_PAD * (hh + 1))
            qh = q[:, sl]
            qh = (qh * cs + pltpu.roll(qh, X2_LANE, 1) * sn) * Q_SCALE
            q_ref[rows, sl] = qh.astype(BF16)
            k_ref[rows, sl] = (kn[:, sl] + kr_rot).astype(BF16)
        vt = lax.dot_general(wvt_ref[...], kvn, (((1,), (1,)), ((), ())), preferred_element_type=F32)
        vt_ref[0, r] = vt.astype(BF16)


def _proj_call(x2d, tm, pos_blocks, ln_g, ln_b, w_in, qg, wuq, kvg, wuk, wvt, ctab, stab):
    t = x2d.shape[0]
    sub_w = min(tm, ATT_T)
    n_sub = tm // sub_w
    full = lambda shape: pl.BlockSpec(shape, lambda i: (0,) * len(shape))
    row = lambda w: pl.BlockSpec((tm, w), lambda i: (i, 0))
    tab = pl.BlockSpec((tm, LANES), lambda i: (i % pos_blocks, 0))
    return pl.pallas_call(
        _proj_kernel,
        grid=(t // tm,),
        in_specs=[row(D_MODEL), full((1, D_MODEL)), full((1, D_MODEL)), full((D_MODEL, Z_COLS)),
                  full((1, Q_LORA)), full((Q_LORA, N_HEADS * HEAD_PAD)),
                  full((1, KV_LORA)), full((KV_LORA, N_HEADS * HEAD_PAD)), full((D_ATTN, KV_LORA)), tab, tab],
        out_specs=[row(C_CONV), row(N_HEADS * HEAD_PAD), row(N_HEADS * HEAD_PAD),
                   pl.BlockSpec((1, n_sub, D_ATTN, sub_w), lambda i: (i, 0, 0, 0))],
        out_shape=[jax.ShapeDtypeStruct((t, C_CONV), F32),
                   jax.ShapeDtypeStruct((t, N_HEADS * HEAD_PAD), BF16),
                   jax.ShapeDtypeStruct((t, N_HEADS * HEAD_PAD), BF16),
                   jax.ShapeDtypeStruct((t // tm, n_sub, D_ATTN, sub_w), BF16)],
        compiler_params=pltpu.CompilerParams(dimension_semantics=("arbitrary",),
                                             vmem_limit_bytes=VMEM_LIMIT),
        name="proj",
    )(x2d, ln_g, ln_b, w_in, qg, wuq, kvg, wuk, wvt, ctab, stab)


def _attn_kernel(q_ref, k_ref, vt_ref, km_ref, vtm_ref, o_ref, s_sc):
    seq = q_ref.shape[1]
    nq = seq // ATT_T
    key_chunk = lax.broadcasted_iota(jnp.int32, (ATT_T, ATT_T), 0) // CHUNK
    qry_chunk = lax.broadcasted_iota(jnp.int32, (ATT_T, ATT_T), 1) // CHUNK
    diag_visible = key_chunk <= qry_chunk
    nt = (((1,), (1,)), ((), ()))
    heads = range(ATT_HEADS)
    hsl = [slice(HEAD_PAD * hh, HEAD_PAD * (hh + 1)) for hh in heads]
    vsl = [slice(V_DIM * hh, V_DIM * (hh + 1)) for hh in heads]

    def update(s, vt, m, l, acc):
        m_new = jnp.maximum(m, jnp.max(s, axis=0, keepdims=True))
        alpha = jnp.exp2(m - m_new)
        p = jnp.exp2(s - m_new)
        l = alpha * l + jnp.sum(p, axis=0, keepdims=True)
        acc = alpha * acc + jnp.dot(vt, p.astype(BF16), preferred_element_type=F32)
        return m_new, l, acc

    def scores(qi, j, hh):
        q0 = pl.multiple_of(qi * ATT_T, ATT_T)
        k0 = pl.multiple_of(j * ATT_T, ATT_T)
        return lax.dot_general(k_ref[0, pl.ds(k0, ATT_T), hsl[hh]], q_ref[0, pl.ds(q0, ATT_T), hsl[hh]], nt,
                               preferred_element_type=F32)

    for hh in heads:
        s_sc[0, hh] = scores(0, 0, hh)

    def q_body(qi, carry):
        row0 = pl.multiple_of(qi * ATT_T, ATT_T)
        slot = qi % 2
        state = []
        for hh in heads:
            state += [jnp.full((1, ATT_T), NEG_INF, F32), jnp.zeros((1, ATT_T), F32),
                      jnp.zeros((V_DIM, ATT_T), F32)]

        def kv_body(j, st):
            out = []
            s_next = scores(qi, j + 1, 0)
            for hh in heads:
                s_cur = s_sc[slot, hh]
                s_after = scores(qi, j + 1, hh + 1) if hh + 1 < ATT_HEADS else None
                out += update(s_cur, vt_ref[0, j, vsl[hh], :], *st[3 * hh:3 * hh + 3])
                s_sc[slot, hh] = s_next
                s_next = s_after
            return tuple(out)

        state = lax.fori_loop(0, qi, kv_body, tuple(state))
        q_next = jnp.minimum(qi + 1, nq - 1)
        sms = [lax.dot_general(km_ref[:, hsl[hh]], q_ref[0, pl.ds(row0, ATT_T), hsl[hh]], nt,
                               preferred_element_type=F32) for hh in heads]
        outs = []
        for hh in heads:
            m, l, acc = state[3 * hh:3 * hh + 3]
            s = jnp.where(diag_visible, s_sc[slot, hh], NEG_INF)
            sm = sms[hh]
            m_new = jnp.maximum(m, jnp.maximum(jnp.max(s, axis=0, keepdims=True),
                                               jnp.max(sm, axis=0, keepdims=True)))
            alpha = jnp.exp2(m - m_new)
            p = jnp.exp2(s - m_new)
            pm = jnp.exp2(sm - m_new)
            l = alpha * l + jnp.sum(p, axis=0, keepdims=True) + jnp.sum(pm, axis=0, keepdims=True)
            acc = (alpha * acc + jnp.dot(vt_ref[0, qi, vsl[hh], :], p.astype(BF16), preferred_element_type=F32)
                   + jnp.dot(vtm_ref[0, 0, vsl[hh], :], pm.astype(BF16), preferred_element_type=F32))
            outs.append(acc / l)
        for hh in heads:
            s_sc[1 - slot, hh] = scores(q_next, 0, hh)
        o_ref[0, pl.ds(row0, ATT_T), :] = jnp.concatenate(outs, axis=0).T
        return carry

    lax.fori_loop(0, nq, q_body, 0)


def _attn_call(q, k, vt, k_meta, vt_meta):
    b, seq, _ = q.shape
    nk = seq // ATT_T
    hb = ATT_HEADS
    return pl.pallas_call(
        _attn_kernel,
        grid=(b, N_HEADS // hb),
        in_specs=[pl.BlockSpec((1, seq, hb * HEAD_PAD), lambda i, j: (i, 0, j)),
                  pl.BlockSpec((1, seq, hb * HEAD_PAD), lambda i, j: (i, 0, j)),
                  pl.BlockSpec((1, nk, hb * V_DIM, ATT_T), lambda i, j: (i, 0, j, 0)),
                  pl.BlockSpec((N_META, hb * HEAD_PAD), lambda i, j: (0, j)),
                  pl.BlockSpec((1, 1, hb * V_DIM, N_META), lambda i, j: (0, 0, j, 0))],
        out_specs=pl.BlockSpec((1, seq, hb * V_DIM), lambda i, j: (i, 0, j)),
        out_shape=jax.ShapeDtypeStruct((b, seq, D_ATTN), F32),
        scratch_shapes=[pltpu.VMEM((2, hb, ATT_T, ATT_T), F32)],
        compiler_params=pltpu.CompilerParams(dimension_semantics=("arbitrary", "arbitrary"),
                                             vmem_limit_bytes=VMEM_LIMIT),
        name="attn",
    )(q, k, vt, k_meta, vt_meta)


def _mix_kernel(x_ref, ig_ref, ib_ref, u_ref, uh_ref, pre_ref, cw_ref, cb_ref, clg_ref, clb_ref,
                cog_ref, o_ref, aog_ref, wo_ref, l1g_ref, l1b_ref, wrh_ref, wrl_ref, br_ref, tri_ref,
                h1_ref, hp_ref, ri_ref, rw_ref, cnt_ref, win_sc, sh_sc, carry_sc):
    bi = pl.program_id(0)
    ti = pl.program_id(1)
    tm = MIX_TM

    @pl.when((bi == 0) & (ti == 0))
    def _():
        carry_sc[...] = jnp.zeros_like(carry_sc)

    win_sc[0:HALO, :] = jnp.where(ti == 0, pre_ref[...], uh_ref[0])
    win_sc[HALO:, :] = u_ref[0]
    first = HALO - (CONV_K - 1)
    for res in range(1, SUBLANES):
        rows = tm + HALO - SUBLANES
        sh_sc[res - 1, 0:rows, :] = win_sc[pl.ds(res, rows), :]

    acc = jnp.broadcast_to(cb_ref[...], (tm, C_CONV))
    for o in range(first, first + CONV_K):
        res, lo = o % SUBLANES, o - o % SUBLANES
        src = sh_sc[res - 1, lo:lo + tm, :] if res else win_sc[lo:lo + tm, :]
        acc = acc + src * cw_ref[o - first:o - first + 1, :]
    c = _ln(acc, clg_ref[...], clb_ref[...])
    c = c * _sigmoid(c)
    cn = _rms(c, cog_ref[...])
    on = _rms(o_ref[0], aog_ref[...])
    y = (jnp.dot(cn.astype(BF16), wo_ref[0:C_CONV, :], preferred_element_type=F32)
         + jnp.dot(on.astype(BF16), wo_ref[C_CONV:, :], preferred_element_type=F32))
    h = _ln(x_ref[0], ig_ref[...], ib_ref[...])
    h1 = _ln(DEEPNORM_ALPHA * h + y, l1g_ref[...], l1b_ref[...])
    h1_ref[0] = h1
    hp_ref[0] = _pack_bf16_pair(h1[:, :HALF_D], h1[:, HALF_D:])

    nt = (((1,), (1,)), ((), ()))
    h_hi = h1.astype(BF16)
    h_lo = (h1 - h_hi.astype(F32)).astype(BF16)
    lt = (lax.dot_general(wrh_ref[...], h_hi, nt, preferred_element_type=F32)
          + lax.dot_general(wrl_ref[...], h_hi, nt, preferred_element_type=F32)
          + lax.dot_general(wrh_ref[...], h_lo, nt, preferred_element_type=F32)) + br_ref[:, 0:1]
    row = lax.broadcasted_iota(jnp.int32, (ROUTE_ROWS, tm), 0)
    big = jnp.int32(1 << 20)
    is_g = (row >= N_EXPERTS) & (row < N_EXPERTS + N_GROUPS)
    gl = jnp.where(is_g, lt, -jnp.inf)
    gmax = jnp.max(gl, axis=0, keepdims=True)
    g_idx = jnp.min(jnp.where(gl == gmax, row - N_EXPERTS, big), axis=0, keepdims=True)
    g_w = 1.0 / jnp.sum(jnp.exp(gl - gmax), axis=0, keepdims=True)
    in_grp = (row < N_EXPERTS) & ((row // EXPERTS_PER_GROUP) == g_idx)
    el = jnp.where(in_grp, lt, -jnp.inf)
    v1 = jnp.max(el, axis=0, keepdims=True)
    i1 = jnp.min(jnp.where(el == v1, row, big), axis=0, keepdims=True)
    el2 = jnp.where(row == i1, -jnp.inf, el)
    v2 = jnp.max(el2, axis=0, keepdims=True)
    i2 = jnp.min(jnp.where(el2 == v2, row, big), axis=0, keepdims=True)
    e2 = jnp.exp(v2 - v1)
    w1 = g_w / (1.0 + e2)
    w2 = g_w * e2 / (1.0 + e2)

    oh1 = row == i1
    oh2 = row == i2
    oh = jnp.where(oh1 | oh2, 1.0, 0.0)
    before = jnp.dot(oh.astype(BF16), tri_ref[...], preferred_element_type=F32) + carry_sc[:, 0:1]
    r1 = jnp.sum(jnp.where(oh1, before, 0.0), axis=0, keepdims=True).astype(jnp.int32)
    r2 = jnp.sum(jnp.where(oh2, before, 0.0), axis=0, keepdims=True).astype(jnp.int32)
    carry_sc[...] = carry_sc[...] + jnp.sum(oh, axis=1, keepdims=True)
    cnt_ref[...] = carry_sc[...]

    sub = lax.broadcasted_iota(jnp.int32, (SUBLANES, tm), 0)
    ri_ref[...] = jnp.where(sub == 0, i1, jnp.where(sub == 1, i2, jnp.where(sub == 2, r1,
                            jnp.where(sub == 3, r2, 0))))
    lrow = lax.broadcasted_iota(jnp.int32, (LANES, tm), 0)
    rw_ref[0] = jnp.where(lrow == 0, w1, jnp.where(lrow == 1, w2, 0.0)).T[:, :8]


def _mix_call(x, ig, ib, u, pre, cw, cb, clg, clb, cog, o, aog, wo, l1g, l1b, wrh, wrl, br):
    b, seq, _ = x.shape
    tm = MIX_TM
    nt = seq // tm
    full = lambda shape: pl.BlockSpec(shape, lambda i, j: (0,) * len(shape))
    tile = lambda w: pl.BlockSpec((1, tm, w), lambda i, j: (i, j, 0))
    halo = pl.BlockSpec((1, HALO, C_CONV), lambda i, j: (i, jnp.maximum(j * (tm // HALO) - 1, 0), 0))
    earlier = jnp.asarray(np.triu(np.ones((tm, tm), np.float32), k=1), BF16)
    return pl.pallas_call(
        _mix_kernel,
        grid=(b, nt),
        in_specs=[tile(D_MODEL), full((1, D_MODEL)), full((1, D_MODEL)),
                  tile(C_CONV), halo, full((HALO, C_CONV)), full((HALO, C_CONV)), full((1, C_CONV)),
                  full((1, C_CONV)), full((1, C_CONV)), full((1, C_CONV)),
                  tile(D_ATTN), full((1, D_ATTN)), full((D_MODEL, D_MODEL)),
                  full((1, D_MODEL)), full((1, D_MODEL)), full((ROUTE_ROWS, D_MODEL)),
                  full((ROUTE_ROWS, D_MODEL)), full((ROUTE_ROWS, LANES)), full((tm, tm))],
        out_specs=[tile(D_MODEL), tile(HALF_D), pl.BlockSpec((8, tm), lambda i, j: (0, i * nt + j)), tile(8),
                   full((ROUTE_ROWS, LANES))],
        out_shape=[jax.ShapeDtypeStruct((b, seq, D_MODEL), F32),
                   jax.ShapeDtypeStruct((b, seq, HALF_D), jnp.uint32),
                   jax.ShapeDtypeStruct((8, b * seq), jnp.int32),
                   jax.ShapeDtypeStruct((b, seq, 8), F32),
                   jax.ShapeDtypeStruct((ROUTE_ROWS, LANES), F32)],
        scratch_shapes=[pltpu.VMEM((HALO + tm, C_CONV), F32),
                        pltpu.VMEM((SUBLANES - 1, HALO + tm, C_CONV), F32),
                        pltpu.VMEM((ROUTE_ROWS, LANES), F32)],
        compiler_params=pltpu.CompilerParams(dimension_semantics=("arbitrary", "arbitrary"),
                                             vmem_limit_bytes=VMEM_LIMIT),
        name="mix",
    )(x, ig, ib, u, u, pre, cw, cb, clg, clb, cog, o, aog, wo, l1g, l1b, wrh, wrl, br, earlier)


def _tables_kernel(ps_ref, pe_ref, cn_ref, ri_ref, dest_ref, blk_ref):
    ri = ri_ref[...]
    start = jnp.zeros_like(ri)
    for e in range(N_EXPERTS):
        start = jnp.where(ri == e, ps_ref[e], start)
    dest_ref[...] = jnp.zeros_like(dest_ref)
    dest_ref[0:2, :] = start[0:2, :] + ri_ref[2:4, :]
    blk_start = lax.broadcasted_iota(jnp.int32, blk_ref.shape, 1) * MOE_BM
    be = jnp.zeros(blk_ref.shape, jnp.int32)
    for e in range(N_EXPERTS):
        be = be + jnp.where(pe_ref[e] <= blk_start, 1, 0)
    be = jnp.minimum(be, N_EXPERTS - 1)
    end = jnp.zeros(blk_ref.shape, jnp.int32)
    for e in range(N_EXPERTS):
        end = jnp.where(be == e, ps_ref[e] + cn_ref[e], end)
    valid = jnp.clip(end - blk_start, 0, MOE_BM)
    row = lax.broadcasted_iota(jnp.int32, blk_ref.shape, 0)
    blk_ref[...] = jnp.where(row == 0, be, valid)


def _tables_call(pad_start, pad_end, counts, ri, n_blocks):
    t = ri.shape[1]
    blk_w = -(-n_blocks // LANES) * LANES
    return pl.pallas_call(
        _tables_kernel,
        grid_spec=pltpu.PrefetchScalarGridSpec(
            num_scalar_prefetch=3,
            grid=(1,),
            in_specs=[pl.BlockSpec((SUBLANES, t), lambda i, *_: (0, 0))],
            out_specs=[pl.BlockSpec((SUBLANES, t), lambda i, *_: (0, 0)),
                       pl.BlockSpec((SUBLANES, blk_w), lambda i, *_: (0, 0))]),
        out_shape=[jax.ShapeDtypeStruct((SUBLANES, t), jnp.int32),
                   jax.ShapeDtypeStruct((SUBLANES, blk_w), jnp.int32)],
        compiler_params=pltpu.CompilerParams(dimension_semantics=("arbitrary",),
                                             vmem_limit_bytes=VMEM_LIMIT),
        name="tables",
    )(pad_start, pad_end, counts, ri)


def _expert_kernel(be_ref, nv_ref, x_ref, wg_ref, wu_ref, wd_ref, y_ref, wg_sc, wu_sc, wd_sc):
    i = pl.program_id(0)
    nv = nv_ref[i]

    @pl.when((i == 0) | (be_ref[i] != be_ref[jnp.maximum(i - 1, 0)]))
    def _():
        wg_sc[...] = wg_ref[0, 0].astype(BF16)
        wu_sc[...] = wu_ref[0, 0].astype(BF16)
        wd_sc[...] = wd_ref[0, 0].astype(BF16)

    rowid = lax.broadcasted_iota(jnp.int32, (MOE_SUB, HALF_D), 0)
    for r0 in range(0, x_ref.shape[0], MOE_SUB):
        rows = slice(r0, r0 + MOE_SUB)

        @pl.when(nv > r0)
        def _():
            x_lo, x_hi = _unpack_bf16_pair(jnp.where(rowid < nv - r0, x_ref[rows, :], jnp.uint32(0)))
            x_lo = x_lo.astype(BF16)
            x_hi = x_hi.astype(BF16)
            g = (jnp.dot(x_lo, wg_sc[:HALF_D, :], preferred_element_type=F32)
                 + jnp.dot(x_hi, wg_sc[HALF_D:, :], preferred_element_type=F32))
            up = (jnp.dot(x_lo, wu_sc[:HALF_D, :], preferred_element_type=F32)
                  + jnp.dot(x_hi, wu_sc[HALF_D:, :], preferred_element_type=F32))
            act = g * _sigmoid(g) * up
            y = jnp.dot(act.astype(BF16), wd_sc[...], preferred_element_type=F32)
            y_ref[rows, :] = _pack_bf16_pair(y[:, :HALF_D], y[:, HALF_D:])

        @pl.when(nv <= r0)
        def _():
            y_ref[rows, :] = jnp.zeros((MOE_SUB, HALF_D), jnp.uint32)


def _expert_call(blk_e, blk_valid, xb, w_gate, w_up, w_down):
    r = xb.shape[0]
    bm = MOE_BM
    wspec = lambda a, b: pl.BlockSpec((1, 1, a, b), lambda i, be, nv: (0, be[i], 0, 0))
    return pl.pallas_call(
        _expert_kernel,
        grid_spec=pltpu.PrefetchScalarGridSpec(
            num_scalar_prefetch=2,
            grid=(r // bm,),
            in_specs=[pl.BlockSpec((bm, HALF_D), lambda i, be, nv: (i, 0)),
                      wspec(D_MODEL, D_EXPERT), wspec(D_MODEL, D_EXPERT), wspec(D_EXPERT, D_MODEL)],
            out_specs=pl.BlockSpec((bm, HALF_D), lambda i, be, nv: (i, 0)),
            scratch_shapes=[pltpu.VMEM((D_MODEL, D_EXPERT), BF16), pltpu.VMEM((D_MODEL, D_EXPERT), BF16),
                            pltpu.VMEM((D_EXPERT, D_MODEL), BF16)]),
        out_shape=jax.ShapeDtypeStruct((r, HALF_D), jnp.uint32),
        compiler_params=pltpu.CompilerParams(dimension_semantics=("arbitrary",),
                                             vmem_limit_bytes=VMEM_LIMIT),
        name="experts",
    )(blk_e, blk_valid, xb, w_gate, w_up, w_down)


def _sc_gather_rows(table, idx):
    m = idx.shape[0]
    width = table.shape[1]
    sc = plsc.get_sparse_core_info()
    workers = sc.num_cores * sc.num_subcores
    per_worker = m // workers
    mesh = plsc.VectorSubcoreMesh(core_axis_name="c", subcore_axis_name="s")

    n_win = per_worker // SC_WINDOW
    assert n_win % 2 == 0 and n_win >= 2
    buf = lambda: [pltpu.VMEM((SC_WINDOW,), jnp.int32), pltpu.VMEM((SC_WINDOW, width), table.dtype),
                   pltpu.SemaphoreType.DMA]

    @functools.partial(pl.kernel, mesh=mesh, out_type=jax.ShapeDtypeStruct((m, width), table.dtype),
                       scratch_types=buf() + buf())
    def gather(table_hbm, idx_hbm, out_hbm, idx_a, rows_a, sem_a, idx_b, rows_b, sem_b):
        base = (lax.axis_index("s") * sc.num_cores + lax.axis_index("c")) * per_worker
        bufs = ((idx_a, rows_a, sem_a), (idx_b, rows_b, sem_b))

        def fetch(win, which):
            idx_v, rows_v, sem = bufs[which]
            pltpu.sync_copy(idx_hbm.at[pl.ds(base + win * SC_WINDOW, SC_WINDOW)], idx_v)
            pltpu.async_copy(table_hbm.at[idx_v], rows_v, sem)

        def flush(win, which):
            idx_v, rows_v, sem = bufs[which]
            pltpu.make_async_copy(table_hbm.at[idx_v], rows_v, sem).wait()
            pltpu.sync_copy(rows_v, out_hbm.at[pl.ds(base + win * SC_WINDOW, SC_WINDOW)])

        fetch(0, 0)

        @pl.loop(0, n_win // 2 - 1)
        def _(pair):
            win = 2 * pair
            fetch(win + 1, 1)
            flush(win, 0)
            fetch(win + 2, 0)
            flush(win + 1, 1)

        fetch(n_win - 1, 1)
        flush(n_win - 2, 0)
        flush(n_win - 1, 1)

    return gather(table, idx)


def _sc_scatter_rows(rows, idx0, idx1, n_out):
    m, width = rows.shape
    sc = plsc.get_sparse_core_info()
    workers = sc.num_cores * sc.num_subcores
    per_worker = m // workers
    mesh = plsc.VectorSubcoreMesh(core_axis_name="c", subcore_axis_name="s")

    n_win = per_worker // SC_WINDOW
    assert n_win % 2 == 0 and n_win >= 2
    buf = lambda: [pltpu.VMEM((SC_WINDOW,), jnp.int32), pltpu.VMEM((SC_WINDOW,), jnp.int32),
                   pltpu.VMEM((SC_WINDOW, width), rows.dtype),
                   pltpu.SemaphoreType.DMA, pltpu.SemaphoreType.DMA, pltpu.SemaphoreType.DMA]

    @functools.partial(pl.kernel, mesh=mesh, out_type=jax.ShapeDtypeStruct((n_out, width), rows.dtype),
                       scratch_types=buf() + buf())
    def scatter(rows_hbm, idx0_hbm, idx1_hbm, out_hbm, *scratch):
        base = (lax.axis_index("s") * sc.num_cores + lax.axis_index("c")) * per_worker
        bufs = (scratch[:6], scratch[6:])

        def load(win, which):
            idx0_v, idx1_v, rows_v, lsem, _, _ = bufs[which]
            off = base + win * SC_WINDOW
            pltpu.sync_copy(idx0_hbm.at[pl.ds(off, SC_WINDOW)], idx0_v)
            pltpu.sync_copy(idx1_hbm.at[pl.ds(off, SC_WINDOW)], idx1_v)
            pltpu.async_copy(rows_hbm.at[pl.ds(off, SC_WINDOW)], rows_v, lsem)

        def send(which):
            idx0_v, idx1_v, rows_v, lsem, sem0, sem1 = bufs[which]
            pltpu.make_async_copy(rows_hbm.at[pl.ds(0, SC_WINDOW)], rows_v, lsem).wait()
            first = pltpu.async_copy(rows_v, out_hbm.at[idx0_v], sem0)
            second = pltpu.async_copy(rows_v, out_hbm.at[idx1_v], sem1)
            first.wait()
            second.wait()

        load(0, 0)

        @pl.loop(0, n_win // 2 - 1)
        def _(pair):
            win = 2 * pair
            load(win + 1, 1)
            send(0)
            load(win + 2, 0)
            send(1)

        load(n_win - 1, 1)
        send(0)
        send(1)

    return scatter(rows, idx0, idx1)


def _combine_rows_kernel(h_ref, w_ref, g_ref, b_ref, y0_ref, y1_ref, o_ref):
    w = w_ref[...]
    y0_lo, y0_hi = _unpack_bf16_pair(y0_ref[...])
    y1_lo, y1_hi = _unpack_bf16_pair(y1_ref[...])
    y = jnp.concatenate([w[:, 0:1] * y0_lo + w[:, 1:2] * y1_lo, w[:, 0:1] * y0_hi + w[:, 1:2] * y1_hi], axis=1)
    o_ref[...] = _ln(DEEPNORM_ALPHA * h_ref[...] + y, g_ref[...], b_ref[...])


def _combine_rows_call(h1, rw, g, b, yg):
    t = h1.shape[0]
    tm = COMB_TM
    n = t // tm
    return pl.pallas_call(
        _combine_rows_kernel,
        grid=(n,),
        in_specs=[pl.BlockSpec((tm, D_MODEL), lambda i: (i, 0)),
                  pl.BlockSpec((tm, 8), lambda i: (i, 0)),
                  pl.BlockSpec((1, D_MODEL), lambda i: (0, 0)),
                  pl.BlockSpec((1, D_MODEL), lambda i: (0, 0)),
                  pl.BlockSpec((tm, HALF_D), lambda i: (i, 0)),
                  pl.BlockSpec((tm, HALF_D), lambda i: (n + i, 0))],
        out_specs=pl.BlockSpec((tm, D_MODEL), lambda i: (i, 0)),
        out_shape=jax.ShapeDtypeStruct((t, D_MODEL), F32),
        compiler_params=pltpu.CompilerParams(dimension_semantics=("arbitrary",),
                                             vmem_limit_bytes=VMEM_LIMIT),
        name="combine_rows",
    )(h1, rw, g, b, yg, yg)


def _take_cols(w, idx):
    wz = jnp.concatenate([w, jnp.zeros((w.shape[0], 1), w.dtype)], axis=1)
    return wz[:, np.where(idx < 0, w.shape[1], idx)]


def _layout_indices():
    lane_src = np.full((HEAD_PAD,), -1, np.int64)
    lane_src[0:ROPE_HALF] = QK_NOPE + np.arange(ROPE_HALF)
    lane_src[ROPE_HALF:X2_LANE] = np.arange(X2_LANE - ROPE_HALF)
    lane_src[X2_LANE:X2_LANE + ROPE_HALF] = QK_NOPE + ROPE_HALF + np.arange(ROPE_HALF)
    n_lo = X2_LANE - ROPE_HALF
    lane_src[X2_LANE + ROPE_HALF:X2_LANE + ROPE_HALF + QK_NOPE - n_lo] = n_lo + np.arange(QK_NOPE - n_lo)
    q_idx = np.concatenate([np.where(lane_src >= 0, lane_src + (QK_NOPE + QK_ROPE) * h, -1)
                            for h in range(N_HEADS)])
    nope_src = np.where(lane_src < QK_NOPE, lane_src, -1)
    k_idx = np.concatenate([np.where(nope_src >= 0, nope_src + (QK_NOPE + V_DIM) * h, -1)
                            for h in range(N_HEADS)])
    v_idx = np.concatenate([QK_NOPE + np.arange(V_DIM) + (QK_NOPE + V_DIM) * h for h in range(N_HEADS)])
    kr_idx = np.full((LANES,), -1, np.int64)
    kr_idx[0:ROPE_HALF] = np.arange(ROPE_HALF)
    kr_idx[X2_LANE:X2_LANE + ROPE_HALF] = ROPE_HALF + np.arange(ROPE_HALF)
    return q_idx, k_idx, v_idx, kr_idx


def _rope_lane_tables(length):
    inv_freq = np.float32(ROPE_THETA) ** (-np.arange(ROPE_HALF, dtype=np.float32) / np.float32(ROPE_HALF))
    ang = np.arange(length, dtype=np.float32)[:, None] * inv_freq[None, :].astype(np.float32)
    cos, sin = np.cos(ang).astype(np.float32), np.sin(ang).astype(np.float32)
    ctab = np.ones((length, LANES), np.float32)
    stab = np.zeros((length, LANES), np.float32)
    for lo, sign in ((0, -1.0), (X2_LANE, 1.0)):
        ctab[:, lo:lo + ROPE_HALF] = cos
        stab[:, lo:lo + ROPE_HALF] = sign * sin
    return jnp.asarray(ctab), jnp.asarray(stab)


def kernel(x, meta_tokens, ln_in_g, ln_in_b, w_in, conv_w, conv_b, conv_ln_g, conv_ln_b, q_norm_g, w_uq,
           kv_norm_g, w_ukv, conv_out_g, attn_out_g, w_o, ln1_g, ln1_b, w_rg, b_rg, w_re, b_re,
           w_gate, w_up, w_down, ln2_g, ln2_b):
    assert w_in.shape[0] == 1, "single layer"
    bsz, seq, d = x.shape
    t = bsz * seq
    row = lambda a: a.reshape(1, -1)

    q_idx, k_idx, v_idx, kr_idx = _layout_indices()
    c4 = 2 * C_CONV + Q_LORA + KV_LORA
    w_in_l = jnp.concatenate([w_in[0][:, :c4], _take_cols(w_in[0][:, c4:], kr_idx)], axis=1).astype(BF16)
    wuq_l = _take_cols(w_uq[0], q_idx).astype(BF16)
    wuk_l = _take_cols(w_ukv[0], k_idx).astype(BF16)
    wvt_l = w_ukv[0][:, v_idx].T.astype(BF16)
    wo_l = w_o[0].astype(BF16)
    n_route_pad = ROUTE_ROWS - N_EXPERTS - N_GROUPS
    wr_t = jnp.concatenate([w_re[0].T, w_rg[0].T, jnp.zeros((n_route_pad, d), F32)], axis=0)
    wrh_l = wr_t.astype(BF16)
    wrl_l = (wr_t - wrh_l.astype(F32)).astype(BF16)
    br_l = jnp.broadcast_to(jnp.concatenate([b_re[0], b_rg[0], jnp.zeros((n_route_pad,), F32)])[:, None],
                            (ROUTE_ROWS, LANES))
    ctab, stab = _rope_lane_tables(N_META + seq)

    proj_w = (row(ln_in_g), row(ln_in_b), w_in_l, row(q_norm_g[0]), wuq_l, row(kv_norm_g[0]), wuk_l, wvt_l)
    u_m, _, k_m, vt_m = _proj_call(meta_tokens.astype(F32), N_META, 1, *proj_w, ctab[:N_META], stab[:N_META])
    u, q, k, vt = _proj_call(x.reshape(t, d), PROJ_TM, seq // PROJ_TM, *proj_w, ctab[N_META:], stab[N_META:])

    o = _attn_call(q.reshape(bsz, seq, -1), k.reshape(bsz, seq, -1),
                   vt.reshape(bsz, seq // ATT_T, D_ATTN, ATT_T), k_m, vt_m)

    pre = jnp.concatenate([jnp.zeros((HALO - N_META, C_CONV), F32), u_m], axis=0)
    cw = jnp.concatenate([conv_w[0], jnp.zeros((HALO - CONV_K, C_CONV), F32)], axis=0)
    h1, hp, ri, rw, cnt = _mix_call(x, row(ln_in_g), row(ln_in_b), u.reshape(bsz, seq, C_CONV), pre, cw,
                                    row(conv_b[0]), row(conv_ln_g[0]), row(conv_ln_b[0]), row(conv_out_g[0]),
                                    o, row(attn_out_g[0]), wo_l, row(ln1_g[0]), row(ln1_b[0]), wrh_l, wrl_l, br_l)

    counts = cnt[:N_EXPERTS, 0].astype(jnp.int32)
    padded = (counts + MOE_BM - 1) // MOE_BM * MOE_BM
    pad_end = jnp.cumsum(padded)
    pad_start = pad_end - padded
    n_blocks = (2 * t) // MOE_BM + N_EXPERTS
    dest, blk = _tables_call(pad_start, pad_end, counts, ri, n_blocks)
    blk_e, blk_valid = blk[0, :n_blocks], blk[1, :n_blocks]

    h1f = h1.reshape(t, d)
    dest0, dest1 = dest[0], dest[1]
    xb = _sc_scatter_rows(hp.reshape(t, HALF_D), dest0, dest1, n_blocks * MOE_BM)
    yb = _expert_call(blk_e, blk_valid, xb, w_gate, w_up, w_down)
    yg = _sc_gather_rows(yb, jnp.concatenate([dest0, dest1]))
    out = _combine_rows_call(h1f, rw.reshape(t, 8), row(ln2_g[0]), row(ln2_b[0]), yg)
    return out.reshape(bsz, seq, d)
```

```python
import functools
import math

import numpy as np
import jax
import jax.numpy as jnp
from jax import lax
from jax.experimental import pallas as pl
from jax.experimental.pallas import tpu as pltpu
from jax.experimental.pallas import tpu_sc as plsc

F32 = jnp.float32
BF16 = jnp.bfloat16

D_MODEL = 1024
HALF_D = D_MODEL // 2
N_META = 16
CHUNK = 64
C_CONV = 512
CONV_K = 31
N_HEADS = 8
QK_NOPE = 64
QK_ROPE = 32
V_DIM = 64
Q_LORA = 384
KV_LORA = 256
ROPE_THETA = 10000.0
D_ATTN = N_HEADS * V_DIM
N_GROUPS = 4
EXPERTS_PER_GROUP = 8
N_EXPERTS = N_GROUPS * EXPERTS_PER_GROUP
D_EXPERT = 256
DEEPNORM_ALPHA = 2.0 ** 0.25
EPS = 1e-5
NEG_INF = -1e30

LANES = 128
SUBLANES = 8
HEAD_PAD = LANES
ROPE_HALF = QK_ROPE // 2
X2_LANE = 64
Z_COLS = 2 * C_CONV + Q_LORA + KV_LORA + LANES
Q_SCALE = (QK_NOPE + QK_ROPE) ** -0.5 * math.log2(math.e)

PROJ_TM = 1024
ATT_T = 256
ATT_HEADS = 8
MIX_TM = 512
ROUTE_ROWS = 48
HALO = 32
MOE_BM = 1024
COMB_TM = 1024
SC_WINDOW = 64
VMEM_LIMIT = 56 * 1024 * 1024


def _ln(x, g, b):
    mu = jnp.mean(x, axis=-1, keepdims=True)
    xc = x - mu
    var = jnp.mean(xc * xc, axis=-1, keepdims=True)
    return xc * lax.rsqrt(var + EPS) * g + b


def _rms(x, g):
    return x * lax.rsqrt(jnp.mean(x * x, axis=-1, keepdims=True) + EPS) * g


def _sigmoid(x):
    return 1.0 / (1.0 + jnp.exp(-x))


def _pack_bf16_pair(lo, hi):
    lo_bits = lax.bitcast_convert_type(lo.astype(BF16).astype(F32), jnp.uint32)
    hi_bits = lax.bitcast_convert_type(hi.astype(BF16).astype(F32), jnp.uint32)
    return (lo_bits >> 16) | (hi_bits & jnp.uint32(0xFFFF0000))


def _unpack_bf16_pair(packed):
    lo = lax.bitcast_convert_type(packed << 16, F32)
    hi = lax.bitcast_convert_type(packed & jnp.uint32(0xFFFF0000), F32)
    return lo, hi


def _proj_kernel(x_ref, g_ref, b_ref, win_ref, qg_ref, wuq_ref, kvg_ref, wuk_ref, wvt_ref, c_ref, s_ref,
                 u_ref, q_ref, k_ref, vt_ref):
    n_part, part = vt_ref.shape[1], vt_ref.shape[3]
    zs = []
    for r in range(n_part):
        h = _ln(x_ref[r * part:(r + 1) * part, :], g_ref[...], b_ref[...])
        zs.append(jnp.dot(h.astype(BF16), win_ref[...], preferred_element_type=F32))
    c0 = 2 * C_CONV
    for r, z in enumerate(zs):
        rows = slice(r * part, (r + 1) * part)
        u_ref[rows, :] = z[:, :C_CONV] * _sigmoid(z[:, C_CONV:c0])
        cq = z[:, c0:c0 + Q_LORA]
        ckv = z[:, c0 + Q_LORA:c0 + Q_LORA + KV_LORA]
        kr = z[:, c0 + Q_LORA + KV_LORA:]
        cs = c_ref[rows, :]
        sn = s_ref[rows, :]
        q = jnp.dot(_rms(cq, qg_ref[...]).astype(BF16), wuq_ref[...], preferred_element_type=F32)
        kvn = _rms(ckv, kvg_ref[...]).astype(BF16)
        kn = jnp.dot(kvn, wuk_ref[...], preferred_element_type=F32)
        kr_rot = kr * cs + pltpu.roll(kr, X2_LANE, 1) * sn
        for hh in range(N_HEADS):
            sl = slice(HEAD_PAD * hh, HEAD_PAD * (hh + 1))
            qh = q[:, sl]
            qh = (qh * cs + pltpu.roll(qh, X2_LANE, 1) * sn) * Q_SCALE
            q_ref[rows, sl] = qh.astype(BF16)
            k_ref[rows, sl] = (kn[:, sl] + kr_rot).astype(BF16)
        vt = lax.dot_general(wvt_ref[...], kvn, (((1,), (1,)), ((), ())), preferred_element_type=F32)
        vt_ref[0, r] = vt.astype(BF16)


def _proj_call(x2d, tm, pos_blocks, ln_g, ln_b, w_in, qg, wuq, kvg, wuk, wvt, ctab, stab):
    t = x2d.shape[0]
    sub_w = min(tm, ATT_T)
    n_sub = tm // sub_w
    full = lambda shape: pl.BlockSpec(shape, lambda i: (0,) * len(shape))
    row = lambda w: pl.BlockSpec((tm, w), lambda i: (i, 0))
    tab = pl.BlockSpec((tm, LANES), lambda i: (i % pos_blocks, 0))
    return pl.pallas_call(
        _proj_kernel,
        grid=(t // tm,),
        in_specs=[row(D_MODEL), full((1, D_MODEL)), full((1, D_MODEL)), full((D_MODEL, Z_COLS)),
                  full((1, Q_LORA)), full((Q_LORA, N_HEADS * HEAD_PAD)),
                  full((1, KV_LORA)), full((KV_LORA, N_HEADS * HEAD_PAD)), full((D_ATTN, KV_LORA)), tab, tab],
        out_specs=[row(C_CONV), row(N_HEADS * HEAD_PAD), row(N_HEADS * HEAD_PAD),
                   pl.BlockSpec((1, n_sub, D_ATTN, sub_w), lambda i: (i, 0, 0, 0))],
        out_shape=[jax.ShapeDtypeStruct((t, C_CONV), F32),
                   jax.ShapeDtypeStruct((t, N_HEADS * HEAD_PAD), BF16),
                   jax.ShapeDtypeStruct((t, N_HEADS * HEAD_PAD), BF16),
                   jax.ShapeDtypeStruct((t // tm, n_sub, D_ATTN, sub_w), BF16)],
        compiler_params=pltpu.CompilerParams(dimension_semantics=("arbitrary",),
                                             vmem_limit_bytes=VMEM_LIMIT),
        name="proj",
    )(x2d, ln_g, ln_b, w_in, qg, wuq, kvg, wuk, wvt, ctab, stab)


def _attn_kernel(q_ref, k_ref, vt_ref, km_ref, vtm_ref, o_ref, s_sc):
    seq = q_ref.shape[1]
    nq = seq // ATT_T
    key_chunk = lax.broadcasted_iota(jnp.int32, (ATT_T, ATT_T), 0) // CHUNK
    qry_chunk = lax.broadcasted_iota(jnp.int32, (ATT_T, ATT_T), 1) // CHUNK
    diag_visible = key_chunk <= qry_chunk
    nt = (((1,), (1,)), ((), ()))
    heads = range(ATT_HEADS)
    hsl = [slice(HEAD_PAD * hh, HEAD_PAD * (hh + 1)) for hh in heads]
    vsl = [slice(V_DIM * hh, V_DIM * (hh + 1)) for hh in heads]

    def update(s, vt, m, l, acc):
        m_new = jnp.maximum(m, jnp.max(s, axis=0, keepdims=True))
        alpha = jnp.exp2(m - m_new)
        p = jnp.exp2(s - m_new)
        l = alpha * l + jnp.sum(p, axis=0, keepdims=True)
        acc = alpha * acc + jnp.dot(vt, p.astype(BF16), preferred_element_type=F32)
        return m_new, l, acc

    def scores(qi, j, hh):
        q0 = pl.multiple_of(qi * ATT_T, ATT_T)
        k0 = pl.multiple_of(j * ATT_T, ATT_T)
        return lax.dot_general(k_ref[0, pl.ds(k0, ATT_T), hsl[hh]], q_ref[0, pl.ds(q0, ATT_T), hsl[hh]], nt,
                               preferred_element_type=F32)

    for hh in heads:
        s_sc[0, hh] = scores(0, 0, hh)

    def q_body(qi, carry):
        row0 = pl.multiple_of(qi * ATT_T, ATT_T)
        slot = qi % 2
        state = []
        for hh in heads:
            state += [jnp.full((1, ATT_T), NEG_INF, F32), jnp.zeros((1, ATT_T), F32),
                      jnp.zeros((V_DIM, ATT_T), F32)]

        def kv_body(j, st):
            out = []
            s_next = scores(qi, j + 1, 0)
            for hh in heads:
                s_cur = s_sc[slot, hh]
                s_after = scores(qi, j + 1, hh + 1) if hh + 1 < ATT_HEADS else None
                out += update(s_cur, vt_ref[0, j, vsl[hh], :], *st[3 * hh:3 * hh + 3])
                s_sc[slot, hh] = s_next
                s_next = s_after
            return tuple(out)

        state = lax.fori_loop(0, qi, kv_body, tuple(state))
        q_next = jnp.minimum(qi + 1, nq - 1)
        sms = [lax.dot_general(km_ref[:, hsl[hh]], q_ref[0, pl.ds(row0, ATT_T), hsl[hh]], nt,
                               preferred_element_type=F32) for hh in heads]
        outs = []
        for hh in heads:
            m, l, acc = state[3 * hh:3 * hh + 3]
            s = jnp.where(diag_visible, s_sc[slot, hh], NEG_INF)
            sm = sms[hh]
            m_new = jnp.maximum(m, jnp.maximum(jnp.max(s, axis=0, keepdims=True),
                                               jnp.max(sm, axis=0, keepdims=True)))
            alpha = jnp.exp2(m - m_new)
            p = jnp.exp2(s - m_new)
            pm = jnp.exp2(sm - m_new)
            l = alpha * l + jnp.sum(p, axis=0, keepdims=True) + jnp.sum(pm, axis=0, keepdims=True)
            acc = (alpha * acc + jnp.dot(vt_ref[0, qi, vsl[hh], :], p.astype(BF16), preferred_element_type=F32)
                   + jnp.dot(vtm_ref[0, 0, vsl[hh], :], pm.astype(BF16), preferred_element_type=F32))
            outs.append(acc / l)
        for hh in heads:
            s_sc[1 - slot, hh] = scores(q_next, 0, hh)
        o_ref[0, pl.ds(row0, ATT_T), :] = jnp.concatenate(outs, axis=0).T
        return carry

    lax.fori_loop(0, nq, q_body, 0)


def _attn_call(q, k, vt, k_meta, vt_meta):
    b, seq, _ = q.shape
    nk = seq // ATT_T
    hb = ATT_HEADS
    return pl.pallas_call(
        _attn_kernel,
        grid=(b, N_HEADS // hb),
        in_specs=[pl.BlockSpec((1, seq, hb * HEAD_PAD), lambda i, j: (i, 0, j)),
                  pl.BlockSpec((1, seq, hb * HEAD_PAD), lambda i, j: (i, 0, j)),
                  pl.BlockSpec((1, nk, hb * V_DIM, ATT_T), lambda i, j: (i, 0, j, 0)),
                  pl.BlockSpec((N_META, hb * HEAD_PAD), lambda i, j: (0, j)),
                  pl.BlockSpec((1, 1, hb * V_DIM, N_META), lambda i, j: (0, 0, j, 0))],
        out_specs=pl.BlockSpec((1, seq, hb * V_DIM), lambda i, j: (i, 0, j)),
        out_shape=jax.ShapeDtypeStruct((b, seq, D_ATTN), F32),
        scratch_shapes=[pltpu.VMEM((2, hb, ATT_T, ATT_T), F32)],
        compiler_params=pltpu.CompilerParams(dimension_semantics=("arbitrary", "arbitrary"),
                                             vmem_limit_bytes=VMEM_LIMIT),
        name="attn",
    )(q, k, vt, k_meta, vt_meta)


def _mix_kernel(x_ref, ig_ref, ib_ref, u_ref, uh_ref, pre_ref, cw_ref, cb_ref, clg_ref, clb_ref,
                cog_ref, o_ref, aog_ref, wo_ref, l1g_ref, l1b_ref, wrh_ref, wrl_ref, br_ref, tri_ref,
                h1_ref, hp_ref, ri_ref, rw_ref, cnt_ref, win_sc, sh_sc, carry_sc):
    bi = pl.program_id(0)
    ti = pl.program_id(1)
    tm = MIX_TM

    @pl.when((bi == 0) & (ti == 0))
    def _():
        carry_sc[...] = jnp.zeros_like(carry_sc)

    win_sc[0:HALO, :] = jnp.where(ti == 0, pre_ref[...], uh_ref[0])
    win_sc[HALO:, :] = u_ref[0]
    first = HALO - (CONV_K - 1)
    for res in range(1, SUBLANES):
        rows = tm + HALO - SUBLANES
        sh_sc[res - 1, 0:rows, :] = win_sc[pl.ds(res, rows), :]

    acc = jnp.broadcast_to(cb_ref[...], (tm, C_CONV))
    for o in range(first, first + CONV_K):
        res, lo = o % SUBLANES, o - o % SUBLANES
        src = sh_sc[res - 1, lo:lo + tm, :] if res else win_sc[lo:lo + tm, :]
        acc = acc + src * cw_ref[o - first:o - first + 1, :]
    c = _ln(acc, clg_ref[...], clb_ref[...])
    c = c * _sigmoid(c)
    cn = _rms(c, cog_ref[...])
    on = _rms(o_ref[0], aog_ref[...])
    y = (jnp.dot(cn.astype(BF16), wo_ref[0:C_CONV, :], preferred_element_type=F32)
         + jnp.dot(on.astype(BF16), wo_ref[C_CONV:, :], preferred_element_type=F32))
    h = _ln(x_ref[0], ig_ref[...], ib_ref[...])
    h1 = _ln(DEEPNORM_ALPHA * h + y, l1g_ref[...], l1b_ref[...])
    h1_ref[0] = h1
    hp_ref[0] = _pack_bf16_pair(h1[:, :HALF_D], h1[:, HALF_D:])

    nt = (((1,), (1,)), ((), ()))
    h_hi = h1.astype(BF16)
    h_lo = (h1 - h_hi.astype(F32)).astype(BF16)
    lt = (lax.dot_general(wrh_ref[...], h_hi, nt, preferred_element_type=F32)
          + lax.dot_general(wrl_ref[...], h_hi, nt, preferred_element_type=F32)
          + lax.dot_general(wrh_ref[...], h_lo, nt, preferred_element_type=F32)) + br_ref[:, 0:1]
    row = lax.broadcasted_iota(jnp.int32, (ROUTE_ROWS, tm), 0)
    big = jnp.int32(1 << 20)
    is_g = (row >= N_EXPERTS) & (row < N_EXPERTS + N_GROUPS)
    gl = jnp.where(is_g, lt, -jnp.inf)
    gmax = jnp.max(gl, axis=0, keepdims=True)
    g_idx = jnp.min(jnp.where(gl == gmax, row - N_EXPERTS, big), axis=0, keepdims=True)
    g_w = 1.0 / jnp.sum(jnp.exp(gl - gmax), axis=0, keepdims=True)
    in_grp = (row < N_EXPERTS) & ((row // EXPERTS_PER_GROUP) == g_idx)
    el = jnp.where(in_grp, lt, -jnp.inf)
    v1 = jnp.max(el, axis=0, keepdims=True)
    i1 = jnp.min(jnp.where(el == v1, row, big), axis=0, keepdims=True)
    el2 = jnp.where(row == i1, -jnp.inf, el)
    v2 = jnp.max(el2, axis=0, keepdims=True)
    i2 = jnp.min(jnp.where(el2 == v2, row, big), axis=0, keepdims=True)
    e2 = jnp.exp(v2 - v1)
    w1 = g_w / (1.0 + e2)
    w2 = g_w * e2 / (1.0 + e2)

    oh1 = row == i1
    oh2 = row == i2
    oh = jnp.where(oh1 | oh2, 1.0, 0.0)
    before = jnp.dot(oh.astype(BF16), tri_ref[...], preferred_element_type=F32) + carry_sc[:, 0:1]
    r1 = jnp.sum(jnp.where(oh1, before, 0.0), axis=0, keepdims=True).astype(jnp.int32)
    r2 = jnp.sum(jnp.where(oh2, before, 0.0), axis=0, keepdims=True).astype(jnp.int32)
    carry_sc[...] = carry_sc[...] + jnp.sum(oh, axis=1, keepdims=True)
    cnt_ref[...] = carry_sc[...]

    sub = lax.broadcasted_iota(jnp.int32, (SUBLANES, tm), 0)
    ri_ref[...] = jnp.where(sub == 0, i1, jnp.where(sub == 1, i2, jnp.where(sub == 2, r1,
                            jnp.where(sub == 3, r2, 0))))
    lrow = lax.broadcasted_iota(jnp.int32, (LANES, tm), 0)
    rw_ref[0] = jnp.where(lrow == 0, w1, jnp.where(lrow == 1, w2, 0.0)).T[:, :8]


def _mix_call(x, ig, ib, u, pre, cw, cb, clg, clb, cog, o, aog, wo, l1g, l1b, wrh, wrl, br):
    b, seq, _ = x.shape
    tm = MIX_TM
    nt = seq // tm
    full = lambda shape: pl.BlockSpec(shape, lambda i, j: (0,) * len(shape))
    tile = lambda w: pl.BlockSpec((1, tm, w), lambda i, j: (i, j, 0))
    halo = pl.BlockSpec((1, HALO, C_CONV), lambda i, j: (i, jnp.maximum(j * (tm // HALO) - 1, 0), 0))
    earlier = jnp.asarray(np.triu(np.ones((tm, tm), np.float32), k=1), BF16)
    return pl.pallas_call(
        _mix_kernel,
        grid=(b, nt),
        in_specs=[tile(D_MODEL), full((1, D_MODEL)), full((1, D_MODEL)),
                  tile(C_CONV), halo, full((HALO, C_CONV)), full((HALO, C_CONV)), full((1, C_CONV)),
                  full((1, C_CONV)), full((1, C_CONV)), full((1, C_CONV)),
                  tile(D_ATTN), full((1, D_ATTN)), full((D_MODEL, D_MODEL)),
                  full((1, D_MODEL)), full((1, D_MODEL)), full((ROUTE_ROWS, D_MODEL)),
                  full((ROUTE_ROWS, D_MODEL)), full((ROUTE_ROWS, LANES)), full((tm, tm))],
        out_specs=[tile(D_MODEL), tile(HALF_D), pl.BlockSpec((8, tm), lambda i, j: (0, i * nt + j)), tile(8),
                   full((ROUTE_ROWS, LANES))],
        out_shape=[jax.ShapeDtypeStruct((b, seq, D_MODEL), F32),
                   jax.ShapeDtypeStruct((b, seq, HALF_D), jnp.uint32),
                   jax.ShapeDtypeStruct((8, b * seq), jnp.int32),
                   jax.ShapeDtypeStruct((b, seq, 8), F32),
                   jax.ShapeDtypeStruct((ROUTE_ROWS, LANES), F32)],
        scratch_shapes=[pltpu.VMEM((HALO + tm, C_CONV), F32),
                        pltpu.VMEM((SUBLANES - 1, HALO + tm, C_CONV), F32),
                        pltpu.VMEM((ROUTE_ROWS, LANES), F32)],
        compiler_params=pltpu.CompilerParams(dimension_semantics=("arbitrary", "arbitrary"),
                                             vmem_limit_bytes=VMEM_LIMIT),
        name="mix",
    )(x, ig, ib, u, u, pre, cw, cb, clg, clb, cog, o, aog, wo, l1g, l1b, wrh, wrl, br, earlier)


def _tables_kernel(ps_ref, pe_ref, cn_ref, ri_ref, dest_ref, blk_ref):
    ri = ri_ref[...]
    start = jnp.zeros_like(ri)
    for e in range(N_EXPERTS):
        start = jnp.where(ri == e, ps_ref[e], start)
    dest_ref[...] = jnp.zeros_like(dest_ref)
    dest_ref[0:2, :] = start[0:2, :] + ri_ref[2:4, :]
    blk_start = lax.broadcasted_iota(jnp.int32, blk_ref.shape, 1) * MOE_BM
    be = jnp.zeros(blk_ref.shape, jnp.int32)
    for e in range(N_EXPERTS):
        be = be + jnp.where(pe_ref[e] <= blk_start, 1, 0)
    be = jnp.minimum(be, N_EXPERTS - 1)
    end = jnp.zeros(blk_ref.shape, jnp.int32)
    for e in range(N_EXPERTS):
        end = jnp.where(be == e, ps_ref[e] + cn_ref[e], end)
    valid = jnp.clip(end - blk_start, 0, MOE_BM)
    row = lax.broadcasted_iota(jnp.int32, blk_ref.shape, 0)
    blk_ref[...] = jnp.where(row == 0, be, valid)


def _tables_call(pad_start, pad_end, counts, ri, n_blocks):
    t = ri.shape[1]
    blk_w = -(-n_blocks // LANES) * LANES
    return pl.pallas_call(
        _tables_kernel,
        grid_spec=pltpu.PrefetchScalarGridSpec(
            num_scalar_prefetch=3,
            grid=(1,),
            in_specs=[pl.BlockSpec((SUBLANES, t), lambda i, *_: (0, 0))],
            out_specs=[pl.BlockSpec((SUBLANES, t), lambda i, *_: (0, 0)),
                       pl.BlockSpec((SUBLANES, blk_w), lambda i, *_: (0, 0))]),
        out_shape=[jax.ShapeDtypeStruct((SUBLANES, t), jnp.int32),
                   jax.ShapeDtypeStruct((SUBLANES, blk_w), jnp.int32)],
        compiler_params=pltpu.CompilerParams(dimension_semantics=("arbitrary",),
                                             vmem_limit_bytes=VMEM_LIMIT),
        name="tables",
    )(pad_start, pad_end, counts, ri)


def _expert_kernel(be_ref, nv_ref, x_ref, wg_ref, wu_ref, wd_ref, y_ref, wg_sc, wu_sc, wd_sc):
    i = pl.program_id(0)
    nv = nv_ref[i]

    @pl.when((i == 0) | (be_ref[i] != be_ref[jnp.maximum(i - 1, 0)]))
    def _():
        wg_sc[...] = wg_ref[0, 0].astype(BF16)
        wu_sc[...] = wu_ref[0, 0].astype(BF16)
        wd_sc[...] = wd_ref[0, 0].astype(BF16)

    @pl.when(nv > 0)
    def _():
        rowid = lax.broadcasted_iota(jnp.int32, x_ref.shape, 0)
        x_lo, x_hi = _unpack_bf16_pair(jnp.where(rowid < nv, x_ref[...], jnp.uint32(0)))
        x_lo = x_lo.astype(BF16)
        x_hi = x_hi.astype(BF16)
        g = (jnp.dot(x_lo, wg_sc[:HALF_D, :], preferred_element_type=F32)
             + jnp.dot(x_hi, wg_sc[HALF_D:, :], preferred_element_type=F32))
        up = (jnp.dot(x_lo, wu_sc[:HALF_D, :], preferred_element_type=F32)
              + jnp.dot(x_hi, wu_sc[HALF_D:, :], preferred_element_type=F32))
        act = g * _sigmoid(g) * up
        y = jnp.dot(act.astype(BF16), wd_sc[...], preferred_element_type=F32)
        y_ref[...] = _pack_bf16_pair(y[:, :HALF_D], y[:, HALF_D:])

    @pl.when(nv == 0)
    def _():
        y_ref[...] = jnp.zeros_like(y_ref)


def _expert_call(blk_e, blk_valid, xb, w_gate, w_up, w_down):
    r = xb.shape[0]
    bm = MOE_BM
    wspec = lambda a, b: pl.BlockSpec((1, 1, a, b), lambda i, be, nv: (0, be[i], 0, 0))
    return pl.pallas_call(
        _expert_kernel,
        grid_spec=pltpu.PrefetchScalarGridSpec(
            num_scalar_prefetch=2,
            grid=(r // bm,),
            in_specs=[pl.BlockSpec((bm, HALF_D), lambda i, be, nv: (i, 0)),
                      wspec(D_MODEL, D_EXPERT), wspec(D_MODEL, D_EXPERT), wspec(D_EXPERT, D_MODEL)],
            out_specs=pl.BlockSpec((bm, HALF_D), lambda i, be, nv: (i, 0)),
            scratch_shapes=[pltpu.VMEM((D_MODEL, D_EXPERT), BF16), pltpu.VMEM((D_MODEL, D_EXPERT), BF16),
                            pltpu.VMEM((D_EXPERT, D_MODEL), BF16)]),
        out_shape=jax.ShapeDtypeStruct((r, HALF_D), jnp.uint32),
        compiler_params=pltpu.CompilerParams(dimension_semantics=("arbitrary",),
                                             vmem_limit_bytes=VMEM_LIMIT),
        name="experts",
    )(blk_e, blk_valid, xb, w_gate, w_up, w_down)


def _sc_gather_rows(table, idx):
    m = idx.shape[0]
    width = table.shape[1]
    sc = plsc.get_sparse_core_info()
    workers = sc.num_cores * sc.num_subcores
    per_worker = m // workers
    mesh = plsc.VectorSubcoreMesh(core_axis_name="c", subcore_axis_name="s")

    n_win = per_worker // SC_WINDOW
    assert n_win % 2 == 0 and n_win >= 2
    buf = lambda: [pltpu.VMEM((SC_WINDOW, width), table.dtype), pltpu.SemaphoreType.DMA]

    @functools.partial(pl.kernel, mesh=mesh, out_type=jax.ShapeDtypeStruct((m, width), table.dtype),
                       scratch_types=[pltpu.VMEM((per_worker,), jnp.int32)] + buf() + buf())
    def gather(table_hbm, idx_hbm, out_hbm, idx_v, rows_a, sem_a, rows_b, sem_b):
        base = (lax.axis_index("s") * sc.num_cores + lax.axis_index("c")) * per_worker
        bufs = ((rows_a, sem_a), (rows_b, sem_b))
        pltpu.sync_copy(idx_hbm.at[pl.ds(base, per_worker)], idx_v)

        def window_copy(win, which):
            rows_v, sem = bufs[which]
            return pltpu.make_async_copy(table_hbm.at[idx_v.at[pl.ds(win * SC_WINDOW, SC_WINDOW)]], rows_v, sem)

        def fetch(win, which):
            window_copy(win, which).start()

        def flush(win, which):
            window_copy(win, which).wait()
            pltpu.sync_copy(bufs[which][0], out_hbm.at[pl.ds(base + win * SC_WINDOW, SC_WINDOW)])

        fetch(0, 0)

        @pl.loop(0, n_win // 2 - 1)
        def _(pair):
            win = 2 * pair
            fetch(win + 1, 1)
            flush(win, 0)
            fetch(win + 2, 0)
            flush(win + 1, 1)

        fetch(n_win - 1, 1)
        flush(n_win - 2, 0)
        flush(n_win - 1, 1)

    return gather(table, idx)


def _sc_scatter_rows(rows, idx0, idx1, n_out):
    m, width = rows.shape
    sc = plsc.get_sparse_core_info()
    workers = sc.num_cores * sc.num_subcores
    per_worker = m // workers
    mesh = plsc.VectorSubcoreMesh(core_axis_name="c", subcore_axis_name="s")

    n_win = per_worker // SC_WINDOW
    assert n_win % 2 == 0 and n_win >= 2
    buf = lambda: [pltpu.VMEM((SC_WINDOW, width), rows.dtype),
                   pltpu.SemaphoreType.DMA, pltpu.SemaphoreType.DMA, pltpu.SemaphoreType.DMA]

    @functools.partial(pl.kernel, mesh=mesh, out_type=jax.ShapeDtypeStruct((n_out, width), rows.dtype),
                       scratch_types=[pltpu.VMEM((per_worker,), jnp.int32), pltpu.VMEM((per_worker,), jnp.int32)]
                       + buf() + buf())
    def scatter(rows_hbm, idx0_hbm, idx1_hbm, out_hbm, idx0_v, idx1_v, *scratch):
        base = (lax.axis_index("s") * sc.num_cores + lax.axis_index("c")) * per_worker
        bufs = (scratch[:4], scratch[4:])
        pltpu.sync_copy(idx0_hbm.at[pl.ds(base, per_worker)], idx0_v)
        pltpu.sync_copy(idx1_hbm.at[pl.ds(base, per_worker)], idx1_v)

        def load(win, which):
            rows_v, lsem, _, _ = bufs[which]
            pltpu.async_copy(rows_hbm.at[pl.ds(base + win * SC_WINDOW, SC_WINDOW)], rows_v, lsem)

        def send(win, which):
            rows_v, lsem, sem0, sem1 = bufs[which]
            here = pl.ds(win * SC_WINDOW, SC_WINDOW)
            pltpu.make_async_copy(rows_hbm.at[pl.ds(0, SC_WINDOW)], rows_v, lsem).wait()
            first = pltpu.async_copy(rows_v, out_hbm.at[idx0_v.at[here]], sem0)
            second = pltpu.async_copy(rows_v, out_hbm.at[idx1_v.at[here]], sem1)
            first.wait()
            second.wait()

        load(0, 0)

        @pl.loop(0, n_win // 2 - 1)
        def _(pair):
            win = 2 * pair
            load(win + 1, 1)
            send(win, 0)
            load(win + 2, 0)
            send(win + 1, 1)

        load(n_win - 1, 1)
        send(n_win - 2, 0)
        send(n_win - 1, 1)

    return scatter(rows, idx0, idx1)


def _combine_rows_kernel(h_ref, w_ref, g_ref, b_ref, y0_ref, y1_ref, o_ref):
    w = w_ref[...]
    y0_lo, y0_hi = _unpack_bf16_pair(y0_ref[...])
    y1_lo, y1_hi = _unpack_bf16_pair(y1_ref[...])
    y = jnp.concatenate([w[:, 0:1] * y0_lo + w[:, 1:2] * y1_lo, w[:, 0:1] * y0_hi + w[:, 1:2] * y1_hi], axis=1)
    o_ref[...] = _ln(DEEPNORM_ALPHA * h_ref[...] + y, g_ref[...], b_ref[...])


def _combine_rows_call(h1, rw, g, b, yg):
    t = h1.shape[0]
    tm = COMB_TM
    n = t // tm
    return pl.pallas_call(
        _combine_rows_kernel,
        grid=(n,),
        in_specs=[pl.BlockSpec((tm, D_MODEL), lambda i: (i, 0)),
                  pl.BlockSpec((tm, 8), lambda i: (i, 0)),
                  pl.BlockSpec((1, D_MODEL), lambda i: (0, 0)),
                  pl.BlockSpec((1, D_MODEL), lambda i: (0, 0)),
                  pl.BlockSpec((tm, HALF_D), lambda i: (i, 0)),
                  pl.BlockSpec((tm, HALF_D), lambda i: (n + i, 0))],
        out_specs=pl.BlockSpec((tm, D_MODEL), lambda i: (i, 0)),
        out_shape=jax.ShapeDtypeStruct((t, D_MODEL), F32),
        compiler_params=pltpu.CompilerParams(dimension_semantics=("arbitrary",),
                                             vmem_limit_bytes=VMEM_LIMIT),
        name="combine_rows",
    )(h1, rw, g, b, yg, yg)


def _take_cols(w, idx):
    wz = jnp.concatenate([w, jnp.zeros((w.shape[0], 1), w.dtype)], axis=1)
    return wz[:, np.where(idx < 0, w.shape[1], idx)]


def _layout_indices():
    lane_src = np.full((HEAD_PAD,), -1, np.int64)
    lane_src[0:ROPE_HALF] = QK_NOPE + np.arange(ROPE_HALF)
    lane_src[ROPE_HALF:X2_LANE] = np.arange(X2_LANE - ROPE_HALF)
    lane_src[X2_LANE:X2_LANE + ROPE_HALF] = QK_NOPE + ROPE_HALF + np.arange(ROPE_HALF)
    n_lo = X2_LANE - ROPE_HALF
    lane_src[X2_LANE + ROPE_HALF:X2_LANE + ROPE_HALF + QK_NOPE - n_lo] = n_lo + np.arange(QK_NOPE - n_lo)
    q_idx = np.concatenate([np.where(lane_src >= 0, lane_src + (QK_NOPE + QK_ROPE) * h, -1)
                            for h in range(N_HEADS)])
    nope_src = np.where(lane_src < QK_NOPE, lane_src, -1)
    k_idx = np.concatenate([np.where(nope_src >= 0, nope_src + (QK_NOPE + V_DIM) * h, -1)
                            for h in range(N_HEADS)])
    v_idx = np.concatenate([QK_NOPE + np.arange(V_DIM) + (QK_NOPE + V_DIM) * h for h in range(N_HEADS)])
    kr_idx = np.full((LANES,), -1, np.int64)
    kr_idx[0:ROPE_HALF] = np.arange(ROPE_HALF)
    kr_idx[X2_LANE:X2_LANE + ROPE_HALF] = ROPE_HALF + np.arange(ROPE_HALF)
    return q_idx, k_idx, v_idx, kr_idx


def _rope_lane_tables(length):
    inv_freq = np.float32(ROPE_THETA) ** (-np.arange(ROPE_HALF, dtype=np.float32) / np.float32(ROPE_HALF))
    ang = np.arange(length, dtype=np.float32)[:, None] * inv_freq[None, :].astype(np.float32)
    cos, sin = np.cos(ang).astype(np.float32), np.sin(ang).astype(np.float32)
    ctab = np.ones((length, LANES), np.float32)
    stab = np.zeros((length, LANES), np.float32)
    for lo, sign in ((0, -1.0), (X2_LANE, 1.0)):
        ctab[:, lo:lo + ROPE_HALF] = cos
        stab[:, lo:lo + ROPE_HALF] = sign * sin
    return jnp.asarray(ctab), jnp.asarray(stab)


def kernel(x, meta_tokens, ln_in_g, ln_in_b, w_in, conv_w, conv_b, conv_ln_g, conv_ln_b, q_norm_g, w_uq,
           kv_norm_g, w_ukv, conv_out_g, attn_out_g, w_o, ln1_g, ln1_b, w_rg, b_rg, w_re, b_re,
           w_gate, w_up, w_down, ln2_g, ln2_b):
    assert w_in.shape[0] == 1, "single layer"
    bsz, seq, d = x.shape
    t = bsz * seq
    row = lambda a: a.reshape(1, -1)

    q_idx, k_idx, v_idx, kr_idx = _layout_indices()
    c4 = 2 * C_CONV + Q_LORA + KV_LORA
    w_in_l = jnp.concatenate([w_in[0][:, :c4], _take_cols(w_in[0][:, c4:], kr_idx)], axis=1).astype(BF16)
    wuq_l = _take_cols(w_uq[0], q_idx).astype(BF16)
    wuk_l = _take_cols(w_ukv[0], k_idx).astype(BF16)
    wvt_l = w_ukv[0][:, v_idx].T.astype(BF16)
    wo_l = w_o[0].astype(BF16)
    n_route_pad = ROUTE_ROWS - N_EXPERTS - N_GROUPS
    wr_t = jnp.concatenate([w_re[0].T, w_rg[0].T, jnp.zeros((n_route_pad, d), F32)], axis=0)
    wrh_l = wr_t.astype(BF16)
    wrl_l = (wr_t - wrh_l.astype(F32)).astype(BF16)
    br_l = jnp.broadcast_to(jnp.concatenate([b_re[0], b_rg[0], jnp.zeros((n_route_pad,), F32)])[:, None],
                            (ROUTE_ROWS, LANES))
    ctab, stab = _rope_lane_tables(N_META + seq)

    proj_w = (row(ln_in_g), row(ln_in_b), w_in_l, row(q_norm_g[0]), wuq_l, row(kv_norm_g[0]), wuk_l, wvt_l)
    u_m, _, k_m, vt_m = _proj_call(meta_tokens.astype(F32), N_META, 1, *proj_w, ctab[:N_META], stab[:N_META])
    u, q, k, vt = _proj_call(x.reshape(t, d), PROJ_TM, seq // PROJ_TM, *proj_w, ctab[N_META:], stab[N_META:])

    o = _attn_call(q.reshape(bsz, seq, -1), k.reshape(bsz, seq, -1),
                   vt.reshape(bsz, seq // ATT_T, D_ATTN, ATT_T), k_m, vt_m)

    pre = jnp.concatenate([jnp.zeros((HALO - N_META, C_CONV), F32), u_m], axis=0)
    cw = jnp.concatenate([conv_w[0], jnp.zeros((HALO - CONV_K, C_CONV), F32)], axis=0)
    h1, hp, ri, rw, cnt = _mix_call(x, row(ln_in_g), row(ln_in_b), u.reshape(bsz, seq, C_CONV), pre, cw,
                                    row(conv_b[0]), row(conv_ln_g[0]), row(conv_ln_b[0]), row(conv_out_g[0]),
                                    o, row(attn_out_g[0]), wo_l, row(ln1_g[0]), row(ln1_b[0]), wrh_l, wrl_l, br_l)

    counts = cnt[:N_EXPERTS, 0].astype(jnp.int32)
    padded = (counts + MOE_BM - 1) // MOE_BM * MOE_BM
    pad_end = jnp.cumsum(padded)
    pad_start = pad_end - padded
    n_blocks = (2 * t) // MOE_BM + N_EXPERTS
    dest, blk = _tables_call(pad_start, pad_end, counts, ri, n_blocks)
    blk_e, blk_valid = blk[0, :n_blocks], blk[1, :n_blocks]

    h1f = h1.reshape(t, d)
    dest0, dest1 = dest[0], dest[1]
    xb = _sc_scatter_rows(hp.reshape(t, HALF_D), dest0, dest1, n_blocks * MOE_BM)
    yb = _expert_call(blk_e, blk_valid, xb, w_gate, w_up, w_down)
    yg = _sc_gather_rows(yb, jnp.concatenate([dest0, dest1]))
    out = _combine_rows_call(h1f, rw.reshape(t, 8), row(ln2_g[0]), row(ln2_b[0]), yg)
    return out.reshape(bsz, seq, d)
```

```python
import functools
import math

import numpy as np
import jax
import jax.numpy as jnp
from jax import lax
from jax.experimental import pallas as pl
from jax.experimental.pallas import tpu as pltpu
from jax.experimental.pallas import tpu_sc as plsc

F32 = jnp.float32
BF16 = jnp.bfloat16

D_MODEL = 1024
HALF_D = D_MODEL // 2
N_META = 16
CHUNK = 64
C_CONV = 512
CONV_K = 31
N_HEADS = 8
QK_NOPE = 64
QK_ROPE = 32
V_DIM = 64
Q_LORA = 384
KV_LORA = 256
ROPE_THETA = 10000.0
D_ATTN = N_HEADS * V_DIM
N_GROUPS = 4
EXPERTS_PER_GROUP = 8
N_EXPERTS = N_GROUPS * EXPERTS_PER_GROUP
D_EXPERT = 256
DEEPNORM_ALPHA = 2.0 ** 0.25
EPS = 1e-5
NEG_INF = -1e30

LANES = 128
SUBLANES = 8
HEAD_PAD = LANES
ROPE_HALF = QK_ROPE // 2
X2_LANE = 64
Z_COLS = 2 * C_CONV + Q_LORA + KV_LORA + LANES
Q_SCALE = (QK_NOPE + QK_ROPE) ** -0.5 * math.log2(math.e)

PROJ_TM = 1024
ATT_T = 256
ATT_HEADS = 8
MIX_TM = 512
ROUTE_ROWS = 48
HALO = 32
MOE_BM = 512
COMB_TM = 1024
SC_WINDOW = 64
VMEM_LIMIT = 56 * 1024 * 1024


def _ln(x, g, b):
    mu = jnp.mean(x, axis=-1, keepdims=True)
    xc = x - mu
    var = jnp.mean(xc * xc, axis=-1, keepdims=True)
    return xc * lax.rsqrt(var + EPS) * g + b


def _rms(x, g):
    return x * lax.rsqrt(jnp.mean(x * x, axis=-1, keepdims=True) + EPS) * g


def _sigmoid(x):
    return 1.0 / (1.0 + jnp.exp(-x))


def _pack_bf16_pair(lo, hi):
    lo_bits = lax.bitcast_convert_type(lo.astype(BF16).astype(F32), jnp.uint32)
    hi_bits = lax.bitcast_convert_type(hi.astype(BF16).astype(F32), jnp.uint32)
    return (lo_bits >> 16) | (hi_bits & jnp.uint32(0xFFFF0000))


def _unpack_bf16_pair(packed):
    lo = lax.bitcast_convert_type(packed << 16, F32)
    hi = lax.bitcast_convert_type(packed & jnp.uint32(0xFFFF0000), F32)
    return lo, hi


def _proj_kernel(x_ref, g_ref, b_ref, win_ref, qg_ref, wuq_ref, kvg_ref, wuk_ref, wvt_ref, c_ref, s_ref,
                 u_ref, q_ref, k_ref, vt_ref):
    n_part, part = vt_ref.shape[1], vt_ref.shape[3]
    zs = []
    for r in range(n_part):
        h = _ln(x_ref[r * part:(r + 1) * part, :], g_ref[...], b_ref[...])
        zs.append(jnp.dot(h.astype(BF16), win_ref[...], preferred_element_type=F32))
    c0 = 2 * C_CONV
    for r, z in enumerate(zs):
        rows = slice(r * part, (r + 1) * part)
        u_ref[rows, :] = z[:, :C_CONV] * _sigmoid(z[:, C_CONV:c0])
        cq = z[:, c0:c0 + Q_LORA]
        ckv = z[:, c0 + Q_LORA:c0 + Q_LORA + KV_LORA]
        kr = z[:, c0 + Q_LORA + KV_LORA:]
        cs = c_ref[rows, :]
        sn = s_ref[rows, :]
        q = jnp.dot(_rms(cq, qg_ref[...]).astype(BF16), wuq_ref[...], preferred_element_type=F32)
        kvn = _rms(ckv, kvg_ref[...]).astype(BF16)
        kn = jnp.dot(kvn, wuk_ref[...], preferred_element_type=F32)
        kr_rot = kr * cs + pltpu.roll(kr, X2_LANE, 1) * sn
        for hh in range(N_HEADS):
            sl = slice(HEAD_PAD * hh, HEAD_PAD * (hh + 1))
            qh = q[:, sl]
            qh = (qh * cs + pltpu.roll(qh, X2_LANE, 1) * sn) * Q_SCALE
            q_ref[rows, sl] = qh.astype(BF16)
            k_ref[rows, sl] = (kn[:, sl] + kr_rot).astype(BF16)
        vt = lax.dot_general(wvt_ref[...], kvn, (((1,), (1,)), ((), ())), preferred_element_type=F32)
        vt_ref[0, r] = vt.astype(BF16)


def _proj_call(x2d, tm, pos_blocks, ln_g, ln_b, w_in, qg, wuq, kvg, wuk, wvt, ctab, stab):
    t = x2d.shape[0]
    sub_w = min(tm, ATT_T)
    n_sub = tm // sub_w
    full = lambda shape: pl.BlockSpec(shape, lambda i: (0,) * len(shape))
    row = lambda w: pl.BlockSpec((tm, w), lambda i: (i, 0))
    tab = pl.BlockSpec((tm, LANES), lambda i: (i % pos_blocks, 0))
    return pl.pallas_call(
        _proj_kernel,
        grid=(t // tm,),
        in_specs=[row(D_MODEL), full((1, D_MODEL)), full((1, D_MODEL)), full((D_MODEL, Z_COLS)),
                  full((1, Q_LORA)), full((Q_LORA, N_HEADS * HEAD_PAD)),
                  full((1, KV_LORA)), full((KV_LORA, N_HEADS * HEAD_PAD)), full((D_ATTN, KV_LORA)), tab, tab],
        out_specs=[row(C_CONV), row(N_HEADS * HEAD_PAD), row(N_HEADS * HEAD_PAD),
                   pl.BlockSpec((1, n_sub, D_ATTN, sub_w), lambda i: (i, 0, 0, 0))],
        out_shape=[jax.ShapeDtypeStruct((t, C_CONV), F32),
                   jax.ShapeDtypeStruct((t, N_HEADS * HEAD_PAD), BF16),
                   jax.ShapeDtypeStruct((t, N_HEADS * HEAD_PAD), BF16),
                   jax.ShapeDtypeStruct((t // tm, n_sub, D_ATTN, sub_w), BF16)],
        compiler_params=pltpu.CompilerParams(dimension_semantics=("arbitrary",),
                                             vmem_limit_bytes=VMEM_LIMIT),
        name="proj",
    )(x2d, ln_g, ln_b, w_in, qg, wuq, kvg, wuk, wvt, ctab, stab)


def _attn_kernel(q_ref, k_ref, vt_ref, km_ref, vtm_ref, o_ref, s_sc):
    seq = q_ref.shape[1]
    nq = seq // ATT_T
    key_chunk = lax.broadcasted_iota(jnp.int32, (ATT_T, ATT_T), 0) // CHUNK
    qry_chunk = lax.broadcasted_iota(jnp.int32, (ATT_T, ATT_T), 1) // CHUNK
    diag_visible = key_chunk <= qry_chunk
    nt = (((1,), (1,)), ((), ()))
    heads = range(ATT_HEADS)
    hsl = [slice(HEAD_PAD * hh, HEAD_PAD * (hh + 1)) for hh in heads]
    vsl = [slice(V_DIM * hh, V_DIM * (hh + 1)) for hh in heads]

    def update(s, vt, m, l, acc):
        m_new = jnp.maximum(m, jnp.max(s, axis=0, keepdims=True))
        alpha = jnp.exp2(m - m_new)
        p = jnp.exp2(s - m_new)
        l = alpha * l + jnp.sum(p, axis=0, keepdims=True)
        acc = alpha * acc + jnp.dot(vt, p.astype(BF16), preferred_element_type=F32)
        return m_new, l, acc

    def scores(qi, j, hh):
        q0 = pl.multiple_of(qi * ATT_T, ATT_T)
        k0 = pl.multiple_of(j * ATT_T, ATT_T)
        return lax.dot_general(k_ref[0, pl.ds(k0, ATT_T), hsl[hh]], q_ref[0, pl.ds(q0, ATT_T), hsl[hh]], nt,
                               preferred_element_type=F32)

    for hh in heads:
        s_sc[0, hh] = scores(0, 0, hh)

    def q_body(qi, carry):
        row0 = pl.multiple_of(qi * ATT_T, ATT_T)
        slot = qi % 2
        state = []
        for hh in heads:
            state += [jnp.full((1, ATT_T), NEG_INF, F32), jnp.zeros((1, ATT_T), F32),
                      jnp.zeros((V_DIM, ATT_T), F32)]

        def kv_body(j, st):
            out = []
            s_next = scores(qi, j + 1, 0)
            for hh in heads:
                s_cur = s_sc[slot, hh]
                s_after = scores(qi, j + 1, hh + 1) if hh + 1 < ATT_HEADS else None
                out += update(s_cur, vt_ref[0, j, vsl[hh], :], *st[3 * hh:3 * hh + 3])
                s_sc[slot, hh] = s_next
                s_next = s_after
            return tuple(out)

        state = lax.fori_loop(0, qi, kv_body, tuple(state))
        q_next = jnp.minimum(qi + 1, nq - 1)
        sms = [lax.dot_general(km_ref[:, hsl[hh]], q_ref[0, pl.ds(row0, ATT_T), hsl[hh]], nt,
                               preferred_element_type=F32) for hh in heads]
        outs = []
        for hh in heads:
            m, l, acc = state[3 * hh:3 * hh + 3]
            s = jnp.where(diag_visible, s_sc[slot, hh], NEG_INF)
            sm = sms[hh]
            m_new = jnp.maximum(m, jnp.maximum(jnp.max(s, axis=0, keepdims=True),
                                               jnp.max(sm, axis=0, keepdims=True)))
            alpha = jnp.exp2(m - m_new)
            p = jnp.exp2(s - m_new)
            pm = jnp.exp2(sm - m_new)
            l = alpha * l + jnp.sum(p, axis=0, keepdims=True) + jnp.sum(pm, axis=0, keepdims=True)
            acc = (alpha * acc + jnp.dot(vt_ref[0, qi, vsl[hh], :], p.astype(BF16), preferred_element_type=F32)
                   + jnp.dot(vtm_ref[0, 0, vsl[hh], :], pm.astype(BF16), preferred_element_type=F32))
            outs.append(acc / l)
        for hh in heads:
            s_sc[1 - slot, hh] = scores(q_next, 0, hh)
        o_ref[0, pl.ds(row0, ATT_T), :] = jnp.concatenate(outs, axis=0).T
        return carry

    lax.fori_loop(0, nq, q_body, 0)


def _attn_call(q, k, vt, k_meta, vt_meta):
    b, seq, _ = q.shape
    nk = seq // ATT_T
    hb = ATT_HEADS
    return pl.pallas_call(
        _attn_kernel,
        grid=(b, N_HEADS // hb),
        in_specs=[pl.BlockSpec((1, seq, hb * HEAD_PAD), lambda i, j: (i, 0, j)),
                  pl.BlockSpec((1, seq, hb * HEAD_PAD), lambda i, j: (i, 0, j)),
                  pl.BlockSpec((1, nk, hb * V_DIM, ATT_T), lambda i, j: (i, 0, j, 0)),
                  pl.BlockSpec((N_META, hb * HEAD_PAD), lambda i, j: (0, j)),
                  pl.BlockSpec((1, 1, hb * V_DIM, N_META), lambda i, j: (0, 0, j, 0))],
        out_specs=pl.BlockSpec((1, seq, hb * V_DIM), lambda i, j: (i, 0, j)),
        out_shape=jax.ShapeDtypeStruct((b, seq, D_ATTN), F32),
        scratch_shapes=[pltpu.VMEM((2, hb, ATT_T, ATT_T), F32)],
        compiler_params=pltpu.CompilerParams(dimension_semantics=("arbitrary", "arbitrary"),
                                             vmem_limit_bytes=VMEM_LIMIT),
        name="attn",
    )(q, k, vt, k_meta, vt_meta)


def _mix_kernel(x_ref, ig_ref, ib_ref, u_ref, uh_ref, pre_ref, cw_ref, cb_ref, clg_ref, clb_ref,
                cog_ref, o_ref, aog_ref, wo_ref, l1g_ref, l1b_ref, wrh_ref, wrl_ref, br_ref, tri_ref,
                h1_ref, hp_ref, ri_ref, rw_ref, cnt_ref, win_sc, sh_sc, carry_sc):
    bi = pl.program_id(0)
    ti = pl.program_id(1)
    tm = MIX_TM

    @pl.when((bi == 0) & (ti == 0))
    def _():
        carry_sc[...] = jnp.zeros_like(carry_sc)

    win_sc[0:HALO, :] = jnp.where(ti == 0, pre_ref[...], uh_ref[0])
    win_sc[HALO:, :] = u_ref[0]
    first = HALO - (CONV_K - 1)
    for res in range(1, SUBLANES):
        rows = tm + HALO - SUBLANES
        sh_sc[res - 1, 0:rows, :] = win_sc[pl.ds(res, rows), :]

    acc = jnp.broadcast_to(cb_ref[...], (tm, C_CONV))
    for o in range(first, first + CONV_K):
        res, lo = o % SUBLANES, o - o % SUBLANES
        src = sh_sc[res - 1, lo:lo + tm, :] if res else win_sc[lo:lo + tm, :]
        acc = acc + src * cw_ref[o - first:o - first + 1, :]
    c = _ln(acc, clg_ref[...], clb_ref[...])
    c = c * _sigmoid(c)
    cn = _rms(c, cog_ref[...])
    on = _rms(o_ref[0], aog_ref[...])
    y = (jnp.dot(cn.astype(BF16), wo_ref[0:C_CONV, :], preferred_element_type=F32)
         + jnp.dot(on.astype(BF16), wo_ref[C_CONV:, :], preferred_element_type=F32))
    h = _ln(x_ref[0], ig_ref[...], ib_ref[...])
    h1 = _ln(DEEPNORM_ALPHA * h + y, l1g_ref[...], l1b_ref[...])
    h1_ref[0] = h1
    hp_ref[0] = _pack_bf16_pair(h1[:, :HALF_D], h1[:, HALF_D:])

    nt = (((1,), (1,)), ((), ()))
    h_hi = h1.astype(BF16)
    h_lo = (h1 - h_hi.astype(F32)).astype(BF16)
    lt = (lax.dot_general(wrh_ref[...], h_hi, nt, preferred_element_type=F32)
          + lax.dot_general(wrl_ref[...], h_hi, nt, preferred_element_type=F32)
          + lax.dot_general(wrh_ref[...], h_lo, nt, preferred_element_type=F32)) + br_ref[:, 0:1]
    row = lax.broadcasted_iota(jnp.int32, (ROUTE_ROWS, tm), 0)
    big = jnp.int32(1 << 20)
    is_g = (row >= N_EXPERTS) & (row < N_EXPERTS + N_GROUPS)
    gl = jnp.where(is_g, lt, -jnp.inf)
    gmax = jnp.max(gl, axis=0, keepdims=True)
    g_idx = jnp.min(jnp.where(gl == gmax, row - N_EXPERTS, big), axis=0, keepdims=True)
    g_w = 1.0 / jnp.sum(jnp.exp(gl - gmax), axis=0, keepdims=True)
    in_grp = (row < N_EXPERTS) & ((row // EXPERTS_PER_GROUP) == g_idx)
    el = jnp.where(in_grp, lt, -jnp.inf)
    v1 = jnp.max(el, axis=0, keepdims=True)
    i1 = jnp.min(jnp.where(el == v1, row, big), axis=0, keepdims=True)
    el2 = jnp.where(row == i1, -jnp.inf, el)
    v2 = jnp.max(el2, axis=0, keepdims=True)
    i2 = jnp.min(jnp.where(el2 == v2, row, big), axis=0, keepdims=True)
    e2 = jnp.exp(v2 - v1)
    w1 = g_w / (1.0 + e2)
    w2 = g_w * e2 / (1.0 + e2)

    oh1 = row == i1
    oh2 = row == i2
    oh = jnp.where(oh1 | oh2, 1.0, 0.0)
    before = jnp.dot(oh.astype(BF16), tri_ref[...], preferred_element_type=F32) + carry_sc[:, 0:1]
    r1 = jnp.sum(jnp.where(oh1, before, 0.0), axis=0, keepdims=True).astype(jnp.int32)
    r2 = jnp.sum(jnp.where(oh2, before, 0.0), axis=0, keepdims=True).astype(jnp.int32)
    carry_sc[...] = carry_sc[...] + jnp.sum(oh, axis=1, keepdims=True)
    cnt_ref[...] = carry_sc[...]

    sub = lax.broadcasted_iota(jnp.int32, (SUBLANES, tm), 0)
    ri_ref[...] = jnp.where(sub == 0, i1, jnp.where(sub == 1, i2, jnp.where(sub == 2, r1,
                            jnp.where(sub == 3, r2, 0))))
    lrow = lax.broadcasted_iota(jnp.int32, (LANES, tm), 0)
    rw_ref[0] = jnp.where(lrow == 0, w1, jnp.where(lrow == 1, w2, 0.0)).T[:, :8]


def _mix_call(x, ig, ib, u, pre, cw, cb, clg, clb, cog, o, aog, wo, l1g, l1b, wrh, wrl, br):
    b, seq, _ = x.shape
    tm = MIX_TM
    nt = seq // tm
    full = lambda shape: pl.BlockSpec(shape, lambda i, j: (0,) * len(shape))
    tile = lambda w: pl.BlockSpec((1, tm, w), lambda i, j: (i, j, 0))
    halo = pl.BlockSpec((1, HALO, C_CONV), lambda i, j: (i, jnp.maximum(j * (tm // HALO) - 1, 0), 0))
    earlier = jnp.asarray(np.triu(np.ones((tm, tm), np.float32), k=1), BF16)
    return pl.pallas_call(
        _mix_kernel,
        grid=(b, nt),
        in_specs=[tile(D_MODEL), full((1, D_MODEL)), full((1, D_MODEL)),
                  tile(C_CONV), halo, full((HALO, C_CONV)), full((HALO, C_CONV)), full((1, C_CONV)),
                  full((1, C_CONV)), full((1, C_CONV)), full((1, C_CONV)),
                  tile(D_ATTN), full((1, D_ATTN)), full((D_MODEL, D_MODEL)),
                  full((1, D_MODEL)), full((1, D_MODEL)), full((ROUTE_ROWS, D_MODEL)),
                  full((ROUTE_ROWS, D_MODEL)), full((ROUTE_ROWS, LANES)), full((tm, tm))],
        out_specs=[tile(D_MODEL), tile(HALF_D), pl.BlockSpec((8, tm), lambda i, j: (0, i * nt + j)), tile(8),
                   full((ROUTE_ROWS, LANES))],
        out_shape=[jax.ShapeDtypeStruct((b, seq, D_MODEL), F32),
                   jax.ShapeDtypeStruct((b, seq, HALF_D), jnp.uint32),
                   jax.ShapeDtypeStruct((8, b * seq), jnp.int32),
                   jax.ShapeDtypeStruct((b, seq, 8), F32),
                   jax.ShapeDtypeStruct((ROUTE_ROWS, LANES), F32)],
        scratch_shapes=[pltpu.VMEM((HALO + tm, C_CONV), F32),
                        pltpu.VMEM((SUBLANES - 1, HALO + tm, C_CONV), F32),
                        pltpu.VMEM((ROUTE_ROWS, LANES), F32)],
        compiler_params=pltpu.CompilerParams(dimension_semantics=("arbitrary", "arbitrary"),
                                             vmem_limit_bytes=VMEM_LIMIT),
        name="mix",
    )(x, ig, ib, u, u, pre, cw, cb, clg, clb, cog, o, aog, wo, l1g, l1b, wrh, wrl, br, earlier)


def _tables_kernel(ps_ref, pe_ref, cn_ref, ri_ref, dest_ref, blk_ref):
    ri = ri_ref[...]
    start = jnp.zeros_like(ri)
    for e in range(N_EXPERTS):
        start = jnp.where(ri == e, ps_ref[e], start)
    dest_ref[...] = jnp.zeros_like(dest_ref)
    dest_ref[0:2, :] = start[0:2, :] + ri_ref[2:4, :]
    blk_start = lax.broadcasted_iota(jnp.int32, blk_ref.shape, 1) * MOE_BM
    be = jnp.zeros(blk_ref.shape, jnp.int32)
    for e in range(N_EXPERTS):
        be = be + jnp.where(pe_ref[e] <= blk_start, 1, 0)
    be = jnp.minimum(be, N_EXPERTS - 1)
    end = jnp.zeros(blk_ref.shape, jnp.int32)
    for e in range(N_EXPERTS):
        end = jnp.where(be == e, ps_ref[e] + cn_ref[e], end)
    valid = jnp.clip(end - blk_start, 0, MOE_BM)
    row = lax.broadcasted_iota(jnp.int32, blk_ref.shape, 0)
    blk_ref[...] = jnp.where(row == 0, be, valid)


def _tables_call(pad_start, pad_end, counts, ri, n_blocks):
    t = ri.shape[1]
    blk_w = -(-n_blocks // LANES) * LANES
    return pl.pallas_call(
        _tables_kernel,
        grid_spec=pltpu.PrefetchScalarGridSpec(
            num_scalar_prefetch=3,
            grid=(1,),
            in_specs=[pl.BlockSpec((SUBLANES, t), lambda i, *_: (0, 0))],
            out_specs=[pl.BlockSpec((SUBLANES, t), lambda i, *_: (0, 0)),
                       pl.BlockSpec((SUBLANES, blk_w), lambda i, *_: (0, 0))]),
        out_shape=[jax.ShapeDtypeStruct((SUBLANES, t), jnp.int32),
                   jax.ShapeDtypeStruct((SUBLANES, blk_w), jnp.int32)],
        compiler_params=pltpu.CompilerParams(dimension_semantics=("arbitrary",),
                                             vmem_limit_bytes=VMEM_LIMIT),
        name="tables",
    )(pad_start, pad_end, counts, ri)


def _expert_kernel(first_ref, nblk_ref, nrow_ref, x_hbm, wg_ref, wu_ref, wd_ref, y_hbm, wg_sc, wu_sc, wd_sc):
    e = pl.program_id(0)
    first = first_ref[e]
    n_rows = nrow_ref[e]
    wg_sc[...] = wg_ref[0, 0].astype(BF16)
    wu_sc[...] = wu_ref[0, 0].astype(BF16)
    wd_sc[...] = wd_ref[0, 0].astype(BF16)

    def block(idx, x_ref, y_ref):
        rowid = lax.broadcasted_iota(jnp.int32, x_ref.shape, 0)
        x_lo, x_hi = _unpack_bf16_pair(jnp.where(rowid < n_rows - idx[0] * MOE_BM, x_ref[...], jnp.uint32(0)))
        x_lo = x_lo.astype(BF16)
        x_hi = x_hi.astype(BF16)
        g = (jnp.dot(x_lo, wg_sc[:HALF_D, :], preferred_element_type=F32)
             + jnp.dot(x_hi, wg_sc[HALF_D:, :], preferred_element_type=F32))
        up = (jnp.dot(x_lo, wu_sc[:HALF_D, :], preferred_element_type=F32)
              + jnp.dot(x_hi, wu_sc[HALF_D:, :], preferred_element_type=F32))
        act = g * _sigmoid(g) * up
        y = jnp.dot(act.astype(BF16), wd_sc[...], preferred_element_type=F32)
        y_ref[...] = _pack_bf16_pair(y[:, :HALF_D], y[:, HALF_D:])

    @pl.when(nblk_ref[e] > 0)
    def _():
        rows = pl.BlockSpec((MOE_BM, HALF_D), lambda j: (first + j, 0))
        pltpu.emit_pipeline(block, grid=(nblk_ref[e],), in_specs=[rows], out_specs=[rows],
                            _explicit_indices=True)(x_hbm, y_hbm)


def _expert_call(blk_first, blk_count, row_count, xb, w_gate, w_up, w_down):
    wspec = lambda a, b: pl.BlockSpec((1, 1, a, b), lambda e, *_: (0, e, 0, 0))
    return pl.pallas_call(
        _expert_kernel,
        grid_spec=pltpu.PrefetchScalarGridSpec(
            num_scalar_prefetch=3,
            grid=(N_EXPERTS,),
            in_specs=[pl.BlockSpec(memory_space=pl.ANY),
                      wspec(D_MODEL, D_EXPERT), wspec(D_MODEL, D_EXPERT), wspec(D_EXPERT, D_MODEL)],
            out_specs=pl.BlockSpec(memory_space=pl.ANY),
            scratch_shapes=[pltpu.VMEM((D_MODEL, D_EXPERT), BF16), pltpu.VMEM((D_MODEL, D_EXPERT), BF16),
                            pltpu.VMEM((D_EXPERT, D_MODEL), BF16)]),
        out_shape=jax.ShapeDtypeStruct(xb.shape, jnp.uint32),
        compiler_params=pltpu.CompilerParams(dimension_semantics=("arbitrary",),
                                             vmem_limit_bytes=VMEM_LIMIT),
        name="experts",
    )(blk_first, blk_count, row_count, xb, w_gate, w_up, w_down)


def _sc_gather_rows(table, idx):
    m = idx.shape[0]
    width = table.shape[1]
    sc = plsc.get_sparse_core_info()
    workers = sc.num_cores * sc.num_subcores
    per_worker = m // workers
    mesh = plsc.VectorSubcoreMesh(core_axis_name="c", subcore_axis_name="s")

    n_win = per_worker // SC_WINDOW
    assert n_win % 2 == 0 and n_win >= 2
    buf = lambda: [pltpu.VMEM((SC_WINDOW, width), table.dtype), pltpu.SemaphoreType.DMA]

    @functools.partial(pl.kernel, mesh=mesh, out_type=jax.ShapeDtypeStruct((m, width), table.dtype),
                       scratch_types=[pltpu.VMEM((per_worker,), jnp.int32)] + buf() + buf())
    def gather(table_hbm, idx_hbm, out_hbm, idx_v, rows_a, sem_a, rows_b, sem_b):
        base = (lax.axis_index("s") * sc.num_cores + lax.axis_index("c")) * per_worker
        bufs = ((rows_a, sem_a), (rows_b, sem_b))
        pltpu.sync_copy(idx_hbm.at[pl.ds(base, per_worker)], idx_v)

        def window_copy(win, which):
            rows_v, sem = bufs[which]
            return pltpu.make_async_copy(table_hbm.at[idx_v.at[pl.ds(win * SC_WINDOW, SC_WINDOW)]], rows_v, sem)

        def fetch(win, which):
            window_copy(win, which).start()

        def flush(win, which):
            window_copy(win, which).wait()
            pltpu.sync_copy(bufs[which][0], out_hbm.at[pl.ds(base + win * SC_WINDOW, SC_WINDOW)])

        fetch(0, 0)

        @pl.loop(0, n_win // 2 - 1)
        def _(pair):
            win = 2 * pair
            fetch(win + 1, 1)
            flush(win, 0)
            fetch(win + 2, 0)
            flush(win + 1, 1)

        fetch(n_win - 1, 1)
        flush(n_win - 2, 0)
        flush(n_win - 1, 1)

    return gather(table, idx)


def _sc_scatter_rows(rows, idx0, idx1, n_out):
    m, width = rows.shape
    sc = plsc.get_sparse_core_info()
    workers = sc.num_cores * sc.num_subcores
    per_worker = m // workers
    mesh = plsc.VectorSubcoreMesh(core_axis_name="c", subcore_axis_name="s")

    n_win = per_worker // SC_WINDOW
    assert n_win % 2 == 0 and n_win >= 2
    buf = lambda: [pltpu.VMEM((SC_WINDOW, width), rows.dtype),
                   pltpu.SemaphoreType.DMA, pltpu.SemaphoreType.DMA, pltpu.SemaphoreType.DMA]

    @functools.partial(pl.kernel, mesh=mesh, out_type=jax.ShapeDtypeStruct((n_out, width), rows.dtype),
                       scratch_types=[pltpu.VMEM((per_worker,), jnp.int32), pltpu.VMEM((per_worker,), jnp.int32)]
                       + buf() + buf())
    def scatter(rows_hbm, idx0_hbm, idx1_hbm, out_hbm, idx0_v, idx1_v, *scratch):
        base = (lax.axis_index("s") * sc.num_cores + lax.axis_index("c")) * per_worker
        bufs = (scratch[:4], scratch[4:])
        pltpu.sync_copy(idx0_hbm.at[pl.ds(base, per_worker)], idx0_v)
        pltpu.sync_copy(idx1_hbm.at[pl.ds(base, per_worker)], idx1_v)

        def load(win, which):
            rows_v, lsem, _, _ = bufs[which]
            pltpu.async_copy(rows_hbm.at[pl.ds(base + win * SC_WINDOW, SC_WINDOW)], rows_v, lsem)

        def send(win, which):
            rows_v, lsem, sem0, sem1 = bufs[which]
            here = pl.ds(win * SC_WINDOW, SC_WINDOW)
            pltpu.make_async_copy(rows_hbm.at[pl.ds(0, SC_WINDOW)], rows_v, lsem).wait()
            first = pltpu.async_copy(rows_v, out_hbm.at[idx0_v.at[here]], sem0)
            second = pltpu.async_copy(rows_v, out_hbm.at[idx1_v.at[here]], sem1)
            first.wait()
            second.wait()

        load(0, 0)

        @pl.loop(0, n_win // 2 - 1)
        def _(pair):
            win = 2 * pair
            load(win + 1, 1)
            send(win, 0)
            load(win + 2, 0)
            send(win + 1, 1)

        load(n_win - 1, 1)
        send(n_win - 2, 0)
        send(n_win - 1, 1)

    return scatter(rows, idx0, idx1)


def _combine_rows_kernel(h_ref, w_ref, g_ref, b_ref, y0_ref, y1_ref, o_ref):
    w = w_ref[...]
    y0_lo, y0_hi = _unpack_bf16_pair(y0_ref[...])
    y1_lo, y1_hi = _unpack_bf16_pair(y1_ref[...])
    y = jnp.concatenate([w[:, 0:1] * y0_lo + w[:, 1:2] * y1_lo, w[:, 0:1] * y0_hi + w[:, 1:2] * y1_hi], axis=1)
    o_ref[...] = _ln(DEEPNORM_ALPHA * h_ref[...] + y, g_ref[...], b_ref[...])


def _combine_rows_call(h1, rw, g, b, yg):
    t = h1.shape[0]
    tm = COMB_TM
    n = t // tm
    return pl.pallas_call(
        _combine_rows_kernel,
        grid=(n,),
        in_specs=[pl.BlockSpec((tm, D_MODEL), lambda i: (i, 0)),
                  pl.BlockSpec((tm, 8), lambda i: (i, 0)),
                  pl.BlockSpec((1, D_MODEL), lambda i: (0, 0)),
                  pl.BlockSpec((1, D_MODEL), lambda i: (0, 0)),
                  pl.BlockSpec((tm, HALF_D), lambda i: (i, 0)),
                  pl.BlockSpec((tm, HALF_D), lambda i: (n + i, 0))],
        out_specs=pl.BlockSpec((tm, D_MODEL), lambda i: (i, 0)),
        out_shape=jax.ShapeDtypeStruct((t, D_MODEL), F32),
        compiler_params=pltpu.CompilerParams(dimension_semantics=("arbitrary",),
                                             vmem_limit_bytes=VMEM_LIMIT),
        name="combine_rows",
    )(h1, rw, g, b, yg, yg)


def _take_cols(w, idx):
    wz = jnp.concatenate([w, jnp.zeros((w.shape[0], 1), w.dtype)], axis=1)
    return wz[:, np.where(idx < 0, w.shape[1], idx)]


def _layout_indices():
    lane_src = np.full((HEAD_PAD,), -1, np.int64)
    lane_src[0:ROPE_HALF] = QK_NOPE + np.arange(ROPE_HALF)
    lane_src[ROPE_HALF:X2_LANE] = np.arange(X2_LANE - ROPE_HALF)
    lane_src[X2_LANE:X2_LANE + ROPE_HALF] = QK_NOPE + ROPE_HALF + np.arange(ROPE_HALF)
    n_lo = X2_LANE - ROPE_HALF
    lane_src[X2_LANE + ROPE_HALF:X2_LANE + ROPE_HALF + QK_NOPE - n_lo] = n_lo + np.arange(QK_NOPE - n_lo)
    q_idx = np.concatenate([np.where(lane_src >= 0, lane_src + (QK_NOPE + QK_ROPE) * h, -1)
                            for h in range(N_HEADS)])
    nope_src = np.where(lane_src < QK_NOPE, lane_src, -1)
    k_idx = np.concatenate([np.where(nope_src >= 0, nope_src + (QK_NOPE + V_DIM) * h, -1)
                            for h in range(N_HEADS)])
    v_idx = np.concatenate([QK_NOPE + np.arange(V_DIM) + (QK_NOPE + V_DIM) * h for h in range(N_HEADS)])
    kr_idx = np.full((LANES,), -1, np.int64)
    kr_idx[0:ROPE_HALF] = np.arange(ROPE_HALF)
    kr_idx[X2_LANE:X2_LANE + ROPE_HALF] = ROPE_HALF + np.arange(ROPE_HALF)
    return q_idx, k_idx, v_idx, kr_idx


def _rope_lane_tables(length):
    inv_freq = np.float32(ROPE_THETA) ** (-np.arange(ROPE_HALF, dtype=np.float32) / np.float32(ROPE_HALF))
    ang = np.arange(length, dtype=np.float32)[:, None] * inv_freq[None, :].astype(np.float32)
    cos, sin = np.cos(ang).astype(np.float32), np.sin(ang).astype(np.float32)
    ctab = np.ones((length, LANES), np.float32)
    stab = np.zeros((length, LANES), np.float32)
    for lo, sign in ((0, -1.0), (X2_LANE, 1.0)):
        ctab[:, lo:lo + ROPE_HALF] = cos
        stab[:, lo:lo + ROPE_HALF] = sign * sin
    return jnp.asarray(ctab), jnp.asarray(stab)


def kernel(x, meta_tokens, ln_in_g, ln_in_b, w_in, conv_w, conv_b, conv_ln_g, conv_ln_b, q_norm_g, w_uq,
           kv_norm_g, w_ukv, conv_out_g, attn_out_g, w_o, ln1_g, ln1_b, w_rg, b_rg, w_re, b_re,
           w_gate, w_up, w_down, ln2_g, ln2_b):
    assert w_in.shape[0] == 1, "single layer"
    bsz, seq, d = x.shape
    t = bsz * seq
    row = lambda a: a.reshape(1, -1)

    q_idx, k_idx, v_idx, kr_idx = _layout_indices()
    c4 = 2 * C_CONV + Q_LORA + KV_LORA
    w_in_l = jnp.concatenate([w_in[0][:, :c4], _take_cols(w_in[0][:, c4:], kr_idx)], axis=1).astype(BF16)
    wuq_l = _take_cols(w_uq[0], q_idx).astype(BF16)
    wuk_l = _take_cols(w_ukv[0], k_idx).astype(BF16)
    wvt_l = w_ukv[0][:, v_idx].T.astype(BF16)
    wo_l = w_o[0].astype(BF16)
    n_route_pad = ROUTE_ROWS - N_EXPERTS - N_GROUPS
    wr_t = jnp.concatenate([w_re[0].T, w_rg[0].T, jnp.zeros((n_route_pad, d), F32)], axis=0)
    wrh_l = wr_t.astype(BF16)
    wrl_l = (wr_t - wrh_l.astype(F32)).astype(BF16)
    br_l = jnp.broadcast_to(jnp.concatenate([b_re[0], b_rg[0], jnp.zeros((n_route_pad,), F32)])[:, None],
                            (ROUTE_ROWS, LANES))
    ctab, stab = _rope_lane_tables(N_META + seq)

    proj_w = (row(ln_in_g), row(ln_in_b), w_in_l, row(q_norm_g[0]), wuq_l, row(kv_norm_g[0]), wuk_l, wvt_l)
    u_m, _, k_m, vt_m = _proj_call(meta_tokens.astype(F32), N_META, 1, *proj_w, ctab[:N_META], stab[:N_META])
    u, q, k, vt = _proj_call(x.reshape(t, d), PROJ_TM, seq // PROJ_TM, *proj_w, ctab[N_META:], stab[N_META:])

    o = _attn_call(q.reshape(bsz, seq, -1), k.reshape(bsz, seq, -1),
                   vt.reshape(bsz, seq // ATT_T, D_ATTN, ATT_T), k_m, vt_m)

    pre = jnp.concatenate([jnp.zeros((HALO - N_META, C_CONV), F32), u_m], axis=0)
    cw = jnp.concatenate([conv_w[0], jnp.zeros((HALO - CONV_K, C_CONV), F32)], axis=0)
    h1, hp, ri, rw, cnt = _mix_call(x, row(ln_in_g), row(ln_in_b), u.reshape(bsz, seq, C_CONV), pre, cw,
                                    row(conv_b[0]), row(conv_ln_g[0]), row(conv_ln_b[0]), row(conv_out_g[0]),
                                    o, row(attn_out_g[0]), wo_l, row(ln1_g[0]), row(ln1_b[0]), wrh_l, wrl_l, br_l)

    counts = cnt[:N_EXPERTS, 0].astype(jnp.int32)
    padded = (counts + MOE_BM - 1) // MOE_BM * MOE_BM
    pad_end = jnp.cumsum(padded)
    pad_start = pad_end - padded
    n_blocks = (2 * t) // MOE_BM + N_EXPERTS
    dest, _ = _tables_call(pad_start, pad_end, counts, ri, n_blocks)

    h1f = h1.reshape(t, d)
    dest0, dest1 = dest[0], dest[1]
    xb = _sc_scatter_rows(hp.reshape(t, HALF_D), dest0, dest1, n_blocks * MOE_BM)
    yb = _expert_call(pad_start // MOE_BM, padded // MOE_BM, counts, xb, w_gate, w_up, w_down)
    yg = _sc_gather_rows(yb, jnp.concatenate([dest0, dest1]))
    out = _combine_rows_call(h1f, rw.reshape(t, 8), row(ln2_g[0]), row(ln2_b[0]), yg)
    return out.reshape(bsz, seq, d)
```

```python
import functools
import math

import numpy as np
import jax
import jax.numpy as jnp
from jax import lax
from jax.experimental import pallas as pl
from jax.experimental.pallas import tpu as pltpu
from jax.experimental.pallas import tpu_sc as plsc

F32 = jnp.float32
BF16 = jnp.bfloat16

D_MODEL = 1024
HALF_D = D_MODEL // 2
N_META = 16
CHUNK = 64
C_CONV = 512
CONV_K = 31
N_HEADS = 8
QK_NOPE = 64
QK_ROPE = 32
V_DIM = 64
Q_LORA = 384
KV_LORA = 256
ROPE_THETA = 10000.0
D_ATTN = N_HEADS * V_DIM
N_GROUPS = 4
EXPERTS_PER_GROUP = 8
N_EXPERTS = N_GROUPS * EXPERTS_PER_GROUP
D_EXPERT = 256
DEEPNORM_ALPHA = 2.0 ** 0.25
EPS = 1e-5
NEG_INF = -1e30

LANES = 128
SUBLANES = 8
HEAD_PAD = LANES
ROPE_HALF = QK_ROPE // 2
X2_LANE = 64
Z_COLS = 2 * C_CONV + Q_LORA + KV_LORA + LANES
Q_SCALE = (QK_NOPE + QK_ROPE) ** -0.5 * math.log2(math.e)

PROJ_TM = 1024
ATT_T = 256
ATT_HEADS = 8
MIX_TM = 1024
ROUTE_ROWS = 48
HALO = 32
MOE_BM = 1024
COMB_TM = 1024
SC_WINDOW = 64
VMEM_LIMIT = 56 * 1024 * 1024


def _ln(x, g, b):
    mu = jnp.mean(x, axis=-1, keepdims=True)
    xc = x - mu
    var = jnp.mean(xc * xc, axis=-1, keepdims=True)
    return xc * lax.rsqrt(var + EPS) * g + b


def _rms(x, g):
    return x * lax.rsqrt(jnp.mean(x * x, axis=-1, keepdims=True) + EPS) * g


def _sigmoid(x):
    return 1.0 / (1.0 + jnp.exp(-x))


def _pack_bf16_pair(lo, hi):
    lo_bits = lax.bitcast_convert_type(lo.astype(BF16).astype(F32), jnp.uint32)
    hi_bits = lax.bitcast_convert_type(hi.astype(BF16).astype(F32), jnp.uint32)
    return (lo_bits >> 16) | (hi_bits & jnp.uint32(0xFFFF0000))


def _unpack_bf16_pair(packed):
    lo = lax.bitcast_convert_type(packed << 16, F32)
    hi = lax.bitcast_convert_type(packed & jnp.uint32(0xFFFF0000), F32)
    return lo, hi


def _proj_kernel(x_ref, g_ref, b_ref, win_ref, qg_ref, wuq_ref, kvg_ref, wuk_ref, wvt_ref, c_ref, s_ref,
                 u_ref, q_ref, k_ref, vt_ref):
    n_part, part = vt_ref.shape[1], vt_ref.shape[3]
    zs = []
    for r in range(n_part):
        h = _ln(x_ref[r * part:(r + 1) * part, :], g_ref[...], b_ref[...])
        zs.append(jnp.dot(h.astype(BF16), win_ref[...], preferred_element_type=F32))
    c0 = 2 * C_CONV
    for r, z in enumerate(zs):
        rows = slice(r * part, (r + 1) * part)
        u_ref[rows, :] = z[:, :C_CONV] * _sigmoid(z[:, C_CONV:c0])
        cq = z[:, c0:c0 + Q_LORA]
        ckv = z[:, c0 + Q_LORA:c0 + Q_LORA + KV_LORA]
        kr = z[:, c0 + Q_LORA + KV_LORA:]
        cs = c_ref[rows, :]
        sn = s_ref[rows, :]
        q = jnp.dot(_rms(cq, qg_ref[...]).astype(BF16), wuq_ref[...], preferred_element_type=F32)
        kvn = _rms(ckv, kvg_ref[...]).astype(BF16)
        kn = jnp.dot(kvn, wuk_ref[...], preferred_element_type=F32)
        kr_rot = kr * cs + pltpu.roll(kr, X2_LANE, 1) * sn
        for hh in range(N_HEADS):
            sl = slice(HEAD_PAD * hh, HEAD_PAD * (hh + 1))
            qh = q[:, sl]
            qh = (qh * cs + pltpu.roll(qh, X2_LANE, 1) * sn) * Q_SCALE
            q_ref[rows, sl] = qh.astype(BF16)
            k_ref[rows, sl] = (kn[:, sl] + kr_rot).astype(BF16)
        vt = lax.dot_general(wvt_ref[...], kvn, (((1,), (1,)), ((), ())), preferred_element_type=F32)
        vt_ref[0, r] = vt.astype(BF16)


def _proj_call(x2d, tm, pos_blocks, ln_g, ln_b, w_in, qg, wuq, kvg, wuk, wvt, ctab, stab):
    t = x2d.shape[0]
    sub_w = min(tm, ATT_T)
    n_sub = tm // sub_w
    full = lambda shape: pl.BlockSpec(shape, lambda i: (0,) * len(shape))
    row = lambda w: pl.BlockSpec((tm, w), lambda i: (i, 0))
    tab = pl.BlockSpec((tm, LANES), lambda i: (i % pos_blocks, 0))
    return pl.pallas_call(
        _proj_kernel,
        grid=(t // tm,),
        in_specs=[row(D_MODEL), full((1, D_MODEL)), full((1, D_MODEL)), full((D_MODEL, Z_COLS)),
                  full((1, Q_LORA)), full((Q_LORA, N_HEADS * HEAD_PAD)),
                  full((1, KV_LORA)), full((KV_LORA, N_HEADS * HEAD_PAD)), full((D_ATTN, KV_LORA)), tab, tab],
        out_specs=[row(C_CONV), row(N_HEADS * HEAD_PAD), row(N_HEADS * HEAD_PAD),
                   pl.BlockSpec((1, n_sub, D_ATTN, sub_w), lambda i: (i, 0, 0, 0))],
        out_shape=[jax.ShapeDtypeStruct((t, C_CONV), F32),
                   jax.ShapeDtypeStruct((t, N_HEADS * HEAD_PAD), BF16),
                   jax.ShapeDtypeStruct((t, N_HEADS * HEAD_PAD), BF16),
                   jax.ShapeDtypeStruct((t // tm, n_sub, D_ATTN, sub_w), BF16)],
        compiler_params=pltpu.CompilerParams(dimension_semantics=("arbitrary",),
                                             vmem_limit_bytes=VMEM_LIMIT),
        name="proj",
    )(x2d, ln_g, ln_b, w_in, qg, wuq, kvg, wuk, wvt, ctab, stab)


def _attn_kernel(q_ref, k_ref, vt_ref, km_ref, vtm_ref, o_ref, s_sc):
    seq = q_ref.shape[1]
    nq = seq // ATT_T
    key_chunk = lax.broadcasted_iota(jnp.int32, (ATT_T, ATT_T), 0) // CHUNK
    qry_chunk = lax.broadcasted_iota(jnp.int32, (ATT_T, ATT_T), 1) // CHUNK
    diag_visible = key_chunk <= qry_chunk
    nt = (((1,), (1,)), ((), ()))
    heads = range(ATT_HEADS)
    hsl = [slice(HEAD_PAD * hh, HEAD_PAD * (hh + 1)) for hh in heads]
    vsl = [slice(V_DIM * hh, V_DIM * (hh + 1)) for hh in heads]

    def update(s, vt, m, l, acc):
        m_new = jnp.maximum(m, jnp.max(s, axis=0, keepdims=True))
        alpha = jnp.exp2(m - m_new)
        p = jnp.exp2(s - m_new)
        l = alpha * l + jnp.sum(p, axis=0, keepdims=True)
        acc = alpha * acc + jnp.dot(vt, p.astype(BF16), preferred_element_type=F32)
        return m_new, l, acc

    def scores(qi, j, hh):
        q0 = pl.multiple_of(qi * ATT_T, ATT_T)
        k0 = pl.multiple_of(j * ATT_T, ATT_T)
        return lax.dot_general(k_ref[0, pl.ds(k0, ATT_T), hsl[hh]], q_ref[0, pl.ds(q0, ATT_T), hsl[hh]], nt,
                               preferred_element_type=F32)

    for hh in heads:
        s_sc[0, hh] = scores(0, 0, hh)

    def q_body(qi, carry):
        row0 = pl.multiple_of(qi * ATT_T, ATT_T)
        slot = qi % 2
        state = []
        for hh in heads:
            state += [jnp.full((1, ATT_T), NEG_INF, F32), jnp.zeros((1, ATT_T), F32),
                      jnp.zeros((V_DIM, ATT_T), F32)]

        def kv_body(j, st):
            out = []
            s_next = scores(qi, j + 1, 0)
            for hh in heads:
                s_cur = s_sc[slot, hh]
                s_after = scores(qi, j + 1, hh + 1) if hh + 1 < ATT_HEADS else None
                out += update(s_cur, vt_ref[0, j, vsl[hh], :], *st[3 * hh:3 * hh + 3])
                s_sc[slot, hh] = s_next
                s_next = s_after
            return tuple(out)

        state = lax.fori_loop(0, qi, kv_body, tuple(state))
        q_next = jnp.minimum(qi + 1, nq - 1)
        sms = [lax.dot_general(km_ref[:, hsl[hh]], q_ref[0, pl.ds(row0, ATT_T), hsl[hh]], nt,
                               preferred_element_type=F32) for hh in heads]
        outs = []
        for hh in heads:
            m, l, acc = state[3 * hh:3 * hh + 3]
            s = jnp.where(diag_visible, s_sc[slot, hh], NEG_INF)
            sm = sms[hh]
            m_new = jnp.maximum(m, jnp.maximum(jnp.max(s, axis=0, keepdims=True),
                                               jnp.max(sm, axis=0, keepdims=True)))
            alpha = jnp.exp2(m - m_new)
            p = jnp.exp2(s - m_new)
            pm = jnp.exp2(sm - m_new)
            l = alpha * l + jnp.sum(p, axis=0, keepdims=True) + jnp.sum(pm, axis=0, keepdims=True)
            acc = (alpha * acc + jnp.dot(vt_ref[0, qi, vsl[hh], :], p.astype(BF16), preferred_element_type=F32)
                   + jnp.dot(vtm_ref[0, 0, vsl[hh], :], pm.astype(BF16), preferred_element_type=F32))
            outs.append(acc / l)
        for hh in heads:
            s_sc[1 - slot, hh] = scores(q_next, 0, hh)
        o_ref[0, pl.ds(row0, ATT_T), :] = jnp.concatenate(outs, axis=0).T
        return carry

    lax.fori_loop(0, nq, q_body, 0)


def _attn_call(q, k, vt, k_meta, vt_meta):
    b, seq, _ = q.shape
    nk = seq // ATT_T
    hb = ATT_HEADS
    return pl.pallas_call(
        _attn_kernel,
        grid=(b, N_HEADS // hb),
        in_specs=[pl.BlockSpec((1, seq, hb * HEAD_PAD), lambda i, j: (i, 0, j)),
                  pl.BlockSpec((1, seq, hb * HEAD_PAD), lambda i, j: (i, 0, j)),
                  pl.BlockSpec((1, nk, hb * V_DIM, ATT_T), lambda i, j: (i, 0, j, 0)),
                  pl.BlockSpec((N_META, hb * HEAD_PAD), lambda i, j: (0, j)),
                  pl.BlockSpec((1, 1, hb * V_DIM, N_META), lambda i, j: (0, 0, j, 0))],
        out_specs=pl.BlockSpec((1, seq, hb * V_DIM), lambda i, j: (i, 0, j)),
        out_shape=jax.ShapeDtypeStruct((b, seq, D_ATTN), F32),
        scratch_shapes=[pltpu.VMEM((2, hb, ATT_T, ATT_T), F32)],
        compiler_params=pltpu.CompilerParams(dimension_semantics=("arbitrary", "arbitrary"),
                                             vmem_limit_bytes=VMEM_LIMIT),
        name="attn",
    )(q, k, vt, k_meta, vt_meta)


def _mix_kernel(x_ref, ig_ref, ib_ref, u_ref, uh_ref, pre_ref, cw_ref, cb_ref, clg_ref, clb_ref,
                cog_ref, o_ref, aog_ref, wo_ref, l1g_ref, l1b_ref, wrh_ref, wrl_ref, br_ref, tri_ref,
                h1_ref, hp_ref, ri_ref, rw_ref, cnt_ref, win_sc, sh_sc, carry_sc):
    bi = pl.program_id(0)
    ti = pl.program_id(1)
    tm = MIX_TM

    @pl.when((bi == 0) & (ti == 0))
    def _():
        carry_sc[...] = jnp.zeros_like(carry_sc)

    win_sc[0:HALO, :] = jnp.where(ti == 0, pre_ref[...], uh_ref[0])
    win_sc[HALO:, :] = u_ref[0]
    first = HALO - (CONV_K - 1)
    for res in range(1, SUBLANES):
        rows = tm + HALO - SUBLANES
        sh_sc[res - 1, 0:rows, :] = win_sc[pl.ds(res, rows), :]

    acc = jnp.broadcast_to(cb_ref[...], (tm, C_CONV))
    for o in range(first, first + CONV_K):
        res, lo = o % SUBLANES, o - o % SUBLANES
        src = sh_sc[res - 1, lo:lo + tm, :] if res else win_sc[lo:lo + tm, :]
        acc = acc + src * cw_ref[o - first:o - first + 1, :]
    c = _ln(acc, clg_ref[...], clb_ref[...])
    c = c * _sigmoid(c)
    cn = _rms(c, cog_ref[...])
    on = _rms(o_ref[0], aog_ref[...])
    y = (jnp.dot(cn.astype(BF16), wo_ref[0:C_CONV, :], preferred_element_type=F32)
         + jnp.dot(on.astype(BF16), wo_ref[C_CONV:, :], preferred_element_type=F32))
    h = _ln(x_ref[0], ig_ref[...], ib_ref[...])
    h1 = _ln(DEEPNORM_ALPHA * h + y, l1g_ref[...], l1b_ref[...])
    h1_ref[0] = h1
    hp_ref[0] = _pack_bf16_pair(h1[:, :HALF_D], h1[:, HALF_D:])

    nt = (((1,), (1,)), ((), ()))
    h_hi = h1.astype(BF16)
    h_lo = (h1 - h_hi.astype(F32)).astype(BF16)
    lt = (lax.dot_general(wrh_ref[...], h_hi, nt, preferred_element_type=F32)
          + lax.dot_general(wrl_ref[...], h_hi, nt, preferred_element_type=F32)
          + lax.dot_general(wrh_ref[...], h_lo, nt, preferred_element_type=F32)) + br_ref[:, 0:1]
    row = lax.broadcasted_iota(jnp.int32, (ROUTE_ROWS, tm), 0)
    big = jnp.int32(1 << 20)
    is_g = (row >= N_EXPERTS) & (row < N_EXPERTS + N_GROUPS)
    gl = jnp.where(is_g, lt, -jnp.inf)
    gmax = jnp.max(gl, axis=0, keepdims=True)
    g_idx = jnp.min(jnp.where(gl == gmax, row - N_EXPERTS, big), axis=0, keepdims=True)
    g_w = 1.0 / jnp.sum(jnp.exp(gl - gmax), axis=0, keepdims=True)
    in_grp = (row < N_EXPERTS) & ((row // EXPERTS_PER_GROUP) == g_idx)
    el = jnp.where(in_grp, lt, -jnp.inf)
    v1 = jnp.max(el, axis=0, keepdims=True)
    i1 = jnp.min(jnp.where(el == v1, row, big), axis=0, keepdims=True)
    el2 = jnp.where(row == i1, -jnp.inf, el)
    v2 = jnp.max(el2, axis=0, keepdims=True)
    i2 = jnp.min(jnp.where(el2 == v2, row, big), axis=0, keepdims=True)
    e2 = jnp.exp(v2 - v1)
    w1 = g_w / (1.0 + e2)
    w2 = g_w * e2 / (1.0 + e2)

    oh1 = row == i1
    oh2 = row == i2
    oh = jnp.where(oh1 | oh2, 1.0, 0.0)
    before = jnp.dot(oh.astype(BF16), tri_ref[...], preferred_element_type=F32) + carry_sc[:, 0:1]
    r1 = jnp.sum(jnp.where(oh1, before, 0.0), axis=0, keepdims=True).astype(jnp.int32)
    r2 = jnp.sum(jnp.where(oh2, before, 0.0), axis=0, keepdims=True).astype(jnp.int32)
    carry_sc[...] = carry_sc[...] + jnp.sum(oh, axis=1, keepdims=True)
    cnt_ref[...] = carry_sc[...]

    sub = lax.broadcasted_iota(jnp.int32, (SUBLANES, tm), 0)
    ri_ref[...] = jnp.where(sub == 0, i1, jnp.where(sub == 1, i2, jnp.where(sub == 2, r1,
                            jnp.where(sub == 3, r2, 0))))
    lrow = lax.broadcasted_iota(jnp.int32, (LANES, tm), 0)
    rw_ref[0] = jnp.where(lrow == 0, w1, jnp.where(lrow == 1, w2, 0.0)).T[:, :8]


def _mix_call(x, ig, ib, u, pre, cw, cb, clg, clb, cog, o, aog, wo, l1g, l1b, wrh, wrl, br):
    b, seq, _ = x.shape
    tm = MIX_TM
    nt = seq // tm
    full = lambda shape: pl.BlockSpec(shape, lambda i, j: (0,) * len(shape))
    tile = lambda w: pl.BlockSpec((1, tm, w), lambda i, j: (i, j, 0))
    halo = pl.BlockSpec((1, HALO, C_CONV), lambda i, j: (i, jnp.maximum(j * (tm // HALO) - 1, 0), 0))
    earlier = jnp.asarray(np.triu(np.ones((tm, tm), np.float32), k=1), BF16)
    return pl.pallas_call(
        _mix_kernel,
        grid=(b, nt),
        in_specs=[tile(D_MODEL), full((1, D_MODEL)), full((1, D_MODEL)),
                  tile(C_CONV), halo, full((HALO, C_CONV)), full((HALO, C_CONV)), full((1, C_CONV)),
                  full((1, C_CONV)), full((1, C_CONV)), full((1, C_CONV)),
                  tile(D_ATTN), full((1, D_ATTN)), full((D_MODEL, D_MODEL)),
                  full((1, D_MODEL)), full((1, D_MODEL)), full((ROUTE_ROWS, D_MODEL)),
                  full((ROUTE_ROWS, D_MODEL)), full((ROUTE_ROWS, LANES)), full((tm, tm))],
        out_specs=[tile(D_MODEL), tile(HALF_D), pl.BlockSpec((8, tm), lambda i, j: (0, i * nt + j)), tile(8),
                   full((ROUTE_ROWS, LANES))],
        out_shape=[jax.ShapeDtypeStruct((b, seq, D_MODEL), F32),
                   jax.ShapeDtypeStruct((b, seq, HALF_D), jnp.uint32),
                   jax.ShapeDtypeStruct((8, b * seq), jnp.int32),
                   jax.ShapeDtypeStruct((b, seq, 8), F32),
                   jax.ShapeDtypeStruct((ROUTE_ROWS, LANES), F32)],
        scratch_shapes=[pltpu.VMEM((HALO + tm, C_CONV), F32),
                        pltpu.VMEM((SUBLANES - 1, HALO + tm, C_CONV), F32),
                        pltpu.VMEM((ROUTE_ROWS, LANES), F32)],
        compiler_params=pltpu.CompilerParams(dimension_semantics=("arbitrary", "arbitrary"),
                                             vmem_limit_bytes=VMEM_LIMIT),
        name="mix",
    )(x, ig, ib, u, u, pre, cw, cb, clg, clb, cog, o, aog, wo, l1g, l1b, wrh, wrl, br, earlier)


def _tables_kernel(ps_ref, pe_ref, cn_ref, ri_ref, dest_ref, blk_ref):
    ri = ri_ref[...]
    start = jnp.zeros_like(ri)
    for e in range(N_EXPERTS):
        start = jnp.where(ri == e, ps_ref[e], start)
    dest_ref[...] = jnp.zeros_like(dest_ref)
    dest_ref[0:2, :] = start[0:2, :] + ri_ref[2:4, :]
    blk_start = lax.broadcasted_iota(jnp.int32, blk_ref.shape, 1) * MOE_BM
    be = jnp.zeros(blk_ref.shape, jnp.int32)
    for e in range(N_EXPERTS):
        be = be + jnp.where(pe_ref[e] <= blk_start, 1, 0)
    be = jnp.minimum(be, N_EXPERTS - 1)
    end = jnp.zeros(blk_ref.shape, jnp.int32)
    for e in range(N_EXPERTS):
        end = jnp.where(be == e, ps_ref[e] + cn_ref[e], end)
    valid = jnp.clip(end - blk_start, 0, MOE_BM)
    row = lax.broadcasted_iota(jnp.int32, blk_ref.shape, 0)
    blk_ref[...] = jnp.where(row == 0, be, valid)


def _tables_call(pad_start, pad_end, counts, ri, n_blocks):
    t = ri.shape[1]
    blk_w = -(-n_blocks // LANES) * LANES
    return pl.pallas_call(
        _tables_kernel,
        grid_spec=pltpu.PrefetchScalarGridSpec(
            num_scalar_prefetch=3,
            grid=(1,),
            in_specs=[pl.BlockSpec((SUBLANES, t), lambda i, *_: (0, 0))],
            out_specs=[pl.BlockSpec((SUBLANES, t), lambda i, *_: (0, 0)),
                       pl.BlockSpec((SUBLANES, blk_w), lambda i, *_: (0, 0))]),
        out_shape=[jax.ShapeDtypeStruct((SUBLANES, t), jnp.int32),
                   jax.ShapeDtypeStruct((SUBLANES, blk_w), jnp.int32)],
        compiler_params=pltpu.CompilerParams(dimension_semantics=("arbitrary",),
                                             vmem_limit_bytes=VMEM_LIMIT),
        name="tables",
    )(pad_start, pad_end, counts, ri)


def _expert_kernel(be_ref, nv_ref, x_ref, wg_ref, wu_ref, wd_ref, y_ref, wg_sc, wu_sc, wd_sc):
    i = pl.program_id(0)
    nv = nv_ref[i]

    @pl.when((i == 0) | (be_ref[i] != be_ref[jnp.maximum(i - 1, 0)]))
    def _():
        wg_sc[...] = wg_ref[0, 0].astype(BF16)
        wu_sc[...] = wu_ref[0, 0].astype(BF16)
        wd_sc[...] = wd_ref[0, 0].astype(BF16)

    @pl.when(nv > 0)
    def _():
        rowid = lax.broadcasted_iota(jnp.int32, x_ref.shape, 0)
        x_lo, x_hi = _unpack_bf16_pair(jnp.where(rowid < nv, x_ref[...], jnp.uint32(0)))
        x_lo = x_lo.astype(BF16)
        x_hi = x_hi.astype(BF16)
        g = (jnp.dot(x_lo, wg_sc[:HALF_D, :], preferred_element_type=F32)
             + jnp.dot(x_hi, wg_sc[HALF_D:, :], preferred_element_type=F32))
        up = (jnp.dot(x_lo, wu_sc[:HALF_D, :], preferred_element_type=F32)
              + jnp.dot(x_hi, wu_sc[HALF_D:, :], preferred_element_type=F32))
        act = g * _sigmoid(g) * up
        y = jnp.dot(act.astype(BF16), wd_sc[...], preferred_element_type=F32)
        y_ref[...] = _pack_bf16_pair(y[:, :HALF_D], y[:, HALF_D:])

    @pl.when(nv == 0)
    def _():
        y_ref[...] = jnp.zeros_like(y_ref)


def _expert_call(blk_e, blk_valid, xb, w_gate, w_up, w_down):
    r = xb.shape[0]
    bm = MOE_BM
    wspec = lambda a, b: pl.BlockSpec((1, 1, a, b), lambda i, be, nv: (0, be[i], 0, 0))
    return pl.pallas_call(
        _expert_kernel,
        grid_spec=pltpu.PrefetchScalarGridSpec(
            num_scalar_prefetch=2,
            grid=(r // bm,),
            in_specs=[pl.BlockSpec((bm, HALF_D), lambda i, be, nv: (i, 0)),
                      wspec(D_MODEL, D_EXPERT), wspec(D_MODEL, D_EXPERT), wspec(D_EXPERT, D_MODEL)],
            out_specs=pl.BlockSpec((bm, HALF_D), lambda i, be, nv: (i, 0)),
            scratch_shapes=[pltpu.VMEM((D_MODEL, D_EXPERT), BF16), pltpu.VMEM((D_MODEL, D_EXPERT), BF16),
                            pltpu.VMEM((D_EXPERT, D_MODEL), BF16)]),
        out_shape=jax.ShapeDtypeStruct((r, HALF_D), jnp.uint32),
        compiler_params=pltpu.CompilerParams(dimension_semantics=("arbitrary",),
                                             vmem_limit_bytes=VMEM_LIMIT),
        name="experts",
    )(blk_e, blk_valid, xb, w_gate, w_up, w_down)


def _sc_gather_rows(table, idx):
    m = idx.shape[0]
    width = table.shape[1]
    sc = plsc.get_sparse_core_info()
    workers = sc.num_cores * sc.num_subcores
    per_worker = m // workers
    mesh = plsc.VectorSubcoreMesh(core_axis_name="c", subcore_axis_name="s")

    n_win = per_worker // SC_WINDOW
    assert n_win % 2 == 0 and n_win >= 2
    buf = lambda: [pltpu.VMEM((SC_WINDOW, width), table.dtype), pltpu.SemaphoreType.DMA]

    @functools.partial(pl.kernel, mesh=mesh, out_type=jax.ShapeDtypeStruct((m, width), table.dtype),
                       scratch_types=[pltpu.VMEM((per_worker,), jnp.int32)] + buf() + buf())
    def gather(table_hbm, idx_hbm, out_hbm, idx_v, rows_a, sem_a, rows_b, sem_b):
        base = (lax.axis_index("s") * sc.num_cores + lax.axis_index("c")) * per_worker
        bufs = ((rows_a, sem_a), (rows_b, sem_b))
        pltpu.sync_copy(idx_hbm.at[pl.ds(base, per_worker)], idx_v)

        def window_copy(win, which):
            rows_v, sem = bufs[which]
            return pltpu.make_async_copy(table_hbm.at[idx_v.at[pl.ds(win * SC_WINDOW, SC_WINDOW)]], rows_v, sem)

        def fetch(win, which):
            window_copy(win, which).start()

        def flush(win, which):
            window_copy(win, which).wait()
            pltpu.sync_copy(bufs[which][0], out_hbm.at[pl.ds(base + win * SC_WINDOW, SC_WINDOW)])

        fetch(0, 0)

        @pl.loop(0, n_win // 2 - 1)
        def _(pair):
            win = 2 * pair
            fetch(win + 1, 1)
            flush(win, 0)
            fetch(win + 2, 0)
            flush(win + 1, 1)

        fetch(n_win - 1, 1)
        flush(n_win - 2, 0)
        flush(n_win - 1, 1)

    return gather(table, idx)


def _sc_scatter_rows(rows, idx0, idx1, n_out):
    m, width = rows.shape
    sc = plsc.get_sparse_core_info()
    workers = sc.num_cores * sc.num_subcores
    per_worker = m // workers
    mesh = plsc.VectorSubcoreMesh(core_axis_name="c", subcore_axis_name="s")

    n_win = per_worker // SC_WINDOW
    assert n_win % 2 == 0 and n_win >= 2
    buf = lambda: [pltpu.VMEM((SC_WINDOW, width), rows.dtype),
                   pltpu.SemaphoreType.DMA, pltpu.SemaphoreType.DMA, pltpu.SemaphoreType.DMA]

    @functools.partial(pl.kernel, mesh=mesh, out_type=jax.ShapeDtypeStruct((n_out, width), rows.dtype),
                       scratch_types=[pltpu.VMEM((per_worker,), jnp.int32), pltpu.VMEM((per_worker,), jnp.int32)]
                       + buf() + buf())
    def scatter(rows_hbm, idx0_hbm, idx1_hbm, out_hbm, idx0_v, idx1_v, *scratch):
        base = (lax.axis_index("s") * sc.num_cores + lax.axis_index("c")) * per_worker
        bufs = (scratch[:4], scratch[4:])
        pltpu.sync_copy(idx0_hbm.at[pl.ds(base, per_worker)], idx0_v)
        pltpu.sync_copy(idx1_hbm.at[pl.ds(base, per_worker)], idx1_v)

        def load(win, which):
            rows_v, lsem, _, _ = bufs[which]
            pltpu.async_copy(rows_hbm.at[pl.ds(base + win * SC_WINDOW, SC_WINDOW)], rows_v, lsem)

        def send(win, which):
            rows_v, lsem, sem0, sem1 = bufs[which]
            here = pl.ds(win * SC_WINDOW, SC_WINDOW)
            pltpu.make_async_copy(rows_hbm.at[pl.ds(0, SC_WINDOW)], rows_v, lsem).wait()
            first = pltpu.async_copy(rows_v, out_hbm.at[idx0_v.at[here]], sem0)
            second = pltpu.async_copy(rows_v, out_hbm.at[idx1_v.at[here]], sem1)
            first.wait()
            second.wait()

        load(0, 0)

        @pl.loop(0, n_win // 2 - 1)
        def _(pair):
            win = 2 * pair
            load(win + 1, 1)
            send(win, 0)
            load(win + 2, 0)
            send(win + 1, 1)

        load(n_win - 1, 1)
        send(n_win - 2, 0)
        send(n_win - 1, 1)

    return scatter(rows, idx0, idx1)


def _combine_rows_kernel(h_ref, w_ref, g_ref, b_ref, y0_ref, y1_ref, o_ref):
    w = w_ref[...]
    y0_lo, y0_hi = _unpack_bf16_pair(y0_ref[...])
    y1_lo, y1_hi = _unpack_bf16_pair(y1_ref[...])
    y = jnp.concatenate([w[:, 0:1] * y0_lo + w[:, 1:2] * y1_lo, w[:, 0:1] * y0_hi + w[:, 1:2] * y1_hi], axis=1)
    o_ref[...] = _ln(DEEPNORM_ALPHA * h_ref[...] + y, g_ref[...], b_ref[...])


def _combine_rows_call(h1, rw, g, b, yg):
    t = h1.shape[0]
    tm = COMB_TM
    n = t // tm
    return pl.pallas_call(
        _combine_rows_kernel,
        grid=(n,),
        in_specs=[pl.BlockSpec((tm, D_MODEL), lambda i: (i, 0)),
                  pl.BlockSpec((tm, 8), lambda i: (i, 0)),
                  pl.BlockSpec((1, D_MODEL), lambda i: (0, 0)),
                  pl.BlockSpec((1, D_MODEL), lambda i: (0, 0)),
                  pl.BlockSpec((tm, HALF_D), lambda i: (i, 0)),
                  pl.BlockSpec((tm, HALF_D), lambda i: (n + i, 0))],
        out_specs=pl.BlockSpec((tm, D_MODEL), lambda i: (i, 0)),
        out_shape=jax.ShapeDtypeStruct((t, D_MODEL), F32),
        compiler_params=pltpu.CompilerParams(dimension_semantics=("arbitrary",),
                                             vmem_limit_bytes=VMEM_LIMIT),
        name="combine_rows",
    )(h1, rw, g, b, yg, yg)


def _take_cols(w, idx):
    wz = jnp.concatenate([w, jnp.zeros((w.shape[0], 1), w.dtype)], axis=1)
    return wz[:, np.where(idx < 0, w.shape[1], idx)]


def _layout_indices():
    lane_src = np.full((HEAD_PAD,), -1, np.int64)
    lane_src[0:ROPE_HALF] = QK_NOPE + np.arange(ROPE_HALF)
    lane_src[ROPE_HALF:X2_LANE] = np.arange(X2_LANE - ROPE_HALF)
    lane_src[X2_LANE:X2_LANE + ROPE_HALF] = QK_NOPE + ROPE_HALF + np.arange(ROPE_HALF)
    n_lo = X2_LANE - ROPE_HALF
    lane_src[X2_LANE + ROPE_HALF:X2_LANE + ROPE_HALF + QK_NOPE - n_lo] = n_lo + np.arange(QK_NOPE - n_lo)
    q_idx = np.concatenate([np.where(lane_src >= 0, lane_src + (QK_NOPE + QK_ROPE) * h, -1)
                            for h in range(N_HEADS)])
    nope_src = np.where(lane_src < QK_NOPE, lane_src, -1)
    k_idx = np.concatenate([np.where(nope_src >= 0, nope_src + (QK_NOPE + V_DIM) * h, -1)
                            for h in range(N_HEADS)])
    v_idx = np.concatenate([QK_NOPE + np.arange(V_DIM) + (QK_NOPE + V_DIM) * h for h in range(N_HEADS)])
    kr_idx = np.full((LANES,), -1, np.int64)
    kr_idx[0:ROPE_HALF] = np.arange(ROPE_HALF)
    kr_idx[X2_LANE:X2_LANE + ROPE_HALF] = ROPE_HALF + np.arange(ROPE_HALF)
    return q_idx, k_idx, v_idx, kr_idx


def _rope_lane_tables(length):
    inv_freq = np.float32(ROPE_THETA) ** (-np.arange(ROPE_HALF, dtype=np.float32) / np.float32(ROPE_HALF))
    ang = np.arange(length, dtype=np.float32)[:, None] * inv_freq[None, :].astype(np.float32)
    cos, sin = np.cos(ang).astype(np.float32), np.sin(ang).astype(np.float32)
    ctab = np.ones((length, LANES), np.float32)
    stab = np.zeros((length, LANES), np.float32)
    for lo, sign in ((0, -1.0), (X2_LANE, 1.0)):
        ctab[:, lo:lo + ROPE_HALF] = cos
        stab[:, lo:lo + ROPE_HALF] = sign * sin
    return jnp.asarray(ctab), jnp.asarray(stab)


def kernel(x, meta_tokens, ln_in_g, ln_in_b, w_in, conv_w, conv_b, conv_ln_g, conv_ln_b, q_norm_g, w_uq,
           kv_norm_g, w_ukv, conv_out_g, attn_out_g, w_o, ln1_g, ln1_b, w_rg, b_rg, w_re, b_re,
           w_gate, w_up, w_down, ln2_g, ln2_b):
    assert w_in.shape[0] == 1, "single layer"
    bsz, seq, d = x.shape
    t = bsz * seq
    row = lambda a: a.reshape(1, -1)

    q_idx, k_idx, v_idx, kr_idx = _layout_indices()
    c4 = 2 * C_CONV + Q_LORA + KV_LORA
    w_in_l = jnp.concatenate([w_in[0][:, :c4], _take_cols(w_in[0][:, c4:], kr_idx)], axis=1).astype(BF16)
    wuq_l = _take_cols(w_uq[0], q_idx).astype(BF16)
    wuk_l = _take_cols(w_ukv[0], k_idx).astype(BF16)
    wvt_l = w_ukv[0][:, v_idx].T.astype(BF16)
    wo_l = w_o[0].astype(BF16)
    n_route_pad = ROUTE_ROWS - N_EXPERTS - N_GROUPS
    wr_t = jnp.concatenate([w_re[0].T, w_rg[0].T, jnp.zeros((n_route_pad, d), F32)], axis=0)
    wrh_l = wr_t.astype(BF16)
    wrl_l = (wr_t - wrh_l.astype(F32)).astype(BF16)
    br_l = jnp.broadcast_to(jnp.concatenate([b_re[0], b_rg[0], jnp.zeros((n_route_pad,), F32)])[:, None],
                            (ROUTE_ROWS, LANES))
    ctab, stab = _rope_lane_tables(N_META + seq)

    proj_w = (row(ln_in_g), row(ln_in_b), w_in_l, row(q_norm_g[0]), wuq_l, row(kv_norm_g[0]), wuk_l, wvt_l)
    u_m, _, k_m, vt_m = _proj_call(meta_tokens.astype(F32), N_META, 1, *proj_w, ctab[:N_META], stab[:N_META])
    u, q, k, vt = _proj_call(x.reshape(t, d), PROJ_TM, seq // PROJ_TM, *proj_w, ctab[N_META:], stab[N_META:])

    o = _attn_call(q.reshape(bsz, seq, -1), k.reshape(bsz, seq, -1),
                   vt.reshape(bsz, seq // ATT_T, D_ATTN, ATT_T), k_m, vt_m)

    pre = jnp.concatenate([jnp.zeros((HALO - N_META, C_CONV), F32), u_m], axis=0)
    cw = jnp.concatenate([conv_w[0], jnp.zeros((HALO - CONV_K, C_CONV), F32)], axis=0)
    h1, hp, ri, rw, cnt = _mix_call(x, row(ln_in_g), row(ln_in_b), u.reshape(bsz, seq, C_CONV), pre, cw,
                                    row(conv_b[0]), row(conv_ln_g[0]), row(conv_ln_b[0]), row(conv_out_g[0]),
                                    o, row(attn_out_g[0]), wo_l, row(ln1_g[0]), row(ln1_b[0]), wrh_l, wrl_l, br_l)

    counts = cnt[:N_EXPERTS, 0].astype(jnp.int32)
    padded = (counts + MOE_BM - 1) // MOE_BM * MOE_BM
    pad_end = jnp.cumsum(padded)
    pad_start = pad_end - padded
    n_blocks = (2 * t) // MOE_BM + N_EXPERTS
    dest, blk = _tables_call(pad_start, pad_end, counts, ri, n_blocks)
    blk_e, blk_valid = blk[0, :n_blocks], blk[1, :n_blocks]

    h1f = h1.reshape(t, d)
    dest0, dest1 = dest[0], dest[1]
    xb = _sc_scatter_rows(hp.reshape(t, HALF_D), dest0, dest1, n_blocks * MOE_BM)
    yb = _expert_call(blk_e, blk_valid, xb, w_gate, w_up, w_down)
    yg = _sc_gather_rows(yb, jnp.concatenate([dest0, dest1]))
    out = _combine_rows_call(h1f, rw.reshape(t, 8), row(ln2_g[0]), row(ln2_b[0]), yg)
    return out.reshape(bsz, seq, d)
```

```python
import functools
import math

import numpy as np
import jax
import jax.numpy as jnp
from jax import lax
from jax.experimental import pallas as pl
from jax.experimental.pallas import tpu as pltpu
from jax.experimental.pallas import tpu_sc as plsc

F32 = jnp.float32
BF16 = jnp.bfloat16

D_MODEL = 1024
HALF_D = D_MODEL // 2
N_META = 16
CHUNK = 64
C_CONV = 512
CONV_K = 31
N_HEADS = 8
QK_NOPE = 64
QK_ROPE = 32
V_DIM = 64
Q_LORA = 384
KV_LORA = 256
ROPE_THETA = 10000.0
D_ATTN = N_HEADS * V_DIM
N_GROUPS = 4
EXPERTS_PER_GROUP = 8
N_EXPERTS = N_GROUPS * EXPERTS_PER_GROUP
D_EXPERT = 256
DEEPNORM_ALPHA = 2.0 ** 0.25
EPS = 1e-5
NEG_INF = -1e30

LANES = 128
SUBLANES = 8
HEAD_PAD = LANES
ROPE_HALF = QK_ROPE // 2
X2_LANE = 64
Z_COLS = 2 * C_CONV + Q_LORA + KV_LORA + LANES
Q_SCALE = (QK_NOPE + QK_ROPE) ** -0.5 * math.log2(math.e)

PROJ_TM = 1024
ATT_T = 256
ATT_HEADS = 8
MIX_TM = 512
ROUTE_ROWS = 48
HALO = 32
MOE_BM = 1024
COMB_TM = 1024
SC_WINDOW = 64
VMEM_LIMIT = 56 * 1024 * 1024


def _ln(x, g, b):
    mu = jnp.mean(x, axis=-1, keepdims=True)
    xc = x - mu
    var = jnp.mean(xc * xc, axis=-1, keepdims=True)
    return xc * lax.rsqrt(var + EPS) * g + b


def _rms(x, g):
    return x * lax.rsqrt(jnp.mean(x * x, axis=-1, keepdims=True) + EPS) * g


def _sigmoid(x):
    return 1.0 / (1.0 + jnp.exp(-x))


def _pack_bf16_pair(lo, hi):
    lo_bits = lax.bitcast_convert_type(lo.astype(BF16).astype(F32), jnp.uint32)
    hi_bits = lax.bitcast_convert_type(hi.astype(BF16).astype(F32), jnp.uint32)
    return (lo_bits >> 16) | (hi_bits & jnp.uint32(0xFFFF0000))


def _unpack_bf16_pair(packed):
    lo = lax.bitcast_convert_type(packed << 16, F32)
    hi = lax.bitcast_convert_type(packed & jnp.uint32(0xFFFF0000), F32)
    return lo, hi


def _proj_kernel(x_ref, g_ref, b_ref, win_ref, qg_ref, wuq_ref, kvg_ref, wuk_ref, wvt_ref, c_ref, s_ref,
                 u_ref, q_ref, k_ref, vt_ref):
    n_part, part = vt_ref.shape[1], vt_ref.shape[3]
    zs = []
    for r in range(n_part):
        h = _ln(x_ref[r * part:(r + 1) * part, :], g_ref[...], b_ref[...])
        zs.append(jnp.dot(h.astype(BF16), win_ref[...], preferred_element_type=F32))
    c0 = 2 * C_CONV
    for r, z in enumerate(zs):
        rows = slice(r * part, (r + 1) * part)
        u_ref[rows, :] = z[:, :C_CONV] * _sigmoid(z[:, C_CONV:c0])
        cq = z[:, c0:c0 + Q_LORA]
        ckv = z[:, c0 + Q_LORA:c0 + Q_LORA + KV_LORA]
        kr = z[:, c0 + Q_LORA + KV_LORA:]
        cs = c_ref[rows, :]
        sn = s_ref[rows, :]
        q = jnp.dot(_rms(cq, qg_ref[...]).astype(BF16), wuq_ref[...], preferred_element_type=F32)
        kvn = _rms(ckv, kvg_ref[...]).astype(BF16)
        kn = jnp.dot(kvn, wuk_ref[...], preferred_element_type=F32)
        kr_rot = kr * cs + pltpu.roll(kr, X2_LANE, 1) * sn
        for hh in range(N_HEADS):
            sl = slice(HEAD_PAD * hh, HEAD_PAD * (hh + 1))
            qh = q[:, sl]
            qh = (qh * cs + pltpu.roll(qh, X2_LANE, 1) * sn) * Q_SCALE
            q_ref[rows, sl] = qh.astype(BF16)
            k_ref[rows, sl] = (kn[:, sl] + kr_rot).astype(BF16)
        vt = lax.dot_general(wvt_ref[...], kvn, (((1,), (1,)), ((), ())), preferred_element_type=F32)
        vt_ref[0, r] = vt.astype(BF16)


def _proj_call(x2d, tm, pos_blocks, ln_g, ln_b, w_in, qg, wuq, kvg, wuk, wvt, ctab, stab):
    t = x2d.shape[0]
    sub_w = min(tm, ATT_T)
    n_sub = tm // sub_w
    full = lambda shape: pl.BlockSpec(shape, lambda i: (0,) * len(shape))
    row = lambda w: pl.BlockSpec((tm, w), lambda i: (i, 0))
    tab = pl.BlockSpec((tm, LANES), lambda i: (i % pos_blocks, 0))
    return pl.pallas_call(
        _proj_kernel,
        grid=(t // tm,),
        in_specs=[row(D_MODEL), full((1, D_MODEL)), full((1, D_MODEL)), full((D_MODEL, Z_COLS)),
                  full((1, Q_LORA)), full((Q_LORA, N_HEADS * HEAD_PAD)),
                  full((1, KV_LORA)), full((KV_LORA, N_HEADS * HEAD_PAD)), full((D_ATTN, KV_LORA)), tab, tab],
        out_specs=[row(C_CONV), row(N_HEADS * HEAD_PAD), row(N_HEADS * HEAD_PAD),
                   pl.BlockSpec((1, n_sub, D_ATTN, sub_w), lambda i: (i, 0, 0, 0))],
        out_shape=[jax.ShapeDtypeStruct((t, C_CONV), F32),
                   jax.ShapeDtypeStruct((t, N_HEADS * HEAD_PAD), BF16),
                   jax.ShapeDtypeStruct((t, N_HEADS * HEAD_PAD), BF16),
                   jax.ShapeDtypeStruct((t // tm, n_sub, D_ATTN, sub_w), BF16)],
        compiler_params=pltpu.CompilerParams(dimension_semantics=("arbitrary",),
                                             vmem_limit_bytes=VMEM_LIMIT),
        name="proj",
    )(x2d, ln_g, ln_b, w_in, qg, wuq, kvg, wuk, wvt, ctab, stab)


def _attn_kernel(q_ref, k_ref, vt_ref, km_ref, vtm_ref, o_ref, s_sc):
    seq = q_ref.shape[1]
    nq = seq // ATT_T
    key_chunk = lax.broadcasted_iota(jnp.int32, (ATT_T, ATT_T), 0) // CHUNK
    qry_chunk = lax.broadcasted_iota(jnp.int32, (ATT_T, ATT_T), 1) // CHUNK
    diag_visible = key_chunk <= qry_chunk
    nt = (((1,), (1,)), ((), ()))
    heads = range(ATT_HEADS)
    hsl = [slice(HEAD_PAD * hh, HEAD_PAD * (hh + 1)) for hh in heads]
    vsl = [slice(V_DIM * hh, V_DIM * (hh + 1)) for hh in heads]

    def update(s, vt, m, l, acc):
        m_new = jnp.maximum(m, jnp.max(s, axis=0, keepdims=True))
        alpha = jnp.exp2(m - m_new)
        p = jnp.exp2(s - m_new)
        vt1 = jnp.concatenate([vt, jnp.ones((2 * SUBLANES, vt.shape[1]), BF16)], axis=0)
        pv = jnp.dot(vt1, p.astype(BF16), preferred_element_type=F32)
        return m_new, alpha * l + pv[V_DIM:V_DIM + 1], alpha * acc + pv[:V_DIM]

    def scores(qi, j, hh):
        q0 = pl.multiple_of(qi * ATT_T, ATT_T)
        k0 = pl.multiple_of(j * ATT_T, ATT_T)
        return lax.dot_general(k_ref[0, pl.ds(k0, ATT_T), hsl[hh]], q_ref[0, pl.ds(q0, ATT_T), hsl[hh]], nt,
                               preferred_element_type=F32)

    for hh in heads:
        s_sc[0, hh] = scores(0, 0, hh)

    def q_body(qi, carry):
        row0 = pl.multiple_of(qi * ATT_T, ATT_T)
        slot = qi % 2
        state = []
        for hh in heads:
            state += [jnp.full((1, ATT_T), NEG_INF, F32), jnp.zeros((1, ATT_T), F32),
                      jnp.zeros((V_DIM, ATT_T), F32)]

        def kv_body(j, st):
            out = []
            s_next = scores(qi, j + 1, 0)
            for hh in heads:
                s_cur = s_sc[slot, hh]
                s_after = scores(qi, j + 1, hh + 1) if hh + 1 < ATT_HEADS else None
                out += update(s_cur, vt_ref[0, j, vsl[hh], :], *st[3 * hh:3 * hh + 3])
                s_sc[slot, hh] = s_next
                s_next = s_after
            return tuple(out)

        state = lax.fori_loop(0, qi, kv_body, tuple(state))
        q_next = jnp.minimum(qi + 1, nq - 1)
        sms = [lax.dot_general(km_ref[:, hsl[hh]], q_ref[0, pl.ds(row0, ATT_T), hsl[hh]], nt,
                               preferred_element_type=F32) for hh in heads]
        outs = []
        for hh in heads:
            m, l, acc = state[3 * hh:3 * hh + 3]
            s = jnp.where(diag_visible, s_sc[slot, hh], NEG_INF)
            sm = sms[hh]
            m_new = jnp.maximum(m, jnp.maximum(jnp.max(s, axis=0, keepdims=True),
                                               jnp.max(sm, axis=0, keepdims=True)))
            alpha = jnp.exp2(m - m_new)
            p = jnp.exp2(s - m_new)
            pm = jnp.exp2(sm - m_new)
            l = alpha * l + jnp.sum(p, axis=0, keepdims=True) + jnp.sum(pm, axis=0, keepdims=True)
            acc = (alpha * acc + jnp.dot(vt_ref[0, qi, vsl[hh], :], p.astype(BF16), preferred_element_type=F32)
                   + jnp.dot(vtm_ref[0, 0, vsl[hh], :], pm.astype(BF16), preferred_element_type=F32))
            outs.append(acc / l)
        for hh in heads:
            s_sc[1 - slot, hh] = scores(q_next, 0, hh)
        o_ref[0, pl.ds(row0, ATT_T), :] = jnp.concatenate(outs, axis=0).T
        return carry

    lax.fori_loop(0, nq, q_body, 0)


def _attn_call(q, k, vt, k_meta, vt_meta):
    b, seq, _ = q.shape
    nk = seq // ATT_T
    hb = ATT_HEADS
    return pl.pallas_call(
        _attn_kernel,
        grid=(b, N_HEADS // hb),
        in_specs=[pl.BlockSpec((1, seq, hb * HEAD_PAD), lambda i, j: (i, 0, j)),
                  pl.BlockSpec((1, seq, hb * HEAD_PAD), lambda i, j: (i, 0, j)),
                  pl.BlockSpec((1, nk, hb * V_DIM, ATT_T), lambda i, j: (i, 0, j, 0)),
                  pl.BlockSpec((N_META, hb * HEAD_PAD), lambda i, j: (0, j)),
                  pl.BlockSpec((1, 1, hb * V_DIM, N_META), lambda i, j: (0, 0, j, 0))],
        out_specs=pl.BlockSpec((1, seq, hb * V_DIM), lambda i, j: (i, 0, j)),
        out_shape=jax.ShapeDtypeStruct((b, seq, D_ATTN), F32),
        scratch_shapes=[pltpu.VMEM((2, hb, ATT_T, ATT_T), F32)],
        compiler_params=pltpu.CompilerParams(dimension_semantics=("arbitrary", "arbitrary"),
                                             vmem_limit_bytes=VMEM_LIMIT),
        name="attn",
    )(q, k, vt, k_meta, vt_meta)


def _mix_kernel(x_ref, ig_ref, ib_ref, u_ref, uh_ref, pre_ref, cw_ref, cb_ref, clg_ref, clb_ref,
                cog_ref, o_ref, aog_ref, wo_ref, l1g_ref, l1b_ref, wrh_ref, wrl_ref, br_ref, tri_ref,
                h1_ref, hp_ref, ri_ref, rw_ref, cnt_ref, win_sc, sh_sc, carry_sc):
    bi = pl.program_id(0)
    ti = pl.program_id(1)
    tm = MIX_TM

    @pl.when((bi == 0) & (ti == 0))
    def _():
        carry_sc[...] = jnp.zeros_like(carry_sc)

    win_sc[0:HALO, :] = jnp.where(ti == 0, pre_ref[...], uh_ref[0])
    win_sc[HALO:, :] = u_ref[0]
    first = HALO - (CONV_K - 1)
    for res in range(1, SUBLANES):
        rows = tm + HALO - SUBLANES
        sh_sc[res - 1, 0:rows, :] = win_sc[pl.ds(res, rows), :]

    acc = jnp.broadcast_to(cb_ref[...], (tm, C_CONV))
    for o in range(first, first + CONV_K):
        res, lo = o % SUBLANES, o - o % SUBLANES
        src = sh_sc[res - 1, lo:lo + tm, :] if res else win_sc[lo:lo + tm, :]
        acc = acc + src * cw_ref[o - first:o - first + 1, :]
    c = _ln(acc, clg_ref[...], clb_ref[...])
    c = c * _sigmoid(c)
    cn = _rms(c, cog_ref[...])
    on = _rms(o_ref[0], aog_ref[...])
    y = (jnp.dot(cn.astype(BF16), wo_ref[0:C_CONV, :], preferred_element_type=F32)
         + jnp.dot(on.astype(BF16), wo_ref[C_CONV:, :], preferred_element_type=F32))
    h = _ln(x_ref[0], ig_ref[...], ib_ref[...])
    h1 = _ln(DEEPNORM_ALPHA * h + y, l1g_ref[...], l1b_ref[...])
    h1_ref[0] = h1
    hp_ref[0] = _pack_bf16_pair(h1[:, :HALF_D], h1[:, HALF_D:])

    nt = (((1,), (1,)), ((), ()))
    h_hi = h1.astype(BF16)
    h_lo = (h1 - h_hi.astype(F32)).astype(BF16)
    lt = (lax.dot_general(wrh_ref[...], h_hi, nt, preferred_element_type=F32)
          + lax.dot_general(wrl_ref[...], h_hi, nt, preferred_element_type=F32)
          + lax.dot_general(wrh_ref[...], h_lo, nt, preferred_element_type=F32)) + br_ref[:, 0:1]
    row = lax.broadcasted_iota(jnp.int32, (ROUTE_ROWS, tm), 0)
    big = jnp.int32(1 << 20)
    is_g = (row >= N_EXPERTS) & (row < N_EXPERTS + N_GROUPS)
    gl = jnp.where(is_g, lt, -jnp.inf)
    gmax = jnp.max(gl, axis=0, keepdims=True)
    g_idx = jnp.min(jnp.where(gl == gmax, row - N_EXPERTS, big), axis=0, keepdims=True)
    g_w = 1.0 / jnp.sum(jnp.exp(gl - gmax), axis=0, keepdims=True)
    in_grp = (row < N_EXPERTS) & ((row // EXPERTS_PER_GROUP) == g_idx)
    el = jnp.where(in_grp, lt, -jnp.inf)
    v1 = jnp.max(el, axis=0, keepdims=True)
    i1 = jnp.min(jnp.where(el == v1, row, big), axis=0, keepdims=True)
    el2 = jnp.where(row == i1, -jnp.inf, el)
    v2 = jnp.max(el2, axis=0, keepdims=True)
    i2 = jnp.min(jnp.where(el2 == v2, row, big), axis=0, keepdims=True)
    e2 = jnp.exp(v2 - v1)
    w1 = g_w / (1.0 + e2)
    w2 = g_w * e2 / (1.0 + e2)

    oh1 = row == i1
    oh2 = row == i2
    oh = jnp.where(oh1 | oh2, 1.0, 0.0)
    before = jnp.dot(oh.astype(BF16), tri_ref[...], preferred_element_type=F32) + carry_sc[:, 0:1]
    r1 = jnp.sum(jnp.where(oh1, before, 0.0), axis=0, keepdims=True).astype(jnp.int32)
    r2 = jnp.sum(jnp.where(oh2, before, 0.0), axis=0, keepdims=True).astype(jnp.int32)
    carry_sc[...] = carry_sc[...] + jnp.sum(oh, axis=1, keepdims=True)
    cnt_ref[...] = carry_sc[...]

    sub = lax.broadcasted_iota(jnp.int32, (SUBLANES, tm), 0)
    ri_ref[...] = jnp.where(sub == 0, i1, jnp.where(sub == 1, i2, jnp.where(sub == 2, r1,
                            jnp.where(sub == 3, r2, 0))))
    lrow = lax.broadcasted_iota(jnp.int32, (LANES, tm), 0)
    rw_ref[0] = jnp.where(lrow == 0, w1, jnp.where(lrow == 1, w2, 0.0)).T[:, :8]


def _mix_call(x, ig, ib, u, pre, cw, cb, clg, clb, cog, o, aog, wo, l1g, l1b, wrh, wrl, br):
    b, seq, _ = x.shape
    tm = MIX_TM
    nt = seq // tm
    full = lambda shape: pl.BlockSpec(shape, lambda i, j: (0,) * len(shape))
    tile = lambda w: pl.BlockSpec((1, tm, w), lambda i, j: (i, j, 0))
    halo = pl.BlockSpec((1, HALO, C_CONV), lambda i, j: (i, jnp.maximum(j * (tm // HALO) - 1, 0), 0))
    earlier = jnp.asarray(np.triu(np.ones((tm, tm), np.float32), k=1), BF16)
    return pl.pallas_call(
        _mix_kernel,
        grid=(b, nt),
        in_specs=[tile(D_MODEL), full((1, D_MODEL)), full((1, D_MODEL)),
                  tile(C_CONV), halo, full((HALO, C_CONV)), full((HALO, C_CONV)), full((1, C_CONV)),
                  full((1, C_CONV)), full((1, C_CONV)), full((1, C_CONV)),
                  tile(D_ATTN), full((1, D_ATTN)), full((D_MODEL, D_MODEL)),
                  full((1, D_MODEL)), full((1, D_MODEL)), full((ROUTE_ROWS, D_MODEL)),
                  full((ROUTE_ROWS, D_MODEL)), full((ROUTE_ROWS, LANES)), full((tm, tm))],
        out_specs=[tile(D_MODEL), tile(HALF_D), pl.BlockSpec((8, tm), lambda i, j: (0, i * nt + j)), tile(8),
                   full((ROUTE_ROWS, LANES))],
        out_shape=[jax.ShapeDtypeStruct((b, seq, D_MODEL), F32),
                   jax.ShapeDtypeStruct((b, seq, HALF_D), jnp.uint32),
                   jax.ShapeDtypeStruct((8, b * seq), jnp.int32),
                   jax.ShapeDtypeStruct((b, seq, 8), F32),
                   jax.ShapeDtypeStruct((ROUTE_ROWS, LANES), F32)],
        scratch_shapes=[pltpu.VMEM((HALO + tm, C_CONV), F32),
                        pltpu.VMEM((SUBLANES - 1, HALO + tm, C_CONV), F32),
                        pltpu.VMEM((ROUTE_ROWS, LANES), F32)],
        compiler_params=pltpu.CompilerParams(dimension_semantics=("arbitrary", "arbitrary"),
                                             vmem_limit_bytes=VMEM_LIMIT),
        name="mix",
    )(x, ig, ib, u, u, pre, cw, cb, clg, clb, cog, o, aog, wo, l1g, l1b, wrh, wrl, br, earlier)


def _tables_kernel(ps_ref, pe_ref, cn_ref, ri_ref, dest_ref, blk_ref):
    ri = ri_ref[...]
    start = jnp.zeros_like(ri)
    for e in range(N_EXPERTS):
        start = jnp.where(ri == e, ps_ref[e], start)
    dest_ref[...] = jnp.zeros_like(dest_ref)
    dest_ref[0:2, :] = start[0:2, :] + ri_ref[2:4, :]
    blk_start = lax.broadcasted_iota(jnp.int32, blk_ref.shape, 1) * MOE_BM
    be = jnp.zeros(blk_ref.shape, jnp.int32)
    for e in range(N_EXPERTS):
        be = be + jnp.where(pe_ref[e] <= blk_start, 1, 0)
    be = jnp.minimum(be, N_EXPERTS - 1)
    end = jnp.zeros(blk_ref.shape, jnp.int32)
    for e in range(N_EXPERTS):
        end = jnp.where(be == e, ps_ref[e] + cn_ref[e], end)
    valid = jnp.clip(end - blk_start, 0, MOE_BM)
    row = lax.broadcasted_iota(jnp.int32, blk_ref.shape, 0)
    blk_ref[...] = jnp.where(row == 0, be, valid)


def _tables_call(pad_start, pad_end, counts, ri, n_blocks):
    t = ri.shape[1]
    blk_w = -(-n_blocks // LANES) * LANES
    return pl.pallas_call(
        _tables_kernel,
        grid_spec=pltpu.PrefetchScalarGridSpec(
            num_scalar_prefetch=3,
            grid=(1,),
            in_specs=[pl.BlockSpec((SUBLANES, t), lambda i, *_: (0, 0))],
            out_specs=[pl.BlockSpec((SUBLANES, t), lambda i, *_: (0, 0)),
                       pl.BlockSpec((SUBLANES, blk_w), lambda i, *_: (0, 0))]),
        out_shape=[jax.ShapeDtypeStruct((SUBLANES, t), jnp.int32),
                   jax.ShapeDtypeStruct((SUBLANES, blk_w), jnp.int32)],
        compiler_params=pltpu.CompilerParams(dimension_semantics=("arbitrary",),
                                             vmem_limit_bytes=VMEM_LIMIT),
        name="tables",
    )(pad_start, pad_end, counts, ri)


def _expert_kernel(be_ref, nv_ref, x_ref, wg_ref, wu_ref, wd_ref, y_ref, wg_sc, wu_sc, wd_sc):
    i = pl.program_id(0)
    nv = nv_ref[i]

    @pl.when((i == 0) | (be_ref[i] != be_ref[jnp.maximum(i - 1, 0)]))
    def _():
        wg_sc[...] = wg_ref[0, 0].astype(BF16)
        wu_sc[...] = wu_ref[0, 0].astype(BF16)
        wd_sc[...] = wd_ref[0, 0].astype(BF16)

    @pl.when(nv > 0)
    def _():
        rowid = lax.broadcasted_iota(jnp.int32, x_ref.shape, 0)
        x_lo, x_hi = _unpack_bf16_pair(jnp.where(rowid < nv, x_ref[...], jnp.uint32(0)))
        x_lo = x_lo.astype(BF16)
        x_hi = x_hi.astype(BF16)
        g = (jnp.dot(x_lo, wg_sc[:HALF_D, :], preferred_element_type=F32)
             + jnp.dot(x_hi, wg_sc[HALF_D:, :], preferred_element_type=F32))
        up = (jnp.dot(x_lo, wu_sc[:HALF_D, :], preferred_element_type=F32)
              + jnp.dot(x_hi, wu_sc[HALF_D:, :], preferred_element_type=F32))
        act = g * _sigmoid(g) * up
        y = jnp.dot(act.astype(BF16), wd_sc[...], preferred_element_type=F32)
        y_ref[...] = _pack_bf16_pair(y[:, :HALF_D], y[:, HALF_D:])

    @pl.when(nv == 0)
    def _():
        y_ref[...] = jnp.zeros_like(y_ref)


def _expert_call(blk_e, blk_valid, xb, w_gate, w_up, w_down):
    r = xb.shape[0]
    bm = MOE_BM
    wspec = lambda a, b: pl.BlockSpec((1, 1, a, b), lambda i, be, nv: (0, be[i], 0, 0))
    return pl.pallas_call(
        _expert_kernel,
        grid_spec=pltpu.PrefetchScalarGridSpec(
            num_scalar_prefetch=2,
            grid=(r // bm,),
            in_specs=[pl.BlockSpec((bm, HALF_D), lambda i, be, nv: (i, 0)),
                      wspec(D_MODEL, D_EXPERT), wspec(D_MODEL, D_EXPERT), wspec(D_EXPERT, D_MODEL)],
            out_specs=pl.BlockSpec((bm, HALF_D), lambda i, be, nv: (i, 0)),
            scratch_shapes=[pltpu.VMEM((D_MODEL, D_EXPERT), BF16), pltpu.VMEM((D_MODEL, D_EXPERT), BF16),
                            pltpu.VMEM((D_EXPERT, D_MODEL), BF16)]),
        out_shape=jax.ShapeDtypeStruct((r, HALF_D), jnp.uint32),
        compiler_params=pltpu.CompilerParams(dimension_semantics=("arbitrary",),
                                             vmem_limit_bytes=VMEM_LIMIT),
        name="experts",
    )(blk_e, blk_valid, xb, w_gate, w_up, w_down)


def _sc_gather_rows(table, idx):
    m = idx.shape[0]
    width = table.shape[1]
    sc = plsc.get_sparse_core_info()
    workers = sc.num_cores * sc.num_subcores
    per_worker = m // workers
    mesh = plsc.VectorSubcoreMesh(core_axis_name="c", subcore_axis_name="s")

    n_win = per_worker // SC_WINDOW
    assert n_win % 2 == 0 and n_win >= 2
    buf = lambda: [pltpu.VMEM((SC_WINDOW, width), table.dtype), pltpu.SemaphoreType.DMA]

    @functools.partial(pl.kernel, mesh=mesh, out_type=jax.ShapeDtypeStruct((m, width), table.dtype),
                       scratch_types=[pltpu.VMEM((per_worker,), jnp.int32)] + buf() + buf())
    def gather(table_hbm, idx_hbm, out_hbm, idx_v, rows_a, sem_a, rows_b, sem_b):
        base = (lax.axis_index("s") * sc.num_cores + lax.axis_index("c")) * per_worker
        bufs = ((rows_a, sem_a), (rows_b, sem_b))
        pltpu.sync_copy(idx_hbm.at[pl.ds(base, per_worker)], idx_v)

        def window_copy(win, which):
            rows_v, sem = bufs[which]
            return pltpu.make_async_copy(table_hbm.at[idx_v.at[pl.ds(win * SC_WINDOW, SC_WINDOW)]], rows_v, sem)

        def fetch(win, which):
            window_copy(win, which).start()

        def flush(win, which):
            window_copy(win, which).wait()
            pltpu.sync_copy(bufs[which][0], out_hbm.at[pl.ds(base + win * SC_WINDOW, SC_WINDOW)])

        fetch(0, 0)

        @pl.loop(0, n_win // 2 - 1)
        def _(pair):
            win = 2 * pair
            fetch(win + 1, 1)
            flush(win, 0)
            fetch(win + 2, 0)
            flush(win + 1, 1)

        fetch(n_win - 1, 1)
        flush(n_win - 2, 0)
        flush(n_win - 1, 1)

    return gather(table, idx)


def _sc_scatter_rows(rows, idx0, idx1, n_out):
    m, width = rows.shape
    sc = plsc.get_sparse_core_info()
    workers = sc.num_cores * sc.num_subcores
    per_worker = m // workers
    mesh = plsc.VectorSubcoreMesh(core_axis_name="c", subcore_axis_name="s")

    n_win = per_worker // SC_WINDOW
    assert n_win % 2 == 0 and n_win >= 2
    buf = lambda: [pltpu.VMEM((SC_WINDOW, width), rows.dtype),
                   pltpu.SemaphoreType.DMA, pltpu.SemaphoreType.DMA, pltpu.SemaphoreType.DMA]

    @functools.partial(pl.kernel, mesh=mesh, out_type=jax.ShapeDtypeStruct((n_out, width), rows.dtype),
                       scratch_types=[pltpu.VMEM((per_worker,), jnp.int32), pltpu.VMEM((per_worker,), jnp.int32)]
                       + buf() + buf())
    def scatter(rows_hbm, idx0_hbm, idx1_hbm, out_hbm, idx0_v, idx1_v, *scratch):
        base = (lax.axis_index("s") * sc.num_cores + lax.axis_index("c")) * per_worker
        bufs = (scratch[:4], scratch[4:])
        pltpu.sync_copy(idx0_hbm.at[pl.ds(base, per_worker)], idx0_v)
        pltpu.sync_copy(idx1_hbm.at[pl.ds(base, per_worker)], idx1_v)

        def load(win, which):
            rows_v, lsem, _, _ = bufs[which]
            pltpu.async_copy(rows_hbm.at[pl.ds(base + win * SC_WINDOW, SC_WINDOW)], rows_v, lsem)

        def send(win, which):
            rows_v, lsem, sem0, sem1 = bufs[which]
            here = pl.ds(win * SC_WINDOW, SC_WINDOW)
            pltpu.make_async_copy(rows_hbm.at[pl.ds(0, SC_WINDOW)], rows_v, lsem).wait()
            first = pltpu.async_copy(rows_v, out_hbm.at[idx0_v.at[here]], sem0)
            second = pltpu.async_copy(rows_v, out_hbm.at[idx1_v.at[here]], sem1)
            first.wait()
            second.wait()

        load(0, 0)

        @pl.loop(0, n_win // 2 - 1)
        def _(pair):
            win = 2 * pair
            load(win + 1, 1)
            send(win, 0)
            load(win + 2, 0)
            send(win + 1, 1)

        load(n_win - 1, 1)
        send(n_win - 2, 0)
        send(n_win - 1, 1)

    return scatter(rows, idx0, idx1)


def _combine_rows_kernel(h_ref, w_ref, g_ref, b_ref, y0_ref, y1_ref, o_ref):
    w = w_ref[...]
    y0_lo, y0_hi = _unpack_bf16_pair(y0_ref[...])
    y1_lo, y1_hi = _unpack_bf16_pair(y1_ref[...])
    y = jnp.concatenate([w[:, 0:1] * y0_lo + w[:, 1:2] * y1_lo, w[:, 0:1] * y0_hi + w[:, 1:2] * y1_hi], axis=1)
    o_ref[...] = _ln(DEEPNORM_ALPHA * h_ref[...] + y, g_ref[...], b_ref[...])


def _combine_rows_call(h1, rw, g, b, yg):
    t = h1.shape[0]
    tm = COMB_TM
    n = t // tm
    return pl.pallas_call(
        _combine_rows_kernel,
        grid=(n,),
        in_specs=[pl.BlockSpec((tm, D_MODEL), lambda i: (i, 0)),
                  pl.BlockSpec((tm, 8), lambda i: (i, 0)),
                  pl.BlockSpec((1, D_MODEL), lambda i: (0, 0)),
                  pl.BlockSpec((1, D_MODEL), lambda i: (0, 0)),
                  pl.BlockSpec((tm, HALF_D), lambda i: (i, 0)),
                  pl.BlockSpec((tm, HALF_D), lambda i: (n + i, 0))],
        out_specs=pl.BlockSpec((tm, D_MODEL), lambda i: (i, 0)),
        out_shape=jax.ShapeDtypeStruct((t, D_MODEL), F32),
        compiler_params=pltpu.CompilerParams(dimension_semantics=("arbitrary",),
                                             vmem_limit_bytes=VMEM_LIMIT),
        name="combine_rows",
    )(h1, rw, g, b, yg, yg)


def _take_cols(w, idx):
    wz = jnp.concatenate([w, jnp.zeros((w.shape[0], 1), w.dtype)], axis=1)
    return wz[:, np.where(idx < 0, w.shape[1], idx)]


def _layout_indices():
    lane_src = np.full((HEAD_PAD,), -1, np.int64)
    lane_src[0:ROPE_HALF] = QK_NOPE + np.arange(ROPE_HALF)
    lane_src[ROPE_HALF:X2_LANE] = np.arange(X2_LANE - ROPE_HALF)
    lane_src[X2_LANE:X2_LANE + ROPE_HALF] = QK_NOPE + ROPE_HALF + np.arange(ROPE_HALF)
    n_lo = X2_LANE - ROPE_HALF
    lane_src[X2_LANE + ROPE_HALF:X2_LANE + ROPE_HALF + QK_NOPE - n_lo] = n_lo + np.arange(QK_NOPE - n_lo)
    q_idx = np.concatenate([np.where(lane_src >= 0, lane_src + (QK_NOPE + QK_ROPE) * h, -1)
                            for h in range(N_HEADS)])
    nope_src = np.where(lane_src < QK_NOPE, lane_src, -1)
    k_idx = np.concatenate([np.where(nope_src >= 0, nope_src + (QK_NOPE + V_DIM) * h, -1)
                            for h in range(N_HEADS)])
    v_idx = np.concatenate([QK_NOPE + np.arange(V_DIM) + (QK_NOPE + V_DIM) * h for h in range(N_HEADS)])
    kr_idx = np.full((LANES,), -1, np.int64)
    kr_idx[0:ROPE_HALF] = np.arange(ROPE_HALF)
    kr_idx[X2_LANE:X2_LANE + ROPE_HALF] = ROPE_HALF + np.arange(ROPE_HALF)
    return q_idx, k_idx, v_idx, kr_idx


def _rope_lane_tables(length):
    inv_freq = np.float32(ROPE_THETA) ** (-np.arange(ROPE_HALF, dtype=np.float32) / np.float32(ROPE_HALF))
    ang = np.arange(length, dtype=np.float32)[:, None] * inv_freq[None, :].astype(np.float32)
    cos, sin = np.cos(ang).astype(np.float32), np.sin(ang).astype(np.float32)
    ctab = np.ones((length, LANES), np.float32)
    stab = np.zeros((length, LANES), np.float32)
    for lo, sign in ((0, -1.0), (X2_LANE, 1.0)):
        ctab[:, lo:lo + ROPE_HALF] = cos
        stab[:, lo:lo + ROPE_HALF] = sign * sin
    return jnp.asarray(ctab), jnp.asarray(stab)


def kernel(x, meta_tokens, ln_in_g, ln_in_b, w_in, conv_w, conv_b, conv_ln_g, conv_ln_b, q_norm_g, w_uq,
           kv_norm_g, w_ukv, conv_out_g, attn_out_g, w_o, ln1_g, ln1_b, w_rg, b_rg, w_re, b_re,
           w_gate, w_up, w_down, ln2_g, ln2_b):
    assert w_in.shape[0] == 1, "single layer"
    bsz, seq, d = x.shape
    t = bsz * seq
    row = lambda a: a.reshape(1, -1)

    q_idx, k_idx, v_idx, kr_idx = _layout_indices()
    c4 = 2 * C_CONV + Q_LORA + KV_LORA
    w_in_l = jnp.concatenate([w_in[0][:, :c4], _take_cols(w_in[0][:, c4:], kr_idx)], axis=1).astype(BF16)
    wuq_l = _take_cols(w_uq[0], q_idx).astype(BF16)
    wuk_l = _take_cols(w_ukv[0], k_idx).astype(BF16)
    wvt_l = w_ukv[0][:, v_idx].T.astype(BF16)
    wo_l = w_o[0].astype(BF16)
    n_route_pad = ROUTE_ROWS - N_EXPERTS - N_GROUPS
    wr_t = jnp.concatenate([w_re[0].T, w_rg[0].T, jnp.zeros((n_route_pad, d), F32)], axis=0)
    wrh_l = wr_t.astype(BF16)
    wrl_l = (wr_t - wrh_l.astype(F32)).astype(BF16)
    br_l = jnp.broadcast_to(jnp.concatenate([b_re[0], b_rg[0], jnp.zeros((n_route_pad,), F32)])[:, None],
                            (ROUTE_ROWS, LANES))
    ctab, stab = _rope_lane_tables(N_META + seq)

    proj_w = (row(ln_in_g), row(ln_in_b), w_in_l, row(q_norm_g[0]), wuq_l, row(kv_norm_g[0]), wuk_l, wvt_l)
    u_m, _, k_m, vt_m = _proj_call(meta_tokens.astype(F32), N_META, 1, *proj_w, ctab[:N_META], stab[:N_META])
    u, q, k, vt = _proj_call(x.reshape(t, d), PROJ_TM, seq // PROJ_TM, *proj_w, ctab[N_META:], stab[N_META:])

    o = _attn_call(q.reshape(bsz, seq, -1), k.reshape(bsz, seq, -1),
                   vt.reshape(bsz, seq // ATT_T, D_ATTN, ATT_T), k_m, vt_m)

    pre = jnp.concatenate([jnp.zeros((HALO - N_META, C_CONV), F32), u_m], axis=0)
    cw = jnp.concatenate([conv_w[0], jnp.zeros((HALO - CONV_K, C_CONV), F32)], axis=0)
    h1, hp, ri, rw, cnt = _mix_call(x, row(ln_in_g), row(ln_in_b), u.reshape(bsz, seq, C_CONV), pre, cw,
                                    row(conv_b[0]), row(conv_ln_g[0]), row(conv_ln_b[0]), row(conv_out_g[0]),
                                    o, row(attn_out_g[0]), wo_l, row(ln1_g[0]), row(ln1_b[0]), wrh_l, wrl_l, br_l)

    counts = cnt[:N_EXPERTS, 0].astype(jnp.int32)
    padded = (counts + MOE_BM - 1) // MOE_BM * MOE_BM
    pad_end = jnp.cumsum(padded)
    pad_start = pad_end - padded
    n_blocks = (2 * t) // MOE_BM + N_EXPERTS
    dest, blk = _tables_call(pad_start, pad_end, counts, ri, n_blocks)
    blk_e, blk_valid = blk[0, :n_blocks], blk[1, :n_blocks]

    h1f = h1.reshape(t, d)
    dest0, dest1 = dest[0], dest[1]
    xb = _sc_scatter_rows(hp.reshape(t, HALF_D), dest0, dest1, n_blocks * MOE_BM)
    yb = _expert_call(blk_e, blk_valid, xb, w_gate, w_up, w_down)
    yg = _sc_gather_rows(yb, jnp.concatenate([dest0, dest1]))
    out = _combine_rows_call(h1f, rw.reshape(t, 8), row(ln2_g[0]), row(ln2_b[0]), yg)
    return out.reshape(bsz, seq, d)
```

```python
import functools
import math

import numpy as np
import jax
import jax.numpy as jnp
from jax import lax
from jax.experimental import pallas as pl
from jax.experimental.pallas import tpu as pltpu
from jax.experimental.pallas import tpu_sc as plsc

F32 = jnp.float32
BF16 = jnp.bfloat16

D_MODEL = 1024
HALF_D = D_MODEL // 2
N_META = 16
CHUNK = 64
C_CONV = 512
CONV_K = 31
N_HEADS = 8
QK_NOPE = 64
QK_ROPE = 32
V_DIM = 64
Q_LORA = 384
KV_LORA = 256
ROPE_THETA = 10000.0
D_ATTN = N_HEADS * V_DIM
N_GROUPS = 4
EXPERTS_PER_GROUP = 8
N_EXPERTS = N_GROUPS * EXPERTS_PER_GROUP
D_EXPERT = 256
DEEPNORM_ALPHA = 2.0 ** 0.25
EPS = 1e-5
NEG_INF = -1e30

LANES = 128
SUBLANES = 8
HEAD_PAD = LANES
ROPE_HALF = QK_ROPE // 2
X2_LANE = 64
Z_COLS = 2 * C_CONV + Q_LORA + KV_LORA + LANES
Q_SCALE = (QK_NOPE + QK_ROPE) ** -0.5 * math.log2(math.e)

PROJ_TM = 1024
ATT_T = 256
ATT_HEADS = 8
MIX_TM = 512
ROUTE_ROWS = 48
HALO = 32
MOE_BM = 1024
COMB_TM = 1024
SC_WINDOW = 64
VMEM_LIMIT = 56 * 1024 * 1024


def _ln(x, g, b):
    mu = jnp.mean(x, axis=-1, keepdims=True)
    xc = x - mu
    var = jnp.mean(xc * xc, axis=-1, keepdims=True)
    return xc * lax.rsqrt(var + EPS) * g + b


def _rms(x, g):
    return x * lax.rsqrt(jnp.mean(x * x, axis=-1, keepdims=True) + EPS) * g


def _sigmoid(x):
    return 1.0 / (1.0 + jnp.exp(-x))


def _pack_bf16_pair(lo, hi):
    lo_bits = lax.bitcast_convert_type(lo.astype(BF16).astype(F32), jnp.uint32)
    hi_bits = lax.bitcast_convert_type(hi.astype(BF16).astype(F32), jnp.uint32)
    return (lo_bits >> 16) | (hi_bits & jnp.uint32(0xFFFF0000))


def _unpack_bf16_pair(packed):
    lo = lax.bitcast_convert_type(packed << 16, F32)
    hi = lax.bitcast_convert_type(packed & jnp.uint32(0xFFFF0000), F32)
    return lo, hi


def _proj_kernel(x_ref, g_ref, b_ref, win_ref, qg_ref, wuq_ref, kvg_ref, wuk_ref, wvt_ref, c_ref, s_ref,
                 u_ref, q_ref, k_ref, vt_ref):
    n_part, part = vt_ref.shape[1], vt_ref.shape[3]
    zs = []
    for r in range(n_part):
        h = _ln(x_ref[r * part:(r + 1) * part, :], g_ref[...], b_ref[...])
        zs.append(jnp.dot(h.astype(BF16), win_ref[...], preferred_element_type=F32))
    c0 = 2 * C_CONV
    for r, z in enumerate(zs):
        rows = slice(r * part, (r + 1) * part)
        u_ref[rows, :] = z[:, :C_CONV] * _sigmoid(z[:, C_CONV:c0])
        cq = z[:, c0:c0 + Q_LORA]
        ckv = z[:, c0 + Q_LORA:c0 + Q_LORA + KV_LORA]
        kr = z[:, c0 + Q_LORA + KV_LORA:]
        cs = c_ref[rows, :]
        sn = s_ref[rows, :]
        q = jnp.dot(_rms(cq, qg_ref[...]).astype(BF16), wuq_ref[...], preferred_element_type=F32)
        kvn = _rms(ckv, kvg_ref[...]).astype(BF16)
        kn = jnp.dot(kvn, wuk_ref[...], preferred_element_type=F32)
        kr_rot = kr * cs + pltpu.roll(kr, X2_LANE, 1) * sn
        for hh in range(N_HEADS):
            sl = slice(HEAD_PAD * hh, HEAD_PAD * (hh + 1))
            qh = q[:, sl]
            qh = (qh * cs + pltpu.roll(qh, X2_LANE, 1) * sn) * Q_SCALE
            q_ref[rows, sl] = qh.astype(BF16)
            k_ref[rows, sl] = (kn[:, sl] + kr_rot).astype(BF16)
        vt = lax.dot_general(wvt_ref[...], kvn, (((1,), (1,)), ((), ())), preferred_element_type=F32)
        vt_ref[0, r] = vt.astype(BF16)


def _proj_call(x2d, tm, pos_blocks, ln_g, ln_b, w_in, qg, wuq, kvg, wuk, wvt, ctab, stab):
    t = x2d.shape[0]
    sub_w = min(tm, ATT_T)
    n_sub = tm // sub_w
    full = lambda shape: pl.BlockSpec(shape, lambda i: (0,) * len(shape))
    row = lambda w: pl.BlockSpec((tm, w), lambda i: (i, 0))
    tab = pl.BlockSpec((tm, LANES), lambda i: (i % pos_blocks, 0))
    return pl.pallas_call(
        _proj_kernel,
        grid=(t // tm,),
        in_specs=[row(D_MODEL), full((1, D_MODEL)), full((1, D_MODEL)), full((D_MODEL, Z_COLS)),
                  full((1, Q_LORA)), full((Q_LORA, N_HEADS * HEAD_PAD)),
                  full((1, KV_LORA)), full((KV_LORA, N_HEADS * HEAD_PAD)), full((D_ATTN, KV_LORA)), tab, tab],
        out_specs=[row(C_CONV), row(N_HEADS * HEAD_PAD), row(N_HEADS * HEAD_PAD),
                   pl.BlockSpec((1, n_sub, D_ATTN, sub_w), lambda i: (i, 0, 0, 0))],
        out_shape=[jax.ShapeDtypeStruct((t, C_CONV), F32),
                   jax.ShapeDtypeStruct((t, N_HEADS * HEAD_PAD), BF16),
                   jax.ShapeDtypeStruct((t, N_HEADS * HEAD_PAD), BF16),
                   jax.ShapeDtypeStruct((t // tm, n_sub, D_ATTN, sub_w), BF16)],
        compiler_params=pltpu.CompilerParams(dimension_semantics=("arbitrary",),
                                             vmem_limit_bytes=VMEM_LIMIT),
        name="proj",
    )(x2d, ln_g, ln_b, w_in, qg, wuq, kvg, wuk, wvt, ctab, stab)


def _attn_kernel(q_ref, k_ref, vt_ref, km_ref, vtm_ref, g_ref, o_ref, s_sc):
    seq = q_ref.shape[1]
    nq = seq // ATT_T
    key_chunk = lax.broadcasted_iota(jnp.int32, (ATT_T, ATT_T), 0) // CHUNK
    qry_chunk = lax.broadcasted_iota(jnp.int32, (ATT_T, ATT_T), 1) // CHUNK
    diag_visible = key_chunk <= qry_chunk
    nt = (((1,), (1,)), ((), ()))
    heads = range(ATT_HEADS)
    hsl = [slice(HEAD_PAD * hh, HEAD_PAD * (hh + 1)) for hh in heads]
    vsl = [slice(V_DIM * hh, V_DIM * (hh + 1)) for hh in heads]

    def update(s, vt, m, l, acc):
        m_new = jnp.maximum(m, jnp.max(s, axis=0, keepdims=True))
        alpha = jnp.exp2(m - m_new)
        p = jnp.exp2(s - m_new)
        vt1 = jnp.concatenate([vt, jnp.ones((2 * SUBLANES, vt.shape[1]), BF16)], axis=0)
        pv = jnp.dot(vt1, p.astype(BF16), preferred_element_type=F32)
        return m_new, alpha * l + pv[V_DIM:V_DIM + 1], alpha * acc + pv[:V_DIM]

    def scores(qi, j, hh):
        q0 = pl.multiple_of(qi * ATT_T, ATT_T)
        k0 = pl.multiple_of(j * ATT_T, ATT_T)
        return lax.dot_general(k_ref[0, pl.ds(k0, ATT_T), hsl[hh]], q_ref[0, pl.ds(q0, ATT_T), hsl[hh]], nt,
                               preferred_element_type=F32)

    for hh in heads:
        s_sc[0, hh] = scores(0, 0, hh)

    def q_body(qi, carry):
        row0 = pl.multiple_of(qi * ATT_T, ATT_T)
        slot = qi % 2
        state = []
        for hh in heads:
            state += [jnp.full((1, ATT_T), NEG_INF, F32), jnp.zeros((1, ATT_T), F32),
                      jnp.zeros((V_DIM, ATT_T), F32)]

        def kv_body(j, st):
            out = []
            s_next = scores(qi, j + 1, 0)
            for hh in heads:
                s_cur = s_sc[slot, hh]
                s_after = scores(qi, j + 1, hh + 1) if hh + 1 < ATT_HEADS else None
                out += update(s_cur, vt_ref[0, j, vsl[hh], :], *st[3 * hh:3 * hh + 3])
                s_sc[slot, hh] = s_next
                s_next = s_after
            return tuple(out)

        state = lax.fori_loop(0, qi, kv_body, tuple(state))
        q_next = jnp.minimum(qi + 1, nq - 1)
        sms = [lax.dot_general(km_ref[:, hsl[hh]], q_ref[0, pl.ds(row0, ATT_T), hsl[hh]], nt,
                               preferred_element_type=F32) for hh in heads]
        outs = []
        for hh in heads:
            m, l, acc = state[3 * hh:3 * hh + 3]
            s = jnp.where(diag_visible, s_sc[slot, hh], NEG_INF)
            sm = sms[hh]
            m_new = jnp.maximum(m, jnp.maximum(jnp.max(s, axis=0, keepdims=True),
                                               jnp.max(sm, axis=0, keepdims=True)))
            alpha = jnp.exp2(m - m_new)
            p = jnp.exp2(s - m_new)
            pm = jnp.exp2(sm - m_new)
            l = alpha * l + jnp.sum(p, axis=0, keepdims=True) + jnp.sum(pm, axis=0, keepdims=True)
            acc = (alpha * acc + jnp.dot(vt_ref[0, qi, vsl[hh], :], p.astype(BF16), preferred_element_type=F32)
                   + jnp.dot(vtm_ref[0, 0, vsl[hh], :], pm.astype(BF16), preferred_element_type=F32))
            outs.append(acc / l)
        for hh in heads:
            s_sc[1 - slot, hh] = scores(q_next, 0, hh)
        cat = jnp.concatenate(outs, axis=0)
        cat = cat * lax.rsqrt(jnp.mean(cat * cat, axis=0, keepdims=True) + EPS)
        o_ref[0, pl.ds(row0, ATT_T), :] = (cat.T * g_ref[...]).astype(BF16)
        return carry

    lax.fori_loop(0, nq, q_body, 0)


def _attn_call(q, k, vt, k_meta, vt_meta, out_g):
    b, seq, _ = q.shape
    nk = seq // ATT_T
    hb = ATT_HEADS
    assert hb == N_HEADS
    return pl.pallas_call(
        _attn_kernel,
        grid=(b, N_HEADS // hb),
        in_specs=[pl.BlockSpec((1, seq, hb * HEAD_PAD), lambda i, j: (i, 0, j)),
                  pl.BlockSpec((1, seq, hb * HEAD_PAD), lambda i, j: (i, 0, j)),
                  pl.BlockSpec((1, nk, hb * V_DIM, ATT_T), lambda i, j: (i, 0, j, 0)),
                  pl.BlockSpec((N_META, hb * HEAD_PAD), lambda i, j: (0, j)),
                  pl.BlockSpec((1, 1, hb * V_DIM, N_META), lambda i, j: (0, 0, j, 0)),
                  pl.BlockSpec((1, D_ATTN), lambda i, j: (0, 0))],
        out_specs=pl.BlockSpec((1, seq, hb * V_DIM), lambda i, j: (i, 0, j)),
        out_shape=jax.ShapeDtypeStruct((b, seq, D_ATTN), BF16),
        scratch_shapes=[pltpu.VMEM((2, hb, ATT_T, ATT_T), F32)],
        compiler_params=pltpu.CompilerParams(dimension_semantics=("arbitrary", "arbitrary"),
                                             vmem_limit_bytes=VMEM_LIMIT),
        name="attn",
    )(q, k, vt, k_meta, vt_meta, out_g)


def _mix_kernel(x_ref, ig_ref, ib_ref, u_ref, uh_ref, pre_ref, cw_ref, cb_ref, clg_ref, clb_ref,
                cog_ref, o_ref, wo_ref, l1g_ref, l1b_ref, wrh_ref, wrl_ref, br_ref, tri_ref,
                h1_ref, hp_ref, ri_ref, rw_ref, cnt_ref, win_sc, sh_sc, carry_sc):
    bi = pl.program_id(0)
    ti = pl.program_id(1)
    tm = MIX_TM

    @pl.when((bi == 0) & (ti == 0))
    def _():
        carry_sc[...] = jnp.zeros_like(carry_sc)

    win_sc[0:HALO, :] = jnp.where(ti == 0, pre_ref[...], uh_ref[0])
    win_sc[HALO:, :] = u_ref[0]
    first = HALO - (CONV_K - 1)
    for res in range(1, SUBLANES):
        rows = tm + HALO - SUBLANES
        sh_sc[res - 1, 0:rows, :] = win_sc[pl.ds(res, rows), :]

    acc = jnp.broadcast_to(cb_ref[...], (tm, C_CONV))
    for o in range(first, first + CONV_K):
        res, lo = o % SUBLANES, o - o % SUBLANES
        src = sh_sc[res - 1, lo:lo + tm, :] if res else win_sc[lo:lo + tm, :]
        acc = acc + src * cw_ref[o - first:o - first + 1, :]
    c = _ln(acc, clg_ref[...], clb_ref[...])
    c = c * _sigmoid(c)
    cn = _rms(c, cog_ref[...])
    y = (jnp.dot(cn.astype(BF16), wo_ref[0:C_CONV, :], preferred_element_type=F32)
         + jnp.dot(o_ref[0], wo_ref[C_CONV:, :], preferred_element_type=F32))
    h = _ln(x_ref[0], ig_ref[...], ib_ref[...])
    h1 = _ln(DEEPNORM_ALPHA * h + y, l1g_ref[...], l1b_ref[...])
    h1_ref[0] = h1
    hp_ref[0] = _pack_bf16_pair(h1[:, :HALF_D], h1[:, HALF_D:])

    nt = (((1,), (1,)), ((), ()))
    h_hi = h1.astype(BF16)
    h_lo = (h1 - h_hi.astype(F32)).astype(BF16)
    lt = (lax.dot_general(wrh_ref[...], h_hi, nt, preferred_element_type=F32)
          + lax.dot_general(wrl_ref[...], h_hi, nt, preferred_element_type=F32)
          + lax.dot_general(wrh_ref[...], h_lo, nt, preferred_element_type=F32)) + br_ref[:, 0:1]
    row = lax.broadcasted_iota(jnp.int32, (ROUTE_ROWS, tm), 0)
    big = jnp.int32(1 << 20)
    is_g = (row >= N_EXPERTS) & (row < N_EXPERTS + N_GROUPS)
    gl = jnp.where(is_g, lt, -jnp.inf)
    gmax = jnp.max(gl, axis=0, keepdims=True)
    g_idx = jnp.min(jnp.where(gl == gmax, row - N_EXPERTS, big), axis=0, keepdims=True)
    g_w = 1.0 / jnp.sum(jnp.exp(gl - gmax), axis=0, keepdims=True)
    in_grp = (row < N_EXPERTS) & ((row // EXPERTS_PER_GROUP) == g_idx)
    el = jnp.where(in_grp, lt, -jnp.inf)
    v1 = jnp.max(el, axis=0, keepdims=True)
    i1 = jnp.min(jnp.where(el == v1, row, big), axis=0, keepdims=True)
    el2 = jnp.where(row == i1, -jnp.inf, el)
    v2 = jnp.max(el2, axis=0, keepdims=True)
    i2 = jnp.min(jnp.where(el2 == v2, row, big), axis=0, keepdims=True)
    e2 = jnp.exp(v2 - v1)
    w1 = g_w / (1.0 + e2)
    w2 = g_w * e2 / (1.0 + e2)

    oh1 = row == i1
    oh2 = row == i2
    oh = jnp.where(oh1 | oh2, 1.0, 0.0)
    before = jnp.dot(oh.astype(BF16), tri_ref[...], preferred_element_type=F32) + carry_sc[:, 0:1]
    r1 = jnp.sum(jnp.where(oh1, before, 0.0), axis=0, keepdims=True).astype(jnp.int32)
    r2 = jnp.sum(jnp.where(oh2, before, 0.0), axis=0, keepdims=True).astype(jnp.int32)
    carry_sc[...] = carry_sc[...] + jnp.sum(oh, axis=1, keepdims=True)
    cnt_ref[...] = carry_sc[...]

    sub = lax.broadcasted_iota(jnp.int32, (SUBLANES, tm), 0)
    ri_ref[...] = jnp.where(sub == 0, i1, jnp.where(sub == 1, i2, jnp.where(sub == 2, r1,
                            jnp.where(sub == 3, r2, 0))))
    lrow = lax.broadcasted_iota(jnp.int32, (LANES, tm), 0)
    rw_ref[0] = jnp.where(lrow == 0, w1, jnp.where(lrow == 1, w2, 0.0)).T[:, :8]


def _mix_call(x, ig, ib, u, pre, cw, cb, clg, clb, cog, o, wo, l1g, l1b, wrh, wrl, br):
    b, seq, _ = x.shape
    tm = MIX_TM
    nt = seq // tm
    full = lambda shape: pl.BlockSpec(shape, lambda i, j: (0,) * len(shape))
    tile = lambda w: pl.BlockSpec((1, tm, w), lambda i, j: (i, j, 0))
    halo = pl.BlockSpec((1, HALO, C_CONV), lambda i, j: (i, jnp.maximum(j * (tm // HALO) - 1, 0), 0))
    earlier = jnp.asarray(np.triu(np.ones((tm, tm), np.float32), k=1), BF16)
    return pl.pallas_call(
        _mix_kernel,
        grid=(b, nt),
        in_specs=[tile(D_MODEL), full((1, D_MODEL)), full((1, D_MODEL)),
                  tile(C_CONV), halo, full((HALO, C_CONV)), full((HALO, C_CONV)), full((1, C_CONV)),
                  full((1, C_CONV)), full((1, C_CONV)), full((1, C_CONV)),
                  tile(D_ATTN), full((D_MODEL, D_MODEL)),
                  full((1, D_MODEL)), full((1, D_MODEL)), full((ROUTE_ROWS, D_MODEL)),
                  full((ROUTE_ROWS, D_MODEL)), full((ROUTE_ROWS, LANES)), full((tm, tm))],
        out_specs=[tile(D_MODEL), tile(HALF_D), pl.BlockSpec((8, tm), lambda i, j: (0, i * nt + j)), tile(8),
                   full((ROUTE_ROWS, LANES))],
        out_shape=[jax.ShapeDtypeStruct((b, seq, D_MODEL), F32),
                   jax.ShapeDtypeStruct((b, seq, HALF_D), jnp.uint32),
                   jax.ShapeDtypeStruct((8, b * seq), jnp.int32),
                   jax.ShapeDtypeStruct((b, seq, 8), F32),
                   jax.ShapeDtypeStruct((ROUTE_ROWS, LANES), F32)],
        scratch_shapes=[pltpu.VMEM((HALO + tm, C_CONV), F32),
                        pltpu.VMEM((SUBLANES - 1, HALO + tm, C_CONV), F32),
                        pltpu.VMEM((ROUTE_ROWS, LANES), F32)],
        compiler_params=pltpu.CompilerParams(dimension_semantics=("arbitrary", "arbitrary"),
                                             vmem_limit_bytes=VMEM_LIMIT),
        name="mix",
    )(x, ig, ib, u, u, pre, cw, cb, clg, clb, cog, o, wo, l1g, l1b, wrh, wrl, br, earlier)


def _tables_kernel(ps_ref, pe_ref, cn_ref, ri_ref, dest_ref, blk_ref):
    ri = ri_ref[...]
    start = jnp.zeros_like(ri)
    for e in range(N_EXPERTS):
        start = jnp.where(ri == e, ps_ref[e], start)
    dest_ref[...] = jnp.zeros_like(dest_ref)
    dest_ref[0:2, :] = start[0:2, :] + ri_ref[2:4, :]
    blk_start = lax.broadcasted_iota(jnp.int32, blk_ref.shape, 1) * MOE_BM
    be = jnp.zeros(blk_ref.shape, jnp.int32)
    for e in range(N_EXPERTS):
        be = be + jnp.where(pe_ref[e] <= blk_start, 1, 0)
    be = jnp.minimum(be, N_EXPERTS - 1)
    end = jnp.zeros(blk_ref.shape, jnp.int32)
    for e in range(N_EXPERTS):
        end = jnp.where(be == e, ps_ref[e] + cn_ref[e], end)
    valid = jnp.clip(end - blk_start, 0, MOE_BM)
    row = lax.broadcasted_iota(jnp.int32, blk_ref.shape, 0)
    blk_ref[...] = jnp.where(row == 0, be, valid)


def _tables_call(pad_start, pad_end, counts, ri, n_blocks):
    t = ri.shape[1]
    blk_w = -(-n_blocks // LANES) * LANES
    return pl.pallas_call(
        _tables_kernel,
        grid_spec=pltpu.PrefetchScalarGridSpec(
            num_scalar_prefetch=3,
            grid=(1,),
            in_specs=[pl.BlockSpec((SUBLANES, t), lambda i, *_: (0, 0))],
            out_specs=[pl.BlockSpec((SUBLANES, t), lambda i, *_: (0, 0)),
                       pl.BlockSpec((SUBLANES, blk_w), lambda i, *_: (0, 0))]),
        out_shape=[jax.ShapeDtypeStruct((SUBLANES, t), jnp.int32),
                   jax.ShapeDtypeStruct((SUBLANES, blk_w), jnp.int32)],
        compiler_params=pltpu.CompilerParams(dimension_semantics=("arbitrary",),
                                             vmem_limit_bytes=VMEM_LIMIT),
        name="tables",
    )(pad_start, pad_end, counts, ri)


def _expert_kernel(be_ref, nv_ref, x_ref, wg_ref, wu_ref, wd_ref, y_ref, wg_sc, wu_sc, wd_sc):
    i = pl.program_id(0)
    nv = nv_ref[i]

    @pl.when((i == 0) | (be_ref[i] != be_ref[jnp.maximum(i - 1, 0)]))
    def _():
        wg_sc[...] = wg_ref[0, 0].astype(BF16)
        wu_sc[...] = wu_ref[0, 0].astype(BF16)
        wd_sc[...] = wd_ref[0, 0].astype(BF16)

    @pl.when(nv > 0)
    def _():
        rowid = lax.broadcasted_iota(jnp.int32, x_ref.shape, 0)
        x_lo, x_hi = _unpack_bf16_pair(jnp.where(rowid < nv, x_ref[...], jnp.uint32(0)))
        x_lo = x_lo.astype(BF16)
        x_hi = x_hi.astype(BF16)
        g = (jnp.dot(x_lo, wg_sc[:HALF_D, :], preferred_element_type=F32)
             + jnp.dot(x_hi, wg_sc[HALF_D:, :], preferred_element_type=F32))
        up = (jnp.dot(x_lo, wu_sc[:HALF_D, :], preferred_element_type=F32)
              + jnp.dot(x_hi, wu_sc[HALF_D:, :], preferred_element_type=F32))
        act = g * _sigmoid(g) * up
        y = jnp.dot(act.astype(BF16), wd_sc[...], preferred_element_type=F32)
        y_ref[...] = _pack_bf16_pair(y[:, :HALF_D], y[:, HALF_D:])

    @pl.when(nv == 0)
    def _():
        y_ref[...] = jnp.zeros_like(y_ref)


def _expert_call(blk_e, blk_valid, xb, w_gate, w_up, w_down):
    r = xb.shape[0]
    bm = MOE_BM
    wspec = lambda a, b: pl.BlockSpec((1, 1, a, b), lambda i, be, nv: (0, be[i], 0, 0))
    return pl.pallas_call(
        _expert_kernel,
        grid_spec=pltpu.PrefetchScalarGridSpec(
            num_scalar_prefetch=2,
            grid=(r // bm,),
            in_specs=[pl.BlockSpec((bm, HALF_D), lambda i, be, nv: (i, 0)),
                      wspec(D_MODEL, D_EXPERT), wspec(D_MODEL, D_EXPERT), wspec(D_EXPERT, D_MODEL)],
            out_specs=pl.BlockSpec((bm, HALF_D), lambda i, be, nv: (i, 0)),
            scratch_shapes=[pltpu.VMEM((D_MODEL, D_EXPERT), BF16), pltpu.VMEM((D_MODEL, D_EXPERT), BF16),
                            pltpu.VMEM((D_EXPERT, D_MODEL), BF16)]),
        out_shape=jax.ShapeDtypeStruct((r, HALF_D), jnp.uint32),
        compiler_params=pltpu.CompilerParams(dimension_semantics=("arbitrary",),
                                             vmem_limit_bytes=VMEM_LIMIT),
        name="experts",
    )(blk_e, blk_valid, xb, w_gate, w_up, w_down)


def _sc_gather_rows(table, idx):
    m = idx.shape[0]
    width = table.shape[1]
    sc = plsc.get_sparse_core_info()
    workers = sc.num_cores * sc.num_subcores
    per_worker = m // workers
    mesh = plsc.VectorSubcoreMesh(core_axis_name="c", subcore_axis_name="s")

    n_win = per_worker // SC_WINDOW
    assert n_win % 2 == 0 and n_win >= 2
    buf = lambda: [pltpu.VMEM((SC_WINDOW, width), table.dtype), pltpu.SemaphoreType.DMA]

    @functools.partial(pl.kernel, mesh=mesh, out_type=jax.ShapeDtypeStruct((m, width), table.dtype),
                       scratch_types=[pltpu.VMEM((per_worker,), jnp.int32)] + buf() + buf())
    def gather(table_hbm, idx_hbm, out_hbm, idx_v, rows_a, sem_a, rows_b, sem_b):
        base = (lax.axis_index("s") * sc.num_cores + lax.axis_index("c")) * per_worker
        bufs = ((rows_a, sem_a), (rows_b, sem_b))
        pltpu.sync_copy(idx_hbm.at[pl.ds(base, per_worker)], idx_v)

        def window_copy(win, which):
            rows_v, sem = bufs[which]
            return pltpu.make_async_copy(table_hbm.at[idx_v.at[pl.ds(win * SC_WINDOW, SC_WINDOW)]], rows_v, sem)

        def fetch(win, which):
            window_copy(win, which).start()

        def flush(win, which):
            window_copy(win, which).wait()
            pltpu.sync_copy(bufs[which][0], out_hbm.at[pl.ds(base + win * SC_WINDOW, SC_WINDOW)])

        fetch(0, 0)

        @pl.loop(0, n_win // 2 - 1)
        def _(pair):
            win = 2 * pair
            fetch(win + 1, 1)
            flush(win, 0)
            fetch(win + 2, 0)
            flush(win + 1, 1)

        fetch(n_win - 1, 1)
        flush(n_win - 2, 0)
        flush(n_win - 1, 1)

    return gather(table, idx)


def _sc_scatter_rows(rows, idx0, idx1, n_out):
    m, width = rows.shape
    sc = plsc.get_sparse_core_info()
    workers = sc.num_cores * sc.num_subcores
    per_worker = m // workers
    mesh = plsc.VectorSubcoreMesh(core_axis_name="c", subcore_axis_name="s")

    n_win = per_worker // SC_WINDOW
    assert n_win % 2 == 0 and n_win >= 2
    buf = lambda: [pltpu.VMEM((SC_WINDOW, width), rows.dtype),
                   pltpu.SemaphoreType.DMA, pltpu.SemaphoreType.DMA, pltpu.SemaphoreType.DMA]

    @functools.partial(pl.kernel, mesh=mesh, out_type=jax.ShapeDtypeStruct((n_out, width), rows.dtype),
                       scratch_types=[pltpu.VMEM((per_worker,), jnp.int32), pltpu.VMEM((per_worker,), jnp.int32)]
                       + buf() + buf())
    def scatter(rows_hbm, idx0_hbm, idx1_hbm, out_hbm, idx0_v, idx1_v, *scratch):
        base = (lax.axis_index("s") * sc.num_cores + lax.axis_index("c")) * per_worker
        bufs = (scratch[:4], scratch[4:])
        pltpu.sync_copy(idx0_hbm.at[pl.ds(base, per_worker)], idx0_v)
        pltpu.sync_copy(idx1_hbm.at[pl.ds(base, per_worker)], idx1_v)

        def load(win, which):
            rows_v, lsem, _, _ = bufs[which]
            pltpu.async_copy(rows_hbm.at[pl.ds(base + win * SC_WINDOW, SC_WINDOW)], rows_v, lsem)

        def send(win, which):
            rows_v, lsem, sem0, sem1 = bufs[which]
            here = pl.ds(win * SC_WINDOW, SC_WINDOW)
            pltpu.make_async_copy(rows_hbm.at[pl.ds(0, SC_WINDOW)], rows_v, lsem).wait()
            first = pltpu.async_copy(rows_v, out_hbm.at[idx0_v.at[here]], sem0)
            second = pltpu.async_copy(rows_v, out_hbm.at[idx1_v.at[here]], sem1)
            first.wait()
            second.wait()

        load(0, 0)

        @pl.loop(0, n_win // 2 - 1)
        def _(pair):
            win = 2 * pair
            load(win + 1, 1)
            send(win, 0)
            load(win + 2, 0)
            send(win + 1, 1)

        load(n_win - 1, 1)
        send(n_win - 2, 0)
        send(n_win - 1, 1)

    return scatter(rows, idx0, idx1)


def _combine_rows_kernel(h_ref, w_ref, g_ref, b_ref, y0_ref, y1_ref, o_ref):
    w = w_ref[...]
    y0_lo, y0_hi = _unpack_bf16_pair(y0_ref[...])
    y1_lo, y1_hi = _unpack_bf16_pair(y1_ref[...])
    y = jnp.concatenate([w[:, 0:1] * y0_lo + w[:, 1:2] * y1_lo, w[:, 0:1] * y0_hi + w[:, 1:2] * y1_hi], axis=1)
    o_ref[...] = _ln(DEEPNORM_ALPHA * h_ref[...] + y, g_ref[...], b_ref[...])


def _combine_rows_call(h1, rw, g, b, yg):
    t = h1.shape[0]
    tm = COMB_TM
    n = t // tm
    return pl.pallas_call(
        _combine_rows_kernel,
        grid=(n,),
        in_specs=[pl.BlockSpec((tm, D_MODEL), lambda i: (i, 0)),
                  pl.BlockSpec((tm, 8), lambda i: (i, 0)),
                  pl.BlockSpec((1, D_MODEL), lambda i: (0, 0)),
                  pl.BlockSpec((1, D_MODEL), lambda i: (0, 0)),
                  pl.BlockSpec((tm, HALF_D), lambda i: (i, 0)),
                  pl.BlockSpec((tm, HALF_D), lambda i: (n + i, 0))],
        out_specs=pl.BlockSpec((tm, D_MODEL), lambda i: (i, 0)),
        out_shape=jax.ShapeDtypeStruct((t, D_MODEL), F32),
        compiler_params=pltpu.CompilerParams(dimension_semantics=("arbitrary",),
                                             vmem_limit_bytes=VMEM_LIMIT),
        name="combine_rows",
    )(h1, rw, g, b, yg, yg)


def _take_cols(w, idx):
    wz = jnp.concatenate([w, jnp.zeros((w.shape[0], 1), w.dtype)], axis=1)
    return wz[:, np.where(idx < 0, w.shape[1], idx)]


def _layout_indices():
    lane_src = np.full((HEAD_PAD,), -1, np.int64)
    lane_src[0:ROPE_HALF] = QK_NOPE + np.arange(ROPE_HALF)
    lane_src[ROPE_HALF:X2_LANE] = np.arange(X2_LANE - ROPE_HALF)
    lane_src[X2_LANE:X2_LANE + ROPE_HALF] = QK_NOPE + ROPE_HALF + np.arange(ROPE_HALF)
    n_lo = X2_LANE - ROPE_HALF
    lane_src[X2_LANE + ROPE_HALF:X2_LANE + ROPE_HALF + QK_NOPE - n_lo] = n_lo + np.arange(QK_NOPE - n_lo)
    q_idx = np.concatenate([np.where(lane_src >= 0, lane_src + (QK_NOPE + QK_ROPE) * h, -1)
                            for h in range(N_HEADS)])
    nope_src = np.where(lane_src < QK_NOPE, lane_src, -1)
    k_idx = np.concatenate([np.where(nope_src >= 0, nope_src + (QK_NOPE + V_DIM) * h, -1)
                            for h in range(N_HEADS)])
    v_idx = np.concatenate([QK_NOPE + np.arange(V_DIM) + (QK_NOPE + V_DIM) * h for h in range(N_HEADS)])
    kr_idx = np.full((LANES,), -1, np.int64)
    kr_idx[0:ROPE_HALF] = np.arange(ROPE_HALF)
    kr_idx[X2_LANE:X2_LANE + ROPE_HALF] = ROPE_HALF + np.arange(ROPE_HALF)
    return q_idx, k_idx, v_idx, kr_idx


def _rope_lane_tables(length):
    inv_freq = np.float32(ROPE_THETA) ** (-np.arange(ROPE_HALF, dtype=np.float32) / np.float32(ROPE_HALF))
    ang = np.arange(length, dtype=np.float32)[:, None] * inv_freq[None, :].astype(np.float32)
    cos, sin = np.cos(ang).astype(np.float32), np.sin(ang).astype(np.float32)
    ctab = np.ones((length, LANES), np.float32)
    stab = np.zeros((length, LANES), np.float32)
    for lo, sign in ((0, -1.0), (X2_LANE, 1.0)):
        ctab[:, lo:lo + ROPE_HALF] = cos
        stab[:, lo:lo + ROPE_HALF] = sign * sin
    return jnp.asarray(ctab), jnp.asarray(stab)


def kernel(x, meta_tokens, ln_in_g, ln_in_b, w_in, conv_w, conv_b, conv_ln_g, conv_ln_b, q_norm_g, w_uq,
           kv_norm_g, w_ukv, conv_out_g, attn_out_g, w_o, ln1_g, ln1_b, w_rg, b_rg, w_re, b_re,
           w_gate, w_up, w_down, ln2_g, ln2_b):
    assert w_in.shape[0] == 1, "single layer"
    bsz, seq, d = x.shape
    t = bsz * seq
    row = lambda a: a.reshape(1, -1)

    q_idx, k_idx, v_idx, kr_idx = _layout_indices()
    c4 = 2 * C_CONV + Q_LORA + KV_LORA
    w_in_l = jnp.concatenate([w_in[0][:, :c4], _take_cols(w_in[0][:, c4:], kr_idx)], axis=1).astype(BF16)
    wuq_l = _take_cols(w_uq[0], q_idx).astype(BF16)
    wuk_l = _take_cols(w_ukv[0], k_idx).astype(BF16)
    wvt_l = w_ukv[0][:, v_idx].T.astype(BF16)
    wo_l = w_o[0].astype(BF16)
    n_route_pad = ROUTE_ROWS - N_EXPERTS - N_GROUPS
    wr_t = jnp.concatenate([w_re[0].T, w_rg[0].T, jnp.zeros((n_route_pad, d), F32)], axis=0)
    wrh_l = wr_t.astype(BF16)
    wrl_l = (wr_t - wrh_l.astype(F32)).astype(BF16)
    br_l = jnp.broadcast_to(jnp.concatenate([b_re[0], b_rg[0], jnp.zeros((n_route_pad,), F32)])[:, None],
                            (ROUTE_ROWS, LANES))
    ctab, stab = _rope_lane_tables(N_META + seq)

    proj_w = (row(ln_in_g), row(ln_in_b), w_in_l, row(q_norm_g[0]), wuq_l, row(kv_norm_g[0]), wuk_l, wvt_l)
    u_m, _, k_m, vt_m = _proj_call(meta_tokens.astype(F32), N_META, 1, *proj_w, ctab[:N_META], stab[:N_META])
    u, q, k, vt = _proj_call(x.reshape(t, d), PROJ_TM, seq // PROJ_TM, *proj_w, ctab[N_META:], stab[N_META:])

    o = _attn_call(q.reshape(bsz, seq, -1), k.reshape(bsz, seq, -1),
                   vt.reshape(bsz, seq // ATT_T, D_ATTN, ATT_T), k_m, vt_m, row(attn_out_g[0]))

    pre = jnp.concatenate([jnp.zeros((HALO - N_META, C_CONV), F32), u_m], axis=0)
    cw = jnp.concatenate([conv_w[0], jnp.zeros((HALO - CONV_K, C_CONV), F32)], axis=0)
    h1, hp, ri, rw, cnt = _mix_call(x, row(ln_in_g), row(ln_in_b), u.reshape(bsz, seq, C_CONV), pre, cw,
                                    row(conv_b[0]), row(conv_ln_g[0]), row(conv_ln_b[0]), row(conv_out_g[0]),
                                    o, wo_l, row(ln1_g[0]), row(ln1_b[0]), wrh_l, wrl_l, br_l)

    counts = cnt[:N_EXPERTS, 0].astype(jnp.int32)
    padded = (counts + MOE_BM - 1) // MOE_BM * MOE_BM
    pad_end = jnp.cumsum(padded)
    pad_start = pad_end - padded
    n_blocks = (2 * t) // MOE_BM + N_EXPERTS
    dest, blk = _tables_call(pad_start, pad_end, counts, ri, n_blocks)
    blk_e, blk_valid = blk[0, :n_blocks], blk[1, :n_blocks]

    h1f = h1.reshape(t, d)
    dest0, dest1 = dest[0], dest[1]
    xb = _sc_scatter_rows(hp.reshape(t, HALF_D), dest0, dest1, n_blocks * MOE_BM)
    yb = _expert_call(blk_e, blk_valid, xb, w_gate, w_up, w_down)
    yg = _sc_gather_rows(yb, jnp.concatenate([dest0, dest1]))
    out = _combine_rows_call(h1f, rw.reshape(t, 8), row(ln2_g[0]), row(ln2_b[0]), yg)
    return out.reshape(bsz, seq, d)
```

```python
import functools
import math

import numpy as np
import jax
import jax.numpy as jnp
from jax import lax
from jax.experimental import pallas as pl
from jax.experimental.pallas import tpu as pltpu
from jax.experimental.pallas import tpu_sc as plsc

F32 = jnp.float32
BF16 = jnp.bfloat16

D_MODEL = 1024
HALF_D = D_MODEL // 2
N_META = 16
CHUNK = 64
C_CONV = 512
CONV_K = 31
N_HEADS = 8
QK_NOPE = 64
QK_ROPE = 32
V_DIM = 64
Q_LORA = 384
KV_LORA = 256
ROPE_THETA = 10000.0
D_ATTN = N_HEADS * V_DIM
N_GROUPS = 4
EXPERTS_PER_GROUP = 8
N_EXPERTS = N_GROUPS * EXPERTS_PER_GROUP
D_EXPERT = 256
DEEPNORM_ALPHA = 2.0 ** 0.25
EPS = 1e-5
NEG_INF = -1e30

LANES = 128
SUBLANES = 8
HEAD_PAD = LANES
ROPE_HALF = QK_ROPE // 2
X2_LANE = 64
Z_COLS = 2 * C_CONV + Q_LORA + KV_LORA + LANES
Q_SCALE = (QK_NOPE + QK_ROPE) ** -0.5 * math.log2(math.e)

PROJ_TM = 1024
ATT_T = 256
ATT_HEADS = 8
MIX_TM = 512
ROUTE_ROWS = 48
HALO = 32
MOE_BM = 1024
COMB_TM = 1024
SC_WINDOW = 64
VMEM_LIMIT = 56 * 1024 * 1024


def _ln(x, g, b):
    mu = jnp.mean(x, axis=-1, keepdims=True)
    xc = x - mu
    var = jnp.mean(xc * xc, axis=-1, keepdims=True)
    return xc * lax.rsqrt(var + EPS) * g + b


def _rms(x, g):
    return x * lax.rsqrt(jnp.mean(x * x, axis=-1, keepdims=True) + EPS) * g


def _sigmoid(x):
    return 1.0 / (1.0 + jnp.exp(-x))


def _pack_bf16_pair(lo, hi):
    lo_bits = lax.bitcast_convert_type(lo.astype(BF16).astype(F32), jnp.uint32)
    hi_bits = lax.bitcast_convert_type(hi.astype(BF16).astype(F32), jnp.uint32)
    return (lo_bits >> 16) | (hi_bits & jnp.uint32(0xFFFF0000))


def _unpack_bf16_pair(packed):
    lo = lax.bitcast_convert_type(packed << 16, F32)
    hi = lax.bitcast_convert_type(packed & jnp.uint32(0xFFFF0000), F32)
    return lo, hi


def _proj_kernel(x_ref, g_ref, b_ref, win_ref, qg_ref, wuq_ref, kvg_ref, wuk_ref, wvt_ref, c_ref, s_ref,
                 u_ref, q_ref, k_ref, vt_ref):
    n_part, part = vt_ref.shape[1], vt_ref.shape[3]
    zs = []
    for r in range(n_part):
        h = _ln(x_ref[r * part:(r + 1) * part, :], g_ref[...], b_ref[...])
        zs.append(jnp.dot(h.astype(BF16), win_ref[...], preferred_element_type=F32))
    c0 = 2 * C_CONV
    for r, z in enumerate(zs):
        rows = slice(r * part, (r + 1) * part)
        u_ref[rows, :] = z[:, :C_CONV] * _sigmoid(z[:, C_CONV:c0])
        cq = z[:, c0:c0 + Q_LORA]
        ckv = z[:, c0 + Q_LORA:c0 + Q_LORA + KV_LORA]
        kr = z[:, c0 + Q_LORA + KV_LORA:]
        cs = c_ref[rows, :]
        sn = s_ref[rows, :]
        q = jnp.dot(_rms(cq, qg_ref[...]).astype(BF16), wuq_ref[...], preferred_element_type=F32)
        kvn = _rms(ckv, kvg_ref[...]).astype(BF16)
        kn = jnp.dot(kvn, wuk_ref[...], preferred_element_type=F32)
        kr_rot = kr * cs + pltpu.roll(kr, X2_LANE, 1) * sn
        for hh in range(N_HEADS):
            sl = slice(HEAD_PAD * hh, HEAD_PAD * (hh + 1))
            qh = q[:, sl]
            qh = (qh * cs + pltpu.roll(qh, X2_LANE, 1) * sn) * Q_SCALE
            q_ref[rows, sl] = qh.astype(BF16)
            k_ref[rows, sl] = (kn[:, sl] + kr_rot).astype(BF16)
        vt = lax.dot_general(wvt_ref[...], kvn, (((1,), (1,)), ((), ())), preferred_element_type=F32)
        vt_ref[0, r] = vt.astype(BF16)


def _proj_call(x2d, tm, pos_blocks, ln_g, ln_b, w_in, qg, wuq, kvg, wuk, wvt, ctab, stab):
    t = x2d.shape[0]
    sub_w = min(tm, ATT_T)
    n_sub = tm // sub_w
    full = lambda shape: pl.BlockSpec(shape, lambda i: (0,) * len(shape))
    row = lambda w: pl.BlockSpec((tm, w), lambda i: (i, 0))
    tab = pl.BlockSpec((tm, LANES), lambda i: (i % pos_blocks, 0))
    return pl.pallas_call(
        _proj_kernel,
        grid=(t // tm,),
        in_specs=[row(D_MODEL), full((1, D_MODEL)), full((1, D_MODEL)), full((D_MODEL, Z_COLS)),
                  full((1, Q_LORA)), full((Q_LORA, N_HEADS * HEAD_PAD)),
                  full((1, KV_LORA)), full((KV_LORA, N_HEADS * HEAD_PAD)), full((D_ATTN, KV_LORA)), tab, tab],
        out_specs=[row(C_CONV), row(N_HEADS * HEAD_PAD), row(N_HEADS * HEAD_PAD),
                   pl.BlockSpec((1, n_sub, D_ATTN, sub_w), lambda i: (i, 0, 0, 0))],
        out_shape=[jax.ShapeDtypeStruct((t, C_CONV), F32),
                   jax.ShapeDtypeStruct((t, N_HEADS * HEAD_PAD), BF16),
                   jax.ShapeDtypeStruct((t, N_HEADS * HEAD_PAD), BF16),
                   jax.ShapeDtypeStruct((t // tm, n_sub, D_ATTN, sub_w), BF16)],
        compiler_params=pltpu.CompilerParams(dimension_semantics=("arbitrary",),
                                             vmem_limit_bytes=VMEM_LIMIT),
        name="proj",
    )(x2d, ln_g, ln_b, w_in, qg, wuq, kvg, wuk, wvt, ctab, stab)


def _attn_kernel(q_ref, k_ref, vt_ref, km_ref, vtm_ref, g_ref, o_ref, s_sc):
    seq = q_ref.shape[1]
    nq = seq // ATT_T
    key_chunk = lax.broadcasted_iota(jnp.int32, (ATT_T, ATT_T), 0) // CHUNK
    qry_chunk = lax.broadcasted_iota(jnp.int32, (ATT_T, ATT_T), 1) // CHUNK
    diag_visible = key_chunk <= qry_chunk
    nt = (((1,), (1,)), ((), ()))
    heads = range(ATT_HEADS)
    hsl = [slice(HEAD_PAD * hh, HEAD_PAD * (hh + 1)) for hh in heads]
    vsl = [slice(V_DIM * hh, V_DIM * (hh + 1)) for hh in heads]

    def update(s, vt, m, l, acc):
        m_new = jnp.maximum(m, jnp.max(s, axis=0, keepdims=True))
        alpha = jnp.exp2(m - m_new)
        p = jnp.exp2(s - m_new)
        vt1 = jnp.concatenate([vt, jnp.ones((2 * SUBLANES, vt.shape[1]), BF16)], axis=0)
        pv = jnp.dot(vt1, p.astype(BF16), preferred_element_type=F32)
        return m_new, alpha * l + pv[V_DIM:V_DIM + 1], alpha * acc + pv[:V_DIM]

    def scores(qi, j, hh):
        q0 = pl.multiple_of(qi * ATT_T, ATT_T)
        k0 = pl.multiple_of(j * ATT_T, ATT_T)
        return lax.dot_general(k_ref[0, pl.ds(k0, ATT_T), hsl[hh]], q_ref[0, pl.ds(q0, ATT_T), hsl[hh]], nt,
                               preferred_element_type=F32)

    for hh in heads:
        s_sc[0, hh] = scores(0, 0, hh)

    def q_body(qi, carry):
        row0 = pl.multiple_of(qi * ATT_T, ATT_T)
        slot = qi % 2
        state = []
        for hh in heads:
            state += [jnp.full((1, ATT_T), NEG_INF, F32), jnp.zeros((1, ATT_T), F32),
                      jnp.zeros((V_DIM, ATT_T), F32)]

        def kv_body(j, st):
            out = []
            s_next = scores(qi, j + 1, 0)
            for hh in heads:
                s_cur = s_sc[slot, hh]
                s_after = scores(qi, j + 1, hh + 1) if hh + 1 < ATT_HEADS else None
                out += update(s_cur, vt_ref[0, j, vsl[hh], :], *st[3 * hh:3 * hh + 3])
                s_sc[slot, hh] = s_next
                s_next = s_after
            return tuple(out)

        state = lax.fori_loop(0, qi, kv_body, tuple(state))
        q_next = jnp.minimum(qi + 1, nq - 1)
        sms = [lax.dot_general(km_ref[:, hsl[hh]], q_ref[0, pl.ds(row0, ATT_T), hsl[hh]], nt,
                               preferred_element_type=F32) for hh in heads]
        outs = []
        for hh in heads:
            m, l, acc = state[3 * hh:3 * hh + 3]
            s = jnp.where(diag_visible, s_sc[slot, hh], NEG_INF)
            sm = sms[hh]
            m_new = jnp.maximum(m, jnp.maximum(jnp.max(s, axis=0, keepdims=True),
                                               jnp.max(sm, axis=0, keepdims=True)))
            alpha = jnp.exp2(m - m_new)
            p = jnp.exp2(s - m_new)
            pm = jnp.exp2(sm - m_new)
            l = alpha * l + jnp.sum(p, axis=0, keepdims=True) + jnp.sum(pm, axis=0, keepdims=True)
            acc = (alpha * acc + jnp.dot(vt_ref[0, qi, vsl[hh], :], p.astype(BF16), preferred_element_type=F32)
                   + jnp.dot(vtm_ref[0, 0, vsl[hh], :], pm.astype(BF16), preferred_element_type=F32))
            outs.append(acc / l)
        for hh in heads:
            s_sc[1 - slot, hh] = scores(q_next, 0, hh)
        cat = jnp.concatenate(outs, axis=0)
        cat = cat * lax.rsqrt(jnp.mean(cat * cat, axis=0, keepdims=True) + EPS)
        o_ref[0, pl.ds(row0, ATT_T), :] = (cat.T * g_ref[...]).astype(BF16)
        return carry

    lax.fori_loop(0, nq, q_body, 0)


def _attn_call(q, k, vt, k_meta, vt_meta, out_g):
    b, seq, _ = q.shape
    nk = seq // ATT_T
    hb = ATT_HEADS
    assert hb == N_HEADS
    return pl.pallas_call(
        _attn_kernel,
        grid=(b, N_HEADS // hb),
        in_specs=[pl.BlockSpec((1, seq, hb * HEAD_PAD), lambda i, j: (i, 0, j)),
                  pl.BlockSpec((1, seq, hb * HEAD_PAD), lambda i, j: (i, 0, j)),
                  pl.BlockSpec((1, nk, hb * V_DIM, ATT_T), lambda i, j: (i, 0, j, 0)),
                  pl.BlockSpec((N_META, hb * HEAD_PAD), lambda i, j: (0, j)),
                  pl.BlockSpec((1, 1, hb * V_DIM, N_META), lambda i, j: (0, 0, j, 0)),
                  pl.BlockSpec((1, D_ATTN), lambda i, j: (0, 0))],
        out_specs=pl.BlockSpec((1, seq, hb * V_DIM), lambda i, j: (i, 0, j)),
        out_shape=jax.ShapeDtypeStruct((b, seq, D_ATTN), BF16),
        scratch_shapes=[pltpu.VMEM((2, hb, ATT_T, ATT_T), F32)],
        compiler_params=pltpu.CompilerParams(dimension_semantics=("arbitrary", "arbitrary"),
                                             vmem_limit_bytes=VMEM_LIMIT),
        name="attn",
    )(q, k, vt, k_meta, vt_meta, out_g)


def _mix_kernel(x_ref, ig_ref, ib_ref, u_ref, uh_ref, pre_ref, cw_ref, cb_ref, clg_ref, clb_ref,
                cog_ref, o_ref, wo_ref, l1g_ref, l1b_ref, wrh_ref, br_ref, tri_ref,
                h1_ref, hp_ref, ri_ref, rw_ref, cnt_ref, win_sc, sh_sc, carry_sc):
    bi = pl.program_id(0)
    ti = pl.program_id(1)
    tm = MIX_TM

    @pl.when((bi == 0) & (ti == 0))
    def _():
        carry_sc[...] = jnp.zeros_like(carry_sc)

    win_sc[0:HALO, :] = jnp.where(ti == 0, pre_ref[...], uh_ref[0])
    win_sc[HALO:, :] = u_ref[0]
    first = HALO - (CONV_K - 1)
    for res in range(1, SUBLANES):
        rows = tm + HALO - SUBLANES
        sh_sc[res - 1, 0:rows, :] = win_sc[pl.ds(res, rows), :]

    acc = jnp.broadcast_to(cb_ref[...], (tm, C_CONV))
    for o in range(first, first + CONV_K):
        res, lo = o % SUBLANES, o - o % SUBLANES
        src = sh_sc[res - 1, lo:lo + tm, :] if res else win_sc[lo:lo + tm, :]
        acc = acc + src * cw_ref[o - first:o - first + 1, :]
    c = _ln(acc, clg_ref[...], clb_ref[...])
    c = c * _sigmoid(c)
    cn = _rms(c, cog_ref[...])
    y = (jnp.dot(cn.astype(BF16), wo_ref[0:C_CONV, :], preferred_element_type=F32)
         + jnp.dot(o_ref[0], wo_ref[C_CONV:, :], preferred_element_type=F32))
    h = _ln(x_ref[0], ig_ref[...], ib_ref[...])
    h1 = _ln(DEEPNORM_ALPHA * h + y, l1g_ref[...], l1b_ref[...])
    h1_ref[0] = h1
    hp_ref[0] = _pack_bf16_pair(h1[:, :HALF_D], h1[:, HALF_D:])

    nt = (((1,), (1,)), ((), ()))
    lt = lax.dot_general(wrh_ref[...], h1.astype(BF16), nt, preferred_element_type=F32) + br_ref[:, 0:1]
    row = lax.broadcasted_iota(jnp.int32, (ROUTE_ROWS, tm), 0)
    big = jnp.int32(1 << 20)
    is_g = (row >= N_EXPERTS) & (row < N_EXPERTS + N_GROUPS)
    gl = jnp.where(is_g, lt, -jnp.inf)
    gmax = jnp.max(gl, axis=0, keepdims=True)
    g_idx = jnp.min(jnp.where(gl == gmax, row - N_EXPERTS, big), axis=0, keepdims=True)
    g_w = 1.0 / jnp.sum(jnp.exp(gl - gmax), axis=0, keepdims=True)
    in_grp = (row < N_EXPERTS) & ((row // EXPERTS_PER_GROUP) == g_idx)
    el = jnp.where(in_grp, lt, -jnp.inf)
    v1 = jnp.max(el, axis=0, keepdims=True)
    i1 = jnp.min(jnp.where(el == v1, row, big), axis=0, keepdims=True)
    el2 = jnp.where(row == i1, -jnp.inf, el)
    v2 = jnp.max(el2, axis=0, keepdims=True)
    i2 = jnp.min(jnp.where(el2 == v2, row, big), axis=0, keepdims=True)
    e2 = jnp.exp(v2 - v1)
    w1 = g_w / (1.0 + e2)
    w2 = g_w * e2 / (1.0 + e2)

    oh1 = row == i1
    oh2 = row == i2
    oh = jnp.where(oh1 | oh2, 1.0, 0.0)
    before = jnp.dot(oh.astype(BF16), tri_ref[...], preferred_element_type=F32) + carry_sc[:, 0:1]
    r1 = jnp.sum(jnp.where(oh1, before, 0.0), axis=0, keepdims=True).astype(jnp.int32)
    r2 = jnp.sum(jnp.where(oh2, before, 0.0), axis=0, keepdims=True).astype(jnp.int32)
    carry_sc[...] = carry_sc[...] + jnp.sum(oh, axis=1, keepdims=True)
    cnt_ref[...] = carry_sc[...]

    sub = lax.broadcasted_iota(jnp.int32, (SUBLANES, tm), 0)
    ri_ref[...] = jnp.where(sub == 0, i1, jnp.where(sub == 1, i2, jnp.where(sub == 2, r1,
                            jnp.where(sub == 3, r2, 0))))
    lrow = lax.broadcasted_iota(jnp.int32, (LANES, tm), 0)
    rw_ref[0] = jnp.where(lrow == 0, w1, jnp.where(lrow == 1, w2, 0.0)).T[:, :8]


def _mix_call(x, ig, ib, u, pre, cw, cb, clg, clb, cog, o, wo, l1g, l1b, wrh, br):
    b, seq, _ = x.shape
    tm = MIX_TM
    nt = seq // tm
    full = lambda shape: pl.BlockSpec(shape, lambda i, j: (0,) * len(shape))
    tile = lambda w: pl.BlockSpec((1, tm, w), lambda i, j: (i, j, 0))
    halo = pl.BlockSpec((1, HALO, C_CONV), lambda i, j: (i, jnp.maximum(j * (tm // HALO) - 1, 0), 0))
    earlier = jnp.asarray(np.triu(np.ones((tm, tm), np.float32), k=1), BF16)
    return pl.pallas_call(
        _mix_kernel,
        grid=(b, nt),
        in_specs=[tile(D_MODEL), full((1, D_MODEL)), full((1, D_MODEL)),
                  tile(C_CONV), halo, full((HALO, C_CONV)), full((HALO, C_CONV)), full((1, C_CONV)),
                  full((1, C_CONV)), full((1, C_CONV)), full((1, C_CONV)),
                  tile(D_ATTN), full((D_MODEL, D_MODEL)),
                  full((1, D_MODEL)), full((1, D_MODEL)), full((ROUTE_ROWS, D_MODEL)),
                  full((ROUTE_ROWS, LANES)), full((tm, tm))],
        out_specs=[tile(D_MODEL), tile(HALF_D), pl.BlockSpec((8, tm), lambda i, j: (0, i * nt + j)), tile(8),
                   full((ROUTE_ROWS, LANES))],
        out_shape=[jax.ShapeDtypeStruct((b, seq, D_MODEL), F32),
                   jax.ShapeDtypeStruct((b, seq, HALF_D), jnp.uint32),
                   jax.ShapeDtypeStruct((8, b * seq), jnp.int32),
                   jax.ShapeDtypeStruct((b, seq, 8), F32),
                   jax.ShapeDtypeStruct((ROUTE_ROWS, LANES), F32)],
        scratch_shapes=[pltpu.VMEM((HALO + tm, C_CONV), F32),
                        pltpu.VMEM((SUBLANES - 1, HALO + tm, C_CONV), F32),
                        pltpu.VMEM((ROUTE_ROWS, LANES), F32)],
        compiler_params=pltpu.CompilerParams(dimension_semantics=("arbitrary", "arbitrary"),
                                             vmem_limit_bytes=VMEM_LIMIT),
        name="mix",
    )(x, ig, ib, u, u, pre, cw, cb, clg, clb, cog, o, wo, l1g, l1b, wrh, br, earlier)


def _tables_kernel(ps_ref, pe_ref, cn_ref, ri_ref, dest_ref, blk_ref):
    ri = ri_ref[...]
    start = jnp.zeros_like(ri)
    for e in range(N_EXPERTS):
        start = jnp.where(ri == e, ps_ref[e], start)
    dest_ref[...] = jnp.zeros_like(dest_ref)
    dest_ref[0:2, :] = start[0:2, :] + ri_ref[2:4, :]
    blk_start = lax.broadcasted_iota(jnp.int32, blk_ref.shape, 1) * MOE_BM
    be = jnp.zeros(blk_ref.shape, jnp.int32)
    for e in range(N_EXPERTS):
        be = be + jnp.where(pe_ref[e] <= blk_start, 1, 0)
    be = jnp.minimum(be, N_EXPERTS - 1)
    end = jnp.zeros(blk_ref.shape, jnp.int32)
    for e in range(N_EXPERTS):
        end = jnp.where(be == e, ps_ref[e] + cn_ref[e], end)
    valid = jnp.clip(end - blk_start, 0, MOE_BM)
    row = lax.broadcasted_iota(jnp.int32, blk_ref.shape, 0)
    blk_ref[...] = jnp.where(row == 0, be, valid)


def _tables_call(pad_start, pad_end, counts, ri, n_blocks):
    t = ri.shape[1]
    blk_w = -(-n_blocks // LANES) * LANES
    return pl.pallas_call(
        _tables_kernel,
        grid_spec=pltpu.PrefetchScalarGridSpec(
            num_scalar_prefetch=3,
            grid=(1,),
            in_specs=[pl.BlockSpec((SUBLANES, t), lambda i, *_: (0, 0))],
            out_specs=[pl.BlockSpec((SUBLANES, t), lambda i, *_: (0, 0)),
                       pl.BlockSpec((SUBLANES, blk_w), lambda i, *_: (0, 0))]),
        out_shape=[jax.ShapeDtypeStruct((SUBLANES, t), jnp.int32),
                   jax.ShapeDtypeStruct((SUBLANES, blk_w), jnp.int32)],
        compiler_params=pltpu.CompilerParams(dimension_semantics=("arbitrary",),
                                             vmem_limit_bytes=VMEM_LIMIT),
        name="tables",
    )(pad_start, pad_end, counts, ri)


def _expert_kernel(be_ref, nv_ref, x_ref, wg_ref, wu_ref, wd_ref, y_ref, wg_sc, wu_sc, wd_sc):
    i = pl.program_id(0)
    nv = nv_ref[i]

    @pl.when((i == 0) | (be_ref[i] != be_ref[jnp.maximum(i - 1, 0)]))
    def _():
        wg_sc[...] = wg_ref[0, 0].astype(BF16)
        wu_sc[...] = wu_ref[0, 0].astype(BF16)
        wd_sc[...] = wd_ref[0, 0].astype(BF16)

    @pl.when(nv > 0)
    def _():
        rowid = lax.broadcasted_iota(jnp.int32, x_ref.shape, 0)
        x_lo, x_hi = _unpack_bf16_pair(jnp.where(rowid < nv, x_ref[...], jnp.uint32(0)))
        x_lo = x_lo.astype(BF16)
        x_hi = x_hi.astype(BF16)
        g = (jnp.dot(x_lo, wg_sc[:HALF_D, :], preferred_element_type=F32)
             + jnp.dot(x_hi, wg_sc[HALF_D:, :], preferred_element_type=F32))
        up = (jnp.dot(x_lo, wu_sc[:HALF_D, :], preferred_element_type=F32)
              + jnp.dot(x_hi, wu_sc[HALF_D:, :], preferred_element_type=F32))
        act = g * _sigmoid(g) * up
        y = jnp.dot(act.astype(BF16), wd_sc[...], preferred_element_type=F32)
        y_ref[...] = _pack_bf16_pair(y[:, :HALF_D], y[:, HALF_D:])

    @pl.when(nv == 0)
    def _():
        y_ref[...] = jnp.zeros_like(y_ref)


def _expert_call(blk_e, blk_valid, xb, w_gate, w_up, w_down):
    r = xb.shape[0]
    bm = MOE_BM
    wspec = lambda a, b: pl.BlockSpec((1, 1, a, b), lambda i, be, nv: (0, be[i], 0, 0))
    return pl.pallas_call(
        _expert_kernel,
        grid_spec=pltpu.PrefetchScalarGridSpec(
            num_scalar_prefetch=2,
            grid=(r // bm,),
            in_specs=[pl.BlockSpec((bm, HALF_D), lambda i, be, nv: (i, 0)),
                      wspec(D_MODEL, D_EXPERT), wspec(D_MODEL, D_EXPERT), wspec(D_EXPERT, D_MODEL)],
            out_specs=pl.BlockSpec((bm, HALF_D), lambda i, be, nv: (i, 0)),
            scratch_shapes=[pltpu.VMEM((D_MODEL, D_EXPERT), BF16), pltpu.VMEM((D_MODEL, D_EXPERT), BF16),
                            pltpu.VMEM((D_EXPERT, D_MODEL), BF16)]),
        out_shape=jax.ShapeDtypeStruct((r, HALF_D), jnp.uint32),
        compiler_params=pltpu.CompilerParams(dimension_semantics=("arbitrary",),
                                             vmem_limit_bytes=VMEM_LIMIT),
        name="experts",
    )(blk_e, blk_valid, xb, w_gate, w_up, w_down)


def _sc_gather_rows(table, idx):
    m = idx.shape[0]
    width = table.shape[1]
    sc = plsc.get_sparse_core_info()
    workers = sc.num_cores * sc.num_subcores
    per_worker = m // workers
    mesh = plsc.VectorSubcoreMesh(core_axis_name="c", subcore_axis_name="s")

    n_win = per_worker // SC_WINDOW
    assert n_win % 2 == 0 and n_win >= 2
    buf = lambda: [pltpu.VMEM((SC_WINDOW, width), table.dtype), pltpu.SemaphoreType.DMA]

    @functools.partial(pl.kernel, mesh=mesh, out_type=jax.ShapeDtypeStruct((m, width), table.dtype),
                       scratch_types=[pltpu.VMEM((per_worker,), jnp.int32)] + buf() + buf())
    def gather(table_hbm, idx_hbm, out_hbm, idx_v, rows_a, sem_a, rows_b, sem_b):
        base = (lax.axis_index("s") * sc.num_cores + lax.axis_index("c")) * per_worker
        bufs = ((rows_a, sem_a), (rows_b, sem_b))
        pltpu.sync_copy(idx_hbm.at[pl.ds(base, per_worker)], idx_v)

        def window_copy(win, which):
            rows_v, sem = bufs[which]
            return pltpu.make_async_copy(table_hbm.at[idx_v.at[pl.ds(win * SC_WINDOW, SC_WINDOW)]], rows_v, sem)

        def fetch(win, which):
            window_copy(win, which).start()

        def flush(win, which):
            window_copy(win, which).wait()
            pltpu.sync_copy(bufs[which][0], out_hbm.at[pl.ds(base + win * SC_WINDOW, SC_WINDOW)])

        fetch(0, 0)

        @pl.loop(0, n_win // 2 - 1)
        def _(pair):
            win = 2 * pair
            fetch(win + 1, 1)
            flush(win, 0)
            fetch(win + 2, 0)
            flush(win + 1, 1)

        fetch(n_win - 1, 1)
        flush(n_win - 2, 0)
        flush(n_win - 1, 1)

    return gather(table, idx)


def _sc_scatter_rows(rows, idx0, idx1, n_out):
    m, width = rows.shape
    sc = plsc.get_sparse_core_info()
    workers = sc.num_cores * sc.num_subcores
    per_worker = m // workers
    mesh = plsc.VectorSubcoreMesh(core_axis_name="c", subcore_axis_name="s")

    n_win = per_worker // SC_WINDOW
    assert n_win % 2 == 0 and n_win >= 2
    buf = lambda: [pltpu.VMEM((SC_WINDOW, width), rows.dtype),
                   pltpu.SemaphoreType.DMA, pltpu.SemaphoreType.DMA, pltpu.SemaphoreType.DMA]

    @functools.partial(pl.kernel, mesh=mesh, out_type=jax.ShapeDtypeStruct((n_out, width), rows.dtype),
                       scratch_types=[pltpu.VMEM((per_worker,), jnp.int32), pltpu.VMEM((per_worker,), jnp.int32)]
                       + buf() + buf())
    def scatter(rows_hbm, idx0_hbm, idx1_hbm, out_hbm, idx0_v, idx1_v, *scratch):
        base = (lax.axis_index("s") * sc.num_cores + lax.axis_index("c")) * per_worker
        bufs = (scratch[:4], scratch[4:])
        pltpu.sync_copy(idx0_hbm.at[pl.ds(base, per_worker)], idx0_v)
        pltpu.sync_copy(idx1_hbm.at[pl.ds(base, per_worker)], idx1_v)

        def load(win, which):
            rows_v, lsem, _, _ = bufs[which]
            pltpu.async_copy(rows_hbm.at[pl.ds(base + win * SC_WINDOW, SC_WINDOW)], rows_v, lsem)

        def send(win, which):
            rows_v, lsem, sem0, sem1 = bufs[which]
            here = pl.ds(win * SC_WINDOW, SC_WINDOW)
            pltpu.make_async_copy(rows_hbm.at[pl.ds(0, SC_WINDOW)], rows_v, lsem).wait()
            first = pltpu.async_copy(rows_v, out_hbm.at[idx0_v.at[here]], sem0)
            second = pltpu.async_copy(rows_v, out_hbm.at[idx1_v.at[here]], sem1)
            first.wait()
            second.wait()

        load(0, 0)

        @pl.loop(0, n_win // 2 - 1)
        def _(pair):
            win = 2 * pair
            load(win + 1, 1)
            send(win, 0)
            load(win + 2, 0)
            send(win + 1, 1)

        load(n_win - 1, 1)
        send(n_win - 2, 0)
        send(n_win - 1, 1)

    return scatter(rows, idx0, idx1)


def _combine_rows_kernel(h_ref, w_ref, g_ref, b_ref, y0_ref, y1_ref, o_ref):
    w = w_ref[...]
    y0_lo, y0_hi = _unpack_bf16_pair(y0_ref[...])
    y1_lo, y1_hi = _unpack_bf16_pair(y1_ref[...])
    y = jnp.concatenate([w[:, 0:1] * y0_lo + w[:, 1:2] * y1_lo, w[:, 0:1] * y0_hi + w[:, 1:2] * y1_hi], axis=1)
    o_ref[...] = _ln(DEEPNORM_ALPHA * h_ref[...] + y, g_ref[...], b_ref[...])


def _combine_rows_call(h1, rw, g, b, yg):
    t = h1.shape[0]
    tm = COMB_TM
    n = t // tm
    return pl.pallas_call(
        _combine_rows_kernel,
        grid=(n,),
        in_specs=[pl.BlockSpec((tm, D_MODEL), lambda i: (i, 0)),
                  pl.BlockSpec((tm, 8), lambda i: (i, 0)),
                  pl.BlockSpec((1, D_MODEL), lambda i: (0, 0)),
                  pl.BlockSpec((1, D_MODEL), lambda i: (0, 0)),
                  pl.BlockSpec((tm, HALF_D), lambda i: (i, 0)),
                  pl.BlockSpec((tm, HALF_D), lambda i: (n + i, 0))],
        out_specs=pl.BlockSpec((tm, D_MODEL), lambda i: (i, 0)),
        out_shape=jax.ShapeDtypeStruct((t, D_MODEL), F32),
        compiler_params=pltpu.CompilerParams(dimension_semantics=("arbitrary",),
                                             vmem_limit_bytes=VMEM_LIMIT),
        name="combine_rows",
    )(h1, rw, g, b, yg, yg)


def _take_cols(w, idx):
    wz = jnp.concatenate([w, jnp.zeros((w.shape[0], 1), w.dtype)], axis=1)
    return wz[:, np.where(idx < 0, w.shape[1], idx)]


def _layout_indices():
    lane_src = np.full((HEAD_PAD,), -1, np.int64)
    lane_src[0:ROPE_HALF] = QK_NOPE + np.arange(ROPE_HALF)
    lane_src[ROPE_HALF:X2_LANE] = np.arange(X2_LANE - ROPE_HALF)
    lane_src[X2_LANE:X2_LANE + ROPE_HALF] = QK_NOPE + ROPE_HALF + np.arange(ROPE_HALF)
    n_lo = X2_LANE - ROPE_HALF
    lane_src[X2_LANE + ROPE_HALF:X2_LANE + ROPE_HALF + QK_NOPE - n_lo] = n_lo + np.arange(QK_NOPE - n_lo)
    q_idx = np.concatenate([np.where(lane_src >= 0, lane_src + (QK_NOPE + QK_ROPE) * h, -1)
                            for h in range(N_HEADS)])
    nope_src = np.where(lane_src < QK_NOPE, lane_src, -1)
    k_idx = np.concatenate([np.where(nope_src >= 0, nope_src + (QK_NOPE + V_DIM) * h, -1)
                            for h in range(N_HEADS)])
    v_idx = np.concatenate([QK_NOPE + np.arange(V_DIM) + (QK_NOPE + V_DIM) * h for h in range(N_HEADS)])
    kr_idx = np.full((LANES,), -1, np.int64)
    kr_idx[0:ROPE_HALF] = np.arange(ROPE_HALF)
    kr_idx[X2_LANE:X2_LANE + ROPE_HALF] = ROPE_HALF + np.arange(ROPE_HALF)
    return q_idx, k_idx, v_idx, kr_idx


def _rope_lane_tables(length):
    inv_freq = np.float32(ROPE_THETA) ** (-np.arange(ROPE_HALF, dtype=np.float32) / np.float32(ROPE_HALF))
    ang = np.arange(length, dtype=np.float32)[:, None] * inv_freq[None, :].astype(np.float32)
    cos, sin = np.cos(ang).astype(np.float32), np.sin(ang).astype(np.float32)
    ctab = np.ones((length, LANES), np.float32)
    stab = np.zeros((length, LANES), np.float32)
    for lo, sign in ((0, -1.0), (X2_LANE, 1.0)):
        ctab[:, lo:lo + ROPE_HALF] = cos
        stab[:, lo:lo + ROPE_HALF] = sign * sin
    return jnp.asarray(ctab), jnp.asarray(stab)


def kernel(x, meta_tokens, ln_in_g, ln_in_b, w_in, conv_w, conv_b, conv_ln_g, conv_ln_b, q_norm_g, w_uq,
           kv_norm_g, w_ukv, conv_out_g, attn_out_g, w_o, ln1_g, ln1_b, w_rg, b_rg, w_re, b_re,
           w_gate, w_up, w_down, ln2_g, ln2_b):
    assert w_in.shape[0] == 1, "single layer"
    bsz, seq, d = x.shape
    t = bsz * seq
    row = lambda a: a.reshape(1, -1)

    q_idx, k_idx, v_idx, kr_idx = _layout_indices()
    c4 = 2 * C_CONV + Q_LORA + KV_LORA
    w_in_l = jnp.concatenate([w_in[0][:, :c4], _take_cols(w_in[0][:, c4:], kr_idx)], axis=1).astype(BF16)
    wuq_l = _take_cols(w_uq[0], q_idx).astype(BF16)
    wuk_l = _take_cols(w_ukv[0], k_idx).astype(BF16)
    wvt_l = w_ukv[0][:, v_idx].T.astype(BF16)
    wo_l = w_o[0].astype(BF16)
    n_route_pad = ROUTE_ROWS - N_EXPERTS - N_GROUPS
    wr_t = jnp.concatenate([w_re[0].T, w_rg[0].T, jnp.zeros((n_route_pad, d), F32)], axis=0)
    wrh_l = wr_t.astype(BF16)
    br_l = jnp.broadcast_to(jnp.concatenate([b_re[0], b_rg[0], jnp.zeros((n_route_pad,), F32)])[:, None],
                            (ROUTE_ROWS, LANES))
    ctab, stab = _rope_lane_tables(N_META + seq)

    proj_w = (row(ln_in_g), row(ln_in_b), w_in_l, row(q_norm_g[0]), wuq_l, row(kv_norm_g[0]), wuk_l, wvt_l)
    u_m, _, k_m, vt_m = _proj_call(meta_tokens.astype(F32), N_META, 1, *proj_w, ctab[:N_META], stab[:N_META])
    u, q, k, vt = _proj_call(x.reshape(t, d), PROJ_TM, seq // PROJ_TM, *proj_w, ctab[N_META:], stab[N_META:])

    o = _attn_call(q.reshape(bsz, seq, -1), k.reshape(bsz, seq, -1),
                   vt.reshape(bsz, seq // ATT_T, D_ATTN, ATT_T), k_m, vt_m, row(attn_out_g[0]))

    pre = jnp.concatenate([jnp.zeros((HALO - N_META, C_CONV), F32), u_m], axis=0)
    cw = jnp.concatenate([conv_w[0], jnp.zeros((HALO - CONV_K, C_CONV), F32)], axis=0)
    h1, hp, ri, rw, cnt = _mix_call(x, row(ln_in_g), row(ln_in_b), u.reshape(bsz, seq, C_CONV), pre, cw,
                                    row(conv_b[0]), row(conv_ln_g[0]), row(conv_ln_b[0]), row(conv_out_g[0]),
                                    o, wo_l, row(ln1_g[0]), row(ln1_b[0]), wrh_l, br_l)

    counts = cnt[:N_EXPERTS, 0].astype(jnp.int32)
    padded = (counts + MOE_BM - 1) // MOE_BM * MOE_BM
    pad_end = jnp.cumsum(padded)
    pad_start = pad_end - padded
    n_blocks = (2 * t) // MOE_BM + N_EXPERTS
    dest, blk = _tables_call(pad_start, pad_end, counts, ri, n_blocks)
    blk_e, blk_valid = blk[0, :n_blocks], blk[1, :n_blocks]

    h1f = h1.reshape(t, d)
    dest0, dest1 = dest[0], dest[1]
    xb = _sc_scatter_rows(hp.reshape(t, HALF_D), dest0, dest1, n_blocks * MOE_BM)
    yb = _expert_call(blk_e, blk_valid, xb, w_gate, w_up, w_down)
    yg = _sc_gather_rows(yb, jnp.concatenate([dest0, dest1]))
    out = _combine_rows_call(h1f, rw.reshape(t, 8), row(ln2_g[0]), row(ln2_b[0]), yg)
    return out.reshape(bsz, seq, d)
```

```python
import functools
import math

import numpy as np
import jax
import jax.numpy as jnp
from jax import lax
from jax.experimental import pallas as pl
from jax.experimental.pallas import tpu as pltpu
from jax.experimental.pallas import tpu_sc as plsc

F32 = jnp.float32
BF16 = jnp.bfloat16

D_MODEL = 1024
HALF_D = D_MODEL // 2
N_META = 16
CHUNK = 64
C_CONV = 512
CONV_K = 31
N_HEADS = 8
QK_NOPE = 64
QK_ROPE = 32
V_DIM = 64
Q_LORA = 384
KV_LORA = 256
ROPE_THETA = 10000.0
D_ATTN = N_HEADS * V_DIM
N_GROUPS = 4
EXPERTS_PER_GROUP = 8
N_EXPERTS = N_GROUPS * EXPERTS_PER_GROUP
D_EXPERT = 256
DEEPNORM_ALPHA = 2.0 ** 0.25
EPS = 1e-5
NEG_INF = -1e30

LANES = 128
SUBLANES = 8
HEAD_PAD = LANES
ROPE_HALF = QK_ROPE // 2
X2_LANE = 64
Z_COLS = 2 * C_CONV + Q_LORA + KV_LORA + LANES
Q_SCALE = (QK_NOPE + QK_ROPE) ** -0.5 * math.log2(math.e)

PROJ_TM = 1024
ATT_T = 256
ATT_HEADS = 8
MIX_TM = 512
ROUTE_ROWS = 48
HALO = 32
MOE_BM = 1024
COMB_TM = 1024
SC_WINDOW = 64
VMEM_LIMIT = 56 * 1024 * 1024


def _ln(x, g, b):
    mu = jnp.mean(x, axis=-1, keepdims=True)
    xc = x - mu
    var = jnp.mean(xc * xc, axis=-1, keepdims=True)
    return xc * lax.rsqrt(var + EPS) * g + b


def _rms(x, g):
    return x * lax.rsqrt(jnp.mean(x * x, axis=-1, keepdims=True) + EPS) * g


def _sigmoid(x):
    return 1.0 / (1.0 + jnp.exp(-x))


def _pack_bf16_pair(lo, hi):
    lo_bits = lax.bitcast_convert_type(lo.astype(BF16).astype(F32), jnp.uint32)
    hi_bits = lax.bitcast_convert_type(hi.astype(BF16).astype(F32), jnp.uint32)
    return (lo_bits >> 16) | (hi_bits & jnp.uint32(0xFFFF0000))


def _unpack_bf16_pair(packed):
    lo = lax.bitcast_convert_type(packed << 16, F32)
    hi = lax.bitcast_convert_type(packed & jnp.uint32(0xFFFF0000), F32)
    return lo, hi


def _proj_kernel(x_ref, g_ref, b_ref, win_ref, qg_ref, wuq_ref, kvg_ref, wuk_ref, wvt_ref, c_ref, s_ref,
                 u_ref, q_ref, k_ref, vt_ref):
    n_part, part = vt_ref.shape[1], vt_ref.shape[3]
    zs = []
    for r in range(n_part):
        h = _ln(x_ref[r * part:(r + 1) * part, :], g_ref[...], b_ref[...])
        zs.append(jnp.dot(h.astype(BF16), win_ref[...], preferred_element_type=F32))
    c0 = 2 * C_CONV
    for r, z in enumerate(zs):
        rows = slice(r * part, (r + 1) * part)
        u_ref[rows, :] = z[:, :C_CONV] * _sigmoid(z[:, C_CONV:c0])
        cq = z[:, c0:c0 + Q_LORA]
        ckv = z[:, c0 + Q_LORA:c0 + Q_LORA + KV_LORA]
        kr = z[:, c0 + Q_LORA + KV_LORA:]
        cs = c_ref[rows, :]
        sn = s_ref[rows, :]
        q = jnp.dot(_rms(cq, qg_ref[...]).astype(BF16), wuq_ref[...], preferred_element_type=F32)
        kvn = _rms(ckv, kvg_ref[...]).astype(BF16)
        kn = jnp.dot(kvn, wuk_ref[...], preferred_element_type=F32)
        kr_rot = kr * cs + pltpu.roll(kr, X2_LANE, 1) * sn
        for hh in range(N_HEADS):
            sl = slice(HEAD_PAD * hh, HEAD_PAD * (hh + 1))
            qh = q[:, sl]
            qh = (qh * cs + pltpu.roll(qh, X2_LANE, 1) * sn) * Q_SCALE
            q_ref[rows, sl] = qh.astype(BF16)
            k_ref[rows, sl] = (kn[:, sl] + kr_rot).astype(BF16)
        vt = lax.dot_general(wvt_ref[...], kvn, (((1,), (1,)), ((), ())), preferred_element_type=F32)
        vt_ref[0, r] = vt.astype(BF16)


def _proj_call(x2d, tm, pos_blocks, ln_g, ln_b, w_in, qg, wuq, kvg, wuk, wvt, ctab, stab):
    t = x2d.shape[0]
    sub_w = min(tm, ATT_T)
    n_sub = tm // sub_w
    full = lambda shape: pl.BlockSpec(shape, lambda i: (0,) * len(shape))
    row = lambda w: pl.BlockSpec((tm, w), lambda i: (i, 0))
    tab = pl.BlockSpec((tm, LANES), lambda i: (i % pos_blocks, 0))
    return pl.pallas_call(
        _proj_kernel,
        grid=(t // tm,),
        in_specs=[row(D_MODEL), full((1, D_MODEL)), full((1, D_MODEL)), full((D_MODEL, Z_COLS)),
                  full((1, Q_LORA)), full((Q_LORA, N_HEADS * HEAD_PAD)),
                  full((1, KV_LORA)), full((KV_LORA, N_HEADS * HEAD_PAD)), full((D_ATTN, KV_LORA)), tab, tab],
        out_specs=[row(C_CONV), row(N_HEADS * HEAD_PAD), row(N_HEADS * HEAD_PAD),
                   pl.BlockSpec((1, n_sub, D_ATTN, sub_w), lambda i: (i, 0, 0, 0))],
        out_shape=[jax.ShapeDtypeStruct((t, C_CONV), F32),
                   jax.ShapeDtypeStruct((t, N_HEADS * HEAD_PAD), BF16),
                   jax.ShapeDtypeStruct((t, N_HEADS * HEAD_PAD), BF16),
                   jax.ShapeDtypeStruct((t // tm, n_sub, D_ATTN, sub_w), BF16)],
        compiler_params=pltpu.CompilerParams(dimension_semantics=("arbitrary",),
                                             vmem_limit_bytes=VMEM_LIMIT),
        name="proj",
    )(x2d, ln_g, ln_b, w_in, qg, wuq, kvg, wuk, wvt, ctab, stab)


def _attn_kernel(q_ref, k_ref, vt_ref, km_ref, vtm_ref, g_ref, o_ref, s_sc):
    seq = q_ref.shape[1]
    nq = seq // ATT_T
    key_chunk = lax.broadcasted_iota(jnp.int32, (ATT_T, ATT_T), 0) // CHUNK
    qry_chunk = lax.broadcasted_iota(jnp.int32, (ATT_T, ATT_T), 1) // CHUNK
    diag_visible = key_chunk <= qry_chunk
    nt = (((1,), (1,)), ((), ()))
    heads = range(ATT_HEADS)
    hsl = [slice(HEAD_PAD * hh, HEAD_PAD * (hh + 1)) for hh in heads]
    vsl = [slice(V_DIM * hh, V_DIM * (hh + 1)) for hh in heads]

    def update(s, vt, m, l, acc):
        m_new = jnp.maximum(m, jnp.max(s, axis=0, keepdims=True))
        alpha = jnp.exp2(m - m_new)
        p = jnp.exp2(s - m_new)
        vt1 = jnp.concatenate([vt, jnp.ones((2 * SUBLANES, vt.shape[1]), BF16)], axis=0)
        pv = jnp.dot(vt1, p.astype(BF16), preferred_element_type=F32)
        return m_new, alpha * l + pv[V_DIM:V_DIM + 1], alpha * acc + pv[:V_DIM]

    def scores(qi, j, hh):
        q0 = pl.multiple_of(qi * ATT_T, ATT_T)
        k0 = pl.multiple_of(j * ATT_T, ATT_T)
        return lax.dot_general(k_ref[0, pl.ds(k0, ATT_T), hsl[hh]], q_ref[0, pl.ds(q0, ATT_T), hsl[hh]], nt,
                               preferred_element_type=F32)

    for hh in heads:
        s_sc[0, hh] = scores(0, 0, hh)

    def q_body(qi, carry):
        row0 = pl.multiple_of(qi * ATT_T, ATT_T)
        slot = qi % 2
        state = []
        for hh in heads:
            state += [jnp.full((1, ATT_T), NEG_INF, F32), jnp.zeros((1, ATT_T), F32),
                      jnp.zeros((V_DIM, ATT_T), F32)]

        def kv_body(j, st):
            out = []
            s_next = scores(qi, j + 1, 0)
            for hh in heads:
                s_cur = s_sc[slot, hh]
                s_after = scores(qi, j + 1, hh + 1) if hh + 1 < ATT_HEADS else None
                out += update(s_cur, vt_ref[0, j, vsl[hh], :], *st[3 * hh:3 * hh + 3])
                s_sc[slot, hh] = s_next
                s_next = s_after
            return tuple(out)

        state = lax.fori_loop(0, qi, kv_body, tuple(state))
        q_next = jnp.minimum(qi + 1, nq - 1)
        sms = [lax.dot_general(km_ref[:, hsl[hh]], q_ref[0, pl.ds(row0, ATT_T), hsl[hh]], nt,
                               preferred_element_type=F32) for hh in heads]
        outs = []
        for hh in heads:
            m, l, acc = state[3 * hh:3 * hh + 3]
            s = jnp.where(diag_visible, s_sc[slot, hh], NEG_INF)
            sm = sms[hh]
            m_new = jnp.maximum(m, jnp.maximum(jnp.max(s, axis=0, keepdims=True),
                                               jnp.max(sm, axis=0, keepdims=True)))
            alpha = jnp.exp2(m - m_new)
            p = jnp.exp2(s - m_new)
            pm = jnp.exp2(sm - m_new)
            l = alpha * l + jnp.sum(p, axis=0, keepdims=True) + jnp.sum(pm, axis=0, keepdims=True)
            acc = (alpha * acc + jnp.dot(vt_ref[0, qi, vsl[hh], :], p.astype(BF16), preferred_element_type=F32)
                   + jnp.dot(vtm_ref[0, 0, vsl[hh], :], pm.astype(BF16), preferred_element_type=F32))
            outs.append(acc / l)
        for hh in heads:
            s_sc[1 - slot, hh] = scores(q_next, 0, hh)
        cat = jnp.concatenate(outs, axis=0)
        cat = cat * lax.rsqrt(jnp.mean(cat * cat, axis=0, keepdims=True) + EPS)
        o_ref[0, pl.ds(row0, ATT_T), :] = (cat.T * g_ref[...]).astype(BF16)
        return carry

    lax.fori_loop(0, nq, q_body, 0)


def _attn_call(q, k, vt, k_meta, vt_meta, out_g):
    b, seq, _ = q.shape
    nk = seq // ATT_T
    hb = ATT_HEADS
    assert hb == N_HEADS
    return pl.pallas_call(
        _attn_kernel,
        grid=(b, N_HEADS // hb),
        in_specs=[pl.BlockSpec((1, seq, hb * HEAD_PAD), lambda i, j: (i, 0, j)),
                  pl.BlockSpec((1, seq, hb * HEAD_PAD), lambda i, j: (i, 0, j)),
                  pl.BlockSpec((1, nk, hb * V_DIM, ATT_T), lambda i, j: (i, 0, j, 0)),
                  pl.BlockSpec((N_META, hb * HEAD_PAD), lambda i, j: (0, j)),
                  pl.BlockSpec((1, 1, hb * V_DIM, N_META), lambda i, j: (0, 0, j, 0)),
                  pl.BlockSpec((1, D_ATTN), lambda i, j: (0, 0))],
        out_specs=pl.BlockSpec((1, seq, hb * V_DIM), lambda i, j: (i, 0, j)),
        out_shape=jax.ShapeDtypeStruct((b, seq, D_ATTN), BF16),
        scratch_shapes=[pltpu.VMEM((2, hb, ATT_T, ATT_T), F32)],
        compiler_params=pltpu.CompilerParams(dimension_semantics=("arbitrary", "arbitrary"),
                                             vmem_limit_bytes=VMEM_LIMIT),
        name="attn",
    )(q, k, vt, k_meta, vt_meta, out_g)


def _mix_kernel(x_ref, ig_ref, ib_ref, u_ref, uh_ref, pre_ref, cw_ref, cb_ref, clg_ref, clb_ref,
                cog_ref, o_ref, wo_ref, l1g_ref, l1b_ref, wrh_ref, br_ref, tri_ref,
                h1_ref, hp_ref, ri_ref, rw_ref, cnt_ref, win_sc, sh_sc, carry_sc):
    bi = pl.program_id(0)
    ti = pl.program_id(1)
    tm = MIX_TM

    @pl.when((bi == 0) & (ti == 0))
    def _():
        carry_sc[...] = jnp.zeros_like(carry_sc)

    win_sc[0:HALO, :] = jnp.where(ti == 0, pre_ref[...], uh_ref[0])
    win_sc[HALO:, :] = u_ref[0]
    first = HALO - (CONV_K - 1)
    for res in range(1, SUBLANES):
        rows = tm + HALO - SUBLANES
        sh_sc[res - 1, 0:rows, :] = win_sc[pl.ds(res, rows), :]

    acc = jnp.broadcast_to(cb_ref[...], (tm, C_CONV))
    for o in range(first, first + CONV_K):
        res, lo = o % SUBLANES, o - o % SUBLANES
        src = sh_sc[res - 1, lo:lo + tm, :] if res else win_sc[lo:lo + tm, :]
        acc = acc + src * cw_ref[o - first:o - first + 1, :]
    c = _ln(acc, clg_ref[...], clb_ref[...])
    c = c * _sigmoid(c)
    cn = _rms(c, cog_ref[...])
    y = (jnp.dot(cn.astype(BF16), wo_ref[0:C_CONV, :], preferred_element_type=F32)
         + jnp.dot(o_ref[0], wo_ref[C_CONV:, :], preferred_element_type=F32))
    h = _ln(x_ref[0], ig_ref[...], ib_ref[...])
    h1 = _ln(DEEPNORM_ALPHA * h + y, l1g_ref[...], l1b_ref[...])
    h1_ref[0] = h1
    h1_bf = h1.astype(BF16)
    hp_ref[0] = _pack_bf16_pair(h1_bf[:, :HALF_D], h1_bf[:, HALF_D:])

    nt = (((1,), (1,)), ((), ()))
    lt = lax.dot_general(wrh_ref[...], h1_bf, nt, preferred_element_type=F32) + br_ref[:, 0:1]
    row = lax.broadcasted_iota(jnp.int32, (ROUTE_ROWS, tm), 0)
    big = jnp.int32(1 << 20)
    is_g = (row >= N_EXPERTS) & (row < N_EXPERTS + N_GROUPS)
    gl = jnp.where(is_g, lt, -jnp.inf)
    gmax = jnp.max(gl, axis=0, keepdims=True)
    g_idx = jnp.min(jnp.where(gl == gmax, row - N_EXPERTS, big), axis=0, keepdims=True)
    g_w = 1.0 / jnp.sum(jnp.exp(gl - gmax), axis=0, keepdims=True)
    in_grp = (row < N_EXPERTS) & ((row // EXPERTS_PER_GROUP) == g_idx)
    el = jnp.where(in_grp, lt, -jnp.inf)
    v1 = jnp.max(el, axis=0, keepdims=True)
    i1 = jnp.min(jnp.where(el == v1, row, big), axis=0, keepdims=True)
    el2 = jnp.where(row == i1, -jnp.inf, el)
    v2 = jnp.max(el2, axis=0, keepdims=True)
    i2 = jnp.min(jnp.where(el2 == v2, row, big), axis=0, keepdims=True)
    e2 = jnp.exp(v2 - v1)
    w1 = g_w / (1.0 + e2)
    w2 = g_w * e2 / (1.0 + e2)

    oh1 = row == i1
    oh2 = row == i2
    oh = jnp.where(oh1 | oh2, 1.0, 0.0)
    before = jnp.dot(oh.astype(BF16), tri_ref[...], preferred_element_type=F32) + carry_sc[:, 0:1]
    r1 = jnp.sum(jnp.where(oh1, before, 0.0), axis=0, keepdims=True).astype(jnp.int32)
    r2 = jnp.sum(jnp.where(oh2, before, 0.0), axis=0, keepdims=True).astype(jnp.int32)
    carry_sc[...] = carry_sc[...] + jnp.sum(oh, axis=1, keepdims=True)
    cnt_ref[...] = carry_sc[...]

    sub = lax.broadcasted_iota(jnp.int32, (SUBLANES, tm), 0)
    ri_ref[...] = jnp.where(sub == 0, i1, jnp.where(sub == 1, i2, jnp.where(sub == 2, r1,
                            jnp.where(sub == 3, r2, 0))))
    lrow = lax.broadcasted_iota(jnp.int32, (LANES, tm), 0)
    rw_ref[0] = jnp.where(lrow == 0, w1, jnp.where(lrow == 1, w2, 0.0)).T[:, :8]


def _mix_call(x, ig, ib, u, pre, cw, cb, clg, clb, cog, o, wo, l1g, l1b, wrh, br):
    b, seq, _ = x.shape
    tm = MIX_TM
    nt = seq // tm
    full = lambda shape: pl.BlockSpec(shape, lambda i, j: (0,) * len(shape))
    tile = lambda w: pl.BlockSpec((1, tm, w), lambda i, j: (i, j, 0))
    halo = pl.BlockSpec((1, HALO, C_CONV), lambda i, j: (i, jnp.maximum(j * (tm // HALO) - 1, 0), 0))
    earlier = jnp.asarray(np.triu(np.ones((tm, tm), np.float32), k=1), BF16)
    return pl.pallas_call(
        _mix_kernel,
        grid=(b, nt),
        in_specs=[tile(D_MODEL), full((1, D_MODEL)), full((1, D_MODEL)),
                  tile(C_CONV), halo, full((HALO, C_CONV)), full((HALO, C_CONV)), full((1, C_CONV)),
                  full((1, C_CONV)), full((1, C_CONV)), full((1, C_CONV)),
                  tile(D_ATTN), full((D_MODEL, D_MODEL)),
                  full((1, D_MODEL)), full((1, D_MODEL)), full((ROUTE_ROWS, D_MODEL)),
                  full((ROUTE_ROWS, LANES)), full((tm, tm))],
        out_specs=[tile(D_MODEL), tile(HALF_D), pl.BlockSpec((8, tm), lambda i, j: (0, i * nt + j)), tile(8),
                   full((ROUTE_ROWS, LANES))],
        out_shape=[jax.ShapeDtypeStruct((b, seq, D_MODEL), F32),
                   jax.ShapeDtypeStruct((b, seq, HALF_D), jnp.uint32),
                   jax.ShapeDtypeStruct((8, b * seq), jnp.int32),
                   jax.ShapeDtypeStruct((b, seq, 8), F32),
                   jax.ShapeDtypeStruct((ROUTE_ROWS, LANES), F32)],
        scratch_shapes=[pltpu.VMEM((HALO + tm, C_CONV), F32),
                        pltpu.VMEM((SUBLANES - 1, HALO + tm, C_CONV), F32),
                        pltpu.VMEM((ROUTE_ROWS, LANES), F32)],
        compiler_params=pltpu.CompilerParams(dimension_semantics=("arbitrary", "arbitrary"),
                                             vmem_limit_bytes=VMEM_LIMIT),
        name="mix",
    )(x, ig, ib, u, u, pre, cw, cb, clg, clb, cog, o, wo, l1g, l1b, wrh, br, earlier)


def _tables_kernel(ps_ref, pe_ref, cn_ref, ri_ref, dest_ref, blk_ref):
    ri = ri_ref[...]
    start = jnp.zeros_like(ri)
    for e in range(N_EXPERTS):
        start = jnp.where(ri == e, ps_ref[e], start)
    dest_ref[...] = jnp.zeros_like(dest_ref)
    dest_ref[0:2, :] = start[0:2, :] + ri_ref[2:4, :]
    blk_start = lax.broadcasted_iota(jnp.int32, blk_ref.shape, 1) * MOE_BM
    be = jnp.zeros(blk_ref.shape, jnp.int32)
    for e in range(N_EXPERTS):
        be = be + jnp.where(pe_ref[e] <= blk_start, 1, 0)
    be = jnp.minimum(be, N_EXPERTS - 1)
    end = jnp.zeros(blk_ref.shape, jnp.int32)
    for e in range(N_EXPERTS):
        end = jnp.where(be == e, ps_ref[e] + cn_ref[e], end)
    valid = jnp.clip(end - blk_start, 0, MOE_BM)
    row = lax.broadcasted_iota(jnp.int32, blk_ref.shape, 0)
    blk_ref[...] = jnp.where(row == 0, be, valid)


def _tables_call(pad_start, pad_end, counts, ri, n_blocks):
    t = ri.shape[1]
    blk_w = -(-n_blocks // LANES) * LANES
    return pl.pallas_call(
        _tables_kernel,
        grid_spec=pltpu.PrefetchScalarGridSpec(
            num_scalar_prefetch=3,
            grid=(1,),
            in_specs=[pl.BlockSpec((SUBLANES, t), lambda i, *_: (0, 0))],
            out_specs=[pl.BlockSpec((SUBLANES, t), lambda i, *_: (0, 0)),
                       pl.BlockSpec((SUBLANES, blk_w), lambda i, *_: (0, 0))]),
        out_shape=[jax.ShapeDtypeStruct((SUBLANES, t), jnp.int32),
                   jax.ShapeDtypeStruct((SUBLANES, blk_w), jnp.int32)],
        compiler_params=pltpu.CompilerParams(dimension_semantics=("arbitrary",),
                                             vmem_limit_bytes=VMEM_LIMIT),
        name="tables",
    )(pad_start, pad_end, counts, ri)


def _expert_kernel(be_ref, nv_ref, x_ref, wg_ref, wu_ref, wd_ref, y_ref, wg_sc, wu_sc, wd_sc):
    i = pl.program_id(0)
    nv = nv_ref[i]

    @pl.when((i == 0) | (be_ref[i] != be_ref[jnp.maximum(i - 1, 0)]))
    def _():
        wg_sc[...] = wg_ref[0, 0].astype(BF16)
        wu_sc[...] = wu_ref[0, 0].astype(BF16)
        wd_sc[...] = wd_ref[0, 0].astype(BF16)

    @pl.when(nv > 0)
    def _():
        rowid = lax.broadcasted_iota(jnp.int32, x_ref.shape, 0)
        x_lo, x_hi = _unpack_bf16_pair(jnp.where(rowid < nv, x_ref[...], jnp.uint32(0)))
        x_lo = x_lo.astype(BF16)
        x_hi = x_hi.astype(BF16)
        g = (jnp.dot(x_lo, wg_sc[:HALF_D, :], preferred_element_type=F32)
             + jnp.dot(x_hi, wg_sc[HALF_D:, :], preferred_element_type=F32))
        up = (jnp.dot(x_lo, wu_sc[:HALF_D, :], preferred_element_type=F32)
              + jnp.dot(x_hi, wu_sc[HALF_D:, :], preferred_element_type=F32))
        act = g * _sigmoid(g) * up
        y = jnp.dot(act.astype(BF16), wd_sc[...], preferred_element_type=F32)
        y_ref[...] = _pack_bf16_pair(y[:, :HALF_D], y[:, HALF_D:])

    @pl.when(nv == 0)
    def _():
        y_ref[...] = jnp.zeros_like(y_ref)


def _expert_call(blk_e, blk_valid, xb, w_gate, w_up, w_down):
    r = xb.shape[0]
    bm = MOE_BM
    wspec = lambda a, b: pl.BlockSpec((1, 1, a, b), lambda i, be, nv: (0, be[i], 0, 0))
    return pl.pallas_call(
        _expert_kernel,
        grid_spec=pltpu.PrefetchScalarGridSpec(
            num_scalar_prefetch=2,
            grid=(r // bm,),
            in_specs=[pl.BlockSpec((bm, HALF_D), lambda i, be, nv: (i, 0)),
                      wspec(D_MODEL, D_EXPERT), wspec(D_MODEL, D_EXPERT), wspec(D_EXPERT, D_MODEL)],
            out_specs=pl.BlockSpec((bm, HALF_D), lambda i, be, nv: (i, 0)),
            scratch_shapes=[pltpu.VMEM((D_MODEL, D_EXPERT), BF16), pltpu.VMEM((D_MODEL, D_EXPERT), BF16),
                            pltpu.VMEM((D_EXPERT, D_MODEL), BF16)]),
        out_shape=jax.ShapeDtypeStruct((r, HALF_D), jnp.uint32),
        compiler_params=pltpu.CompilerParams(dimension_semantics=("arbitrary",),
                                             vmem_limit_bytes=VMEM_LIMIT),
        name="experts",
    )(blk_e, blk_valid, xb, w_gate, w_up, w_down)


def _sc_gather_rows(table, idx):
    m = idx.shape[0]
    width = table.shape[1]
    sc = plsc.get_sparse_core_info()
    workers = sc.num_cores * sc.num_subcores
    per_worker = m // workers
    mesh = plsc.VectorSubcoreMesh(core_axis_name="c", subcore_axis_name="s")

    n_win = per_worker // SC_WINDOW
    assert n_win % 2 == 0 and n_win >= 2
    buf = lambda: [pltpu.VMEM((SC_WINDOW, width), table.dtype), pltpu.SemaphoreType.DMA]

    @functools.partial(pl.kernel, mesh=mesh, out_type=jax.ShapeDtypeStruct((m, width), table.dtype),
                       scratch_types=[pltpu.VMEM((per_worker,), jnp.int32)] + buf() + buf())
    def gather(table_hbm, idx_hbm, out_hbm, idx_v, rows_a, sem_a, rows_b, sem_b):
        base = (lax.axis_index("s") * sc.num_cores + lax.axis_index("c")) * per_worker
        bufs = ((rows_a, sem_a), (rows_b, sem_b))
        pltpu.sync_copy(idx_hbm.at[pl.ds(base, per_worker)], idx_v)

        def window_copy(win, which):
            rows_v, sem = bufs[which]
            return pltpu.make_async_copy(table_hbm.at[idx_v.at[pl.ds(win * SC_WINDOW, SC_WINDOW)]], rows_v, sem)

        def fetch(win, which):
            window_copy(win, which).start()

        def flush(win, which):
            window_copy(win, which).wait()
            pltpu.sync_copy(bufs[which][0], out_hbm.at[pl.ds(base + win * SC_WINDOW, SC_WINDOW)])

        fetch(0, 0)

        @pl.loop(0, n_win // 2 - 1)
        def _(pair):
            win = 2 * pair
            fetch(win + 1, 1)
            flush(win, 0)
            fetch(win + 2, 0)
            flush(win + 1, 1)

        fetch(n_win - 1, 1)
        flush(n_win - 2, 0)
        flush(n_win - 1, 1)

    return gather(table, idx)


def _sc_scatter_rows(rows, idx0, idx1, n_out):
    m, width = rows.shape
    sc = plsc.get_sparse_core_info()
    workers = sc.num_cores * sc.num_subcores
    per_worker = m // workers
    mesh = plsc.VectorSubcoreMesh(core_axis_name="c", subcore_axis_name="s")

    n_win = per_worker // SC_WINDOW
    assert n_win % 2 == 0 and n_win >= 2
    buf = lambda: [pltpu.VMEM((SC_WINDOW, width), rows.dtype),
                   pltpu.SemaphoreType.DMA, pltpu.SemaphoreType.DMA, pltpu.SemaphoreType.DMA]

    @functools.partial(pl.kernel, mesh=mesh, out_type=jax.ShapeDtypeStruct((n_out, width), rows.dtype),
                       scratch_types=[pltpu.VMEM((per_worker,), jnp.int32), pltpu.VMEM((per_worker,), jnp.int32)]
                       + buf() + buf())
    def scatter(rows_hbm, idx0_hbm, idx1_hbm, out_hbm, idx0_v, idx1_v, *scratch):
        base = (lax.axis_index("s") * sc.num_cores + lax.axis_index("c")) * per_worker
        bufs = (scratch[:4], scratch[4:])
        pltpu.sync_copy(idx0_hbm.at[pl.ds(base, per_worker)], idx0_v)
        pltpu.sync_copy(idx1_hbm.at[pl.ds(base, per_worker)], idx1_v)

        def load(win, which):
            rows_v, lsem, _, _ = bufs[which]
            pltpu.async_copy(rows_hbm.at[pl.ds(base + win * SC_WINDOW, SC_WINDOW)], rows_v, lsem)

        def send(win, which):
            rows_v, lsem, sem0, sem1 = bufs[which]
            here = pl.ds(win * SC_WINDOW, SC_WINDOW)
            pltpu.make_async_copy(rows_hbm.at[pl.ds(0, SC_WINDOW)], rows_v, lsem).wait()
            first = pltpu.async_copy(rows_v, out_hbm.at[idx0_v.at[here]], sem0)
            second = pltpu.async_copy(rows_v, out_hbm.at[idx1_v.at[here]], sem1)
            first.wait()
            second.wait()

        load(0, 0)

        @pl.loop(0, n_win // 2 - 1)
        def _(pair):
            win = 2 * pair
            load(win + 1, 1)
            send(win, 0)
            load(win + 2, 0)
            send(win + 1, 1)

        load(n_win - 1, 1)
        send(n_win - 2, 0)
        send(n_win - 1, 1)

    return scatter(rows, idx0, idx1)


def _combine_rows_kernel(h_ref, w_ref, g_ref, b_ref, y0_ref, y1_ref, o_ref):
    w = w_ref[...]
    y0_lo, y0_hi = _unpack_bf16_pair(y0_ref[...])
    y1_lo, y1_hi = _unpack_bf16_pair(y1_ref[...])
    y = jnp.concatenate([w[:, 0:1] * y0_lo + w[:, 1:2] * y1_lo, w[:, 0:1] * y0_hi + w[:, 1:2] * y1_hi], axis=1)
    o_ref[...] = _ln(DEEPNORM_ALPHA * h_ref[...] + y, g_ref[...], b_ref[...])


def _combine_rows_call(h1, rw, g, b, yg):
    t = h1.shape[0]
    tm = COMB_TM
    n = t // tm
    return pl.pallas_call(
        _combine_rows_kernel,
        grid=(n,),
        in_specs=[pl.BlockSpec((tm, D_MODEL), lambda i: (i, 0)),
                  pl.BlockSpec((tm, 8), lambda i: (i, 0)),
                  pl.BlockSpec((1, D_MODEL), lambda i: (0, 0)),
                  pl.BlockSpec((1, D_MODEL), lambda i: (0, 0)),
                  pl.BlockSpec((tm, HALF_D), lambda i: (i, 0)),
                  pl.BlockSpec((tm, HALF_D), lambda i: (n + i, 0))],
        out_specs=pl.BlockSpec((tm, D_MODEL), lambda i: (i, 0)),
        out_shape=jax.ShapeDtypeStruct((t, D_MODEL), F32),
        compiler_params=pltpu.CompilerParams(dimension_semantics=("arbitrary",),
                                             vmem_limit_bytes=VMEM_LIMIT),
        name="combine_rows",
    )(h1, rw, g, b, yg, yg)


def _take_cols(w, idx):
    wz = jnp.concatenate([w, jnp.zeros((w.shape[0], 1), w.dtype)], axis=1)
    return wz[:, np.where(idx < 0, w.shape[1], idx)]


def _layout_indices():
    lane_src = np.full((HEAD_PAD,), -1, np.int64)
    lane_src[0:ROPE_HALF] = QK_NOPE + np.arange(ROPE_HALF)
    lane_src[ROPE_HALF:X2_LANE] = np.arange(X2_LANE - ROPE_HALF)
    lane_src[X2_LANE:X2_LANE + ROPE_HALF] = QK_NOPE + ROPE_HALF + np.arange(ROPE_HALF)
    n_lo = X2_LANE - ROPE_HALF
    lane_src[X2_LANE + ROPE_HALF:X2_LANE + ROPE_HALF + QK_NOPE - n_lo] = n_lo + np.arange(QK_NOPE - n_lo)
    q_idx = np.concatenate([np.where(lane_src >= 0, lane_src + (QK_NOPE + QK_ROPE) * h, -1)
                            for h in range(N_HEADS)])
    nope_src = np.where(lane_src < QK_NOPE, lane_src, -1)
    k_idx = np.concatenate([np.where(nope_src >= 0, nope_src + (QK_NOPE + V_DIM) * h, -1)
                            for h in range(N_HEADS)])
    v_idx = np.concatenate([QK_NOPE + np.arange(V_DIM) + (QK_NOPE + V_DIM) * h for h in range(N_HEADS)])
    kr_idx = np.full((LANES,), -1, np.int64)
    kr_idx[0:ROPE_HALF] = np.arange(ROPE_HALF)
    kr_idx[X2_LANE:X2_LANE + ROPE_HALF] = ROPE_HALF + np.arange(ROPE_HALF)
    return q_idx, k_idx, v_idx, kr_idx


def _rope_lane_tables(length):
    inv_freq = np.float32(ROPE_THETA) ** (-np.arange(ROPE_HALF, dtype=np.float32) / np.float32(ROPE_HALF))
    ang = np.arange(length, dtype=np.float32)[:, None] * inv_freq[None, :].astype(np.float32)
    cos, sin = np.cos(ang).astype(np.float32), np.sin(ang).astype(np.float32)
    ctab = np.ones((length, LANES), np.float32)
    stab = np.zeros((length, LANES), np.float32)
    for lo, sign in ((0, -1.0), (X2_LANE, 1.0)):
        ctab[:, lo:lo + ROPE_HALF] = cos
        stab[:, lo:lo + ROPE_HALF] = sign * sin
    return jnp.asarray(ctab), jnp.asarray(stab)


def kernel(x, meta_tokens, ln_in_g, ln_in_b, w_in, conv_w, conv_b, conv_ln_g, conv_ln_b, q_norm_g, w_uq,
           kv_norm_g, w_ukv, conv_out_g, attn_out_g, w_o, ln1_g, ln1_b, w_rg, b_rg, w_re, b_re,
           w_gate, w_up, w_down, ln2_g, ln2_b):
    assert w_in.shape[0] == 1, "single layer"
    bsz, seq, d = x.shape
    t = bsz * seq
    row = lambda a: a.reshape(1, -1)

    q_idx, k_idx, v_idx, kr_idx = _layout_indices()
    c4 = 2 * C_CONV + Q_LORA + KV_LORA
    w_in_l = jnp.concatenate([w_in[0][:, :c4], _take_cols(w_in[0][:, c4:], kr_idx)], axis=1).astype(BF16)
    wuq_l = _take_cols(w_uq[0], q_idx).astype(BF16)
    wuk_l = _take_cols(w_ukv[0], k_idx).astype(BF16)
    wvt_l = w_ukv[0][:, v_idx].T.astype(BF16)
    wo_l = w_o[0].astype(BF16)
    n_route_pad = ROUTE_ROWS - N_EXPERTS - N_GROUPS
    wr_t = jnp.concatenate([w_re[0].T, w_rg[0].T, jnp.zeros((n_route_pad, d), F32)], axis=0)
    wrh_l = wr_t.astype(BF16)
    br_l = jnp.broadcast_to(jnp.concatenate([b_re[0], b_rg[0], jnp.zeros((n_route_pad,), F32)])[:, None],
                            (ROUTE_ROWS, LANES))
    ctab, stab = _rope_lane_tables(N_META + seq)

    proj_w = (row(ln_in_g), row(ln_in_b), w_in_l, row(q_norm_g[0]), wuq_l, row(kv_norm_g[0]), wuk_l, wvt_l)
    u_m, _, k_m, vt_m = _proj_call(meta_tokens.astype(F32), N_META, 1, *proj_w, ctab[:N_META], stab[:N_META])
    u, q, k, vt = _proj_call(x.reshape(t, d), PROJ_TM, seq // PROJ_TM, *proj_w, ctab[N_META:], stab[N_META:])

    o = _attn_call(q.reshape(bsz, seq, -1), k.reshape(bsz, seq, -1),
                   vt.reshape(bsz, seq // ATT_T, D_ATTN, ATT_T), k_m, vt_m, row(attn_out_g[0]))

    pre = jnp.concatenate([jnp.zeros((HALO - N_META, C_CONV), F32), u_m], axis=0)
    cw = jnp.concatenate([conv_w[0], jnp.zeros((HALO - CONV_K, C_CONV), F32)], axis=0)
    h1, hp, ri, rw, cnt = _mix_call(x, row(ln_in_g), row(ln_in_b), u.reshape(bsz, seq, C_CONV), pre, cw,
                                    row(conv_b[0]), row(conv_ln_g[0]), row(conv_ln_b[0]), row(conv_out_g[0]),
                                    o, wo_l, row(ln1_g[0]), row(ln1_b[0]), wrh_l, br_l)

    counts = cnt[:N_EXPERTS, 0].astype(jnp.int32)
    padded = (counts + MOE_BM - 1) // MOE_BM * MOE_BM
    pad_end = jnp.cumsum(padded)
    pad_start = pad_end - padded
    n_blocks = (2 * t) // MOE_BM + N_EXPERTS
    dest, blk = _tables_call(pad_start, pad_end, counts, ri, n_blocks)
    blk_e, blk_valid = blk[0, :n_blocks], blk[1, :n_blocks]

    h1f = h1.reshape(t, d)
    dest0, dest1 = dest[0], dest[1]
    xb = _sc_scatter_rows(hp.reshape(t, HALF_D), dest0, dest1, n_blocks * MOE_BM)
    yb = _expert_call(blk_e, blk_valid, xb, w_gate, w_up, w_down)
    yg = _sc_gather_rows(yb, jnp.concatenate([dest0, dest1]))
    out = _combine_rows_call(h1f, rw.reshape(t, 8), row(ln2_g[0]), row(ln2_b[0]), yg)
    return out.reshape(bsz, seq, d)
```

```python
import functools
import math

import numpy as np
import jax
import jax.numpy as jnp
from jax import lax
from jax.experimental import pallas as pl
from jax.experimental.pallas import tpu as pltpu
from jax.experimental.pallas import tpu_sc as plsc

F32 = jnp.float32
BF16 = jnp.bfloat16

D_MODEL = 1024
HALF_D = D_MODEL // 2
N_META = 16
CHUNK = 64
C_CONV = 512
CONV_K = 31
N_HEADS = 8
QK_NOPE = 64
QK_ROPE = 32
V_DIM = 64
Q_LORA = 384
KV_LORA = 256
ROPE_THETA = 10000.0
D_ATTN = N_HEADS * V_DIM
N_GROUPS = 4
EXPERTS_PER_GROUP = 8
N_EXPERTS = N_GROUPS * EXPERTS_PER_GROUP
D_EXPERT = 256
DEEPNORM_ALPHA = 2.0 ** 0.25
EPS = 1e-5
NEG_INF = -1e30

LANES = 128
SUBLANES = 8
HEAD_PAD = LANES
ROPE_HALF = QK_ROPE // 2
X2_LANE = 64
Z_COLS = 2 * C_CONV + Q_LORA + KV_LORA + LANES
Q_SCALE = (QK_NOPE + QK_ROPE) ** -0.5 * math.log2(math.e)

PROJ_TM = 1024
ATT_T = 256
ATT_HEADS = 8
MIX_TM = 512
ROUTE_ROWS = 48
HALO = 32
MOE_BM = 1024
COMB_TM = 1024
SC_WINDOW = 64
VMEM_LIMIT = 56 * 1024 * 1024


def _ln(x, g, b):
    mu = jnp.mean(x, axis=-1, keepdims=True)
    xc = x - mu
    var = jnp.mean(xc * xc, axis=-1, keepdims=True)
    return xc * lax.rsqrt(var + EPS) * g + b


def _rms(x, g):
    return x * lax.rsqrt(jnp.mean(x * x, axis=-1, keepdims=True) + EPS) * g


def _sigmoid(x):
    return 0.5 * jnp.tanh(0.5 * x) + 0.5


def _pack_bf16_pair(lo, hi):
    lo_bits = lax.bitcast_convert_type(lo.astype(BF16).astype(F32), jnp.uint32)
    hi_bits = lax.bitcast_convert_type(hi.astype(BF16).astype(F32), jnp.uint32)
    return (lo_bits >> 16) | (hi_bits & jnp.uint32(0xFFFF0000))


def _unpack_bf16_pair(packed):
    lo = lax.bitcast_convert_type(packed << 16, F32)
    hi = lax.bitcast_convert_type(packed & jnp.uint32(0xFFFF0000), F32)
    return lo, hi


def _proj_kernel(x_ref, g_ref, b_ref, win_ref, qg_ref, wuq_ref, kvg_ref, wuk_ref, wvt_ref, c_ref, s_ref,
                 u_ref, q_ref, k_ref, vt_ref):
    n_part, part = vt_ref.shape[1], vt_ref.shape[3]
    zs = []
    for r in range(n_part):
        h = _ln(x_ref[r * part:(r + 1) * part, :], g_ref[...], b_ref[...])
        zs.append(jnp.dot(h.astype(BF16), win_ref[...], preferred_element_type=F32))
    c0 = 2 * C_CONV
    for r, z in enumerate(zs):
        rows = slice(r * part, (r + 1) * part)
        u_ref[rows, :] = z[:, :C_CONV] * _sigmoid(z[:, C_CONV:c0])
        cq = z[:, c0:c0 + Q_LORA]
        ckv = z[:, c0 + Q_LORA:c0 + Q_LORA + KV_LORA]
        kr = z[:, c0 + Q_LORA + KV_LORA:]
        cs = c_ref[rows, :]
        sn = s_ref[rows, :]
        q = jnp.dot(_rms(cq, qg_ref[...]).astype(BF16), wuq_ref[...], preferred_element_type=F32)
        kvn = _rms(ckv, kvg_ref[...]).astype(BF16)
        kn = jnp.dot(kvn, wuk_ref[...], preferred_element_type=F32)
        kr_rot = kr * cs + pltpu.roll(kr, X2_LANE, 1) * sn
        for hh in range(N_HEADS):
            sl = slice(HEAD_PAD * hh, HEAD_PAD * (hh + 1))
            qh = q[:, sl]
            qh = (qh * cs + pltpu.roll(qh, X2_LANE, 1) * sn) * Q_SCALE
            q_ref[rows, sl] = qh.astype(BF16)
            k_ref[rows, sl] = (kn[:, sl] + kr_rot).astype(BF16)
        vt = lax.dot_general(wvt_ref[...], kvn, (((1,), (1,)), ((), ())), preferred_element_type=F32)
        vt_ref[0, r] = vt.astype(BF16)


def _proj_call(x2d, tm, pos_blocks, ln_g, ln_b, w_in, qg, wuq, kvg, wuk, wvt, ctab, stab):
    t = x2d.shape[0]
    sub_w = min(tm, ATT_T)
    n_sub = tm // sub_w
    full = lambda shape: pl.BlockSpec(shape, lambda i: (0,) * len(shape))
    row = lambda w: pl.BlockSpec((tm, w), lambda i: (i, 0))
    tab = pl.BlockSpec((tm, LANES), lambda i: (i % pos_blocks, 0))
    return pl.pallas_call(
        _proj_kernel,
        grid=(t // tm,),
        in_specs=[row(D_MODEL), full((1, D_MODEL)), full((1, D_MODEL)), full((D_MODEL, Z_COLS)),
                  full((1, Q_LORA)), full((Q_LORA, N_HEADS * HEAD_PAD)),
                  full((1, KV_LORA)), full((KV_LORA, N_HEADS * HEAD_PAD)), full((D_ATTN, KV_LORA)), tab, tab],
        out_specs=[row(C_CONV), row(N_HEADS * HEAD_PAD), row(N_HEADS * HEAD_PAD),
                   pl.BlockSpec((1, n_sub, D_ATTN, sub_w), lambda i: (i, 0, 0, 0))],
        out_shape=[jax.ShapeDtypeStruct((t, C_CONV), F32),
                   jax.ShapeDtypeStruct((t, N_HEADS * HEAD_PAD), BF16),
                   jax.ShapeDtypeStruct((t, N_HEADS * HEAD_PAD), BF16),
                   jax.ShapeDtypeStruct((t // tm, n_sub, D_ATTN, sub_w), BF16)],
        compiler_params=pltpu.CompilerParams(dimension_semantics=("arbitrary",),
                                             vmem_limit_bytes=VMEM_LIMIT),
        name="proj",
    )(x2d, ln_g, ln_b, w_in, qg, wuq, kvg, wuk, wvt, ctab, stab)


def _attn_kernel(q_ref, k_ref, vt_ref, km_ref, vtm_ref, g_ref, o_ref, s_sc):
    seq = q_ref.shape[1]
    nq = seq // ATT_T
    key_chunk = lax.broadcasted_iota(jnp.int32, (ATT_T, ATT_T), 0) // CHUNK
    qry_chunk = lax.broadcasted_iota(jnp.int32, (ATT_T, ATT_T), 1) // CHUNK
    diag_visible = key_chunk <= qry_chunk
    nt = (((1,), (1,)), ((), ()))
    heads = range(ATT_HEADS)
    hsl = [slice(HEAD_PAD * hh, HEAD_PAD * (hh + 1)) for hh in heads]
    vsl = [slice(V_DIM * hh, V_DIM * (hh + 1)) for hh in heads]

    def update(s, vt, m, l, acc):
        m_new = jnp.maximum(m, jnp.max(s, axis=0, keepdims=True))
        alpha = jnp.exp2(m - m_new)
        p = jnp.exp2(s - m_new)
        vt1 = jnp.concatenate([vt, jnp.ones((2 * SUBLANES, vt.shape[1]), BF16)], axis=0)
        pv = jnp.dot(vt1, p.astype(BF16), preferred_element_type=F32)
        return m_new, alpha * l + pv[V_DIM:V_DIM + 1], alpha * acc + pv[:V_DIM]

    def scores(qi, j, hh):
        q0 = pl.multiple_of(qi * ATT_T, ATT_T)
        k0 = pl.multiple_of(j * ATT_T, ATT_T)
        return lax.dot_general(k_ref[0, pl.ds(k0, ATT_T), hsl[hh]], q_ref[0, pl.ds(q0, ATT_T), hsl[hh]], nt,
                               preferred_element_type=F32)

    for hh in heads:
        s_sc[0, hh] = scores(0, 0, hh)

    def q_body(qi, carry):
        row0 = pl.multiple_of(qi * ATT_T, ATT_T)
        slot = qi % 2
        state = []
        for hh in heads:
            state += [jnp.full((1, ATT_T), NEG_INF, F32), jnp.zeros((1, ATT_T), F32),
                      jnp.zeros((V_DIM, ATT_T), F32)]

        def kv_body(j, st):
            out = []
            s_next = scores(qi, j + 1, 0)
            for hh in heads:
                s_cur = s_sc[slot, hh]
                s_after = scores(qi, j + 1, hh + 1) if hh + 1 < ATT_HEADS else None
                out += update(s_cur, vt_ref[0, j, vsl[hh], :], *st[3 * hh:3 * hh + 3])
                s_sc[slot, hh] = s_next
                s_next = s_after
            return tuple(out)

        state = lax.fori_loop(0, qi, kv_body, tuple(state))
        q_next = jnp.minimum(qi + 1, nq - 1)
        sms = [lax.dot_general(km_ref[:, hsl[hh]], q_ref[0, pl.ds(row0, ATT_T), hsl[hh]], nt,
                               preferred_element_type=F32) for hh in heads]
        outs = []
        for hh in heads:
            m, l, acc = state[3 * hh:3 * hh + 3]
            s = jnp.where(diag_visible, s_sc[slot, hh], NEG_INF)
            sm = sms[hh]
            m_new = jnp.maximum(m, jnp.maximum(jnp.max(s, axis=0, keepdims=True),
                                               jnp.max(sm, axis=0, keepdims=True)))
            alpha = jnp.exp2(m - m_new)
            p = jnp.exp2(s - m_new)
            pm = jnp.exp2(sm - m_new)
            l = alpha * l + jnp.sum(p, axis=0, keepdims=True) + jnp.sum(pm, axis=0, keepdims=True)
            acc = (alpha * acc + jnp.dot(vt_ref[0, qi, vsl[hh], :], p.astype(BF16), preferred_element_type=F32)
                   + jnp.dot(vtm_ref[0, 0, vsl[hh], :], pm.astype(BF16), preferred_element_type=F32))
            outs.append(acc / l)
        for hh in heads:
            s_sc[1 - slot, hh] = scores(q_next, 0, hh)
        cat = jnp.concatenate(outs, axis=0)
        cat = cat * lax.rsqrt(jnp.mean(cat * cat, axis=0, keepdims=True) + EPS)
        o_ref[0, pl.ds(row0, ATT_T), :] = (cat.T * g_ref[...]).astype(BF16)
        return carry

    lax.fori_loop(0, nq, q_body, 0)


def _attn_call(q, k, vt, k_meta, vt_meta, out_g):
    b, seq, _ = q.shape
    nk = seq // ATT_T
    hb = ATT_HEADS
    assert hb == N_HEADS
    return pl.pallas_call(
        _attn_kernel,
        grid=(b, N_HEADS // hb),
        in_specs=[pl.BlockSpec((1, seq, hb * HEAD_PAD), lambda i, j: (i, 0, j)),
                  pl.BlockSpec((1, seq, hb * HEAD_PAD), lambda i, j: (i, 0, j)),
                  pl.BlockSpec((1, nk, hb * V_DIM, ATT_T), lambda i, j: (i, 0, j, 0)),
                  pl.BlockSpec((N_META, hb * HEAD_PAD), lambda i, j: (0, j)),
                  pl.BlockSpec((1, 1, hb * V_DIM, N_META), lambda i, j: (0, 0, j, 0)),
                  pl.BlockSpec((1, D_ATTN), lambda i, j: (0, 0))],
        out_specs=pl.BlockSpec((1, seq, hb * V_DIM), lambda i, j: (i, 0, j)),
        out_shape=jax.ShapeDtypeStruct((b, seq, D_ATTN), BF16),
        scratch_shapes=[pltpu.VMEM((2, hb, ATT_T, ATT_T), F32)],
        compiler_params=pltpu.CompilerParams(dimension_semantics=("arbitrary", "arbitrary"),
                                             vmem_limit_bytes=VMEM_LIMIT),
        name="attn",
    )(q, k, vt, k_meta, vt_meta, out_g)


def _mix_kernel(x_ref, ig_ref, ib_ref, u_ref, uh_ref, pre_ref, cw_ref, cb_ref, clg_ref, clb_ref,
                cog_ref, o_ref, wo_ref, l1g_ref, l1b_ref, wrh_ref, br_ref, tri_ref,
                h1_ref, hp_ref, ri_ref, rw_ref, cnt_ref, win_sc, sh_sc, carry_sc):
    bi = pl.program_id(0)
    ti = pl.program_id(1)
    tm = MIX_TM

    @pl.when((bi == 0) & (ti == 0))
    def _():
        carry_sc[...] = jnp.zeros_like(carry_sc)

    win_sc[0:HALO, :] = jnp.where(ti == 0, pre_ref[...], uh_ref[0])
    win_sc[HALO:, :] = u_ref[0]
    first = HALO - (CONV_K - 1)
    for res in range(1, SUBLANES):
        rows = tm + HALO - SUBLANES
        sh_sc[res - 1, 0:rows, :] = win_sc[pl.ds(res, rows), :]

    acc = jnp.broadcast_to(cb_ref[...], (tm, C_CONV))
    for o in range(first, first + CONV_K):
        res, lo = o % SUBLANES, o - o % SUBLANES
        src = sh_sc[res - 1, lo:lo + tm, :] if res else win_sc[lo:lo + tm, :]
        acc = acc + src * cw_ref[o - first:o - first + 1, :]
    c = _ln(acc, clg_ref[...], clb_ref[...])
    c = c * _sigmoid(c)
    cn = _rms(c, cog_ref[...])
    y = (jnp.dot(cn.astype(BF16), wo_ref[0:C_CONV, :], preferred_element_type=F32)
         + jnp.dot(o_ref[0], wo_ref[C_CONV:, :], preferred_element_type=F32))
    h = _ln(x_ref[0], ig_ref[...], ib_ref[...])
    h1 = _ln(DEEPNORM_ALPHA * h + y, l1g_ref[...], l1b_ref[...])
    h1_ref[0] = h1
    hp_ref[0] = _pack_bf16_pair(h1[:, :HALF_D], h1[:, HALF_D:])

    nt = (((1,), (1,)), ((), ()))
    lt = lax.dot_general(wrh_ref[...], h1.astype(BF16), nt, preferred_element_type=F32) + br_ref[:, 0:1]
    row = lax.broadcasted_iota(jnp.int32, (ROUTE_ROWS, tm), 0)
    big = jnp.int32(1 << 20)
    is_g = (row >= N_EXPERTS) & (row < N_EXPERTS + N_GROUPS)
    gl = jnp.where(is_g, lt, -jnp.inf)
    gmax = jnp.max(gl, axis=0, keepdims=True)
    g_idx = jnp.min(jnp.where(gl == gmax, row - N_EXPERTS, big), axis=0, keepdims=True)
    g_w = 1.0 / jnp.sum(jnp.exp(gl - gmax), axis=0, keepdims=True)
    in_grp = (row < N_EXPERTS) & ((row // EXPERTS_PER_GROUP) == g_idx)
    el = jnp.where(in_grp, lt, -jnp.inf)
    v1 = jnp.max(el, axis=0, keepdims=True)
    i1 = jnp.min(jnp.where(el == v1, row, big), axis=0, keepdims=True)
    el2 = jnp.where(row == i1, -jnp.inf, el)
    v2 = jnp.max(el2, axis=0, keepdims=True)
    i2 = jnp.min(jnp.where(el2 == v2, row, big), axis=0, keepdims=True)
    e2 = jnp.exp(v2 - v1)
    w1 = g_w / (1.0 + e2)
    w2 = g_w * e2 / (1.0 + e2)

    oh1 = row == i1
    oh2 = row == i2
    oh = jnp.where(oh1 | oh2, 1.0, 0.0)
    before = jnp.dot(oh.astype(BF16), tri_ref[...], preferred_element_type=F32) + carry_sc[:, 0:1]
    r1 = jnp.sum(jnp.where(oh1, before, 0.0), axis=0, keepdims=True).astype(jnp.int32)
    r2 = jnp.sum(jnp.where(oh2, before, 0.0), axis=0, keepdims=True).astype(jnp.int32)
    carry_sc[...] = carry_sc[...] + jnp.sum(oh, axis=1, keepdims=True)
    cnt_ref[...] = carry_sc[...]

    sub = lax.broadcasted_iota(jnp.int32, (SUBLANES, tm), 0)
    ri_ref[...] = jnp.where(sub == 0, i1, jnp.where(sub == 1, i2, jnp.where(sub == 2, r1,
                            jnp.where(sub == 3, r2, 0))))
    lrow = lax.broadcasted_iota(jnp.int32, (LANES, tm), 0)
    rw_ref[0] = jnp.where(lrow == 0, w1, jnp.where(lrow == 1, w2, 0.0)).T[:, :8]


def _mix_call(x, ig, ib, u, pre, cw, cb, clg, clb, cog, o, wo, l1g, l1b, wrh, br):
    b, seq, _ = x.shape
    tm = MIX_TM
    nt = seq // tm
    full = lambda shape: pl.BlockSpec(shape, lambda i, j: (0,) * len(shape))
    tile = lambda w: pl.BlockSpec((1, tm, w), lambda i, j: (i, j, 0))
    halo = pl.BlockSpec((1, HALO, C_CONV), lambda i, j: (i, jnp.maximum(j * (tm // HALO) - 1, 0), 0))
    earlier = jnp.asarray(np.triu(np.ones((tm, tm), np.float32), k=1), BF16)
    return pl.pallas_call(
        _mix_kernel,
        grid=(b, nt),
        in_specs=[tile(D_MODEL), full((1, D_MODEL)), full((1, D_MODEL)),
                  tile(C_CONV), halo, full((HALO, C_CONV)), full((HALO, C_CONV)), full((1, C_CONV)),
                  full((1, C_CONV)), full((1, C_CONV)), full((1, C_CONV)),
                  tile(D_ATTN), full((D_MODEL, D_MODEL)),
                  full((1, D_MODEL)), full((1, D_MODEL)), full((ROUTE_ROWS, D_MODEL)),
                  full((ROUTE_ROWS, LANES)), full((tm, tm))],
        out_specs=[tile(D_MODEL), tile(HALF_D), pl.BlockSpec((8, tm), lambda i, j: (0, i * nt + j)), tile(8),
                   full((ROUTE_ROWS, LANES))],
        out_shape=[jax.ShapeDtypeStruct((b, seq, D_MODEL), F32),
                   jax.ShapeDtypeStruct((b, seq, HALF_D), jnp.uint32),
                   jax.ShapeDtypeStruct((8, b * seq), jnp.int32),
                   jax.ShapeDtypeStruct((b, seq, 8), F32),
                   jax.ShapeDtypeStruct((ROUTE_ROWS, LANES), F32)],
        scratch_shapes=[pltpu.VMEM((HALO + tm, C_CONV), F32),
                        pltpu.VMEM((SUBLANES - 1, HALO + tm, C_CONV), F32),
                        pltpu.VMEM((ROUTE_ROWS, LANES), F32)],
        compiler_params=pltpu.CompilerParams(dimension_semantics=("arbitrary", "arbitrary"),
                                             vmem_limit_bytes=VMEM_LIMIT),
        name="mix",
    )(x, ig, ib, u, u, pre, cw, cb, clg, clb, cog, o, wo, l1g, l1b, wrh, br, earlier)


def _tables_kernel(ps_ref, pe_ref, cn_ref, ri_ref, dest_ref, blk_ref):
    ri = ri_ref[...]
    start = jnp.zeros_like(ri)
    for e in range(N_EXPERTS):
        start = jnp.where(ri == e, ps_ref[e], start)
    dest_ref[...] = jnp.zeros_like(dest_ref)
    dest_ref[0:2, :] = start[0:2, :] + ri_ref[2:4, :]
    blk_start = lax.broadcasted_iota(jnp.int32, blk_ref.shape, 1) * MOE_BM
    be = jnp.zeros(blk_ref.shape, jnp.int32)
    for e in range(N_EXPERTS):
        be = be + jnp.where(pe_ref[e] <= blk_start, 1, 0)
    be = jnp.minimum(be, N_EXPERTS - 1)
    end = jnp.zeros(blk_ref.shape, jnp.int32)
    for e in range(N_EXPERTS):
        end = jnp.where(be == e, ps_ref[e] + cn_ref[e], end)
    valid = jnp.clip(end - blk_start, 0, MOE_BM)
    row = lax.broadcasted_iota(jnp.int32, blk_ref.shape, 0)
    blk_ref[...] = jnp.where(row == 0, be, valid)


def _tables_call(pad_start, pad_end, counts, ri, n_blocks):
    t = ri.shape[1]
    blk_w = -(-n_blocks // LANES) * LANES
    return pl.pallas_call(
        _tables_kernel,
        grid_spec=pltpu.PrefetchScalarGridSpec(
            num_scalar_prefetch=3,
            grid=(1,),
            in_specs=[pl.BlockSpec((SUBLANES, t), lambda i, *_: (0, 0))],
            out_specs=[pl.BlockSpec((SUBLANES, t), lambda i, *_: (0, 0)),
                       pl.BlockSpec((SUBLANES, blk_w), lambda i, *_: (0, 0))]),
        out_shape=[jax.ShapeDtypeStruct((SUBLANES, t), jnp.int32),
                   jax.ShapeDtypeStruct((SUBLANES, blk_w), jnp.int32)],
        compiler_params=pltpu.CompilerParams(dimension_semantics=("arbitrary",),
                                             vmem_limit_bytes=VMEM_LIMIT),
        name="tables",
    )(pad_start, pad_end, counts, ri)


def _expert_kernel(be_ref, nv_ref, x_ref, wg_ref, wu_ref, wd_ref, y_ref, wg_sc, wu_sc, wd_sc):
    i = pl.program_id(0)
    nv = nv_ref[i]

    @pl.when((i == 0) | (be_ref[i] != be_ref[jnp.maximum(i - 1, 0)]))
    def _():
        wg_sc[...] = wg_ref[0, 0].astype(BF16)
        wu_sc[...] = wu_ref[0, 0].astype(BF16)
        wd_sc[...] = wd_ref[0, 0].astype(BF16)

    @pl.when(nv > 0)
    def _():
        rowid = lax.broadcasted_iota(jnp.int32, x_ref.shape, 0)
        x_lo, x_hi = _unpack_bf16_pair(jnp.where(rowid < nv, x_ref[...], jnp.uint32(0)))
        x_lo = x_lo.astype(BF16)
        x_hi = x_hi.astype(BF16)
        g = (jnp.dot(x_lo, wg_sc[:HALF_D, :], preferred_element_type=F32)
             + jnp.dot(x_hi, wg_sc[HALF_D:, :], preferred_element_type=F32))
        up = (jnp.dot(x_lo, wu_sc[:HALF_D, :], preferred_element_type=F32)
              + jnp.dot(x_hi, wu_sc[HALF_D:, :], preferred_element_type=F32))
        act = g * _sigmoid(g) * up
        y = jnp.dot(act.astype(BF16), wd_sc[...], preferred_element_type=F32)
        y_ref[...] = _pack_bf16_pair(y[:, :HALF_D], y[:, HALF_D:])

    @pl.when(nv == 0)
    def _():
        y_ref[...] = jnp.zeros_like(y_ref)


def _expert_call(blk_e, blk_valid, xb, w_gate, w_up, w_down):
    r = xb.shape[0]
    bm = MOE_BM
    wspec = lambda a, b: pl.BlockSpec((1, 1, a, b), lambda i, be, nv: (0, be[i], 0, 0))
    return pl.pallas_call(
        _expert_kernel,
        grid_spec=pltpu.PrefetchScalarGridSpec(
            num_scalar_prefetch=2,
            grid=(r // bm,),
            in_specs=[pl.BlockSpec((bm, HALF_D), lambda i, be, nv: (i, 0)),
                      wspec(D_MODEL, D_EXPERT), wspec(D_MODEL, D_EXPERT), wspec(D_EXPERT, D_MODEL)],
            out_specs=pl.BlockSpec((bm, HALF_D), lambda i, be, nv: (i, 0)),
            scratch_shapes=[pltpu.VMEM((D_MODEL, D_EXPERT), BF16), pltpu.VMEM((D_MODEL, D_EXPERT), BF16),
                            pltpu.VMEM((D_EXPERT, D_MODEL), BF16)]),
        out_shape=jax.ShapeDtypeStruct((r, HALF_D), jnp.uint32),
        compiler_params=pltpu.CompilerParams(dimension_semantics=("arbitrary",),
                                             vmem_limit_bytes=VMEM_LIMIT),
        name="experts",
    )(blk_e, blk_valid, xb, w_gate, w_up, w_down)


def _sc_gather_rows(table, idx):
    m = idx.shape[0]
    width = table.shape[1]
    sc = plsc.get_sparse_core_info()
    workers = sc.num_cores * sc.num_subcores
    per_worker = m // workers
    mesh = plsc.VectorSubcoreMesh(core_axis_name="c", subcore_axis_name="s")

    n_win = per_worker // SC_WINDOW
    assert n_win % 2 == 0 and n_win >= 2
    buf = lambda: [pltpu.VMEM((SC_WINDOW, width), table.dtype), pltpu.SemaphoreType.DMA]

    @functools.partial(pl.kernel, mesh=mesh, out_type=jax.ShapeDtypeStruct((m, width), table.dtype),
                       scratch_types=[pltpu.VMEM((per_worker,), jnp.int32)] + buf() + buf())
    def gather(table_hbm, idx_hbm, out_hbm, idx_v, rows_a, sem_a, rows_b, sem_b):
        base = (lax.axis_index("s") * sc.num_cores + lax.axis_index("c")) * per_worker
        bufs = ((rows_a, sem_a), (rows_b, sem_b))
        pltpu.sync_copy(idx_hbm.at[pl.ds(base, per_worker)], idx_v)

        def window_copy(win, which):
            rows_v, sem = bufs[which]
            return pltpu.make_async_copy(table_hbm.at[idx_v.at[pl.ds(win * SC_WINDOW, SC_WINDOW)]], rows_v, sem)

        def fetch(win, which):
            window_copy(win, which).start()

        def flush(win, which):
            window_copy(win, which).wait()
            pltpu.sync_copy(bufs[which][0], out_hbm.at[pl.ds(base + win * SC_WINDOW, SC_WINDOW)])

        fetch(0, 0)

        @pl.loop(0, n_win // 2 - 1)
        def _(pair):
            win = 2 * pair
            fetch(win + 1, 1)
            flush(win, 0)
            fetch(win + 2, 0)
            flush(win + 1, 1)

        fetch(n_win - 1, 1)
        flush(n_win - 2, 0)
        flush(n_win - 1, 1)

    return gather(table, idx)


def _sc_scatter_rows(rows, idx0, idx1, n_out):
    m, width = rows.shape
    sc = plsc.get_sparse_core_info()
    workers = sc.num_cores * sc.num_subcores
    per_worker = m // workers
    mesh = plsc.VectorSubcoreMesh(core_axis_name="c", subcore_axis_name="s")

    n_win = per_worker // SC_WINDOW
    assert n_win % 2 == 0 and n_win >= 2
    buf = lambda: [pltpu.VMEM((SC_WINDOW, width), rows.dtype),
                   pltpu.SemaphoreType.DMA, pltpu.SemaphoreType.DMA, pltpu.SemaphoreType.DMA]

    @functools.partial(pl.kernel, mesh=mesh, out_type=jax.ShapeDtypeStruct((n_out, width), rows.dtype),
                       scratch_types=[pltpu.VMEM((per_worker,), jnp.int32), pltpu.VMEM((per_worker,), jnp.int32)]
                       + buf() + buf())
    def scatter(rows_hbm, idx0_hbm, idx1_hbm, out_hbm, idx0_v, idx1_v, *scratch):
        base = (lax.axis_index("s") * sc.num_cores + lax.axis_index("c")) * per_worker
        bufs = (scratch[:4], scratch[4:])
        pltpu.sync_copy(idx0_hbm.at[pl.ds(base, per_worker)], idx0_v)
        pltpu.sync_copy(idx1_hbm.at[pl.ds(base, per_worker)], idx1_v)

        def load(win, which):
            rows_v, lsem, _, _ = bufs[which]
            pltpu.async_copy(rows_hbm.at[pl.ds(base + win * SC_WINDOW, SC_WINDOW)], rows_v, lsem)

        def send(win, which):
            rows_v, lsem, sem0, sem1 = bufs[which]
            here = pl.ds(win * SC_WINDOW, SC_WINDOW)
            pltpu.make_async_copy(rows_hbm.at[pl.ds(0, SC_WINDOW)], rows_v, lsem).wait()
            first = pltpu.async_copy(rows_v, out_hbm.at[idx0_v.at[here]], sem0)
            second = pltpu.async_copy(rows_v, out_hbm.at[idx1_v.at[here]], sem1)
            first.wait()
            second.wait()

        load(0, 0)

        @pl.loop(0, n_win // 2 - 1)
        def _(pair):
            win = 2 * pair
            load(win + 1, 1)
            send(win, 0)
            load(win + 2, 0)
            send(win + 1, 1)

        load(n_win - 1, 1)
        send(n_win - 2, 0)
        send(n_win - 1, 1)

    return scatter(rows, idx0, idx1)


def _combine_rows_kernel(h_ref, w_ref, g_ref, b_ref, y0_ref, y1_ref, o_ref):
    w = w_ref[...]
    y0_lo, y0_hi = _unpack_bf16_pair(y0_ref[...])
    y1_lo, y1_hi = _unpack_bf16_pair(y1_ref[...])
    y = jnp.concatenate([w[:, 0:1] * y0_lo + w[:, 1:2] * y1_lo, w[:, 0:1] * y0_hi + w[:, 1:2] * y1_hi], axis=1)
    o_ref[...] = _ln(DEEPNORM_ALPHA * h_ref[...] + y, g_ref[...], b_ref[...])


def _combine_rows_call(h1, rw, g, b, yg):
    t = h1.shape[0]
    tm = COMB_TM
    n = t // tm
    return pl.pallas_call(
        _combine_rows_kernel,
        grid=(n,),
        in_specs=[pl.BlockSpec((tm, D_MODEL), lambda i: (i, 0)),
                  pl.BlockSpec((tm, 8), lambda i: (i, 0)),
                  pl.BlockSpec((1, D_MODEL), lambda i: (0, 0)),
                  pl.BlockSpec((1, D_MODEL), lambda i: (0, 0)),
                  pl.BlockSpec((tm, HALF_D), lambda i: (i, 0)),
                  pl.BlockSpec((tm, HALF_D), lambda i: (n + i, 0))],
        out_specs=pl.BlockSpec((tm, D_MODEL), lambda i: (i, 0)),
        out_shape=jax.ShapeDtypeStruct((t, D_MODEL), F32),
        compiler_params=pltpu.CompilerParams(dimension_semantics=("arbitrary",),
                                             vmem_limit_bytes=VMEM_LIMIT),
        name="combine_rows",
    )(h1, rw, g, b, yg, yg)


def _take_cols(w, idx):
    wz = jnp.concatenate([w, jnp.zeros((w.shape[0], 1), w.dtype)], axis=1)
    return wz[:, np.where(idx < 0, w.shape[1], idx)]


def _layout_indices():
    lane_src = np.full((HEAD_PAD,), -1, np.int64)
    lane_src[0:ROPE_HALF] = QK_NOPE + np.arange(ROPE_HALF)
    lane_src[ROPE_HALF:X2_LANE] = np.arange(X2_LANE - ROPE_HALF)
    lane_src[X2_LANE:X2_LANE + ROPE_HALF] = QK_NOPE + ROPE_HALF + np.arange(ROPE_HALF)
    n_lo = X2_LANE - ROPE_HALF
    lane_src[X2_LANE + ROPE_HALF:X2_LANE + ROPE_HALF + QK_NOPE - n_lo] = n_lo + np.arange(QK_NOPE - n_lo)
    q_idx = np.concatenate([np.where(lane_src >= 0, lane_src + (QK_NOPE + QK_ROPE) * h, -1)
                            for h in range(N_HEADS)])
    nope_src = np.where(lane_src < QK_NOPE, lane_src, -1)
    k_idx = np.concatenate([np.where(nope_src >= 0, nope_src + (QK_NOPE + V_DIM) * h, -1)
                            for h in range(N_HEADS)])
    v_idx = np.concatenate([QK_NOPE + np.arange(V_DIM) + (QK_NOPE + V_DIM) * h for h in range(N_HEADS)])
    kr_idx = np.full((LANES,), -1, np.int64)
    kr_idx[0:ROPE_HALF] = np.arange(ROPE_HALF)
    kr_idx[X2_LANE:X2_LANE + ROPE_HALF] = ROPE_HALF + np.arange(ROPE_HALF)
    return q_idx, k_idx, v_idx, kr_idx


def _rope_lane_tables(length):
    inv_freq = np.float32(ROPE_THETA) ** (-np.arange(ROPE_HALF, dtype=np.float32) / np.float32(ROPE_HALF))
    ang = np.arange(length, dtype=np.float32)[:, None] * inv_freq[None, :].astype(np.float32)
    cos, sin = np.cos(ang).astype(np.float32), np.sin(ang).astype(np.float32)
    ctab = np.ones((length, LANES), np.float32)
    stab = np.zeros((length, LANES), np.float32)
    for lo, sign in ((0, -1.0), (X2_LANE, 1.0)):
        ctab[:, lo:lo + ROPE_HALF] = cos
        stab[:, lo:lo + ROPE_HALF] = sign * sin
    return jnp.asarray(ctab), jnp.asarray(stab)


def kernel(x, meta_tokens, ln_in_g, ln_in_b, w_in, conv_w, conv_b, conv_ln_g, conv_ln_b, q_norm_g, w_uq,
           kv_norm_g, w_ukv, conv_out_g, attn_out_g, w_o, ln1_g, ln1_b, w_rg, b_rg, w_re, b_re,
           w_gate, w_up, w_down, ln2_g, ln2_b):
    assert w_in.shape[0] == 1, "single layer"
    bsz, seq, d = x.shape
    t = bsz * seq
    row = lambda a: a.reshape(1, -1)

    q_idx, k_idx, v_idx, kr_idx = _layout_indices()
    c4 = 2 * C_CONV + Q_LORA + KV_LORA
    w_in_l = jnp.concatenate([w_in[0][:, :c4], _take_cols(w_in[0][:, c4:], kr_idx)], axis=1).astype(BF16)
    wuq_l = _take_cols(w_uq[0], q_idx).astype(BF16)
    wuk_l = _take_cols(w_ukv[0], k_idx).astype(BF16)
    wvt_l = w_ukv[0][:, v_idx].T.astype(BF16)
    wo_l = w_o[0].astype(BF16)
    n_route_pad = ROUTE_ROWS - N_EXPERTS - N_GROUPS
    wr_t = jnp.concatenate([w_re[0].T, w_rg[0].T, jnp.zeros((n_route_pad, d), F32)], axis=0)
    wrh_l = wr_t.astype(BF16)
    br_l = jnp.broadcast_to(jnp.concatenate([b_re[0], b_rg[0], jnp.zeros((n_route_pad,), F32)])[:, None],
                            (ROUTE_ROWS, LANES))
    ctab, stab = _rope_lane_tables(N_META + seq)

    proj_w = (row(ln_in_g), row(ln_in_b), w_in_l, row(q_norm_g[0]), wuq_l, row(kv_norm_g[0]), wuk_l, wvt_l)
    u_m, _, k_m, vt_m = _proj_call(meta_tokens.astype(F32), N_META, 1, *proj_w, ctab[:N_META], stab[:N_META])
    u, q, k, vt = _proj_call(x.reshape(t, d), PROJ_TM, seq // PROJ_TM, *proj_w, ctab[N_META:], stab[N_META:])

    o = _attn_call(q.reshape(bsz, seq, -1), k.reshape(bsz, seq, -1),
                   vt.reshape(bsz, seq // ATT_T, D_ATTN, ATT_T), k_m, vt_m, row(attn_out_g[0]))

    pre = jnp.concatenate([jnp.zeros((HALO - N_META, C_CONV), F32), u_m], axis=0)
    cw = jnp.concatenate([conv_w[0], jnp.zeros((HALO - CONV_K, C_CONV), F32)], axis=0)
    h1, hp, ri, rw, cnt = _mix_call(x, row(ln_in_g), row(ln_in_b), u.reshape(bsz, seq, C_CONV), pre, cw,
                                    row(conv_b[0]), row(conv_ln_g[0]), row(conv_ln_b[0]), row(conv_out_g[0]),
                                    o, wo_l, row(ln1_g[0]), row(ln1_b[0]), wrh_l, br_l)

    counts = cnt[:N_EXPERTS, 0].astype(jnp.int32)
    padded = (counts + MOE_BM - 1) // MOE_BM * MOE_BM
    pad_end = jnp.cumsum(padded)
    pad_start = pad_end - padded
    n_blocks = (2 * t) // MOE_BM + N_EXPERTS
    dest, blk = _tables_call(pad_start, pad_end, counts, ri, n_blocks)
    blk_e, blk_valid = blk[0, :n_blocks], blk[1, :n_blocks]

    h1f = h1.reshape(t, d)
    dest0, dest1 = dest[0], dest[1]
    xb = _sc_scatter_rows(hp.reshape(t, HALF_D), dest0, dest1, n_blocks * MOE_BM)
    yb = _expert_call(blk_e, blk_valid, xb, w_gate, w_up, w_down)
    yg = _sc_gather_rows(yb, jnp.concatenate([dest0, dest1]))
    out = _combine_rows_call(h1f, rw.reshape(t, 8), row(ln2_g[0]), row(ln2_b[0]), yg)
    return out.reshape(bsz, seq, d)
```
